```python
import jax, jax.numpy as jnp
from jax import lax
import numpy as np

D_MODEL = 1024
BATCH = 8
SEQ = 8192
DEPTH = 1

HEAD_DIM = 64
GRID_W = 64
NA_HEADS = 8
NA_KH = 8
NA_KW = 16
DIL_CONFIGS = ((128, 1), (512, 4), (2048, 16))
DIL_HEADS_PER_GROUP = 4
DIL_HEADS = DIL_HEADS_PER_GROUP * len(DIL_CONFIGS)
DIL_QBLOCK = 64
ROT_DIM = HEAD_DIM // 4
ROPE_THETA = 500000.0
D_FF = -(-8 * D_MODEL // (3 * 256)) * 256
EPS = 1e-6
NEG_INF = -1e30
WA = NA_HEADS * HEAD_DIM
WB = DIL_HEADS * HEAD_DIM
WB_OUT = DIL_HEADS_PER_GROUP * HEAD_DIM
W_IN = 3 * WA + 3 * WB + 2 * D_MODEL

kernel_name = "hybrid_natten_dilated_gated_encoder"

f32 = jnp.float32


def rms_norm(x, g):
    x32 = x.astype(f32)
    y = x32 * lax.rsqrt(jnp.mean(x32 * x32, axis=-1, keepdims=True) + EPS)
    return (y * g.astype(f32)).astype(x.dtype)


def partial_rotary(t, pos):
    half = ROT_DIM // 2
    inv_freq = ROPE_THETA ** (-(jnp.arange(half, dtype=f32) * 2.0) / ROT_DIM)
    ang = pos.astype(f32)[:, None] * inv_freq[None, :]
    cos = jnp.cos(ang)[None, :, None, :]
    sin = jnp.sin(ang)[None, :, None, :]
    x1 = t[..., :half].astype(f32)
    x2 = t[..., half:ROT_DIM].astype(f32)
    rot = jnp.concatenate([x1 * cos - x2 * sin, x2 * cos + x1 * sin], axis=-1).astype(t.dtype)
    return jnp.concatenate([rot, t[..., ROT_DIM:]], axis=-1)


def neighbourhood_attention(q, k, v, rpb):
    b, s, h, dh = q.shape
    rows = s // GRID_W
    kh = min(NA_KH, rows)
    qg = q.reshape(b, rows, GRID_W, h, dh)
    kg = k.reshape(b, rows, GRID_W, h, dh)
    vg = v.reshape(b, rows, GRID_W, h, dh)
    r = jnp.arange(rows)
    row_start = jnp.clip(r - kh // 2, 0, rows - kh)
    row_idx = row_start[:, None] + jnp.arange(kh)[None, :]
    kn = kg[:, row_idx].reshape(b, rows, kh * GRID_W, h, dh)
    vn = vg[:, row_idx].reshape(b, rows, kh * GRID_W, h, dh)
    col = jnp.arange(GRID_W)
    col_start = jnp.clip(col - NA_KW // 2, 0, GRID_W - NA_KW)
    col_mask = (col[None, :] >= col_start[:, None]) & (col[None, :] < col_start[:, None] + NA_KW)
    mask = jnp.broadcast_to(col_mask[:, None, :], (GRID_W, kh, GRID_W)).reshape(GRID_W, kh * GRID_W)
    row_off = row_idx - r[:, None] + (NA_KH - 1)
    col_off = jnp.clip(col[None, :] - col[:, None] + (NA_KW - 1), 0, 2 * NA_KW - 2)
    bias = rpb[:, row_off[:, None, :, None], col_off[None, :, None, :]]
    bias = bias.reshape(h, rows, GRID_W, kh * GRID_W).astype(f32)
    scores = jnp.einsum('brqhd,brkhd->bhrqk', qg, kn).astype(f32) * (dh ** -0.5) + bias[None]
    scores = jnp.where(mask, scores, NEG_INF)
    p = jax.nn.softmax(scores, axis=-1).astype(v.dtype)
    o = jnp.einsum('bhrqk,brkhd->brqhd', p, vn)
    return o.reshape(b, s, h * dh)


def dilated_window_attention(q, k, v, window, dilation):
    b, s, h, dh = q.shape
    half = (window // 2) // dilation
    seg = s // dilation

    def split(t):
        return t.reshape(b, seg, dilation, h, dh).transpose(0, 2, 1, 3, 4).reshape(b * dilation, seg, h, dh)

    qs, ks, vs = split(q), split(k), split(v)
    nb = -(-seg // DIL_QBLOCK)
    lp = nb * DIL_QBLOCK
    qs = jnp.pad(qs, ((0, 0), (0, lp - seg), (0, 0), (0, 0)))
    pad_k = ((0, 0), (half, lp - seg + half), (0, 0), (0, 0))
    kp, vp = jnp.pad(ks, pad_k), jnp.pad(vs, pad_k)
    span = DIL_QBLOCK + 2 * half
    key_idx = jnp.arange(nb)[:, None] * DIL_QBLOCK + jnp.arange(span)[None, :]
    kb = kp[:, key_idx]
    vb = vp[:, key_idx]
    qb = qs.reshape(-1, nb, DIL_QBLOCK, h, dh)
    qpos = jnp.arange(nb)[:, None] * DIL_QBLOCK + jnp.arange(DIL_QBLOCK)[None, :]
    kpos = key_idx - half
    rel = kpos[:, None, :] - qpos[:, :, None]
    mask = (jnp.abs(rel) <= half) & (kpos[:, None, :] >= 0) & (kpos[:, None, :] < seg)
    scores = jnp.einsum('nbqhd,nbkhd->nhbqk', qb, kb).astype(f32) * (dh ** -0.5)
    scores = jnp.where(mask[None, None], scores, NEG_INF)
    lse = jax.nn.logsumexp(scores, axis=-1)
    p = jnp.exp(scores - lse[..., None]).astype(v.dtype)
    o = jnp.einsum('nhbqk,nbkhd->nbqhd', p, vb).reshape(-1, lp, h, dh)[:, :seg]
    lse = lse.transpose(0, 2, 3, 1).reshape(-1, lp, h)[:, :seg]
    o = o.reshape(b, dilation, seg, h, dh).transpose(0, 2, 1, 3, 4).reshape(b, s, h, dh)
    lse = lse.reshape(b, dilation, seg, h).transpose(0, 2, 1, 3).reshape(b, s, h)
    return o, lse


def _fwd_setup_inputs(seed: int = 0) -> dict:
    key = jax.random.key(seed)
    ks = jax.random.split(key, 20)
    nrm = lambda k, shape, scale: jax.random.normal(k, shape, f32) * scale
    d = D_MODEL
    return {
        "x": nrm(ks[0], (BATCH, SEQ, d), 1.0),
        "c": nrm(ks[1], (BATCH, d), 1.0),
        "w_ada": nrm(ks[2], (DEPTH, d, 6 * d), 0.5 * d ** -0.5),
        "b_ada": nrm(ks[3], (DEPTH, 6 * d), 0.02),
        "g_norm1": 1.0 + nrm(ks[4], (DEPTH, d), 0.05),
        "g_norm2": 1.0 + nrm(ks[5], (DEPTH, d), 0.05),
        "w_in": nrm(ks[6], (DEPTH, d, W_IN), d ** -0.5),
        "b_gate": nrm(ks[7], (DEPTH, 2 * d), 0.02),
        "g_qa": 1.0 + nrm(ks[8], (DEPTH, HEAD_DIM), 0.05),
        "g_ka": 1.0 + nrm(ks[9], (DEPTH, HEAD_DIM), 0.05),
        "g_qb": 1.0 + nrm(ks[10], (DEPTH, HEAD_DIM), 0.05),
        "g_kb": 1.0 + nrm(ks[11], (DEPTH, HEAD_DIM), 0.05),
        "rpb": nrm(ks[12], (DEPTH, NA_HEADS, 2 * NA_KH - 1, 2 * NA_KW - 1), 0.1),
        "w_proj_a": nrm(ks[13], (DEPTH, WA, d), WA ** -0.5),
        "w_proj_b": nrm(ks[14], (DEPTH, WB_OUT, d), WB_OUT ** -0.5),
        "w_o": nrm(ks[15], (DEPTH, d, d), d ** -0.5),
        "w_ffn_in": nrm(ks[16], (DEPTH, d, 2 * D_FF), d ** -0.5),
        "w_ffn_out": nrm(ks[17], (DEPTH, D_FF, d), D_FF ** -0.5),
    }


def _fwd_reference(x, c, w_ada, b_ada, g_norm1, g_norm2, w_in, b_gate, g_qa, g_ka, g_qb, g_kb,
              rpb, w_proj_a, w_proj_b, w_o, w_ffn_in, w_ffn_out):
    b, s, _ = x.shape
    pos = jnp.arange(s)
    c_act = jax.nn.silu(c)
    split_at = [WA, 2 * WA, 3 * WA, 3 * WA + WB, 3 * WA + 2 * WB, 3 * WA + 3 * WB]
    for l in range(DEPTH):
        mod = c_act @ w_ada[l] + b_ada[l]
        sh1, sc1, gt1, sh2, sc2, gt2 = [m[:, None, :] for m in jnp.split(mod, 6, axis=-1)]

        h = rms_norm(x, g_norm1[l]) * (1.0 + sc1) + sh1
        proj = h @ w_in[l]
        qa, ka, va, qb, kb, vb, gates = jnp.split(proj, split_at, axis=-1)

        qa = rms_norm(qa.reshape(b, s, NA_HEADS, HEAD_DIM), g_qa[l])
        ka = rms_norm(ka.reshape(b, s, NA_HEADS, HEAD_DIM), g_ka[l])
        va = va.reshape(b, s, NA_HEADS, HEAD_DIM)
        o_a = neighbourhood_attention(qa, ka, va, rpb[l])

        qb = partial_rotary(rms_norm(qb.reshape(b, s, DIL_HEADS, HEAD_DIM), g_qb[l]), pos)
        kb = partial_rotary(rms_norm(kb.reshape(b, s, DIL_HEADS, HEAD_DIM), g_kb[l]), pos)
        vb = vb.reshape(b, s, DIL_HEADS, HEAD_DIM)
        outs, lses = [], []
        for g, (win, dil) in enumerate(DIL_CONFIGS):
            sl = slice(g * DIL_HEADS_PER_GROUP, (g + 1) * DIL_HEADS_PER_GROUP)
            o_g, lse_g = dilated_window_attention(qb[:, :, sl], kb[:, :, sl], vb[:, :, sl], win, dil)
            outs.append(o_g)
            lses.append(lse_g)
        wts = jax.nn.softmax(jnp.stack(lses, axis=0), axis=0)
        o_b = jnp.einsum('gbsh,gbshd->bshd', wts.astype(vb.dtype), jnp.stack(outs, axis=0))
        o_b = o_b.reshape(b, s, WB_OUT)

        gate_a, gate_b = jnp.split(jax.nn.sigmoid(gates + b_gate[l]), 2, axis=-1)
        merged = gate_a * (o_a @ w_proj_a[l]) + gate_b * (o_b @ w_proj_b[l])
        x = x + gt1 * (merged @ w_o[l])

        h2 = rms_norm(x, g_norm2[l]) * (1.0 + sc2) + sh2
        a, up = jnp.split(h2 @ w_ffn_in[l], 2, axis=-1)
        x = x + gt2 * ((jax.nn.silu(a) * up) @ w_ffn_out[l])
    return x


import jax as _jax
import jax.numpy as _jnp

TWIN_FORMAT = 'train_step'
FWD_PARAMS = ['x', 'c', 'w_ada', 'b_ada', 'g_norm1', 'g_norm2', 'w_in', 'b_gate', 'g_qa', 'g_ka', 'g_qb', 'g_kb', 'rpb', 'w_proj_a', 'w_proj_b', 'w_o', 'w_ffn_in', 'w_ffn_out']
TWIN_WEIGHTS = ['w_ada', 'b_ada', 'g_norm1', 'g_norm2', 'w_in', 'b_gate', 'g_qa', 'g_ka', 'g_qb', 'g_kb', 'rpb', 'w_proj_a', 'w_proj_b', 'w_o', 'w_ffn_in', 'w_ffn_out']
TWIN_DIFF_INPUT = 'x'
TWIN_INPUTS = ['x', 'c', 'w_ada', 'b_ada', 'g_norm1', 'g_norm2', 'w_in', 'b_gate', 'g_qa', 'g_ka', 'g_qb', 'g_kb', 'rpb', 'w_proj_a', 'w_proj_b', 'w_o', 'w_ffn_in', 'w_ffn_out', 'loss_target', 'm_w_ada', 'm_b_ada', 'm_g_norm1', 'm_g_norm2', 'm_w_in', 'm_b_gate', 'm_g_qa', 'm_g_ka', 'm_g_qb', 'm_g_kb', 'm_rpb', 'm_w_proj_a', 'm_w_proj_b', 'm_w_o', 'm_w_ffn_in', 'm_w_ffn_out', 'v_w_ada', 'v_b_ada', 'v_g_norm1', 'v_g_norm2', 'v_w_in', 'v_b_gate', 'v_g_qa', 'v_g_ka', 'v_g_qb', 'v_g_kb', 'v_rpb', 'v_w_proj_a', 'v_w_proj_b', 'v_w_o', 'v_w_ffn_in', 'v_w_ffn_out']
TWIN_OUTPUTS = ['loss', 'grad_x', 'grad_w_ada', 'grad_b_ada', 'grad_g_norm1', 'grad_g_norm2', 'grad_w_in', 'grad_b_gate', 'grad_g_qa', 'grad_g_ka', 'grad_g_qb', 'grad_g_kb', 'grad_rpb', 'grad_w_proj_a', 'grad_w_proj_b', 'grad_w_o', 'grad_w_ffn_in', 'grad_w_ffn_out', 'delta_w_ada', 'delta_b_ada', 'delta_g_norm1', 'delta_g_norm2', 'delta_w_in', 'delta_b_gate', 'delta_g_qa', 'delta_g_ka', 'delta_g_qb', 'delta_g_kb', 'delta_rpb', 'delta_w_proj_a', 'delta_w_proj_b', 'delta_w_o', 'delta_w_ffn_in', 'delta_w_ffn_out', 'new_m_w_ada', 'new_m_b_ada', 'new_m_g_norm1', 'new_m_g_norm2', 'new_m_w_in', 'new_m_b_gate', 'new_m_g_qa', 'new_m_g_ka', 'new_m_g_qb', 'new_m_g_kb', 'new_m_rpb', 'new_m_w_proj_a', 'new_m_w_proj_b', 'new_m_w_o', 'new_m_w_ffn_in', 'new_m_w_ffn_out', 'new_v_w_ada', 'new_v_b_ada', 'new_v_g_norm1', 'new_v_g_norm2', 'new_v_w_in', 'new_v_b_gate', 'new_v_g_qa', 'new_v_g_ka', 'new_v_g_qb', 'new_v_g_kb', 'new_v_rpb', 'new_v_w_proj_a', 'new_v_w_proj_b', 'new_v_w_o', 'new_v_w_ffn_in', 'new_v_w_ffn_out']
TWIN_LEAF_KINDS = {'loss': 'loss', 'grad_x': 'grad_x', 'grad_w_ada': 'grad_w', 'grad_b_ada': 'grad_w', 'grad_g_norm1': 'grad_w', 'grad_g_norm2': 'grad_w', 'grad_w_in': 'grad_w', 'grad_b_gate': 'grad_w', 'grad_g_qa': 'grad_w', 'grad_g_ka': 'grad_w', 'grad_g_qb': 'grad_w', 'grad_g_kb': 'grad_w', 'grad_rpb': 'grad_w', 'grad_w_proj_a': 'grad_w', 'grad_w_proj_b': 'grad_w', 'grad_w_o': 'grad_w', 'grad_w_ffn_in': 'grad_w', 'grad_w_ffn_out': 'grad_w', 'delta_w_ada': 'delta_w', 'delta_b_ada': 'delta_w', 'delta_g_norm1': 'delta_w', 'delta_g_norm2': 'delta_w', 'delta_w_in': 'delta_w', 'delta_b_gate': 'delta_w', 'delta_g_qa': 'delta_w', 'delta_g_ka': 'delta_w', 'delta_g_qb': 'delta_w', 'delta_g_kb': 'delta_w', 'delta_rpb': 'delta_w', 'delta_w_proj_a': 'delta_w', 'delta_w_proj_b': 'delta_w', 'delta_w_o': 'delta_w', 'delta_w_ffn_in': 'delta_w', 'delta_w_ffn_out': 'delta_w', 'new_m_w_ada': 'new_m', 'new_m_b_ada': 'new_m', 'new_m_g_norm1': 'new_m', 'new_m_g_norm2': 'new_m', 'new_m_w_in': 'new_m', 'new_m_b_gate': 'new_m', 'new_m_g_qa': 'new_m', 'new_m_g_ka': 'new_m', 'new_m_g_qb': 'new_m', 'new_m_g_kb': 'new_m', 'new_m_rpb': 'new_m', 'new_m_w_proj_a': 'new_m', 'new_m_w_proj_b': 'new_m', 'new_m_w_o': 'new_m', 'new_m_w_ffn_in': 'new_m', 'new_m_w_ffn_out': 'new_m', 'new_v_w_ada': 'new_v', 'new_v_b_ada': 'new_v', 'new_v_g_norm1': 'new_v', 'new_v_g_norm2': 'new_v', 'new_v_w_in': 'new_v', 'new_v_b_gate': 'new_v', 'new_v_g_qa': 'new_v', 'new_v_g_ka': 'new_v', 'new_v_g_qb': 'new_v', 'new_v_g_kb': 'new_v', 'new_v_rpb': 'new_v', 'new_v_w_proj_a': 'new_v', 'new_v_w_proj_b': 'new_v', 'new_v_w_o': 'new_v', 'new_v_w_ffn_in': 'new_v', 'new_v_w_ffn_out': 'new_v'}


def _forward(args):
    return _fwd_reference(*[args[k] for k in FWD_PARAMS])


def _output_shape():
    def fwd():
        inp = _fwd_setup_inputs(0)
        return _fwd_reference(*[inp[k] for k in FWD_PARAMS])
    out = _jax.eval_shape(fwd)
    return out.shape, out.dtype

N_MICROBATCH = 1
ADAM_LR = 0.001
ADAM_B1 = 0.9
ADAM_B2 = 0.999
ADAM_EPS = 1e-08
ADAM_WD = 0.01
ADAM_STEP = 10
PER_EXAMPLE_BATCH_AXIS = {'x': 0, 'c': 0, 'loss_target': 0}
SHARED_INPUTS = []
_WEIGHT_DTYPES = {'w_ada': _jnp.float32, 'b_ada': _jnp.float32, 'g_norm1': _jnp.float32, 'g_norm2': _jnp.float32, 'w_in': _jnp.float32, 'b_gate': _jnp.float32, 'g_qa': _jnp.float32, 'g_ka': _jnp.float32, 'g_qb': _jnp.float32, 'g_kb': _jnp.float32, 'rpb': _jnp.float32, 'w_proj_a': _jnp.float32, 'w_proj_b': _jnp.float32, 'w_o': _jnp.float32, 'w_ffn_in': _jnp.float32, 'w_ffn_out': _jnp.float32}
MOMENT_SCALE = {'w_ada': 1.151020e+00, 'b_ada': 3.158105e+00, 'g_norm1': 4.119962e-02, 'g_norm2': 6.431450e+00, 'w_in': 5.443751e-02, 'b_gate': 5.381151e-02, 'g_qa': 1.944319e-01, 'g_ka': 1.965313e-01, 'g_qb': 1.750317e-01, 'g_kb': 1.748044e-01, 'rpb': 6.203481e-03, 'w_proj_a': 1.024613e-01, 'w_proj_b': 5.192445e-02, 'w_o': 9.415711e-02, 'w_ffn_in': 8.770849e-02, 'w_ffn_out': 1.071574e-01}


def _to_microbatches(a, axis):
    t = _jnp.moveaxis(a, axis, 0)
    t = t.reshape((N_MICROBATCH, t.shape[0] // N_MICROBATCH) + t.shape[1:])
    return _jnp.moveaxis(t, 1, axis + 1)


def setup_inputs(seed: int = 0) -> dict:
    inp = _fwd_setup_inputs(seed)
    key = _jax.random.fold_in(_jax.random.key(seed), 7919)
    shape, _ = _output_shape()
    out = dict(inp)
    out["loss_target"] = _jax.random.normal(_jax.random.fold_in(key, 0), shape, _jnp.float32)
    for i, name in enumerate(TWIN_WEIGHTS):
        w = inp[name].astype(_jnp.float32)
        if MOMENT_SCALE is None:
            s = _jnp.sqrt(_jnp.mean(_jnp.square(w)) + 1e-30)
        else:
            s = MOMENT_SCALE[name]
        km, kv = _jax.random.split(_jax.random.fold_in(key, i + 1))
        out[name] = w
        out["m_" + name] = s * _jax.random.normal(km, w.shape, _jnp.float32)
        out["v_" + name] = (s * s) * _jax.random.uniform(kv, w.shape, _jnp.float32, 0.5, 1.5)
    if N_MICROBATCH > 1:
        for name, axis in PER_EXAMPLE_BATCH_AXIS.items():
            out[name] = _to_microbatches(out[name], axis)
    return {'x': out['x'], 'c': out['c'], 'w_ada': out['w_ada'], 'b_ada': out['b_ada'], 'g_norm1': out['g_norm1'], 'g_norm2': out['g_norm2'], 'w_in': out['w_in'], 'b_gate': out['b_gate'], 'g_qa': out['g_qa'], 'g_ka': out['g_ka'], 'g_qb': out['g_qb'], 'g_kb': out['g_kb'], 'rpb': out['rpb'], 'w_proj_a': out['w_proj_a'], 'w_proj_b': out['w_proj_b'], 'w_o': out['w_o'], 'w_ffn_in': out['w_ffn_in'], 'w_ffn_out': out['w_ffn_out'], 'loss_target': out['loss_target'], 'm_w_ada': out['m_w_ada'], 'm_b_ada': out['m_b_ada'], 'm_g_norm1': out['m_g_norm1'], 'm_g_norm2': out['m_g_norm2'], 'm_w_in': out['m_w_in'], 'm_b_gate': out['m_b_gate'], 'm_g_qa': out['m_g_qa'], 'm_g_ka': out['m_g_ka'], 'm_g_qb': out['m_g_qb'], 'm_g_kb': out['m_g_kb'], 'm_rpb': out['m_rpb'], 'm_w_proj_a': out['m_w_proj_a'], 'm_w_proj_b': out['m_w_proj_b'], 'm_w_o': out['m_w_o'], 'm_w_ffn_in': out['m_w_ffn_in'], 'm_w_ffn_out': out['m_w_ffn_out'], 'v_w_ada': out['v_w_ada'], 'v_b_ada': out['v_b_ada'], 'v_g_norm1': out['v_g_norm1'], 'v_g_norm2': out['v_g_norm2'], 'v_w_in': out['v_w_in'], 'v_b_gate': out['v_b_gate'], 'v_g_qa': out['v_g_qa'], 'v_g_ka': out['v_g_ka'], 'v_g_qb': out['v_g_qb'], 'v_g_kb': out['v_g_kb'], 'v_rpb': out['v_rpb'], 'v_w_proj_a': out['v_w_proj_a'], 'v_w_proj_b': out['v_w_proj_b'], 'v_w_o': out['v_w_o'], 'v_w_ffn_in': out['v_w_ffn_in'], 'v_w_ffn_out': out['v_w_ffn_out']}


def _loss(weights, diff, rest, loss_target):
    with _jax.named_scope("forward"):
        args = {**rest, TWIN_DIFF_INPUT: diff, **{k: w.astype(_WEIGHT_DTYPES[k]) for k, w in weights.items()}}
        y = _forward(args)
    with _jax.named_scope("loss_head"):
        err = _jnp.square(y.astype(_jnp.float32) - loss_target)
        return 0.5 * _jnp.sum(_jnp.mean(err, axis=-1)) if err.ndim else 0.5 * err


def _adamw(w, g, m, v):
    m = ADAM_B1 * m + (1.0 - ADAM_B1) * g
    v = ADAM_B2 * v + (1.0 - ADAM_B2) * _jnp.square(g)
    m_hat = m / (1.0 - ADAM_B1 ** ADAM_STEP)
    v_hat = v / (1.0 - ADAM_B2 ** ADAM_STEP)
    delta = -ADAM_LR * (m_hat / (_jnp.sqrt(v_hat) + ADAM_EPS) + ADAM_WD * w)
    return delta, m, v


def reference(x, c, w_ada, b_ada, g_norm1, g_norm2, w_in, b_gate, g_qa, g_ka, g_qb, g_kb, rpb, w_proj_a, w_proj_b, w_o, w_ffn_in, w_ffn_out, loss_target, m_w_ada, m_b_ada, m_g_norm1, m_g_norm2, m_w_in, m_b_gate, m_g_qa, m_g_ka, m_g_qb, m_g_kb, m_rpb, m_w_proj_a, m_w_proj_b, m_w_o, m_w_ffn_in, m_w_ffn_out, v_w_ada, v_b_ada, v_g_norm1, v_g_norm2, v_w_in, v_b_gate, v_g_qa, v_g_ka, v_g_qb, v_g_kb, v_rpb, v_w_proj_a, v_w_proj_b, v_w_o, v_w_ffn_in, v_w_ffn_out):
    given = dict(x=x, c=c, w_ada=w_ada, b_ada=b_ada, g_norm1=g_norm1, g_norm2=g_norm2, w_in=w_in, b_gate=b_gate, g_qa=g_qa, g_ka=g_ka, g_qb=g_qb, g_kb=g_kb, rpb=rpb, w_proj_a=w_proj_a, w_proj_b=w_proj_b, w_o=w_o, w_ffn_in=w_ffn_in, w_ffn_out=w_ffn_out, loss_target=loss_target, m_w_ada=m_w_ada, m_b_ada=m_b_ada, m_g_norm1=m_g_norm1, m_g_norm2=m_g_norm2, m_w_in=m_w_in, m_b_gate=m_b_gate, m_g_qa=m_g_qa, m_g_ka=m_g_ka, m_g_qb=m_g_qb, m_g_kb=m_g_kb, m_rpb=m_rpb, m_w_proj_a=m_w_proj_a, m_w_proj_b=m_w_proj_b, m_w_o=m_w_o, m_w_ffn_in=m_w_ffn_in, m_w_ffn_out=m_w_ffn_out, v_w_ada=v_w_ada, v_b_ada=v_b_ada, v_g_norm1=v_g_norm1, v_g_norm2=v_g_norm2, v_w_in=v_w_in, v_b_gate=v_b_gate, v_g_qa=v_g_qa, v_g_ka=v_g_ka, v_g_qb=v_g_qb, v_g_kb=v_g_kb, v_rpb=v_rpb, v_w_proj_a=v_w_proj_a, v_w_proj_b=v_w_proj_b, v_w_o=v_w_o, v_w_ffn_in=v_w_ffn_in, v_w_ffn_out=v_w_ffn_out)
    weights = {n: given[n] for n in TWIN_WEIGHTS}
    shared = {n: given[n] for n in SHARED_INPUTS}
    per_example = {n: given[n] for n in ['x', 'c']}
    grad_fn = _jax.value_and_grad(_loss, argnums=(0, 1))

    def one_microbatch(ex, loss_target):
        ex = dict(ex)
        diff = ex.pop(TWIN_DIFF_INPUT)
        return grad_fn(weights, diff, {**shared, **ex}, loss_target)

    if N_MICROBATCH == 1:
        loss, (grad_w, grad_x) = one_microbatch(per_example, given["loss_target"])
    else:
        def body(carry, xs):
            loss_sum, grad_sum = carry
            l_k, (gw_k, gx_k) = one_microbatch(xs[0], xs[1])
            with _jax.named_scope("update"):
                return (loss_sum + l_k, _jax.tree.map(_jnp.add, grad_sum, gw_k)), gx_k

        init = (_jnp.zeros((), _jnp.float32), _jax.tree.map(_jnp.zeros_like, weights))
        (loss, grad_w), grad_x = _jax.lax.scan(body, init, (per_example, given["loss_target"]))
    with _jax.named_scope("update"):
        delta_w, new_m, new_v = {}, {}, {}
        for n in TWIN_WEIGHTS:
            delta_w[n], new_m[n], new_v[n] = _adamw(weights[n], grad_w[n], given["m_" + n], given["v_" + n])
    return (loss, grad_x, *[grad_w[n] for n in TWIN_WEIGHTS], *[delta_w[n] for n in TWIN_WEIGHTS],
            *[new_m[n] for n in TWIN_WEIGHTS], *[new_v[n] for n in TWIN_WEIGHTS])
```

```python
import numpy as np

import jax
import jax.numpy as jnp
from jax import lax
from jax.experimental import pallas as pl
from jax.experimental.pallas import tpu as pltpu

F32 = jnp.float32
BF16 = jnp.bfloat16

D_MODEL = 1024
SEQ = 8192
HEAD_DIM = 64
GRID_W = 64
ROWS = SEQ // GRID_W
NA_HEADS = 8
NA_KH = 8
NA_KW = 16
DIL_CONFIGS = ((128, 1), (512, 4), (2048, 16))
DIL_HEADS_PER_GROUP = 4
DIL_HEADS = 12
ROT_DIM = 16
ROPE_THETA = 500000.0
D_FF = 2816
EPS = 1e-6
NEG = -1e30
WA = 512
WB = 768
WB_OUT = 256
W_QKV = 3 * WA + 3 * WB
W_GATES = 2 * D_MODEL
SCALE = HEAD_DIM ** -0.5

ADAM_LR = 0.001
ADAM_B1 = 0.9
ADAM_B2 = 0.999
ADAM_EPS = 1e-08
ADAM_WD = 0.01
ADAM_STEP = 10

LANES = 128
ROW_TILE = 256
Q_BLOCK = 256
NA_QROWS = Q_BLOCK // GRID_W
NA_KROWS = NA_QROWS + NA_KH - 1
NA_NK = NA_KROWS * GRID_W
DIL_HALF = 64
DIL_NK = Q_BLOCK + 2 * DIL_HALF
N_QBLK = SEQ // Q_BLOCK

N_DEV = 8
N_CHIP = 4
PACK_ROWS = 4032
HALF_ROWS = PACK_ROWS // 2
STATS_W = 14336


def _pcall(body, *, name, **kw):
    return pl.pallas_call(body, name=name, **kw)


def _pick(n, cap):
    if n <= cap:
        return n
    best = None
    for t in range(LANES, cap + 1, LANES):
        if n % t == 0:
            best = t
    assert best is not None, (n, cap)
    return best


def _matmul(a, b, *, name, ta=False, tb=False, out_dtype=F32, add=None, tm_cap=512, tn_cap=2944, tk_cap=2944):
    if ta:
        kd, m = a.shape
    else:
        m, kd = a.shape
    if tb:
        n, kb = b.shape
    else:
        kb, n = b.shape
    assert kd == kb, (a.shape, b.shape)
    tm, tn, tk = _pick(m, tm_cap), _pick(n, tn_cap), _pick(kd, tk_cap)
    nk = kd // tk
    dims = (((0 if ta else 1,), (1 if tb else 0,)), ((), ()))
    has_add = add is not None

    def body(*refs):
        a_ref, b_ref = refs[0], refs[1]
        add_ref = refs[2] if has_add else None
        o_ref = refs[3] if has_add else refs[2]
        r = lax.dot_general(a_ref[...].astype(BF16), b_ref[...].astype(BF16), dims, preferred_element_type=F32)
        if nk == 1:
            if has_add:
                r = r + add_ref[...].astype(F32)
            o_ref[...] = r.astype(out_dtype)
        else:
            acc = refs[-1]
            k = pl.program_id(2)

            @pl.when(k == 0)
            def _():
                acc[...] = r

            @pl.when(k > 0)
            def _():
                acc[...] += r

            @pl.when(k == nk - 1)
            def _():
                t = acc[...]
                if has_add:
                    t = t + add_ref[...].astype(F32)
                o_ref[...] = t.astype(out_dtype)

    a_spec = pl.BlockSpec((tk, tm), lambda i, j, k: (k, i)) if ta else pl.BlockSpec((tm, tk), lambda i, j, k: (i, k))
    b_spec = pl.BlockSpec((tn, tk), lambda i, j, k: (j, k)) if tb else pl.BlockSpec((tk, tn), lambda i, j, k: (k, j))
    o_spec = pl.BlockSpec((tm, tn), lambda i, j, k: (i, j))
    in_specs = [a_spec, b_spec] + ([o_spec] if has_add else [])
    args = (a, b) + ((add,) if has_add else ())
    return _pcall(
        body, name=name, grid=(m // tm, n // tn, nk), in_specs=in_specs, out_specs=o_spec,
        out_shape=jax.ShapeDtypeStruct((m, n), out_dtype),
        scratch_shapes=[pltpu.VMEM((tm, tn), F32)] if nk > 1 else [],
        compiler_params=pltpu.CompilerParams(dimension_semantics=("parallel", "parallel", "arbitrary")),
    )(*args)


def _row_spec(w, tr=ROW_TILE):
    return pl.BlockSpec((tr, w), lambda i: (i, 0))


def _vec_spec(w, r=1):
    return pl.BlockSpec((r, w), lambda i: (0, 0))


def _fold8(t):
    r, w = t.shape
    return jnp.sum(t.reshape(r // 8, 8, w), axis=0)


_ARB = pltpu.CompilerParams(dimension_semantics=("arbitrary",))
_PAR = pltpu.CompilerParams(dimension_semantics=("parallel",))


def _norm_fwd(x, g, sc, sh, *, name, res=None, gt=None):
    s, d = x.shape
    has_res = res is not None

    def body(*refs):
        if has_res:
            x_ref, res_ref, gt_ref, g_ref, sc_ref, sh_ref, x1_ref, h_ref = refs
            xv = x_ref[...] + gt_ref[...] * res_ref[...]
            x1_ref[...] = xv
        else:
            x_ref, g_ref, sc_ref, sh_ref, h_ref = refs
            xv = x_ref[...]
        rstd = lax.rsqrt(jnp.mean(xv * xv, axis=1, keepdims=True) + EPS)
        h = (xv * rstd * g_ref[...]) * (1.0 + sc_ref[...]) + sh_ref[...]
        h_ref[...] = h.astype(BF16)

    if has_res:
        in_specs = [_row_spec(d), _row_spec(d), _vec_spec(d), _vec_spec(d), _vec_spec(d), _vec_spec(d)]
        args = (x, res, gt, g, sc, sh)
        out_specs = [_row_spec(d), _row_spec(d)]
        out_shape = [jax.ShapeDtypeStruct((s, d), F32), jax.ShapeDtypeStruct((s, d), BF16)]
    else:
        in_specs = [_row_spec(d), _vec_spec(d), _vec_spec(d), _vec_spec(d)]
        args = (x, g, sc, sh)
        out_specs = _row_spec(d)
        out_shape = jax.ShapeDtypeStruct((s, d), BF16)
    return _pcall(body, name=name, grid=(s // ROW_TILE,), in_specs=in_specs, out_specs=out_specs,
                  out_shape=out_shape, compiler_params=_PAR)(*args)


def _norm_bwd(x, dh, dres, g, sc, *, name, mo=None, gt=None):
    s, d = x.shape
    nsteps = s // ROW_TILE
    has_mo = mo is not None

    def body(*refs):
        if has_mo:
            x_ref, dh_ref, dres_ref, g_ref, sc_ref, mo_ref, gt_ref, dx_ref, sums_ref, dmo_ref, acc = refs
        else:
            x_ref, dh_ref, dres_ref, g_ref, sc_ref, dx_ref, sums_ref, acc = refs
        i = pl.program_id(0)

        @pl.when(i == 0)
        def _():
            acc[...] = jnp.zeros_like(acc)

        xv = x_ref[...]
        dhv = dh_ref[...].astype(F32)
        gv = g_ref[...]
        rstd = lax.rsqrt(jnp.mean(xv * xv, axis=1, keepdims=True) + EPS)
        xhat = xv * rstd
        dn = dhv * (1.0 + sc_ref[...])
        dxhat = dn * gv
        dxn = rstd * (dxhat - xhat * jnp.mean(dxhat * xhat, axis=1, keepdims=True))
        dx = dres_ref[...] + dxn
        dx_ref[...] = dx
        acc[0] += _fold8(dhv)
        acc[1] += _fold8(dhv * (xhat * gv))
        acc[2] += _fold8(dn * xhat)
        if has_mo:
            dmo_ref[...] = (gt_ref[...] * dx).astype(BF16)
            acc[3] += _fold8(dx * mo_ref[...])

        @pl.when(i == nsteps - 1)
        def _():
            sums_ref[...] = jnp.sum(acc[...], axis=1)

    nacc = 4 if has_mo else 3
    in_specs = [_row_spec(d), _row_spec(d), _row_spec(d), _vec_spec(d), _vec_spec(d)]
    args = [x, dh, dres, g, sc]
    out_specs = [_row_spec(d), _vec_spec(d, nacc)]
    out_shape = [jax.ShapeDtypeStruct((s, d), F32), jax.ShapeDtypeStruct((nacc, d), F32)]
    if has_mo:
        in_specs += [_row_spec(d), _vec_spec(d)]
        args += [mo, gt]
        out_specs.append(_row_spec(d))
        out_shape.append(jax.ShapeDtypeStruct((s, d), BF16))
    return _pcall(body, name=name, grid=(nsteps,), in_specs=in_specs, out_specs=out_specs, out_shape=out_shape,
                  scratch_shapes=[pltpu.VMEM((nacc, 8, d), F32)], compiler_params=_ARB)(*args)


def _loss_bwd(x1, ffo, tgt, gt2, *, name):
    s, d = x1.shape
    nsteps = s // ROW_TILE

    def body(x1_ref, ffo_ref, tgt_ref, gt_ref, dy_ref, dffo_ref, dgt_ref, loss_ref, acc):
        i = pl.program_id(0)

        @pl.when(i == 0)
        def _():
            acc[...] = jnp.zeros_like(acc)

        ffo = ffo_ref[...]
        gtv = gt_ref[...]
        e = x1_ref[...] + gtv * ffo - tgt_ref[...]
        dy = e * (1.0 / d)
        dy_ref[...] = dy
        dffo_ref[...] = (gtv * dy).astype(BF16)
        acc[0] += _fold8(dy * ffo)
        acc[1] += _fold8(e * e)

        @pl.when(i == nsteps - 1)
        def _():
            dgt_ref[...] = jnp.sum(acc[0], axis=0, keepdims=True)
            tot = jnp.sum(jnp.sum(acc[1], axis=0, keepdims=True), axis=1, keepdims=True)
            loss_ref[...] = jnp.broadcast_to(tot * (0.5 / d), (1, LANES))

    return _pcall(
        body, name=name, grid=(nsteps,),
        in_specs=[_row_spec(d), _row_spec(d), _row_spec(d), _vec_spec(d)],
        out_specs=[_row_spec(d), _row_spec(d), _vec_spec(d), _vec_spec(LANES)],
        out_shape=[jax.ShapeDtypeStruct((s, d), F32), jax.ShapeDtypeStruct((s, d), BF16),
                   jax.ShapeDtypeStruct((1, d), F32), jax.ShapeDtypeStruct((1, LANES), F32)],
        scratch_shapes=[pltpu.VMEM((2, 8, d), F32)], compiler_params=_ARB)(x1, ffo, tgt, gt2)


def _sigmoid(t):
    return 1.0 / (1.0 + jnp.exp(-t))


def _swiglu_fwd(ff, *, name):
    s = ff.shape[0]

    def body(ff_ref, act_ref):
        a = ff_ref[:, :D_FF]
        up = ff_ref[:, D_FF:]
        act_ref[...] = (a * _sigmoid(a) * up).astype(BF16)

    return _pcall(body, name=name, grid=(s // ROW_TILE,), in_specs=[_row_spec(2 * D_FF)], out_specs=_row_spec(D_FF),
                  out_shape=jax.ShapeDtypeStruct((s, D_FF), BF16), compiler_params=_PAR)(ff)


def _swiglu_bwd(ff, dact, *, name):
    s = ff.shape[0]

    def body(ff_ref, dact_ref, dff_ref):
        a = ff_ref[:, :D_FF]
        up = ff_ref[:, D_FF:]
        da = dact_ref[...]
        sg = _sigmoid(a)
        dff_ref[:, :D_FF] = (da * up * (sg * (1.0 + a * (1.0 - sg)))).astype(BF16)
        dff_ref[:, D_FF:] = (da * (a * sg)).astype(BF16)

    return _pcall(body, name=name, grid=(s // ROW_TILE,), in_specs=[_row_spec(2 * D_FF), _row_spec(D_FF)],
                  out_specs=_row_spec(2 * D_FF), out_shape=jax.ShapeDtypeStruct((s, 2 * D_FF), BF16),
                  compiler_params=_PAR)(ff, dact)


def _gate_fwd(gates, b_gate, pa, pb, *, name):
    s, d = pa.shape

    def body(gates_ref, b_ref, pa_ref, pb_ref, m_ref):
        ga = _sigmoid(gates_ref[:, :d] + b_ref[:, :d])
        gb = _sigmoid(gates_ref[:, d:] + b_ref[:, d:])
        m_ref[...] = (ga * pa_ref[...] + gb * pb_ref[...]).astype(BF16)

    return _pcall(body, name=name, grid=(s // ROW_TILE,),
                  in_specs=[_row_spec(2 * d), _vec_spec(2 * d), _row_spec(d), _row_spec(d)], out_specs=_row_spec(d),
                  out_shape=jax.ShapeDtypeStruct((s, d), BF16), compiler_params=_PAR)(gates, b_gate, pa, pb)


def _gate_bwd(gates, b_gate, pa, pb, dmerged, *, name):
    s, d = pa.shape
    nsteps = s // ROW_TILE

    def body(gates_ref, b_ref, pa_ref, pb_ref, dm_ref, dpa_ref, dpb_ref, dgates_ref, dbg_ref, acc):
        i = pl.program_id(0)

        @pl.when(i == 0)
        def _():
            acc[...] = jnp.zeros_like(acc)

        dm = dm_ref[...]
        ga = _sigmoid(gates_ref[:, :d] + b_ref[:, :d])
        gb = _sigmoid(gates_ref[:, d:] + b_ref[:, d:])
        dpa_ref[...] = (dm * ga).astype(BF16)
        dpb_ref[...] = (dm * gb).astype(BF16)
        dga = dm * pa_ref[...] * ga * (1.0 - ga)
        dgb = dm * pb_ref[...] * gb * (1.0 - gb)
        dgates_ref[:, :d] = dga.astype(BF16)
        dgates_ref[:, d:] = dgb.astype(BF16)
        acc[:, :d] += _fold8(dga)
        acc[:, d:] += _fold8(dgb)

        @pl.when(i == nsteps - 1)
        def _():
            dbg_ref[...] = jnp.sum(acc[...], axis=0, keepdims=True)

    return _pcall(
        body, name=name, grid=(nsteps,),
        in_specs=[_row_spec(2 * d), _vec_spec(2 * d), _row_spec(d), _row_spec(d), _row_spec(d)],
        out_specs=[_row_spec(d), _row_spec(d), _row_spec(2 * d), _vec_spec(2 * d)],
        out_shape=[jax.ShapeDtypeStruct((s, d), BF16), jax.ShapeDtypeStruct((s, d), BF16),
                   jax.ShapeDtypeStruct((s, 2 * d), BF16), jax.ShapeDtypeStruct((1, 2 * d), F32)],
        scratch_shapes=[pltpu.VMEM((8, 2 * d), F32)], compiler_params=_ARB)(gates, b_gate, pa, pb, dmerged)


_N_CHUNK = W_QKV // LANES
_CHUNK_KIND = ([0] * 4 + [1] * 4 + [-1] * 4 + [2] * 6 + [3] * 6 + [-1] * 6)
_CHUNK_ROT = ([False] * 12 + [True] * 12 + [False] * 6)


def _head_lanes():
    return lax.broadcasted_iota(jnp.int32, (1, LANES), 1) < HEAD_DIM


def _head_mean(t, lo):
    s_lo = jnp.sum(jnp.where(lo, t, 0.0), axis=1, keepdims=True)
    s_hi = jnp.sum(jnp.where(lo, 0.0, t), axis=1, keepdims=True)
    return jnp.where(lo, s_lo, s_hi) * (1.0 / HEAD_DIM)


def _rope_tables():
    half = ROT_DIM // 2
    inv_freq = ROPE_THETA ** (-(jnp.arange(half, dtype=F32) * 2.0) / ROT_DIM)
    ang = jnp.arange(SEQ).astype(F32)[:, None] * inv_freq[None, :]
    cos, sin = jnp.cos(ang), jnp.sin(ang)
    rest = HEAD_DIM - ROT_DIM
    z8, zr, one = jnp.zeros((SEQ, half), F32), jnp.zeros((SEQ, rest), F32), jnp.ones((SEQ, rest), F32)
    cos64 = jnp.concatenate([cos, cos, one], axis=1)
    sa64 = jnp.concatenate([z8, sin, zr], axis=1)
    sb64 = jnp.concatenate([-sin, z8, zr], axis=1)
    two = lambda t: jnp.concatenate([t, t], axis=1)
    return two(cos64), two(sa64), two(sb64)


def _prep_fwd(qkv, gains, cos_t, sa_t, sb_t, *, name):
    s = qkv.shape[0]
    half = ROT_DIM // 2

    def body(qkv_ref, g_ref, cos_ref, sa_ref, sb_ref, out_ref):
        lo = _head_lanes()
        cosv, sav, sbv = cos_ref[...], sa_ref[...], sb_ref[...]
        for ch in range(_N_CHUNK):
            sl = slice(ch * LANES, (ch + 1) * LANES)
            t = qkv_ref[:, sl]
            kind = _CHUNK_KIND[ch]
            if kind >= 0:
                rstd = lax.rsqrt(_head_mean(t * t, lo) + EPS)
                t = t * rstd * g_ref[kind:kind + 1, :]
                if _CHUNK_ROT[ch]:
                    t = t * cosv + pltpu.roll(t, half, 1) * sav + pltpu.roll(t, LANES - half, 1) * sbv
            out_ref[:, sl] = t.astype(BF16)

    return _pcall(
        body, name=name, grid=(s // ROW_TILE,),
        in_specs=[_row_spec(W_QKV), _vec_spec(LANES, 4), _row_spec(LANES), _row_spec(LANES), _row_spec(LANES)],
        out_specs=_row_spec(W_QKV), out_shape=jax.ShapeDtypeStruct((s, W_QKV), BF16), compiler_params=_PAR,
    )(qkv, gains, cos_t, sa_t, sb_t)


def _prep_bwd(qkv, dqkvn, gains, cos_t, sa_t, sb_t, *, name):
    s = qkv.shape[0]
    nsteps = s // ROW_TILE
    half = ROT_DIM // 2

    def body(qkv_ref, d_ref, g_ref, cos_ref, sa_ref, sb_ref, out_ref, dg_ref, acc):
        i = pl.program_id(0)

        @pl.when(i == 0)
        def _():
            acc[...] = jnp.zeros_like(acc)

        lo = _head_lanes()
        cosv, sav, sbv = cos_ref[...], sa_ref[...], sb_ref[...]
        for ch in range(_N_CHUNK):
            sl = slice(ch * LANES, (ch + 1) * LANES)
            dt = d_ref[:, sl].astype(F32)
            kind = _CHUNK_KIND[ch]
            if kind >= 0:
                if _CHUNK_ROT[ch]:
                    dt = dt * cosv + pltpu.roll(dt * sav, LANES - half, 1) + pltpu.roll(dt * sbv, half, 1)
                t = qkv_ref[:, sl]
                gv = g_ref[kind:kind + 1, :]
                rstd = lax.rsqrt(_head_mean(t * t, lo) + EPS)
                xhat = t * rstd
                acc[kind] += _fold8(dt * xhat)
                dxhat = dt * gv
                dt = rstd * (dxhat - xhat * _head_mean(dxhat * xhat, lo))
            out_ref[:, sl] = dt.astype(BF16)

        @pl.when(i == nsteps - 1)
        def _():
            t = jnp.sum(acc[...], axis=1)
            dg_ref[...] = t + pltpu.roll(t, HEAD_DIM, 1)

    return _pcall(
        body, name=name, grid=(nsteps,),
        in_specs=[_row_spec(W_QKV), _row_spec(W_QKV), _vec_spec(LANES, 4), _row_spec(LANES), _row_spec(LANES),
                  _row_spec(LANES)],
        out_specs=[_row_spec(W_QKV), _vec_spec(LANES, 4)],
        out_shape=[jax.ShapeDtypeStruct((s, W_QKV), BF16), jax.ShapeDtypeStruct((4, LANES), F32)],
        scratch_shapes=[pltpu.VMEM((4, 8, LANES), F32)], compiler_params=_ARB,
    )(qkv, dqkvn, gains, cos_t, sa_t, sb_t)


_NT = (((1,), (1,)), ((), ()))
_TN = (((0,), (0,)), ((), ()))


def _attn_fwd(qkv, qc0, kc0, vc0, npairs, table, kstart, cls, nk, *, name):
    s = qkv.shape[0]
    per_head = table.shape[1] > 1
    hb = 2 if per_head else 1

    def body(ks_ref, cls_ref, q_ref, k_ref, v_ref, b_ref, o_ref, lse_ref):
        i = pl.program_id(1)
        ks = pl.multiple_of(ks_ref[i], 64)
        q2 = q_ref[...]
        k2 = k_ref[pl.ds(ks, nk), :]
        v2 = v_ref[pl.ds(ks, nk), :]
        lo = _head_lanes()
        outs, lses = [], []
        for h in range(2):
            qm = jnp.where(lo if h == 0 else jnp.logical_not(lo), q2, jnp.zeros_like(q2))
            sc = lax.dot_general(qm, k2, _NT, preferred_element_type=F32) * SCALE + b_ref[0, h if per_head else 0]
            m = jnp.max(sc, axis=1, keepdims=True)
            p = jnp.exp(sc - m)
            l = jnp.sum(p, axis=1, keepdims=True)
            pv = jnp.dot(p.astype(BF16), v2, preferred_element_type=F32)
            outs.append(pv / l)
            lses.append(m + jnp.log(l))
        o_ref[...] = jnp.where(lo, outs[0], outs[1])
        lse_ref[...] = jnp.where(lo, lses[0], lses[1])

    w = npairs * LANES
    grid_spec = pltpu.PrefetchScalarGridSpec(
        num_scalar_prefetch=2, grid=(npairs, N_QBLK),
        in_specs=[
            pl.BlockSpec((Q_BLOCK, LANES), lambda p, i, ks, cl: (i, qc0 + p)),
            pl.BlockSpec((s, LANES), lambda p, i, ks, cl: (0, kc0 + p)),
            pl.BlockSpec((s, LANES), lambda p, i, ks, cl: (0, vc0 + p)),
            pl.BlockSpec((1, hb, Q_BLOCK, nk), lambda p, i, ks, cl: (cl[i], p if per_head else 0, 0, 0)),
        ],
        out_specs=[pl.BlockSpec((Q_BLOCK, LANES), lambda p, i, ks, cl: (i, p)),
                   pl.BlockSpec((Q_BLOCK, LANES), lambda p, i, ks, cl: (i, p))],
    )
    return _pcall(body, name=name, grid_spec=grid_spec,
                  out_shape=[jax.ShapeDtypeStruct((s, w), F32), jax.ShapeDtypeStruct((s, w), F32)],
                  compiler_params=pltpu.CompilerParams(dimension_semantics=("parallel", "arbitrary")),
                  )(kstart, cls, qkv, qkv, qkv, table)


def _attn_bwd(qkv, qc0, kc0, vc0, npairs, table, kstart, cls, nk, do, o, lse, *, name, dlse=None, want_dbias=False):
    s = qkv.shape[0]
    per_head = table.shape[1] > 1
    hb = 2 if per_head else 1
    has_dlse = dlse is not None

    def body(ks_ref, cls_ref, q_ref, k_ref, v_ref, b_ref, do_ref, o_ref, lse_ref, *rest):
        if has_dlse:
            dlse_ref, rest = rest[0], rest[1:]
        dq_ref, dk_ref, dv_ref = rest[0], rest[1], rest[2]
        db_ref = rest[3] if want_dbias else None
        i = pl.program_id(1)

        @pl.when(i == 0)
        def _():
            dk_ref[...] = jnp.zeros_like(dk_ref)
            dv_ref[...] = jnp.zeros_like(dv_ref)

        ks = pl.multiple_of(ks_ref[i], 64)
        q2 = q_ref[...]
        k2 = k_ref[pl.ds(ks, nk), :]
        v2 = v_ref[pl.ds(ks, nk), :]
        do2 = do_ref[...]
        lse2 = lse_ref[...]
        doo = do2 * o_ref[...]
        do2b = do2.astype(BF16)
        lo = _head_lanes()
        lane = lax.broadcasted_iota(jnp.int32, (1, LANES), 1)
        if want_dbias:
            first = jnp.logical_or(i == 0, cls_ref[i] != cls_ref[jnp.maximum(i - 1, 0)])
        dqs, dks, dvs = [], [], []
        for h in range(2):
            mh = lo if h == 0 else jnp.logical_not(lo)
            qm = jnp.where(mh, q2, jnp.zeros_like(q2))
            sc = lax.dot_general(qm, k2, _NT, preferred_element_type=F32) * SCALE + b_ref[0, h if per_head else 0]
            lse_h = jnp.max(jnp.where(mh, lse2, NEG), axis=1, keepdims=True)
            p = jnp.exp(sc - lse_h)
            delta = jnp.sum(jnp.where(mh, doo, 0.0), axis=1, keepdims=True)
            dom = jnp.where(mh, do2b, jnp.zeros_like(do2b))
            dp = lax.dot_general(dom, v2, _NT, preferred_element_type=F32)
            t = dp - delta
            if has_dlse:
                t = t + jnp.sum(jnp.where(lane == h * HEAD_DIM, dlse_ref[...], 0.0), axis=1, keepdims=True)
            ds = p * t
            if want_dbias:
                @pl.when(first)
                def _():
                    db_ref[0, h] = ds

                @pl.when(jnp.logical_not(first))
                def _():
                    db_ref[0, h] += ds
            dsb = ds.astype(BF16)
            dqs.append(jnp.dot(dsb, k2, preferred_element_type=F32))
            dks.append(lax.dot_general(dsb, q2, _TN, preferred_element_type=F32))
            dvs.append(lax.dot_general(p.astype(BF16), do2b, _TN, preferred_element_type=F32))
        dq_ref[...] = jnp.where(lo, dqs[0], dqs[1]) * SCALE
        dk_ref[pl.ds(ks, nk), :] += jnp.where(lo, dks[0], dks[1]) * SCALE
        dv_ref[pl.ds(ks, nk), :] += jnp.where(lo, dvs[0], dvs[1])

    w = npairs * LANES
    blk = lambda: pl.BlockSpec((Q_BLOCK, LANES), lambda p, i, ks, cl: (i, p))
    full = lambda: pl.BlockSpec((s, LANES), lambda p, i, ks, cl: (0, p))
    tab = lambda: pl.BlockSpec((1, hb, Q_BLOCK, nk), lambda p, i, ks, cl: (cl[i], p if per_head else 0, 0, 0))
    in_specs = [
        pl.BlockSpec((Q_BLOCK, LANES), lambda p, i, ks, cl: (i, qc0 + p)),
        pl.BlockSpec((s, LANES), lambda p, i, ks, cl: (0, kc0 + p)),
        pl.BlockSpec((s, LANES), lambda p, i, ks, cl: (0, vc0 + p)),
        tab(), blk(), blk(), blk(),
    ]
    args = [kstart, cls, qkv, qkv, qkv, table, do, o, lse]
    if has_dlse:
        in_specs.append(blk())
        args.append(dlse)
    out_specs = [blk(), full(), full()]
    out_shape = [jax.ShapeDtypeStruct((s, w), F32)] * 3
    if want_dbias:
        assert per_head
        out_specs.append(tab())
        out_shape.append(jax.ShapeDtypeStruct(table.shape, F32))
    grid_spec = pltpu.PrefetchScalarGridSpec(num_scalar_prefetch=2, grid=(npairs, N_QBLK), in_specs=in_specs,
                                             out_specs=out_specs)
    return _pcall(body, name=name, grid_spec=grid_spec, out_shape=out_shape,
                  compiler_params=pltpu.CompilerParams(dimension_semantics=("arbitrary", "arbitrary")))(*args)


def _combine_fwd(os_, lses, *, name):
    s, w = os_[0].shape

    def body(o0, o1, o2, l0, l1, l2, ob_ref):
        ls = [l0[...], l1[...], l2[...]]
        m = jnp.maximum(jnp.maximum(ls[0], ls[1]), ls[2])
        es = [jnp.exp(t - m) for t in ls]
        den = es[0] + es[1] + es[2]
        num = es[0] * o0[...] + es[1] * o1[...] + es[2] * o2[...]
        ob_ref[...] = (num / den).astype(BF16)

    return _pcall(body, name=name, grid=(s // ROW_TILE,), in_specs=[_row_spec(w)] * 6, out_specs=_row_spec(w),
                  out_shape=jax.ShapeDtypeStruct((s, w), BF16), compiler_params=_PAR)(*os_, *lses)


def _combine_bwd(dob, os_, lses, *, name):
    s, w = dob.shape

    def body(dob_ref, o0, o1, o2, l0, l1, l2, d0, d1, d2, e0, e1, e2):
        lo = _head_lanes()
        o_refs, l_refs, do_refs, dl_refs = (o0, o1, o2), (l0, l1, l2), (d0, d1, d2), (e0, e1, e2)
        for ch in range(w // LANES):
            sl = slice(ch * LANES, (ch + 1) * LANES)
            dv = dob_ref[:, sl]
            ls = [r[:, sl] for r in l_refs]
            m = jnp.maximum(jnp.maximum(ls[0], ls[1]), ls[2])
            es = [jnp.exp(t - m) for t in ls]
            den = es[0] + es[1] + es[2]
            ws = [e / den for e in es]
            ts = [_head_mean(dv * r[:, sl], lo) * float(HEAD_DIM) for r in o_refs]
            tbar = ws[0] * ts[0] + ws[1] * ts[1] + ws[2] * ts[2]
            for g in range(3):
                do_refs[g][:, sl] = ws[g] * dv
                dl_refs[g][:, sl] = ws[g] * (ts[g] - tbar)

    return _pcall(body, name=name, grid=(s // ROW_TILE,), in_specs=[_row_spec(w)] * 7, out_specs=[_row_spec(w)] * 6,
                  out_shape=[jax.ShapeDtypeStruct((s, w), F32)] * 6, compiler_params=_PAR)(dob, *os_, *lses)


_NA_CLASS_R0 = (0, NA_QROWS, ROWS - NA_QROWS)
_NA_CLASS_K0 = (0, 0, ROWS - NA_KROWS)
_RPB_ROWS = 3 * NA_QROWS * NA_KROWS
_RPB_ROWS_PAD = 136
_RPB_RO = 2 * NA_KH - 1
_RPB_CO = 2 * NA_KW - 1


def _na_constants():
    a = np.arange(NA_QROWS)
    b = np.arange(NA_KROWS)
    col = np.arange(GRID_W)
    oh_row = np.zeros((_RPB_ROWS_PAD, 16), np.float32)
    vrow = np.zeros((3, NA_QROWS, NA_KROWS), bool)
    for t in range(3):
        qr = _NA_CLASS_R0[t] + a
        kr = _NA_CLASS_K0[t] + b
        rs = np.clip(qr - NA_KH // 2, 0, ROWS - NA_KH)
        vrow[t] = (kr[None, :] >= rs[:, None]) & (kr[None, :] < rs[:, None] + NA_KH)
        ro = kr[None, :] - qr[:, None] + (NA_KH - 1)
        for ai in range(NA_QROWS):
            for bi in range(NA_KROWS):
                if vrow[t, ai, bi]:
                    oh_row[(t * NA_QROWS + ai) * NA_KROWS + bi, ro[ai, bi]] = 1.0
    cs = np.clip(col - NA_KW // 2, 0, GRID_W - NA_KW)
    vcol = (col[None, :] >= cs[:, None]) & (col[None, :] < cs[:, None] + NA_KW)
    co = col[None, :] - col[:, None] + (NA_KW - 1)
    oh_col = np.zeros((GRID_W * GRID_W, LANES), np.float32)
    for qc in range(GRID_W):
        for kc in range(GRID_W):
            if vcol[qc, kc]:
                oh_col[qc * GRID_W + kc, co[qc, kc]] = 1.0
    valid = vrow[:, :, None, :, None] & vcol[None, None, :, None, :]
    mask = np.where(valid, 0.0, NEG).astype(np.float32).reshape(3, 1, Q_BLOCK, NA_NK)
    ks = np.clip(np.arange(N_QBLK) * NA_QROWS - NA_KH // 2, 0, ROWS - NA_KROWS) * GRID_W
    cls = np.ones(N_QBLK, np.int32)
    cls[0], cls[-1] = 0, 2
    return oh_row, oh_col, mask, ks.astype(np.int32), cls


def _rpb_expand(rpb_pad, oh_row, oh_col_t, *, name):
    def body(r_ref, ohr_ref, ohc_ref, o_ref):
        t = jnp.dot(ohr_ref[...], r_ref[0], preferred_element_type=F32, precision=lax.Precision.HIGHEST)
        o_ref[0] = jnp.dot(t, ohc_ref[...], preferred_element_type=F32, precision=lax.Precision.HIGHEST)

    return _pcall(
        body, name=name, grid=(NA_HEADS,),
        in_specs=[pl.BlockSpec((1, 16, LANES), lambda h: (h, 0, 0)), pl.BlockSpec((_RPB_ROWS_PAD, 16), lambda h: (0, 0)),
                  pl.BlockSpec((LANES, GRID_W * GRID_W), lambda h: (0, 0))],
        out_specs=pl.BlockSpec((1, _RPB_ROWS_PAD, GRID_W * GRID_W), lambda h: (h, 0, 0)),
        out_shape=jax.ShapeDtypeStruct((NA_HEADS, _RPB_ROWS_PAD, GRID_W * GRID_W), F32), compiler_params=_PAR,
    )(rpb_pad, oh_row, oh_col_t)


def _rpb_reduce(dx, oh_row_t, oh_col, *, name):
    def body(d_ref, ohr_ref, ohc_ref, o_ref):
        t = jnp.dot(d_ref[0], ohc_ref[...], preferred_element_type=F32, precision=lax.Precision.HIGHEST)
        o_ref[0] = jnp.dot(ohr_ref[...], t, preferred_element_type=F32, precision=lax.Precision.HIGHEST)

    return _pcall(
        body, name=name, grid=(NA_HEADS,),
        in_specs=[pl.BlockSpec((1, _RPB_ROWS_PAD, GRID_W * GRID_W), lambda h: (h, 0, 0)),
                  pl.BlockSpec((16, _RPB_ROWS_PAD), lambda h: (0, 0)),
                  pl.BlockSpec((GRID_W * GRID_W, LANES), lambda h: (0, 0))],
        out_specs=pl.BlockSpec((1, 16, LANES), lambda h: (h, 0, 0)),
        out_shape=jax.ShapeDtypeStruct((NA_HEADS, 16, LANES), F32), compiler_params=_PAR,
    )(dx, oh_row_t, oh_col)


def _dil_constants(dilation):
    seg = SEQ // dilation
    nb = seg // Q_BLOCK
    shift = (0, -DIL_HALF, -2 * DIL_HALF)
    qi = np.arange(Q_BLOCK)[:, None]
    ki = np.arange(DIL_NK)[None, :]
    mask = np.stack([np.where(np.abs(ki + sh - qi) <= DIL_HALF, 0.0, NEG) for sh in shift]).astype(np.float32)
    ks, cls = [], []
    for i in range(N_QBLK):
        sub, blk = divmod(i, nb)
        t = 0 if blk == 0 else (2 if blk == nb - 1 else 1)
        cls.append(t)
        ks.append(sub * seg + blk * Q_BLOCK + shift[t])
    return mask.reshape(3, 1, Q_BLOCK, DIL_NK), np.asarray(ks, np.int32), np.asarray(cls, np.int32)


_VM = pl.BlockSpec(memory_space=pltpu.VMEM)


def _ada_fwd(c_all, w, b, *, name):
    def body(c_ref, w_ref, b_ref, o_ref):
        cv = c_ref[...]
        o_ref[...] = jnp.dot(cv * _sigmoid(cv), w_ref[...], preferred_element_type=F32,
                             precision=lax.Precision.HIGHEST) + b_ref[...]

    return _pcall(body, name=name, in_specs=[_VM, _VM, _VM], out_specs=_VM,
                  out_shape=jax.ShapeDtypeStruct((c_all.shape[0], w.shape[1]), F32))(c_all, w, b)


def _ada_bwd(c_all_t, dmod, *, name):
    def body(c_ref, d_ref, o_ref):
        cv = c_ref[...]
        o_ref[...] = jnp.dot(cv * _sigmoid(cv), d_ref[...], preferred_element_type=F32,
                             precision=lax.Precision.HIGHEST)

    return _pcall(body, name=name, in_specs=[_VM, _VM], out_specs=_VM,
                  out_shape=jax.ShapeDtypeStruct((c_all_t.shape[0], dmod.shape[1]), F32))(c_all_t, dmod)


def _row_sum(t, *, name):
    def body(t_ref, o_ref):
        o_ref[...] = jnp.sum(t_ref[...], axis=0, keepdims=True)

    return _pcall(body, name=name, in_specs=[_VM], out_specs=_VM,
                  out_shape=jax.ShapeDtypeStruct((1, t.shape[1]), F32))(t)


def _adamw(w, g, m, v, *, name):
    rows, cols = w.shape
    tr = rows
    for cand in range(8, 513, 8):
        if rows % cand == 0:
            tr = cand
    c1 = 1.0 - ADAM_B1 ** ADAM_STEP
    c2 = 1.0 - ADAM_B2 ** ADAM_STEP

    def body(w_ref, g_ref, m_ref, v_ref, d_ref, nm_ref, nv_ref):
        gv = g_ref[...]
        nm = ADAM_B1 * m_ref[...] + (1.0 - ADAM_B1) * gv
        nv = ADAM_B2 * v_ref[...] + (1.0 - ADAM_B2) * (gv * gv)
        m_hat = nm / c1
        v_hat = nv / c2
        d_ref[...] = -ADAM_LR * (m_hat / (jnp.sqrt(v_hat) + ADAM_EPS) + ADAM_WD * w_ref[...])
        nm_ref[...] = nm
        nv_ref[...] = nv

    spec = pl.BlockSpec((tr, cols), lambda i: (i, 0))
    return _pcall(body, name=name, grid=(rows // tr,), in_specs=[spec] * 4, out_specs=[spec] * 3,
                  out_shape=[jax.ShapeDtypeStruct((rows, cols), F32)] * 3, compiler_params=_PAR)(w, g, m, v)


_MESH = pl.DeviceIdType.MESH
_ANY = pl.BlockSpec(memory_space=pl.ANY)
_CHIP_FLIPS = ((1, 0), (0, 1), (1, 1))


def _pos():
    return lax.axis_index("x"), lax.axis_index("y"), lax.axis_index("c")


def _flip(v, f):
    return 1 - v if f else v


def _small_allgather(blk, *, name):
    m_per, n = blk.shape

    def body(x_ref, out_ref, send_sems, recv_sems, local_sem):
        x, y, c = _pos()
        me, sibling = (x, y, c), (x, y, 1 - c)
        chips = [(_flip(x, fx), _flip(y, fy)) for fx, fy in _CHIP_FLIPS]

        def rows(px, py, pc):
            return out_ref.at[pl.ds((4 * px + 2 * py + pc) * m_per, m_per), :]

        def copy(k, block, to, src=None):
            return pltpu.make_async_remote_copy(
                src_ref=rows(*block) if src is None else src, dst_ref=rows(*block),
                send_sem=send_sems.at[k], recv_sem=recv_sems.at[k], device_id=to, device_id_type=_MESH)

        mine = pltpu.make_async_copy(x_ref, rows(*me), local_sem)
        mine.start()
        first = [copy(0, me, sibling, src=x_ref)]
        first += [copy(1 + j, me, (*chip, c), src=x_ref) for j, chip in enumerate(chips)]
        for cp in first:
            cp.start()
        passed = [copy(4 + j, (*chip, c), sibling) for j, chip in enumerate(chips)]
        for j, chip in enumerate(chips):
            copy(1 + j, (*chip, c), me).wait_recv()
            passed[j].start()
        copy(0, sibling, me).wait_recv()
        for j, chip in enumerate(chips):
            copy(4 + j, (*chip, 1 - c), me).wait_recv()
        for cp in first + passed:
            cp.wait_send()
        mine.wait()

    return _pcall(
        body, name=name, out_shape=jax.ShapeDtypeStruct((N_DEV * m_per, n), blk.dtype),
        in_specs=[_VM], out_specs=_VM,
        scratch_shapes=[pltpu.SemaphoreType.DMA((7,)), pltpu.SemaphoreType.DMA((7,)), pltpu.SemaphoreType.DMA],
    )(blk)


def _allgather_weights(shard, *, name):
    _, r, w = shard.shape

    def body(sh_ref, full_ref, send_sems, recv_sems, local_sem):
        x, y, c = _pos()
        j = 2 * x + y
        chips = [(_flip(x, fx), _flip(y, fy)) for fx, fy in _CHIP_FLIPS]

        def copy(k, src, dst, to):
            return pltpu.make_async_remote_copy(src_ref=src, dst_ref=dst, send_sem=send_sems.at[k],
                                                recv_sem=recv_sems.at[k], device_id=to, device_id_type=_MESH)

        mine = pltpu.make_async_copy(sh_ref, full_ref.at[j], local_sem)
        mine.start()
        first = [copy(k, sh_ref.at[c], full_ref.at[j, c], (px, py, c)) for k, (px, py) in enumerate(chips)]
        for cp in first:
            cp.start()
        passed = []
        for k, (px, py) in enumerate(chips):
            jp = 2 * px + py
            copy(k, sh_ref.at[c], full_ref.at[jp, c], (x, y, c)).wait_recv()
            cp = copy(3 + k, full_ref.at[jp, c], full_ref.at[jp, c], (x, y, 1 - c))
            cp.start()
            passed.append(cp)
        for k, (px, py) in enumerate(chips):
            jp = 2 * px + py
            copy(3 + k, sh_ref.at[c], full_ref.at[jp, 1 - c], (x, y, c)).wait_recv()
        for cp in first + passed:
            cp.wait_send()
        mine.wait()

    return _pcall(
        body, name=name, out_shape=jax.ShapeDtypeStruct((N_CHIP, 2, r, w), shard.dtype),
        in_specs=[_ANY], out_specs=_ANY,
        scratch_shapes=[pltpu.SemaphoreType.DMA((6,)), pltpu.SemaphoreType.DMA((6,)), pltpu.SemaphoreType.DMA],
    )(shard)


def _sibling_send_halves(g, *, name):
    n, _, r, w = g.shape

    def body(g_ref, out_ref, send_sems, recv_sems):
        x, y, c = _pos()
        cps = [pltpu.make_async_remote_copy(src_ref=g_ref.at[k, 1 - c], dst_ref=out_ref.at[k], send_sem=send_sems.at[k],
                                            recv_sem=recv_sems.at[k], device_id=(x, y, 1 - c), device_id_type=_MESH)
               for k in range(n)]
        for cp in cps:
            cp.start()
        for cp in cps:
            cp.wait()

    return _pcall(body, name=name, out_shape=jax.ShapeDtypeStruct((n, r, w), g.dtype), in_specs=[_ANY], out_specs=_ANY,
                  scratch_shapes=[pltpu.SemaphoreType.DMA((n,)), pltpu.SemaphoreType.DMA((n,))])(g)


def _chip_exchange(sb, *, name):
    _, r, w = sb.shape

    def body(s_ref, out_ref, send_sems, recv_sems):
        x, y, c = _pos()
        cps = []
        for k, (fx, fy) in enumerate(_CHIP_FLIPS):
            px, py = _flip(x, fx), _flip(y, fy)
            cps.append(pltpu.make_async_remote_copy(
                src_ref=s_ref.at[2 * px + py], dst_ref=out_ref.at[k], send_sem=send_sems.at[k], recv_sem=recv_sems.at[k],
                device_id=(px, py, c), device_id_type=_MESH))
        for cp in cps:
            cp.start()
        for cp in cps:
            cp.wait()

    return _pcall(body, name=name, out_shape=jax.ShapeDtypeStruct((3, r, w), sb.dtype), in_specs=[_ANY], out_specs=_ANY,
                  scratch_shapes=[pltpu.SemaphoreType.DMA((3,)), pltpu.SemaphoreType.DMA((3,))])(sb)


def _sibling_allgather(t, *, name):
    r, w = t.shape

    def body(t_ref, out_ref, send_sem, recv_sem, local_sem):
        x, y, c = _pos()
        mine = pltpu.make_async_copy(t_ref, out_ref.at[c], local_sem)
        mine.start()
        cp = pltpu.make_async_remote_copy(src_ref=t_ref, dst_ref=out_ref.at[c], send_sem=send_sem, recv_sem=recv_sem,
                                          device_id=(x, y, 1 - c), device_id_type=_MESH)
        cp.start()
        cp.wait()
        mine.wait()

    return _pcall(body, name=name, out_shape=jax.ShapeDtypeStruct((2, r, w), t.dtype), in_specs=[_ANY], out_specs=_ANY,
                  scratch_shapes=[pltpu.SemaphoreType.DMA, pltpu.SemaphoreType.DMA, pltpu.SemaphoreType.DMA])(t)


_RS_TILE = 336


def _rs_add(g, ra, c_arr, *, name):
    n, _, r, w = g.shape

    def body(c_ref, g_ref, ra_ref, s_ref, sb_ref):
        t = g_ref[0, 0] + ra_ref[0]
        s_ref[0] = t
        sb_ref[0] = t.astype(BF16)

    grid_spec = pltpu.PrefetchScalarGridSpec(
        num_scalar_prefetch=1, grid=(n, r // _RS_TILE),
        in_specs=[pl.BlockSpec((1, 1, _RS_TILE, w), lambda k, i, c: (k, c[0], i, 0)),
                  pl.BlockSpec((1, _RS_TILE, w), lambda k, i, c: (k, i, 0))],
        out_specs=[pl.BlockSpec((1, _RS_TILE, w), lambda k, i, c: (k, i, 0))] * 2)
    return _pcall(body, name=name, grid_spec=grid_spec,
                  out_shape=[jax.ShapeDtypeStruct((n, r, w), F32), jax.ShapeDtypeStruct((n, r, w), BF16)],
                  compiler_params=pltpu.CompilerParams(dimension_semantics=("parallel", "parallel")))(c_arr, g, ra)


def _rs_final(s, rb, j_arr, *, name):
    _, r, w = s.shape

    def body(j_ref, s_ref, rb_ref, t_ref):
        t_ref[...] = ((s_ref[0] + rb_ref[0].astype(F32)) + rb_ref[1].astype(F32)) + rb_ref[2].astype(F32)

    grid_spec = pltpu.PrefetchScalarGridSpec(
        num_scalar_prefetch=1, grid=(r // _RS_TILE,),
        in_specs=[pl.BlockSpec((1, _RS_TILE, w), lambda i, j: (j[0], i, 0)),
                  pl.BlockSpec((3, _RS_TILE, w), lambda i, j: (0, i, 0))],
        out_specs=pl.BlockSpec((_RS_TILE, w), lambda i, j: (i, 0)))
    return _pcall(body, name=name, grid_spec=grid_spec, out_shape=jax.ShapeDtypeStruct((r, w), F32),
                  compiler_params=_PAR)(j_arr, s, rb)


def _perm_rows(t, d):
    s, w = t.shape
    return t.reshape(s // d, d, w).transpose(1, 0, 2).reshape(s, w)


def _unperm_rows(t, d):
    s, w = t.shape
    return t.reshape(d, s // d, w).transpose(1, 0, 2).reshape(s, w)


def _tile2(g):
    return jnp.concatenate([g, g], axis=1)


def _device_step(x2, tgt, mod, wts, g_norm1, g_norm2, b_gate, g_qa, g_ka, g_qb, g_kb, rpb):
    d = D_MODEL
    sh1, sc1, gt1, sh2, sc2, gt2 = [mod[:, k * d:(k + 1) * d] for k in range(6)]

    oh_row, oh_col, na_mask, na_ks, na_cls = _na_constants()
    rpb_pad = jnp.pad(rpb, ((0, 0), (0, 16 - _RPB_RO), (0, LANES - _RPB_CO)))
    tab = _rpb_expand(rpb_pad, jnp.asarray(oh_row), jnp.asarray(oh_col.T.copy()), name="rpb_expand")
    tab = tab[:, :_RPB_ROWS].reshape(NA_HEADS, 3, NA_QROWS, NA_KROWS, GRID_W, GRID_W)
    tab_a = tab.transpose(1, 0, 2, 4, 3, 5).reshape(3, NA_HEADS, Q_BLOCK, NA_NK) + jnp.asarray(na_mask)
    na_ks, na_cls = jnp.asarray(na_ks), jnp.asarray(na_cls)
    dil = [_dil_constants(dd) for _, dd in DIL_CONFIGS]
    tab_d = jnp.asarray(dil[0][0])
    gains = jnp.concatenate([_tile2(g_qa), _tile2(g_ka), _tile2(g_qb), _tile2(g_kb)], axis=0)
    cos_t, sa_t, sb_t = _rope_tables()

    h1 = _norm_fwd(x2, g_norm1, sc1, sh1, name="norm1_fwd")
    qkv = _matmul(h1, wts["w_qkv"], name="mm_qkv", tn_cap=1920)
    gates = _matmul(h1, wts["w_gates"], name="mm_gates")
    qkvn = _prep_fwd(qkv, gains, cos_t, sa_t, sb_t, name="prep_fwd")
    o_a, lse_a = _attn_fwd(qkvn, 0, 4, 8, 4, tab_a, na_ks, na_cls, NA_NK, name="attn_a_fwd")
    arrs, o_p, l_p, o_g, l_g = [], [], [], [], []
    for g, (_, dd) in enumerate(DIL_CONFIGS):
        ks_g, cls_g = jnp.asarray(dil[g][1]), jnp.asarray(dil[g][2])
        if dd == 1:
            arr, cb = qkvn, (12, 18, 24)
        else:
            col = lambda base: qkvn[:, base + WB_OUT * g: base + WB_OUT * (g + 1)]
            arr = _perm_rows(jnp.concatenate([col(3 * WA), col(3 * WA + WB), col(3 * WA + 2 * WB)], axis=1), dd)
            cb = (0, 2, 4)
        op, lp = _attn_fwd(arr, cb[0], cb[1], cb[2], 2, tab_d, ks_g, cls_g, DIL_NK, name=f"attn_d{g}_fwd")
        arrs.append((arr, cb, ks_g, cls_g))
        o_p.append(op)
        l_p.append(lp)
        o_g.append(op if dd == 1 else _unperm_rows(op, dd))
        l_g.append(lp if dd == 1 else _unperm_rows(lp, dd))
    o_b = _combine_fwd(o_g, l_g, name="combine_fwd")
    pa = _matmul(o_a, wts["w_pa"], name="mm_pa")
    pb = _matmul(o_b, wts["w_pb"], name="mm_pb")
    merged = _gate_fwd(gates, b_gate, pa, pb, name="gate_fwd")
    mo = _matmul(merged, wts["w_o"], name="mm_o")
    x1, h2 = _norm_fwd(x2, g_norm2, sc2, sh2, name="norm2_fwd", res=mo, gt=gt1)
    ff = _matmul(h2, wts["w_ffn_in"], name="mm_ffn_in", tn_cap=2816)
    act = _swiglu_fwd(ff, name="swiglu_fwd")
    ffo = _matmul(act, wts["w_ffn_out"], name="mm_ffn_out")

    dy, dffo, dgt2, loss_v = _loss_bwd(x1, ffo, tgt, gt2, name="loss_bwd")
    grads = {}
    grads["w_ffn_out"] = _matmul(act, dffo, name="mmg_ffn_out", ta=True, tm_cap=1408, tn_cap=1024, tk_cap=512)
    dact = _matmul(dffo, wts["w_ffn_out"], name="mmd_ffn_out", tb=True, tn_cap=2816)
    dff = _swiglu_bwd(ff, dact, name="swiglu_bwd")
    grads["w_ffn_in"] = _matmul(h2, dff, name="mmg_ffn_in", ta=True, tm_cap=512, tn_cap=2816, tk_cap=512)
    dh2 = _matmul(dff, wts["w_ffn_in"], name="mmd_ffn_in", tb=True, tm_cap=1024, tk_cap=2816)
    dx1, sums2, dmo = _norm_bwd(x1, dh2, dy, g_norm2, sc2, name="norm2_bwd", mo=mo, gt=gt1)
    grads["w_o"] = _matmul(merged, dmo, name="mmg_o", ta=True, tm_cap=1024, tn_cap=1024, tk_cap=512)
    dmerged = _matmul(dmo, wts["w_o"], name="mmd_o", tb=True)
    dpa, dpb, dgates, dbg = _gate_bwd(gates, b_gate, pa, pb, dmerged, name="gate_bwd")
    grads["w_proj_a"] = _matmul(o_a, dpa, name="mmg_pa", ta=True, tm_cap=512, tn_cap=1024, tk_cap=512)
    do_a = _matmul(dpa, wts["w_pa"], name="mmd_pa", tb=True)
    grads["w_proj_b"] = _matmul(o_b, dpb, name="mmg_pb", ta=True, tm_cap=256, tn_cap=1024, tk_cap=512)
    do_b = _matmul(dpb, wts["w_pb"], name="mmd_pb", tb=True)
    cb_out = _combine_bwd(do_b, o_g, l_g, name="combine_bwd")
    do_g, dl_g = cb_out[:3], cb_out[3:]
    dqa, dka, dva, dtab = _attn_bwd(qkvn, 0, 4, 8, 4, tab_a, na_ks, na_cls, NA_NK, do_a, o_a, lse_a,
                                    name="attn_a_bwd", want_dbias=True)
    dqs, dks, dvs = [], [], []
    for g, (_, dd) in enumerate(DIL_CONFIGS):
        arr, cb, ks_g, cls_g = arrs[g]
        dog = do_g[g] if dd == 1 else _perm_rows(do_g[g], dd)
        dlg = dl_g[g] if dd == 1 else _perm_rows(dl_g[g], dd)
        dq, dk, dv = _attn_bwd(arr, cb[0], cb[1], cb[2], 2, tab_d, ks_g, cls_g, DIL_NK, dog, o_p[g], l_p[g],
                               name=f"attn_d{g}_bwd", dlse=dlg)
        if dd != 1:
            dq, dk, dv = _unperm_rows(dq, dd), _unperm_rows(dk, dd), _unperm_rows(dv, dd)
        dqs.append(dq)
        dks.append(dk)
        dvs.append(dv)
    dqkvn = jnp.concatenate([dqa, dka, dva] + dqs + dks + dvs, axis=1)
    dqkv, dgains = _prep_bwd(qkv, dqkvn, gains, cos_t, sa_t, sb_t, name="prep_bwd")
    g_qkv = _matmul(h1, dqkv, name="mmg_qkv", ta=True, tm_cap=512, tn_cap=1920, tk_cap=512)
    g_gates = _matmul(h1, dgates, name="mmg_gates", ta=True, tm_cap=512, tn_cap=2048, tk_cap=512)
    grads["w_in"] = jnp.concatenate([g_qkv, g_gates], axis=1)
    dh1 = _matmul(dqkv, wts["w_qkv"], name="mmd_qkv", tb=True, tk_cap=1920)
    dh1 = _matmul(dgates, wts["w_gates"], name="mmd_gates", tb=True, add=dh1)
    grad_x, sums1 = _norm_bwd(x2, dh1, dx1, g_norm1, sc1, name="norm1_bwd")

    dtab = dtab.reshape(3, NA_HEADS, NA_QROWS, GRID_W, NA_KROWS, GRID_W).transpose(1, 0, 2, 4, 3, 5)
    dtab = jnp.pad(dtab.reshape(NA_HEADS, _RPB_ROWS, GRID_W * GRID_W), ((0, 0), (0, _RPB_ROWS_PAD - _RPB_ROWS), (0, 0)))
    g_rpb = _rpb_reduce(dtab, jnp.asarray(oh_row.T.copy()), jnp.asarray(oh_col), name="rpb_reduce")
    g_rpb = g_rpb[:, :_RPB_RO, :_RPB_CO]

    dmod = jnp.concatenate([sums1[0:1], sums1[1:2], sums2[3:4], sums2[0:1], sums2[1:2], dgt2], axis=1)
    small = dict(g_norm1=sums1[2:3], g_norm2=sums2[2:3], b_gate=dbg, g_qa=dgains[0:1, :HEAD_DIM],
                 g_ka=dgains[1:2, :HEAD_DIM], g_qb=dgains[2:3, :HEAD_DIM], g_kb=dgains[3:4, :HEAD_DIM], rpb=g_rpb)
    return loss_v, grad_x, grads, dmod, small


_BIG = ("w_in", "w_ffn_in", "w_ffn_out", "w_o", "w_proj_a", "w_proj_b")
_BIG_SHARD = {"w_in": (1024, 1472), "w_ffn_in": (1024, 1408), "w_ffn_out": (704, 1024), "w_o": (256, 1024),
              "w_proj_a": (512, 256), "w_proj_b": (256, 256)}
_BIG_COL_SHARDED = {"w_in": True, "w_ffn_in": True, "w_ffn_out": False, "w_o": False, "w_proj_a": True, "w_proj_b": True}


def _pack(parts):
    return jnp.concatenate([parts[n].reshape(parts[n].shape[:-2] + (-1, 1024)) for n in _BIG], axis=-2)


def _unpack(p):
    out, at = {}, 0
    for n in _BIG:
        r, c = _BIG_SHARD[n]
        rows = r * c // 1024
        out[n] = p[..., at:at + rows, :].reshape(p.shape[:-2] + (r, c))
        at += rows
    assert at == PACK_ROWS
    return out


def _join_chips(n, t):
    _, r, c = t.shape
    if _BIG_COL_SHARDED[n]:
        return t.transpose(1, 0, 2).reshape(r, N_CHIP * c)
    return t.reshape(N_CHIP * r, c)


def _split_chips(n, t):
    r, c = _BIG_SHARD[n]
    if _BIG_COL_SHARDED[n]:
        return t.reshape(r, N_CHIP, c).transpose(1, 0, 2)
    return t.reshape(N_CHIP, r, c)


_SMALL = ("b_ada", "g_norm1", "g_norm2", "b_gate", "g_qa", "g_ka", "g_qb", "g_kb", "rpb")
_SMALL_N = {"b_ada": 6 * D_MODEL, "g_norm1": D_MODEL, "g_norm2": D_MODEL, "b_gate": 2 * D_MODEL, "g_qa": HEAD_DIM,
            "g_ka": HEAD_DIM, "g_qb": HEAD_DIM, "g_kb": HEAD_DIM, "rpb": NA_HEADS * _RPB_RO * _RPB_CO}


def _pack_small(parts):
    flat = [parts[n].reshape(1, _SMALL_N[n]) for n in _SMALL]
    used = sum(_SMALL_N.values())
    return jnp.concatenate(flat + [jnp.zeros((1, STATS_W - used), F32)], axis=1)


def _unpack_small(v, shapes):
    out, at = {}, 0
    for n in _SMALL:
        out[n] = v[:, at:at + _SMALL_N[n]].reshape(shapes[n])
        at += _SMALL_N[n]
    return out


def kernel(x, c, w_ada, b_ada, g_norm1, g_norm2, w_in, b_gate, g_qa, g_ka, g_qb, g_kb, rpb, w_proj_a, w_proj_b, w_o, w_ffn_in, w_ffn_out, loss_target, m_w_ada, m_b_ada, m_g_norm1, m_g_norm2, m_w_in, m_b_gate, m_g_qa, m_g_ka, m_g_qb, m_g_kb, m_rpb, m_w_proj_a, m_w_proj_b, m_w_o, m_w_ffn_in, m_w_ffn_out, v_w_ada, v_b_ada, v_g_norm1, v_g_norm2, v_w_in, v_b_gate, v_g_qa, v_g_ka, v_g_qb, v_g_kb, v_rpb, v_w_proj_a, v_w_proj_b, v_w_o, v_w_ffn_in, v_w_ffn_out):
    names = ("w_ada", "b_ada", "g_norm1", "g_norm2", "w_in", "b_gate", "g_qa", "g_ka", "g_qb", "g_kb", "rpb",
             "w_proj_a", "w_proj_b", "w_o", "w_ffn_in", "w_ffn_out")
    w = dict(zip(names, (w_ada, b_ada, g_norm1, g_norm2, w_in, b_gate, g_qa, g_ka, g_qb, g_kb, rpb, w_proj_a, w_proj_b,
                         w_o, w_ffn_in, w_ffn_out)))
    m = dict(zip(names, (m_w_ada, m_b_ada, m_g_norm1, m_g_norm2, m_w_in, m_b_gate, m_g_qa, m_g_ka, m_g_qb, m_g_kb, m_rpb,
                         m_w_proj_a, m_w_proj_b, m_w_o, m_w_ffn_in, m_w_ffn_out)))
    v = dict(zip(names, (v_w_ada, v_b_ada, v_g_norm1, v_g_norm2, v_w_in, v_b_gate, v_g_qa, v_g_ka, v_g_qb, v_g_kb, v_rpb,
                         v_w_proj_a, v_w_proj_b, v_w_o, v_w_ffn_in, v_w_ffn_out)))
    d = D_MODEL
    xi, yi, ci = _pos()
    chip = 2 * xi + yi
    me = 2 * chip + ci
    ada_cols = 6 * d // N_CHIP

    c_all = _small_allgather(jnp.broadcast_to(c, (8, d)), name="ag_c")[::8]
    b_sh = lax.dynamic_slice(b_ada, (0, chip * ada_cols), (1, ada_cols))
    mod_part = _ada_fwd(c_all, w_ada[0], b_sh, name="ada_fwd")
    mod_all = _small_allgather(mod_part, name="ag_mod").reshape(N_CHIP, 2, 8, ada_cols)[:, 0]
    mod = lax.dynamic_index_in_dim(mod_all, me, axis=1, keepdims=False).reshape(1, 6 * d)

    shard = _pack({n: w[n][0] for n in _BIG}).astype(BF16).reshape(2, HALF_ROWS, 1024)
    full = _unpack(_allgather_weights(shard, name="ag_weights").reshape(N_CHIP, PACK_ROWS, 1024))
    full = {n: _join_chips(n, full[n]) for n in _BIG}
    wts = dict(w_qkv=full["w_in"][:, :W_QKV], w_gates=full["w_in"][:, W_QKV:], w_pa=full["w_proj_a"],
               w_pb=full["w_proj_b"], w_o=full["w_o"], w_ffn_in=full["w_ffn_in"], w_ffn_out=full["w_ffn_out"])

    loss_v, grad_x, grads, dmod, small = _device_step(
        x[0], loss_target[0], mod, wts, g_norm1, g_norm2, b_gate, g_qa, g_ka, g_qb, g_kb, rpb[0])

    gp = _pack({n: _split_chips(n, grads[n]) for n in _BIG}).reshape(N_CHIP, 2, HALF_ROWS, 1024)
    ra = _sibling_send_halves(gp, name="rs_sibling")
    s_f, s_b = _rs_add(gp, ra, ci.reshape(1).astype(jnp.int32), name="rs_add")
    rb = _chip_exchange(s_b, name="rs_chips")
    t = _rs_final(s_f, rb, chip.reshape(1).astype(jnp.int32), name="rs_final")
    g_big = _unpack(_sibling_allgather(t, name="rs_pair").reshape(PACK_ROWS, 1024))

    stats = _pack_small(dict(b_ada=dmod, **small))
    rows = _small_allgather(jnp.broadcast_to(stats, (8, STATS_W)), name="ag_stats")[::8]
    dmod_sh = lax.dynamic_slice(rows, (0, chip * ada_cols), (8, ada_cols))
    g = dict(g_big)
    g["w_ada"] = _ada_bwd(c_all.T, dmod_sh, name="ada_bwd")
    tot = _row_sum(rows, name="stats_sum")
    g_small = _unpack_small(tot, {n: w[n].shape for n in _SMALL})

    delta, new_m, new_v = {}, {}, {}
    for n in _BIG + ("w_ada",):
        dl, nm, nv = _adamw(w[n][0], g[n], m[n][0], v[n][0], name=f"adamw_{n}")
        delta[n], new_m[n], new_v[n], g[n] = dl[None], nm[None], nv[None], g[n][None]
    shapes = {n: w[n].shape for n in _SMALL}
    dl, nm, nv = _adamw(_pack_small({n: w[n] for n in _SMALL}), tot, _pack_small({n: m[n] for n in _SMALL}),
                        _pack_small({n: v[n] for n in _SMALL}), name="adamw_small")
    delta.update(_unpack_small(dl, shapes))
    new_m.update(_unpack_small(nm, shapes))
    new_v.update(_unpack_small(nv, shapes))
    g.update(g_small)

    loss = lax.psum(loss_v[0, 0], ("x", "y", "c"))
    return (loss, grad_x[None], *[g[n] for n in names], *[delta[n] for n in names], *[new_m[n] for n in names],
            *[new_v[n] for n in names])
```

```python
import numpy as np

import jax
import jax.numpy as jnp
from jax import lax
from jax.experimental import pallas as pl
from jax.experimental.pallas import tpu as pltpu

F32 = jnp.float32
BF16 = jnp.bfloat16

D_MODEL = 1024
SEQ = 8192
HEAD_DIM = 64
GRID_W = 64
ROWS = SEQ // GRID_W
NA_HEADS = 8
NA_KH = 8
NA_KW = 16
DIL_CONFIGS = ((128, 1), (512, 4), (2048, 16))
DIL_HEADS_PER_GROUP = 4
DIL_HEADS = 12
ROT_DIM = 16
ROPE_THETA = 500000.0
D_FF = 2816
EPS = 1e-6
NEG = -1e30
WA = 512
WB = 768
WB_OUT = 256
W_QKV = 3 * WA + 3 * WB
W_GATES = 2 * D_MODEL
SCALE = HEAD_DIM ** -0.5

ADAM_LR = 0.001
ADAM_B1 = 0.9
ADAM_B2 = 0.999
ADAM_EPS = 1e-08
ADAM_WD = 0.01
ADAM_STEP = 10

LANES = 128
ROW_TILE = 256
Q_BLOCK = 256
NA_QROWS = Q_BLOCK // GRID_W
NA_KROWS = NA_QROWS + NA_KH - 1
NA_NK = NA_KROWS * GRID_W
DIL_HALF = 64
DIL_NK = Q_BLOCK + 2 * DIL_HALF
N_QBLK = SEQ // Q_BLOCK

N_DEV = 8
N_CHIP = 4
PACK_ROWS = 4032
HALF_ROWS = PACK_ROWS // 2
STATS_W = 14336


def _pcall(body, *, name, **kw):
    return pl.pallas_call(body, name=name, **kw)


def _pick(n, cap):
    if n <= cap:
        return n
    best = None
    for t in range(LANES, cap + 1, LANES):
        if n % t == 0:
            best = t
    assert best is not None, (n, cap)
    return best


def _matmul(a, b, *, name, ta=False, tb=False, out_dtype=F32, add=None, out_chips=None, tm_cap=512, tn_cap=2944,
            tk_cap=2944):
    b3 = b.ndim == 3
    if ta:
        kd, m = a.shape
    else:
        m, kd = a.shape
    if b3:
        nchip, d1, nc = b.shape
        n, kb = (d1, nchip * nc) if tb else (nchip * nc, d1)
    elif tb:
        n, kb = b.shape
    else:
        kb, n = b.shape
    assert kd == kb, (a.shape, b.shape)
    tm, tn, tk = _pick(m, tm_cap), _pick(n, tn_cap), _pick(kd, tk_cap)
    if b3 and tb:
        tk = nc
    elif b3:
        tn = nc
    if out_chips is not None:
        tn = n // out_chips
    nk = kd // tk
    dims = (((0 if ta else 1,), (1 if tb else 0,)), ((), ()))
    has_add = add is not None

    def body(*refs):
        a_ref, b_ref = refs[0], refs[1]
        add_ref = refs[2] if has_add else None
        o_ref = refs[3] if has_add else refs[2]
        r = lax.dot_general(a_ref[...].astype(BF16), b_ref[...].astype(BF16), dims, preferred_element_type=F32)
        if nk == 1:
            if has_add:
                r = r + add_ref[...].astype(F32)
            o_ref[...] = r.astype(out_dtype)
        else:
            acc = refs[-1]
            k = pl.program_id(2)

            @pl.when(k == 0)
            def _():
                acc[...] = r

            @pl.when(k > 0)
            def _():
                acc[...] += r

            @pl.when(k == nk - 1)
            def _():
                t = acc[...]
                if has_add:
                    t = t + add_ref[...].astype(F32)
                o_ref[...] = t.astype(out_dtype)

    a_spec = pl.BlockSpec((tk, tm), lambda i, j, k: (k, i)) if ta else pl.BlockSpec((tm, tk), lambda i, j, k: (i, k))
    if b3 and tb:
        b_spec = pl.BlockSpec((None, tn, tk), lambda i, j, k: (k, j, 0))
    elif b3:
        b_spec = pl.BlockSpec((None, tk, tn), lambda i, j, k: (j, k, 0))
    elif tb:
        b_spec = pl.BlockSpec((tn, tk), lambda i, j, k: (j, k))
    else:
        b_spec = pl.BlockSpec((tk, tn), lambda i, j, k: (k, j))
    if out_chips is not None:
        assert not has_add
        o_spec = pl.BlockSpec((None, tm, tn), lambda i, j, k: (j, i, 0))
        out_shape = jax.ShapeDtypeStruct((out_chips, m, tn), out_dtype)
    else:
        o_spec = pl.BlockSpec((tm, tn), lambda i, j, k: (i, j))
        out_shape = jax.ShapeDtypeStruct((m, n), out_dtype)
    in_specs = [a_spec, b_spec] + ([o_spec] if has_add else [])
    args = (a, b) + ((add,) if has_add else ())
    return _pcall(
        body, name=name, grid=(m // tm, n // tn, nk), in_specs=in_specs, out_specs=o_spec,
        out_shape=out_shape,
        scratch_shapes=[pltpu.VMEM((tm, tn), F32)] if nk > 1 else [],
        compiler_params=pltpu.CompilerParams(dimension_semantics=("parallel", "parallel", "arbitrary")),
    )(*args)


def _row_spec(w, tr=ROW_TILE):
    return pl.BlockSpec((tr, w), lambda i: (i, 0))


def _vec_spec(w, r=1):
    return pl.BlockSpec((r, w), lambda i: (0, 0))


def _fold8(t):
    r, w = t.shape
    return jnp.sum(t.reshape(r // 8, 8, w), axis=0)


_ARB = pltpu.CompilerParams(dimension_semantics=("arbitrary",))
_PAR = pltpu.CompilerParams(dimension_semantics=("parallel",))


def _norm_fwd(x, g, sc, sh, *, name, res=None, gt=None):
    s, d = x.shape
    has_res = res is not None

    def body(*refs):
        if has_res:
            x_ref, res_ref, gt_ref, g_ref, sc_ref, sh_ref, x1_ref, h_ref = refs
            xv = x_ref[...] + gt_ref[...] * res_ref[...]
            x1_ref[...] = xv
        else:
            x_ref, g_ref, sc_ref, sh_ref, h_ref = refs
            xv = x_ref[...]
        rstd = lax.rsqrt(jnp.mean(xv * xv, axis=1, keepdims=True) + EPS)
        h = (xv * rstd * g_ref[...]) * (1.0 + sc_ref[...]) + sh_ref[...]
        h_ref[...] = h.astype(BF16)

    if has_res:
        in_specs = [_row_spec(d), _row_spec(d), _vec_spec(d), _vec_spec(d), _vec_spec(d), _vec_spec(d)]
        args = (x, res, gt, g, sc, sh)
        out_specs = [_row_spec(d), _row_spec(d)]
        out_shape = [jax.ShapeDtypeStruct((s, d), F32), jax.ShapeDtypeStruct((s, d), BF16)]
    else:
        in_specs = [_row_spec(d), _vec_spec(d), _vec_spec(d), _vec_spec(d)]
        args = (x, g, sc, sh)
        out_specs = _row_spec(d)
        out_shape = jax.ShapeDtypeStruct((s, d), BF16)
    return _pcall(body, name=name, grid=(s // ROW_TILE,), in_specs=in_specs, out_specs=out_specs,
                  out_shape=out_shape, compiler_params=_PAR)(*args)


def _norm_bwd(x, dh, dres, g, sc, *, name, mo=None, gt=None):
    s, d = x.shape
    nsteps = s // ROW_TILE
    has_mo = mo is not None

    def body(*refs):
        if has_mo:
            x_ref, dh_ref, dres_ref, g_ref, sc_ref, mo_ref, gt_ref, dx_ref, sums_ref, dmo_ref, acc = refs
        else:
            x_ref, dh_ref, dres_ref, g_ref, sc_ref, dx_ref, sums_ref, acc = refs
        i = pl.program_id(0)

        @pl.when(i == 0)
        def _():
            acc[...] = jnp.zeros_like(acc)

        xv = x_ref[...]
        dhv = dh_ref[...].astype(F32)
        gv = g_ref[...]
        rstd = lax.rsqrt(jnp.mean(xv * xv, axis=1, keepdims=True) + EPS)
        xhat = xv * rstd
        dn = dhv * (1.0 + sc_ref[...])
        dxhat = dn * gv
        dxn = rstd * (dxhat - xhat * jnp.mean(dxhat * xhat, axis=1, keepdims=True))
        dx = dres_ref[...] + dxn
        dx_ref[...] = dx
        acc[0] += _fold8(dhv)
        acc[1] += _fold8(dhv * (xhat * gv))
        acc[2] += _fold8(dn * xhat)
        if has_mo:
            dmo_ref[...] = (gt_ref[...] * dx).astype(BF16)
            acc[3] += _fold8(dx * mo_ref[...])

        @pl.when(i == nsteps - 1)
        def _():
            sums_ref[...] = jnp.sum(acc[...], axis=1)

    nacc = 4 if has_mo else 3
    in_specs = [_row_spec(d), _row_spec(d), _row_spec(d), _vec_spec(d), _vec_spec(d)]
    args = [x, dh, dres, g, sc]
    out_specs = [_row_spec(d), _vec_spec(d, nacc)]
    out_shape = [jax.ShapeDtypeStruct((s, d), F32), jax.ShapeDtypeStruct((nacc, d), F32)]
    if has_mo:
        in_specs += [_row_spec(d), _vec_spec(d)]
        args += [mo, gt]
        out_specs.append(_row_spec(d))
        out_shape.append(jax.ShapeDtypeStruct((s, d), BF16))
    return _pcall(body, name=name, grid=(nsteps,), in_specs=in_specs, out_specs=out_specs, out_shape=out_shape,
                  scratch_shapes=[pltpu.VMEM((nacc, 8, d), F32)], compiler_params=_ARB)(*args)


def _loss_bwd(x1, ffo, tgt, gt2, *, name):
    s, d = x1.shape
    nsteps = s // ROW_TILE

    def body(x1_ref, ffo_ref, tgt_ref, gt_ref, dy_ref, dffo_ref, dgt_ref, loss_ref, acc):
        i = pl.program_id(0)

        @pl.when(i == 0)
        def _():
            acc[...] = jnp.zeros_like(acc)

        ffo = ffo_ref[...]
        gtv = gt_ref[...]
        e = x1_ref[...] + gtv * ffo - tgt_ref[...]
        dy = e * (1.0 / d)
        dy_ref[...] = dy
        dffo_ref[...] = (gtv * dy).astype(BF16)
        acc[0] += _fold8(dy * ffo)
        acc[1] += _fold8(e * e)

        @pl.when(i == nsteps - 1)
        def _():
            dgt_ref[...] = jnp.sum(acc[0], axis=0, keepdims=True)
            tot = jnp.sum(jnp.sum(acc[1], axis=0, keepdims=True), axis=1, keepdims=True)
            loss_ref[...] = jnp.broadcast_to(tot * (0.5 / d), (1, LANES))

    return _pcall(
        body, name=name, grid=(nsteps,),
        in_specs=[_row_spec(d), _row_spec(d), _row_spec(d), _vec_spec(d)],
        out_specs=[_row_spec(d), _row_spec(d), _vec_spec(d), _vec_spec(LANES)],
        out_shape=[jax.ShapeDtypeStruct((s, d), F32), jax.ShapeDtypeStruct((s, d), BF16),
                   jax.ShapeDtypeStruct((1, d), F32), jax.ShapeDtypeStruct((1, LANES), F32)],
        scratch_shapes=[pltpu.VMEM((2, 8, d), F32)], compiler_params=_ARB)(x1, ffo, tgt, gt2)


def _sigmoid(t):
    return 1.0 / (1.0 + jnp.exp(-t))


def _swiglu_fwd(ff, *, name):
    s = ff.shape[0]

    def body(ff_ref, act_ref):
        a = ff_ref[:, :D_FF]
        up = ff_ref[:, D_FF:]
        act_ref[...] = (a * _sigmoid(a) * up).astype(BF16)

    return _pcall(body, name=name, grid=(s // ROW_TILE,), in_specs=[_row_spec(2 * D_FF)], out_specs=_row_spec(D_FF),
                  out_shape=jax.ShapeDtypeStruct((s, D_FF), BF16), compiler_params=_PAR)(ff)


def _swiglu_bwd(ff, dact, *, name):
    s = ff.shape[0]

    def body(ff_ref, dact_ref, dff_ref):
        a = ff_ref[:, :D_FF]
        up = ff_ref[:, D_FF:]
        da = dact_ref[...]
        sg = _sigmoid(a)
        dff_ref[:, :D_FF] = (da * up * (sg * (1.0 + a * (1.0 - sg)))).astype(BF16)
        dff_ref[:, D_FF:] = (da * (a * sg)).astype(BF16)

    return _pcall(body, name=name, grid=(s // ROW_TILE,), in_specs=[_row_spec(2 * D_FF), _row_spec(D_FF)],
                  out_specs=_row_spec(2 * D_FF), out_shape=jax.ShapeDtypeStruct((s, 2 * D_FF), BF16),
                  compiler_params=_PAR)(ff, dact)


def _gate_fwd(gates, b_gate, pa, pb, *, name):
    s, d = pa.shape

    def body(gates_ref, b_ref, pa_ref, pb_ref, m_ref):
        ga = _sigmoid(gates_ref[:, :d] + b_ref[:, :d])
        gb = _sigmoid(gates_ref[:, d:] + b_ref[:, d:])
        m_ref[...] = (ga * pa_ref[...] + gb * pb_ref[...]).astype(BF16)

    return _pcall(body, name=name, grid=(s // ROW_TILE,),
                  in_specs=[_row_spec(2 * d), _vec_spec(2 * d), _row_spec(d), _row_spec(d)], out_specs=_row_spec(d),
                  out_shape=jax.ShapeDtypeStruct((s, d), BF16), compiler_params=_PAR)(gates, b_gate, pa, pb)


def _gate_bwd(gates, b_gate, pa, pb, dmerged, *, name):
    s, d = pa.shape
    nsteps = s // ROW_TILE

    def body(gates_ref, b_ref, pa_ref, pb_ref, dm_ref, dpa_ref, dpb_ref, dgates_ref, dbg_ref, acc):
        i = pl.program_id(0)

        @pl.when(i == 0)
        def _():
            acc[...] = jnp.zeros_like(acc)

        dm = dm_ref[...]
        ga = _sigmoid(gates_ref[:, :d] + b_ref[:, :d])
        gb = _sigmoid(gates_ref[:, d:] + b_ref[:, d:])
        dpa_ref[...] = (dm * ga).astype(BF16)
        dpb_ref[...] = (dm * gb).astype(BF16)
        dga = dm * pa_ref[...] * ga * (1.0 - ga)
        dgb = dm * pb_ref[...] * gb * (1.0 - gb)
        dgates_ref[:, :d] = dga.astype(BF16)
        dgates_ref[:, d:] = dgb.astype(BF16)
        acc[:, :d] += _fold8(dga)
        acc[:, d:] += _fold8(dgb)

        @pl.when(i == nsteps - 1)
        def _():
            dbg_ref[...] = jnp.sum(acc[...], axis=0, keepdims=True)

    return _pcall(
        body, name=name, grid=(nsteps,),
        in_specs=[_row_spec(2 * d), _vec_spec(2 * d), _row_spec(d), _row_spec(d), _row_spec(d)],
        out_specs=[_row_spec(d), _row_spec(d), _row_spec(2 * d), _vec_spec(2 * d)],
        out_shape=[jax.ShapeDtypeStruct((s, d), BF16), jax.ShapeDtypeStruct((s, d), BF16),
                   jax.ShapeDtypeStruct((s, 2 * d), BF16), jax.ShapeDtypeStruct((1, 2 * d), F32)],
        scratch_shapes=[pltpu.VMEM((8, 2 * d), F32)], compiler_params=_ARB)(gates, b_gate, pa, pb, dmerged)


_N_CHUNK = W_QKV // LANES
_CHUNK_KIND = ([0] * 4 + [1] * 4 + [-1] * 4 + [2] * 6 + [3] * 6 + [-1] * 6)
_CHUNK_ROT = ([False] * 12 + [True] * 12 + [False] * 6)


def _head_lanes():
    return lax.broadcasted_iota(jnp.int32, (1, LANES), 1) < HEAD_DIM


def _head_mean(t, lo):
    s_lo = jnp.sum(jnp.where(lo, t, 0.0), axis=1, keepdims=True)
    s_hi = jnp.sum(jnp.where(lo, 0.0, t), axis=1, keepdims=True)
    return jnp.where(lo, s_lo, s_hi) * (1.0 / HEAD_DIM)


def _rope_tables():
    half = ROT_DIM // 2
    inv_freq = ROPE_THETA ** (-(jnp.arange(half, dtype=F32) * 2.0) / ROT_DIM)
    lane = np.arange(LANES) % HEAD_DIM
    ang = jnp.arange(SEQ).astype(F32)[:, None] * jnp.tile(inv_freq, LANES // half)[None, :]
    cos, sin = jnp.cos(ang), jnp.sin(ang)
    first, second = jnp.asarray(lane < half)[None, :], jnp.asarray((lane >= half) & (lane < ROT_DIM))[None, :]
    cos_t = jnp.where(first | second, cos, 1.0)
    return cos_t, jnp.where(second, sin, 0.0), jnp.where(first, -sin, 0.0)


def _prep_fwd(qkv, gains, cos_t, sa_t, sb_t, *, name):
    s = qkv.shape[0]
    half = ROT_DIM // 2

    def body(qkv_ref, g_ref, cos_ref, sa_ref, sb_ref, out_ref):
        lo = _head_lanes()
        cosv, sav, sbv = cos_ref[...], sa_ref[...], sb_ref[...]
        for ch in range(_N_CHUNK):
            sl = slice(ch * LANES, (ch + 1) * LANES)
            t = qkv_ref[:, sl]
            kind = _CHUNK_KIND[ch]
            if kind >= 0:
                rstd = lax.rsqrt(_head_mean(t * t, lo) + EPS)
                t = t * rstd * g_ref[kind:kind + 1, :]
                if _CHUNK_ROT[ch]:
                    t = t * cosv + pltpu.roll(t, half, 1) * sav + pltpu.roll(t, LANES - half, 1) * sbv
            out_ref[:, sl] = t.astype(BF16)

    return _pcall(
        body, name=name, grid=(s // ROW_TILE,),
        in_specs=[_row_spec(W_QKV), _vec_spec(LANES, 4), _row_spec(LANES), _row_spec(LANES), _row_spec(LANES)],
        out_specs=_row_spec(W_QKV), out_shape=jax.ShapeDtypeStruct((s, W_QKV), BF16), compiler_params=_PAR,
    )(qkv, gains, cos_t, sa_t, sb_t)


def _prep_bwd(qkv, d_parts, gains, cos_t, sa_t, sb_t, *, name):
    s = qkv.shape[0]
    nsteps = s // ROW_TILE
    half = ROT_DIM // 2
    nparts = len(d_parts)
    where = []
    for pi, part in enumerate(d_parts):
        where += [(pi, cj) for cj in range(part.shape[1] // LANES)]
    assert len(where) == _N_CHUNK

    def body(*refs):
        qkv_ref, d_refs = refs[0], refs[1:1 + nparts]
        g_ref, cos_ref, sa_ref, sb_ref, out_ref, dg_ref, acc = refs[1 + nparts:]
        i = pl.program_id(0)

        @pl.when(i == 0)
        def _():
            acc[...] = jnp.zeros_like(acc)

        lo = _head_lanes()
        cosv, sav, sbv = cos_ref[...], sa_ref[...], sb_ref[...]
        for ch in range(_N_CHUNK):
            sl = slice(ch * LANES, (ch + 1) * LANES)
            pi, cj = where[ch]
            dt = d_refs[pi][:, cj * LANES:(cj + 1) * LANES].astype(F32)
            kind = _CHUNK_KIND[ch]
            if kind >= 0:
                if _CHUNK_ROT[ch]:
                    dt = dt * cosv + pltpu.roll(dt * sav, LANES - half, 1) + pltpu.roll(dt * sbv, half, 1)
                t = qkv_ref[:, sl]
                gv = g_ref[kind:kind + 1, :]
                rstd = lax.rsqrt(_head_mean(t * t, lo) + EPS)
                xhat = t * rstd
                acc[kind] += _fold8(dt * xhat)
                dxhat = dt * gv
                dt = rstd * (dxhat - xhat * _head_mean(dxhat * xhat, lo))
            out_ref[:, sl] = dt.astype(BF16)

        @pl.when(i == nsteps - 1)
        def _():
            t = jnp.sum(acc[...], axis=1)
            dg_ref[...] = t + pltpu.roll(t, HEAD_DIM, 1)

    return _pcall(
        body, name=name, grid=(nsteps,),
        in_specs=[_row_spec(W_QKV)] + [_row_spec(p.shape[1]) for p in d_parts]
        + [_vec_spec(LANES, 4), _row_spec(LANES), _row_spec(LANES), _row_spec(LANES)],
        out_specs=[_row_spec(W_QKV), _vec_spec(LANES, 4)],
        out_shape=[jax.ShapeDtypeStruct((s, W_QKV), BF16), jax.ShapeDtypeStruct((4, LANES), F32)],
        scratch_shapes=[pltpu.VMEM((4, 8, LANES), F32)], compiler_params=_ARB,
    )(qkv, *d_parts, gains, cos_t, sa_t, sb_t)


_NT = (((1,), (1,)), ((), ()))
_TN = (((0,), (0,)), ((), ()))


def _attn_fwd(qkv, qc0, kc0, vc0, npairs, table, kstart, cls, nk, *, name):
    s = qkv.shape[0]
    per_head = table.shape[1] > 1
    hb = 2 if per_head else 1

    def body(ks_ref, cls_ref, q_ref, k_ref, v_ref, b_ref, o_ref, lse_ref):
        i = pl.program_id(1)
        ks = pl.multiple_of(ks_ref[i], 64)
        q2 = q_ref[...]
        k2 = k_ref[pl.ds(ks, nk), :]
        v2 = v_ref[pl.ds(ks, nk), :]
        lo = _head_lanes()
        outs, lses = [], []
        for h in range(2):
            qm = jnp.where(lo if h == 0 else jnp.logical_not(lo), q2, jnp.zeros_like(q2))
            sc = lax.dot_general(qm, k2, _NT, preferred_element_type=F32) * SCALE + b_ref[0, h if per_head else 0]
            m = jnp.max(sc, axis=1, keepdims=True)
            p = jnp.exp(sc - m)
            l = jnp.sum(p, axis=1, keepdims=True)
            pv = jnp.dot(p.astype(BF16), v2, preferred_element_type=F32)
            outs.append(pv / l)
            lses.append(m + jnp.log(l))
        o_ref[...] = jnp.where(lo, outs[0], outs[1])
        lse_ref[...] = jnp.where(lo, lses[0], lses[1])

    w = npairs * LANES
    grid_spec = pltpu.PrefetchScalarGridSpec(
        num_scalar_prefetch=2, grid=(npairs, N_QBLK),
        in_specs=[
            pl.BlockSpec((Q_BLOCK, LANES), lambda p, i, ks, cl: (i, qc0 + p)),
            pl.BlockSpec((s, LANES), lambda p, i, ks, cl: (0, kc0 + p)),
            pl.BlockSpec((s, LANES), lambda p, i, ks, cl: (0, vc0 + p)),
            pl.BlockSpec((1, hb, Q_BLOCK, nk), lambda p, i, ks, cl: (cl[i], p if per_head else 0, 0, 0)),
        ],
        out_specs=[pl.BlockSpec((Q_BLOCK, LANES), lambda p, i, ks, cl: (i, p)),
                   pl.BlockSpec((Q_BLOCK, LANES), lambda p, i, ks, cl: (i, p))],
    )
    return _pcall(body, name=name, grid_spec=grid_spec,
                  out_shape=[jax.ShapeDtypeStruct((s, w), F32), jax.ShapeDtypeStruct((s, w), F32)],
                  compiler_params=pltpu.CompilerParams(dimension_semantics=("parallel", "arbitrary")),
                  )(kstart, cls, qkv, qkv, qkv, table)


def _attn_bwd(qkv, qc0, kc0, vc0, npairs, table, kstart, cls, nk, do, o, lse, *, name, dlse=None, want_dbias=False):
    s = qkv.shape[0]
    per_head = table.shape[1] > 1
    hb = 2 if per_head else 1
    has_dlse = dlse is not None

    def body(ks_ref, cls_ref, q_ref, k_ref, v_ref, b_ref, do_ref, o_ref, lse_ref, *rest):
        if has_dlse:
            dlse_ref, rest = rest[0], rest[1:]
        dq_ref, dk_ref, dv_ref = rest[0], rest[1], rest[2]
        db_ref = rest[3] if want_dbias else None
        i = pl.program_id(1)

        @pl.when(i == 0)
        def _():
            dk_ref[...] = jnp.zeros_like(dk_ref)
            dv_ref[...] = jnp.zeros_like(dv_ref)

        ks = pl.multiple_of(ks_ref[i], 64)
        q2 = q_ref[...]
        k2 = k_ref[pl.ds(ks, nk), :]
        v2 = v_ref[pl.ds(ks, nk), :]
        do2 = do_ref[...]
        lse2 = lse_ref[...]
        doo = do2 * o_ref[...]
        do2b = do2.astype(BF16)
        lo = _head_lanes()
        lane = lax.broadcasted_iota(jnp.int32, (1, LANES), 1)
        if want_dbias:
            first = jnp.logical_or(i == 0, cls_ref[i] != cls_ref[jnp.maximum(i - 1, 0)])
        dqs, dks, dvs = [], [], []
        for h in range(2):
            mh = lo if h == 0 else jnp.logical_not(lo)
            qm = jnp.where(mh, q2, jnp.zeros_like(q2))
            sc = lax.dot_general(qm, k2, _NT, preferred_element_type=F32) * SCALE + b_ref[0, h if per_head else 0]
            lse_h = jnp.max(jnp.where(mh, lse2, NEG), axis=1, keepdims=True)
            p = jnp.exp(sc - lse_h)
            delta = jnp.sum(jnp.where(mh, doo, 0.0), axis=1, keepdims=True)
            dom = jnp.where(mh, do2b, jnp.zeros_like(do2b))
            dp = lax.dot_general(dom, v2, _NT, preferred_element_type=F32)
            t = dp - delta
            if has_dlse:
                t = t + jnp.sum(jnp.where(lane == h * HEAD_DIM, dlse_ref[...], 0.0), axis=1, keepdims=True)
            ds = p * t
            if want_dbias:
                @pl.when(first)
                def _():
                    db_ref[0, h] = ds

                @pl.when(jnp.logical_not(first))
                def _():
                    db_ref[0, h] += ds
            dsb = ds.astype(BF16)
            dqs.append(jnp.dot(dsb, k2, preferred_element_type=F32))
            dks.append(lax.dot_general(dsb, q2, _TN, preferred_element_type=F32))
            dvs.append(lax.dot_general(p.astype(BF16), do2b, _TN, preferred_element_type=F32))
        dq_ref[...] = jnp.where(lo, dqs[0], dqs[1]) * SCALE
        dk_ref[pl.ds(ks, nk), :] += jnp.where(lo, dks[0], dks[1]) * SCALE
        dv_ref[pl.ds(ks, nk), :] += jnp.where(lo, dvs[0], dvs[1])

    w = npairs * LANES
    blk = lambda: pl.BlockSpec((Q_BLOCK, LANES), lambda p, i, ks, cl: (i, p))
    full = lambda: pl.BlockSpec((s, LANES), lambda p, i, ks, cl: (0, p))
    tab = lambda: pl.BlockSpec((1, hb, Q_BLOCK, nk), lambda p, i, ks, cl: (cl[i], p if per_head else 0, 0, 0))
    in_specs = [
        pl.BlockSpec((Q_BLOCK, LANES), lambda p, i, ks, cl: (i, qc0 + p)),
        pl.BlockSpec((s, LANES), lambda p, i, ks, cl: (0, kc0 + p)),
        pl.BlockSpec((s, LANES), lambda p, i, ks, cl: (0, vc0 + p)),
        tab(), blk(), blk(), blk(),
    ]
    args = [kstart, cls, qkv, qkv, qkv, table, do, o, lse]
    if has_dlse:
        in_specs.append(blk())
        args.append(dlse)
    out_specs = [blk(), full(), full()]
    out_shape = [jax.ShapeDtypeStruct((s, w), F32)] * 3
    if want_dbias:
        assert per_head
        out_specs.append(tab())
        out_shape.append(jax.ShapeDtypeStruct(table.shape, F32))
    grid_spec = pltpu.PrefetchScalarGridSpec(num_scalar_prefetch=2, grid=(npairs, N_QBLK), in_specs=in_specs,
                                             out_specs=out_specs)
    return _pcall(body, name=name, grid_spec=grid_spec, out_shape=out_shape,
                  compiler_params=pltpu.CompilerParams(dimension_semantics=("arbitrary", "arbitrary")))(*args)


def _combine_fwd(os_, lses, *, name):
    s, w = os_[0].shape

    def body(o0, o1, o2, l0, l1, l2, ob_ref):
        ls = [l0[...], l1[...], l2[...]]
        m = jnp.maximum(jnp.maximum(ls[0], ls[1]), ls[2])
        es = [jnp.exp(t - m) for t in ls]
        den = es[0] + es[1] + es[2]
        num = es[0] * o0[...] + es[1] * o1[...] + es[2] * o2[...]
        ob_ref[...] = (num / den).astype(BF16)

    return _pcall(body, name=name, grid=(s // ROW_TILE,), in_specs=[_row_spec(w)] * 6, out_specs=_row_spec(w),
                  out_shape=jax.ShapeDtypeStruct((s, w), BF16), compiler_params=_PAR)(*os_, *lses)


def _combine_bwd(dob, os_, lses, *, name):
    s, w = dob.shape

    def body(dob_ref, o0, o1, o2, l0, l1, l2, d0, d1, d2, e0, e1, e2):
        lo = _head_lanes()
        o_refs, l_refs, do_refs, dl_refs = (o0, o1, o2), (l0, l1, l2), (d0, d1, d2), (e0, e1, e2)
        for ch in range(w // LANES):
            sl = slice(ch * LANES, (ch + 1) * LANES)
            dv = dob_ref[:, sl]
            ls = [r[:, sl] for r in l_refs]
            m = jnp.maximum(jnp.maximum(ls[0], ls[1]), ls[2])
            es = [jnp.exp(t - m) for t in ls]
            den = es[0] + es[1] + es[2]
            ws = [e / den for e in es]
            ts = [_head_mean(dv * r[:, sl], lo) * float(HEAD_DIM) for r in o_refs]
            tbar = ws[0] * ts[0] + ws[1] * ts[1] + ws[2] * ts[2]
            for g in range(3):
                do_refs[g][:, sl] = ws[g] * dv
                dl_refs[g][:, sl] = ws[g] * (ts[g] - tbar)

    return _pcall(body, name=name, grid=(s // ROW_TILE,), in_specs=[_row_spec(w)] * 7, out_specs=[_row_spec(w)] * 6,
                  out_shape=[jax.ShapeDtypeStruct((s, w), F32)] * 6, compiler_params=_PAR)(dob, *os_, *lses)


_NA_CLASS_R0 = (0, NA_QROWS, ROWS - NA_QROWS)
_NA_CLASS_K0 = (0, 0, ROWS - NA_KROWS)
_RPB_ROWS = 3 * NA_QROWS * NA_KROWS
_RPB_ROWS_PAD = 136
_RPB_RO = 2 * NA_KH - 1
_RPB_CO = 2 * NA_KW - 1


def _na_constants():
    a = np.arange(NA_QROWS)
    b = np.arange(NA_KROWS)
    col = np.arange(GRID_W)
    oh_row = np.zeros((_RPB_ROWS_PAD, 16), np.float32)
    vrow = np.zeros((3, NA_QROWS, NA_KROWS), bool)
    for t in range(3):
        qr = _NA_CLASS_R0[t] + a
        kr = _NA_CLASS_K0[t] + b
        rs = np.clip(qr - NA_KH // 2, 0, ROWS - NA_KH)
        vrow[t] = (kr[None, :] >= rs[:, None]) & (kr[None, :] < rs[:, None] + NA_KH)
        ro = kr[None, :] - qr[:, None] + (NA_KH - 1)
        for ai in range(NA_QROWS):
            for bi in range(NA_KROWS):
                if vrow[t, ai, bi]:
                    oh_row[(t * NA_QROWS + ai) * NA_KROWS + bi, ro[ai, bi]] = 1.0
    cs = np.clip(col - NA_KW // 2, 0, GRID_W - NA_KW)
    vcol = (col[None, :] >= cs[:, None]) & (col[None, :] < cs[:, None] + NA_KW)
    co = col[None, :] - col[:, None] + (NA_KW - 1)
    oh_col = np.zeros((GRID_W * GRID_W, LANES), np.float32)
    for qc in range(GRID_W):
        for kc in range(GRID_W):
            if vcol[qc, kc]:
                oh_col[qc * GRID_W + kc, co[qc, kc]] = 1.0
    valid = vrow[:, :, None, :, None] & vcol[None, None, :, None, :]
    mask = np.where(valid, 0.0, NEG).astype(np.float32).reshape(3, 1, Q_BLOCK, NA_NK)
    ks = np.clip(np.arange(N_QBLK) * NA_QROWS - NA_KH // 2, 0, ROWS - NA_KROWS) * GRID_W
    cls = np.ones(N_QBLK, np.int32)
    cls[0], cls[-1] = 0, 2
    return oh_row, oh_col, mask, ks.astype(np.int32), cls


def _rpb_expand(rpb_pad, oh_row, oh_col_t, *, name):
    def body(r_ref, ohr_ref, ohc_ref, o_ref):
        t = jnp.dot(ohr_ref[...], r_ref[0], preferred_element_type=F32, precision=lax.Precision.HIGHEST)
        o_ref[0] = jnp.dot(t, ohc_ref[...], preferred_element_type=F32, precision=lax.Precision.HIGHEST)

    return _pcall(
        body, name=name, grid=(NA_HEADS,),
        in_specs=[pl.BlockSpec((1, 16, LANES), lambda h: (h, 0, 0)), pl.BlockSpec((_RPB_ROWS_PAD, 16), lambda h: (0, 0)),
                  pl.BlockSpec((LANES, GRID_W * GRID_W), lambda h: (0, 0))],
        out_specs=pl.BlockSpec((1, _RPB_ROWS_PAD, GRID_W * GRID_W), lambda h: (h, 0, 0)),
        out_shape=jax.ShapeDtypeStruct((NA_HEADS, _RPB_ROWS_PAD, GRID_W * GRID_W), F32), compiler_params=_PAR,
    )(rpb_pad, oh_row, oh_col_t)


def _rpb_reduce(dx, oh_row_t, oh_col, *, name):
    def body(d_ref, ohr_ref, ohc_ref, o_ref):
        t = jnp.dot(d_ref[0], ohc_ref[...], preferred_element_type=F32, precision=lax.Precision.HIGHEST)
        o_ref[0] = jnp.dot(ohr_ref[...], t, preferred_element_type=F32, precision=lax.Precision.HIGHEST)

    return _pcall(
        body, name=name, grid=(NA_HEADS,),
        in_specs=[pl.BlockSpec((1, _RPB_ROWS_PAD, GRID_W * GRID_W), lambda h: (h, 0, 0)),
                  pl.BlockSpec((16, _RPB_ROWS_PAD), lambda h: (0, 0)),
                  pl.BlockSpec((GRID_W * GRID_W, LANES), lambda h: (0, 0))],
        out_specs=pl.BlockSpec((1, 16, LANES), lambda h: (h, 0, 0)),
        out_shape=jax.ShapeDtypeStruct((NA_HEADS, 16, LANES), F32), compiler_params=_PAR,
    )(dx, oh_row_t, oh_col)


def _dil_constants(dilation):
    seg = SEQ // dilation
    nb = seg // Q_BLOCK
    shift = (0, -DIL_HALF, -2 * DIL_HALF)
    qi = np.arange(Q_BLOCK)[:, None]
    ki = np.arange(DIL_NK)[None, :]
    mask = np.stack([np.where(np.abs(ki + sh - qi) <= DIL_HALF, 0.0, NEG) for sh in shift]).astype(np.float32)
    ks, cls = [], []
    for i in range(N_QBLK):
        sub, blk = divmod(i, nb)
        t = 0 if blk == 0 else (2 if blk == nb - 1 else 1)
        cls.append(t)
        ks.append(sub * seg + blk * Q_BLOCK + shift[t])
    return mask.reshape(3, 1, Q_BLOCK, DIL_NK), np.asarray(ks, np.int32), np.asarray(cls, np.int32)


_VM = pl.BlockSpec(memory_space=pltpu.VMEM)


def _ada_fwd(c_all, w, b, *, name):
    def body(c_ref, w_ref, b_ref, o_ref):
        cv = c_ref[...]
        o_ref[...] = jnp.dot(cv * _sigmoid(cv), w_ref[...], preferred_element_type=F32,
                             precision=lax.Precision.HIGHEST) + b_ref[...]

    return _pcall(body, name=name, in_specs=[_VM, _VM, _VM], out_specs=_VM,
                  out_shape=jax.ShapeDtypeStruct((c_all.shape[0], w.shape[1]), F32))(c_all, w, b)


def _ada_bwd(c_all_t, dmod, *, name):
    def body(c_ref, d_ref, o_ref):
        cv = c_ref[...]
        o_ref[...] = jnp.dot(cv * _sigmoid(cv), d_ref[...], preferred_element_type=F32,
                             precision=lax.Precision.HIGHEST)

    return _pcall(body, name=name, in_specs=[_VM, _VM], out_specs=_VM,
                  out_shape=jax.ShapeDtypeStruct((c_all_t.shape[0], dmod.shape[1]), F32))(c_all_t, dmod)


def _row_sum(t, *, name):
    def body(t_ref, o_ref):
        o_ref[...] = jnp.sum(t_ref[...], axis=0, keepdims=True)

    return _pcall(body, name=name, in_specs=[_VM], out_specs=_VM,
                  out_shape=jax.ShapeDtypeStruct((1, t.shape[1]), F32))(t)


def _adamw(w, g, m, v, *, name):
    rows, cols = w.shape
    tr = rows
    for cand in range(8, 513, 8):
        if rows % cand == 0:
            tr = cand
    c1 = 1.0 - ADAM_B1 ** ADAM_STEP
    c2 = 1.0 - ADAM_B2 ** ADAM_STEP

    def body(w_ref, g_ref, m_ref, v_ref, d_ref, nm_ref, nv_ref):
        gv = g_ref[...]
        nm = ADAM_B1 * m_ref[...] + (1.0 - ADAM_B1) * gv
        nv = ADAM_B2 * v_ref[...] + (1.0 - ADAM_B2) * (gv * gv)
        m_hat = nm / c1
        v_hat = nv / c2
        d_ref[...] = -ADAM_LR * (m_hat / (jnp.sqrt(v_hat) + ADAM_EPS) + ADAM_WD * w_ref[...])
        nm_ref[...] = nm
        nv_ref[...] = nv

    spec = pl.BlockSpec((tr, cols), lambda i: (i, 0))
    return _pcall(body, name=name, grid=(rows // tr,), in_specs=[spec] * 4, out_specs=[spec] * 3,
                  out_shape=[jax.ShapeDtypeStruct((rows, cols), F32)] * 3, compiler_params=_PAR)(w, g, m, v)


_MESH = pl.DeviceIdType.MESH
_ANY = pl.BlockSpec(memory_space=pl.ANY)
_CHIP_FLIPS = ((1, 0), (0, 1), (1, 1))


def _pos():
    return lax.axis_index("x"), lax.axis_index("y"), lax.axis_index("c")


def _flip(v, f):
    return 1 - v if f else v


def _small_allgather(blk, *, name):
    m_per, n = blk.shape

    def body(x_ref, out_ref, send_sems, recv_sems, local_sem):
        x, y, c = _pos()
        me, sibling = (x, y, c), (x, y, 1 - c)
        chips = [(_flip(x, fx), _flip(y, fy)) for fx, fy in _CHIP_FLIPS]

        def rows(px, py, pc):
            return out_ref.at[pl.ds((4 * px + 2 * py + pc) * m_per, m_per), :]

        def copy(k, block, to, src=None):
            return pltpu.make_async_remote_copy(
                src_ref=rows(*block) if src is None else src, dst_ref=rows(*block),
                send_sem=send_sems.at[k], recv_sem=recv_sems.at[k], device_id=to, device_id_type=_MESH)

        mine = pltpu.make_async_copy(x_ref, rows(*me), local_sem)
        mine.start()
        first = [copy(0, me, sibling, src=x_ref)]
        first += [copy(1 + j, me, (*chip, c), src=x_ref) for j, chip in enumerate(chips)]
        for cp in first:
            cp.start()
        passed = [copy(4 + j, (*chip, c), sibling) for j, chip in enumerate(chips)]
        for j, chip in enumerate(chips):
            copy(1 + j, (*chip, c), me).wait_recv()
            passed[j].start()
        copy(0, sibling, me).wait_recv()
        for j, chip in enumerate(chips):
            copy(4 + j, (*chip, 1 - c), me).wait_recv()
        for cp in first + passed:
            cp.wait_send()
        mine.wait()

    return _pcall(
        body, name=name, out_shape=jax.ShapeDtypeStruct((N_DEV * m_per, n), blk.dtype),
        in_specs=[_VM], out_specs=_VM,
        scratch_shapes=[pltpu.SemaphoreType.DMA((7,)), pltpu.SemaphoreType.DMA((7,)), pltpu.SemaphoreType.DMA],
    )(blk)


def _sem_pairs(n):
    return [pltpu.SemaphoreType.DMA((n,)), pltpu.SemaphoreType.DMA((n,))]


def _allgather_weights(shards, *, name):
    nw = len(shards)

    def body(*refs):
        sh_refs, full_refs = refs[:nw], refs[nw:2 * nw]
        send_sems, recv_sems = refs[2 * nw:]
        x, y, c = _pos()
        j = 2 * x + y
        chips = [(_flip(x, fx), _flip(y, fy)) for fx, fy in _CHIP_FLIPS]

        def copy(k, src, dst, to):
            return pltpu.make_async_remote_copy(src_ref=src, dst_ref=dst, send_sem=send_sems.at[k],
                                                recv_sem=recv_sems.at[k], device_id=to, device_id_type=_MESH)

        first = [copy(6 * wi + k, sh_refs[wi].at[c], full_refs[wi].at[j, c], (px, py, c))
                 for wi in range(nw) for k, (px, py) in enumerate(chips)]
        for cp in first:
            cp.start()
        passed = []
        for wi in range(nw):
            for k, (px, py) in enumerate(chips):
                jp = 2 * px + py
                copy(6 * wi + k, sh_refs[wi].at[c], full_refs[wi].at[jp, c], (x, y, c)).wait_recv()
                cp = copy(6 * wi + 3 + k, full_refs[wi].at[jp, c], full_refs[wi].at[jp, c], (x, y, 1 - c))
                cp.start()
                passed.append(cp)
        for wi in range(nw):
            for k, (px, py) in enumerate(chips):
                jp = 2 * px + py
                copy(6 * wi + 3 + k, sh_refs[wi].at[c], full_refs[wi].at[jp, 1 - c], (x, y, c)).wait_recv()
        for cp in first + passed:
            cp.wait_send()

    return _pcall(
        body, name=name, out_shape=[jax.ShapeDtypeStruct((N_CHIP,) + s.shape, s.dtype) for s in shards],
        in_specs=[_ANY] * nw, out_specs=[_ANY] * nw, scratch_shapes=_sem_pairs(6 * nw),
    )(*shards)


def _sibling_send_halves(gs, *, name):
    nw = len(gs)

    def body(*refs):
        g_refs, out_refs = refs[:nw], refs[nw:2 * nw]
        send_sems, recv_sems = refs[2 * nw:]
        x, y, c = _pos()
        cps = [pltpu.make_async_remote_copy(src_ref=g_refs[wi].at[k, 1 - c], dst_ref=out_refs[wi].at[k],
                                            send_sem=send_sems.at[4 * wi + k], recv_sem=recv_sems.at[4 * wi + k],
                                            device_id=(x, y, 1 - c), device_id_type=_MESH)
               for wi in range(nw) for k in range(N_CHIP)]
        for cp in cps:
            cp.start()
        for cp in cps:
            cp.wait()

    return _pcall(body, name=name,
                  out_shape=[jax.ShapeDtypeStruct((g.shape[0],) + g.shape[2:], g.dtype) for g in gs],
                  in_specs=[_ANY] * nw, out_specs=[_ANY] * nw, scratch_shapes=_sem_pairs(N_CHIP * nw))(*gs)


def _chip_exchange(sbs, *, name):
    nw = len(sbs)

    def body(*refs):
        s_refs, out_refs = refs[:nw], refs[nw:2 * nw]
        send_sems, recv_sems = refs[2 * nw:]
        x, y, c = _pos()
        cps = []
        for wi in range(nw):
            for k, (fx, fy) in enumerate(_CHIP_FLIPS):
                px, py = _flip(x, fx), _flip(y, fy)
                cps.append(pltpu.make_async_remote_copy(
                    src_ref=s_refs[wi].at[2 * px + py], dst_ref=out_refs[wi].at[k], send_sem=send_sems.at[3 * wi + k],
                    recv_sem=recv_sems.at[3 * wi + k], device_id=(px, py, c), device_id_type=_MESH))
        for cp in cps:
            cp.start()
        for cp in cps:
            cp.wait()

    return _pcall(body, name=name, out_shape=[jax.ShapeDtypeStruct((3,) + s.shape[1:], s.dtype) for s in sbs],
                  in_specs=[_ANY] * nw, out_specs=[_ANY] * nw, scratch_shapes=_sem_pairs(3 * nw))(*sbs)


def _sibling_swap(ts, *, name):
    nw = len(ts)

    def body(*refs):
        t_refs, out_refs = refs[:nw], refs[nw:2 * nw]
        send_sems, recv_sems = refs[2 * nw:]
        x, y, c = _pos()
        cps = [pltpu.make_async_remote_copy(src_ref=t_refs[wi], dst_ref=out_refs[wi], send_sem=send_sems.at[wi],
                                            recv_sem=recv_sems.at[wi], device_id=(x, y, 1 - c), device_id_type=_MESH)
               for wi in range(nw)]
        for cp in cps:
            cp.start()
        for cp in cps:
            cp.wait()

    return _pcall(body, name=name, out_shape=[jax.ShapeDtypeStruct(t.shape, t.dtype) for t in ts],
                  in_specs=[_ANY] * nw, out_specs=[_ANY] * nw, scratch_shapes=_sem_pairs(nw))(*ts)


def _rs_add(g, ra, c_arr, *, name):
    n, _, r, w = g.shape

    def body(c_ref, g_ref, ra_ref, s_ref, sb_ref):
        t = g_ref[...] + ra_ref[...]
        s_ref[...] = t
        sb_ref[...] = t.astype(BF16)

    grid_spec = pltpu.PrefetchScalarGridSpec(
        num_scalar_prefetch=1, grid=(n,),
        in_specs=[pl.BlockSpec((None, None, r, w), lambda k, c: (k, c[0], 0, 0)),
                  pl.BlockSpec((None, r, w), lambda k, c: (k, 0, 0))],
        out_specs=[pl.BlockSpec((None, r, w), lambda k, c: (k, 0, 0))] * 2)
    return _pcall(body, name=name, grid_spec=grid_spec,
                  out_shape=[jax.ShapeDtypeStruct((n, r, w), F32), jax.ShapeDtypeStruct((n, r, w), BF16)],
                  compiler_params=_PAR)(c_arr, g, ra)


def _rs_final(s, rb, j_arr, *, name):
    _, r, w = s.shape

    def body(j_ref, s_ref, rb_ref, t_ref):
        t_ref[...] = ((s_ref[...] + rb_ref[0].astype(F32)) + rb_ref[1].astype(F32)) + rb_ref[2].astype(F32)

    grid_spec = pltpu.PrefetchScalarGridSpec(
        num_scalar_prefetch=1, grid=(1,),
        in_specs=[pl.BlockSpec((None, r, w), lambda i, j: (j[0], 0, 0)),
                  pl.BlockSpec((3, r, w), lambda i, j: (0, 0, 0))],
        out_specs=pl.BlockSpec((r, w), lambda i, j: (0, 0)))
    return _pcall(body, name=name, grid_spec=grid_spec, out_shape=jax.ShapeDtypeStruct((r, w), F32),
                  compiler_params=_ARB)(j_arr, s, rb)


def _perm_rows(t, d):
    s, w = t.shape
    return t.reshape(s // d, d, w).transpose(1, 0, 2).reshape(s, w)


def _unperm_rows(t, d):
    s, w = t.shape
    return t.reshape(d, s // d, w).transpose(1, 0, 2).reshape(s, w)


def _tile2(g):
    return jnp.concatenate([g, g], axis=1)


def _device_step(x2, tgt, mod, wts, g_norm1, g_norm2, b_gate, g_qa, g_ka, g_qb, g_kb, rpb):
    d = D_MODEL
    sh1, sc1, gt1, sh2, sc2, gt2 = [mod[:, k * d:(k + 1) * d] for k in range(6)]

    oh_row, oh_col, na_mask, na_ks, na_cls = _na_constants()
    rpb_pad = jnp.pad(rpb, ((0, 0), (0, 16 - _RPB_RO), (0, LANES - _RPB_CO)))
    tab = _rpb_expand(rpb_pad, jnp.asarray(oh_row), jnp.asarray(oh_col.T.copy()), name="rpb_expand")
    tab = tab[:, :_RPB_ROWS].reshape(NA_HEADS, 3, NA_QROWS, NA_KROWS, GRID_W, GRID_W)
    tab_a = tab.transpose(1, 0, 2, 4, 3, 5).reshape(3, NA_HEADS, Q_BLOCK, NA_NK) + jnp.asarray(na_mask)
    na_ks, na_cls = jnp.asarray(na_ks), jnp.asarray(na_cls)
    dil = [_dil_constants(dd) for _, dd in DIL_CONFIGS]
    tab_d = jnp.asarray(dil[0][0])
    gains = jnp.concatenate([_tile2(g_qa), _tile2(g_ka), _tile2(g_qb), _tile2(g_kb)], axis=0)
    cos_t, sa_t, sb_t = _rope_tables()

    h1 = _norm_fwd(x2, g_norm1, sc1, sh1, name="norm1_fwd")
    qkv = _matmul(h1, wts["w_qkv"], name="mm_qkv", tn_cap=1920)
    gates = _matmul(h1, wts["w_gates"], name="mm_gates")
    qkvn = _prep_fwd(qkv, gains, cos_t, sa_t, sb_t, name="prep_fwd")
    o_a, lse_a = _attn_fwd(qkvn, 0, 4, 8, 4, tab_a, na_ks, na_cls, NA_NK, name="attn_a_fwd")
    arrs, o_p, l_p, o_g, l_g = [], [], [], [], []
    for g, (_, dd) in enumerate(DIL_CONFIGS):
        ks_g, cls_g = jnp.asarray(dil[g][1]), jnp.asarray(dil[g][2])
        if dd == 1:
            arr, cb = qkvn, (12, 18, 24)
        else:
            col = lambda base: qkvn[:, base + WB_OUT * g: base + WB_OUT * (g + 1)]
            arr = _perm_rows(jnp.concatenate([col(3 * WA), col(3 * WA + WB), col(3 * WA + 2 * WB)], axis=1), dd)
            cb = (0, 2, 4)
        op, lp = _attn_fwd(arr, cb[0], cb[1], cb[2], 2, tab_d, ks_g, cls_g, DIL_NK, name=f"attn_d{g}_fwd")
        arrs.append((arr, cb, ks_g, cls_g))
        o_p.append(op)
        l_p.append(lp)
        o_g.append(op if dd == 1 else _unperm_rows(op, dd))
        l_g.append(lp if dd == 1 else _unperm_rows(lp, dd))
    o_b = _combine_fwd(o_g, l_g, name="combine_fwd")
    pa = _matmul(o_a, wts["w_pa"], name="mm_pa")
    pb = _matmul(o_b, wts["w_pb"], name="mm_pb")
    merged = _gate_fwd(gates, b_gate, pa, pb, name="gate_fwd")
    mo = _matmul(merged, wts["w_o"], name="mm_o")
    x1, h2 = _norm_fwd(x2, g_norm2, sc2, sh2, name="norm2_fwd", res=mo, gt=gt1)
    ff = _matmul(h2, wts["w_ffn_in"], name="mm_ffn_in")
    act = _swiglu_fwd(ff, name="swiglu_fwd")
    ffo = _matmul(act, wts["w_ffn_out"], name="mm_ffn_out")

    dy, dffo, dgt2, loss_v = _loss_bwd(x1, ffo, tgt, gt2, name="loss_bwd")
    grads = {}
    g_ffn_out = _matmul(act, dffo, name="mmg_ffn_out", ta=True, tm_cap=1408, tn_cap=1024, tk_cap=512)
    grads["w_ffn_out"] = g_ffn_out.reshape(N_CHIP, D_FF // N_CHIP, d)
    dact = _matmul(dffo, wts["w_ffn_out"], name="mmd_ffn_out", tb=True, tn_cap=2816)
    dff = _swiglu_bwd(ff, dact, name="swiglu_bwd")
    grads["w_ffn_in"] = _matmul(h2, dff, name="mmg_ffn_in", ta=True, out_chips=N_CHIP, tm_cap=512, tk_cap=512)
    dh2 = _matmul(dff, wts["w_ffn_in"], name="mmd_ffn_in", tb=True, tm_cap=1024)
    dx1, sums2, dmo = _norm_bwd(x1, dh2, dy, g_norm2, sc2, name="norm2_bwd", mo=mo, gt=gt1)
    g_o = _matmul(merged, dmo, name="mmg_o", ta=True, tm_cap=1024, tn_cap=1024, tk_cap=512)
    grads["w_o"] = g_o.reshape(N_CHIP, d // N_CHIP, d)
    dmerged = _matmul(dmo, wts["w_o"], name="mmd_o", tb=True)
    dpa, dpb, dgates, dbg = _gate_bwd(gates, b_gate, pa, pb, dmerged, name="gate_bwd")
    grads["w_proj_a"] = _matmul(o_a, dpa, name="mmg_pa", ta=True, out_chips=N_CHIP, tm_cap=512, tk_cap=512)
    do_a = _matmul(dpa, wts["w_pa"], name="mmd_pa", tb=True)
    grads["w_proj_b"] = _matmul(o_b, dpb, name="mmg_pb", ta=True, out_chips=N_CHIP, tm_cap=256, tk_cap=512)
    do_b = _matmul(dpb, wts["w_pb"], name="mmd_pb", tb=True)
    cb_out = _combine_bwd(do_b, o_g, l_g, name="combine_bwd")
    do_g, dl_g = cb_out[:3], cb_out[3:]
    dqa, dka, dva, dtab = _attn_bwd(qkvn, 0, 4, 8, 4, tab_a, na_ks, na_cls, NA_NK, do_a, o_a, lse_a,
                                    name="attn_a_bwd", want_dbias=True)
    dqs, dks, dvs = [], [], []
    for g, (_, dd) in enumerate(DIL_CONFIGS):
        arr, cb, ks_g, cls_g = arrs[g]
        dog = do_g[g] if dd == 1 else _perm_rows(do_g[g], dd)
        dlg = dl_g[g] if dd == 1 else _perm_rows(dl_g[g], dd)
        dq, dk, dv = _attn_bwd(arr, cb[0], cb[1], cb[2], 2, tab_d, ks_g, cls_g, DIL_NK, dog, o_p[g], l_p[g],
                               name=f"attn_d{g}_bwd", dlse=dlg)
        if dd != 1:
            dq, dk, dv = _unperm_rows(dq, dd), _unperm_rows(dk, dd), _unperm_rows(dv, dd)
        dqs.append(dq)
        dks.append(dk)
        dvs.append(dv)
    dqkv, dgains = _prep_bwd(qkv, [dqa, dka, dva] + dqs + dks + dvs, gains, cos_t, sa_t, sb_t, name="prep_bwd")
    g_qkv = _matmul(h1, dqkv, name="mmg_qkv", ta=True, tm_cap=512, tn_cap=1920, tk_cap=512)
    g_gates = _matmul(h1, dgates, name="mmg_gates", ta=True, tm_cap=512, tn_cap=2048, tk_cap=512)
    nc, cut = _BIG_SHARD["w_in"][1], 3 * _BIG_SHARD["w_in"][1] - W_QKV
    grads["w_in"] = jnp.stack([g_qkv[:, :nc], g_qkv[:, nc:2 * nc],
                               jnp.concatenate([g_qkv[:, 2 * nc:], g_gates[:, :cut]], axis=1), g_gates[:, cut:]])
    dh1 = _matmul(dqkv, wts["w_qkv"], name="mmd_qkv", tb=True, tk_cap=1920)
    dh1 = _matmul(dgates, wts["w_gates"], name="mmd_gates", tb=True, add=dh1)
    grad_x, sums1 = _norm_bwd(x2, dh1, dx1, g_norm1, sc1, name="norm1_bwd")

    dtab = dtab.reshape(3, NA_HEADS, NA_QROWS, GRID_W, NA_KROWS, GRID_W).transpose(1, 0, 2, 4, 3, 5)
    dtab = jnp.pad(dtab.reshape(NA_HEADS, _RPB_ROWS, GRID_W * GRID_W), ((0, 0), (0, _RPB_ROWS_PAD - _RPB_ROWS), (0, 0)))
    g_rpb = _rpb_reduce(dtab, jnp.asarray(oh_row.T.copy()), jnp.asarray(oh_col), name="rpb_reduce")
    g_rpb = g_rpb[:, :_RPB_RO, :_RPB_CO]

    dmod = jnp.concatenate([sums1[0:1], sums1[1:2], sums2[3:4], sums2[0:1], sums2[1:2], dgt2], axis=1)
    small = dict(g_norm1=sums1[2:3], g_norm2=sums2[2:3], b_gate=dbg, g_qa=dgains[0:1, :HEAD_DIM],
                 g_ka=dgains[1:2, :HEAD_DIM], g_qb=dgains[2:3, :HEAD_DIM], g_kb=dgains[3:4, :HEAD_DIM], rpb=g_rpb)
    return loss_v, grad_x, grads, dmod, small


_BIG = ("w_in", "w_ffn_in", "w_ffn_out", "w_o", "w_proj_a", "w_proj_b")
_BIG_SHARD = {"w_in": (1024, 1472), "w_ffn_in": (1024, 1408), "w_ffn_out": (704, 1024), "w_o": (256, 1024),
              "w_proj_a": (512, 256), "w_proj_b": (256, 256)}


_SMALL = ("b_ada", "g_norm1", "g_norm2", "b_gate", "g_qa", "g_ka", "g_qb", "g_kb", "rpb")
_SMALL_N = {"b_ada": 6 * D_MODEL, "g_norm1": D_MODEL, "g_norm2": D_MODEL, "b_gate": 2 * D_MODEL, "g_qa": HEAD_DIM,
            "g_ka": HEAD_DIM, "g_qb": HEAD_DIM, "g_kb": HEAD_DIM, "rpb": NA_HEADS * _RPB_RO * _RPB_CO}


def _pack_small(parts):
    flat = [parts[n].reshape(1, _SMALL_N[n]) for n in _SMALL]
    used = sum(_SMALL_N.values())
    return jnp.concatenate(flat + [jnp.zeros((1, STATS_W - used), F32)], axis=1)


def _unpack_small(v, shapes):
    out, at = {}, 0
    for n in _SMALL:
        out[n] = v[:, at:at + _SMALL_N[n]].reshape(shapes[n])
        at += _SMALL_N[n]
    return out


def kernel(x, c, w_ada, b_ada, g_norm1, g_norm2, w_in, b_gate, g_qa, g_ka, g_qb, g_kb, rpb, w_proj_a, w_proj_b, w_o, w_ffn_in, w_ffn_out, loss_target, m_w_ada, m_b_ada, m_g_norm1, m_g_norm2, m_w_in, m_b_gate, m_g_qa, m_g_ka, m_g_qb, m_g_kb, m_rpb, m_w_proj_a, m_w_proj_b, m_w_o, m_w_ffn_in, m_w_ffn_out, v_w_ada, v_b_ada, v_g_norm1, v_g_norm2, v_w_in, v_b_gate, v_g_qa, v_g_ka, v_g_qb, v_g_kb, v_rpb, v_w_proj_a, v_w_proj_b, v_w_o, v_w_ffn_in, v_w_ffn_out):
    names = ("w_ada", "b_ada", "g_norm1", "g_norm2", "w_in", "b_gate", "g_qa", "g_ka", "g_qb", "g_kb", "rpb",
             "w_proj_a", "w_proj_b", "w_o", "w_ffn_in", "w_ffn_out")
    w = dict(zip(names, (w_ada, b_ada, g_norm1, g_norm2, w_in, b_gate, g_qa, g_ka, g_qb, g_kb, rpb, w_proj_a, w_proj_b,
                         w_o, w_ffn_in, w_ffn_out)))
    m = dict(zip(names, (m_w_ada, m_b_ada, m_g_norm1, m_g_norm2, m_w_in, m_b_gate, m_g_qa, m_g_ka, m_g_qb, m_g_kb, m_rpb,
                         m_w_proj_a, m_w_proj_b, m_w_o, m_w_ffn_in, m_w_ffn_out)))
    v = dict(zip(names, (v_w_ada, v_b_ada, v_g_norm1, v_g_norm2, v_w_in, v_b_gate, v_g_qa, v_g_ka, v_g_qb, v_g_kb, v_rpb,
                         v_w_proj_a, v_w_proj_b, v_w_o, v_w_ffn_in, v_w_ffn_out)))
    d = D_MODEL
    xi, yi, ci = _pos()
    chip = 2 * xi + yi
    me = 2 * chip + ci
    ada_cols = 6 * d // N_CHIP

    c_all = _small_allgather(jnp.broadcast_to(c, (8, d)), name="ag_c")[::8]
    b_sh = lax.dynamic_slice(b_ada, (0, chip * ada_cols), (1, ada_cols))
    mod_part = _ada_fwd(c_all, w_ada[0], b_sh, name="ada_fwd")
    mod_all = _small_allgather(mod_part, name="ag_mod").reshape(N_CHIP, 2, 8, ada_cols)[:, 0]
    mod = lax.dynamic_index_in_dim(mod_all, me, axis=1, keepdims=False).reshape(1, 6 * d)

    halves = {n: (2, _BIG_SHARD[n][0] // 2, _BIG_SHARD[n][1]) for n in _BIG}
    shards = [w[n][0].astype(BF16).reshape(halves[n]) for n in _BIG]
    gathered = _allgather_weights(shards, name="ag_weights")
    full = {}
    for n, sh, ga in zip(_BIG, shards, gathered):
        ga = lax.dynamic_update_slice(ga, sh[None], (chip, 0, 0, 0))
        full[n] = ga.reshape((N_CHIP,) + _BIG_SHARD[n])
    p_in = full["w_in"]
    nc, cut = _BIG_SHARD["w_in"][1], W_QKV - 2 * _BIG_SHARD["w_in"][1]
    wts = dict(w_qkv=jnp.concatenate([p_in[0], p_in[1], p_in[2][:, :cut]], axis=1),
               w_gates=jnp.concatenate([p_in[2][:, cut:], p_in[3]], axis=1),
               w_pa=full["w_proj_a"], w_pb=full["w_proj_b"], w_o=full["w_o"].reshape(d, d),
               w_ffn_in=full["w_ffn_in"], w_ffn_out=full["w_ffn_out"].reshape(D_FF, d))

    loss_v, grad_x, grads, dmod, small = _device_step(
        x[0], loss_target[0], mod, wts, g_norm1, g_norm2, b_gate, g_qa, g_ka, g_qb, g_kb, rpb[0])

    c_arr, chip_arr = ci.reshape(1).astype(jnp.int32), chip.reshape(1).astype(jnp.int32)
    gps = [grads[n].reshape((N_CHIP,) + halves[n]) for n in _BIG]
    ras = _sibling_send_halves(gps, name="rs_sibling")
    sums = [_rs_add(gp, ra, c_arr, name=f"rs_add_{n}") for n, gp, ra in zip(_BIG, gps, ras)]
    rbs = _chip_exchange([sb for _, sb in sums], name="rs_chips")
    ts = [_rs_final(sf, rb, chip_arr, name=f"rs_final_{n}") for n, (sf, _), rb in zip(_BIG, sums, rbs)]
    others = _sibling_swap(ts, name="rs_pair")
    g_big = {n: jnp.where(ci == 0, jnp.concatenate([t, o], axis=0), jnp.concatenate([o, t], axis=0))
             for n, t, o in zip(_BIG, ts, others)}

    stats = _pack_small(dict(b_ada=dmod, **small))
    rows = _small_allgather(jnp.broadcast_to(stats, (8, STATS_W)), name="ag_stats")[::8]
    dmod_sh = lax.dynamic_slice(rows, (0, chip * ada_cols), (8, ada_cols))
    g = dict(g_big)
    g["w_ada"] = _ada_bwd(c_all.T, dmod_sh, name="ada_bwd")
    tot = _row_sum(rows, name="stats_sum")
    g_small = _unpack_small(tot, {n: w[n].shape for n in _SMALL})

    delta, new_m, new_v = {}, {}, {}
    for n in _BIG + ("w_ada",):
        dl, nm, nv = _adamw(w[n][0], g[n], m[n][0], v[n][0], name=f"adamw_{n}")
        delta[n], new_m[n], new_v[n], g[n] = dl[None], nm[None], nv[None], g[n][None]
    shapes = {n: w[n].shape for n in _SMALL}
    dl, nm, nv = _adamw(_pack_small({n: w[n] for n in _SMALL}), tot, _pack_small({n: m[n] for n in _SMALL}),
                        _pack_small({n: v[n] for n in _SMALL}), name="adamw_small")
    delta.update(_unpack_small(dl, shapes))
    new_m.update(_unpack_small(nm, shapes))
    new_v.update(_unpack_small(nv, shapes))
    g.update(g_small)

    loss = lax.psum(loss_v[0, 0], ("x", "y", "c"))
    return (loss, grad_x[None], *[g[n] for n in names], *[delta[n] for n in names], *[new_m[n] for n in names],
            *[new_v[n] for n in names])
```

```python
import numpy as np

import jax
import jax.numpy as jnp
from jax import lax
from jax.experimental import pallas as pl
from jax.experimental.pallas import tpu as pltpu

F32 = jnp.float32
BF16 = jnp.bfloat16

D_MODEL = 1024
SEQ = 8192
HEAD_DIM = 64
GRID_W = 64
ROWS = SEQ // GRID_W
NA_HEADS = 8
NA_KH = 8
NA_KW = 16
DIL_CONFIGS = ((128, 1), (512, 4), (2048, 16))
ROT_DIM = 16
ROPE_THETA = 500000.0
D_FF = 2816
EPS = 1e-6
NEG = -1e30
WA = 512
WB = 768
WB_OUT = 256
W_QKV = 3 * WA + 3 * WB
W_QK = 2 * WA + 2 * WB
W_GATES = 2 * D_MODEL
SCALE = HEAD_DIM ** -0.5

ADAM_LR = 0.001
ADAM_B1 = 0.9
ADAM_B2 = 0.999
ADAM_EPS = 1e-08
ADAM_WD = 0.01
ADAM_STEP = 10

LANES = 128
ROW_TILE = 256
Q_BLOCK = 256
NA_QROWS = Q_BLOCK // GRID_W
NA_KROWS = NA_QROWS + NA_KH - 1
NA_NK = NA_KROWS * GRID_W
DIL_HALF = 64
DIL_NK = Q_BLOCK + 2 * DIL_HALF
N_QBLK = SEQ // Q_BLOCK

N_DEV = 8
N_CHIP = 4
FF_CHIP = 2 * D_FF // N_CHIP
STATS_W = 14336


def _pcall(body, *, name, **kw):
    return pl.pallas_call(body, name=name, **kw)


_NT = (((1,), (1,)), ((), ()))
_TN = (((0,), (0,)), ((), ()))
_ARB = pltpu.CompilerParams(dimension_semantics=("arbitrary",))
_PAR = pltpu.CompilerParams(dimension_semantics=("parallel",))


def _dot(a, b):
    return jnp.dot(a, b, preferred_element_type=F32)


def _dot_nt(a, b):
    return lax.dot_general(a, b, _NT, preferred_element_type=F32)


def _wgrad(a, b, *, name, tm, tn, tk=512, chips=None):
    s, ma = a.shape
    nb = b.shape[1]
    nk = s // tk
    nc = nb // chips if chips else tn
    cpb = tn // nc

    def body(a_ref, b_ref, o_ref, acc):
        k = pl.program_id(2)
        r = lax.dot_general(a_ref[...].astype(BF16), b_ref[...].astype(BF16), _TN, preferred_element_type=F32)

        @pl.when(k == 0)
        def _():
            acc[...] = r

        @pl.when(k > 0)
        def _():
            acc[...] += r

        @pl.when(k == nk - 1)
        def _():
            if chips:
                for q in range(cpb):
                    o_ref[q] = acc[:, q * nc:(q + 1) * nc]
            else:
                o_ref[...] = acc[...]

    if chips:
        o_spec = pl.BlockSpec((cpb, tm, nc), lambda i, j, k: (j, i, 0))
        out_shape = jax.ShapeDtypeStruct((chips, ma, nc), F32)
    else:
        o_spec = pl.BlockSpec((tm, tn), lambda i, j, k: (i, j))
        out_shape = jax.ShapeDtypeStruct((ma, nb), F32)
    return _pcall(
        body, name=name, grid=(ma // tm, nb // tn, nk),
        in_specs=[pl.BlockSpec((tk, tm), lambda i, j, k: (k, i)), pl.BlockSpec((tk, tn), lambda i, j, k: (k, j))],
        out_specs=o_spec, out_shape=out_shape, scratch_shapes=[pltpu.VMEM((tm, tn), F32)],
        compiler_params=pltpu.CompilerParams(dimension_semantics=("parallel", "parallel", "arbitrary")),
    )(a, b)


def _row_call(body, *, name, row_ins, res_ins, row_outs, acc_outs=(), scratch=()):
    s = row_ins[0].shape[0]
    n = s // ROW_TILE
    nri, nre, nro, nao = len(row_ins), len(res_ins), len(row_outs), len(acc_outs)

    def whole(shape):
        nd = len(shape)
        return pl.BlockSpec(tuple(shape), lambda i: (0,) * nd, pipeline_mode=pl.Buffered(1))

    def whole_out(shape):
        nd = len(shape)
        return pl.BlockSpec(tuple(shape), lambda i: (0,) * nd)

    in_specs = [pl.BlockSpec((ROW_TILE, a.shape[1]), lambda i: (i, 0)) for a in row_ins]
    in_specs += [whole(a.shape) for a in res_ins]
    out_specs = [pl.BlockSpec((ROW_TILE, w), lambda i: (i, 0)) for w, _ in row_outs]
    out_specs += [whole_out(shp) for shp, _ in acc_outs]
    out_shape = [jax.ShapeDtypeStruct((s, w), dt) for w, dt in row_outs]
    out_shape += [jax.ShapeDtypeStruct(tuple(shp), dt) for shp, dt in acc_outs]

    def wrapped(*refs):
        at = [0, nri, nri + nre, nri + nre + nro, nri + nre + nro + nao]
        body(pl.program_id(0), n, refs[at[0]:at[1]], refs[at[1]:at[2]], refs[at[2]:at[3]], refs[at[3]:at[4]],
             refs[at[4]:])

    return _pcall(wrapped, name=name, grid=(n,), in_specs=in_specs, out_specs=out_specs, out_shape=out_shape,
                  scratch_shapes=list(scratch), compiler_params=_ARB)(*row_ins, *res_ins)


def _fold8(t):
    r, w = t.shape
    return jnp.sum(t.reshape(r // 8, 8, w), axis=0)


def _sigmoid(t):
    return 1.0 / (1.0 + jnp.exp(-t))


def _head_lanes():
    return lax.broadcasted_iota(jnp.int32, (1, LANES), 1) < HEAD_DIM


def _head_mean(t, lo):
    s_lo = jnp.sum(jnp.where(lo, t, 0.0), axis=1, keepdims=True)
    s_hi = jnp.sum(jnp.where(lo, 0.0, t), axis=1, keepdims=True)
    return jnp.where(lo, s_lo, s_hi) * (1.0 / HEAD_DIM)


def _rms_mod(xv, g, sc, sh):
    rstd = lax.rsqrt(jnp.mean(xv * xv, axis=1, keepdims=True) + EPS)
    return (xv * rstd * g) * (1.0 + sc) + sh


def _rms_mod_bwd(xv, dh, g, sc):
    rstd = lax.rsqrt(jnp.mean(xv * xv, axis=1, keepdims=True) + EPS)
    xhat = xv * rstd
    dn = dh * (1.0 + sc)
    dxhat = dn * g
    dx = rstd * (dxhat - xhat * jnp.mean(dxhat * xhat, axis=1, keepdims=True))
    return dx, dh, dh * (xhat * g), dn * xhat


def _mix_weights(ls):
    m = jnp.maximum(jnp.maximum(ls[0], ls[1]), ls[2])
    es = [jnp.exp(t - m) for t in ls]
    den = es[0] + es[1] + es[2]
    return [e / den for e in es]


def _rope_tables():
    half = ROT_DIM // 2
    inv_freq = ROPE_THETA ** (-(jnp.arange(half, dtype=F32) * 2.0) / ROT_DIM)
    lane = np.arange(LANES) % HEAD_DIM
    ang = jnp.arange(SEQ).astype(F32)[:, None] * jnp.tile(inv_freq, LANES // half)[None, :]
    cos, sin = jnp.cos(ang), jnp.sin(ang)
    first, second = jnp.asarray(lane < half)[None, :], jnp.asarray((lane >= half) & (lane < ROT_DIM))[None, :]
    cos_t = jnp.where(first | second, cos, 1.0)
    return cos_t, jnp.where(second, sin, 0.0), jnp.where(first, -sin, 0.0)


_SECTIONS = ((0, WA, 0, False), (WA, 2 * WA, 1, False), (2 * WA, 3 * WA, -1, False),
             (3 * WA, 3 * WA + WB, 2, True), (3 * WA + WB, 3 * WA + 2 * WB, 3, True), (3 * WA + 2 * WB, W_QKV, -1, False))


def _pre_attn_fwd(x, cos_t, sa_t, sb_t, g1, sc1, sh1, w_qkv, w_gates, gains, *, name):
    half = ROT_DIM // 2

    def body(i, n, rin, res, rout, aout, scr):
        x_ref, cos_ref, sa_ref, sb_ref = rin
        g_ref, sc_ref, sh_ref, wq_ref, wg_ref, gains_ref = res
        h1_ref, qkvn_ref, pre_ref, gates_ref = rout
        hb = _rms_mod(x_ref[...], g_ref[...], sc_ref[...], sh_ref[...]).astype(BF16)
        h1_ref[...] = hb
        gates_ref[...] = _dot(hb, wg_ref[...]).astype(BF16)
        lo = _head_lanes()
        cosv, sav, sbv = cos_ref[...], sa_ref[...], sb_ref[...]
        pre_at = 0
        for c0, c1, kind, rot in _SECTIONS:
            sec = _dot(hb, wq_ref[:, c0:c1])
            for ch in range((c1 - c0) // LANES):
                t = sec[:, ch * LANES:(ch + 1) * LANES]
                if kind >= 0:
                    pre_ref[:, pre_at:pre_at + LANES] = t.astype(BF16)
                    pre_at += LANES
                    t = t * lax.rsqrt(_head_mean(t * t, lo) + EPS) * gains_ref[kind:kind + 1, :]
                    if rot:
                        t = t * cosv + pltpu.roll(t, half, 1) * sav + pltpu.roll(t, LANES - half, 1) * sbv
                qkvn_ref[:, c0 + ch * LANES:c0 + (ch + 1) * LANES] = t.astype(BF16)

    return _row_call(body, name=name, row_ins=[x, cos_t, sa_t, sb_t], res_ins=[g1, sc1, sh1, w_qkv, w_gates, gains],
                     row_outs=[(D_MODEL, BF16), (W_QKV, BF16), (W_QK, BF16), (W_GATES, BF16)])


def _pre_attn_bwd(qk_pre, d_parts, dgates, x, dx1, cos_t, sa_t, sb_t, w_qkv, w_gates, gains, g1, sc1, *, name):
    half = ROT_DIM // 2
    nparts = len(d_parts)
    where = []
    for pi, part in enumerate(d_parts):
        where += [(pi, cj) for cj in range(part.shape[1] // LANES)]
    assert len(where) == W_QKV // LANES

    def body(i, n, rin, res, rout, aout, scr):
        pre_ref, d_refs = rin[0], rin[1:1 + nparts]
        dgates_ref, x_ref, dx1_ref, cos_ref, sa_ref, sb_ref = rin[1 + nparts:]
        wq_ref, wg_ref, gains_ref, g_ref, sc_ref = res
        dqkv_ref, gx_ref = rout
        dgains_ref, sums_ref = aout
        accg, accs = scr

        @pl.when(i == 0)
        def _():
            accg[...] = jnp.zeros_like(accg)
            accs[...] = jnp.zeros_like(accs)

        lo = _head_lanes()
        cosv, sav, sbv = cos_ref[...], sa_ref[...], sb_ref[...]
        dh = _dot_nt(dgates_ref[...], wg_ref[...])
        pre_at = 0
        for c0, c1, kind, rot in _SECTIONS:
            for ch in range((c1 - c0) // LANES):
                pi, cj = where[c0 // LANES + ch]
                dt = d_refs[pi][:, cj * LANES:(cj + 1) * LANES]
                if kind >= 0:
                    if rot:
                        dt = dt * cosv + pltpu.roll(dt * sav, LANES - half, 1) + pltpu.roll(dt * sbv, half, 1)
                    t = pre_ref[:, pre_at:pre_at + LANES].astype(F32)
                    pre_at += LANES
                    rstd = lax.rsqrt(_head_mean(t * t, lo) + EPS)
                    xhat = t * rstd
                    accg[kind] += _fold8(dt * xhat)
                    dxhat = dt * gains_ref[kind:kind + 1, :]
                    dt = rstd * (dxhat - xhat * _head_mean(dxhat * xhat, lo))
                dqkv_ref[:, c0 + ch * LANES:c0 + (ch + 1) * LANES] = dt.astype(BF16)
            dh = dh + _dot_nt(dqkv_ref[:, c0:c1], wq_ref[:, c0:c1])
        dx, t_sh, t_sc, t_g = _rms_mod_bwd(x_ref[...], dh, g_ref[...], sc_ref[...])
        gx_ref[...] = dx1_ref[...] + dx
        accs[0] += _fold8(t_sh)
        accs[1] += _fold8(t_sc)
        accs[2] += _fold8(t_g)

        @pl.when(i == n - 1)
        def _():
            t = jnp.sum(accg[...], axis=1)
            dgains_ref[...] = t + pltpu.roll(t, HEAD_DIM, 1)
            sums_ref[...] = jnp.sum(accs[...], axis=1)

    return _row_call(
        body, name=name, row_ins=[qk_pre, *d_parts, dgates, x, dx1, cos_t, sa_t, sb_t],
        res_ins=[w_qkv, w_gates, gains, g1, sc1], row_outs=[(W_QKV, BF16), (D_MODEL, F32)],
        acc_outs=[((4, LANES), F32), ((3, D_MODEL), F32)],
        scratch=[pltpu.VMEM((4, 8, LANES), F32), pltpu.VMEM((3, 8, D_MODEL), F32)])


def _post_attn_fwd(o_a, o_g, l_g, gates, x, w_pa, w_pb, w_o, b_gate, gt1, g2, sc2, sh2, *, name):
    d = D_MODEL

    def body(i, n, rin, res, rout, aout, scr):
        oa_ref, o0, o1, o2, l0, l1, l2, gates_ref, x_ref = rin
        wpa_ref, wpb_ref, wo_ref, b_ref, gt_ref, g_ref, sc_ref, sh_ref = res
        ob_ref, merged_ref, mo_ref, x1_ref, h2_ref = rout
        ws = _mix_weights([l0[...], l1[...], l2[...]])
        obb = (ws[0] * o0[...] + ws[1] * o1[...] + ws[2] * o2[...]).astype(BF16)
        ob_ref[...] = obb
        pa = _dot(oa_ref[...].astype(BF16), wpa_ref[...])
        pb = _dot(obb, wpb_ref[...])
        ga = _sigmoid(gates_ref[:, :d].astype(F32) + b_ref[:, :d])
        gb = _sigmoid(gates_ref[:, d:].astype(F32) + b_ref[:, d:])
        merged = (ga * pa + gb * pb).astype(BF16)
        merged_ref[...] = merged
        mo = _dot(merged, wo_ref[...])
        mo_ref[...] = mo.astype(BF16)
        x1 = x_ref[...] + gt_ref[...] * mo
        x1_ref[...] = x1
        h2_ref[...] = _rms_mod(x1, g_ref[...], sc_ref[...], sh_ref[...]).astype(BF16)

    return _row_call(body, name=name, row_ins=[o_a, *o_g, *l_g, gates, x],
                     res_ins=[w_pa, w_pb, w_o, b_gate, gt1, g2, sc2, sh2],
                     row_outs=[(WB_OUT, BF16), (d, BF16), (d, BF16), (d, F32), (d, BF16)])


def _ffn_fwd(h2, w_ffn_in, *, name):
    def body(i, n, rin, res, rout, aout, scr):
        (h_ref,), (w_ref,), (act_ref, ff_ref) = rin, res, rout
        hv = h_ref[...]
        for q in range(2):
            a = _dot(hv, w_ref[q])
            up = _dot(hv, w_ref[q + 2])
            sl = slice(q * FF_CHIP, (q + 1) * FF_CHIP)
            act_ref[:, sl] = (a * _sigmoid(a) * up).astype(BF16)
            ff_ref[:, sl] = a.astype(BF16)
            ff_ref[:, D_FF + q * FF_CHIP:D_FF + (q + 1) * FF_CHIP] = up.astype(BF16)

    return _row_call(body, name=name, row_ins=[h2], res_ins=[w_ffn_in], row_outs=[(D_FF, BF16), (2 * D_FF, BF16)])


def _ffn_mid(act, ff, x1, tgt, w_ffn_out, gt2, *, name):
    d = D_MODEL

    def body(i, n, rin, res, rout, aout, scr):
        act_ref, ff_ref, x1_ref, tgt_ref = rin
        wo_ref, gt_ref = res
        dy_ref, dffo_ref, dff_ref = rout
        dgt_ref, loss_ref = aout
        (acc,) = scr

        @pl.when(i == 0)
        def _():
            acc[...] = jnp.zeros_like(acc)

        ffo = _dot(act_ref[...], wo_ref[...])
        gtv = gt_ref[...]
        e = x1_ref[...] + gtv * ffo - tgt_ref[...]
        dy = e * (1.0 / d)
        dy_ref[...] = dy
        dffo = (gtv * dy).astype(BF16)
        dffo_ref[...] = dffo
        acc[0] += _fold8(dy * ffo)
        acc[1] += _fold8(e * e)
        for q in range(2):
            sl = slice(q * FF_CHIP, (q + 1) * FF_CHIP)
            su = slice(D_FF + q * FF_CHIP, D_FF + (q + 1) * FF_CHIP)
            dact = _dot_nt(dffo, wo_ref[sl, :])
            a = ff_ref[:, sl].astype(F32)
            up = ff_ref[:, su].astype(F32)
            sg = _sigmoid(a)
            dff_ref[:, sl] = (dact * up * (sg * (1.0 + a * (1.0 - sg)))).astype(BF16)
            dff_ref[:, su] = (dact * (a * sg)).astype(BF16)

        @pl.when(i == n - 1)
        def _():
            dgt_ref[...] = jnp.sum(acc[0], axis=0, keepdims=True)
            tot = jnp.sum(jnp.sum(acc[1], axis=0, keepdims=True), axis=1, keepdims=True)
            loss_ref[...] = jnp.broadcast_to(tot * (0.5 / d), (1, LANES))

    return _row_call(body, name=name, row_ins=[act, ff, x1, tgt], res_ins=[w_ffn_out, gt2],
                     row_outs=[(d, F32), (d, BF16), (2 * D_FF, BF16)], acc_outs=[((1, d), F32), ((1, LANES), F32)],
                     scratch=[pltpu.VMEM((2, 8, d), F32)])


def _ffn_in_bwd(dff, x1, dy, mo, w_ffn_in, g2, sc2, gt1, *, name):
    d = D_MODEL

    def body(i, n, rin, res, rout, aout, scr):
        dff_ref, x1_ref, dy_ref, mo_ref = rin
        w_ref, g_ref, sc_ref, gt_ref = res
        dx1_ref, dmo_ref = rout
        (sums_ref,) = aout
        (acc,) = scr

        @pl.when(i == 0)
        def _():
            acc[...] = jnp.zeros_like(acc)

        dh = _dot_nt(dff_ref[:, :FF_CHIP], w_ref[0])
        for q in range(1, N_CHIP):
            dh = dh + _dot_nt(dff_ref[:, q * FF_CHIP:(q + 1) * FF_CHIP], w_ref[q])
        dx, t_sh, t_sc, t_g = _rms_mod_bwd(x1_ref[...], dh, g_ref[...], sc_ref[...])
        dx1 = dy_ref[...] + dx
        dx1_ref[...] = dx1
        dmo_ref[...] = (gt_ref[...] * dx1).astype(BF16)
        acc[0] += _fold8(t_sh)
        acc[1] += _fold8(t_sc)
        acc[2] += _fold8(t_g)
        acc[3] += _fold8(dx1 * mo_ref[...].astype(F32))

        @pl.when(i == n - 1)
        def _():
            sums_ref[...] = jnp.sum(acc[...], axis=1)

    return _row_call(body, name=name, row_ins=[dff, x1, dy, mo], res_ins=[w_ffn_in, g2, sc2, gt1],
                     row_outs=[(d, F32), (d, BF16)], acc_outs=[((4, d), F32)], scratch=[pltpu.VMEM((4, 8, d), F32)])


def _post_attn_bwd(dmo, gates, o_a, o_g, l_g, w_pa, w_pb, w_o, b_gate, *, name):
    d = D_MODEL

    def body(i, n, rin, res, rout, aout, scr):
        dmo_ref, gates_ref, oa_ref, o0, o1, o2, l0, l1, l2 = rin
        wpa_ref, wpb_ref, wo_ref, b_ref = res
        dpa_ref, dpb_ref, dgates_ref, doa_ref = rout[:4]
        do_refs, dl_refs = rout[4:7], rout[7:10]
        (dbg_ref,) = aout
        (acc,) = scr

        @pl.when(i == 0)
        def _():
            acc[...] = jnp.zeros_like(acc)

        ogs = [o0[...], o1[...], o2[...]]
        ws = _mix_weights([l0[...], l1[...], l2[...]])
        obb = (ws[0] * ogs[0] + ws[1] * ogs[1] + ws[2] * ogs[2]).astype(BF16)
        pa = _dot(oa_ref[...].astype(BF16), wpa_ref[...])
        pb = _dot(obb, wpb_ref[...])
        ga = _sigmoid(gates_ref[:, :d].astype(F32) + b_ref[:, :d])
        gb = _sigmoid(gates_ref[:, d:].astype(F32) + b_ref[:, d:])
        dm = _dot_nt(dmo_ref[...], wo_ref[...])
        dpa = (dm * ga).astype(BF16)
        dpb = (dm * gb).astype(BF16)
        dpa_ref[...] = dpa
        dpb_ref[...] = dpb
        dga = dm * pa * ga * (1.0 - ga)
        dgb = dm * pb * gb * (1.0 - gb)
        dgates_ref[:, :d] = dga.astype(BF16)
        dgates_ref[:, d:] = dgb.astype(BF16)
        acc[:, :d] += _fold8(dga)
        acc[:, d:] += _fold8(dgb)
        doa_ref[...] = _dot_nt(dpa, wpa_ref[...])
        dob = _dot_nt(dpb, wpb_ref[...])
        lo = _head_lanes()
        for ch in range(WB_OUT // LANES):
            sl = slice(ch * LANES, (ch + 1) * LANES)
            dv = dob[:, sl]
            wc = [w[:, sl] for w in ws]
            ts = [_head_mean(dv * og[:, sl], lo) * float(HEAD_DIM) for og in ogs]
            tbar = wc[0] * ts[0] + wc[1] * ts[1] + wc[2] * ts[2]
            for g in range(3):
                do_refs[g][:, sl] = wc[g] * dv
                dl_refs[g][:, sl] = wc[g] * (ts[g] - tbar)

        @pl.when(i == n - 1)
        def _():
            dbg_ref[...] = jnp.sum(acc[...], axis=0, keepdims=True)

    return _row_call(body, name=name, row_ins=[dmo, gates, o_a, *o_g, *l_g], res_ins=[w_pa, w_pb, w_o, b_gate],
                     row_outs=[(d, BF16), (d, BF16), (2 * d, BF16), (WA, F32)] + [(WB_OUT, F32)] * 6,
                     acc_outs=[((1, 2 * d), F32)], scratch=[pltpu.VMEM((8, 2 * d), F32)])


def _attn_fwd(qkv, qc0, kc0, vc0, npairs, table, kstart, cls, nk, *, name):
    s = qkv.shape[0]
    per_head = table.shape[1] > 1
    hb = 2 if per_head else 1

    def body(ks_ref, cls_ref, q_ref, k_ref, v_ref, b_ref, o_ref, lse_ref):
        i = pl.program_id(1)
        ks = pl.multiple_of(ks_ref[i], 64)
        q2 = q_ref[...]
        k2 = k_ref[pl.ds(ks, nk), :]
        v2 = v_ref[pl.ds(ks, nk), :]
        lo = _head_lanes()
        outs, lses = [], []
        for h in range(2):
            qm = jnp.where(lo if h == 0 else jnp.logical_not(lo), q2, jnp.zeros_like(q2))
            sc = _dot_nt(qm, k2) * SCALE + b_ref[0, h if per_head else 0]
            m = jnp.max(sc, axis=1, keepdims=True)
            p = jnp.exp(sc - m)
            l = jnp.sum(p, axis=1, keepdims=True)
            pv = _dot(p.astype(BF16), v2)
            outs.append(pv / l)
            lses.append(m + jnp.log(l))
        o_ref[...] = jnp.where(lo, outs[0], outs[1])
        lse_ref[...] = jnp.where(lo, lses[0], lses[1])

    w = npairs * LANES
    grid_spec = pltpu.PrefetchScalarGridSpec(
        num_scalar_prefetch=2, grid=(npairs, N_QBLK),
        in_specs=[
            pl.BlockSpec((Q_BLOCK, LANES), lambda p, i, ks, cl: (i, qc0 + p)),
            pl.BlockSpec((s, LANES), lambda p, i, ks, cl: (0, kc0 + p)),
            pl.BlockSpec((s, LANES), lambda p, i, ks, cl: (0, vc0 + p)),
            pl.BlockSpec((1, hb, Q_BLOCK, nk), lambda p, i, ks, cl: (cl[i], p if per_head else 0, 0, 0)),
        ],
        out_specs=[pl.BlockSpec((Q_BLOCK, LANES), lambda p, i, ks, cl: (i, p)),
                   pl.BlockSpec((Q_BLOCK, LANES), lambda p, i, ks, cl: (i, p))],
    )
    return _pcall(body, name=name, grid_spec=grid_spec,
                  out_shape=[jax.ShapeDtypeStruct((s, w), F32), jax.ShapeDtypeStruct((s, w), F32)],
                  compiler_params=pltpu.CompilerParams(dimension_semantics=("parallel", "arbitrary")),
                  )(kstart, cls, qkv, qkv, qkv, table)


def _attn_bwd(qkv, qc0, kc0, vc0, npairs, table, kstart, cls, nk, do, o, lse, *, name, dlse=None, want_dbias=False):
    s = qkv.shape[0]
    per_head = table.shape[1] > 1
    hb = 2 if per_head else 1
    has_dlse = dlse is not None

    def body(ks_ref, cls_ref, q_ref, k_ref, v_ref, b_ref, do_ref, o_ref, lse_ref, *rest):
        if has_dlse:
            dlse_ref, rest = rest[0], rest[1:]
        dq_ref, dk_ref, dv_ref = rest[0], rest[1], rest[2]
        db_ref = rest[3] if want_dbias else None
        i = pl.program_id(1)

        @pl.when(i == 0)
        def _():
            dk_ref[...] = jnp.zeros_like(dk_ref)
            dv_ref[...] = jnp.zeros_like(dv_ref)

        ks = pl.multiple_of(ks_ref[i], 64)
        q2 = q_ref[...]
        k2 = k_ref[pl.ds(ks, nk), :]
        v2 = v_ref[pl.ds(ks, nk), :]
        do2 = do_ref[...]
        lse2 = lse_ref[...]
        doo = do2 * o_ref[...]
        do2b = do2.astype(BF16)
        lo = _head_lanes()
        lane = lax.broadcasted_iota(jnp.int32, (1, LANES), 1)
        if want_dbias:
            first = jnp.logical_or(i == 0, cls_ref[i] != cls_ref[jnp.maximum(i - 1, 0)])
        dqs, dks, dvs = [], [], []
        for h in range(2):
            mh = lo if h == 0 else jnp.logical_not(lo)
            qm = jnp.where(mh, q2, jnp.zeros_like(q2))
            sc = _dot_nt(qm, k2) * SCALE + b_ref[0, h if per_head else 0]
            lse_h = jnp.max(jnp.where(mh, lse2, NEG), axis=1, keepdims=True)
            p = jnp.exp(sc - lse_h)
            delta = jnp.sum(jnp.where(mh, doo, 0.0), axis=1, keepdims=True)
            dom = jnp.where(mh, do2b, jnp.zeros_like(do2b))
            dp = _dot_nt(dom, v2)
            t = dp - delta
            if has_dlse:
                t = t + jnp.sum(jnp.where(lane == h * HEAD_DIM, dlse_ref[...], 0.0), axis=1, keepdims=True)
            ds = p * t
            if want_dbias:
                @pl.when(first)
                def _():
                    db_ref[0, h] = ds

                @pl.when(jnp.logical_not(first))
                def _():
                    db_ref[0, h] += ds
            dsb = ds.astype(BF16)
            dqs.append(_dot(dsb, k2))
            dks.append(lax.dot_general(dsb, q2, _TN, preferred_element_type=F32))
            dvs.append(lax.dot_general(p.astype(BF16), do2b, _TN, preferred_element_type=F32))
        dq_ref[...] = jnp.where(lo, dqs[0], dqs[1]) * SCALE
        dk_ref[pl.ds(ks, nk), :] += jnp.where(lo, dks[0], dks[1]) * SCALE
        dv_ref[pl.ds(ks, nk), :] += jnp.where(lo, dvs[0], dvs[1])

    w = npairs * LANES
    blk = lambda: pl.BlockSpec((Q_BLOCK, LANES), lambda p, i, ks, cl: (i, p))
    full = lambda: pl.BlockSpec((s, LANES), lambda p, i, ks, cl: (0, p))
    tab = lambda: pl.BlockSpec((1, hb, Q_BLOCK, nk), lambda p, i, ks, cl: (cl[i], p if per_head else 0, 0, 0))
    in_specs = [
        pl.BlockSpec((Q_BLOCK, LANES), lambda p, i, ks, cl: (i, qc0 + p)),
        pl.BlockSpec((s, LANES), lambda p, i, ks, cl: (0, kc0 + p)),
        pl.BlockSpec((s, LANES), lambda p, i, ks, cl: (0, vc0 + p)),
        tab(), blk(), blk(), blk(),
    ]
    args = [kstart, cls, qkv, qkv, qkv, table, do, o, lse]
    if has_dlse:
        in_specs.append(blk())
        args.append(dlse)
    out_specs = [blk(), full(), full()]
    out_shape = [jax.ShapeDtypeStruct((s, w), F32)] * 3
    if want_dbias:
        assert per_head
        out_specs.append(tab())
        out_shape.append(jax.ShapeDtypeStruct(table.shape, F32))
    grid_spec = pltpu.PrefetchScalarGridSpec(num_scalar_prefetch=2, grid=(npairs, N_QBLK), in_specs=in_specs,
                                             out_specs=out_specs)
    return _pcall(body, name=name, grid_spec=grid_spec, out_shape=out_shape,
                  compiler_params=pltpu.CompilerParams(dimension_semantics=("arbitrary", "arbitrary")))(*args)


_NA_CLASS_R0 = (0, NA_QROWS, ROWS - NA_QROWS)
_NA_CLASS_K0 = (0, 0, ROWS - NA_KROWS)
_RPB_ROWS = 3 * NA_QROWS * NA_KROWS
_RPB_ROWS_PAD = 136
_RPB_RO = 2 * NA_KH - 1
_RPB_CO = 2 * NA_KW - 1


def _na_constants():
    a = np.arange(NA_QROWS)
    b = np.arange(NA_KROWS)
    col = np.arange(GRID_W)
    oh_row = np.zeros((_RPB_ROWS_PAD, 16), np.float32)
    vrow = np.zeros((3, NA_QROWS, NA_KROWS), bool)
    for t in range(3):
        qr = _NA_CLASS_R0[t] + a
        kr = _NA_CLASS_K0[t] + b
        rs = np.clip(qr - NA_KH // 2, 0, ROWS - NA_KH)
        vrow[t] = (kr[None, :] >= rs[:, None]) & (kr[None, :] < rs[:, None] + NA_KH)
        ro = kr[None, :] - qr[:, None] + (NA_KH - 1)
        for ai in range(NA_QROWS):
            for bi in range(NA_KROWS):
                if vrow[t, ai, bi]:
                    oh_row[(t * NA_QROWS + ai) * NA_KROWS + bi, ro[ai, bi]] = 1.0
    cs = np.clip(col - NA_KW // 2, 0, GRID_W - NA_KW)
    vcol = (col[None, :] >= cs[:, None]) & (col[None, :] < cs[:, None] + NA_KW)
    co = col[None, :] - col[:, None] + (NA_KW - 1)
    oh_col = np.zeros((GRID_W * GRID_W, LANES), np.float32)
    for qc in range(GRID_W):
        for kc in range(GRID_W):
            if vcol[qc, kc]:
                oh_col[qc * GRID_W + kc, co[qc, kc]] = 1.0
    valid = vrow[:, :, None, :, None] & vcol[None, None, :, None, :]
    mask = np.where(valid, 0.0, NEG).astype(np.float32).reshape(3, 1, Q_BLOCK, NA_NK)
    ks = np.clip(np.arange(N_QBLK) * NA_QROWS - NA_KH // 2, 0, ROWS - NA_KROWS) * GRID_W
    cls = np.ones(N_QBLK, np.int32)
    cls[0], cls[-1] = 0, 2
    return oh_row, oh_col, mask, ks.astype(np.int32), cls


def _rpb_expand(rpb_pad, oh_row, oh_col_t, *, name):
    def body(r_ref, ohr_ref, ohc_ref, o_ref):
        t = jnp.dot(ohr_ref[...], r_ref[0], preferred_element_type=F32, precision=lax.Precision.HIGHEST)
        o_ref[0] = jnp.dot(t, ohc_ref[...], preferred_element_type=F32, precision=lax.Precision.HIGHEST)

    return _pcall(
        body, name=name, grid=(NA_HEADS,),
        in_specs=[pl.BlockSpec((1, 16, LANES), lambda h: (h, 0, 0)), pl.BlockSpec((_RPB_ROWS_PAD, 16), lambda h: (0, 0)),
                  pl.BlockSpec((LANES, GRID_W * GRID_W), lambda h: (0, 0))],
        out_specs=pl.BlockSpec((1, _RPB_ROWS_PAD, GRID_W * GRID_W), lambda h: (h, 0, 0)),
        out_shape=jax.ShapeDtypeStruct((NA_HEADS, _RPB_ROWS_PAD, GRID_W * GRID_W), F32), compiler_params=_PAR,
    )(rpb_pad, oh_row, oh_col_t)


def _rpb_reduce(dx, oh_row_t, oh_col, *, name):
    def body(d_ref, ohr_ref, ohc_ref, o_ref):
        t = jnp.dot(d_ref[0], ohc_ref[...], preferred_element_type=F32, precision=lax.Precision.HIGHEST)
        o_ref[0] = jnp.dot(ohr_ref[...], t, preferred_element_type=F32, precision=lax.Precision.HIGHEST)

    return _pcall(
        body, name=name, grid=(NA_HEADS,),
        in_specs=[pl.BlockSpec((1, _RPB_ROWS_PAD, GRID_W * GRID_W), lambda h: (h, 0, 0)),
                  pl.BlockSpec((16, _RPB_ROWS_PAD), lambda h: (0, 0)),
                  pl.BlockSpec((GRID_W * GRID_W, LANES), lambda h: (0, 0))],
        out_specs=pl.BlockSpec((1, 16, LANES), lambda h: (h, 0, 0)),
        out_shape=jax.ShapeDtypeStruct((NA_HEADS, 16, LANES), F32), compiler_params=_PAR,
    )(dx, oh_row_t, oh_col)


def _dil_constants(dilation):
    seg = SEQ // dilation
    nb = seg // Q_BLOCK
    shift = (0, -DIL_HALF, -2 * DIL_HALF)
    qi = np.arange(Q_BLOCK)[:, None]
    ki = np.arange(DIL_NK)[None, :]
    mask = np.stack([np.where(np.abs(ki + sh - qi) <= DIL_HALF, 0.0, NEG) for sh in shift]).astype(np.float32)
    ks, cls = [], []
    for i in range(N_QBLK):
        sub, blk = divmod(i, nb)
        t = 0 if blk == 0 else (2 if blk == nb - 1 else 1)
        cls.append(t)
        ks.append(sub * seg + blk * Q_BLOCK + shift[t])
    return mask.reshape(3, 1, Q_BLOCK, DIL_NK), np.asarray(ks, np.int32), np.asarray(cls, np.int32)


_VM = pl.BlockSpec(memory_space=pltpu.VMEM)


def _ada_fwd(c_all, w, b, *, name):
    def body(c_ref, w_ref, b_ref, o_ref):
        cv = c_ref[...]
        o_ref[...] = jnp.dot(cv * _sigmoid(cv), w_ref[...], preferred_element_type=F32,
                             precision=lax.Precision.HIGHEST) + b_ref[...]

    return _pcall(body, name=name, in_specs=[_VM, _VM, _VM], out_specs=_VM,
                  out_shape=jax.ShapeDtypeStruct((c_all.shape[0], w.shape[1]), F32))(c_all, w, b)


def _ada_bwd(c_all_t, dmod, *, name):
    def body(c_ref, d_ref, o_ref):
        cv = c_ref[...]
        o_ref[...] = jnp.dot(cv * _sigmoid(cv), d_ref[...], preferred_element_type=F32,
                             precision=lax.Precision.HIGHEST)

    return _pcall(body, name=name, in_specs=[_VM, _VM], out_specs=_VM,
                  out_shape=jax.ShapeDtypeStruct((c_all_t.shape[0], dmod.shape[1]), F32))(c_all_t, dmod)


def _row_sum(t, *, name):
    def body(t_ref, o_ref):
        o_ref[...] = jnp.sum(t_ref[...], axis=0, keepdims=True)

    return _pcall(body, name=name, in_specs=[_VM], out_specs=_VM,
                  out_shape=jax.ShapeDtypeStruct((1, t.shape[1]), F32))(t)


def _adamw(w, g, m, v, *, name):
    rows, cols = w.shape
    tr = rows
    for cand in range(8, 513, 8):
        if rows % cand == 0:
            tr = cand
    c1 = 1.0 - ADAM_B1 ** ADAM_STEP
    c2 = 1.0 - ADAM_B2 ** ADAM_STEP

    def body(w_ref, g_ref, m_ref, v_ref, d_ref, nm_ref, nv_ref):
        gv = g_ref[...]
        nm = ADAM_B1 * m_ref[...] + (1.0 - ADAM_B1) * gv
        nv = ADAM_B2 * v_ref[...] + (1.0 - ADAM_B2) * (gv * gv)
        m_hat = nm / c1
        v_hat = nv / c2
        d_ref[...] = -ADAM_LR * (m_hat / (jnp.sqrt(v_hat) + ADAM_EPS) + ADAM_WD * w_ref[...])
        nm_ref[...] = nm
        nv_ref[...] = nv

    spec = pl.BlockSpec((tr, cols), lambda i: (i, 0))
    return _pcall(body, name=name, grid=(rows // tr,), in_specs=[spec] * 4, out_specs=[spec] * 3,
                  out_shape=[jax.ShapeDtypeStruct((rows, cols), F32)] * 3, compiler_params=_PAR)(w, g, m, v)


_MESH = pl.DeviceIdType.MESH
_ANY = pl.BlockSpec(memory_space=pl.ANY)
_CHIP_FLIPS = ((1, 0), (0, 1), (1, 1))


def _pos():
    return lax.axis_index("x"), lax.axis_index("y"), lax.axis_index("c")


def _flip(v, f):
    return 1 - v if f else v


def _sem_pairs(n):
    return [pltpu.SemaphoreType.DMA((n,)), pltpu.SemaphoreType.DMA((n,))]


def _small_allgather(blk, *, name):
    m_per, n = blk.shape

    def body(x_ref, out_ref, send_sems, recv_sems, local_sem):
        x, y, c = _pos()
        me, sibling = (x, y, c), (x, y, 1 - c)
        chips = [(_flip(x, fx), _flip(y, fy)) for fx, fy in _CHIP_FLIPS]

        def rows(px, py, pc):
            return out_ref.at[pl.ds((4 * px + 2 * py + pc) * m_per, m_per), :]

        def copy(k, block, to, src=None):
            return pltpu.make_async_remote_copy(
                src_ref=rows(*block) if src is None else src, dst_ref=rows(*block),
                send_sem=send_sems.at[k], recv_sem=recv_sems.at[k], device_id=to, device_id_type=_MESH)

        mine = pltpu.make_async_copy(x_ref, rows(*me), local_sem)
        mine.start()
        first = [copy(0, me, sibling, src=x_ref)]
        first += [copy(1 + j, me, (*chip, c), src=x_ref) for j, chip in enumerate(chips)]
        for cp in first:
            cp.start()
        passed = [copy(4 + j, (*chip, c), sibling) for j, chip in enumerate(chips)]
        for j, chip in enumerate(chips):
            copy(1 + j, (*chip, c), me).wait_recv()
            passed[j].start()
        copy(0, sibling, me).wait_recv()
        for j, chip in enumerate(chips):
            copy(4 + j, (*chip, 1 - c), me).wait_recv()
        for cp in first + passed:
            cp.wait_send()
        mine.wait()

    return _pcall(
        body, name=name, out_shape=jax.ShapeDtypeStruct((N_DEV * m_per, n), blk.dtype),
        in_specs=[_VM], out_specs=_VM,
        scratch_shapes=_sem_pairs(7) + [pltpu.SemaphoreType.DMA],
    )(blk)


def _allgather_weights(shards, *, name):
    nw = len(shards)

    def body(*refs):
        sh_refs, full_refs = refs[:nw], refs[nw:2 * nw]
        send_sems, recv_sems = refs[2 * nw:]
        x, y, c = _pos()
        j = 2 * x + y
        chips = [(_flip(x, fx), _flip(y, fy)) for fx, fy in _CHIP_FLIPS]

        def copy(k, src, dst, to):
            return pltpu.make_async_remote_copy(src_ref=src, dst_ref=dst, send_sem=send_sems.at[k],
                                                recv_sem=recv_sems.at[k], device_id=to, device_id_type=_MESH)

        first = [copy(6 * wi + k, sh_refs[wi].at[c], full_refs[wi].at[j, c], (px, py, c))
                 for wi in range(nw) for k, (px, py) in enumerate(chips)]
        for cp in first:
            cp.start()
        passed = []
        for wi in range(nw):
            for k, (px, py) in enumerate(chips):
                jp = 2 * px + py
                copy(6 * wi + k, sh_refs[wi].at[c], full_refs[wi].at[jp, c], (x, y, c)).wait_recv()
                cp = copy(6 * wi + 3 + k, full_refs[wi].at[jp, c], full_refs[wi].at[jp, c], (x, y, 1 - c))
                cp.start()
                passed.append(cp)
        for wi in range(nw):
            for k, (px, py) in enumerate(chips):
                jp = 2 * px + py
                copy(6 * wi + 3 + k, sh_refs[wi].at[c], full_refs[wi].at[jp, 1 - c], (x, y, c)).wait_recv()
        for cp in first + passed:
            cp.wait_send()

    return _pcall(
        body, name=name, out_shape=[jax.ShapeDtypeStruct((N_CHIP,) + s.shape, s.dtype) for s in shards],
        in_specs=[_ANY] * nw, out_specs=[_ANY] * nw, scratch_shapes=_sem_pairs(6 * nw),
    )(*shards)


def _sibling_send_halves(gs, *, name):
    nw = len(gs)

    def body(*refs):
        g_refs, out_refs = refs[:nw], refs[nw:2 * nw]
        send_sems, recv_sems = refs[2 * nw:]
        x, y, c = _pos()
        cps = [pltpu.make_async_remote_copy(src_ref=g_refs[wi].at[k, 1 - c], dst_ref=out_refs[wi].at[k],
                                            send_sem=send_sems.at[4 * wi + k], recv_sem=recv_sems.at[4 * wi + k],
                                            device_id=(x, y, 1 - c), device_id_type=_MESH)
               for wi in range(nw) for k in range(N_CHIP)]
        for cp in cps:
            cp.start()
        for cp in cps:
            cp.wait()

    return _pcall(body, name=name,
                  out_shape=[jax.ShapeDtypeStruct((g.shape[0],) + g.shape[2:], g.dtype) for g in gs],
                  in_specs=[_ANY] * nw, out_specs=[_ANY] * nw, scratch_shapes=_sem_pairs(N_CHIP * nw))(*gs)


def _chip_exchange(sbs, *, name):
    nw = len(sbs)

    def body(*refs):
        s_refs, out_refs = refs[:nw], refs[nw:2 * nw]
        send_sems, recv_sems = refs[2 * nw:]
        x, y, c = _pos()
        cps = []
        for wi in range(nw):
            for k, (fx, fy) in enumerate(_CHIP_FLIPS):
                px, py = _flip(x, fx), _flip(y, fy)
                cps.append(pltpu.make_async_remote_copy(
                    src_ref=s_refs[wi].at[2 * px + py], dst_ref=out_refs[wi].at[k], send_sem=send_sems.at[3 * wi + k],
                    recv_sem=recv_sems.at[3 * wi + k], device_id=(px, py, c), device_id_type=_MESH))
        for cp in cps:
            cp.start()
        for cp in cps:
            cp.wait()

    return _pcall(body, name=name, out_shape=[jax.ShapeDtypeStruct((3,) + s.shape[1:], s.dtype) for s in sbs],
                  in_specs=[_ANY] * nw, out_specs=[_ANY] * nw, scratch_shapes=_sem_pairs(3 * nw))(*sbs)


def _sibling_swap(ts, *, name):
    nw = len(ts)

    def body(*refs):
        t_refs, out_refs = refs[:nw], refs[nw:2 * nw]
        send_sems, recv_sems = refs[2 * nw:]
        x, y, c = _pos()
        cps = [pltpu.make_async_remote_copy(src_ref=t_refs[wi], dst_ref=out_refs[wi], send_sem=send_sems.at[wi],
                                            recv_sem=recv_sems.at[wi], device_id=(x, y, 1 - c), device_id_type=_MESH)
               for wi in range(nw)]
        for cp in cps:
            cp.start()
        for cp in cps:
            cp.wait()

    return _pcall(body, name=name, out_shape=[jax.ShapeDtypeStruct(t.shape, t.dtype) for t in ts],
                  in_specs=[_ANY] * nw, out_specs=[_ANY] * nw, scratch_shapes=_sem_pairs(nw))(*ts)


def _rs_add(g, ra, c_arr, *, name):
    n, _, r, w = g.shape

    def body(c_ref, g_ref, ra_ref, s_ref, sb_ref):
        t = g_ref[...] + ra_ref[...]
        s_ref[...] = t
        sb_ref[...] = t.astype(BF16)

    grid_spec = pltpu.PrefetchScalarGridSpec(
        num_scalar_prefetch=1, grid=(n,),
        in_specs=[pl.BlockSpec((None, None, r, w), lambda k, c: (k, c[0], 0, 0)),
                  pl.BlockSpec((None, r, w), lambda k, c: (k, 0, 0))],
        out_specs=[pl.BlockSpec((None, r, w), lambda k, c: (k, 0, 0))] * 2)
    return _pcall(body, name=name, grid_spec=grid_spec,
                  out_shape=[jax.ShapeDtypeStruct((n, r, w), F32), jax.ShapeDtypeStruct((n, r, w), BF16)],
                  compiler_params=_PAR)(c_arr, g, ra)


def _rs_final(s, rb, j_arr, *, name):
    _, r, w = s.shape

    def body(j_ref, s_ref, rb_ref, t_ref):
        t_ref[...] = ((s_ref[...] + rb_ref[0].astype(F32)) + rb_ref[1].astype(F32)) + rb_ref[2].astype(F32)

    grid_spec = pltpu.PrefetchScalarGridSpec(
        num_scalar_prefetch=1, grid=(1,),
        in_specs=[pl.BlockSpec((None, r, w), lambda i, j: (j[0], 0, 0)),
                  pl.BlockSpec((3, r, w), lambda i, j: (0, 0, 0))],
        out_specs=pl.BlockSpec((r, w), lambda i, j: (0, 0)))
    return _pcall(body, name=name, grid_spec=grid_spec, out_shape=jax.ShapeDtypeStruct((r, w), F32),
                  compiler_params=_ARB)(j_arr, s, rb)


def _perm_rows(t, d):
    s, w = t.shape
    return t.reshape(s // d, d, w).transpose(1, 0, 2).reshape(s, w)


def _unperm_rows(t, d):
    s, w = t.shape
    return t.reshape(d, s // d, w).transpose(1, 0, 2).reshape(s, w)


def _tile2(g):
    return jnp.concatenate([g, g], axis=1)


_BIG = ("w_in", "w_ffn_in", "w_ffn_out", "w_o", "w_proj_a", "w_proj_b")
_BIG_SHARD = {"w_in": (1024, 1472), "w_ffn_in": (1024, 1408), "w_ffn_out": (704, 1024), "w_o": (256, 1024),
              "w_proj_a": (512, 256), "w_proj_b": (256, 256)}


def _device_step(x2, tgt, mod, wts, g_norm1, g_norm2, b_gate, g_qa, g_ka, g_qb, g_kb, rpb):
    d = D_MODEL
    sh1, sc1, gt1, sh2, sc2, gt2 = [mod[:, k * d:(k + 1) * d] for k in range(6)]

    oh_row, oh_col, na_mask, na_ks, na_cls = _na_constants()
    rpb_pad = jnp.pad(rpb, ((0, 0), (0, 16 - _RPB_RO), (0, LANES - _RPB_CO)))
    tab = _rpb_expand(rpb_pad, jnp.asarray(oh_row), jnp.asarray(oh_col.T.copy()), name="rpb_expand")
    tab = tab[:, :_RPB_ROWS].reshape(NA_HEADS, 3, NA_QROWS, NA_KROWS, GRID_W, GRID_W)
    tab_a = tab.transpose(1, 0, 2, 4, 3, 5).reshape(3, NA_HEADS, Q_BLOCK, NA_NK) + jnp.asarray(na_mask)
    na_ks, na_cls = jnp.asarray(na_ks), jnp.asarray(na_cls)
    dil = [_dil_constants(dd) for _, dd in DIL_CONFIGS]
    tab_d = jnp.asarray(dil[0][0])
    gains = jnp.concatenate([_tile2(g_qa), _tile2(g_ka), _tile2(g_qb), _tile2(g_kb)], axis=0)
    cos_t, sa_t, sb_t = _rope_tables()

    h1, qkvn, qk_pre, gates = _pre_attn_fwd(x2, cos_t, sa_t, sb_t, g_norm1, sc1, sh1, wts["w_qkv"], wts["w_gates"],
                                            gains, name="pre_attn_fwd")
    o_a, lse_a = _attn_fwd(qkvn, 0, 4, 8, 4, tab_a, na_ks, na_cls, NA_NK, name="attn_a_fwd")
    arrs, o_p, l_p, o_g, l_g = [], [], [], [], []
    for g, (_, dd) in enumerate(DIL_CONFIGS):
        ks_g, cls_g = jnp.asarray(dil[g][1]), jnp.asarray(dil[g][2])
        if dd == 1:
            arr, cb = qkvn, (12, 18, 24)
        else:
            col = lambda base: qkvn[:, base + WB_OUT * g: base + WB_OUT * (g + 1)]
            arr = _perm_rows(jnp.concatenate([col(3 * WA), col(3 * WA + WB), col(3 * WA + 2 * WB)], axis=1), dd)
            cb = (0, 2, 4)
        op, lp = _attn_fwd(arr, cb[0], cb[1], cb[2], 2, tab_d, ks_g, cls_g, DIL_NK, name=f"attn_d{g}_fwd")
        arrs.append((arr, cb, ks_g, cls_g))
        o_p.append(op)
        l_p.append(lp)
        o_g.append(op if dd == 1 else _unperm_rows(op, dd))
        l_g.append(lp if dd == 1 else _unperm_rows(lp, dd))
    o_b, merged, mo, x1, h2 = _post_attn_fwd(o_a, o_g, l_g, gates, x2, wts["w_pa"], wts["w_pb"], wts["w_o"], b_gate,
                                             gt1, g_norm2, sc2, sh2, name="post_attn_fwd")
    act, ff = _ffn_fwd(h2, wts["w_ffn_in"], name="ffn_fwd")

    dy, dffo, dff, dgt2, loss_v = _ffn_mid(act, ff, x1, tgt, wts["w_ffn_out"], gt2, name="ffn_mid")
    grads = {}
    g_ffn_out = _wgrad(act, dffo, name="wg_ffn_out", tm=D_FF // 2, tn=d)
    grads["w_ffn_out"] = g_ffn_out.reshape(N_CHIP, D_FF // N_CHIP, d)
    grads["w_ffn_in"] = _wgrad(h2, dff, name="wg_ffn_in", tm=512, tn=2 * FF_CHIP, chips=N_CHIP)
    dx1, dmo, sums2 = _ffn_in_bwd(dff, x1, dy, mo, wts["w_ffn_in"], g_norm2, sc2, gt1, name="ffn_in_bwd")
    grads["w_o"] = _wgrad(merged, dmo, name="wg_o", tm=d, tn=d).reshape(N_CHIP, d // N_CHIP, d)
    pab = _post_attn_bwd(dmo, gates, o_a, o_g, l_g, wts["w_pa"], wts["w_pb"], wts["w_o"], b_gate, name="post_attn_bwd")
    dpa, dpb, dgates, do_a = pab[:4]
    do_g, dl_g, dbg = pab[4:7], pab[7:10], pab[10]
    g_pa = _wgrad(o_a, dpa, name="wg_pa", tm=WA, tn=d)
    g_pb = _wgrad(o_b, dpb, name="wg_pb", tm=WB_OUT, tn=d)
    grads["w_proj_a"] = g_pa.reshape(WA, N_CHIP, d // N_CHIP).transpose(1, 0, 2)
    grads["w_proj_b"] = g_pb.reshape(WB_OUT, N_CHIP, d // N_CHIP).transpose(1, 0, 2)
    dqa, dka, dva, dtab = _attn_bwd(qkvn, 0, 4, 8, 4, tab_a, na_ks, na_cls, NA_NK, do_a, o_a, lse_a,
                                    name="attn_a_bwd", want_dbias=True)
    dqs, dks, dvs = [], [], []
    for g, (_, dd) in enumerate(DIL_CONFIGS):
        arr, cb, ks_g, cls_g = arrs[g]
        dog = do_g[g] if dd == 1 else _perm_rows(do_g[g], dd)
        dlg = dl_g[g] if dd == 1 else _perm_rows(dl_g[g], dd)
        dq, dk, dv = _attn_bwd(arr, cb[0], cb[1], cb[2], 2, tab_d, ks_g, cls_g, DIL_NK, dog, o_p[g], l_p[g],
                               name=f"attn_d{g}_bwd", dlse=dlg)
        if dd != 1:
            dq, dk, dv = _unperm_rows(dq, dd), _unperm_rows(dk, dd), _unperm_rows(dv, dd)
        dqs.append(dq)
        dks.append(dk)
        dvs.append(dv)
    dqkv, grad_x, dgains, sums1 = _pre_attn_bwd(qk_pre, [dqa, dka, dva] + dqs + dks + dvs, dgates, x2, dx1, cos_t, sa_t,
                                                sb_t, wts["w_qkv"], wts["w_gates"], gains, g_norm1, sc1,
                                                name="pre_attn_bwd")
    g_qkv = _wgrad(h1, dqkv, name="wg_qkv", tm=d, tn=W_QKV // 2)
    g_gates = _wgrad(h1, dgates, name="wg_gates", tm=d, tn=W_GATES)
    nc, cut = _BIG_SHARD["w_in"][1], 3 * _BIG_SHARD["w_in"][1] - W_QKV
    grads["w_in"] = jnp.stack([g_qkv[:, :nc], g_qkv[:, nc:2 * nc],
                               jnp.concatenate([g_qkv[:, 2 * nc:], g_gates[:, :cut]], axis=1), g_gates[:, cut:]])

    dtab = dtab.reshape(3, NA_HEADS, NA_QROWS, GRID_W, NA_KROWS, GRID_W).transpose(1, 0, 2, 4, 3, 5)
    dtab = jnp.pad(dtab.reshape(NA_HEADS, _RPB_ROWS, GRID_W * GRID_W), ((0, 0), (0, _RPB_ROWS_PAD - _RPB_ROWS), (0, 0)))
    g_rpb = _rpb_reduce(dtab, jnp.asarray(oh_row.T.copy()), jnp.asarray(oh_col), name="rpb_reduce")
    g_rpb = g_rpb[:, :_RPB_RO, :_RPB_CO]

    dmod = jnp.concatenate([sums1[0:1], sums1[1:2], sums2[3:4], sums2[0:1], sums2[1:2], dgt2], axis=1)
    small = dict(g_norm1=sums1[2:3], g_norm2=sums2[2:3], b_gate=dbg, g_qa=dgains[0:1, :HEAD_DIM],
                 g_ka=dgains[1:2, :HEAD_DIM], g_qb=dgains[2:3, :HEAD_DIM], g_kb=dgains[3:4, :HEAD_DIM], rpb=g_rpb)
    return loss_v, grad_x, grads, dmod, small


_SMALL = ("b_ada", "g_norm1", "g_norm2", "b_gate", "g_qa", "g_ka", "g_qb", "g_kb", "rpb")
_SMALL_N = {"b_ada": 6 * D_MODEL, "g_norm1": D_MODEL, "g_norm2": D_MODEL, "b_gate": 2 * D_MODEL, "g_qa": HEAD_DIM,
            "g_ka": HEAD_DIM, "g_qb": HEAD_DIM, "g_kb": HEAD_DIM, "rpb": NA_HEADS * _RPB_RO * _RPB_CO}


def _pack_small(parts):
    flat = [parts[n].reshape(1, _SMALL_N[n]) for n in _SMALL]
    used = sum(_SMALL_N.values())
    return jnp.concatenate(flat + [jnp.zeros((1, STATS_W - used), F32)], axis=1)


def _unpack_small(v, shapes):
    out, at = {}, 0
    for n in _SMALL:
        out[n] = v[:, at:at + _SMALL_N[n]].reshape(shapes[n])
        at += _SMALL_N[n]
    return out


def _join_cols(t):
    _, r, c = t.shape
    return t.transpose(1, 0, 2).reshape(r, N_CHIP * c)


def kernel(x, c, w_ada, b_ada, g_norm1, g_norm2, w_in, b_gate, g_qa, g_ka, g_qb, g_kb, rpb, w_proj_a, w_proj_b, w_o, w_ffn_in, w_ffn_out, loss_target, m_w_ada, m_b_ada, m_g_norm1, m_g_norm2, m_w_in, m_b_gate, m_g_qa, m_g_ka, m_g_qb, m_g_kb, m_rpb, m_w_proj_a, m_w_proj_b, m_w_o, m_w_ffn_in, m_w_ffn_out, v_w_ada, v_b_ada, v_g_norm1, v_g_norm2, v_w_in, v_b_gate, v_g_qa, v_g_ka, v_g_qb, v_g_kb, v_rpb, v_w_proj_a, v_w_proj_b, v_w_o, v_w_ffn_in, v_w_ffn_out):
    names = ("w_ada", "b_ada", "g_norm1", "g_norm2", "w_in", "b_gate", "g_qa", "g_ka", "g_qb", "g_kb", "rpb",
             "w_proj_a", "w_proj_b", "w_o", "w_ffn_in", "w_ffn_out")
    w = dict(zip(names, (w_ada, b_ada, g_norm1, g_norm2, w_in, b_gate, g_qa, g_ka, g_qb, g_kb, rpb, w_proj_a, w_proj_b,
                         w_o, w_ffn_in, w_ffn_out)))
    m = dict(zip(names, (m_w_ada, m_b_ada, m_g_norm1, m_g_norm2, m_w_in, m_b_gate, m_g_qa, m_g_ka, m_g_qb, m_g_kb, m_rpb,
                         m_w_proj_a, m_w_proj_b, m_w_o, m_w_ffn_in, m_w_ffn_out)))
    v = dict(zip(names, (v_w_ada, v_b_ada, v_g_norm1, v_g_norm2, v_w_in, v_b_gate, v_g_qa, v_g_ka, v_g_qb, v_g_kb, v_rpb,
                         v_w_proj_a, v_w_proj_b, v_w_o, v_w_ffn_in, v_w_ffn_out)))
    d = D_MODEL
    xi, yi, ci = _pos()
    chip = 2 * xi + yi
    me = 2 * chip + ci
    ada_cols = 6 * d // N_CHIP

    c_all = _small_allgather(jnp.broadcast_to(c, (8, d)), name="ag_c")[::8]
    b_sh = lax.dynamic_slice(b_ada, (0, chip * ada_cols), (1, ada_cols))
    mod_part = _ada_fwd(c_all, w_ada[0], b_sh, name="ada_fwd")
    mod_all = _small_allgather(mod_part, name="ag_mod").reshape(N_CHIP, 2, 8, ada_cols)[:, 0]
    mod = lax.dynamic_index_in_dim(mod_all, me, axis=1, keepdims=False).reshape(1, 6 * d)

    halves = {n: (2, _BIG_SHARD[n][0] // 2, _BIG_SHARD[n][1]) for n in _BIG}
    shards = [w[n][0].astype(BF16).reshape(halves[n]) for n in _BIG]
    gathered = _allgather_weights(shards, name="ag_weights")
    full = {}
    for n, sh, ga in zip(_BIG, shards, gathered):
        ga = lax.dynamic_update_slice(ga, sh[None], (chip, 0, 0, 0))
        full[n] = ga.reshape((N_CHIP,) + _BIG_SHARD[n])
    p_in = full["w_in"]
    cut = W_QKV - 2 * _BIG_SHARD["w_in"][1]
    wts = dict(w_qkv=jnp.concatenate([p_in[0], p_in[1], p_in[2][:, :cut]], axis=1),
               w_gates=jnp.concatenate([p_in[2][:, cut:], p_in[3]], axis=1),
               w_pa=_join_cols(full["w_proj_a"]), w_pb=_join_cols(full["w_proj_b"]), w_o=full["w_o"].reshape(d, d),
               w_ffn_in=full["w_ffn_in"], w_ffn_out=full["w_ffn_out"].reshape(D_FF, d))

    loss_v, grad_x, grads, dmod, small = _device_step(
        x[0], loss_target[0], mod, wts, g_norm1, g_norm2, b_gate, g_qa, g_ka, g_qb, g_kb, rpb[0])

    c_arr, chip_arr = ci.reshape(1).astype(jnp.int32), chip.reshape(1).astype(jnp.int32)
    gps = [grads[n].reshape((N_CHIP,) + halves[n]) for n in _BIG]
    ras = _sibling_send_halves(gps, name="rs_sibling")
    sums = [_rs_add(gp, ra, c_arr, name=f"rs_add_{n}") for n, gp, ra in zip(_BIG, gps, ras)]
    rbs = _chip_exchange([sb for _, sb in sums], name="rs_chips")
    ts = [_rs_final(sf, rb, chip_arr, name=f"rs_final_{n}") for n, (sf, _), rb in zip(_BIG, sums, rbs)]
    others = _sibling_swap(ts, name="rs_pair")
    g_big = {n: jnp.where(ci == 0, jnp.concatenate([t, o], axis=0), jnp.concatenate([o, t], axis=0))
             for n, t, o in zip(_BIG, ts, others)}

    stats = _pack_small(dict(b_ada=dmod, **small))
    rows = _small_allgather(jnp.broadcast_to(stats, (8, STATS_W)), name="ag_stats")[::8]
    dmod_sh = lax.dynamic_slice(rows, (0, chip * ada_cols), (8, ada_cols))
    g = dict(g_big)
    g["w_ada"] = _ada_bwd(c_all.T, dmod_sh, name="ada_bwd")
    tot = _row_sum(rows, name="stats_sum")
    g_small = _unpack_small(tot, {n: w[n].shape for n in _SMALL})

    delta, new_m, new_v = {}, {}, {}
    for n in _BIG + ("w_ada",):
        dl, nm, nv = _adamw(w[n][0], g[n], m[n][0], v[n][0], name=f"adamw_{n}")
        delta[n], new_m[n], new_v[n], g[n] = dl[None], nm[None], nv[None], g[n][None]
    shapes = {n: w[n].shape for n in _SMALL}
    dl, nm, nv = _adamw(_pack_small({n: w[n] for n in _SMALL}), tot, _pack_small({n: m[n] for n in _SMALL}),
                        _pack_small({n: v[n] for n in _SMALL}), name="adamw_small")
    delta.update(_unpack_small(dl, shapes))
    new_m.update(_unpack_small(nm, shapes))
    new_v.update(_unpack_small(nv, shapes))
    g.update(g_small)

    loss = lax.psum(loss_v[0, 0], ("x", "y", "c"))
    return (loss, grad_x[None], *[g[n] for n in names], *[delta[n] for n in names], *[new_m[n] for n in names],
            *[new_v[n] for n in names])
```

```python
import numpy as np

import jax
import jax.numpy as jnp
from jax import lax
from jax.experimental import pallas as pl
from jax.experimental.pallas import tpu as pltpu

F32 = jnp.float32
BF16 = jnp.bfloat16

D_MODEL = 1024
SEQ = 8192
HEAD_DIM = 64
GRID_W = 64
ROWS = SEQ // GRID_W
NA_HEADS = 8
NA_KH = 8
NA_KW = 16
DIL_CONFIGS = ((128, 1), (512, 4), (2048, 16))
ROT_DIM = 16
ROPE_THETA = 500000.0
D_FF = 2816
EPS = 1e-6
NEG = -1e30
WA = 512
WB = 768
WB_OUT = 256
W_QKV = 3 * WA + 3 * WB
W_QK = 2 * WA + 2 * WB
W_GATES = 2 * D_MODEL
SCALE = HEAD_DIM ** -0.5

ADAM_LR = 0.001
ADAM_B1 = 0.9
ADAM_B2 = 0.999
ADAM_EPS = 1e-08
ADAM_WD = 0.01
ADAM_STEP = 10

LANES = 128
ROW_TILE = 256
Q_BLOCK = 256
NA_QROWS = Q_BLOCK // GRID_W
NA_KROWS = NA_QROWS + NA_KH - 1
NA_NK = NA_KROWS * GRID_W
DIL_HALF = 64
DIL_NK = Q_BLOCK + 2 * DIL_HALF
N_QBLK = SEQ // Q_BLOCK

N_DEV = 8
N_CHIP = 4
FF_CHIP = 2 * D_FF // N_CHIP
STATS_W = 14336


def _pcall(body, *, name, **kw):
    return pl.pallas_call(body, name=name, **kw)


_NT = (((1,), (1,)), ((), ()))
_TN = (((0,), (0,)), ((), ()))
_ARB = pltpu.CompilerParams(dimension_semantics=("arbitrary",))
_PAR = pltpu.CompilerParams(dimension_semantics=("parallel",))


def _dot(a, b):
    return jnp.dot(a, b, preferred_element_type=F32)


def _dot_nt(a, b):
    return lax.dot_general(a, b, _NT, preferred_element_type=F32)


def _wgrad(a, b, *, name, tm, tn, tk=1024, chips=None):
    s, ma = a.shape
    nb = b.shape[1]
    nk = s // tk
    nc = nb // chips if chips else tn
    cpb = tn // nc

    def body(a_ref, b_ref, o_ref, acc):
        k = pl.program_id(2)
        r = lax.dot_general(a_ref[...].astype(BF16), b_ref[...].astype(BF16), _TN, preferred_element_type=F32)

        @pl.when(k == 0)
        def _():
            acc[...] = r

        @pl.when(k > 0)
        def _():
            acc[...] += r

        @pl.when(k == nk - 1)
        def _():
            if chips:
                for q in range(cpb):
                    o_ref[q] = acc[:, q * nc:(q + 1) * nc]
            else:
                o_ref[...] = acc[...]

    if chips:
        o_spec = pl.BlockSpec((cpb, tm, nc), lambda i, j, k: (j, i, 0))
        out_shape = jax.ShapeDtypeStruct((chips, ma, nc), F32)
    else:
        o_spec = pl.BlockSpec((tm, tn), lambda i, j, k: (i, j))
        out_shape = jax.ShapeDtypeStruct((ma, nb), F32)
    return _pcall(
        body, name=name, grid=(ma // tm, nb // tn, nk),
        in_specs=[pl.BlockSpec((tk, tm), lambda i, j, k: (k, i)), pl.BlockSpec((tk, tn), lambda i, j, k: (k, j))],
        out_specs=o_spec, out_shape=out_shape, scratch_shapes=[pltpu.VMEM((tm, tn), F32)],
        compiler_params=pltpu.CompilerParams(dimension_semantics=("parallel", "parallel", "arbitrary")),
    )(a, b)


def _row_call(body, *, name, row_ins, res_ins, row_outs, acc_outs=(), scratch=()):
    s = row_ins[0].shape[0]
    n = s // ROW_TILE
    nri, nre, nro, nao = len(row_ins), len(res_ins), len(row_outs), len(acc_outs)

    def whole(shape):
        nd = len(shape)
        return pl.BlockSpec(tuple(shape), lambda i: (0,) * nd, pipeline_mode=pl.Buffered(1))

    def whole_out(shape):
        nd = len(shape)
        return pl.BlockSpec(tuple(shape), lambda i: (0,) * nd)

    in_specs = [pl.BlockSpec((ROW_TILE, a.shape[1]), lambda i: (i, 0)) for a in row_ins]
    in_specs += [whole(a.shape) for a in res_ins]
    out_specs = [pl.BlockSpec((ROW_TILE, w), lambda i: (i, 0)) for w, _ in row_outs]
    out_specs += [whole_out(shp) for shp, _ in acc_outs]
    out_shape = [jax.ShapeDtypeStruct((s, w), dt) for w, dt in row_outs]
    out_shape += [jax.ShapeDtypeStruct(tuple(shp), dt) for shp, dt in acc_outs]

    def wrapped(*refs):
        at = [0, nri, nri + nre, nri + nre + nro, nri + nre + nro + nao]
        body(pl.program_id(0), n, refs[at[0]:at[1]], refs[at[1]:at[2]], refs[at[2]:at[3]], refs[at[3]:at[4]],
             refs[at[4]:])

    return _pcall(wrapped, name=name, grid=(n,), in_specs=in_specs, out_specs=out_specs, out_shape=out_shape,
                  scratch_shapes=list(scratch), compiler_params=_ARB)(*row_ins, *res_ins)


def _fold8(t):
    r, w = t.shape
    return jnp.sum(t.reshape(r // 8, 8, w), axis=0)


def _sigmoid(t):
    return 1.0 / (1.0 + jnp.exp(-t))


def _head_lanes():
    return lax.broadcasted_iota(jnp.int32, (1, LANES), 1) < HEAD_DIM


def _head_mean(t, lo):
    s_lo = jnp.sum(jnp.where(lo, t, 0.0), axis=1, keepdims=True)
    s_hi = jnp.sum(jnp.where(lo, 0.0, t), axis=1, keepdims=True)
    return jnp.where(lo, s_lo, s_hi) * (1.0 / HEAD_DIM)


def _rms_mod(xv, g, sc, sh):
    rstd = lax.rsqrt(jnp.mean(xv * xv, axis=1, keepdims=True) + EPS)
    return (xv * rstd * g) * (1.0 + sc) + sh


def _rms_mod_bwd(xv, dh, g, sc):
    rstd = lax.rsqrt(jnp.mean(xv * xv, axis=1, keepdims=True) + EPS)
    xhat = xv * rstd
    dn = dh * (1.0 + sc)
    dxhat = dn * g
    dx = rstd * (dxhat - xhat * jnp.mean(dxhat * xhat, axis=1, keepdims=True))
    return dx, dh, dh * (xhat * g), dn * xhat


def _mix_weights(ls):
    m = jnp.maximum(jnp.maximum(ls[0], ls[1]), ls[2])
    es = [jnp.exp(t - m) for t in ls]
    den = es[0] + es[1] + es[2]
    return [e / den for e in es]


def _rope_tables():
    half = ROT_DIM // 2
    inv_freq = ROPE_THETA ** (-(jnp.arange(half, dtype=F32) * 2.0) / ROT_DIM)
    lane = np.arange(LANES) % HEAD_DIM
    ang = jnp.arange(SEQ).astype(F32)[:, None] * jnp.tile(inv_freq, LANES // half)[None, :]
    cos, sin = jnp.cos(ang), jnp.sin(ang)
    first, second = jnp.asarray(lane < half)[None, :], jnp.asarray((lane >= half) & (lane < ROT_DIM))[None, :]
    cos_t = jnp.where(first | second, cos, 1.0)
    return cos_t, jnp.where(second, sin, 0.0), jnp.where(first, -sin, 0.0)


_SECTIONS = ((0, WA, 0, False), (WA, 2 * WA, 1, False), (2 * WA, 3 * WA, -1, False),
             (3 * WA, 3 * WA + WB, 2, True), (3 * WA + WB, 3 * WA + 2 * WB, 3, True), (3 * WA + 2 * WB, W_QKV, -1, False))


def _pre_attn_fwd(x, cos_t, sa_t, sb_t, g1, sc1, sh1, w_qkv, w_gates, gains, *, name):
    half = ROT_DIM // 2

    def body(i, n, rin, res, rout, aout, scr):
        x_ref, cos_ref, sa_ref, sb_ref = rin
        g_ref, sc_ref, sh_ref, wq_ref, wg_ref, gains_ref = res
        h1_ref, qkvn_ref, pre_ref, gates_ref = rout
        hb = _rms_mod(x_ref[...], g_ref[...], sc_ref[...], sh_ref[...]).astype(BF16)
        h1_ref[...] = hb
        gates_ref[...] = _dot(hb, wg_ref[...]).astype(BF16)
        lo = _head_lanes()
        cosv, sav, sbv = cos_ref[...], sa_ref[...], sb_ref[...]
        pre_at = 0
        for c0, c1, kind, rot in _SECTIONS:
            sec = _dot(hb, wq_ref[:, c0:c1])
            for ch in range((c1 - c0) // LANES):
                t = sec[:, ch * LANES:(ch + 1) * LANES]
                if kind >= 0:
                    pre_ref[:, pre_at:pre_at + LANES] = t.astype(BF16)
                    pre_at += LANES
                    t = t * lax.rsqrt(_head_mean(t * t, lo) + EPS) * gains_ref[kind:kind + 1, :]
                    if rot:
                        t = t * cosv + pltpu.roll(t, half, 1) * sav + pltpu.roll(t, LANES - half, 1) * sbv
                qkvn_ref[:, c0 + ch * LANES:c0 + (ch + 1) * LANES] = t.astype(BF16)

    return _row_call(body, name=name, row_ins=[x, cos_t, sa_t, sb_t], res_ins=[g1, sc1, sh1, w_qkv, w_gates, gains],
                     row_outs=[(D_MODEL, BF16), (W_QKV, BF16), (W_QK, BF16), (W_GATES, BF16)])


def _pre_attn_bwd(qk_pre, d_parts, dgates, x, dx1, cos_t, sa_t, sb_t, w_qkv, w_gates, gains, g1, sc1, *, name):
    half = ROT_DIM // 2
    nparts = len(d_parts)
    where = []
    for pi, part in enumerate(d_parts):
        where += [(pi, cj) for cj in range(part.shape[1] // LANES)]
    assert len(where) == W_QKV // LANES

    def body(i, n, rin, res, rout, aout, scr):
        pre_ref, d_refs = rin[0], rin[1:1 + nparts]
        dgates_ref, x_ref, dx1_ref, cos_ref, sa_ref, sb_ref = rin[1 + nparts:]
        wq_ref, wg_ref, gains_ref, g_ref, sc_ref = res
        dqkv_ref, gx_ref = rout
        dgains_ref, sums_ref = aout
        accg, accs = scr

        @pl.when(i == 0)
        def _():
            accg[...] = jnp.zeros_like(accg)
            accs[...] = jnp.zeros_like(accs)

        lo = _head_lanes()
        cosv, sav, sbv = cos_ref[...], sa_ref[...], sb_ref[...]
        dh = _dot_nt(dgates_ref[...], wg_ref[...])
        pre_at = 0
        for c0, c1, kind, rot in _SECTIONS:
            for ch in range((c1 - c0) // LANES):
                pi, cj = where[c0 // LANES + ch]
                dt = d_refs[pi][:, cj * LANES:(cj + 1) * LANES]
                if kind >= 0:
                    if rot:
                        dt = dt * cosv + pltpu.roll(dt * sav, LANES - half, 1) + pltpu.roll(dt * sbv, half, 1)
                    t = pre_ref[:, pre_at:pre_at + LANES].astype(F32)
                    pre_at += LANES
                    rstd = lax.rsqrt(_head_mean(t * t, lo) + EPS)
                    xhat = t * rstd
                    accg[kind] += _fold8(dt * xhat)
                    dxhat = dt * gains_ref[kind:kind + 1, :]
                    dt = rstd * (dxhat - xhat * _head_mean(dxhat * xhat, lo))
                dqkv_ref[:, c0 + ch * LANES:c0 + (ch + 1) * LANES] = dt.astype(BF16)
            dh = dh + _dot_nt(dqkv_ref[:, c0:c1], wq_ref[:, c0:c1])
        dx, t_sh, t_sc, t_g = _rms_mod_bwd(x_ref[...], dh, g_ref[...], sc_ref[...])
        gx_ref[...] = dx1_ref[...] + dx
        accs[0] += _fold8(t_sh)
        accs[1] += _fold8(t_sc)
        accs[2] += _fold8(t_g)

        @pl.when(i == n - 1)
        def _():
            t = jnp.sum(accg[...], axis=1)
            dgains_ref[...] = t + pltpu.roll(t, HEAD_DIM, 1)
            sums_ref[...] = jnp.sum(accs[...], axis=1)

    return _row_call(
        body, name=name, row_ins=[qk_pre, *d_parts, dgates, x, dx1, cos_t, sa_t, sb_t],
        res_ins=[w_qkv, w_gates, gains, g1, sc1], row_outs=[(W_QKV, BF16), (D_MODEL, F32)],
        acc_outs=[((4, LANES), F32), ((3, D_MODEL), F32)],
        scratch=[pltpu.VMEM((4, 8, LANES), F32), pltpu.VMEM((3, 8, D_MODEL), F32)])


def _post_attn_fwd(o_a, o_g, l_g, gates, x, w_pa, w_pb, w_o, b_gate, gt1, g2, sc2, sh2, *, name):
    d = D_MODEL

    def body(i, n, rin, res, rout, aout, scr):
        oa_ref, o0, o1, o2, l0, l1, l2, gates_ref, x_ref = rin
        wpa_ref, wpb_ref, wo_ref, b_ref, gt_ref, g_ref, sc_ref, sh_ref = res
        ob_ref, merged_ref, mo_ref, x1_ref, h2_ref = rout
        ws = _mix_weights([l0[...], l1[...], l2[...]])
        obb = (ws[0] * o0[...] + ws[1] * o1[...] + ws[2] * o2[...]).astype(BF16)
        ob_ref[...] = obb
        pa = _dot(oa_ref[...].astype(BF16), wpa_ref[...])
        pb = _dot(obb, wpb_ref[...])
        ga = _sigmoid(gates_ref[:, :d].astype(F32) + b_ref[:, :d])
        gb = _sigmoid(gates_ref[:, d:].astype(F32) + b_ref[:, d:])
        merged = (ga * pa + gb * pb).astype(BF16)
        merged_ref[...] = merged
        mo = _dot(merged, wo_ref[...])
        mo_ref[...] = mo.astype(BF16)
        x1 = x_ref[...] + gt_ref[...] * mo
        x1_ref[...] = x1
        h2_ref[...] = _rms_mod(x1, g_ref[...], sc_ref[...], sh_ref[...]).astype(BF16)

    return _row_call(body, name=name, row_ins=[o_a, *o_g, *l_g, gates, x],
                     res_ins=[w_pa, w_pb, w_o, b_gate, gt1, g2, sc2, sh2],
                     row_outs=[(WB_OUT, BF16), (d, BF16), (d, BF16), (d, F32), (d, BF16)])


def _ffn_fwd(h2, w_ffn_in, *, name):
    def body(i, n, rin, res, rout, aout, scr):
        (h_ref,), (w_ref,), (act_ref, ff_ref) = rin, res, rout
        hv = h_ref[...]
        for q in range(2):
            a = _dot(hv, w_ref[q])
            up = _dot(hv, w_ref[q + 2])
            sl = slice(q * FF_CHIP, (q + 1) * FF_CHIP)
            act_ref[:, sl] = (a * _sigmoid(a) * up).astype(BF16)
            ff_ref[:, sl] = a.astype(BF16)
            ff_ref[:, D_FF + q * FF_CHIP:D_FF + (q + 1) * FF_CHIP] = up.astype(BF16)

    return _row_call(body, name=name, row_ins=[h2], res_ins=[w_ffn_in], row_outs=[(D_FF, BF16), (2 * D_FF, BF16)])


def _ffn_mid(act, ff, x1, tgt, w_ffn_out, gt2, *, name):
    d = D_MODEL

    def body(i, n, rin, res, rout, aout, scr):
        act_ref, ff_ref, x1_ref, tgt_ref = rin
        wo_ref, gt_ref = res
        dy_ref, dffo_ref, dff_ref = rout
        dgt_ref, loss_ref = aout
        (acc,) = scr

        @pl.when(i == 0)
        def _():
            acc[...] = jnp.zeros_like(acc)

        ffo = _dot(act_ref[...], wo_ref[...])
        gtv = gt_ref[...]
        e = x1_ref[...] + gtv * ffo - tgt_ref[...]
        dy = e * (1.0 / d)
        dy_ref[...] = dy
        dffo = (gtv * dy).astype(BF16)
        dffo_ref[...] = dffo
        acc[0] += _fold8(dy * ffo)
        acc[1] += _fold8(e * e)
        for q in range(2):
            sl = slice(q * FF_CHIP, (q + 1) * FF_CHIP)
            su = slice(D_FF + q * FF_CHIP, D_FF + (q + 1) * FF_CHIP)
            dact = _dot_nt(dffo, wo_ref[sl, :])
            a = ff_ref[:, sl].astype(F32)
            up = ff_ref[:, su].astype(F32)
            sg = _sigmoid(a)
            dff_ref[:, sl] = (dact * up * (sg * (1.0 + a * (1.0 - sg)))).astype(BF16)
            dff_ref[:, su] = (dact * (a * sg)).astype(BF16)

        @pl.when(i == n - 1)
        def _():
            dgt_ref[...] = jnp.sum(acc[0], axis=0, keepdims=True)
            tot = jnp.sum(jnp.sum(acc[1], axis=0, keepdims=True), axis=1, keepdims=True)
            loss_ref[...] = jnp.broadcast_to(tot * (0.5 / d), (1, LANES))

    return _row_call(body, name=name, row_ins=[act, ff, x1, tgt], res_ins=[w_ffn_out, gt2],
                     row_outs=[(d, F32), (d, BF16), (2 * D_FF, BF16)], acc_outs=[((1, d), F32), ((1, LANES), F32)],
                     scratch=[pltpu.VMEM((2, 8, d), F32)])


def _ffn_in_bwd(dff, x1, dy, mo, w_ffn_in, g2, sc2, gt1, *, name):
    d = D_MODEL

    def body(i, n, rin, res, rout, aout, scr):
        dff_ref, x1_ref, dy_ref, mo_ref = rin
        w_ref, g_ref, sc_ref, gt_ref = res
        dx1_ref, dmo_ref = rout
        (sums_ref,) = aout
        (acc,) = scr

        @pl.when(i == 0)
        def _():
            acc[...] = jnp.zeros_like(acc)

        dh = _dot_nt(dff_ref[:, :FF_CHIP], w_ref[0])
        for q in range(1, N_CHIP):
            dh = dh + _dot_nt(dff_ref[:, q * FF_CHIP:(q + 1) * FF_CHIP], w_ref[q])
        dx, t_sh, t_sc, t_g = _rms_mod_bwd(x1_ref[...], dh, g_ref[...], sc_ref[...])
        dx1 = dy_ref[...] + dx
        dx1_ref[...] = dx1
        dmo_ref[...] = (gt_ref[...] * dx1).astype(BF16)
        acc[0] += _fold8(t_sh)
        acc[1] += _fold8(t_sc)
        acc[2] += _fold8(t_g)
        acc[3] += _fold8(dx1 * mo_ref[...].astype(F32))

        @pl.when(i == n - 1)
        def _():
            sums_ref[...] = jnp.sum(acc[...], axis=1)

    return _row_call(body, name=name, row_ins=[dff, x1, dy, mo], res_ins=[w_ffn_in, g2, sc2, gt1],
                     row_outs=[(d, F32), (d, BF16)], acc_outs=[((4, d), F32)], scratch=[pltpu.VMEM((4, 8, d), F32)])


def _post_attn_bwd(dmo, gates, o_a, o_g, l_g, w_pa, w_pb, w_o, b_gate, *, name):
    d = D_MODEL

    def body(i, n, rin, res, rout, aout, scr):
        dmo_ref, gates_ref, oa_ref, o0, o1, o2, l0, l1, l2 = rin
        wpa_ref, wpb_ref, wo_ref, b_ref = res
        dpa_ref, dpb_ref, dgates_ref, doa_ref = rout[:4]
        do_refs, dl_refs = rout[4:7], rout[7:10]
        (dbg_ref,) = aout
        (acc,) = scr

        @pl.when(i == 0)
        def _():
            acc[...] = jnp.zeros_like(acc)

        ogs = [o0[...], o1[...], o2[...]]
        ws = _mix_weights([l0[...], l1[...], l2[...]])
        obb = (ws[0] * ogs[0] + ws[1] * ogs[1] + ws[2] * ogs[2]).astype(BF16)
        pa = _dot(oa_ref[...].astype(BF16), wpa_ref[...])
        pb = _dot(obb, wpb_ref[...])
        ga = _sigmoid(gates_ref[:, :d].astype(F32) + b_ref[:, :d])
        gb = _sigmoid(gates_ref[:, d:].astype(F32) + b_ref[:, d:])
        dm = _dot_nt(dmo_ref[...], wo_ref[...])
        dpa = (dm * ga).astype(BF16)
        dpb = (dm * gb).astype(BF16)
        dpa_ref[...] = dpa
        dpb_ref[...] = dpb
        dga = dm * pa * ga * (1.0 - ga)
        dgb = dm * pb * gb * (1.0 - gb)
        dgates_ref[:, :d] = dga.astype(BF16)
        dgates_ref[:, d:] = dgb.astype(BF16)
        acc[:, :d] += _fold8(dga)
        acc[:, d:] += _fold8(dgb)
        doa_ref[...] = _dot_nt(dpa, wpa_ref[...])
        dob = _dot_nt(dpb, wpb_ref[...])
        lo = _head_lanes()
        for ch in range(WB_OUT // LANES):
            sl = slice(ch * LANES, (ch + 1) * LANES)
            dv = dob[:, sl]
            wc = [w[:, sl] for w in ws]
            ts = [_head_mean(dv * og[:, sl], lo) * float(HEAD_DIM) for og in ogs]
            tbar = wc[0] * ts[0] + wc[1] * ts[1] + wc[2] * ts[2]
            for g in range(3):
                do_refs[g][:, sl] = wc[g] * dv
                dl_refs[g][:, sl] = wc[g] * (ts[g] - tbar)

        @pl.when(i == n - 1)
        def _():
            dbg_ref[...] = jnp.sum(acc[...], axis=0, keepdims=True)

    return _row_call(body, name=name, row_ins=[dmo, gates, o_a, *o_g, *l_g], res_ins=[w_pa, w_pb, w_o, b_gate],
                     row_outs=[(d, BF16), (d, BF16), (2 * d, BF16), (WA, F32)] + [(WB_OUT, F32)] * 6,
                     acc_outs=[((1, 2 * d), F32)], scratch=[pltpu.VMEM((8, 2 * d), F32)])


def _attn_fwd(qkv, qc0, kc0, vc0, npairs, table, kstart, cls, nk, *, name):
    s = qkv.shape[0]
    per_head = table.shape[1] > 1
    hb = 2 if per_head else 1

    def body(ks_ref, cls_ref, q_ref, k_ref, v_ref, b_ref, o_ref, lse_ref):
        i = pl.program_id(1)
        ks = pl.multiple_of(ks_ref[i], 64)
        q2 = q_ref[...]
        k2 = k_ref[pl.ds(ks, nk), :]
        v2 = v_ref[pl.ds(ks, nk), :]
        lo = _head_lanes()
        outs, lses = [], []
        for h in range(2):
            qm = jnp.where(lo if h == 0 else jnp.logical_not(lo), q2, jnp.zeros_like(q2))
            sc = _dot_nt(qm, k2) * SCALE + b_ref[0, h if per_head else 0]
            m = jnp.max(sc, axis=1, keepdims=True)
            p = jnp.exp(sc - m)
            l = jnp.sum(p, axis=1, keepdims=True)
            pv = _dot(p.astype(BF16), v2)
            outs.append(pv / l)
            lses.append(m + jnp.log(l))
        o_ref[...] = jnp.where(lo, outs[0], outs[1])
        lse_ref[...] = jnp.where(lo, lses[0], lses[1])

    w = npairs * LANES
    grid_spec = pltpu.PrefetchScalarGridSpec(
        num_scalar_prefetch=2, grid=(npairs, N_QBLK),
        in_specs=[
            pl.BlockSpec((Q_BLOCK, LANES), lambda p, i, ks, cl: (i, qc0 + p)),
            pl.BlockSpec((s, LANES), lambda p, i, ks, cl: (0, kc0 + p)),
            pl.BlockSpec((s, LANES), lambda p, i, ks, cl: (0, vc0 + p)),
            pl.BlockSpec((1, hb, Q_BLOCK, nk), lambda p, i, ks, cl: (cl[i], p if per_head else 0, 0, 0)),
        ],
        out_specs=[pl.BlockSpec((Q_BLOCK, LANES), lambda p, i, ks, cl: (i, p)),
                   pl.BlockSpec((Q_BLOCK, LANES), lambda p, i, ks, cl: (i, p))],
    )
    return _pcall(body, name=name, grid_spec=grid_spec,
                  out_shape=[jax.ShapeDtypeStruct((s, w), F32), jax.ShapeDtypeStruct((s, w), F32)],
                  compiler_params=pltpu.CompilerParams(dimension_semantics=("parallel", "arbitrary")),
                  )(kstart, cls, qkv, qkv, qkv, table)


def _attn_bwd(qkv, qc0, kc0, vc0, npairs, table, kstart, cls, nk, do, o, lse, *, name, dlse=None, want_dbias=False):
    s = qkv.shape[0]
    per_head = table.shape[1] > 1
    hb = 2 if per_head else 1
    has_dlse = dlse is not None

    def body(ks_ref, cls_ref, q_ref, k_ref, v_ref, b_ref, do_ref, o_ref, lse_ref, *rest):
        if has_dlse:
            dlse_ref, rest = rest[0], rest[1:]
        dq_ref, dk_ref, dv_ref = rest[0], rest[1], rest[2]
        db_ref = rest[3] if want_dbias else None
        i = pl.program_id(1)

        @pl.when(i == 0)
        def _():
            dk_ref[...] = jnp.zeros_like(dk_ref)
            dv_ref[...] = jnp.zeros_like(dv_ref)

        ks = pl.multiple_of(ks_ref[i], 64)
        q2 = q_ref[...]
        k2 = k_ref[pl.ds(ks, nk), :]
        v2 = v_ref[pl.ds(ks, nk), :]
        do2 = do_ref[...]
        lse2 = lse_ref[...]
        doo = do2 * o_ref[...]
        do2b = do2.astype(BF16)
        lo = _head_lanes()
        lane = lax.broadcasted_iota(jnp.int32, (1, LANES), 1)
        if want_dbias:
            first = jnp.logical_or(i == 0, cls_ref[i] != cls_ref[jnp.maximum(i - 1, 0)])
        dqs, dks, dvs = [], [], []
        for h in range(2):
            mh = lo if h == 0 else jnp.logical_not(lo)
            qm = jnp.where(mh, q2, jnp.zeros_like(q2))
            sc = _dot_nt(qm, k2) * SCALE + b_ref[0, h if per_head else 0]
            lse_h = jnp.max(jnp.where(mh, lse2, NEG), axis=1, keepdims=True)
            p = jnp.exp(sc - lse_h)
            delta = jnp.sum(jnp.where(mh, doo, 0.0), axis=1, keepdims=True)
            dom = jnp.where(mh, do2b, jnp.zeros_like(do2b))
            dp = _dot_nt(dom, v2)
            t = dp - delta
            if has_dlse:
                t = t + jnp.sum(jnp.where(lane == h * HEAD_DIM, dlse_ref[...], 0.0), axis=1, keepdims=True)
            ds = p * t
            if want_dbias:
                @pl.when(first)
                def _():
                    db_ref[0, h] = ds

                @pl.when(jnp.logical_not(first))
                def _():
                    db_ref[0, h] += ds
            dsb = ds.astype(BF16)
            dqs.append(_dot(dsb, k2))
            dks.append(lax.dot_general(dsb, q2, _TN, preferred_element_type=F32))
            dvs.append(lax.dot_general(p.astype(BF16), do2b, _TN, preferred_element_type=F32))
        dq_ref[...] = jnp.where(lo, dqs[0], dqs[1]) * SCALE
        dk_ref[pl.ds(ks, nk), :] += jnp.where(lo, dks[0], dks[1]) * SCALE
        dv_ref[pl.ds(ks, nk), :] += jnp.where(lo, dvs[0], dvs[1])

    w = npairs * LANES
    blk = lambda: pl.BlockSpec((Q_BLOCK, LANES), lambda p, i, ks, cl: (i, p))
    full = lambda: pl.BlockSpec((s, LANES), lambda p, i, ks, cl: (0, p))
    tab = lambda: pl.BlockSpec((1, hb, Q_BLOCK, nk), lambda p, i, ks, cl: (cl[i], p if per_head else 0, 0, 0))
    in_specs = [
        pl.BlockSpec((Q_BLOCK, LANES), lambda p, i, ks, cl: (i, qc0 + p)),
        pl.BlockSpec((s, LANES), lambda p, i, ks, cl: (0, kc0 + p)),
        pl.BlockSpec((s, LANES), lambda p, i, ks, cl: (0, vc0 + p)),
        tab(), blk(), blk(), blk(),
    ]
    args = [kstart, cls, qkv, qkv, qkv, table, do, o, lse]
    if has_dlse:
        in_specs.append(blk())
        args.append(dlse)
    out_specs = [blk(), full(), full()]
    out_shape = [jax.ShapeDtypeStruct((s, w), F32)] * 3
    if want_dbias:
        assert per_head
        out_specs.append(tab())
        out_shape.append(jax.ShapeDtypeStruct(table.shape, F32))
    grid_spec = pltpu.PrefetchScalarGridSpec(num_scalar_prefetch=2, grid=(npairs, N_QBLK), in_specs=in_specs,
                                             out_specs=out_specs)
    return _pcall(body, name=name, grid_spec=grid_spec, out_shape=out_shape,
                  compiler_params=pltpu.CompilerParams(dimension_semantics=("arbitrary", "arbitrary")))(*args)


_NA_CLASS_R0 = (0, NA_QROWS, ROWS - NA_QROWS)
_NA_CLASS_K0 = (0, 0, ROWS - NA_KROWS)
_RPB_ROWS = 3 * NA_QROWS * NA_KROWS
_RPB_ROWS_PAD = 136
_RPB_RO = 2 * NA_KH - 1
_RPB_CO = 2 * NA_KW - 1


def _na_constants():
    a = np.arange(NA_QROWS)
    b = np.arange(NA_KROWS)
    col = np.arange(GRID_W)
    oh_row = np.zeros((_RPB_ROWS_PAD, 16), np.float32)
    vrow = np.zeros((3, NA_QROWS, NA_KROWS), bool)
    for t in range(3):
        qr = _NA_CLASS_R0[t] + a
        kr = _NA_CLASS_K0[t] + b
        rs = np.clip(qr - NA_KH // 2, 0, ROWS - NA_KH)
        vrow[t] = (kr[None, :] >= rs[:, None]) & (kr[None, :] < rs[:, None] + NA_KH)
        ro = kr[None, :] - qr[:, None] + (NA_KH - 1)
        for ai in range(NA_QROWS):
            for bi in range(NA_KROWS):
                if vrow[t, ai, bi]:
                    oh_row[(t * NA_QROWS + ai) * NA_KROWS + bi, ro[ai, bi]] = 1.0
    cs = np.clip(col - NA_KW // 2, 0, GRID_W - NA_KW)
    vcol = (col[None, :] >= cs[:, None]) & (col[None, :] < cs[:, None] + NA_KW)
    co = col[None, :] - col[:, None] + (NA_KW - 1)
    oh_col = np.zeros((GRID_W * GRID_W, LANES), np.float32)
    for qc in range(GRID_W):
        for kc in range(GRID_W):
            if vcol[qc, kc]:
                oh_col[qc * GRID_W + kc, co[qc, kc]] = 1.0
    valid = vrow[:, :, None, :, None] & vcol[None, None, :, None, :]
    mask = np.where(valid, 0.0, NEG).astype(np.float32).reshape(3, 1, Q_BLOCK, NA_NK)
    ks = np.clip(np.arange(N_QBLK) * NA_QROWS - NA_KH // 2, 0, ROWS - NA_KROWS) * GRID_W
    cls = np.ones(N_QBLK, np.int32)
    cls[0], cls[-1] = 0, 2
    return oh_row, oh_col, mask, ks.astype(np.int32), cls


def _rpb_expand(rpb_pad, oh_row, oh_col_t, *, name):
    def body(r_ref, ohr_ref, ohc_ref, o_ref):
        t = jnp.dot(ohr_ref[...], r_ref[0], preferred_element_type=F32, precision=lax.Precision.HIGHEST)
        o_ref[0] = jnp.dot(t, ohc_ref[...], preferred_element_type=F32, precision=lax.Precision.HIGHEST)

    return _pcall(
        body, name=name, grid=(NA_HEADS,),
        in_specs=[pl.BlockSpec((1, 16, LANES), lambda h: (h, 0, 0)), pl.BlockSpec((_RPB_ROWS_PAD, 16), lambda h: (0, 0)),
                  pl.BlockSpec((LANES, GRID_W * GRID_W), lambda h: (0, 0))],
        out_specs=pl.BlockSpec((1, _RPB_ROWS_PAD, GRID_W * GRID_W), lambda h: (h, 0, 0)),
        out_shape=jax.ShapeDtypeStruct((NA_HEADS, _RPB_ROWS_PAD, GRID_W * GRID_W), F32), compiler_params=_PAR,
    )(rpb_pad, oh_row, oh_col_t)


def _rpb_reduce(dx, oh_row_t, oh_col, *, name):
    def body(d_ref, ohr_ref, ohc_ref, o_ref):
        t = jnp.dot(d_ref[0], ohc_ref[...], preferred_element_type=F32, precision=lax.Precision.HIGHEST)
        o_ref[0] = jnp.dot(ohr_ref[...], t, preferred_element_type=F32, precision=lax.Precision.HIGHEST)

    return _pcall(
        body, name=name, grid=(NA_HEADS,),
        in_specs=[pl.BlockSpec((1, _RPB_ROWS_PAD, GRID_W * GRID_W), lambda h: (h, 0, 0)),
                  pl.BlockSpec((16, _RPB_ROWS_PAD), lambda h: (0, 0)),
                  pl.BlockSpec((GRID_W * GRID_W, LANES), lambda h: (0, 0))],
        out_specs=pl.BlockSpec((1, 16, LANES), lambda h: (h, 0, 0)),
        out_shape=jax.ShapeDtypeStruct((NA_HEADS, 16, LANES), F32), compiler_params=_PAR,
    )(dx, oh_row_t, oh_col)


def _dil_constants(dilation):
    seg = SEQ // dilation
    nb = seg // Q_BLOCK
    shift = (0, -DIL_HALF, -2 * DIL_HALF)
    qi = np.arange(Q_BLOCK)[:, None]
    ki = np.arange(DIL_NK)[None, :]
    mask = np.stack([np.where(np.abs(ki + sh - qi) <= DIL_HALF, 0.0, NEG) for sh in shift]).astype(np.float32)
    ks, cls = [], []
    for i in range(N_QBLK):
        sub, blk = divmod(i, nb)
        t = 0 if blk == 0 else (2 if blk == nb - 1 else 1)
        cls.append(t)
        ks.append(sub * seg + blk * Q_BLOCK + shift[t])
    return mask.reshape(3, 1, Q_BLOCK, DIL_NK), np.asarray(ks, np.int32), np.asarray(cls, np.int32)


_VM = pl.BlockSpec(memory_space=pltpu.VMEM)


def _ada_fwd(c_all, w, b, *, name):
    def body(c_ref, w_ref, b_ref, o_ref):
        cv = c_ref[...]
        o_ref[...] = jnp.dot(cv * _sigmoid(cv), w_ref[...], preferred_element_type=F32,
                             precision=lax.Precision.HIGHEST) + b_ref[...]

    return _pcall(body, name=name, in_specs=[_VM, _VM, _VM], out_specs=_VM,
                  out_shape=jax.ShapeDtypeStruct((c_all.shape[0], w.shape[1]), F32))(c_all, w, b)


def _ada_bwd(c_all_t, dmod, *, name):
    def body(c_ref, d_ref, o_ref):
        cv = c_ref[...]
        o_ref[...] = jnp.dot(cv * _sigmoid(cv), d_ref[...], preferred_element_type=F32,
                             precision=lax.Precision.HIGHEST)

    return _pcall(body, name=name, in_specs=[_VM, _VM], out_specs=_VM,
                  out_shape=jax.ShapeDtypeStruct((c_all_t.shape[0], dmod.shape[1]), F32))(c_all_t, dmod)


def _row_sum(t, *, name):
    def body(t_ref, o_ref):
        o_ref[...] = jnp.sum(t_ref[...], axis=0, keepdims=True)

    return _pcall(body, name=name, in_specs=[_VM], out_specs=_VM,
                  out_shape=jax.ShapeDtypeStruct((1, t.shape[1]), F32))(t)


def _row_tile(rows):
    tr = rows
    for cand in range(8, 513, 8):
        if rows % cand == 0:
            tr = cand
    return tr


def _adamw_math(wv, gv, mv, vv):
    nm = ADAM_B1 * mv + (1.0 - ADAM_B1) * gv
    nv = ADAM_B2 * vv + (1.0 - ADAM_B2) * (gv * gv)
    m_hat = nm / (1.0 - ADAM_B1 ** ADAM_STEP)
    v_hat = nv / (1.0 - ADAM_B2 ** ADAM_STEP)
    return -ADAM_LR * (m_hat / (jnp.sqrt(v_hat) + ADAM_EPS) + ADAM_WD * wv), nm, nv


def _adamw(w, g, m, v, *, name):
    rows, cols = w.shape
    tr = _row_tile(rows)

    def body(w_ref, g_ref, m_ref, v_ref, d_ref, nm_ref, nv_ref):
        d_ref[...], nm_ref[...], nv_ref[...] = _adamw_math(w_ref[...], g_ref[...], m_ref[...], v_ref[...])

    spec = pl.BlockSpec((tr, cols), lambda i: (i, 0))
    return _pcall(body, name=name, grid=(rows // tr,), in_specs=[spec] * 4, out_specs=[spec] * 3,
                  out_shape=[jax.ShapeDtypeStruct((rows, cols), F32)] * 3, compiler_params=_PAR)(w, g, m, v)


def _adamw_halves(w, g_mine, g_other, m, v, c_arr, *, name):
    rows, cols = w.shape
    hr = rows // 2
    tr = _row_tile(hr)
    nt = hr // tr

    def body(c_ref, w_ref, t_ref, o_ref, m_ref, v_ref, g_ref, d_ref, nm_ref, nv_ref):
        gv = jnp.where(pl.program_id(0) == c_ref[0], t_ref[...], o_ref[...])
        g_ref[...] = gv
        d_ref[...], nm_ref[...], nv_ref[...] = _adamw_math(w_ref[...], gv, m_ref[...], v_ref[...])

    full = pl.BlockSpec((tr, cols), lambda h, i, c: (h * nt + i, 0))
    half = pl.BlockSpec((tr, cols), lambda h, i, c: (i, 0))
    grid_spec = pltpu.PrefetchScalarGridSpec(num_scalar_prefetch=1, grid=(2, nt),
                                             in_specs=[full, half, half, full, full], out_specs=[full] * 4)
    return _pcall(body, name=name, grid_spec=grid_spec, out_shape=[jax.ShapeDtypeStruct((rows, cols), F32)] * 4,
                  compiler_params=pltpu.CompilerParams(dimension_semantics=("parallel", "parallel")),
                  )(c_arr, w, g_mine, g_other, m, v)


_MESH = pl.DeviceIdType.MESH
_ANY = pl.BlockSpec(memory_space=pl.ANY)
_CHIP_FLIPS = ((1, 0), (0, 1), (1, 1))


def _pos():
    return lax.axis_index("x"), lax.axis_index("y"), lax.axis_index("c")


def _flip(v, f):
    return 1 - v if f else v


def _sem_pairs(n):
    return [pltpu.SemaphoreType.DMA((n,)), pltpu.SemaphoreType.DMA((n,))]


def _small_allgather(blk, *, name):
    m_per, n = blk.shape

    def body(x_ref, out_ref, send_sems, recv_sems, local_sem):
        x, y, c = _pos()
        me, sibling = (x, y, c), (x, y, 1 - c)
        chips = [(_flip(x, fx), _flip(y, fy)) for fx, fy in _CHIP_FLIPS]

        def rows(px, py, pc):
            return out_ref.at[pl.ds((4 * px + 2 * py + pc) * m_per, m_per), :]

        def copy(k, block, to, src=None):
            return pltpu.make_async_remote_copy(
                src_ref=rows(*block) if src is None else src, dst_ref=rows(*block),
                send_sem=send_sems.at[k], recv_sem=recv_sems.at[k], device_id=to, device_id_type=_MESH)

        mine = pltpu.make_async_copy(x_ref, rows(*me), local_sem)
        mine.start()
        first = [copy(0, me, sibling, src=x_ref)]
        first += [copy(1 + j, me, (*chip, c), src=x_ref) for j, chip in enumerate(chips)]
        for cp in first:
            cp.start()
        passed = [copy(4 + j, (*chip, c), sibling) for j, chip in enumerate(chips)]
        for j, chip in enumerate(chips):
            copy(1 + j, (*chip, c), me).wait_recv()
            passed[j].start()
        copy(0, sibling, me).wait_recv()
        for j, chip in enumerate(chips):
            copy(4 + j, (*chip, 1 - c), me).wait_recv()
        for cp in first + passed:
            cp.wait_send()
        mine.wait()

    return _pcall(
        body, name=name, out_shape=jax.ShapeDtypeStruct((N_DEV * m_per, n), blk.dtype),
        in_specs=[_VM], out_specs=_VM,
        scratch_shapes=_sem_pairs(7) + [pltpu.SemaphoreType.DMA],
    )(blk)


def _allgather_weights(shards, *, name):
    nw = len(shards)

    def body(*refs):
        sh_refs, full_refs = refs[:nw], refs[nw:2 * nw]
        send_sems, recv_sems = refs[2 * nw:]
        x, y, c = _pos()
        j = 2 * x + y
        chips = [(_flip(x, fx), _flip(y, fy)) for fx, fy in _CHIP_FLIPS]

        def copy(k, src, dst, to):
            return pltpu.make_async_remote_copy(src_ref=src, dst_ref=dst, send_sem=send_sems.at[k],
                                                recv_sem=recv_sems.at[k], device_id=to, device_id_type=_MESH)

        first = [copy(6 * wi + k, sh_refs[wi].at[c], full_refs[wi].at[j, c], (px, py, c))
                 for wi in range(nw) for k, (px, py) in enumerate(chips)]
        own = [copy(6 * nw + wi, sh_refs[wi], full_refs[wi].at[j], (x, y, 1 - c)) for wi in range(nw)]
        for cp in first + own:
            cp.start()
        passed = []
        for wi in range(nw):
            for k, (px, py) in enumerate(chips):
                jp = 2 * px + py
                copy(6 * wi + k, sh_refs[wi].at[c], full_refs[wi].at[jp, c], (x, y, c)).wait_recv()
                cp = copy(6 * wi + 3 + k, full_refs[wi].at[jp, c], full_refs[wi].at[jp, c], (x, y, 1 - c))
                cp.start()
                passed.append(cp)
        for wi in range(nw):
            for k, (px, py) in enumerate(chips):
                jp = 2 * px + py
                copy(6 * wi + 3 + k, sh_refs[wi].at[c], full_refs[wi].at[jp, 1 - c], (x, y, c)).wait_recv()
        for cp in first + passed:
            cp.wait_send()
        for cp in own:
            cp.wait()

    return _pcall(
        body, name=name, out_shape=[jax.ShapeDtypeStruct((N_CHIP,) + s.shape, s.dtype) for s in shards],
        in_specs=[_ANY] * nw, out_specs=[_ANY] * nw, scratch_shapes=_sem_pairs(7 * nw),
    )(*shards)


def _sibling_send_halves(gs, *, name):
    nw = len(gs)

    def body(*refs):
        g_refs, out_refs = refs[:nw], refs[nw:2 * nw]
        send_sems, recv_sems = refs[2 * nw:]
        x, y, c = _pos()
        cps = [pltpu.make_async_remote_copy(src_ref=g_refs[wi].at[k, 1 - c], dst_ref=out_refs[wi].at[k],
                                            send_sem=send_sems.at[4 * wi + k], recv_sem=recv_sems.at[4 * wi + k],
                                            device_id=(x, y, 1 - c), device_id_type=_MESH)
               for wi in range(nw) for k in range(N_CHIP)]
        for cp in cps:
            cp.start()
        for cp in cps:
            cp.wait()

    return _pcall(body, name=name,
                  out_shape=[jax.ShapeDtypeStruct((g.shape[0],) + g.shape[2:], g.dtype) for g in gs],
                  in_specs=[_ANY] * nw, out_specs=[_ANY] * nw, scratch_shapes=_sem_pairs(N_CHIP * nw))(*gs)


def _chip_exchange(sbs, *, name):
    nw = len(sbs)

    def body(*refs):
        s_refs, out_refs = refs[:nw], refs[nw:2 * nw]
        send_sems, recv_sems = refs[2 * nw:]
        x, y, c = _pos()
        cps = []
        for wi in range(nw):
            for k, (fx, fy) in enumerate(_CHIP_FLIPS):
                px, py = _flip(x, fx), _flip(y, fy)
                cps.append(pltpu.make_async_remote_copy(
                    src_ref=s_refs[wi].at[2 * px + py], dst_ref=out_refs[wi].at[k], send_sem=send_sems.at[3 * wi + k],
                    recv_sem=recv_sems.at[3 * wi + k], device_id=(px, py, c), device_id_type=_MESH))
        for cp in cps:
            cp.start()
        for cp in cps:
            cp.wait()

    return _pcall(body, name=name, out_shape=[jax.ShapeDtypeStruct((3,) + s.shape[1:], s.dtype) for s in sbs],
                  in_specs=[_ANY] * nw, out_specs=[_ANY] * nw, scratch_shapes=_sem_pairs(3 * nw))(*sbs)


def _sibling_swap(ts, *, name):
    nw = len(ts)

    def body(*refs):
        t_refs, out_refs = refs[:nw], refs[nw:2 * nw]
        send_sems, recv_sems = refs[2 * nw:]
        x, y, c = _pos()
        cps = [pltpu.make_async_remote_copy(src_ref=t_refs[wi], dst_ref=out_refs[wi], send_sem=send_sems.at[wi],
                                            recv_sem=recv_sems.at[wi], device_id=(x, y, 1 - c), device_id_type=_MESH)
               for wi in range(nw)]
        for cp in cps:
            cp.start()
        for cp in cps:
            cp.wait()

    return _pcall(body, name=name, out_shape=[jax.ShapeDtypeStruct(t.shape, t.dtype) for t in ts],
                  in_specs=[_ANY] * nw, out_specs=[_ANY] * nw, scratch_shapes=_sem_pairs(nw))(*ts)


def _rs_add(g, ra, c_arr, *, name):
    n, _, r, w = g.shape

    def body(c_ref, g_ref, ra_ref, s_ref, sb_ref):
        t = g_ref[...] + ra_ref[...]
        s_ref[...] = t
        sb_ref[...] = t.astype(BF16)

    grid_spec = pltpu.PrefetchScalarGridSpec(
        num_scalar_prefetch=1, grid=(n,),
        in_specs=[pl.BlockSpec((None, None, r, w), lambda k, c: (k, c[0], 0, 0)),
                  pl.BlockSpec((None, r, w), lambda k, c: (k, 0, 0))],
        out_specs=[pl.BlockSpec((None, r, w), lambda k, c: (k, 0, 0))] * 2)
    return _pcall(body, name=name, grid_spec=grid_spec,
                  out_shape=[jax.ShapeDtypeStruct((n, r, w), F32), jax.ShapeDtypeStruct((n, r, w), BF16)],
                  compiler_params=_PAR)(c_arr, g, ra)


def _rs_final(s, rb, j_arr, *, name):
    _, r, w = s.shape

    def body(j_ref, s_ref, rb_ref, t_ref):
        t_ref[...] = ((s_ref[...] + rb_ref[0].astype(F32)) + rb_ref[1].astype(F32)) + rb_ref[2].astype(F32)

    grid_spec = pltpu.PrefetchScalarGridSpec(
        num_scalar_prefetch=1, grid=(1,),
        in_specs=[pl.BlockSpec((None, r, w), lambda i, j: (j[0], 0, 0)),
                  pl.BlockSpec((3, r, w), lambda i, j: (0, 0, 0))],
        out_specs=pl.BlockSpec((r, w), lambda i, j: (0, 0)))
    return _pcall(body, name=name, grid_spec=grid_spec, out_shape=jax.ShapeDtypeStruct((r, w), F32),
                  compiler_params=_ARB)(j_arr, s, rb)


def _perm_rows(t, d):
    s, w = t.shape
    return t.reshape(s // d, d, w).transpose(1, 0, 2).reshape(s, w)


def _unperm_rows(t, d):
    s, w = t.shape
    return t.reshape(d, s // d, w).transpose(1, 0, 2).reshape(s, w)


def _tile2(g):
    return jnp.concatenate([g, g], axis=1)


_BIG = ("w_in", "w_ffn_in", "w_ffn_out", "w_o", "w_proj_a", "w_proj_b")
_BIG_SHARD = {"w_in": (1024, 1472), "w_ffn_in": (1024, 1408), "w_ffn_out": (704, 1024), "w_o": (256, 1024),
              "w_proj_a": (512, 256), "w_proj_b": (256, 256)}


def _device_step(x2, tgt, mod, wts, g_norm1, g_norm2, b_gate, g_qa, g_ka, g_qb, g_kb, rpb):
    d = D_MODEL
    sh1, sc1, gt1, sh2, sc2, gt2 = [mod[:, k * d:(k + 1) * d] for k in range(6)]

    oh_row, oh_col, na_mask, na_ks, na_cls = _na_constants()
    rpb_pad = jnp.pad(rpb, ((0, 0), (0, 16 - _RPB_RO), (0, LANES - _RPB_CO)))
    tab = _rpb_expand(rpb_pad, jnp.asarray(oh_row), jnp.asarray(oh_col.T.copy()), name="rpb_expand")
    tab = tab[:, :_RPB_ROWS].reshape(NA_HEADS, 3, NA_QROWS, NA_KROWS, GRID_W, GRID_W)
    tab_a = tab.transpose(1, 0, 2, 4, 3, 5).reshape(3, NA_HEADS, Q_BLOCK, NA_NK) + jnp.asarray(na_mask)
    na_ks, na_cls = jnp.asarray(na_ks), jnp.asarray(na_cls)
    dil = [_dil_constants(dd) for _, dd in DIL_CONFIGS]
    tab_d = jnp.asarray(dil[0][0])
    gains = jnp.concatenate([_tile2(g_qa), _tile2(g_ka), _tile2(g_qb), _tile2(g_kb)], axis=0)
    cos_t, sa_t, sb_t = _rope_tables()

    h1, qkvn, qk_pre, gates = _pre_attn_fwd(x2, cos_t, sa_t, sb_t, g_norm1, sc1, sh1, wts["w_qkv"], wts["w_gates"],
                                            gains, name="pre_attn_fwd")
    o_a, lse_a = _attn_fwd(qkvn, 0, 4, 8, 4, tab_a, na_ks, na_cls, NA_NK, name="attn_a_fwd")
    arrs, o_p, l_p, o_g, l_g = [], [], [], [], []
    for g, (_, dd) in enumerate(DIL_CONFIGS):
        ks_g, cls_g = jnp.asarray(dil[g][1]), jnp.asarray(dil[g][2])
        if dd == 1:
            arr, cb = qkvn, (12, 18, 24)
        else:
            col = lambda base: qkvn[:, base + WB_OUT * g: base + WB_OUT * (g + 1)]
            arr = _perm_rows(jnp.concatenate([col(3 * WA), col(3 * WA + WB), col(3 * WA + 2 * WB)], axis=1), dd)
            cb = (0, 2, 4)
        op, lp = _attn_fwd(arr, cb[0], cb[1], cb[2], 2, tab_d, ks_g, cls_g, DIL_NK, name=f"attn_d{g}_fwd")
        arrs.append((arr, cb, ks_g, cls_g))
        o_p.append(op)
        l_p.append(lp)
        o_g.append(op if dd == 1 else _unperm_rows(op, dd))
        l_g.append(lp if dd == 1 else _unperm_rows(lp, dd))
    o_b, merged, mo, x1, h2 = _post_attn_fwd(o_a, o_g, l_g, gates, x2, wts["w_pa"], wts["w_pb"], wts["w_o"], b_gate,
                                             gt1, g_norm2, sc2, sh2, name="post_attn_fwd")
    act, ff = _ffn_fwd(h2, wts["w_ffn_in"], name="ffn_fwd")

    dy, dffo, dff, dgt2, loss_v = _ffn_mid(act, ff, x1, tgt, wts["w_ffn_out"], gt2, name="ffn_mid")
    grads = {}
    g_ffn_out = _wgrad(act, dffo, name="wg_ffn_out", tm=D_FF // 2, tn=d)
    grads["w_ffn_out"] = g_ffn_out.reshape(N_CHIP, D_FF // N_CHIP, d)
    grads["w_ffn_in"] = _wgrad(h2, dff, name="wg_ffn_in", tm=512, tn=2 * FF_CHIP, chips=N_CHIP)
    dx1, dmo, sums2 = _ffn_in_bwd(dff, x1, dy, mo, wts["w_ffn_in"], g_norm2, sc2, gt1, name="ffn_in_bwd")
    grads["w_o"] = _wgrad(merged, dmo, name="wg_o", tm=d, tn=d).reshape(N_CHIP, d // N_CHIP, d)
    pab = _post_attn_bwd(dmo, gates, o_a, o_g, l_g, wts["w_pa"], wts["w_pb"], wts["w_o"], b_gate, name="post_attn_bwd")
    dpa, dpb, dgates, do_a = pab[:4]
    do_g, dl_g, dbg = pab[4:7], pab[7:10], pab[10]
    g_pa = _wgrad(o_a, dpa, name="wg_pa", tm=WA, tn=d)
    g_pb = _wgrad(o_b, dpb, name="wg_pb", tm=WB_OUT, tn=d)
    grads["w_proj_a"] = g_pa.reshape(WA, N_CHIP, d // N_CHIP).transpose(1, 0, 2)
    grads["w_proj_b"] = g_pb.reshape(WB_OUT, N_CHIP, d // N_CHIP).transpose(1, 0, 2)
    dqa, dka, dva, dtab = _attn_bwd(qkvn, 0, 4, 8, 4, tab_a, na_ks, na_cls, NA_NK, do_a, o_a, lse_a,
                                    name="attn_a_bwd", want_dbias=True)
    dqs, dks, dvs = [], [], []
    for g, (_, dd) in enumerate(DIL_CONFIGS):
        arr, cb, ks_g, cls_g = arrs[g]
        dog = do_g[g] if dd == 1 else _perm_rows(do_g[g], dd)
        dlg = dl_g[g] if dd == 1 else _perm_rows(dl_g[g], dd)
        dq, dk, dv = _attn_bwd(arr, cb[0], cb[1], cb[2], 2, tab_d, ks_g, cls_g, DIL_NK, dog, o_p[g], l_p[g],
                               name=f"attn_d{g}_bwd", dlse=dlg)
        if dd != 1:
            dq, dk, dv = _unperm_rows(dq, dd), _unperm_rows(dk, dd), _unperm_rows(dv, dd)
        dqs.append(dq)
        dks.append(dk)
        dvs.append(dv)
    dqkv, grad_x, dgains, sums1 = _pre_attn_bwd(qk_pre, [dqa, dka, dva] + dqs + dks + dvs, dgates, x2, dx1, cos_t, sa_t,
                                                sb_t, wts["w_qkv"], wts["w_gates"], gains, g_norm1, sc1,
                                                name="pre_attn_bwd")
    g_qkv = _wgrad(h1, dqkv, name="wg_qkv", tm=d, tn=W_QKV // 2)
    g_gates = _wgrad(h1, dgates, name="wg_gates", tm=d, tn=W_GATES)
    nc, cut = _BIG_SHARD["w_in"][1], 3 * _BIG_SHARD["w_in"][1] - W_QKV
    grads["w_in"] = jnp.stack([g_qkv[:, :nc], g_qkv[:, nc:2 * nc],
                               jnp.concatenate([g_qkv[:, 2 * nc:], g_gates[:, :cut]], axis=1), g_gates[:, cut:]])

    dtab = dtab.reshape(3, NA_HEADS, NA_QROWS, GRID_W, NA_KROWS, GRID_W).transpose(1, 0, 2, 4, 3, 5)
    dtab = jnp.pad(dtab.reshape(NA_HEADS, _RPB_ROWS, GRID_W * GRID_W), ((0, 0), (0, _RPB_ROWS_PAD - _RPB_ROWS), (0, 0)))
    g_rpb = _rpb_reduce(dtab, jnp.asarray(oh_row.T.copy()), jnp.asarray(oh_col), name="rpb_reduce")
    g_rpb = g_rpb[:, :_RPB_RO, :_RPB_CO]

    dmod = jnp.concatenate([sums1[0:1], sums1[1:2], sums2[3:4], sums2[0:1], sums2[1:2], dgt2], axis=1)
    small = dict(g_norm1=sums1[2:3], g_norm2=sums2[2:3], b_gate=dbg, g_qa=dgains[0:1, :HEAD_DIM],
                 g_ka=dgains[1:2, :HEAD_DIM], g_qb=dgains[2:3, :HEAD_DIM], g_kb=dgains[3:4, :HEAD_DIM], rpb=g_rpb)
    return loss_v, grad_x, grads, dmod, small


_SMALL = ("b_ada", "g_norm1", "g_norm2", "b_gate", "g_qa", "g_ka", "g_qb", "g_kb", "rpb")
_SMALL_N = {"b_ada": 6 * D_MODEL, "g_norm1": D_MODEL, "g_norm2": D_MODEL, "b_gate": 2 * D_MODEL, "g_qa": HEAD_DIM,
            "g_ka": HEAD_DIM, "g_qb": HEAD_DIM, "g_kb": HEAD_DIM, "rpb": NA_HEADS * _RPB_RO * _RPB_CO}


def _pack_small(parts):
    flat = [parts[n].reshape(1, _SMALL_N[n]) for n in _SMALL]
    used = sum(_SMALL_N.values())
    return jnp.concatenate(flat + [jnp.zeros((1, STATS_W - used), F32)], axis=1)


def _unpack_small(v, shapes):
    out, at = {}, 0
    for n in _SMALL:
        out[n] = v[:, at:at + _SMALL_N[n]].reshape(shapes[n])
        at += _SMALL_N[n]
    return out


def _join_cols(t):
    _, r, c = t.shape
    return t.transpose(1, 0, 2).reshape(r, N_CHIP * c)


def kernel(x, c, w_ada, b_ada, g_norm1, g_norm2, w_in, b_gate, g_qa, g_ka, g_qb, g_kb, rpb, w_proj_a, w_proj_b, w_o, w_ffn_in, w_ffn_out, loss_target, m_w_ada, m_b_ada, m_g_norm1, m_g_norm2, m_w_in, m_b_gate, m_g_qa, m_g_ka, m_g_qb, m_g_kb, m_rpb, m_w_proj_a, m_w_proj_b, m_w_o, m_w_ffn_in, m_w_ffn_out, v_w_ada, v_b_ada, v_g_norm1, v_g_norm2, v_w_in, v_b_gate, v_g_qa, v_g_ka, v_g_qb, v_g_kb, v_rpb, v_w_proj_a, v_w_proj_b, v_w_o, v_w_ffn_in, v_w_ffn_out):
    names = ("w_ada", "b_ada", "g_norm1", "g_norm2", "w_in", "b_gate", "g_qa", "g_ka", "g_qb", "g_kb", "rpb",
             "w_proj_a", "w_proj_b", "w_o", "w_ffn_in", "w_ffn_out")
    w = dict(zip(names, (w_ada, b_ada, g_norm1, g_norm2, w_in, b_gate, g_qa, g_ka, g_qb, g_kb, rpb, w_proj_a, w_proj_b,
                         w_o, w_ffn_in, w_ffn_out)))
    m = dict(zip(names, (m_w_ada, m_b_ada, m_g_norm1, m_g_norm2, m_w_in, m_b_gate, m_g_qa, m_g_ka, m_g_qb, m_g_kb, m_rpb,
                         m_w_proj_a, m_w_proj_b, m_w_o, m_w_ffn_in, m_w_ffn_out)))
    v = dict(zip(names, (v_w_ada, v_b_ada, v_g_norm1, v_g_norm2, v_w_in, v_b_gate, v_g_qa, v_g_ka, v_g_qb, v_g_kb, v_rpb,
                         v_w_proj_a, v_w_proj_b, v_w_o, v_w_ffn_in, v_w_ffn_out)))
    d = D_MODEL
    xi, yi, ci = _pos()
    chip = 2 * xi + yi
    me = 2 * chip + ci
    ada_cols = 6 * d // N_CHIP

    c_all = _small_allgather(jnp.broadcast_to(c, (8, d)), name="ag_c")[::8]
    b_sh = lax.dynamic_slice(b_ada, (0, chip * ada_cols), (1, ada_cols))
    mod_part = _ada_fwd(c_all, w_ada[0], b_sh, name="ada_fwd")
    mod_all = _small_allgather(mod_part, name="ag_mod").reshape(N_CHIP, 2, 8, ada_cols)[:, 0]
    mod = lax.dynamic_index_in_dim(mod_all, me, axis=1, keepdims=False).reshape(1, 6 * d)

    halves = {n: (2, _BIG_SHARD[n][0] // 2, _BIG_SHARD[n][1]) for n in _BIG}
    shards = [w[n][0].astype(BF16).reshape(halves[n]) for n in _BIG]
    gathered = _allgather_weights(shards, name="ag_weights")
    full = {n: ga.reshape((N_CHIP,) + _BIG_SHARD[n]) for n, ga in zip(_BIG, gathered)}
    p_in = full["w_in"]
    cut = W_QKV - 2 * _BIG_SHARD["w_in"][1]
    wts = dict(w_qkv=jnp.concatenate([p_in[0], p_in[1], p_in[2][:, :cut]], axis=1),
               w_gates=jnp.concatenate([p_in[2][:, cut:], p_in[3]], axis=1),
               w_pa=_join_cols(full["w_proj_a"]), w_pb=_join_cols(full["w_proj_b"]), w_o=full["w_o"].reshape(d, d),
               w_ffn_in=full["w_ffn_in"], w_ffn_out=full["w_ffn_out"].reshape(D_FF, d))

    loss_v, grad_x, grads, dmod, small = _device_step(
        x[0], loss_target[0], mod, wts, g_norm1, g_norm2, b_gate, g_qa, g_ka, g_qb, g_kb, rpb[0])

    c_arr, chip_arr = ci.reshape(1).astype(jnp.int32), chip.reshape(1).astype(jnp.int32)
    gps = [grads[n].reshape((N_CHIP,) + halves[n]) for n in _BIG]
    ras = _sibling_send_halves(gps, name="rs_sibling")
    sums = [_rs_add(gp, ra, c_arr, name=f"rs_add_{n}") for n, gp, ra in zip(_BIG, gps, ras)]
    rbs = _chip_exchange([sb for _, sb in sums], name="rs_chips")
    ts = [_rs_final(sf, rb, chip_arr, name=f"rs_final_{n}") for n, (sf, _), rb in zip(_BIG, sums, rbs)]
    others = _sibling_swap(ts, name="rs_pair")

    stats = _pack_small(dict(b_ada=dmod, **small))
    rows = _small_allgather(jnp.broadcast_to(stats, (8, STATS_W)), name="ag_stats")[::8]
    dmod_sh = lax.dynamic_slice(rows, (0, chip * ada_cols), (8, ada_cols))
    g_ada = _ada_bwd(c_all.T, dmod_sh, name="ada_bwd")
    tot = _row_sum(rows, name="stats_sum")
    g_small = _unpack_small(tot, {n: w[n].shape for n in _SMALL})

    g, delta, new_m, new_v = {}, {}, {}, {}
    for n, t, o in zip(_BIG, ts, others):
        gg, dl, nm, nv = _adamw_halves(w[n][0], t, o, m[n][0], v[n][0], c_arr, name=f"adamw_{n}")
        g[n], delta[n], new_m[n], new_v[n] = gg[None], dl[None], nm[None], nv[None]
    dl, nm, nv = _adamw(w_ada[0], g_ada, m_w_ada[0], v_w_ada[0], name="adamw_w_ada")
    g["w_ada"], delta["w_ada"], new_m["w_ada"], new_v["w_ada"] = g_ada[None], dl[None], nm[None], nv[None]
    shapes = {n: w[n].shape for n in _SMALL}
    dl, nm, nv = _adamw(_pack_small({n: w[n] for n in _SMALL}), tot, _pack_small({n: m[n] for n in _SMALL}),
                        _pack_small({n: v[n] for n in _SMALL}), name="adamw_small")
    delta.update(_unpack_small(dl, shapes))
    new_m.update(_unpack_small(nm, shapes))
    new_v.update(_unpack_small(nv, shapes))
    g.update(g_small)

    loss = lax.psum(loss_v[0, 0], ("x", "y", "c"))
    return (loss, grad_x[None], *[g[n] for n in names], *[delta[n] for n in names], *[new_m[n] for n in names],
            *[new_v[n] for n in names])
```

```python
import numpy as np

import jax
import jax.numpy as jnp
from jax import lax
from jax.experimental import pallas as pl
from jax.experimental.pallas import tpu as pltpu

F32 = jnp.float32
BF16 = jnp.bfloat16

D_MODEL = 1024
SEQ = 8192
HEAD_DIM = 64
GRID_W = 64
ROWS = SEQ // GRID_W
NA_HEADS = 8
NA_KH = 8
NA_KW = 16
DIL_CONFIGS = ((128, 1), (512, 4), (2048, 16))
ROT_DIM = 16
ROPE_THETA = 500000.0
D_FF = 2816
EPS = 1e-6
NEG = -1e30
WA = 512
WB = 768
WB_OUT = 256
W_QKV = 3 * WA + 3 * WB
W_QK = 2 * WA + 2 * WB
W_GATES = 2 * D_MODEL
SCALE = HEAD_DIM ** -0.5

ADAM_LR = 0.001
ADAM_B1 = 0.9
ADAM_B2 = 0.999
ADAM_EPS = 1e-08
ADAM_WD = 0.01
ADAM_STEP = 10

LANES = 128
ROW_TILE = 256
Q_BLOCK = 256
NA_QROWS = Q_BLOCK // GRID_W
NA_KROWS = NA_QROWS + NA_KH - 1
NA_NK = NA_KROWS * GRID_W
DIL_HALF = 64
DIL_NK = Q_BLOCK + 2 * DIL_HALF
N_QBLK = SEQ // Q_BLOCK

N_DEV = 8
N_CHIP = 4
FF_CHIP = 2 * D_FF // N_CHIP
STATS_W = 14336


def _pcall(body, *, name, **kw):
    return pl.pallas_call(body, name=name, **kw)


_NT = (((1,), (1,)), ((), ()))
_TN = (((0,), (0,)), ((), ()))
_ARB = pltpu.CompilerParams(dimension_semantics=("arbitrary",))
_PAR = pltpu.CompilerParams(dimension_semantics=("parallel",))


def _dot(a, b):
    return jnp.dot(a, b, preferred_element_type=F32)


def _dot_nt(a, b):
    return lax.dot_general(a, b, _NT, preferred_element_type=F32)


def _wgrad(a, b, *, name, tm, tn, tk=1024, chips=None):
    s, ma = a.shape
    nb = b.shape[1]
    nk = s // tk
    nc = nb // chips if chips else tn
    cpb = tn // nc

    def body(a_ref, b_ref, o_ref, acc):
        k = pl.program_id(2)
        r = lax.dot_general(a_ref[...].astype(BF16), b_ref[...].astype(BF16), _TN, preferred_element_type=F32)

        @pl.when(k == 0)
        def _():
            acc[...] = r

        @pl.when(k > 0)
        def _():
            acc[...] += r

        @pl.when(k == nk - 1)
        def _():
            if chips:
                for q in range(cpb):
                    o_ref[q] = acc[:, q * nc:(q + 1) * nc]
            else:
                o_ref[...] = acc[...]

    if chips:
        o_spec = pl.BlockSpec((cpb, tm, nc), lambda i, j, k: (j, i, 0))
        out_shape = jax.ShapeDtypeStruct((chips, ma, nc), F32)
    else:
        o_spec = pl.BlockSpec((tm, tn), lambda i, j, k: (i, j))
        out_shape = jax.ShapeDtypeStruct((ma, nb), F32)
    return _pcall(
        body, name=name, grid=(ma // tm, nb // tn, nk),
        in_specs=[pl.BlockSpec((tk, tm), lambda i, j, k: (k, i)), pl.BlockSpec((tk, tn), lambda i, j, k: (k, j))],
        out_specs=o_spec, out_shape=out_shape, scratch_shapes=[pltpu.VMEM((tm, tn), F32)],
        compiler_params=pltpu.CompilerParams(dimension_semantics=("parallel", "parallel", "arbitrary")),
    )(a, b)


def _row_call(body, *, name, row_ins, res_ins, row_outs, acc_outs=(), scratch=()):
    s = row_ins[0].shape[0]
    n = s // ROW_TILE
    nri, nre, nro, nao = len(row_ins), len(res_ins), len(row_outs), len(acc_outs)

    def whole(shape):
        nd = len(shape)
        return pl.BlockSpec(tuple(shape), lambda i: (0,) * nd, pipeline_mode=pl.Buffered(1))

    def whole_out(shape):
        nd = len(shape)
        return pl.BlockSpec(tuple(shape), lambda i: (0,) * nd)

    in_specs = [pl.BlockSpec((ROW_TILE, a.shape[1]), lambda i: (i, 0)) for a in row_ins]
    in_specs += [whole(a.shape) for a in res_ins]
    out_specs = [pl.BlockSpec((ROW_TILE, w), lambda i: (i, 0)) for w, _ in row_outs]
    out_specs += [whole_out(shp) for shp, _ in acc_outs]
    out_shape = [jax.ShapeDtypeStruct((s, w), dt) for w, dt in row_outs]
    out_shape += [jax.ShapeDtypeStruct(tuple(shp), dt) for shp, dt in acc_outs]

    def wrapped(*refs):
        at = [0, nri, nri + nre, nri + nre + nro, nri + nre + nro + nao]
        body(pl.program_id(0), n, refs[at[0]:at[1]], refs[at[1]:at[2]], refs[at[2]:at[3]], refs[at[3]:at[4]],
             refs[at[4]:])

    return _pcall(wrapped, name=name, grid=(n,), in_specs=in_specs, out_specs=out_specs, out_shape=out_shape,
                  scratch_shapes=list(scratch), compiler_params=_ARB)(*row_ins, *res_ins)


def _fold8(t):
    r, w = t.shape
    return jnp.sum(t.reshape(r // 8, 8, w), axis=0)


def _sigmoid(t):
    return 1.0 / (1.0 + jnp.exp(-t))


def _head_lanes():
    return lax.broadcasted_iota(jnp.int32, (1, LANES), 1) < HEAD_DIM


def _head_mean(t, lo):
    s_lo = jnp.sum(jnp.where(lo, t, 0.0), axis=1, keepdims=True)
    s_hi = jnp.sum(jnp.where(lo, 0.0, t), axis=1, keepdims=True)
    return jnp.where(lo, s_lo, s_hi) * (1.0 / HEAD_DIM)


def _rms_mod(xv, g, sc, sh):
    rstd = lax.rsqrt(jnp.mean(xv * xv, axis=1, keepdims=True) + EPS)
    return (xv * rstd * g) * (1.0 + sc) + sh


def _rms_mod_bwd(xv, dh, g, sc):
    rstd = lax.rsqrt(jnp.mean(xv * xv, axis=1, keepdims=True) + EPS)
    xhat = xv * rstd
    dn = dh * (1.0 + sc)
    dxhat = dn * g
    dx = rstd * (dxhat - xhat * jnp.mean(dxhat * xhat, axis=1, keepdims=True))
    return dx, dh, dh * (xhat * g), dn * xhat


def _mix_weights(ls):
    m = jnp.maximum(jnp.maximum(ls[0], ls[1]), ls[2])
    es = [jnp.exp(t - m) for t in ls]
    den = es[0] + es[1] + es[2]
    return [e / den for e in es]


def _rope_tables():
    half = ROT_DIM // 2
    inv_freq = ROPE_THETA ** (-(jnp.arange(half, dtype=F32) * 2.0) / ROT_DIM)
    lane = np.arange(LANES) % HEAD_DIM
    ang = jnp.arange(SEQ).astype(F32)[:, None] * jnp.tile(inv_freq, LANES // half)[None, :]
    cos, sin = jnp.cos(ang), jnp.sin(ang)
    first, second = jnp.asarray(lane < half)[None, :], jnp.asarray((lane >= half) & (lane < ROT_DIM))[None, :]
    cos_t = jnp.where(first | second, cos, 1.0)
    return cos_t, jnp.where(second, sin, 0.0), jnp.where(first, -sin, 0.0)


_SECTIONS = ((0, WA, 0, False), (WA, 2 * WA, 1, False), (2 * WA, 3 * WA, -1, False),
             (3 * WA, 3 * WA + WB, 2, True), (3 * WA + WB, 3 * WA + 2 * WB, 3, True), (3 * WA + 2 * WB, W_QKV, -1, False))


def _pre_attn_fwd(x, cos_t, sa_t, sb_t, g1, sc1, sh1, w_qkv, w_gates, gains, *, name):
    half = ROT_DIM // 2

    def body(i, n, rin, res, rout, aout, scr):
        x_ref, cos_ref, sa_ref, sb_ref = rin
        g_ref, sc_ref, sh_ref, wq_ref, wg_ref, gains_ref = res
        h1_ref, qkvn_ref, pre_ref, gates_ref = rout
        hb = _rms_mod(x_ref[...], g_ref[...], sc_ref[...], sh_ref[...]).astype(BF16)
        h1_ref[...] = hb
        gates_ref[...] = _dot(hb, wg_ref[...]).astype(BF16)
        lo = _head_lanes()
        cosv, sav, sbv = cos_ref[...], sa_ref[...], sb_ref[...]
        pre_at = 0
        for c0, c1, kind, rot in _SECTIONS:
            sec = _dot(hb, wq_ref[:, c0:c1])
            for ch in range((c1 - c0) // LANES):
                t = sec[:, ch * LANES:(ch + 1) * LANES]
                if kind >= 0:
                    pre_ref[:, pre_at:pre_at + LANES] = t.astype(BF16)
                    pre_at += LANES
                    t = t * lax.rsqrt(_head_mean(t * t, lo) + EPS) * gains_ref[kind:kind + 1, :]
                    if rot:
                        t = t * cosv + pltpu.roll(t, half, 1) * sav + pltpu.roll(t, LANES - half, 1) * sbv
                qkvn_ref[:, c0 + ch * LANES:c0 + (ch + 1) * LANES] = t.astype(BF16)

    return _row_call(body, name=name, row_ins=[x, cos_t, sa_t, sb_t], res_ins=[g1, sc1, sh1, w_qkv, w_gates, gains],
                     row_outs=[(D_MODEL, BF16), (W_QKV, BF16), (W_QK, BF16), (W_GATES, BF16)])


def _pre_attn_bwd(qk_pre, d_parts, dgates, x, dx1, cos_t, sa_t, sb_t, w_qkv, w_gates, gains, g1, sc1, *, name):
    half = ROT_DIM // 2
    nparts = len(d_parts)
    where = []
    for pi, part in enumerate(d_parts):
        where += [(pi, cj) for cj in range(part.shape[1] // LANES)]
    assert len(where) == W_QKV // LANES

    def body(i, n, rin, res, rout, aout, scr):
        pre_ref, d_refs = rin[0], rin[1:1 + nparts]
        dgates_ref, x_ref, dx1_ref, cos_ref, sa_ref, sb_ref = rin[1 + nparts:]
        wq_ref, wg_ref, gains_ref, g_ref, sc_ref = res
        dqkv_ref, gx_ref = rout
        dgains_ref, sums_ref = aout
        accg, accs = scr

        @pl.when(i == 0)
        def _():
            accg[...] = jnp.zeros_like(accg)
            accs[...] = jnp.zeros_like(accs)

        lo = _head_lanes()
        cosv, sav, sbv = cos_ref[...], sa_ref[...], sb_ref[...]
        dh = _dot_nt(dgates_ref[...], wg_ref[...])
        pre_at = 0
        for c0, c1, kind, rot in _SECTIONS:
            for ch in range((c1 - c0) // LANES):
                pi, cj = where[c0 // LANES + ch]
                dt = d_refs[pi][:, cj * LANES:(cj + 1) * LANES]
                if kind >= 0:
                    if rot:
                        dt = dt * cosv + pltpu.roll(dt * sav, LANES - half, 1) + pltpu.roll(dt * sbv, half, 1)
                    t = pre_ref[:, pre_at:pre_at + LANES].astype(F32)
                    pre_at += LANES
                    rstd = lax.rsqrt(_head_mean(t * t, lo) + EPS)
                    xhat = t * rstd
                    accg[kind] += _fold8(dt * xhat)
                    dxhat = dt * gains_ref[kind:kind + 1, :]
                    dt = rstd * (dxhat - xhat * _head_mean(dxhat * xhat, lo))
                dqkv_ref[:, c0 + ch * LANES:c0 + (ch + 1) * LANES] = dt.astype(BF16)
            dh = dh + _dot_nt(dqkv_ref[:, c0:c1], wq_ref[:, c0:c1])
        dx, t_sh, t_sc, t_g = _rms_mod_bwd(x_ref[...], dh, g_ref[...], sc_ref[...])
        gx_ref[...] = dx1_ref[...] + dx
        accs[0] += _fold8(t_sh)
        accs[1] += _fold8(t_sc)
        accs[2] += _fold8(t_g)

        @pl.when(i == n - 1)
        def _():
            t = jnp.sum(accg[...], axis=1)
            dgains_ref[...] = t + pltpu.roll(t, HEAD_DIM, 1)
            sums_ref[...] = jnp.sum(accs[...], axis=1)

    return _row_call(
        body, name=name, row_ins=[qk_pre, *d_parts, dgates, x, dx1, cos_t, sa_t, sb_t],
        res_ins=[w_qkv, w_gates, gains, g1, sc1], row_outs=[(W_QKV, BF16), (D_MODEL, F32)],
        acc_outs=[((4, LANES), F32), ((3, D_MODEL), F32)],
        scratch=[pltpu.VMEM((4, 8, LANES), F32), pltpu.VMEM((3, 8, D_MODEL), F32)])


def _post_attn_fwd(o_a, o_g, l_g, gates, x, w_pa, w_pb, w_o, b_gate, gt1, g2, sc2, sh2, *, name):
    d = D_MODEL

    def body(i, n, rin, res, rout, aout, scr):
        oa_ref, o0, o1, o2, l0, l1, l2, gates_ref, x_ref = rin
        wpa_ref, wpb_ref, wo_ref, b_ref, gt_ref, g_ref, sc_ref, sh_ref = res
        ob_ref, merged_ref, mo_ref, x1_ref, h2_ref = rout
        ws = _mix_weights([l0[...], l1[...], l2[...]])
        obb = (ws[0] * o0[...] + ws[1] * o1[...] + ws[2] * o2[...]).astype(BF16)
        ob_ref[...] = obb
        pa = _dot(oa_ref[...].astype(BF16), wpa_ref[...])
        pb = _dot(obb, wpb_ref[...])
        ga = _sigmoid(gates_ref[:, :d].astype(F32) + b_ref[:, :d])
        gb = _sigmoid(gates_ref[:, d:].astype(F32) + b_ref[:, d:])
        merged = (ga * pa + gb * pb).astype(BF16)
        merged_ref[...] = merged
        mo = _dot(merged, wo_ref[...])
        mo_ref[...] = mo.astype(BF16)
        x1 = x_ref[...] + gt_ref[...] * mo
        x1_ref[...] = x1
        h2_ref[...] = _rms_mod(x1, g_ref[...], sc_ref[...], sh_ref[...]).astype(BF16)

    return _row_call(body, name=name, row_ins=[o_a, *o_g, *l_g, gates, x],
                     res_ins=[w_pa, w_pb, w_o, b_gate, gt1, g2, sc2, sh2],
                     row_outs=[(WB_OUT, BF16), (d, BF16), (d, BF16), (d, F32), (d, BF16)])


def _ffn_fwd(h2, w_ffn_in, *, name):
    def body(i, n, rin, res, rout, aout, scr):
        (h_ref,), (w_ref,), (act_ref, ff_ref) = rin, res, rout
        hv = h_ref[...]
        for q in range(2):
            a = _dot(hv, w_ref[q])
            up = _dot(hv, w_ref[q + 2])
            sl = slice(q * FF_CHIP, (q + 1) * FF_CHIP)
            act_ref[:, sl] = (a * _sigmoid(a) * up).astype(BF16)
            ff_ref[:, sl] = a.astype(BF16)
            ff_ref[:, D_FF + q * FF_CHIP:D_FF + (q + 1) * FF_CHIP] = up.astype(BF16)

    return _row_call(body, name=name, row_ins=[h2], res_ins=[w_ffn_in], row_outs=[(D_FF, BF16), (2 * D_FF, BF16)])


def _ffn_mid(act, ff, x1, tgt, w_ffn_out, gt2, *, name):
    d = D_MODEL

    def body(i, n, rin, res, rout, aout, scr):
        act_ref, ff_ref, x1_ref, tgt_ref = rin
        wo_ref, gt_ref = res
        dy_ref, dffo_ref, dff_ref = rout
        dgt_ref, loss_ref = aout
        (acc,) = scr

        @pl.when(i == 0)
        def _():
            acc[...] = jnp.zeros_like(acc)

        ffo = _dot(act_ref[...], wo_ref[...])
        gtv = gt_ref[...]
        e = x1_ref[...] + gtv * ffo - tgt_ref[...]
        dy = e * (1.0 / d)
        dy_ref[...] = dy
        dffo = (gtv * dy).astype(BF16)
        dffo_ref[...] = dffo
        acc[0] += _fold8(dy * ffo)
        acc[1] += _fold8(e * e)
        for q in range(2):
            sl = slice(q * FF_CHIP, (q + 1) * FF_CHIP)
            su = slice(D_FF + q * FF_CHIP, D_FF + (q + 1) * FF_CHIP)
            dact = _dot_nt(dffo, wo_ref[sl, :])
            a = ff_ref[:, sl].astype(F32)
            up = ff_ref[:, su].astype(F32)
            sg = _sigmoid(a)
            dff_ref[:, sl] = (dact * up * (sg * (1.0 + a * (1.0 - sg)))).astype(BF16)
            dff_ref[:, su] = (dact * (a * sg)).astype(BF16)

        @pl.when(i == n - 1)
        def _():
            dgt_ref[...] = jnp.sum(acc[0], axis=0, keepdims=True)
            tot = jnp.sum(jnp.sum(acc[1], axis=0, keepdims=True), axis=1, keepdims=True)
            loss_ref[...] = jnp.broadcast_to(tot * (0.5 / d), (1, LANES))

    return _row_call(body, name=name, row_ins=[act, ff, x1, tgt], res_ins=[w_ffn_out, gt2],
                     row_outs=[(d, F32), (d, BF16), (2 * D_FF, BF16)], acc_outs=[((1, d), F32), ((1, LANES), F32)],
                     scratch=[pltpu.VMEM((2, 8, d), F32)])


def _ffn_in_bwd(dff, x1, dy, mo, w_ffn_in, g2, sc2, gt1, *, name):
    d = D_MODEL

    def body(i, n, rin, res, rout, aout, scr):
        dff_ref, x1_ref, dy_ref, mo_ref = rin
        w_ref, g_ref, sc_ref, gt_ref = res
        dx1_ref, dmo_ref = rout
        (sums_ref,) = aout
        (acc,) = scr

        @pl.when(i == 0)
        def _():
            acc[...] = jnp.zeros_like(acc)

        dh = _dot_nt(dff_ref[:, :FF_CHIP], w_ref[0])
        for q in range(1, N_CHIP):
            dh = dh + _dot_nt(dff_ref[:, q * FF_CHIP:(q + 1) * FF_CHIP], w_ref[q])
        dx, t_sh, t_sc, t_g = _rms_mod_bwd(x1_ref[...], dh, g_ref[...], sc_ref[...])
        dx1 = dy_ref[...] + dx
        dx1_ref[...] = dx1
        dmo_ref[...] = (gt_ref[...] * dx1).astype(BF16)
        acc[0] += _fold8(t_sh)
        acc[1] += _fold8(t_sc)
        acc[2] += _fold8(t_g)
        acc[3] += _fold8(dx1 * mo_ref[...].astype(F32))

        @pl.when(i == n - 1)
        def _():
            sums_ref[...] = jnp.sum(acc[...], axis=1)

    return _row_call(body, name=name, row_ins=[dff, x1, dy, mo], res_ins=[w_ffn_in, g2, sc2, gt1],
                     row_outs=[(d, F32), (d, BF16)], acc_outs=[((4, d), F32)], scratch=[pltpu.VMEM((4, 8, d), F32)])


def _post_attn_bwd(dmo, gates, o_a, o_g, l_g, w_pa, w_pb, w_o, b_gate, *, name):
    d = D_MODEL

    def body(i, n, rin, res, rout, aout, scr):
        dmo_ref, gates_ref, oa_ref, o0, o1, o2, l0, l1, l2 = rin
        wpa_ref, wpb_ref, wo_ref, b_ref = res
        dpa_ref, dpb_ref, dgates_ref, doa_ref = rout[:4]
        do_refs, dl_refs = rout[4:7], rout[7:10]
        (dbg_ref,) = aout
        (acc,) = scr

        @pl.when(i == 0)
        def _():
            acc[...] = jnp.zeros_like(acc)

        ogs = [o0[...], o1[...], o2[...]]
        ws = _mix_weights([l0[...], l1[...], l2[...]])
        obb = (ws[0] * ogs[0] + ws[1] * ogs[1] + ws[2] * ogs[2]).astype(BF16)
        pa = _dot(oa_ref[...].astype(BF16), wpa_ref[...])
        pb = _dot(obb, wpb_ref[...])
        ga = _sigmoid(gates_ref[:, :d].astype(F32) + b_ref[:, :d])
        gb = _sigmoid(gates_ref[:, d:].astype(F32) + b_ref[:, d:])
        dm = _dot_nt(dmo_ref[...], wo_ref[...])
        dpa = (dm * ga).astype(BF16)
        dpb = (dm * gb).astype(BF16)
        dpa_ref[...] = dpa
        dpb_ref[...] = dpb
        dga = dm * pa * ga * (1.0 - ga)
        dgb = dm * pb * gb * (1.0 - gb)
        dgates_ref[:, :d] = dga.astype(BF16)
        dgates_ref[:, d:] = dgb.astype(BF16)
        acc[:, :d] += _fold8(dga)
        acc[:, d:] += _fold8(dgb)
        doa_ref[...] = _dot_nt(dpa, wpa_ref[...])
        dob = _dot_nt(dpb, wpb_ref[...])
        lo = _head_lanes()
        for ch in range(WB_OUT // LANES):
            sl = slice(ch * LANES, (ch + 1) * LANES)
            dv = dob[:, sl]
            wc = [w[:, sl] for w in ws]
            ts = [_head_mean(dv * og[:, sl], lo) * float(HEAD_DIM) for og in ogs]
            tbar = wc[0] * ts[0] + wc[1] * ts[1] + wc[2] * ts[2]
            for g in range(3):
                do_refs[g][:, sl] = wc[g] * dv
                dl_refs[g][:, sl] = wc[g] * (ts[g] - tbar)

        @pl.when(i == n - 1)
        def _():
            dbg_ref[...] = jnp.sum(acc[...], axis=0, keepdims=True)

    return _row_call(body, name=name, row_ins=[dmo, gates, o_a, *o_g, *l_g], res_ins=[w_pa, w_pb, w_o, b_gate],
                     row_outs=[(d, BF16), (d, BF16), (2 * d, BF16), (WA, F32)] + [(WB_OUT, F32)] * 6,
                     acc_outs=[((1, 2 * d), F32)], scratch=[pltpu.VMEM((8, 2 * d), F32)])


def _attn_fwd(qkv, qc0, kc0, vc0, npairs, table, kstart, cls, nk, *, name):
    s = qkv.shape[0]
    per_head = table.shape[1] > 1
    hb = 2 if per_head else 1

    def body(ks_ref, cls_ref, q_ref, k_ref, v_ref, b_ref, o_ref, lse_ref):
        i = pl.program_id(1)
        ks = pl.multiple_of(ks_ref[i], 64)
        q2 = q_ref[...]
        k2 = k_ref[pl.ds(ks, nk), :]
        v2 = v_ref[pl.ds(ks, nk), :]
        lo = _head_lanes()
        outs, lses = [], []
        for h in range(2):
            qm = jnp.where(lo if h == 0 else jnp.logical_not(lo), q2, jnp.zeros_like(q2))
            sc = _dot_nt(qm, k2) * SCALE + b_ref[0, h if per_head else 0]
            m = jnp.max(sc, axis=1, keepdims=True)
            p = jnp.exp(sc - m)
            l = jnp.sum(p, axis=1, keepdims=True)
            pv = _dot(p.astype(BF16), v2)
            outs.append(pv / l)
            lses.append(m + jnp.log(l))
        o_ref[...] = jnp.where(lo, outs[0], outs[1])
        lse_ref[...] = jnp.where(lo, lses[0], lses[1])

    w = npairs * LANES
    grid_spec = pltpu.PrefetchScalarGridSpec(
        num_scalar_prefetch=2, grid=(npairs, N_QBLK),
        in_specs=[
            pl.BlockSpec((Q_BLOCK, LANES), lambda p, i, ks, cl: (i, qc0 + p)),
            pl.BlockSpec((s, LANES), lambda p, i, ks, cl: (0, kc0 + p)),
            pl.BlockSpec((s, LANES), lambda p, i, ks, cl: (0, vc0 + p)),
            pl.BlockSpec((1, hb, Q_BLOCK, nk), lambda p, i, ks, cl: (cl[i], p if per_head else 0, 0, 0)),
        ],
        out_specs=[pl.BlockSpec((Q_BLOCK, LANES), lambda p, i, ks, cl: (i, p)),
                   pl.BlockSpec((Q_BLOCK, LANES), lambda p, i, ks, cl: (i, p))],
    )
    return _pcall(body, name=name, grid_spec=grid_spec,
                  out_shape=[jax.ShapeDtypeStruct((s, w), F32), jax.ShapeDtypeStruct((s, w), F32)],
                  compiler_params=pltpu.CompilerParams(dimension_semantics=("parallel", "arbitrary")),
                  )(kstart, cls, qkv, qkv, qkv, table)


def _attn_bwd(qkv, qc0, kc0, vc0, npairs, table, kstart, cls, nk, do, o, lse, *, name, dlse=None, want_dbias=False):
    s = qkv.shape[0]
    per_head = table.shape[1] > 1
    hb = 2 if per_head else 1
    has_dlse = dlse is not None

    def body(ks_ref, cls_ref, q_ref, k_ref, v_ref, b_ref, do_ref, o_ref, lse_ref, *rest):
        if has_dlse:
            dlse_ref, rest = rest[0], rest[1:]
        dq_ref, dk_ref, dv_ref = rest[0], rest[1], rest[2]
        db_ref = rest[3] if want_dbias else None
        i = pl.program_id(1)

        @pl.when(i == 0)
        def _():
            dk_ref[...] = jnp.zeros_like(dk_ref)
            dv_ref[...] = jnp.zeros_like(dv_ref)

        ks = pl.multiple_of(ks_ref[i], 64)
        q2 = q_ref[...]
        k2 = k_ref[pl.ds(ks, nk), :]
        v2 = v_ref[pl.ds(ks, nk), :]
        do2 = do_ref[...]
        lse2 = lse_ref[...]
        doo = do2 * o_ref[...]
        do2b = do2.astype(BF16)
        lo = _head_lanes()
        lane = lax.broadcasted_iota(jnp.int32, (1, LANES), 1)
        if want_dbias:
            first = jnp.logical_or(i == 0, cls_ref[i] != cls_ref[jnp.maximum(i - 1, 0)])
        dqs, dks, dvs = [], [], []
        for h in range(2):
            mh = lo if h == 0 else jnp.logical_not(lo)
            qm = jnp.where(mh, q2, jnp.zeros_like(q2))
            sc = _dot_nt(qm, k2) * SCALE + b_ref[0, h if per_head else 0]
            lse_h = jnp.max(jnp.where(mh, lse2, NEG), axis=1, keepdims=True)
            p = jnp.exp(sc - lse_h)
            delta = jnp.sum(jnp.where(mh, doo, 0.0), axis=1, keepdims=True)
            dom = jnp.where(mh, do2b, jnp.zeros_like(do2b))
            dp = _dot_nt(dom, v2)
            t = dp - delta
            if has_dlse:
                t = t + jnp.sum(jnp.where(lane == h * HEAD_DIM, dlse_ref[...], 0.0), axis=1, keepdims=True)
            ds = p * t
            if want_dbias:
                @pl.when(first)
                def _():
                    db_ref[0, h] = ds

                @pl.when(jnp.logical_not(first))
                def _():
                    db_ref[0, h] += ds
            dsb = ds.astype(BF16)
            dqs.append(_dot(dsb, k2))
            dks.append(lax.dot_general(dsb, q2, _TN, preferred_element_type=F32))
            dvs.append(lax.dot_general(p.astype(BF16), do2b, _TN, preferred_element_type=F32))
        dq_ref[...] = jnp.where(lo, dqs[0], dqs[1]) * SCALE
        dk_ref[pl.ds(ks, nk), :] += jnp.where(lo, dks[0], dks[1]) * SCALE
        dv_ref[pl.ds(ks, nk), :] += jnp.where(lo, dvs[0], dvs[1])

    w = npairs * LANES
    blk = lambda: pl.BlockSpec((Q_BLOCK, LANES), lambda p, i, ks, cl: (i, p))
    full = lambda: pl.BlockSpec((s, LANES), lambda p, i, ks, cl: (0, p))
    tab = lambda: pl.BlockSpec((1, hb, Q_BLOCK, nk), lambda p, i, ks, cl: (cl[i], p if per_head else 0, 0, 0))
    in_specs = [
        pl.BlockSpec((Q_BLOCK, LANES), lambda p, i, ks, cl: (i, qc0 + p)),
        pl.BlockSpec((s, LANES), lambda p, i, ks, cl: (0, kc0 + p)),
        pl.BlockSpec((s, LANES), lambda p, i, ks, cl: (0, vc0 + p)),
        tab(), blk(), blk(), blk(),
    ]
    args = [kstart, cls, qkv, qkv, qkv, table, do, o, lse]
    if has_dlse:
        in_specs.append(blk())
        args.append(dlse)
    out_specs = [blk(), full(), full()]
    out_shape = [jax.ShapeDtypeStruct((s, w), F32)] * 3
    if want_dbias:
        assert per_head
        out_specs.append(tab())
        out_shape.append(jax.ShapeDtypeStruct(table.shape, F32))
    grid_spec = pltpu.PrefetchScalarGridSpec(num_scalar_prefetch=2, grid=(npairs, N_QBLK), in_specs=in_specs,
                                             out_specs=out_specs)
    return _pcall(body, name=name, grid_spec=grid_spec, out_shape=out_shape,
                  compiler_params=pltpu.CompilerParams(dimension_semantics=("arbitrary", "arbitrary")))(*args)


_NA_CLASS_R0 = (0, NA_QROWS, ROWS - NA_QROWS)
_NA_CLASS_K0 = (0, 0, ROWS - NA_KROWS)
_RPB_ROWS = 3 * NA_QROWS * NA_KROWS
_RPB_ROWS_PAD = 136
_RPB_RO = 2 * NA_KH - 1
_RPB_CO = 2 * NA_KW - 1


def _na_constants():
    a = np.arange(NA_QROWS)
    b = np.arange(NA_KROWS)
    col = np.arange(GRID_W)
    oh_row = np.zeros((_RPB_ROWS_PAD, 16), np.float32)
    vrow = np.zeros((3, NA_QROWS, NA_KROWS), bool)
    for t in range(3):
        qr = _NA_CLASS_R0[t] + a
        kr = _NA_CLASS_K0[t] + b
        rs = np.clip(qr - NA_KH // 2, 0, ROWS - NA_KH)
        vrow[t] = (kr[None, :] >= rs[:, None]) & (kr[None, :] < rs[:, None] + NA_KH)
        ro = kr[None, :] - qr[:, None] + (NA_KH - 1)
        for ai in range(NA_QROWS):
            for bi in range(NA_KROWS):
                if vrow[t, ai, bi]:
                    oh_row[(t * NA_QROWS + ai) * NA_KROWS + bi, ro[ai, bi]] = 1.0
    cs = np.clip(col - NA_KW // 2, 0, GRID_W - NA_KW)
    vcol = (col[None, :] >= cs[:, None]) & (col[None, :] < cs[:, None] + NA_KW)
    co = col[None, :] - col[:, None] + (NA_KW - 1)
    oh_col = np.zeros((GRID_W * GRID_W, LANES), np.float32)
    for qc in range(GRID_W):
        for kc in range(GRID_W):
            if vcol[qc, kc]:
                oh_col[qc * GRID_W + kc, co[qc, kc]] = 1.0
    valid = vrow[:, :, None, :, None] & vcol[None, None, :, None, :]
    mask = np.where(valid, 0.0, NEG).astype(np.float32).reshape(3, 1, Q_BLOCK, NA_NK)
    ks = np.clip(np.arange(N_QBLK) * NA_QROWS - NA_KH // 2, 0, ROWS - NA_KROWS) * GRID_W
    cls = np.ones(N_QBLK, np.int32)
    cls[0], cls[-1] = 0, 2
    return oh_row, oh_col, mask, ks.astype(np.int32), cls


def _rpb_expand(rpb_pad, oh_row, oh_col_t, *, name):
    def body(r_ref, ohr_ref, ohc_ref, o_ref):
        t = jnp.dot(ohr_ref[...], r_ref[0], preferred_element_type=F32, precision=lax.Precision.HIGHEST)
        o_ref[0] = jnp.dot(t, ohc_ref[...], preferred_element_type=F32, precision=lax.Precision.HIGHEST)

    return _pcall(
        body, name=name, grid=(NA_HEADS,),
        in_specs=[pl.BlockSpec((1, 16, LANES), lambda h: (h, 0, 0)), pl.BlockSpec((_RPB_ROWS_PAD, 16), lambda h: (0, 0)),
                  pl.BlockSpec((LANES, GRID_W * GRID_W), lambda h: (0, 0))],
        out_specs=pl.BlockSpec((1, _RPB_ROWS_PAD, GRID_W * GRID_W), lambda h: (h, 0, 0)),
        out_shape=jax.ShapeDtypeStruct((NA_HEADS, _RPB_ROWS_PAD, GRID_W * GRID_W), F32), compiler_params=_PAR,
    )(rpb_pad, oh_row, oh_col_t)


def _rpb_reduce(dx, oh_row_t, oh_col, *, name):
    def body(d_ref, ohr_ref, ohc_ref, o_ref):
        t = jnp.dot(d_ref[0], ohc_ref[...], preferred_element_type=F32, precision=lax.Precision.HIGHEST)
        o_ref[0] = jnp.dot(ohr_ref[...], t, preferred_element_type=F32, precision=lax.Precision.HIGHEST)

    return _pcall(
        body, name=name, grid=(NA_HEADS,),
        in_specs=[pl.BlockSpec((1, _RPB_ROWS_PAD, GRID_W * GRID_W), lambda h: (h, 0, 0)),
                  pl.BlockSpec((16, _RPB_ROWS_PAD), lambda h: (0, 0)),
                  pl.BlockSpec((GRID_W * GRID_W, LANES), lambda h: (0, 0))],
        out_specs=pl.BlockSpec((1, 16, LANES), lambda h: (h, 0, 0)),
        out_shape=jax.ShapeDtypeStruct((NA_HEADS, 16, LANES), F32), compiler_params=_PAR,
    )(dx, oh_row_t, oh_col)


def _dil_constants(dilation):
    seg = SEQ // dilation
    nb = seg // Q_BLOCK
    shift = (0, -DIL_HALF, -2 * DIL_HALF)
    qi = np.arange(Q_BLOCK)[:, None]
    ki = np.arange(DIL_NK)[None, :]
    mask = np.stack([np.where(np.abs(ki + sh - qi) <= DIL_HALF, 0.0, NEG) for sh in shift]).astype(np.float32)
    ks, cls = [], []
    for i in range(N_QBLK):
        sub, blk = divmod(i, nb)
        t = 0 if blk == 0 else (2 if blk == nb - 1 else 1)
        cls.append(t)
        ks.append(sub * seg + blk * Q_BLOCK + shift[t])
    return mask.reshape(3, 1, Q_BLOCK, DIL_NK), np.asarray(ks, np.int32), np.asarray(cls, np.int32)


_VM = pl.BlockSpec(memory_space=pltpu.VMEM)


def _ada_fwd(c_all, w, b, *, name):
    def body(c_ref, w_ref, b_ref, o_ref):
        cv = c_ref[...]
        o_ref[...] = jnp.dot(cv * _sigmoid(cv), w_ref[...], preferred_element_type=F32,
                             precision=lax.Precision.HIGHEST) + b_ref[...]

    return _pcall(body, name=name, in_specs=[_VM, _VM, _VM], out_specs=_VM,
                  out_shape=jax.ShapeDtypeStruct((c_all.shape[0], w.shape[1]), F32))(c_all, w, b)


def _ada_bwd(c_all_t, dmod, *, name):
    def body(c_ref, d_ref, o_ref):
        cv = c_ref[...]
        o_ref[...] = jnp.dot(cv * _sigmoid(cv), d_ref[...], preferred_element_type=F32,
                             precision=lax.Precision.HIGHEST)

    return _pcall(body, name=name, in_specs=[_VM, _VM], out_specs=_VM,
                  out_shape=jax.ShapeDtypeStruct((c_all_t.shape[0], dmod.shape[1]), F32))(c_all_t, dmod)


def _row_sum(t, *, name):
    def body(t_ref, o_ref):
        o_ref[...] = jnp.sum(t_ref[...], axis=0, keepdims=True)

    return _pcall(body, name=name, in_specs=[_VM], out_specs=_VM,
                  out_shape=jax.ShapeDtypeStruct((1, t.shape[1]), F32))(t)


def _row_tile(rows):
    tr = rows
    for cand in range(8, 513, 8):
        if rows % cand == 0:
            tr = cand
    return tr


def _adamw_math(wv, gv, mv, vv):
    nm = ADAM_B1 * mv + (1.0 - ADAM_B1) * gv
    nv = ADAM_B2 * vv + (1.0 - ADAM_B2) * (gv * gv)
    m_hat = nm / (1.0 - ADAM_B1 ** ADAM_STEP)
    v_hat = nv / (1.0 - ADAM_B2 ** ADAM_STEP)
    return -ADAM_LR * (m_hat / (jnp.sqrt(v_hat) + ADAM_EPS) + ADAM_WD * wv), nm, nv


def _adamw(w, g, m, v, *, name):
    rows, cols = w.shape
    tr = _row_tile(rows)

    def body(w_ref, g_ref, m_ref, v_ref, d_ref, nm_ref, nv_ref):
        d_ref[...], nm_ref[...], nv_ref[...] = _adamw_math(w_ref[...], g_ref[...], m_ref[...], v_ref[...])

    spec = pl.BlockSpec((tr, cols), lambda i: (i, 0))
    return _pcall(body, name=name, grid=(rows // tr,), in_specs=[spec] * 4, out_specs=[spec] * 3,
                  out_shape=[jax.ShapeDtypeStruct((rows, cols), F32)] * 3, compiler_params=_PAR)(w, g, m, v)


def _adamw_halves(w, g_mine, g_other, m, v, c_arr, *, name):
    rows, cols = w.shape
    hr = rows // 2
    tr = _row_tile(hr)
    nt = hr // tr

    def body(c_ref, w_ref, t_ref, o_ref, m_ref, v_ref, g_ref, d_ref, nm_ref, nv_ref):
        gv = jnp.where(pl.program_id(0) == c_ref[0], t_ref[...], o_ref[...])
        g_ref[...] = gv
        d_ref[...], nm_ref[...], nv_ref[...] = _adamw_math(w_ref[...], gv, m_ref[...], v_ref[...])

    full = pl.BlockSpec((tr, cols), lambda h, i, c: (h * nt + i, 0))
    half = pl.BlockSpec((tr, cols), lambda h, i, c: (i, 0))
    grid_spec = pltpu.PrefetchScalarGridSpec(num_scalar_prefetch=1, grid=(2, nt),
                                             in_specs=[full, half, half, full, full], out_specs=[full] * 4)
    return _pcall(body, name=name, grid_spec=grid_spec, out_shape=[jax.ShapeDtypeStruct((rows, cols), F32)] * 4,
                  compiler_params=pltpu.CompilerParams(dimension_semantics=("parallel", "parallel")),
                  )(c_arr, w, g_mine, g_other, m, v)


_MESH = pl.DeviceIdType.MESH
_ANY = pl.BlockSpec(memory_space=pl.ANY)
_CHIP_FLIPS = ((1, 0), (0, 1), (1, 1))


def _pos():
    return lax.axis_index("x"), lax.axis_index("y"), lax.axis_index("c")


def _flip(v, f):
    return 1 - v if f else v


def _sem_pairs(n):
    return [pltpu.SemaphoreType.DMA((n,)), pltpu.SemaphoreType.DMA((n,))]


def _small_allgather(blk, *, name):
    m_per, n = blk.shape

    def body(x_ref, out_ref, send_sems, recv_sems, local_sem):
        x, y, c = _pos()
        me, sibling = (x, y, c), (x, y, 1 - c)
        chips = [(_flip(x, fx), _flip(y, fy)) for fx, fy in _CHIP_FLIPS]

        def rows(px, py, pc):
            return out_ref.at[pl.ds((4 * px + 2 * py + pc) * m_per, m_per), :]

        def copy(k, block, to, src=None):
            return pltpu.make_async_remote_copy(
                src_ref=rows(*block) if src is None else src, dst_ref=rows(*block),
                send_sem=send_sems.at[k], recv_sem=recv_sems.at[k], device_id=to, device_id_type=_MESH)

        mine = pltpu.make_async_copy(x_ref, rows(*me), local_sem)
        mine.start()
        first = [copy(0, me, sibling, src=x_ref)]
        first += [copy(1 + j, me, (*chip, c), src=x_ref) for j, chip in enumerate(chips)]
        for cp in first:
            cp.start()
        passed = [copy(4 + j, (*chip, c), sibling) for j, chip in enumerate(chips)]
        for j, chip in enumerate(chips):
            copy(1 + j, (*chip, c), me).wait_recv()
            passed[j].start()
        copy(0, sibling, me).wait_recv()
        for j, chip in enumerate(chips):
            copy(4 + j, (*chip, 1 - c), me).wait_recv()
        for cp in first + passed:
            cp.wait_send()
        mine.wait()

    return _pcall(
        body, name=name, out_shape=jax.ShapeDtypeStruct((N_DEV * m_per, n), blk.dtype),
        in_specs=[_VM], out_specs=_VM,
        scratch_shapes=_sem_pairs(7) + [pltpu.SemaphoreType.DMA],
    )(blk)


_HBM = pl.BlockSpec(memory_space=pltpu.HBM)
_SEM = pl.BlockSpec(memory_space=pltpu.SEMAPHORE)
_EFFECT = pltpu.SideEffectType.DATAFLOW_SIDE_EFFECTING


def _split_start(srcs, lands, plan, ncopies, *, name):
    ns, nl = len(srcs), len(lands)

    def body(*refs):
        src_refs, land_refs = refs[:ns], refs[ns:ns + nl]
        send_sems, recv_sems = refs[ns + nl], refs[ns + nl + 1]
        token = refs[-1]
        x, y, c = _pos()
        for k, (src, dst, to, _) in enumerate(plan(x, y, c, src_refs, land_refs)):
            pltpu.make_async_remote_copy(src_ref=src, dst_ref=dst, send_sem=send_sems.at[k], recv_sem=recv_sems.at[k],
                                         device_id=to, device_id_type=_MESH).start()
        token[...] = jnp.zeros_like(token)

    hbm = lambda a: pltpu.HBM(a.shape, a.dtype)
    out = _pcall(
        body, name=name,
        out_shape=(pltpu.SemaphoreType.DMA((ncopies,)), pltpu.SemaphoreType.DMA((ncopies,)),
                   *[hbm(a) for a in srcs], *[hbm(a) for a in lands], jax.ShapeDtypeStruct((8, LANES), F32)),
        in_specs=[_HBM] * (ns + nl), out_specs=(_SEM, _SEM, *[_HBM] * (ns + nl), _VM),
        input_output_aliases={i: 2 + i for i in range(ns + nl)},
        compiler_params=pltpu.CompilerParams(has_side_effects=_EFFECT),
    )(*[pltpu.with_memory_space_constraint(a, pltpu.HBM) for a in (*srcs, *lands)])
    return out[0], out[1], list(out[2:2 + ns]), list(out[2 + ns:2 + ns + nl]), out[-1]


def _split_wait(send_sems, recv_sems, srcs, lands, plan, after, *, name):
    ns, nl = len(srcs), len(lands)

    def body(*refs):
        src_refs, land_refs = refs[:ns], refs[ns:ns + nl]
        send_sems, recv_sems = refs[ns + nl], refs[ns + nl + 1]
        x, y, c = _pos()
        for k, (src, _, _, mine) in enumerate(plan(x, y, c, src_refs, land_refs)):
            cp = pltpu.make_async_remote_copy(src_ref=src, dst_ref=mine, send_sem=send_sems.at[k],
                                              recv_sem=recv_sems.at[k], device_id=(x, y, c), device_id_type=_MESH)
            cp.wait_send()
            cp.wait_recv()

    hbm = lambda a: pltpu.HBM(a.shape, a.dtype)
    out = _pcall(
        body, name=name, out_shape=tuple(hbm(a) for a in (*srcs, *lands)),
        in_specs=[_HBM] * (ns + nl) + [_SEM, _SEM, _ANY], out_specs=tuple([_HBM] * (ns + nl)),
        input_output_aliases={i: i for i in range(ns + nl)},
        compiler_params=pltpu.CompilerParams(has_side_effects=_EFFECT),
    )(*srcs, *lands, send_sems, recv_sems, after)
    return list(out[ns:])


def _ag_plan(nw):
    def plan(x, y, c, sh_refs, full_refs):
        j = 2 * x + y
        out = []
        for wi in range(nw):
            for fx, fy in _CHIP_FLIPS:
                px, py = _flip(x, fx), _flip(y, fy)
                out.append((sh_refs[wi].at[c], full_refs[wi].at[j, c], (px, py, c), full_refs[wi].at[2 * px + py, c]))
            out.append((sh_refs[wi], full_refs[wi].at[j], (x, y, 1 - c), full_refs[wi].at[j]))
        return out
    return plan


def _ag_pass(fulls, *, name):
    nw = len(fulls)

    def body(*refs):
        in_refs, out_refs = refs[:nw], refs[nw:2 * nw]
        send_sems, recv_sems = refs[2 * nw:]
        x, y, c = _pos()
        cps = []
        for wi in range(nw):
            for k, (fx, fy) in enumerate(_CHIP_FLIPS):
                jp = 2 * _flip(x, fx) + _flip(y, fy)
                sems = dict(send_sem=send_sems.at[3 * wi + k], recv_sem=recv_sems.at[3 * wi + k], device_id_type=_MESH)
                send = pltpu.make_async_remote_copy(src_ref=in_refs[wi].at[jp, c], dst_ref=out_refs[wi].at[jp, c],
                                                    device_id=(x, y, 1 - c), **sems)
                recv = pltpu.make_async_remote_copy(src_ref=in_refs[wi].at[jp, c], dst_ref=out_refs[wi].at[jp, 1 - c],
                                                    device_id=(x, y, c), **sems)
                cps.append((send, recv))
        for send, _ in cps:
            send.start()
        for send, recv in cps:
            send.wait_send()
            recv.wait_recv()

    return _pcall(body, name=name, out_shape=[jax.ShapeDtypeStruct(f.shape, f.dtype) for f in fulls],
                  in_specs=[_ANY] * nw, out_specs=[_ANY] * nw, input_output_aliases={i: i for i in range(nw)},
                  scratch_shapes=_sem_pairs(3 * nw))(*fulls)


def _rs_plan(nw):
    def plan(x, y, c, s_refs, rb_refs):
        out = []
        for wi in range(nw):
            for k, (fx, fy) in enumerate(_CHIP_FLIPS):
                px, py = _flip(x, fx), _flip(y, fy)
                out.append((s_refs[wi].at[2 * px + py], rb_refs[wi].at[k], (px, py, c), rb_refs[wi].at[k]))
        return out
    return plan


def _sibling_send_halves(gs, *, name):
    nw = len(gs)

    def body(*refs):
        g_refs, out_refs = refs[:nw], refs[nw:2 * nw]
        send_sems, recv_sems = refs[2 * nw:]
        x, y, c = _pos()
        cps = [pltpu.make_async_remote_copy(src_ref=g_refs[wi].at[k, 1 - c], dst_ref=out_refs[wi].at[k],
                                            send_sem=send_sems.at[4 * wi + k], recv_sem=recv_sems.at[4 * wi + k],
                                            device_id=(x, y, 1 - c), device_id_type=_MESH)
               for wi in range(nw) for k in range(N_CHIP)]
        for cp in cps:
            cp.start()
        for cp in cps:
            cp.wait()

    return _pcall(body, name=name,
                  out_shape=[jax.ShapeDtypeStruct((g.shape[0],) + g.shape[2:], g.dtype) for g in gs],
                  in_specs=[_ANY] * nw, out_specs=[_ANY] * nw, scratch_shapes=_sem_pairs(N_CHIP * nw))(*gs)


def _sibling_swap(ts, *, name):
    nw = len(ts)

    def body(*refs):
        t_refs, out_refs = refs[:nw], refs[nw:2 * nw]
        send_sems, recv_sems = refs[2 * nw:]
        x, y, c = _pos()
        cps = [pltpu.make_async_remote_copy(src_ref=t_refs[wi], dst_ref=out_refs[wi], send_sem=send_sems.at[wi],
                                            recv_sem=recv_sems.at[wi], device_id=(x, y, 1 - c), device_id_type=_MESH)
               for wi in range(nw)]
        for cp in cps:
            cp.start()
        for cp in cps:
            cp.wait()

    return _pcall(body, name=name, out_shape=[jax.ShapeDtypeStruct(t.shape, t.dtype) for t in ts],
                  in_specs=[_ANY] * nw, out_specs=[_ANY] * nw, scratch_shapes=_sem_pairs(nw))(*ts)


def _rs_add(g, ra, c_arr, *, name):
    n, _, r, w = g.shape

    def body(c_ref, g_ref, ra_ref, s_ref, sb_ref):
        t = g_ref[...] + ra_ref[...]
        s_ref[...] = t
        sb_ref[...] = t.astype(BF16)

    grid_spec = pltpu.PrefetchScalarGridSpec(
        num_scalar_prefetch=1, grid=(n,),
        in_specs=[pl.BlockSpec((None, None, r, w), lambda k, c: (k, c[0], 0, 0)),
                  pl.BlockSpec((None, r, w), lambda k, c: (k, 0, 0))],
        out_specs=[pl.BlockSpec((None, r, w), lambda k, c: (k, 0, 0))] * 2)
    return _pcall(body, name=name, grid_spec=grid_spec,
                  out_shape=[jax.ShapeDtypeStruct((n, r, w), F32), jax.ShapeDtypeStruct((n, r, w), BF16)],
                  compiler_params=_PAR)(c_arr, g, ra)


def _rs_final(s, rb, j_arr, *, name):
    _, r, w = s.shape

    def body(j_ref, s_ref, rb_ref, t_ref):
        t_ref[...] = ((s_ref[...] + rb_ref[0].astype(F32)) + rb_ref[1].astype(F32)) + rb_ref[2].astype(F32)

    grid_spec = pltpu.PrefetchScalarGridSpec(
        num_scalar_prefetch=1, grid=(1,),
        in_specs=[pl.BlockSpec((None, r, w), lambda i, j: (j[0], 0, 0)),
                  pl.BlockSpec((3, r, w), lambda i, j: (0, 0, 0))],
        out_specs=pl.BlockSpec((r, w), lambda i, j: (0, 0)))
    return _pcall(body, name=name, grid_spec=grid_spec, out_shape=jax.ShapeDtypeStruct((r, w), F32),
                  compiler_params=_ARB)(j_arr, s, rb)


def _perm_rows(t, d):
    s, w = t.shape
    return t.reshape(s // d, d, w).transpose(1, 0, 2).reshape(s, w)


def _unperm_rows(t, d):
    s, w = t.shape
    return t.reshape(d, s // d, w).transpose(1, 0, 2).reshape(s, w)


def _tile2(g):
    return jnp.concatenate([g, g], axis=1)


_BIG = ("w_in", "w_ffn_in", "w_ffn_out", "w_o", "w_proj_a", "w_proj_b")
_BIG_SHARD = {"w_in": (1024, 1472), "w_ffn_in": (1024, 1408), "w_ffn_out": (704, 1024), "w_o": (256, 1024),
              "w_proj_a": (512, 256), "w_proj_b": (256, 256)}


def _device_step(x2, tgt, mod, wts, late_weights, early_grads, g_norm1, g_norm2, b_gate, g_qa, g_ka, g_qb, g_kb, rpb):
    d = D_MODEL
    sh1, sc1, gt1, sh2, sc2, gt2 = [mod[:, k * d:(k + 1) * d] for k in range(6)]

    oh_row, oh_col, na_mask, na_ks, na_cls = _na_constants()
    rpb_pad = jnp.pad(rpb, ((0, 0), (0, 16 - _RPB_RO), (0, LANES - _RPB_CO)))
    tab = _rpb_expand(rpb_pad, jnp.asarray(oh_row), jnp.asarray(oh_col.T.copy()), name="rpb_expand")
    tab = tab[:, :_RPB_ROWS].reshape(NA_HEADS, 3, NA_QROWS, NA_KROWS, GRID_W, GRID_W)
    tab_a = tab.transpose(1, 0, 2, 4, 3, 5).reshape(3, NA_HEADS, Q_BLOCK, NA_NK) + jnp.asarray(na_mask)
    na_ks, na_cls = jnp.asarray(na_ks), jnp.asarray(na_cls)
    dil = [_dil_constants(dd) for _, dd in DIL_CONFIGS]
    tab_d = jnp.asarray(dil[0][0])
    gains = jnp.concatenate([_tile2(g_qa), _tile2(g_ka), _tile2(g_qb), _tile2(g_kb)], axis=0)
    cos_t, sa_t, sb_t = _rope_tables()

    h1, qkvn, qk_pre, gates = _pre_attn_fwd(x2, cos_t, sa_t, sb_t, g_norm1, sc1, sh1, wts["w_qkv"], wts["w_gates"],
                                            gains, name="pre_attn_fwd")
    o_a, lse_a = _attn_fwd(qkvn, 0, 4, 8, 4, tab_a, na_ks, na_cls, NA_NK, name="attn_a_fwd")
    arrs, o_p, l_p, o_g, l_g = [], [], [], [], []
    for g, (_, dd) in enumerate(DIL_CONFIGS):
        ks_g, cls_g = jnp.asarray(dil[g][1]), jnp.asarray(dil[g][2])
        if dd == 1:
            arr, cb = qkvn, (12, 18, 24)
        else:
            col = lambda base: qkvn[:, base + WB_OUT * g: base + WB_OUT * (g + 1)]
            arr = _perm_rows(jnp.concatenate([col(3 * WA), col(3 * WA + WB), col(3 * WA + 2 * WB)], axis=1), dd)
            cb = (0, 2, 4)
        op, lp = _attn_fwd(arr, cb[0], cb[1], cb[2], 2, tab_d, ks_g, cls_g, DIL_NK, name=f"attn_d{g}_fwd")
        arrs.append((arr, cb, ks_g, cls_g))
        o_p.append(op)
        l_p.append(lp)
        o_g.append(op if dd == 1 else _unperm_rows(op, dd))
        l_g.append(lp if dd == 1 else _unperm_rows(lp, dd))
    wts = dict(wts, **late_weights(o_a))
    o_b, merged, mo, x1, h2 = _post_attn_fwd(o_a, o_g, l_g, gates, x2, wts["w_pa"], wts["w_pb"], wts["w_o"], b_gate,
                                             gt1, g_norm2, sc2, sh2, name="post_attn_fwd")
    act, ff = _ffn_fwd(h2, wts["w_ffn_in"], name="ffn_fwd")

    dy, dffo, dff, dgt2, loss_v = _ffn_mid(act, ff, x1, tgt, wts["w_ffn_out"], gt2, name="ffn_mid")
    grads = {}
    g_ffn_out = _wgrad(act, dffo, name="wg_ffn_out", tm=D_FF // 2, tn=d)
    grads["w_ffn_out"] = g_ffn_out.reshape(N_CHIP, D_FF // N_CHIP, d)
    grads["w_ffn_in"] = _wgrad(h2, dff, name="wg_ffn_in", tm=512, tn=2 * FF_CHIP, chips=N_CHIP)
    dx1, dmo, sums2 = _ffn_in_bwd(dff, x1, dy, mo, wts["w_ffn_in"], g_norm2, sc2, gt1, name="ffn_in_bwd")
    grads["w_o"] = _wgrad(merged, dmo, name="wg_o", tm=d, tn=d).reshape(N_CHIP, d // N_CHIP, d)
    pab = _post_attn_bwd(dmo, gates, o_a, o_g, l_g, wts["w_pa"], wts["w_pb"], wts["w_o"], b_gate, name="post_attn_bwd")
    dpa, dpb, dgates, do_a = pab[:4]
    do_g, dl_g, dbg = pab[4:7], pab[7:10], pab[10]
    g_pa = _wgrad(o_a, dpa, name="wg_pa", tm=WA, tn=d)
    g_pb = _wgrad(o_b, dpb, name="wg_pb", tm=WB_OUT, tn=d)
    grads["w_proj_a"] = g_pa.reshape(WA, N_CHIP, d // N_CHIP).transpose(1, 0, 2)
    grads["w_proj_b"] = g_pb.reshape(WB_OUT, N_CHIP, d // N_CHIP).transpose(1, 0, 2)
    early_grads(grads)
    dqa, dka, dva, dtab = _attn_bwd(qkvn, 0, 4, 8, 4, tab_a, na_ks, na_cls, NA_NK, do_a, o_a, lse_a,
                                    name="attn_a_bwd", want_dbias=True)
    dqs, dks, dvs = [], [], []
    for g, (_, dd) in enumerate(DIL_CONFIGS):
        arr, cb, ks_g, cls_g = arrs[g]
        dog = do_g[g] if dd == 1 else _perm_rows(do_g[g], dd)
        dlg = dl_g[g] if dd == 1 else _perm_rows(dl_g[g], dd)
        dq, dk, dv = _attn_bwd(arr, cb[0], cb[1], cb[2], 2, tab_d, ks_g, cls_g, DIL_NK, dog, o_p[g], l_p[g],
                               name=f"attn_d{g}_bwd", dlse=dlg)
        if dd != 1:
            dq, dk, dv = _unperm_rows(dq, dd), _unperm_rows(dk, dd), _unperm_rows(dv, dd)
        dqs.append(dq)
        dks.append(dk)
        dvs.append(dv)
    dqkv, grad_x, dgains, sums1 = _pre_attn_bwd(qk_pre, [dqa, dka, dva] + dqs + dks + dvs, dgates, x2, dx1, cos_t, sa_t,
                                                sb_t, wts["w_qkv"], wts["w_gates"], gains, g_norm1, sc1,
                                                name="pre_attn_bwd")
    g_qkv = _wgrad(h1, dqkv, name="wg_qkv", tm=d, tn=W_QKV // 2)
    g_gates = _wgrad(h1, dgates, name="wg_gates", tm=d, tn=W_GATES)
    nc, cut = _BIG_SHARD["w_in"][1], 3 * _BIG_SHARD["w_in"][1] - W_QKV
    grads["w_in"] = jnp.stack([g_qkv[:, :nc], g_qkv[:, nc:2 * nc],
                               jnp.concatenate([g_qkv[:, 2 * nc:], g_gates[:, :cut]], axis=1), g_gates[:, cut:]])

    dtab = dtab.reshape(3, NA_HEADS, NA_QROWS, GRID_W, NA_KROWS, GRID_W).transpose(1, 0, 2, 4, 3, 5)
    dtab = jnp.pad(dtab.reshape(NA_HEADS, _RPB_ROWS, GRID_W * GRID_W), ((0, 0), (0, _RPB_ROWS_PAD - _RPB_ROWS), (0, 0)))
    g_rpb = _rpb_reduce(dtab, jnp.asarray(oh_row.T.copy()), jnp.asarray(oh_col), name="rpb_reduce")
    g_rpb = g_rpb[:, :_RPB_RO, :_RPB_CO]

    dmod = jnp.concatenate([sums1[0:1], sums1[1:2], sums2[3:4], sums2[0:1], sums2[1:2], dgt2], axis=1)
    small = dict(g_norm1=sums1[2:3], g_norm2=sums2[2:3], b_gate=dbg, g_qa=dgains[0:1, :HEAD_DIM],
                 g_ka=dgains[1:2, :HEAD_DIM], g_qb=dgains[2:3, :HEAD_DIM], g_kb=dgains[3:4, :HEAD_DIM], rpb=g_rpb)
    return loss_v, grad_x, grads, dmod, small


_SMALL = ("b_ada", "g_norm1", "g_norm2", "b_gate", "g_qa", "g_ka", "g_qb", "g_kb", "rpb")
_SMALL_N = {"b_ada": 6 * D_MODEL, "g_norm1": D_MODEL, "g_norm2": D_MODEL, "b_gate": 2 * D_MODEL, "g_qa": HEAD_DIM,
            "g_ka": HEAD_DIM, "g_qb": HEAD_DIM, "g_kb": HEAD_DIM, "rpb": NA_HEADS * _RPB_RO * _RPB_CO}


def _pack_small(parts):
    flat = [parts[n].reshape(1, _SMALL_N[n]) for n in _SMALL]
    used = sum(_SMALL_N.values())
    return jnp.concatenate(flat + [jnp.zeros((1, STATS_W - used), F32)], axis=1)


def _unpack_small(v, shapes):
    out, at = {}, 0
    for n in _SMALL:
        out[n] = v[:, at:at + _SMALL_N[n]].reshape(shapes[n])
        at += _SMALL_N[n]
    return out


def _join_cols(t):
    _, r, c = t.shape
    return t.transpose(1, 0, 2).reshape(r, N_CHIP * c)


def kernel(x, c, w_ada, b_ada, g_norm1, g_norm2, w_in, b_gate, g_qa, g_ka, g_qb, g_kb, rpb, w_proj_a, w_proj_b, w_o, w_ffn_in, w_ffn_out, loss_target, m_w_ada, m_b_ada, m_g_norm1, m_g_norm2, m_w_in, m_b_gate, m_g_qa, m_g_ka, m_g_qb, m_g_kb, m_rpb, m_w_proj_a, m_w_proj_b, m_w_o, m_w_ffn_in, m_w_ffn_out, v_w_ada, v_b_ada, v_g_norm1, v_g_norm2, v_w_in, v_b_gate, v_g_qa, v_g_ka, v_g_qb, v_g_kb, v_rpb, v_w_proj_a, v_w_proj_b, v_w_o, v_w_ffn_in, v_w_ffn_out):
    names = ("w_ada", "b_ada", "g_norm1", "g_norm2", "w_in", "b_gate", "g_qa", "g_ka", "g_qb", "g_kb", "rpb",
             "w_proj_a", "w_proj_b", "w_o", "w_ffn_in", "w_ffn_out")
    w = dict(zip(names, (w_ada, b_ada, g_norm1, g_norm2, w_in, b_gate, g_qa, g_ka, g_qb, g_kb, rpb, w_proj_a, w_proj_b,
                         w_o, w_ffn_in, w_ffn_out)))
    m = dict(zip(names, (m_w_ada, m_b_ada, m_g_norm1, m_g_norm2, m_w_in, m_b_gate, m_g_qa, m_g_ka, m_g_qb, m_g_kb, m_rpb,
                         m_w_proj_a, m_w_proj_b, m_w_o, m_w_ffn_in, m_w_ffn_out)))
    v = dict(zip(names, (v_w_ada, v_b_ada, v_g_norm1, v_g_norm2, v_w_in, v_b_gate, v_g_qa, v_g_ka, v_g_qb, v_g_kb, v_rpb,
                         v_w_proj_a, v_w_proj_b, v_w_o, v_w_ffn_in, v_w_ffn_out)))
    d = D_MODEL
    xi, yi, ci = _pos()
    chip = 2 * xi + yi
    me = 2 * chip + ci
    ada_cols = 6 * d // N_CHIP

    c_arr, chip_arr = ci.reshape(1).astype(jnp.int32), chip.reshape(1).astype(jnp.int32)
    first, rest = _BIG[:1], _BIG[1:]

    halves = {n: (2, _BIG_SHARD[n][0] // 2, _BIG_SHARD[n][1]) for n in _BIG}
    shards = {n: w[n][0].astype(BF16).reshape(halves[n]) for n in _BIG}
    land = lambda n: lax.empty((N_CHIP,) + halves[n], BF16)
    ag1 = _split_start([shards[n] for n in first], [land(n) for n in first], _ag_plan(1), 4, name="ag1_start")
    ag2 = _split_start([shards[n] for n in rest], [land(n) for n in rest], _ag_plan(len(rest)), 4 * len(rest),
                       name="ag2_start")

    c_all = _small_allgather(jnp.broadcast_to(c + (ag1[4][0, 0] + ag2[4][0, 0]), (8, d)), name="ag_c")[::8]
    b_sh = lax.dynamic_slice(b_ada, (0, chip * ada_cols), (1, ada_cols))
    mod_part = _ada_fwd(c_all, w_ada[0], b_sh, name="ada_fwd")
    mod_all = _small_allgather(mod_part, name="ag_mod").reshape(N_CHIP, 2, 8, ada_cols)[:, 0]
    mod = lax.dynamic_index_in_dim(mod_all, me, axis=1, keepdims=False).reshape(1, 6 * d)

    full1 = _split_wait(ag1[0], ag1[1], ag1[2], ag1[3], _ag_plan(1), mod, name="ag1_wait")
    p_in = _ag_pass(full1, name="ag1_pass")[0].reshape((N_CHIP,) + _BIG_SHARD["w_in"])
    cut = W_QKV - 2 * _BIG_SHARD["w_in"][1]
    wts = dict(w_qkv=jnp.concatenate([p_in[0], p_in[1], p_in[2][:, :cut]], axis=1),
               w_gates=jnp.concatenate([p_in[2][:, cut:], p_in[3]], axis=1))

    def late_weights(after):
        full2 = _split_wait(ag2[0], ag2[1], ag2[2], ag2[3], _ag_plan(len(rest)), after, name="ag2_wait")
        full2 = _ag_pass(full2, name="ag2_pass")
        full = {n: fu.reshape((N_CHIP,) + _BIG_SHARD[n]) for n, fu in zip(rest, full2)}
        return dict(w_pa=_join_cols(full["w_proj_a"]), w_pb=_join_cols(full["w_proj_b"]), w_o=full["w_o"].reshape(d, d),
                    w_ffn_in=full["w_ffn_in"], w_ffn_out=full["w_ffn_out"].reshape(D_FF, d))

    def rs_begin(group, grads, tag):
        gps = [grads[n].reshape((N_CHIP,) + halves[n]) for n in group]
        ras = _sibling_send_halves(gps, name=f"rs_sibling_{tag}")
        sums = [_rs_add(gp, ra, c_arr, name=f"rs_add_{n}") for n, gp, ra in zip(group, gps, ras)]
        lands = [lax.empty((3,) + halves[n][1:], BF16) for n in group]
        st = _split_start([sb for _, sb in sums], lands, _rs_plan(len(group)), 3 * len(group), name=f"rs_{tag}_start")
        return sums, st

    def rs_end(group, begun, after, tag):
        sums, st = begun
        rbs = _split_wait(st[0], st[1], st[2], st[3], _rs_plan(len(group)), after, name=f"rs_{tag}_wait")
        return [_rs_final(sf, rb, chip_arr, name=f"rs_final_{n}") for n, (sf, _), rb in zip(group, sums, rbs)]

    begun = {}

    def early_grads(grads):
        begun["rest"] = rs_begin(rest, grads, "rest")

    loss_v, grad_x, grads, dmod, small = _device_step(
        x[0], loss_target[0], mod, wts, late_weights, early_grads, g_norm1, g_norm2, b_gate, g_qa, g_ka, g_qb, g_kb,
        rpb[0])
    begun["first"] = rs_begin(first, grads, "first")
    ts_rest = rs_end(rest, begun["rest"], begun["first"][0][0][1], "rest")

    stats = _pack_small(dict(b_ada=dmod, **small))
    rows = _small_allgather(jnp.broadcast_to(stats, (8, STATS_W)), name="ag_stats")[::8]
    dmod_sh = lax.dynamic_slice(rows, (0, chip * ada_cols), (8, ada_cols))
    g_ada = _ada_bwd(c_all.T, dmod_sh, name="ada_bwd")
    tot = _row_sum(rows, name="stats_sum")
    g_small = _unpack_small(tot, {n: w[n].shape for n in _SMALL})

    ts = rs_end(first, begun["first"], tot, "first") + ts_rest
    others = _sibling_swap(ts, name="rs_pair")

    g, delta, new_m, new_v = {}, {}, {}, {}
    for n, t, o in zip(_BIG, ts, others):
        gg, dl, nm, nv = _adamw_halves(w[n][0], t, o, m[n][0], v[n][0], c_arr, name=f"adamw_{n}")
        g[n], delta[n], new_m[n], new_v[n] = gg[None], dl[None], nm[None], nv[None]
    dl, nm, nv = _adamw(w_ada[0], g_ada, m_w_ada[0], v_w_ada[0], name="adamw_w_ada")
    g["w_ada"], delta["w_ada"], new_m["w_ada"], new_v["w_ada"] = g_ada[None], dl[None], nm[None], nv[None]
    shapes = {n: w[n].shape for n in _SMALL}
    dl, nm, nv = _adamw(_pack_small({n: w[n] for n in _SMALL}), tot, _pack_small({n: m[n] for n in _SMALL}),
                        _pack_small({n: v[n] for n in _SMALL}), name="adamw_small")
    delta.update(_unpack_small(dl, shapes))
    new_m.update(_unpack_small(nm, shapes))
    new_v.update(_unpack_small(nv, shapes))
    g.update(g_small)

    loss = lax.psum(loss_v[0, 0], ("x", "y", "c"))
    return (loss, grad_x[None], *[g[n] for n in names], *[delta[n] for n in names], *[new_m[n] for n in names],
            *[new_v[n] for n in names])
```

```python
import numpy as np

import jax
import jax.numpy as jnp
from jax import lax
from jax.experimental import pallas as pl
from jax.experimental.pallas import tpu as pltpu

F32 = jnp.float32
BF16 = jnp.bfloat16

D_MODEL = 1024
SEQ = 8192
HEAD_DIM = 64
GRID_W = 64
ROWS = SEQ // GRID_W
NA_HEADS = 8
NA_KH = 8
NA_KW = 16
DIL_CONFIGS = ((128, 1), (512, 4), (2048, 16))
ROT_DIM = 16
ROPE_THETA = 500000.0
D_FF = 2816
EPS = 1e-6
NEG = -1e30
WA = 512
WB = 768
WB_OUT = 256
W_QKV = 3 * WA + 3 * WB
W_QK = 2 * WA + 2 * WB
W_GATES = 2 * D_MODEL
SCALE = HEAD_DIM ** -0.5

ADAM_LR = 0.001
ADAM_B1 = 0.9
ADAM_B2 = 0.999
ADAM_EPS = 1e-08
ADAM_WD = 0.01
ADAM_STEP = 10

LANES = 128
ROW_TILE = 256
Q_BLOCK = 256
NA_QROWS = Q_BLOCK // GRID_W
NA_KROWS = NA_QROWS + NA_KH - 1
NA_NK = NA_KROWS * GRID_W
DIL_HALF = 64
DIL_NK = Q_BLOCK + 2 * DIL_HALF
N_QBLK = SEQ // Q_BLOCK

N_DEV = 8
N_CHIP = 4
FF_CHIP = 2 * D_FF // N_CHIP
STATS_W = 14336


def _pcall(body, *, name, **kw):
    return pl.pallas_call(body, name=name, **kw)


_NT = (((1,), (1,)), ((), ()))
_TN = (((0,), (0,)), ((), ()))
_ARB = pltpu.CompilerParams(dimension_semantics=("arbitrary",))
_PAR = pltpu.CompilerParams(dimension_semantics=("parallel",))


def _dot(a, b):
    return jnp.dot(a, b, preferred_element_type=F32)


def _dot_nt(a, b):
    return lax.dot_general(a, b, _NT, preferred_element_type=F32)


def _wgrad(a, b, *, name, tm, tn, tk=1024, chips=None):
    s, ma = a.shape
    nb = b.shape[1]
    nk = s // tk
    nc = nb // chips if chips else tn
    cpb = tn // nc

    def body(a_ref, b_ref, o_ref, acc):
        k = pl.program_id(2)
        r = lax.dot_general(a_ref[...].astype(BF16), b_ref[...].astype(BF16), _TN, preferred_element_type=F32)

        @pl.when(k == 0)
        def _():
            acc[...] = r

        @pl.when(k > 0)
        def _():
            acc[...] += r

        @pl.when(k == nk - 1)
        def _():
            if chips:
                for q in range(cpb):
                    o_ref[q] = acc[:, q * nc:(q + 1) * nc]
            else:
                o_ref[...] = acc[...]

    if chips:
        o_spec = pl.BlockSpec((cpb, tm, nc), lambda i, j, k: (j, i, 0))
        out_shape = jax.ShapeDtypeStruct((chips, ma, nc), F32)
    else:
        o_spec = pl.BlockSpec((tm, tn), lambda i, j, k: (i, j))
        out_shape = jax.ShapeDtypeStruct((ma, nb), F32)
    return _pcall(
        body, name=name, grid=(ma // tm, nb // tn, nk),
        in_specs=[pl.BlockSpec((tk, tm), lambda i, j, k: (k, i)), pl.BlockSpec((tk, tn), lambda i, j, k: (k, j))],
        out_specs=o_spec, out_shape=out_shape, scratch_shapes=[pltpu.VMEM((tm, tn), F32)],
        compiler_params=pltpu.CompilerParams(dimension_semantics=("parallel", "parallel", "arbitrary")),
    )(a, b)


def _row_call(body, *, name, row_ins, res_ins, row_outs, acc_outs=(), scratch=()):
    s = row_ins[0].shape[0]
    n = s // ROW_TILE
    nri, nre, nro, nao = len(row_ins), len(res_ins), len(row_outs), len(acc_outs)

    def whole(shape):
        nd = len(shape)
        return pl.BlockSpec(tuple(shape), lambda i: (0,) * nd, pipeline_mode=pl.Buffered(1))

    def whole_out(shape):
        nd = len(shape)
        return pl.BlockSpec(tuple(shape), lambda i: (0,) * nd)

    in_specs = [pl.BlockSpec((ROW_TILE, a.shape[1]), lambda i: (i, 0)) for a in row_ins]
    in_specs += [whole(a.shape) for a in res_ins]
    out_specs = [pl.BlockSpec((ROW_TILE, w), lambda i: (i, 0)) for w, _ in row_outs]
    out_specs += [whole_out(shp) for shp, _ in acc_outs]
    out_shape = [jax.ShapeDtypeStruct((s, w), dt) for w, dt in row_outs]
    out_shape += [jax.ShapeDtypeStruct(tuple(shp), dt) for shp, dt in acc_outs]

    def wrapped(*refs):
        at = [0, nri, nri + nre, nri + nre + nro, nri + nre + nro + nao]
        body(pl.program_id(0), n, refs[at[0]:at[1]], refs[at[1]:at[2]], refs[at[2]:at[3]], refs[at[3]:at[4]],
             refs[at[4]:])

    return _pcall(wrapped, name=name, grid=(n,), in_specs=in_specs, out_specs=out_specs, out_shape=out_shape,
                  scratch_shapes=list(scratch), compiler_params=_ARB)(*row_ins, *res_ins)


def _fold8(t):
    r, w = t.shape
    return jnp.sum(t.reshape(r // 8, 8, w), axis=0)


def _sigmoid(t):
    return 1.0 / (1.0 + jnp.exp(-t))


def _head_lanes():
    return lax.broadcasted_iota(jnp.int32, (1, LANES), 1) < HEAD_DIM


def _head_mean(t, lo):
    s_lo = jnp.sum(jnp.where(lo, t, 0.0), axis=1, keepdims=True)
    s_hi = jnp.sum(jnp.where(lo, 0.0, t), axis=1, keepdims=True)
    return jnp.where(lo, s_lo, s_hi) * (1.0 / HEAD_DIM)


def _rms_mod(xv, g, sc, sh):
    rstd = lax.rsqrt(jnp.mean(xv * xv, axis=1, keepdims=True) + EPS)
    return (xv * rstd * g) * (1.0 + sc) + sh


def _rms_mod_bwd(xv, dh, g, sc):
    rstd = lax.rsqrt(jnp.mean(xv * xv, axis=1, keepdims=True) + EPS)
    xhat = xv * rstd
    dn = dh * (1.0 + sc)
    dxhat = dn * g
    dx = rstd * (dxhat - xhat * jnp.mean(dxhat * xhat, axis=1, keepdims=True))
    return dx, dh, dh * (xhat * g), dn * xhat


def _mix_weights(ls):
    m = jnp.maximum(jnp.maximum(ls[0], ls[1]), ls[2])
    es = [jnp.exp(t - m) for t in ls]
    den = es[0] + es[1] + es[2]
    return [e / den for e in es]


def _rope_tables():
    half = ROT_DIM // 2
    inv_freq = ROPE_THETA ** (-(jnp.arange(half, dtype=F32) * 2.0) / ROT_DIM)
    lane = np.arange(LANES) % HEAD_DIM
    ang = jnp.arange(SEQ).astype(F32)[:, None] * jnp.tile(inv_freq, LANES // half)[None, :]
    cos, sin = jnp.cos(ang), jnp.sin(ang)
    first, second = jnp.asarray(lane < half)[None, :], jnp.asarray((lane >= half) & (lane < ROT_DIM))[None, :]
    cos_t = jnp.where(first | second, cos, 1.0)
    return cos_t, jnp.where(second, sin, 0.0), jnp.where(first, -sin, 0.0)


_SECTIONS = ((0, WA, 0, False), (WA, 2 * WA, 1, False), (2 * WA, 3 * WA, -1, False),
             (3 * WA, 3 * WA + WB, 2, True), (3 * WA + WB, 3 * WA + 2 * WB, 3, True), (3 * WA + 2 * WB, W_QKV, -1, False))


def _pre_attn_fwd(x, cos_t, sa_t, sb_t, g1, sc1, sh1, w_qkv, w_gates, gains, *, name):
    half = ROT_DIM // 2

    def body(i, n, rin, res, rout, aout, scr):
        x_ref, cos_ref, sa_ref, sb_ref = rin
        g_ref, sc_ref, sh_ref, wq_ref, wg_ref, gains_ref = res
        h1_ref, qkvn_ref, pre_ref, gates_ref = rout
        hb = _rms_mod(x_ref[...], g_ref[...], sc_ref[...], sh_ref[...]).astype(BF16)
        h1_ref[...] = hb
        gates_ref[...] = _dot(hb, wg_ref[...]).astype(BF16)
        lo = _head_lanes()
        cosv, sav, sbv = cos_ref[...], sa_ref[...], sb_ref[...]
        pre_at = 0
        for c0, c1, kind, rot in _SECTIONS:
            sec = _dot(hb, wq_ref[:, c0:c1])
            for ch in range((c1 - c0) // LANES):
                t = sec[:, ch * LANES:(ch + 1) * LANES]
                if kind >= 0:
                    pre_ref[:, pre_at:pre_at + LANES] = t.astype(BF16)
                    pre_at += LANES
                    t = t * lax.rsqrt(_head_mean(t * t, lo) + EPS) * gains_ref[kind:kind + 1, :]
                    if rot:
                        t = t * cosv + pltpu.roll(t, half, 1) * sav + pltpu.roll(t, LANES - half, 1) * sbv
                qkvn_ref[:, c0 + ch * LANES:c0 + (ch + 1) * LANES] = t.astype(BF16)

    return _row_call(body, name=name, row_ins=[x, cos_t, sa_t, sb_t], res_ins=[g1, sc1, sh1, w_qkv, w_gates, gains],
                     row_outs=[(D_MODEL, BF16), (W_QKV, BF16), (W_QK, BF16), (W_GATES, BF16)])


def _pre_attn_bwd(qk_pre, d_parts, dgates, x, dx1, cos_t, sa_t, sb_t, w_qkv, w_gates, gains, g1, sc1, *, name):
    half = ROT_DIM // 2
    nparts = len(d_parts)
    where = []
    for pi, part in enumerate(d_parts):
        where += [(pi, cj) for cj in range(part.shape[1] // LANES)]
    assert len(where) == W_QKV // LANES

    def body(i, n, rin, res, rout, aout, scr):
        pre_ref, d_refs = rin[0], rin[1:1 + nparts]
        dgates_ref, x_ref, dx1_ref, cos_ref, sa_ref, sb_ref = rin[1 + nparts:]
        wq_ref, wg_ref, gains_ref, g_ref, sc_ref = res
        dqkv_ref, gx_ref = rout
        dgains_ref, sums_ref = aout
        accg, accs = scr

        @pl.when(i == 0)
        def _():
            accg[...] = jnp.zeros_like(accg)
            accs[...] = jnp.zeros_like(accs)

        lo = _head_lanes()
        cosv, sav, sbv = cos_ref[...], sa_ref[...], sb_ref[...]
        dh = _dot_nt(dgates_ref[...], wg_ref[...])
        pre_at = 0
        for c0, c1, kind, rot in _SECTIONS:
            for ch in range((c1 - c0) // LANES):
                pi, cj = where[c0 // LANES + ch]
                dt = d_refs[pi][:, cj * LANES:(cj + 1) * LANES]
                if kind >= 0:
                    if rot:
                        dt = dt * cosv + pltpu.roll(dt * sav, LANES - half, 1) + pltpu.roll(dt * sbv, half, 1)
                    t = pre_ref[:, pre_at:pre_at + LANES].astype(F32)
                    pre_at += LANES
                    rstd = lax.rsqrt(_head_mean(t * t, lo) + EPS)
                    xhat = t * rstd
                    accg[kind] += _fold8(dt * xhat)
                    dxhat = dt * gains_ref[kind:kind + 1, :]
                    dt = rstd * (dxhat - xhat * _head_mean(dxhat * xhat, lo))
                dqkv_ref[:, c0 + ch * LANES:c0 + (ch + 1) * LANES] = dt.astype(BF16)
            dh = dh + _dot_nt(dqkv_ref[:, c0:c1], wq_ref[:, c0:c1])
        dx, t_sh, t_sc, t_g = _rms_mod_bwd(x_ref[...], dh, g_ref[...], sc_ref[...])
        gx_ref[...] = dx1_ref[...] + dx
        accs[0] += _fold8(t_sh)
        accs[1] += _fold8(t_sc)
        accs[2] += _fold8(t_g)

        @pl.when(i == n - 1)
        def _():
            t = jnp.sum(accg[...], axis=1)
            dgains_ref[...] = t + pltpu.roll(t, HEAD_DIM, 1)
            sums_ref[...] = jnp.sum(accs[...], axis=1)

    return _row_call(
        body, name=name, row_ins=[qk_pre, *d_parts, dgates, x, dx1, cos_t, sa_t, sb_t],
        res_ins=[w_qkv, w_gates, gains, g1, sc1], row_outs=[(W_QKV, BF16), (D_MODEL, F32)],
        acc_outs=[((4, LANES), F32), ((3, D_MODEL), F32)],
        scratch=[pltpu.VMEM((4, 8, LANES), F32), pltpu.VMEM((3, 8, D_MODEL), F32)])


def _post_attn_fwd(o_a, o_g, l_g, gates, x, w_pa, w_pb, w_o, b_gate, gt1, g2, sc2, sh2, *, name):
    d = D_MODEL

    def body(i, n, rin, res, rout, aout, scr):
        oa_ref, o0, o1, o2, l0, l1, l2, gates_ref, x_ref = rin
        wpa_ref, wpb_ref, wo_ref, b_ref, gt_ref, g_ref, sc_ref, sh_ref = res
        ob_ref, merged_ref, mo_ref, x1_ref, h2_ref = rout
        ws = _mix_weights([l0[...], l1[...], l2[...]])
        obb = (ws[0] * o0[...] + ws[1] * o1[...] + ws[2] * o2[...]).astype(BF16)
        ob_ref[...] = obb
        pa = _dot(oa_ref[...].astype(BF16), wpa_ref[...])
        pb = _dot(obb, wpb_ref[...])
        ga = _sigmoid(gates_ref[:, :d].astype(F32) + b_ref[:, :d])
        gb = _sigmoid(gates_ref[:, d:].astype(F32) + b_ref[:, d:])
        merged = (ga * pa + gb * pb).astype(BF16)
        merged_ref[...] = merged
        mo = _dot(merged, wo_ref[...])
        mo_ref[...] = mo.astype(BF16)
        x1 = x_ref[...] + gt_ref[...] * mo
        x1_ref[...] = x1
        h2_ref[...] = _rms_mod(x1, g_ref[...], sc_ref[...], sh_ref[...]).astype(BF16)

    return _row_call(body, name=name, row_ins=[o_a, *o_g, *l_g, gates, x],
                     res_ins=[w_pa, w_pb, w_o, b_gate, gt1, g2, sc2, sh2],
                     row_outs=[(WB_OUT, BF16), (d, BF16), (d, BF16), (d, F32), (d, BF16)])


def _ffn_fwd(h2, w_ffn_in, *, name):
    def body(i, n, rin, res, rout, aout, scr):
        (h_ref,), (w_ref,), (act_ref, ff_ref) = rin, res, rout
        hv = h_ref[...]
        for q in range(2):
            a = _dot(hv, w_ref[q])
            up = _dot(hv, w_ref[q + 2])
            sl = slice(q * FF_CHIP, (q + 1) * FF_CHIP)
            act_ref[:, sl] = (a * _sigmoid(a) * up).astype(BF16)
            ff_ref[:, sl] = a.astype(BF16)
            ff_ref[:, D_FF + q * FF_CHIP:D_FF + (q + 1) * FF_CHIP] = up.astype(BF16)

    return _row_call(body, name=name, row_ins=[h2], res_ins=[w_ffn_in], row_outs=[(D_FF, BF16), (2 * D_FF, BF16)])


def _ffn_mid(act, ff, x1, tgt, w_ffn_out, gt2, *, name):
    d = D_MODEL

    def body(i, n, rin, res, rout, aout, scr):
        act_ref, ff_ref, x1_ref, tgt_ref = rin
        wo_ref, gt_ref = res
        dy_ref, dffo_ref, dff_ref = rout
        dgt_ref, loss_ref = aout
        (acc,) = scr

        @pl.when(i == 0)
        def _():
            acc[...] = jnp.zeros_like(acc)

        ffo = _dot(act_ref[...], wo_ref[...])
        gtv = gt_ref[...]
        e = x1_ref[...] + gtv * ffo - tgt_ref[...]
        dy = e * (1.0 / d)
        dy_ref[...] = dy
        dffo = (gtv * dy).astype(BF16)
        dffo_ref[...] = dffo
        acc[0] += _fold8(dy * ffo)
        acc[1] += _fold8(e * e)
        for q in range(2):
            sl = slice(q * FF_CHIP, (q + 1) * FF_CHIP)
            su = slice(D_FF + q * FF_CHIP, D_FF + (q + 1) * FF_CHIP)
            dact = _dot_nt(dffo, wo_ref[sl, :])
            a = ff_ref[:, sl].astype(F32)
            up = ff_ref[:, su].astype(F32)
            sg = _sigmoid(a)
            dff_ref[:, sl] = (dact * up * (sg * (1.0 + a * (1.0 - sg)))).astype(BF16)
            dff_ref[:, su] = (dact * (a * sg)).astype(BF16)

        @pl.when(i == n - 1)
        def _():
            dgt_ref[...] = jnp.sum(acc[0], axis=0, keepdims=True)
            tot = jnp.sum(jnp.sum(acc[1], axis=0, keepdims=True), axis=1, keepdims=True)
            loss_ref[...] = jnp.broadcast_to(tot * (0.5 / d), (1, LANES))

    return _row_call(body, name=name, row_ins=[act, ff, x1, tgt], res_ins=[w_ffn_out, gt2],
                     row_outs=[(d, F32), (d, BF16), (2 * D_FF, BF16)], acc_outs=[((1, d), F32), ((1, LANES), F32)],
                     scratch=[pltpu.VMEM((2, 8, d), F32)])


def _ffn_in_bwd(dff, x1, dy, mo, w_ffn_in, g2, sc2, gt1, *, name):
    d = D_MODEL

    def body(i, n, rin, res, rout, aout, scr):
        dff_ref, x1_ref, dy_ref, mo_ref = rin
        w_ref, g_ref, sc_ref, gt_ref = res
        dx1_ref, dmo_ref = rout
        (sums_ref,) = aout
        (acc,) = scr

        @pl.when(i == 0)
        def _():
            acc[...] = jnp.zeros_like(acc)

        dh = _dot_nt(dff_ref[:, :FF_CHIP], w_ref[0])
        for q in range(1, N_CHIP):
            dh = dh + _dot_nt(dff_ref[:, q * FF_CHIP:(q + 1) * FF_CHIP], w_ref[q])
        dx, t_sh, t_sc, t_g = _rms_mod_bwd(x1_ref[...], dh, g_ref[...], sc_ref[...])
        dx1 = dy_ref[...] + dx
        dx1_ref[...] = dx1
        dmo_ref[...] = (gt_ref[...] * dx1).astype(BF16)
        acc[0] += _fold8(t_sh)
        acc[1] += _fold8(t_sc)
        acc[2] += _fold8(t_g)
        acc[3] += _fold8(dx1 * mo_ref[...].astype(F32))

        @pl.when(i == n - 1)
        def _():
            sums_ref[...] = jnp.sum(acc[...], axis=1)

    return _row_call(body, name=name, row_ins=[dff, x1, dy, mo], res_ins=[w_ffn_in, g2, sc2, gt1],
                     row_outs=[(d, F32), (d, BF16)], acc_outs=[((4, d), F32)], scratch=[pltpu.VMEM((4, 8, d), F32)])


def _post_attn_bwd(dmo, gates, o_a, o_g, l_g, w_pa, w_pb, w_o, b_gate, *, name):
    d = D_MODEL

    def body(i, n, rin, res, rout, aout, scr):
        dmo_ref, gates_ref, oa_ref, o0, o1, o2, l0, l1, l2 = rin
        wpa_ref, wpb_ref, wo_ref, b_ref = res
        dpa_ref, dpb_ref, dgates_ref, doa_ref = rout[:4]
        do_refs, dl_refs = rout[4:7], rout[7:10]
        (dbg_ref,) = aout
        (acc,) = scr

        @pl.when(i == 0)
        def _():
            acc[...] = jnp.zeros_like(acc)

        ogs = [o0[...], o1[...], o2[...]]
        ws = _mix_weights([l0[...], l1[...], l2[...]])
        obb = (ws[0] * ogs[0] + ws[1] * ogs[1] + ws[2] * ogs[2]).astype(BF16)
        pa = _dot(oa_ref[...].astype(BF16), wpa_ref[...])
        pb = _dot(obb, wpb_ref[...])
        ga = _sigmoid(gates_ref[:, :d].astype(F32) + b_ref[:, :d])
        gb = _sigmoid(gates_ref[:, d:].astype(F32) + b_ref[:, d:])
        dm = _dot_nt(dmo_ref[...], wo_ref[...])
        dpa = (dm * ga).astype(BF16)
        dpb = (dm * gb).astype(BF16)
        dpa_ref[...] = dpa
        dpb_ref[...] = dpb
        dga = dm * pa * ga * (1.0 - ga)
        dgb = dm * pb * gb * (1.0 - gb)
        dgates_ref[:, :d] = dga.astype(BF16)
        dgates_ref[:, d:] = dgb.astype(BF16)
        acc[:, :d] += _fold8(dga)
        acc[:, d:] += _fold8(dgb)
        doa_ref[...] = _dot_nt(dpa, wpa_ref[...])
        dob = _dot_nt(dpb, wpb_ref[...])
        lo = _head_lanes()
        for ch in range(WB_OUT // LANES):
            sl = slice(ch * LANES, (ch + 1) * LANES)
            dv = dob[:, sl]
            wc = [w[:, sl] for w in ws]
            ts = [_head_mean(dv * og[:, sl], lo) * float(HEAD_DIM) for og in ogs]
            tbar = wc[0] * ts[0] + wc[1] * ts[1] + wc[2] * ts[2]
            for g in range(3):
                do_refs[g][:, sl] = wc[g] * dv
                dl_refs[g][:, sl] = wc[g] * (ts[g] - tbar)

        @pl.when(i == n - 1)
        def _():
            dbg_ref[...] = jnp.sum(acc[...], axis=0, keepdims=True)

    return _row_call(body, name=name, row_ins=[dmo, gates, o_a, *o_g, *l_g], res_ins=[w_pa, w_pb, w_o, b_gate],
                     row_outs=[(d, BF16), (d, BF16), (2 * d, BF16), (WA, F32)] + [(WB_OUT, F32)] * 6,
                     acc_outs=[((1, 2 * d), F32)], scratch=[pltpu.VMEM((8, 2 * d), F32)])


def _attn_fwd(qkv, qc0, kc0, vc0, npairs, table, kstart, cls, nk, *, name):
    s = qkv.shape[0]
    per_head = table.shape[1] > 1
    hb = 2 if per_head else 1

    def body(ks_ref, cls_ref, q_ref, k_ref, v_ref, b_ref, o_ref, lse_ref):
        i = pl.program_id(1)
        ks = pl.multiple_of(ks_ref[i], 64)
        q2 = q_ref[...]
        k2 = k_ref[pl.ds(ks, nk), :]
        v2 = v_ref[pl.ds(ks, nk), :]
        lo = _head_lanes()
        outs, lses = [], []
        for h in range(2):
            qm = jnp.where(lo if h == 0 else jnp.logical_not(lo), q2, jnp.zeros_like(q2))
            sc = _dot_nt(qm, k2) * SCALE + b_ref[0, h if per_head else 0]
            m = jnp.max(sc, axis=1, keepdims=True)
            p = jnp.exp(sc - m)
            l = jnp.sum(p, axis=1, keepdims=True)
            pv = _dot(p.astype(BF16), v2)
            outs.append(pv / l)
            lses.append(m + jnp.log(l))
        o_ref[...] = jnp.where(lo, outs[0], outs[1])
        lse_ref[...] = jnp.where(lo, lses[0], lses[1])

    w = npairs * LANES
    grid_spec = pltpu.PrefetchScalarGridSpec(
        num_scalar_prefetch=2, grid=(npairs, N_QBLK),
        in_specs=[
            pl.BlockSpec((Q_BLOCK, LANES), lambda p, i, ks, cl: (i, qc0 + p)),
            pl.BlockSpec((s, LANES), lambda p, i, ks, cl: (0, kc0 + p)),
            pl.BlockSpec((s, LANES), lambda p, i, ks, cl: (0, vc0 + p)),
            pl.BlockSpec((1, hb, Q_BLOCK, nk), lambda p, i, ks, cl: (cl[i], p if per_head else 0, 0, 0)),
        ],
        out_specs=[pl.BlockSpec((Q_BLOCK, LANES), lambda p, i, ks, cl: (i, p)),
                   pl.BlockSpec((Q_BLOCK, LANES), lambda p, i, ks, cl: (i, p))],
    )
    return _pcall(body, name=name, grid_spec=grid_spec,
                  out_shape=[jax.ShapeDtypeStruct((s, w), F32), jax.ShapeDtypeStruct((s, w), F32)],
                  compiler_params=pltpu.CompilerParams(dimension_semantics=("parallel", "arbitrary")),
                  )(kstart, cls, qkv, qkv, qkv, table)


def _attn_bwd(qkv, qc0, kc0, vc0, npairs, table, kstart, cls, nk, do, o, lse, *, name, dlse=None, want_dbias=False):
    s = qkv.shape[0]
    per_head = table.shape[1] > 1
    hb = 2 if per_head else 1
    has_dlse = dlse is not None

    def body(ks_ref, cls_ref, q_ref, k_ref, v_ref, b_ref, do_ref, o_ref, lse_ref, *rest):
        if has_dlse:
            dlse_ref, rest = rest[0], rest[1:]
        dq_ref, dk_ref, dv_ref = rest[0], rest[1], rest[2]
        db_ref = rest[3] if want_dbias else None
        i = pl.program_id(1)

        @pl.when(i == 0)
        def _():
            dk_ref[...] = jnp.zeros_like(dk_ref)
            dv_ref[...] = jnp.zeros_like(dv_ref)

        ks = pl.multiple_of(ks_ref[i], 64)
        q2 = q_ref[...]
        k2 = k_ref[pl.ds(ks, nk), :]
        v2 = v_ref[pl.ds(ks, nk), :]
        do2 = do_ref[...]
        lse2 = lse_ref[...]
        doo = do2 * o_ref[...]
        do2b = do2.astype(BF16)
        lo = _head_lanes()
        lane = lax.broadcasted_iota(jnp.int32, (1, LANES), 1)
        if want_dbias:
            first = jnp.logical_or(i == 0, cls_ref[i] != cls_ref[jnp.maximum(i - 1, 0)])
        dqs, dks, dvs = [], [], []
        for h in range(2):
            mh = lo if h == 0 else jnp.logical_not(lo)
            qm = jnp.where(mh, q2, jnp.zeros_like(q2))
            sc = _dot_nt(qm, k2) * SCALE + b_ref[0, h if per_head else 0]
            lse_h = jnp.max(jnp.where(mh, lse2, NEG), axis=1, keepdims=True)
            p = jnp.exp(sc - lse_h)
            delta = jnp.sum(jnp.where(mh, doo, 0.0), axis=1, keepdims=True)
            dom = jnp.where(mh, do2b, jnp.zeros_like(do2b))
            dp = _dot_nt(dom, v2)
            t = dp - delta
            if has_dlse:
                t = t + jnp.sum(jnp.where(lane == h * HEAD_DIM, dlse_ref[...], 0.0), axis=1, keepdims=True)
            ds = p * t
            if want_dbias:
                @pl.when(first)
                def _():
                    db_ref[0, h] = ds

                @pl.when(jnp.logical_not(first))
                def _():
                    db_ref[0, h] += ds
            dsb = ds.astype(BF16)
            dqs.append(_dot(dsb, k2))
            dks.append(lax.dot_general(dsb, q2, _TN, preferred_element_type=F32))
            dvs.append(lax.dot_general(p.astype(BF16), do2b, _TN, preferred_element_type=F32))
        dq_ref[...] = jnp.where(lo, dqs[0], dqs[1]) * SCALE
        dk_ref[pl.ds(ks, nk), :] += jnp.where(lo, dks[0], dks[1]) * SCALE
        dv_ref[pl.ds(ks, nk), :] += jnp.where(lo, dvs[0], dvs[1])

    w = npairs * LANES
    blk = lambda: pl.BlockSpec((Q_BLOCK, LANES), lambda p, i, ks, cl: (i, p))
    full = lambda: pl.BlockSpec((s, LANES), lambda p, i, ks, cl: (0, p))
    tab = lambda: pl.BlockSpec((1, hb, Q_BLOCK, nk), lambda p, i, ks, cl: (cl[i], p if per_head else 0, 0, 0))
    in_specs = [
        pl.BlockSpec((Q_BLOCK, LANES), lambda p, i, ks, cl: (i, qc0 + p)),
        pl.BlockSpec((s, LANES), lambda p, i, ks, cl: (0, kc0 + p)),
        pl.BlockSpec((s, LANES), lambda p, i, ks, cl: (0, vc0 + p)),
        tab(), blk(), blk(), blk(),
    ]
    args = [kstart, cls, qkv, qkv, qkv, table, do, o, lse]
    if has_dlse:
        in_specs.append(blk())
        args.append(dlse)
    out_specs = [blk(), full(), full()]
    out_shape = [jax.ShapeDtypeStruct((s, w), F32)] * 3
    if want_dbias:
        assert per_head
        out_specs.append(tab())
        out_shape.append(jax.ShapeDtypeStruct(table.shape, F32))
    grid_spec = pltpu.PrefetchScalarGridSpec(num_scalar_prefetch=2, grid=(npairs, N_QBLK), in_specs=in_specs,
                                             out_specs=out_specs)
    return _pcall(body, name=name, grid_spec=grid_spec, out_shape=out_shape,
                  compiler_params=pltpu.CompilerParams(dimension_semantics=("arbitrary", "arbitrary")))(*args)


_NA_CLASS_R0 = (0, NA_QROWS, ROWS - NA_QROWS)
_NA_CLASS_K0 = (0, 0, ROWS - NA_KROWS)
_RPB_ROWS = 3 * NA_QROWS * NA_KROWS
_RPB_ROWS_PAD = 136
_RPB_RO = 2 * NA_KH - 1
_RPB_CO = 2 * NA_KW - 1


def _na_constants():
    a = np.arange(NA_QROWS)
    b = np.arange(NA_KROWS)
    col = np.arange(GRID_W)
    oh_row = np.zeros((_RPB_ROWS_PAD, 16), np.float32)
    vrow = np.zeros((3, NA_QROWS, NA_KROWS), bool)
    for t in range(3):
        qr = _NA_CLASS_R0[t] + a
        kr = _NA_CLASS_K0[t] + b
        rs = np.clip(qr - NA_KH // 2, 0, ROWS - NA_KH)
        vrow[t] = (kr[None, :] >= rs[:, None]) & (kr[None, :] < rs[:, None] + NA_KH)
        ro = kr[None, :] - qr[:, None] + (NA_KH - 1)
        for ai in range(NA_QROWS):
            for bi in range(NA_KROWS):
                if vrow[t, ai, bi]:
                    oh_row[(t * NA_QROWS + ai) * NA_KROWS + bi, ro[ai, bi]] = 1.0
    cs = np.clip(col - NA_KW // 2, 0, GRID_W - NA_KW)
    vcol = (col[None, :] >= cs[:, None]) & (col[None, :] < cs[:, None] + NA_KW)
    co = col[None, :] - col[:, None] + (NA_KW - 1)
    oh_col = np.zeros((GRID_W * GRID_W, LANES), np.float32)
    for qc in range(GRID_W):
        for kc in range(GRID_W):
            if vcol[qc, kc]:
                oh_col[qc * GRID_W + kc, co[qc, kc]] = 1.0
    valid = vrow[:, :, None, :, None] & vcol[None, None, :, None, :]
    mask = np.where(valid, 0.0, NEG).astype(np.float32).reshape(3, 1, Q_BLOCK, NA_NK)
    ks = np.clip(np.arange(N_QBLK) * NA_QROWS - NA_KH // 2, 0, ROWS - NA_KROWS) * GRID_W
    cls = np.ones(N_QBLK, np.int32)
    cls[0], cls[-1] = 0, 2
    return oh_row, oh_col, mask, ks.astype(np.int32), cls


def _rpb_expand(rpb_pad, oh_row, oh_col_t, *, name):
    def body(r_ref, ohr_ref, ohc_ref, o_ref):
        t = jnp.dot(ohr_ref[...], r_ref[0], preferred_element_type=F32, precision=lax.Precision.HIGHEST)
        o_ref[0] = jnp.dot(t, ohc_ref[...], preferred_element_type=F32, precision=lax.Precision.HIGHEST)

    return _pcall(
        body, name=name, grid=(NA_HEADS,),
        in_specs=[pl.BlockSpec((1, 16, LANES), lambda h: (h, 0, 0)), pl.BlockSpec((_RPB_ROWS_PAD, 16), lambda h: (0, 0)),
                  pl.BlockSpec((LANES, GRID_W * GRID_W), lambda h: (0, 0))],
        out_specs=pl.BlockSpec((1, _RPB_ROWS_PAD, GRID_W * GRID_W), lambda h: (h, 0, 0)),
        out_shape=jax.ShapeDtypeStruct((NA_HEADS, _RPB_ROWS_PAD, GRID_W * GRID_W), F32), compiler_params=_PAR,
    )(rpb_pad, oh_row, oh_col_t)


def _rpb_reduce(dx, oh_row_t, oh_col, *, name):
    def body(d_ref, ohr_ref, ohc_ref, o_ref):
        t = jnp.dot(d_ref[0], ohc_ref[...], preferred_element_type=F32, precision=lax.Precision.HIGHEST)
        o_ref[0] = jnp.dot(ohr_ref[...], t, preferred_element_type=F32, precision=lax.Precision.HIGHEST)

    return _pcall(
        body, name=name, grid=(NA_HEADS,),
        in_specs=[pl.BlockSpec((1, _RPB_ROWS_PAD, GRID_W * GRID_W), lambda h: (h, 0, 0)),
                  pl.BlockSpec((16, _RPB_ROWS_PAD), lambda h: (0, 0)),
                  pl.BlockSpec((GRID_W * GRID_W, LANES), lambda h: (0, 0))],
        out_specs=pl.BlockSpec((1, 16, LANES), lambda h: (h, 0, 0)),
        out_shape=jax.ShapeDtypeStruct((NA_HEADS, 16, LANES), F32), compiler_params=_PAR,
    )(dx, oh_row_t, oh_col)


def _dil_constants(dilation):
    seg = SEQ // dilation
    nb = seg // Q_BLOCK
    shift = (0, -DIL_HALF, -2 * DIL_HALF)
    qi = np.arange(Q_BLOCK)[:, None]
    ki = np.arange(DIL_NK)[None, :]
    mask = np.stack([np.where(np.abs(ki + sh - qi) <= DIL_HALF, 0.0, NEG) for sh in shift]).astype(np.float32)
    ks, cls = [], []
    for i in range(N_QBLK):
        sub, blk = divmod(i, nb)
        t = 0 if blk == 0 else (2 if blk == nb - 1 else 1)
        cls.append(t)
        ks.append(sub * seg + blk * Q_BLOCK + shift[t])
    return mask.reshape(3, 1, Q_BLOCK, DIL_NK), np.asarray(ks, np.int32), np.asarray(cls, np.int32)


_VM = pl.BlockSpec(memory_space=pltpu.VMEM)


def _ada_fwd(c_all, w, b, *, name):
    def body(c_ref, w_ref, b_ref, o_ref):
        cv = c_ref[...]
        o_ref[...] = jnp.dot(cv * _sigmoid(cv), w_ref[...], preferred_element_type=F32,
                             precision=lax.Precision.HIGHEST) + b_ref[...]

    return _pcall(body, name=name, in_specs=[_VM, _VM, _VM], out_specs=_VM,
                  out_shape=jax.ShapeDtypeStruct((c_all.shape[0], w.shape[1]), F32))(c_all, w, b)


def _ada_bwd(c_all_t, dmod, *, name):
    def body(c_ref, d_ref, o_ref):
        cv = c_ref[...]
        o_ref[...] = jnp.dot(cv * _sigmoid(cv), d_ref[...], preferred_element_type=F32,
                             precision=lax.Precision.HIGHEST)

    return _pcall(body, name=name, in_specs=[_VM, _VM], out_specs=_VM,
                  out_shape=jax.ShapeDtypeStruct((c_all_t.shape[0], dmod.shape[1]), F32))(c_all_t, dmod)


def _row_sum(t, *, name):
    def body(t_ref, o_ref):
        o_ref[...] = jnp.sum(t_ref[...], axis=0, keepdims=True)

    return _pcall(body, name=name, in_specs=[_VM], out_specs=_VM,
                  out_shape=jax.ShapeDtypeStruct((1, t.shape[1]), F32))(t)


def _row_tile(rows):
    tr = rows
    for cand in range(8, 513, 8):
        if rows % cand == 0:
            tr = cand
    return tr


def _adamw_math(wv, gv, mv, vv):
    nm = ADAM_B1 * mv + (1.0 - ADAM_B1) * gv
    nv = ADAM_B2 * vv + (1.0 - ADAM_B2) * (gv * gv)
    m_hat = nm / (1.0 - ADAM_B1 ** ADAM_STEP)
    v_hat = nv / (1.0 - ADAM_B2 ** ADAM_STEP)
    return -ADAM_LR * (m_hat / (jnp.sqrt(v_hat) + ADAM_EPS) + ADAM_WD * wv), nm, nv


def _adamw(w, g, m, v, *, name):
    rows, cols = w.shape
    tr = _row_tile(rows)

    def body(w_ref, g_ref, m_ref, v_ref, d_ref, nm_ref, nv_ref):
        d_ref[...], nm_ref[...], nv_ref[...] = _adamw_math(w_ref[...], g_ref[...], m_ref[...], v_ref[...])

    spec = pl.BlockSpec((tr, cols), lambda i: (i, 0))
    return _pcall(body, name=name, grid=(rows // tr,), in_specs=[spec] * 4, out_specs=[spec] * 3,
                  out_shape=[jax.ShapeDtypeStruct((rows, cols), F32)] * 3, compiler_params=_PAR)(w, g, m, v)


def _adamw_halves(w, g_mine, g_other, m, v, c_arr, *, name):
    rows, cols = w.shape
    hr = rows // 2
    tr = _row_tile(hr)
    nt = hr // tr

    def body(c_ref, w_ref, t_ref, o_ref, m_ref, v_ref, g_ref, d_ref, nm_ref, nv_ref):
        gv = jnp.where(pl.program_id(0) == c_ref[0], t_ref[...], o_ref[...])
        g_ref[...] = gv
        d_ref[...], nm_ref[...], nv_ref[...] = _adamw_math(w_ref[...], gv, m_ref[...], v_ref[...])

    full = pl.BlockSpec((tr, cols), lambda h, i, c: (h * nt + i, 0))
    half = pl.BlockSpec((tr, cols), lambda h, i, c: (i, 0))
    grid_spec = pltpu.PrefetchScalarGridSpec(num_scalar_prefetch=1, grid=(2, nt),
                                             in_specs=[full, half, half, full, full], out_specs=[full] * 4)
    return _pcall(body, name=name, grid_spec=grid_spec, out_shape=[jax.ShapeDtypeStruct((rows, cols), F32)] * 4,
                  compiler_params=pltpu.CompilerParams(dimension_semantics=("parallel", "parallel")),
                  )(c_arr, w, g_mine, g_other, m, v)


_MESH = pl.DeviceIdType.MESH
_ANY = pl.BlockSpec(memory_space=pl.ANY)
_CHIP_FLIPS = ((1, 0), (0, 1), (1, 1))


def _pos():
    return lax.axis_index("x"), lax.axis_index("y"), lax.axis_index("c")


def _flip(v, f):
    return 1 - v if f else v


def _sem_pairs(n):
    return [pltpu.SemaphoreType.DMA((n,)), pltpu.SemaphoreType.DMA((n,))]


def _small_allgather(blk, *, name):
    m_per, n = blk.shape

    def body(x_ref, out_ref, send_sems, recv_sems, local_sem):
        x, y, c = _pos()
        me, sibling = (x, y, c), (x, y, 1 - c)
        chips = [(_flip(x, fx), _flip(y, fy)) for fx, fy in _CHIP_FLIPS]

        def rows(px, py, pc):
            return out_ref.at[pl.ds((4 * px + 2 * py + pc) * m_per, m_per), :]

        def copy(k, block, to, src=None):
            return pltpu.make_async_remote_copy(
                src_ref=rows(*block) if src is None else src, dst_ref=rows(*block),
                send_sem=send_sems.at[k], recv_sem=recv_sems.at[k], device_id=to, device_id_type=_MESH)

        mine = pltpu.make_async_copy(x_ref, rows(*me), local_sem)
        mine.start()
        first = [copy(0, me, sibling, src=x_ref)]
        first += [copy(1 + j, me, (*chip, c), src=x_ref) for j, chip in enumerate(chips)]
        for cp in first:
            cp.start()
        passed = [copy(4 + j, (*chip, c), sibling) for j, chip in enumerate(chips)]
        for j, chip in enumerate(chips):
            copy(1 + j, (*chip, c), me).wait_recv()
            passed[j].start()
        copy(0, sibling, me).wait_recv()
        for j, chip in enumerate(chips):
            copy(4 + j, (*chip, 1 - c), me).wait_recv()
        for cp in first + passed:
            cp.wait_send()
        mine.wait()

    return _pcall(
        body, name=name, out_shape=jax.ShapeDtypeStruct((N_DEV * m_per, n), blk.dtype),
        in_specs=[_VM], out_specs=_VM,
        scratch_shapes=_sem_pairs(7) + [pltpu.SemaphoreType.DMA],
    )(blk)


_HBM = pl.BlockSpec(memory_space=pltpu.HBM)
_SEM = pl.BlockSpec(memory_space=pltpu.SEMAPHORE)
_EFFECT = pltpu.SideEffectType.DATAFLOW_SIDE_EFFECTING


def _split_start(srcs, lands, plan, ncopies, after, *, name):
    ns, nl = len(srcs), len(lands)

    def body(*refs):
        src_refs, land_refs = refs[:ns], refs[ns:ns + nl]
        send_sems, recv_sems = refs[ns + nl + 1], refs[ns + nl + 2]
        token = refs[-1]
        x, y, c = _pos()
        for k, (src, dst, to, _) in enumerate(plan(x, y, c, src_refs, land_refs)):
            pltpu.make_async_remote_copy(src_ref=src, dst_ref=dst, send_sem=send_sems.at[k], recv_sem=recv_sems.at[k],
                                         device_id=to, device_id_type=_MESH).start()
        token[...] = jnp.zeros_like(token)

    hbm = lambda a: pltpu.HBM(a.shape, a.dtype)
    out = _pcall(
        body, name=name,
        out_shape=(pltpu.SemaphoreType.DMA((ncopies,)), pltpu.SemaphoreType.DMA((ncopies,)),
                   *[hbm(a) for a in srcs], *[hbm(a) for a in lands], jax.ShapeDtypeStruct((8, LANES), F32)),
        in_specs=[_HBM] * (ns + nl) + [_ANY], out_specs=(_SEM, _SEM, *[_HBM] * (ns + nl), _VM),
        input_output_aliases={i: 2 + i for i in range(ns + nl)},
        compiler_params=pltpu.CompilerParams(has_side_effects=_EFFECT),
    )(*[pltpu.with_memory_space_constraint(a, pltpu.HBM) for a in (*srcs, *lands)], after)
    return out[0], out[1], list(out[2:2 + ns]), list(out[2 + ns:2 + ns + nl]), out[-1]


def _split_wait(send_sems, recv_sems, srcs, lands, plan, after, *, name):
    ns, nl = len(srcs), len(lands)

    def body(*refs):
        src_refs, land_refs = refs[:ns], refs[ns:ns + nl]
        send_sems, recv_sems = refs[ns + nl], refs[ns + nl + 1]
        x, y, c = _pos()
        for k, (src, _, _, mine) in enumerate(plan(x, y, c, src_refs, land_refs)):
            cp = pltpu.make_async_remote_copy(src_ref=src, dst_ref=mine, send_sem=send_sems.at[k],
                                              recv_sem=recv_sems.at[k], device_id=(x, y, c), device_id_type=_MESH)
            cp.wait_send()
            cp.wait_recv()

    hbm = lambda a: pltpu.HBM(a.shape, a.dtype)
    out = _pcall(
        body, name=name, out_shape=tuple(hbm(a) for a in (*srcs, *lands)),
        in_specs=[_HBM] * (ns + nl) + [_SEM, _SEM, _ANY], out_specs=tuple([_HBM] * (ns + nl)),
        input_output_aliases={i: i for i in range(ns + nl)},
        compiler_params=pltpu.CompilerParams(has_side_effects=_EFFECT),
    )(*srcs, *lands, send_sems, recv_sems, after)
    return list(out[ns:])


def _ag_plan(nw):
    def plan(x, y, c, sh_refs, full_refs):
        j = 2 * x + y
        out = []
        for wi in range(nw):
            for fx, fy in _CHIP_FLIPS:
                px, py = _flip(x, fx), _flip(y, fy)
                out.append((sh_refs[wi].at[c], full_refs[wi].at[j, c], (px, py, c), full_refs[wi].at[2 * px + py, c]))
            out.append((sh_refs[wi], full_refs[wi].at[j], (x, y, 1 - c), full_refs[wi].at[j]))
        return out
    return plan


def _ag_pass(fulls, *, name):
    nw = len(fulls)

    def body(*refs):
        in_refs, out_refs = refs[:nw], refs[nw:2 * nw]
        send_sems, recv_sems = refs[2 * nw:]
        x, y, c = _pos()
        cps = []
        for wi in range(nw):
            for k, (fx, fy) in enumerate(_CHIP_FLIPS):
                jp = 2 * _flip(x, fx) + _flip(y, fy)
                sems = dict(send_sem=send_sems.at[3 * wi + k], recv_sem=recv_sems.at[3 * wi + k], device_id_type=_MESH)
                send = pltpu.make_async_remote_copy(src_ref=in_refs[wi].at[jp, c], dst_ref=out_refs[wi].at[jp, c],
                                                    device_id=(x, y, 1 - c), **sems)
                recv = pltpu.make_async_remote_copy(src_ref=in_refs[wi].at[jp, c], dst_ref=out_refs[wi].at[jp, 1 - c],
                                                    device_id=(x, y, c), **sems)
                cps.append((send, recv))
        for send, _ in cps:
            send.start()
        for send, recv in cps:
            send.wait_send()
            recv.wait_recv()

    return _pcall(body, name=name, out_shape=[jax.ShapeDtypeStruct(f.shape, f.dtype) for f in fulls],
                  in_specs=[_ANY] * nw, out_specs=[_ANY] * nw, input_output_aliases={i: i for i in range(nw)},
                  scratch_shapes=_sem_pairs(3 * nw))(*fulls)


def _rs_plan(nw):
    def plan(x, y, c, s_refs, rb_refs):
        out = []
        for wi in range(nw):
            for k, (fx, fy) in enumerate(_CHIP_FLIPS):
                px, py = _flip(x, fx), _flip(y, fy)
                out.append((s_refs[wi].at[2 * px + py], rb_refs[wi].at[k], (px, py, c), rb_refs[wi].at[k]))
        return out
    return plan


def _sibling_send_halves(gs, *, name):
    nw = len(gs)

    def body(*refs):
        g_refs, out_refs = refs[:nw], refs[nw:2 * nw]
        send_sems, recv_sems = refs[2 * nw:]
        x, y, c = _pos()
        cps = [pltpu.make_async_remote_copy(src_ref=g_refs[wi].at[k, 1 - c], dst_ref=out_refs[wi].at[k],
                                            send_sem=send_sems.at[4 * wi + k], recv_sem=recv_sems.at[4 * wi + k],
                                            device_id=(x, y, 1 - c), device_id_type=_MESH)
               for wi in range(nw) for k in range(N_CHIP)]
        for cp in cps:
            cp.start()
        for cp in cps:
            cp.wait()

    return _pcall(body, name=name,
                  out_shape=[jax.ShapeDtypeStruct((g.shape[0],) + g.shape[2:], g.dtype) for g in gs],
                  in_specs=[_ANY] * nw, out_specs=[_ANY] * nw, scratch_shapes=_sem_pairs(N_CHIP * nw))(*gs)


def _sibling_swap(ts, *, name):
    nw = len(ts)

    def body(*refs):
        t_refs, out_refs = refs[:nw], refs[nw:2 * nw]
        send_sems, recv_sems = refs[2 * nw:]
        x, y, c = _pos()
        cps = [pltpu.make_async_remote_copy(src_ref=t_refs[wi], dst_ref=out_refs[wi], send_sem=send_sems.at[wi],
                                            recv_sem=recv_sems.at[wi], device_id=(x, y, 1 - c), device_id_type=_MESH)
               for wi in range(nw)]
        for cp in cps:
            cp.start()
        for cp in cps:
            cp.wait()

    return _pcall(body, name=name, out_shape=[jax.ShapeDtypeStruct(t.shape, t.dtype) for t in ts],
                  in_specs=[_ANY] * nw, out_specs=[_ANY] * nw, scratch_shapes=_sem_pairs(nw))(*ts)


def _rs_add(g, ra, c_arr, *, name):
    n, _, r, w = g.shape

    def body(c_ref, g_ref, ra_ref, s_ref, sb_ref):
        t = g_ref[...] + ra_ref[...]
        s_ref[...] = t
        sb_ref[...] = t.astype(BF16)

    grid_spec = pltpu.PrefetchScalarGridSpec(
        num_scalar_prefetch=1, grid=(n,),
        in_specs=[pl.BlockSpec((None, None, r, w), lambda k, c: (k, c[0], 0, 0)),
                  pl.BlockSpec((None, r, w), lambda k, c: (k, 0, 0))],
        out_specs=[pl.BlockSpec((None, r, w), lambda k, c: (k, 0, 0))] * 2)
    return _pcall(body, name=name, grid_spec=grid_spec,
                  out_shape=[jax.ShapeDtypeStruct((n, r, w), F32), jax.ShapeDtypeStruct((n, r, w), BF16)],
                  compiler_params=_PAR)(c_arr, g, ra)


def _rs_final(s, rb, j_arr, *, name):
    _, r, w = s.shape

    def body(j_ref, s_ref, rb_ref, t_ref):
        t_ref[...] = ((s_ref[...] + rb_ref[0].astype(F32)) + rb_ref[1].astype(F32)) + rb_ref[2].astype(F32)

    grid_spec = pltpu.PrefetchScalarGridSpec(
        num_scalar_prefetch=1, grid=(1,),
        in_specs=[pl.BlockSpec((None, r, w), lambda i, j: (j[0], 0, 0)),
                  pl.BlockSpec((3, r, w), lambda i, j: (0, 0, 0))],
        out_specs=pl.BlockSpec((r, w), lambda i, j: (0, 0)))
    return _pcall(body, name=name, grid_spec=grid_spec, out_shape=jax.ShapeDtypeStruct((r, w), F32),
                  compiler_params=_ARB)(j_arr, s, rb)


def _perm_rows(t, d):
    s, w = t.shape
    return t.reshape(s // d, d, w).transpose(1, 0, 2).reshape(s, w)


def _unperm_rows(t, d):
    s, w = t.shape
    return t.reshape(d, s // d, w).transpose(1, 0, 2).reshape(s, w)


def _tile2(g):
    return jnp.concatenate([g, g], axis=1)


_BIG = ("w_in", "w_ffn_in", "w_ffn_out", "w_o", "w_proj_a", "w_proj_b")
_BIG_SHARD = {"w_in": (1024, 1472), "w_ffn_in": (1024, 1408), "w_ffn_out": (704, 1024), "w_o": (256, 1024),
              "w_proj_a": (512, 256), "w_proj_b": (256, 256)}


def _device_step(x2, tgt, mod, first_weights, late_weights, early_grads, g_norm1, g_norm2, b_gate, g_qa, g_ka, g_qb,
                 g_kb, rpb):
    d = D_MODEL
    sh1, sc1, gt1, sh2, sc2, gt2 = [mod[:, k * d:(k + 1) * d] for k in range(6)]

    oh_row, oh_col, na_mask, na_ks, na_cls = _na_constants()
    rpb_pad = jnp.pad(rpb, ((0, 0), (0, 16 - _RPB_RO), (0, LANES - _RPB_CO)))
    tab = _rpb_expand(rpb_pad, jnp.asarray(oh_row), jnp.asarray(oh_col.T.copy()), name="rpb_expand")
    tab = tab[:, :_RPB_ROWS].reshape(NA_HEADS, 3, NA_QROWS, NA_KROWS, GRID_W, GRID_W)
    tab_a = tab.transpose(1, 0, 2, 4, 3, 5).reshape(3, NA_HEADS, Q_BLOCK, NA_NK) + jnp.asarray(na_mask)
    na_ks, na_cls = jnp.asarray(na_ks), jnp.asarray(na_cls)
    dil = [_dil_constants(dd) for _, dd in DIL_CONFIGS]
    tab_d = jnp.asarray(dil[0][0])
    gains = jnp.concatenate([_tile2(g_qa), _tile2(g_ka), _tile2(g_qb), _tile2(g_kb)], axis=0)
    cos_t, sa_t, sb_t = _rope_tables()

    wts = first_weights(tab_a)
    h1, qkvn, qk_pre, gates = _pre_attn_fwd(x2, cos_t, sa_t, sb_t, g_norm1, sc1, sh1, wts["w_qkv"], wts["w_gates"],
                                            gains, name="pre_attn_fwd")
    o_a, lse_a = _attn_fwd(qkvn, 0, 4, 8, 4, tab_a, na_ks, na_cls, NA_NK, name="attn_a_fwd")
    arrs, o_p, l_p, o_g, l_g = [], [], [], [], []
    for g, (_, dd) in enumerate(DIL_CONFIGS):
        ks_g, cls_g = jnp.asarray(dil[g][1]), jnp.asarray(dil[g][2])
        if dd == 1:
            arr, cb = qkvn, (12, 18, 24)
        else:
            col = lambda base: qkvn[:, base + WB_OUT * g: base + WB_OUT * (g + 1)]
            arr = _perm_rows(jnp.concatenate([col(3 * WA), col(3 * WA + WB), col(3 * WA + 2 * WB)], axis=1), dd)
            cb = (0, 2, 4)
        op, lp = _attn_fwd(arr, cb[0], cb[1], cb[2], 2, tab_d, ks_g, cls_g, DIL_NK, name=f"attn_d{g}_fwd")
        arrs.append((arr, cb, ks_g, cls_g))
        o_p.append(op)
        l_p.append(lp)
        o_g.append(op if dd == 1 else _unperm_rows(op, dd))
        l_g.append(lp if dd == 1 else _unperm_rows(lp, dd))
    wts = dict(wts, **late_weights(o_a))
    o_b, merged, mo, x1, h2 = _post_attn_fwd(o_a, o_g, l_g, gates, x2, wts["w_pa"], wts["w_pb"], wts["w_o"], b_gate,
                                             gt1, g_norm2, sc2, sh2, name="post_attn_fwd")
    act, ff = _ffn_fwd(h2, wts["w_ffn_in"], name="ffn_fwd")

    dy, dffo, dff, dgt2, loss_v = _ffn_mid(act, ff, x1, tgt, wts["w_ffn_out"], gt2, name="ffn_mid")
    grads = {}
    g_ffn_out = _wgrad(act, dffo, name="wg_ffn_out", tm=D_FF // 2, tn=d)
    grads["w_ffn_out"] = g_ffn_out.reshape(N_CHIP, D_FF // N_CHIP, d)
    grads["w_ffn_in"] = _wgrad(h2, dff, name="wg_ffn_in", tm=512, tn=2 * FF_CHIP, chips=N_CHIP)
    dx1, dmo, sums2 = _ffn_in_bwd(dff, x1, dy, mo, wts["w_ffn_in"], g_norm2, sc2, gt1, name="ffn_in_bwd")
    grads["w_o"] = _wgrad(merged, dmo, name="wg_o", tm=d, tn=d).reshape(N_CHIP, d // N_CHIP, d)
    pab = _post_attn_bwd(dmo, gates, o_a, o_g, l_g, wts["w_pa"], wts["w_pb"], wts["w_o"], b_gate, name="post_attn_bwd")
    dpa, dpb, dgates, do_a = pab[:4]
    do_g, dl_g, dbg = pab[4:7], pab[7:10], pab[10]
    g_pa = _wgrad(o_a, dpa, name="wg_pa", tm=WA, tn=d)
    g_pb = _wgrad(o_b, dpb, name="wg_pb", tm=WB_OUT, tn=d)
    grads["w_proj_a"] = g_pa.reshape(WA, N_CHIP, d // N_CHIP).transpose(1, 0, 2)
    grads["w_proj_b"] = g_pb.reshape(WB_OUT, N_CHIP, d // N_CHIP).transpose(1, 0, 2)
    early_grads(grads)
    dqa, dka, dva, dtab = _attn_bwd(qkvn, 0, 4, 8, 4, tab_a, na_ks, na_cls, NA_NK, do_a, o_a, lse_a,
                                    name="attn_a_bwd", want_dbias=True)
    dqs, dks, dvs = [], [], []
    for g, (_, dd) in enumerate(DIL_CONFIGS):
        arr, cb, ks_g, cls_g = arrs[g]
        dog = do_g[g] if dd == 1 else _perm_rows(do_g[g], dd)
        dlg = dl_g[g] if dd == 1 else _perm_rows(dl_g[g], dd)
        dq, dk, dv = _attn_bwd(arr, cb[0], cb[1], cb[2], 2, tab_d, ks_g, cls_g, DIL_NK, dog, o_p[g], l_p[g],
                               name=f"attn_d{g}_bwd", dlse=dlg)
        if dd != 1:
            dq, dk, dv = _unperm_rows(dq, dd), _unperm_rows(dk, dd), _unperm_rows(dv, dd)
        dqs.append(dq)
        dks.append(dk)
        dvs.append(dv)
    dqkv, grad_x, dgains, sums1 = _pre_attn_bwd(qk_pre, [dqa, dka, dva] + dqs + dks + dvs, dgates, x2, dx1, cos_t, sa_t,
                                                sb_t, wts["w_qkv"], wts["w_gates"], gains, g_norm1, sc1,
                                                name="pre_attn_bwd")
    g_qkv = _wgrad(h1, dqkv, name="wg_qkv", tm=d, tn=W_QKV // 2)
    g_gates = _wgrad(h1, dgates, name="wg_gates", tm=d, tn=W_GATES)
    nc, cut = _BIG_SHARD["w_in"][1], 3 * _BIG_SHARD["w_in"][1] - W_QKV
    grads["w_in"] = jnp.stack([g_qkv[:, :nc], g_qkv[:, nc:2 * nc],
                               jnp.concatenate([g_qkv[:, 2 * nc:], g_gates[:, :cut]], axis=1), g_gates[:, cut:]])

    dtab = dtab.reshape(3, NA_HEADS, NA_QROWS, GRID_W, NA_KROWS, GRID_W).transpose(1, 0, 2, 4, 3, 5)
    dtab = jnp.pad(dtab.reshape(NA_HEADS, _RPB_ROWS, GRID_W * GRID_W), ((0, 0), (0, _RPB_ROWS_PAD - _RPB_ROWS), (0, 0)))
    g_rpb = _rpb_reduce(dtab, jnp.asarray(oh_row.T.copy()), jnp.asarray(oh_col), name="rpb_reduce")
    g_rpb = g_rpb[:, :_RPB_RO, :_RPB_CO]

    dmod = jnp.concatenate([sums1[0:1], sums1[1:2], sums2[3:4], sums2[0:1], sums2[1:2], dgt2], axis=1)
    small = dict(g_norm1=sums1[2:3], g_norm2=sums2[2:3], b_gate=dbg, g_qa=dgains[0:1, :HEAD_DIM],
                 g_ka=dgains[1:2, :HEAD_DIM], g_qb=dgains[2:3, :HEAD_DIM], g_kb=dgains[3:4, :HEAD_DIM], rpb=g_rpb)
    return loss_v, grad_x, grads, dmod, small


_SMALL = ("b_ada", "g_norm1", "g_norm2", "b_gate", "g_qa", "g_ka", "g_qb", "g_kb", "rpb")
_SMALL_N = {"b_ada": 6 * D_MODEL, "g_norm1": D_MODEL, "g_norm2": D_MODEL, "b_gate": 2 * D_MODEL, "g_qa": HEAD_DIM,
            "g_ka": HEAD_DIM, "g_qb": HEAD_DIM, "g_kb": HEAD_DIM, "rpb": NA_HEADS * _RPB_RO * _RPB_CO}


def _pack_small(parts):
    flat = [parts[n].reshape(1, _SMALL_N[n]) for n in _SMALL]
    used = sum(_SMALL_N.values())
    return jnp.concatenate(flat + [jnp.zeros((1, STATS_W - used), F32)], axis=1)


def _unpack_small(v, shapes):
    out, at = {}, 0
    for n in _SMALL:
        out[n] = v[:, at:at + _SMALL_N[n]].reshape(shapes[n])
        at += _SMALL_N[n]
    return out


def _join_cols(t):
    _, r, c = t.shape
    return t.transpose(1, 0, 2).reshape(r, N_CHIP * c)


def kernel(x, c, w_ada, b_ada, g_norm1, g_norm2, w_in, b_gate, g_qa, g_ka, g_qb, g_kb, rpb, w_proj_a, w_proj_b, w_o, w_ffn_in, w_ffn_out, loss_target, m_w_ada, m_b_ada, m_g_norm1, m_g_norm2, m_w_in, m_b_gate, m_g_qa, m_g_ka, m_g_qb, m_g_kb, m_rpb, m_w_proj_a, m_w_proj_b, m_w_o, m_w_ffn_in, m_w_ffn_out, v_w_ada, v_b_ada, v_g_norm1, v_g_norm2, v_w_in, v_b_gate, v_g_qa, v_g_ka, v_g_qb, v_g_kb, v_rpb, v_w_proj_a, v_w_proj_b, v_w_o, v_w_ffn_in, v_w_ffn_out):
    names = ("w_ada", "b_ada", "g_norm1", "g_norm2", "w_in", "b_gate", "g_qa", "g_ka", "g_qb", "g_kb", "rpb",
             "w_proj_a", "w_proj_b", "w_o", "w_ffn_in", "w_ffn_out")
    w = dict(zip(names, (w_ada, b_ada, g_norm1, g_norm2, w_in, b_gate, g_qa, g_ka, g_qb, g_kb, rpb, w_proj_a, w_proj_b,
                         w_o, w_ffn_in, w_ffn_out)))
    m = dict(zip(names, (m_w_ada, m_b_ada, m_g_norm1, m_g_norm2, m_w_in, m_b_gate, m_g_qa, m_g_ka, m_g_qb, m_g_kb, m_rpb,
                         m_w_proj_a, m_w_proj_b, m_w_o, m_w_ffn_in, m_w_ffn_out)))
    v = dict(zip(names, (v_w_ada, v_b_ada, v_g_norm1, v_g_norm2, v_w_in, v_b_gate, v_g_qa, v_g_ka, v_g_qb, v_g_kb, v_rpb,
                         v_w_proj_a, v_w_proj_b, v_w_o, v_w_ffn_in, v_w_ffn_out)))
    d = D_MODEL
    xi, yi, ci = _pos()
    chip = 2 * xi + yi
    me = 2 * chip + ci
    ada_cols = 6 * d // N_CHIP

    c_arr, chip_arr = ci.reshape(1).astype(jnp.int32), chip.reshape(1).astype(jnp.int32)
    first, rest = _BIG[:1], _BIG[1:]

    c_all = _small_allgather(jnp.broadcast_to(c, (8, d)), name="ag_c")[::8]
    b_sh = lax.dynamic_slice(b_ada, (0, chip * ada_cols), (1, ada_cols))
    mod_part = _ada_fwd(c_all, w_ada[0], b_sh, name="ada_fwd")
    mod_all = _small_allgather(mod_part, name="ag_mod").reshape(N_CHIP, 2, 8, ada_cols)[:, 0]
    mod = lax.dynamic_index_in_dim(mod_all, me, axis=1, keepdims=False).reshape(1, 6 * d)

    halves = {n: (2, _BIG_SHARD[n][0] // 2, _BIG_SHARD[n][1]) for n in _BIG}
    shards = {n: w[n][0].astype(BF16).reshape(halves[n]) for n in _BIG}
    land = lambda n: lax.empty((N_CHIP,) + halves[n], BF16)
    ag1 = _split_start([shards[n] for n in first], [land(n) for n in first], _ag_plan(1), 4, mod, name="ag1_start")
    ag2 = _split_start([shards[n] for n in rest], [land(n) for n in rest], _ag_plan(len(rest)), 4 * len(rest),
                       ag1[4], name="ag2_start")
    rpb_after = rpb[0] + ag2[4][0, 0]

    def first_weights(after):
        full1 = _split_wait(ag1[0], ag1[1], ag1[2], ag1[3], _ag_plan(1), after, name="ag1_wait")
        p_in = _ag_pass(full1, name="ag1_pass")[0].reshape((N_CHIP,) + _BIG_SHARD["w_in"])
        cut = W_QKV - 2 * _BIG_SHARD["w_in"][1]
        return dict(w_qkv=jnp.concatenate([p_in[0], p_in[1], p_in[2][:, :cut]], axis=1),
                    w_gates=jnp.concatenate([p_in[2][:, cut:], p_in[3]], axis=1))

    def late_weights(after):
        full2 = _split_wait(ag2[0], ag2[1], ag2[2], ag2[3], _ag_plan(len(rest)), after, name="ag2_wait")
        full2 = _ag_pass(full2, name="ag2_pass")
        full = {n: fu.reshape((N_CHIP,) + _BIG_SHARD[n]) for n, fu in zip(rest, full2)}
        return dict(w_pa=_join_cols(full["w_proj_a"]), w_pb=_join_cols(full["w_proj_b"]), w_o=full["w_o"].reshape(d, d),
                    w_ffn_in=full["w_ffn_in"], w_ffn_out=full["w_ffn_out"].reshape(D_FF, d))

    def rs_begin(group, grads, tag):
        gps = [grads[n].reshape((N_CHIP,) + halves[n]) for n in group]
        ras = _sibling_send_halves(gps, name=f"rs_sibling_{tag}")
        sums = [_rs_add(gp, ra, c_arr, name=f"rs_add_{n}") for n, gp, ra in zip(group, gps, ras)]
        lands = [lax.empty((3,) + halves[n][1:], BF16) for n in group]
        st = _split_start([sb for _, sb in sums], lands, _rs_plan(len(group)), 3 * len(group), sums[0][0],
                          name=f"rs_{tag}_start")
        return sums, st

    def rs_end(group, begun, after, tag):
        sums, st = begun
        rbs = _split_wait(st[0], st[1], st[2], st[3], _rs_plan(len(group)), after, name=f"rs_{tag}_wait")
        return [_rs_final(sf, rb, chip_arr, name=f"rs_final_{n}") for n, (sf, _), rb in zip(group, sums, rbs)]

    begun = {}

    def early_grads(grads):
        begun["rest"] = rs_begin(rest, grads, "rest")

    loss_v, grad_x, grads, dmod, small = _device_step(
        x[0], loss_target[0], mod, first_weights, late_weights, early_grads, g_norm1, g_norm2, b_gate, g_qa, g_ka, g_qb,
        g_kb, rpb_after)
    begun["first"] = rs_begin(first, grads, "first")
    ts_rest = rs_end(rest, begun["rest"], begun["first"][0][0][1], "rest")

    stats = _pack_small(dict(b_ada=dmod, **small))
    rows = _small_allgather(jnp.broadcast_to(stats, (8, STATS_W)), name="ag_stats")[::8]
    dmod_sh = lax.dynamic_slice(rows, (0, chip * ada_cols), (8, ada_cols))
    g_ada = _ada_bwd(c_all.T, dmod_sh, name="ada_bwd")
    tot = _row_sum(rows, name="stats_sum")
    g_small = _unpack_small(tot, {n: w[n].shape for n in _SMALL})

    ts = rs_end(first, begun["first"], tot, "first") + ts_rest
    others = _sibling_swap(ts, name="rs_pair")

    g, delta, new_m, new_v = {}, {}, {}, {}
    for n, t, o in zip(_BIG, ts, others):
        gg, dl, nm, nv = _adamw_halves(w[n][0], t, o, m[n][0], v[n][0], c_arr, name=f"adamw_{n}")
        g[n], delta[n], new_m[n], new_v[n] = gg[None], dl[None], nm[None], nv[None]
    dl, nm, nv = _adamw(w_ada[0], g_ada, m_w_ada[0], v_w_ada[0], name="adamw_w_ada")
    g["w_ada"], delta["w_ada"], new_m["w_ada"], new_v["w_ada"] = g_ada[None], dl[None], nm[None], nv[None]
    shapes = {n: w[n].shape for n in _SMALL}
    dl, nm, nv = _adamw(_pack_small({n: w[n] for n in _SMALL}), tot, _pack_small({n: m[n] for n in _SMALL}),
                        _pack_small({n: v[n] for n in _SMALL}), name="adamw_small")
    delta.update(_unpack_small(dl, shapes))
    new_m.update(_unpack_small(nm, shapes))
    new_v.update(_unpack_small(nv, shapes))
    g.update(g_small)

    loss = lax.psum(loss_v[0, 0], ("x", "y", "c"))
    return (loss, grad_x[None], *[g[n] for n in names], *[delta[n] for n in names], *[new_m[n] for n in names],
            *[new_v[n] for n in names])
```

```python
import numpy as np

import jax
import jax.numpy as jnp
from jax import lax
from jax.experimental import pallas as pl
from jax.experimental.pallas import tpu as pltpu

F32 = jnp.float32
BF16 = jnp.bfloat16

D_MODEL = 1024
SEQ = 8192
HEAD_DIM = 64
GRID_W = 64
ROWS = SEQ // GRID_W
NA_HEADS = 8
NA_KH = 8
NA_KW = 16
DIL_CONFIGS = ((128, 1), (512, 4), (2048, 16))
ROT_DIM = 16
ROPE_THETA = 500000.0
D_FF = 2816
EPS = 1e-6
NEG = -1e30
WA = 512
WB = 768
WB_OUT = 256
W_QKV = 3 * WA + 3 * WB
W_QK = 2 * WA + 2 * WB
W_GATES = 2 * D_MODEL
SCALE = HEAD_DIM ** -0.5

ADAM_LR = 0.001
ADAM_B1 = 0.9
ADAM_B2 = 0.999
ADAM_EPS = 1e-08
ADAM_WD = 0.01
ADAM_STEP = 10

LANES = 128
ROW_TILE = 256
Q_BLOCK = 256
NA_QROWS = Q_BLOCK // GRID_W
NA_KROWS = NA_QROWS + NA_KH - 1
NA_NK = NA_KROWS * GRID_W
NA_PAIRS = (NA_KROWS + 1) // 2
NA_W = NA_PAIRS * LANES
NA_RO_NONE = 15
NA_SLOTS = 21
RP_LANE0 = GRID_W - NA_KW
DIL_HALF = 64
DIL_NK = Q_BLOCK + 2 * DIL_HALF
N_QBLK = SEQ // Q_BLOCK

N_DEV = 8
N_CHIP = 4
FF_CHIP = 2 * D_FF // N_CHIP
STATS_W = 14336


def _pcall(body, *, name, **kw):
    return pl.pallas_call(body, name=name, **kw)


_NT = (((1,), (1,)), ((), ()))
_TN = (((0,), (0,)), ((), ()))
_ARB = pltpu.CompilerParams(dimension_semantics=("arbitrary",))
_PAR = pltpu.CompilerParams(dimension_semantics=("parallel",))


def _dot(a, b):
    return jnp.dot(a, b, preferred_element_type=F32)


def _dot_nt(a, b):
    return lax.dot_general(a, b, _NT, preferred_element_type=F32)


def _wgrad(a, b, *, name, tm, tn, tk=1024, chips=None):
    s, ma = a.shape
    nb = b.shape[1]
    nk = s // tk
    nc = nb // chips if chips else tn
    cpb = tn // nc

    def body(a_ref, b_ref, o_ref, acc):
        k = pl.program_id(2)
        r = lax.dot_general(a_ref[...].astype(BF16), b_ref[...].astype(BF16), _TN, preferred_element_type=F32)

        @pl.when(k == 0)
        def _():
            acc[...] = r

        @pl.when(k > 0)
        def _():
            acc[...] += r

        @pl.when(k == nk - 1)
        def _():
            if chips:
                for q in range(cpb):
                    o_ref[q] = acc[:, q * nc:(q + 1) * nc]
            else:
                o_ref[...] = acc[...]

    if chips:
        o_spec = pl.BlockSpec((cpb, tm, nc), lambda i, j, k: (j, i, 0))
        out_shape = jax.ShapeDtypeStruct((chips, ma, nc), F32)
    else:
        o_spec = pl.BlockSpec((tm, tn), lambda i, j, k: (i, j))
        out_shape = jax.ShapeDtypeStruct((ma, nb), F32)
    return _pcall(
        body, name=name, grid=(ma // tm, nb // tn, nk),
        in_specs=[pl.BlockSpec((tk, tm), lambda i, j, k: (k, i)), pl.BlockSpec((tk, tn), lambda i, j, k: (k, j))],
        out_specs=o_spec, out_shape=out_shape, scratch_shapes=[pltpu.VMEM((tm, tn), F32)],
        compiler_params=pltpu.CompilerParams(dimension_semantics=("parallel", "parallel", "arbitrary")),
    )(a, b)


def _row_call(body, *, name, row_ins, res_ins, row_outs, acc_outs=(), scratch=()):
    s = row_ins[0].shape[0]
    n = s // ROW_TILE
    nri, nre, nro, nao = len(row_ins), len(res_ins), len(row_outs), len(acc_outs)

    def whole(shape):
        nd = len(shape)
        return pl.BlockSpec(tuple(shape), lambda i: (0,) * nd, pipeline_mode=pl.Buffered(1))

    def whole_out(shape):
        nd = len(shape)
        return pl.BlockSpec(tuple(shape), lambda i: (0,) * nd)

    in_specs = [pl.BlockSpec((ROW_TILE, a.shape[1]), lambda i: (i, 0)) for a in row_ins]
    in_specs += [whole(a.shape) for a in res_ins]
    out_specs = [pl.BlockSpec((ROW_TILE, w), lambda i: (i, 0)) for w, _ in row_outs]
    out_specs += [whole_out(shp) for shp, _ in acc_outs]
    out_shape = [jax.ShapeDtypeStruct((s, w), dt) for w, dt in row_outs]
    out_shape += [jax.ShapeDtypeStruct(tuple(shp), dt) for shp, dt in acc_outs]

    def wrapped(*refs):
        at = [0, nri, nri + nre, nri + nre + nro, nri + nre + nro + nao]
        body(pl.program_id(0), n, refs[at[0]:at[1]], refs[at[1]:at[2]], refs[at[2]:at[3]], refs[at[3]:at[4]],
             refs[at[4]:])

    return _pcall(wrapped, name=name, grid=(n,), in_specs=in_specs, out_specs=out_specs, out_shape=out_shape,
                  scratch_shapes=list(scratch), compiler_params=_ARB)(*row_ins, *res_ins)


def _fold8(t):
    r, w = t.shape
    return jnp.sum(t.reshape(r // 8, 8, w), axis=0)


def _sigmoid(t):
    return 1.0 / (1.0 + jnp.exp(-t))


def _head_lanes():
    return lax.broadcasted_iota(jnp.int32, (1, LANES), 1) < HEAD_DIM


def _head_mean(t, lo):
    s_lo = jnp.sum(jnp.where(lo, t, 0.0), axis=1, keepdims=True)
    s_hi = jnp.sum(jnp.where(lo, 0.0, t), axis=1, keepdims=True)
    return jnp.where(lo, s_lo, s_hi) * (1.0 / HEAD_DIM)


def _rms_mod(xv, g, sc, sh):
    rstd = lax.rsqrt(jnp.mean(xv * xv, axis=1, keepdims=True) + EPS)
    return (xv * rstd * g) * (1.0 + sc) + sh


def _rms_mod_bwd(xv, dh, g, sc):
    rstd = lax.rsqrt(jnp.mean(xv * xv, axis=1, keepdims=True) + EPS)
    xhat = xv * rstd
    dn = dh * (1.0 + sc)
    dxhat = dn * g
    dx = rstd * (dxhat - xhat * jnp.mean(dxhat * xhat, axis=1, keepdims=True))
    return dx, dh, dh * (xhat * g), dn * xhat


def _mix_weights(ls):
    m = jnp.maximum(jnp.maximum(ls[0], ls[1]), ls[2])
    es = [jnp.exp(t - m) for t in ls]
    den = es[0] + es[1] + es[2]
    return [e / den for e in es]


def _rope_tables():
    half = ROT_DIM // 2
    inv_freq = ROPE_THETA ** (-(jnp.arange(half, dtype=F32) * 2.0) / ROT_DIM)
    lane = np.arange(LANES) % HEAD_DIM
    ang = jnp.arange(SEQ).astype(F32)[:, None] * jnp.tile(inv_freq, LANES // half)[None, :]
    cos, sin = jnp.cos(ang), jnp.sin(ang)
    first, second = jnp.asarray(lane < half)[None, :], jnp.asarray((lane >= half) & (lane < ROT_DIM))[None, :]
    cos_t = jnp.where(first | second, cos, 1.0)
    return cos_t, jnp.where(second, sin, 0.0), jnp.where(first, -sin, 0.0)


_SECTIONS = ((0, WA, 0, False), (WA, 2 * WA, 1, False), (2 * WA, 3 * WA, -1, False),
             (3 * WA, 3 * WA + WB, 2, True), (3 * WA + WB, 3 * WA + 2 * WB, 3, True), (3 * WA + 2 * WB, W_QKV, -1, False))


def _pre_attn_fwd(x, cos_t, sa_t, sb_t, g1, sc1, sh1, w_qkv, w_gates, gains, *, name):
    half = ROT_DIM // 2

    def body(i, n, rin, res, rout, aout, scr):
        x_ref, cos_ref, sa_ref, sb_ref = rin
        g_ref, sc_ref, sh_ref, wq_ref, wg_ref, gains_ref = res
        h1_ref, qkvn_ref, pre_ref, gates_ref = rout
        hb = _rms_mod(x_ref[...], g_ref[...], sc_ref[...], sh_ref[...]).astype(BF16)
        h1_ref[...] = hb
        gates_ref[...] = _dot(hb, wg_ref[...]).astype(BF16)
        lo = _head_lanes()
        cosv, sav, sbv = cos_ref[...], sa_ref[...], sb_ref[...]
        pre_at = 0
        for c0, c1, kind, rot in _SECTIONS:
            sec = _dot(hb, wq_ref[:, c0:c1])
            for ch in range((c1 - c0) // LANES):
                t = sec[:, ch * LANES:(ch + 1) * LANES]
                if kind >= 0:
                    pre_ref[:, pre_at:pre_at + LANES] = t.astype(BF16)
                    pre_at += LANES
                    t = t * lax.rsqrt(_head_mean(t * t, lo) + EPS) * gains_ref[kind:kind + 1, :]
                    if rot:
                        t = t * cosv + pltpu.roll(t, half, 1) * sav + pltpu.roll(t, LANES - half, 1) * sbv
                qkvn_ref[:, c0 + ch * LANES:c0 + (ch + 1) * LANES] = t.astype(BF16)

    return _row_call(body, name=name, row_ins=[x, cos_t, sa_t, sb_t], res_ins=[g1, sc1, sh1, w_qkv, w_gates, gains],
                     row_outs=[(D_MODEL, BF16), (W_QKV, BF16), (W_QK, BF16), (W_GATES, BF16)])


def _pre_attn_bwd(qk_pre, d_parts, dgates, x, dx1, cos_t, sa_t, sb_t, w_qkv, w_gates, gains, g1, sc1, *, name):
    half = ROT_DIM // 2
    nparts = len(d_parts)
    where = []
    for pi, part in enumerate(d_parts):
        where += [(pi, cj) for cj in range(part.shape[1] // LANES)]
    assert len(where) == W_QKV // LANES

    def body(i, n, rin, res, rout, aout, scr):
        pre_ref, d_refs = rin[0], rin[1:1 + nparts]
        dgates_ref, x_ref, dx1_ref, cos_ref, sa_ref, sb_ref = rin[1 + nparts:]
        wq_ref, wg_ref, gains_ref, g_ref, sc_ref = res
        dqkv_ref, gx_ref = rout
        dgains_ref, sums_ref = aout
        accg, accs = scr

        @pl.when(i == 0)
        def _():
            accg[...] = jnp.zeros_like(accg)
            accs[...] = jnp.zeros_like(accs)

        lo = _head_lanes()
        cosv, sav, sbv = cos_ref[...], sa_ref[...], sb_ref[...]
        dh = _dot_nt(dgates_ref[...], wg_ref[...])
        pre_at = 0
        for c0, c1, kind, rot in _SECTIONS:
            for ch in range((c1 - c0) // LANES):
                pi, cj = where[c0 // LANES + ch]
                dt = d_refs[pi][:, cj * LANES:(cj + 1) * LANES]
                if kind >= 0:
                    if rot:
                        dt = dt * cosv + pltpu.roll(dt * sav, LANES - half, 1) + pltpu.roll(dt * sbv, half, 1)
                    t = pre_ref[:, pre_at:pre_at + LANES].astype(F32)
                    pre_at += LANES
                    rstd = lax.rsqrt(_head_mean(t * t, lo) + EPS)
                    xhat = t * rstd
                    accg[kind] += _fold8(dt * xhat)
                    dxhat = dt * gains_ref[kind:kind + 1, :]
                    dt = rstd * (dxhat - xhat * _head_mean(dxhat * xhat, lo))
                dqkv_ref[:, c0 + ch * LANES:c0 + (ch + 1) * LANES] = dt.astype(BF16)
            dh = dh + _dot_nt(dqkv_ref[:, c0:c1], wq_ref[:, c0:c1])
        dx, t_sh, t_sc, t_g = _rms_mod_bwd(x_ref[...], dh, g_ref[...], sc_ref[...])
        gx_ref[...] = dx1_ref[...] + dx
        accs[0] += _fold8(t_sh)
        accs[1] += _fold8(t_sc)
        accs[2] += _fold8(t_g)

        @pl.when(i == n - 1)
        def _():
            t = jnp.sum(accg[...], axis=1)
            dgains_ref[...] = t + pltpu.roll(t, HEAD_DIM, 1)
            sums_ref[...] = jnp.sum(accs[...], axis=1)

    return _row_call(
        body, name=name, row_ins=[qk_pre, *d_parts, dgates, x, dx1, cos_t, sa_t, sb_t],
        res_ins=[w_qkv, w_gates, gains, g1, sc1], row_outs=[(W_QKV, BF16), (D_MODEL, F32)],
        acc_outs=[((4, LANES), F32), ((3, D_MODEL), F32)],
        scratch=[pltpu.VMEM((4, 8, LANES), F32), pltpu.VMEM((3, 8, D_MODEL), F32)])


def _post_attn_fwd(o_a, o_g, l_g, gates, x, w_pa, w_pb, w_o, b_gate, gt1, g2, sc2, sh2, *, name):
    d = D_MODEL

    def body(i, n, rin, res, rout, aout, scr):
        oa_ref, o0, o1, o2, l0, l1, l2, gates_ref, x_ref = rin
        wpa_ref, wpb_ref, wo_ref, b_ref, gt_ref, g_ref, sc_ref, sh_ref = res
        ob_ref, merged_ref, mo_ref, x1_ref, h2_ref = rout
        ws = _mix_weights([l0[...], l1[...], l2[...]])
        obb = (ws[0] * o0[...] + ws[1] * o1[...] + ws[2] * o2[...]).astype(BF16)
        ob_ref[...] = obb
        pa = _dot(oa_ref[...].astype(BF16), wpa_ref[...])
        pb = _dot(obb, wpb_ref[...])
        ga = _sigmoid(gates_ref[:, :d].astype(F32) + b_ref[:, :d])
        gb = _sigmoid(gates_ref[:, d:].astype(F32) + b_ref[:, d:])
        merged = (ga * pa + gb * pb).astype(BF16)
        merged_ref[...] = merged
        mo = _dot(merged, wo_ref[...])
        mo_ref[...] = mo.astype(BF16)
        x1 = x_ref[...] + gt_ref[...] * mo
        x1_ref[...] = x1
        h2_ref[...] = _rms_mod(x1, g_ref[...], sc_ref[...], sh_ref[...]).astype(BF16)

    return _row_call(body, name=name, row_ins=[o_a, *o_g, *l_g, gates, x],
                     res_ins=[w_pa, w_pb, w_o, b_gate, gt1, g2, sc2, sh2],
                     row_outs=[(WB_OUT, BF16), (d, BF16), (d, BF16), (d, F32), (d, BF16)])


def _ffn_fwd(h2, w_ffn_in, *, name):
    def body(i, n, rin, res, rout, aout, scr):
        (h_ref,), (w_ref,), (act_ref, ff_ref) = rin, res, rout
        hv = h_ref[...]
        for q in range(2):
            a = _dot(hv, w_ref[q])
            up = _dot(hv, w_ref[q + 2])
            sl = slice(q * FF_CHIP, (q + 1) * FF_CHIP)
            act_ref[:, sl] = (a * _sigmoid(a) * up).astype(BF16)
            ff_ref[:, sl] = a.astype(BF16)
            ff_ref[:, D_FF + q * FF_CHIP:D_FF + (q + 1) * FF_CHIP] = up.astype(BF16)

    return _row_call(body, name=name, row_ins=[h2], res_ins=[w_ffn_in], row_outs=[(D_FF, BF16), (2 * D_FF, BF16)])


def _ffn_mid(act, ff, x1, tgt, w_ffn_out, gt2, *, name):
    d = D_MODEL

    def body(i, n, rin, res, rout, aout, scr):
        act_ref, ff_ref, x1_ref, tgt_ref = rin
        wo_ref, gt_ref = res
        dy_ref, dffo_ref, dff_ref = rout
        dgt_ref, loss_ref = aout
        (acc,) = scr

        @pl.when(i == 0)
        def _():
            acc[...] = jnp.zeros_like(acc)

        ffo = _dot(act_ref[...], wo_ref[...])
        gtv = gt_ref[...]
        e = x1_ref[...] + gtv * ffo - tgt_ref[...]
        dy = e * (1.0 / d)
        dy_ref[...] = dy
        dffo = (gtv * dy).astype(BF16)
        dffo_ref[...] = dffo
        acc[0] += _fold8(dy * ffo)
        acc[1] += _fold8(e * e)
        for q in range(2):
            sl = slice(q * FF_CHIP, (q + 1) * FF_CHIP)
            su = slice(D_FF + q * FF_CHIP, D_FF + (q + 1) * FF_CHIP)
            dact = _dot_nt(dffo, wo_ref[sl, :])
            a = ff_ref[:, sl].astype(F32)
            up = ff_ref[:, su].astype(F32)
            sg = _sigmoid(a)
            dff_ref[:, sl] = (dact * up * (sg * (1.0 + a * (1.0 - sg)))).astype(BF16)
            dff_ref[:, su] = (dact * (a * sg)).astype(BF16)

        @pl.when(i == n - 1)
        def _():
            dgt_ref[...] = jnp.sum(acc[0], axis=0, keepdims=True)
            tot = jnp.sum(jnp.sum(acc[1], axis=0, keepdims=True), axis=1, keepdims=True)
            loss_ref[...] = jnp.broadcast_to(tot * (0.5 / d), (1, LANES))

    return _row_call(body, name=name, row_ins=[act, ff, x1, tgt], res_ins=[w_ffn_out, gt2],
                     row_outs=[(d, F32), (d, BF16), (2 * D_FF, BF16)], acc_outs=[((1, d), F32), ((1, LANES), F32)],
                     scratch=[pltpu.VMEM((2, 8, d), F32)])


def _ffn_in_bwd(dff, x1, dy, mo, w_ffn_in, g2, sc2, gt1, *, name):
    d = D_MODEL

    def body(i, n, rin, res, rout, aout, scr):
        dff_ref, x1_ref, dy_ref, mo_ref = rin
        w_ref, g_ref, sc_ref, gt_ref = res
        dx1_ref, dmo_ref = rout
        (sums_ref,) = aout
        (acc,) = scr

        @pl.when(i == 0)
        def _():
            acc[...] = jnp.zeros_like(acc)

        dh = _dot_nt(dff_ref[:, :FF_CHIP], w_ref[0])
        for q in range(1, N_CHIP):
            dh = dh + _dot_nt(dff_ref[:, q * FF_CHIP:(q + 1) * FF_CHIP], w_ref[q])
        dx, t_sh, t_sc, t_g = _rms_mod_bwd(x1_ref[...], dh, g_ref[...], sc_ref[...])
        dx1 = dy_ref[...] + dx
        dx1_ref[...] = dx1
        dmo_ref[...] = (gt_ref[...] * dx1).astype(BF16)
        acc[0] += _fold8(t_sh)
        acc[1] += _fold8(t_sc)
        acc[2] += _fold8(t_g)
        acc[3] += _fold8(dx1 * mo_ref[...].astype(F32))

        @pl.when(i == n - 1)
        def _():
            sums_ref[...] = jnp.sum(acc[...], axis=1)

    return _row_call(body, name=name, row_ins=[dff, x1, dy, mo], res_ins=[w_ffn_in, g2, sc2, gt1],
                     row_outs=[(d, F32), (d, BF16)], acc_outs=[((4, d), F32)], scratch=[pltpu.VMEM((4, 8, d), F32)])


def _post_attn_bwd(dmo, gates, o_a, o_g, l_g, w_pa, w_pb, w_o, b_gate, *, name):
    d = D_MODEL

    def body(i, n, rin, res, rout, aout, scr):
        dmo_ref, gates_ref, oa_ref, o0, o1, o2, l0, l1, l2 = rin
        wpa_ref, wpb_ref, wo_ref, b_ref = res
        dpa_ref, dpb_ref, dgates_ref, doa_ref = rout[:4]
        do_refs, dl_refs = rout[4:7], rout[7:10]
        (dbg_ref,) = aout
        (acc,) = scr

        @pl.when(i == 0)
        def _():
            acc[...] = jnp.zeros_like(acc)

        ogs = [o0[...], o1[...], o2[...]]
        ws = _mix_weights([l0[...], l1[...], l2[...]])
        obb = (ws[0] * ogs[0] + ws[1] * ogs[1] + ws[2] * ogs[2]).astype(BF16)
        pa = _dot(oa_ref[...].astype(BF16), wpa_ref[...])
        pb = _dot(obb, wpb_ref[...])
        ga = _sigmoid(gates_ref[:, :d].astype(F32) + b_ref[:, :d])
        gb = _sigmoid(gates_ref[:, d:].astype(F32) + b_ref[:, d:])
        dm = _dot_nt(dmo_ref[...], wo_ref[...])
        dpa = (dm * ga).astype(BF16)
        dpb = (dm * gb).astype(BF16)
        dpa_ref[...] = dpa
        dpb_ref[...] = dpb
        dga = dm * pa * ga * (1.0 - ga)
        dgb = dm * pb * gb * (1.0 - gb)
        dgates_ref[:, :d] = dga.astype(BF16)
        dgates_ref[:, d:] = dgb.astype(BF16)
        acc[:, :d] += _fold8(dga)
        acc[:, d:] += _fold8(dgb)
        doa_ref[...] = _dot_nt(dpa, wpa_ref[...])
        dob = _dot_nt(dpb, wpb_ref[...])
        lo = _head_lanes()
        for ch in range(WB_OUT // LANES):
            sl = slice(ch * LANES, (ch + 1) * LANES)
            dv = dob[:, sl]
            wc = [w[:, sl] for w in ws]
            ts = [_head_mean(dv * og[:, sl], lo) * float(HEAD_DIM) for og in ogs]
            tbar = wc[0] * ts[0] + wc[1] * ts[1] + wc[2] * ts[2]
            for g in range(3):
                do_refs[g][:, sl] = wc[g] * dv
                dl_refs[g][:, sl] = wc[g] * (ts[g] - tbar)

        @pl.when(i == n - 1)
        def _():
            dbg_ref[...] = jnp.sum(acc[...], axis=0, keepdims=True)

    return _row_call(body, name=name, row_ins=[dmo, gates, o_a, *o_g, *l_g], res_ins=[w_pa, w_pb, w_o, b_gate],
                     row_outs=[(d, BF16), (d, BF16), (2 * d, BF16), (WA, F32)] + [(WB_OUT, F32)] * 6,
                     acc_outs=[((1, 2 * d), F32)], scratch=[pltpu.VMEM((8, 2 * d), F32)])


def _na_class_tables():
    ro = np.full((3, NA_QROWS, 2 * NA_PAIRS), NA_RO_NONE, np.int64)
    slot = np.zeros((3, NA_QROWS, NA_PAIRS), np.int64)
    for t in range(3):
        for a in range(NA_QROWS):
            qr = _NA_CLASS_R0[t] + a
            rs = min(max(qr - NA_KH // 2, 0), ROWS - NA_KH)
            for b in range(NA_KROWS):
                kr = _NA_CLASS_K0[t] + b
                if rs <= kr < rs + NA_KH:
                    ro[t, a, b] = kr - qr + (NA_KH - 1)
            for j in range(NA_PAIRS):
                slot[t, a, j] = 2 * j - a + (_NA_CLASS_K0[t] - _NA_CLASS_R0[t] + NA_KH - 1) + (NA_QROWS - 1)
    assert slot.min() >= 0 and slot.max() < NA_SLOTS
    return ro, slot


def _na_build_bias(i, cls_ref, rp_ref, cm_ref, bias_scr):
    ro, _ = _na_class_tables()
    lo = _head_lanes()
    first = jnp.logical_or(i == 0, cls_ref[i] != cls_ref[jnp.maximum(i - 1, 0)])
    for t in range(3):
        @pl.when(jnp.logical_and(first, cls_ref[i] == t))
        def _():
            for hh in range(2):
                for a in range(NA_QROWS):
                    for j in range(NA_PAIRS):
                        r0, r1 = int(ro[t, a, 2 * j]), int(ro[t, a, 2 * j + 1])
                        x0 = jnp.broadcast_to(rp_ref[hh, r0:r0 + 1, :], (GRID_W, LANES))
                        x1 = jnp.broadcast_to(rp_ref[hh, r1:r1 + 1, :], (GRID_W, LANES))
                        blk = jnp.where(lo, pltpu.roll(x0, GRID_W + 1, 1, stride=1, stride_axis=0),
                                        pltpu.roll(x1, 1, 1, stride=1, stride_axis=0))
                        bias_scr[hh, a * GRID_W:(a + 1) * GRID_W, j * LANES:(j + 1) * LANES] = blk + cm_ref[...]
    return first


def _attn_fwd(qkv, qc0, kc0, vc0, npairs, table, kstart, cls, nk, *, name, na=None):
    s = qkv.shape[0]

    def body(ks_ref, cls_ref, q_ref, k_ref, v_ref, b_ref, *rest):
        if na:
            cm_ref, o_ref, lse_ref, bias_scr = rest
        else:
            o_ref, lse_ref = rest
        i = pl.program_id(1)
        if na:
            _na_build_bias(i, cls_ref, b_ref, cm_ref, bias_scr)
        ks = pl.multiple_of(ks_ref[i], 64)
        q2 = q_ref[...]
        k2 = k_ref[pl.ds(ks, nk), :]
        v2 = v_ref[pl.ds(ks, nk), :]
        lo = _head_lanes()
        outs, lses = [], []
        for h in range(2):
            qm = jnp.where(lo if h == 0 else jnp.logical_not(lo), q2, jnp.zeros_like(q2))
            sc = _dot_nt(qm, k2) * SCALE + (bias_scr[h, :, :nk] if na else b_ref[0, 0])
            m = jnp.max(sc, axis=1, keepdims=True)
            p = jnp.exp(sc - m)
            l = jnp.sum(p, axis=1, keepdims=True)
            pv = _dot(p.astype(BF16), v2)
            outs.append(pv / l)
            lses.append(m + jnp.log(l))
        o_ref[...] = jnp.where(lo, outs[0], outs[1])
        lse_ref[...] = jnp.where(lo, lses[0], lses[1])

    w = npairs * LANES
    in_specs = [
        pl.BlockSpec((Q_BLOCK, LANES), lambda p, i, ks, cl: (i, qc0 + p)),
        pl.BlockSpec((s, LANES), lambda p, i, ks, cl: (0, kc0 + p)),
        pl.BlockSpec((s, LANES), lambda p, i, ks, cl: (0, vc0 + p)),
    ]
    if na:
        in_specs += _na_bias_specs()
        args, scratch = (kstart, cls, qkv, qkv, qkv, *na), [pltpu.VMEM((2, Q_BLOCK, NA_W), F32)]
    else:
        in_specs.append(pl.BlockSpec((1, 1, Q_BLOCK, nk), lambda p, i, ks, cl: (cl[i], 0, 0, 0)))
        args, scratch = (kstart, cls, qkv, qkv, qkv, table), []
    grid_spec = pltpu.PrefetchScalarGridSpec(
        num_scalar_prefetch=2, grid=(npairs, N_QBLK), in_specs=in_specs,
        out_specs=[pl.BlockSpec((Q_BLOCK, LANES), lambda p, i, ks, cl: (i, p)),
                   pl.BlockSpec((Q_BLOCK, LANES), lambda p, i, ks, cl: (i, p))],
        scratch_shapes=scratch,
    )
    return _pcall(body, name=name, grid_spec=grid_spec,
                  out_shape=[jax.ShapeDtypeStruct((s, w), F32), jax.ShapeDtypeStruct((s, w), F32)],
                  compiler_params=pltpu.CompilerParams(dimension_semantics=("parallel", "arbitrary")),
                  )(*args)


def _na_bias_specs():
    return [pl.BlockSpec((2, 16, LANES), lambda p, i, ks, cl: (p, 0, 0)),
            pl.BlockSpec((GRID_W, LANES), lambda p, i, ks, cl: (0, 0))]


def _attn_bwd(qkv, qc0, kc0, vc0, npairs, table, kstart, cls, nk, do, o, lse, *, name, dlse=None, na=None):
    s = qkv.shape[0]
    has_dlse = dlse is not None
    _, slot = _na_class_tables()

    def body(ks_ref, cls_ref, q_ref, k_ref, v_ref, b_ref, *rest):
        if na:
            cm_ref, rest = rest[0], rest[1:]
        do_ref, o_ref, lse_ref, rest = rest[0], rest[1], rest[2], rest[3:]
        if has_dlse:
            dlse_ref, rest = rest[0], rest[1:]
        dq_ref, dk_ref, dv_ref = rest[0], rest[1], rest[2]
        if na:
            bank_ref, bias_scr, dbias_scr, bank_scr = rest[3:]
        i = pl.program_id(1)
        if na:
            first = _na_build_bias(i, cls_ref, b_ref, cm_ref, bias_scr)

        @pl.when(i == 0)
        def _():
            dk_ref[...] = jnp.zeros_like(dk_ref)
            dv_ref[...] = jnp.zeros_like(dv_ref)
            if na:
                dbias_scr[...] = jnp.zeros_like(dbias_scr)
                bank_scr[...] = jnp.zeros_like(bank_scr)

        ks = pl.multiple_of(ks_ref[i], 64)
        q2 = q_ref[...]
        k2 = k_ref[pl.ds(ks, nk), :]
        v2 = v_ref[pl.ds(ks, nk), :]
        do2 = do_ref[...]
        lse2 = lse_ref[...]
        doo = do2 * o_ref[...]
        do2b = do2.astype(BF16)
        lo = _head_lanes()
        lane = lax.broadcasted_iota(jnp.int32, (1, LANES), 1)
        dqs, dks, dvs = [], [], []
        for h in range(2):
            mh = lo if h == 0 else jnp.logical_not(lo)
            qm = jnp.where(mh, q2, jnp.zeros_like(q2))
            sc = _dot_nt(qm, k2) * SCALE + (bias_scr[h, :, :nk] if na else b_ref[0, 0])
            lse_h = jnp.max(jnp.where(mh, lse2, NEG), axis=1, keepdims=True)
            p = jnp.exp(sc - lse_h)
            delta = jnp.sum(jnp.where(mh, doo, 0.0), axis=1, keepdims=True)
            dom = jnp.where(mh, do2b, jnp.zeros_like(do2b))
            dp = _dot_nt(dom, v2)
            t = dp - delta
            if has_dlse:
                t = t + jnp.sum(jnp.where(lane == h * HEAD_DIM, dlse_ref[...], 0.0), axis=1, keepdims=True)
            ds = p * t
            if na:
                @pl.when(first)
                def _():
                    dbias_scr[h, :, :nk] = ds

                @pl.when(jnp.logical_not(first))
                def _():
                    dbias_scr[h, :, :nk] += ds
            dsb = ds.astype(BF16)
            dqs.append(_dot(dsb, k2))
            dks.append(lax.dot_general(dsb, q2, _TN, preferred_element_type=F32))
            dvs.append(lax.dot_general(p.astype(BF16), do2b, _TN, preferred_element_type=F32))
        dq_ref[...] = jnp.where(lo, dqs[0], dqs[1]) * SCALE
        dk_ref[pl.ds(ks, nk), :] += jnp.where(lo, dks[0], dks[1]) * SCALE
        dv_ref[pl.ds(ks, nk), :] += jnp.where(lo, dvs[0], dvs[1])
        if na:
            last = jnp.logical_or(i == N_QBLK - 1, cls_ref[i] != cls_ref[jnp.minimum(i + 1, N_QBLK - 1)])
            for t in range(3):
                @pl.when(jnp.logical_and(last, cls_ref[i] == t))
                def _():
                    for hh in range(2):
                        for a in range(NA_QROWS):
                            for j in range(NA_PAIRS):
                                bank_scr[hh, int(slot[t, a, j])] += dbias_scr[
                                    hh, a * GRID_W:(a + 1) * GRID_W, j * LANES:(j + 1) * LANES]

            @pl.when(i == N_QBLK - 1)
            def _():
                bank_ref[...] = bank_scr[...]

    w = npairs * LANES
    blk = lambda: pl.BlockSpec((Q_BLOCK, LANES), lambda p, i, ks, cl: (i, p))
    full = lambda: pl.BlockSpec((s, LANES), lambda p, i, ks, cl: (0, p))
    in_specs = [
        pl.BlockSpec((Q_BLOCK, LANES), lambda p, i, ks, cl: (i, qc0 + p)),
        pl.BlockSpec((s, LANES), lambda p, i, ks, cl: (0, kc0 + p)),
        pl.BlockSpec((s, LANES), lambda p, i, ks, cl: (0, vc0 + p)),
    ]
    if na:
        in_specs += _na_bias_specs()
        args = [kstart, cls, qkv, qkv, qkv, *na]
    else:
        in_specs.append(pl.BlockSpec((1, 1, Q_BLOCK, nk), lambda p, i, ks, cl: (cl[i], 0, 0, 0)))
        args = [kstart, cls, qkv, qkv, qkv, table]
    in_specs += [blk(), blk(), blk()]
    args += [do, o, lse]
    if has_dlse:
        in_specs.append(blk())
        args.append(dlse)
    out_specs = [blk(), full(), full()]
    out_shape = [jax.ShapeDtypeStruct((s, w), F32)] * 3
    scratch = []
    if na:
        bank_shape = (2, NA_SLOTS, GRID_W, LANES)
        out_specs.append(pl.BlockSpec(bank_shape, lambda p, i, ks, cl: (p, 0, 0, 0)))
        out_shape.append(jax.ShapeDtypeStruct((2 * npairs,) + bank_shape[1:], F32))
        scratch = [pltpu.VMEM((2, Q_BLOCK, NA_W), F32), pltpu.VMEM((2, Q_BLOCK, NA_W), F32), pltpu.VMEM(bank_shape, F32)]
    grid_spec = pltpu.PrefetchScalarGridSpec(num_scalar_prefetch=2, grid=(npairs, N_QBLK), in_specs=in_specs,
                                             out_specs=out_specs, scratch_shapes=scratch)
    return _pcall(body, name=name, grid_spec=grid_spec, out_shape=out_shape,
                  compiler_params=pltpu.CompilerParams(dimension_semantics=("arbitrary", "arbitrary")))(*args)


_NA_CLASS_R0 = (0, NA_QROWS, ROWS - NA_QROWS)
_NA_CLASS_K0 = (0, 0, ROWS - NA_KROWS)
_RPB_RO = 2 * NA_KH - 1
_RPB_CO = 2 * NA_KW - 1
_BANK_ROWS = 48


def _na_constants():
    col = np.arange(GRID_W)
    cs = np.clip(col - NA_KW // 2, 0, GRID_W - NA_KW)
    vcol = (col[None, :] >= cs[:, None]) & (col[None, :] < cs[:, None] + NA_KW)
    colmask = np.where(np.concatenate([vcol, vcol], axis=1), 0.0, NEG).astype(np.float32)
    co = col[None, :] - col[:, None] + (NA_KW - 1)
    oh_col = np.zeros((GRID_W * GRID_W, LANES), np.float32)
    for qc in range(GRID_W):
        for kc in range(GRID_W):
            if vcol[qc, kc]:
                oh_col[qc * GRID_W + kc, co[qc, kc]] = 1.0
    ks = np.clip(np.arange(N_QBLK) * NA_QROWS - NA_KH // 2, 0, ROWS - NA_KROWS) * GRID_W
    cls = np.ones(N_QBLK, np.int32)
    cls[0], cls[-1] = 0, 2
    return colmask, oh_col, ks.astype(np.int32), cls


def _bank_reduce(bank, oh_col, *, name):
    def body(d_ref, ohc_ref, o_ref):
        o_ref[0] = jnp.dot(d_ref[0], ohc_ref[...], preferred_element_type=F32, precision=lax.Precision.HIGHEST)

    return _pcall(
        body, name=name, grid=(NA_HEADS,),
        in_specs=[pl.BlockSpec((1, _BANK_ROWS, GRID_W * GRID_W), lambda h: (h, 0, 0)),
                  pl.BlockSpec((GRID_W * GRID_W, LANES), lambda h: (0, 0))],
        out_specs=pl.BlockSpec((1, _BANK_ROWS, LANES), lambda h: (h, 0, 0)),
        out_shape=jax.ShapeDtypeStruct((NA_HEADS, _BANK_ROWS, LANES), F32), compiler_params=_PAR,
    )(bank, oh_col)


def _dil_constants(dilation):
    seg = SEQ // dilation
    nb = seg // Q_BLOCK
    shift = (0, -DIL_HALF, -2 * DIL_HALF)
    qi = np.arange(Q_BLOCK)[:, None]
    ki = np.arange(DIL_NK)[None, :]
    mask = np.stack([np.where(np.abs(ki + sh - qi) <= DIL_HALF, 0.0, NEG) for sh in shift]).astype(np.float32)
    ks, cls = [], []
    for i in range(N_QBLK):
        sub, blk = divmod(i, nb)
        t = 0 if blk == 0 else (2 if blk == nb - 1 else 1)
        cls.append(t)
        ks.append(sub * seg + blk * Q_BLOCK + shift[t])
    return mask.reshape(3, 1, Q_BLOCK, DIL_NK), np.asarray(ks, np.int32), np.asarray(cls, np.int32)


_VM = pl.BlockSpec(memory_space=pltpu.VMEM)


def _ada_fwd(c_all, w, b, *, name):
    def body(c_ref, w_ref, b_ref, o_ref):
        cv = c_ref[...]
        o_ref[...] = jnp.dot(cv * _sigmoid(cv), w_ref[...], preferred_element_type=F32,
                             precision=lax.Precision.HIGHEST) + b_ref[...]

    return _pcall(body, name=name, in_specs=[_VM, _VM, _VM], out_specs=_VM,
                  out_shape=jax.ShapeDtypeStruct((c_all.shape[0], w.shape[1]), F32))(c_all, w, b)


def _ada_bwd(c_all_t, dmod, *, name):
    def body(c_ref, d_ref, o_ref):
        cv = c_ref[...]
        o_ref[...] = jnp.dot(cv * _sigmoid(cv), d_ref[...], preferred_element_type=F32,
                             precision=lax.Precision.HIGHEST)

    return _pcall(body, name=name, in_specs=[_VM, _VM], out_specs=_VM,
                  out_shape=jax.ShapeDtypeStruct((c_all_t.shape[0], dmod.shape[1]), F32))(c_all_t, dmod)


def _row_sum(t, *, name):
    def body(t_ref, o_ref):
        o_ref[...] = jnp.sum(t_ref[...], axis=0, keepdims=True)

    return _pcall(body, name=name, in_specs=[_VM], out_specs=_VM,
                  out_shape=jax.ShapeDtypeStruct((1, t.shape[1]), F32))(t)


def _row_tile(rows):
    tr = rows
    for cand in range(8, 513, 8):
        if rows % cand == 0:
            tr = cand
    return tr


def _adamw_math(wv, gv, mv, vv):
    nm = ADAM_B1 * mv + (1.0 - ADAM_B1) * gv
    nv = ADAM_B2 * vv + (1.0 - ADAM_B2) * (gv * gv)
    m_hat = nm / (1.0 - ADAM_B1 ** ADAM_STEP)
    v_hat = nv / (1.0 - ADAM_B2 ** ADAM_STEP)
    return -ADAM_LR * (m_hat / (jnp.sqrt(v_hat) + ADAM_EPS) + ADAM_WD * wv), nm, nv


def _adamw(w, g, m, v, *, name):
    rows, cols = w.shape
    tr = _row_tile(rows)

    def body(w_ref, g_ref, m_ref, v_ref, d_ref, nm_ref, nv_ref):
        d_ref[...], nm_ref[...], nv_ref[...] = _adamw_math(w_ref[...], g_ref[...], m_ref[...], v_ref[...])

    spec = pl.BlockSpec((tr, cols), lambda i: (i, 0))
    return _pcall(body, name=name, grid=(rows // tr,), in_specs=[spec] * 4, out_specs=[spec] * 3,
                  out_shape=[jax.ShapeDtypeStruct((rows, cols), F32)] * 3, compiler_params=_PAR)(w, g, m, v)


def _adamw_halves(w, g_mine, g_other, m, v, c_arr, *, name):
    rows, cols = w.shape
    hr = rows // 2
    tr = _row_tile(hr)
    nt = hr // tr

    def body(c_ref, w_ref, t_ref, o_ref, m_ref, v_ref, g_ref, d_ref, nm_ref, nv_ref):
        gv = jnp.where(pl.program_id(0) == c_ref[0], t_ref[...], o_ref[...])
        g_ref[...] = gv
        d_ref[...], nm_ref[...], nv_ref[...] = _adamw_math(w_ref[...], gv, m_ref[...], v_ref[...])

    full = pl.BlockSpec((tr, cols), lambda h, i, c: (h * nt + i, 0))
    half = pl.BlockSpec((tr, cols), lambda h, i, c: (i, 0))
    grid_spec = pltpu.PrefetchScalarGridSpec(num_scalar_prefetch=1, grid=(2, nt),
                                             in_specs=[full, half, half, full, full], out_specs=[full] * 4)
    return _pcall(body, name=name, grid_spec=grid_spec, out_shape=[jax.ShapeDtypeStruct((rows, cols), F32)] * 4,
                  compiler_params=pltpu.CompilerParams(dimension_semantics=("parallel", "parallel")),
                  )(c_arr, w, g_mine, g_other, m, v)


_MESH = pl.DeviceIdType.MESH
_ANY = pl.BlockSpec(memory_space=pl.ANY)
_CHIP_FLIPS = ((1, 0), (0, 1), (1, 1))


def _pos():
    return lax.axis_index("x"), lax.axis_index("y"), lax.axis_index("c")


def _flip(v, f):
    return 1 - v if f else v


def _sem_pairs(n):
    return [pltpu.SemaphoreType.DMA((n,)), pltpu.SemaphoreType.DMA((n,))]


def _small_allgather(blk, *, name):
    m_per, n = blk.shape

    def body(x_ref, out_ref, send_sems, recv_sems, local_sem):
        x, y, c = _pos()
        me, sibling = (x, y, c), (x, y, 1 - c)
        chips = [(_flip(x, fx), _flip(y, fy)) for fx, fy in _CHIP_FLIPS]

        def rows(px, py, pc):
            return out_ref.at[pl.ds((4 * px + 2 * py + pc) * m_per, m_per), :]

        def copy(k, block, to, src=None):
            return pltpu.make_async_remote_copy(
                src_ref=rows(*block) if src is None else src, dst_ref=rows(*block),
                send_sem=send_sems.at[k], recv_sem=recv_sems.at[k], device_id=to, device_id_type=_MESH)

        mine = pltpu.make_async_copy(x_ref, rows(*me), local_sem)
        mine.start()
        first = [copy(0, me, sibling, src=x_ref)]
        first += [copy(1 + j, me, (*chip, c), src=x_ref) for j, chip in enumerate(chips)]
        for cp in first:
            cp.start()
        passed = [copy(4 + j, (*chip, c), sibling) for j, chip in enumerate(chips)]
        for j, chip in enumerate(chips):
            copy(1 + j, (*chip, c), me).wait_recv()
            passed[j].start()
        copy(0, sibling, me).wait_recv()
        for j, chip in enumerate(chips):
            copy(4 + j, (*chip, 1 - c), me).wait_recv()
        for cp in first + passed:
            cp.wait_send()
        mine.wait()

    return _pcall(
        body, name=name, out_shape=jax.ShapeDtypeStruct((N_DEV * m_per, n), blk.dtype),
        in_specs=[_VM], out_specs=_VM,
        scratch_shapes=_sem_pairs(7) + [pltpu.SemaphoreType.DMA],
    )(blk)


_HBM = pl.BlockSpec(memory_space=pltpu.HBM)
_SEM = pl.BlockSpec(memory_space=pltpu.SEMAPHORE)
_EFFECT = pltpu.SideEffectType.DATAFLOW_SIDE_EFFECTING


def _split_start(srcs, lands, plan, ncopies, after, *, name):
    ns, nl = len(srcs), len(lands)

    def body(*refs):
        src_refs, land_refs = refs[:ns], refs[ns:ns + nl]
        send_sems, recv_sems = refs[ns + nl + 1], refs[ns + nl + 2]
        token = refs[-1]
        x, y, c = _pos()
        for k, (src, dst, to, _) in enumerate(plan(x, y, c, src_refs, land_refs)):
            pltpu.make_async_remote_copy(src_ref=src, dst_ref=dst, send_sem=send_sems.at[k], recv_sem=recv_sems.at[k],
                                         device_id=to, device_id_type=_MESH).start()
        token[...] = jnp.zeros_like(token)

    hbm = lambda a: pltpu.HBM(a.shape, a.dtype)
    out = _pcall(
        body, name=name,
        out_shape=(pltpu.SemaphoreType.DMA((ncopies,)), pltpu.SemaphoreType.DMA((ncopies,)),
                   *[hbm(a) for a in srcs], *[hbm(a) for a in lands], jax.ShapeDtypeStruct((8, LANES), F32)),
        in_specs=[_HBM] * (ns + nl) + [_ANY], out_specs=(_SEM, _SEM, *[_HBM] * (ns + nl), _VM),
        input_output_aliases={i: 2 + i for i in range(ns + nl)},
        compiler_params=pltpu.CompilerParams(has_side_effects=_EFFECT),
    )(*[pltpu.with_memory_space_constraint(a, pltpu.HBM) for a in (*srcs, *lands)], after)
    return out[0], out[1], list(out[2:2 + ns]), list(out[2 + ns:2 + ns + nl]), out[-1]


def _split_wait(send_sems, recv_sems, srcs, lands, plan, after, *, name):
    ns, nl = len(srcs), len(lands)

    def body(*refs):
        src_refs, land_refs = refs[:ns], refs[ns:ns + nl]
        send_sems, recv_sems = refs[ns + nl], refs[ns + nl + 1]
        x, y, c = _pos()
        for k, (src, _, _, mine) in enumerate(plan(x, y, c, src_refs, land_refs)):
            cp = pltpu.make_async_remote_copy(src_ref=src, dst_ref=mine, send_sem=send_sems.at[k],
                                              recv_sem=recv_sems.at[k], device_id=(x, y, c), device_id_type=_MESH)
            cp.wait_send()
            cp.wait_recv()

    hbm = lambda a: pltpu.HBM(a.shape, a.dtype)
    out = _pcall(
        body, name=name, out_shape=tuple(hbm(a) for a in (*srcs, *lands)),
        in_specs=[_HBM] * (ns + nl) + [_SEM, _SEM, _ANY], out_specs=tuple([_HBM] * (ns + nl)),
        input_output_aliases={i: i for i in range(ns + nl)},
        compiler_params=pltpu.CompilerParams(has_side_effects=_EFFECT),
    )(*srcs, *lands, send_sems, recv_sems, after)
    return list(out[ns:])


def _ag_plan(nw):
    def plan(x, y, c, sh_refs, full_refs):
        j = 2 * x + y
        out = []
        for wi in range(nw):
            for fx, fy in _CHIP_FLIPS:
                px, py = _flip(x, fx), _flip(y, fy)
                out.append((sh_refs[wi].at[c], full_refs[wi].at[j, c], (px, py, c), full_refs[wi].at[2 * px + py, c]))
            out.append((sh_refs[wi], full_refs[wi].at[j], (x, y, 1 - c), full_refs[wi].at[j]))
        return out
    return plan


def _ag_pass(fulls, *, name):
    nw = len(fulls)

    def body(*refs):
        in_refs, out_refs = refs[:nw], refs[nw:2 * nw]
        send_sems, recv_sems = refs[2 * nw:]
        x, y, c = _pos()
        cps = []
        for wi in range(nw):
            for k, (fx, fy) in enumerate(_CHIP_FLIPS):
                jp = 2 * _flip(x, fx) + _flip(y, fy)
                sems = dict(send_sem=send_sems.at[3 * wi + k], recv_sem=recv_sems.at[3 * wi + k], device_id_type=_MESH)
                send = pltpu.make_async_remote_copy(src_ref=in_refs[wi].at[jp, c], dst_ref=out_refs[wi].at[jp, c],
                                                    device_id=(x, y, 1 - c), **sems)
                recv = pltpu.make_async_remote_copy(src_ref=in_refs[wi].at[jp, c], dst_ref=out_refs[wi].at[jp, 1 - c],
                                                    device_id=(x, y, c), **sems)
                cps.append((send, recv))
        for send, _ in cps:
            send.start()
        for send, recv in cps:
            send.wait_send()
            recv.wait_recv()

    return _pcall(body, name=name, out_shape=[jax.ShapeDtypeStruct(f.shape, f.dtype) for f in fulls],
                  in_specs=[_ANY] * nw, out_specs=[_ANY] * nw, input_output_aliases={i: i for i in range(nw)},
                  scratch_shapes=_sem_pairs(3 * nw))(*fulls)


def _rs_plan(nw):
    def plan(x, y, c, s_refs, rb_refs):
        out = []
        for wi in range(nw):
            for k, (fx, fy) in enumerate(_CHIP_FLIPS):
                px, py = _flip(x, fx), _flip(y, fy)
                out.append((s_refs[wi].at[2 * px + py], rb_refs[wi].at[k], (px, py, c), rb_refs[wi].at[k]))
        return out
    return plan


def _sibling_send_halves(gs, *, name):
    nw = len(gs)

    def body(*refs):
        g_refs, out_refs = refs[:nw], refs[nw:2 * nw]
        send_sems, recv_sems = refs[2 * nw:]
        x, y, c = _pos()
        cps = [pltpu.make_async_remote_copy(src_ref=g_refs[wi].at[k, 1 - c], dst_ref=out_refs[wi].at[k],
                                            send_sem=send_sems.at[4 * wi + k], recv_sem=recv_sems.at[4 * wi + k],
                                            device_id=(x, y, 1 - c), device_id_type=_MESH)
               for wi in range(nw) for k in range(N_CHIP)]
        for cp in cps:
            cp.start()
        for cp in cps:
            cp.wait()

    return _pcall(body, name=name,
                  out_shape=[jax.ShapeDtypeStruct((g.shape[0],) + g.shape[2:], g.dtype) for g in gs],
                  in_specs=[_ANY] * nw, out_specs=[_ANY] * nw, scratch_shapes=_sem_pairs(N_CHIP * nw))(*gs)


def _sibling_swap(ts, *, name):
    nw = len(ts)

    def body(*refs):
        t_refs, out_refs = refs[:nw], refs[nw:2 * nw]
        send_sems, recv_sems = refs[2 * nw:]
        x, y, c = _pos()
        cps = [pltpu.make_async_remote_copy(src_ref=t_refs[wi], dst_ref=out_refs[wi], send_sem=send_sems.at[wi],
                                            recv_sem=recv_sems.at[wi], device_id=(x, y, 1 - c), device_id_type=_MESH)
               for wi in range(nw)]
        for cp in cps:
            cp.start()
        for cp in cps:
            cp.wait()

    return _pcall(body, name=name, out_shape=[jax.ShapeDtypeStruct(t.shape, t.dtype) for t in ts],
                  in_specs=[_ANY] * nw, out_specs=[_ANY] * nw, scratch_shapes=_sem_pairs(nw))(*ts)


def _rs_add(g, ra, c_arr, *, name):
    n, _, r, w = g.shape

    def body(c_ref, g_ref, ra_ref, s_ref, sb_ref):
        t = g_ref[...] + ra_ref[...]
        s_ref[...] = t
        sb_ref[...] = t.astype(BF16)

    grid_spec = pltpu.PrefetchScalarGridSpec(
        num_scalar_prefetch=1, grid=(n,),
        in_specs=[pl.BlockSpec((None, None, r, w), lambda k, c: (k, c[0], 0, 0)),
                  pl.BlockSpec((None, r, w), lambda k, c: (k, 0, 0))],
        out_specs=[pl.BlockSpec((None, r, w), lambda k, c: (k, 0, 0))] * 2)
    return _pcall(body, name=name, grid_spec=grid_spec,
                  out_shape=[jax.ShapeDtypeStruct((n, r, w), F32), jax.ShapeDtypeStruct((n, r, w), BF16)],
                  compiler_params=_PAR)(c_arr, g, ra)


def _rs_final(s, rb, j_arr, *, name):
    _, r, w = s.shape

    def body(j_ref, s_ref, rb_ref, t_ref):
        t_ref[...] = ((s_ref[...] + rb_ref[0].astype(F32)) + rb_ref[1].astype(F32)) + rb_ref[2].astype(F32)

    grid_spec = pltpu.PrefetchScalarGridSpec(
        num_scalar_prefetch=1, grid=(1,),
        in_specs=[pl.BlockSpec((None, r, w), lambda i, j: (j[0], 0, 0)),
                  pl.BlockSpec((3, r, w), lambda i, j: (0, 0, 0))],
        out_specs=pl.BlockSpec((r, w), lambda i, j: (0, 0)))
    return _pcall(body, name=name, grid_spec=grid_spec, out_shape=jax.ShapeDtypeStruct((r, w), F32),
                  compiler_params=_ARB)(j_arr, s, rb)


def _perm_rows(t, d):
    s, w = t.shape
    return t.reshape(s // d, d, w).transpose(1, 0, 2).reshape(s, w)


def _unperm_rows(t, d):
    s, w = t.shape
    return t.reshape(d, s // d, w).transpose(1, 0, 2).reshape(s, w)


def _tile2(g):
    return jnp.concatenate([g, g], axis=1)


_BIG = ("w_in", "w_ffn_in", "w_ffn_out", "w_o", "w_proj_a", "w_proj_b")
_BIG_SHARD = {"w_in": (1024, 1472), "w_ffn_in": (1024, 1408), "w_ffn_out": (704, 1024), "w_o": (256, 1024),
              "w_proj_a": (512, 256), "w_proj_b": (256, 256)}


def _device_step(x2, tgt, mod, first_weights, late_weights, early_grads, g_norm1, g_norm2, b_gate, g_qa, g_ka, g_qb,
                 g_kb, rpb):
    d = D_MODEL
    sh1, sc1, gt1, sh2, sc2, gt2 = [mod[:, k * d:(k + 1) * d] for k in range(6)]

    colmask, oh_col, na_ks, na_cls = _na_constants()
    rp = jnp.pad(rpb, ((0, 0), (0, 16 - _RPB_RO), (RP_LANE0, LANES - RP_LANE0 - _RPB_CO)), constant_values=NEG)
    na = (rp, jnp.asarray(colmask))
    na_ks, na_cls = jnp.asarray(na_ks), jnp.asarray(na_cls)
    dil = [_dil_constants(dd) for _, dd in DIL_CONFIGS]
    tab_d = jnp.asarray(dil[0][0])
    gains = jnp.concatenate([_tile2(g_qa), _tile2(g_ka), _tile2(g_qb), _tile2(g_kb)], axis=0)
    cos_t, sa_t, sb_t = _rope_tables()

    wts = first_weights(cos_t)
    h1, qkvn, qk_pre, gates = _pre_attn_fwd(x2, cos_t, sa_t, sb_t, g_norm1, sc1, sh1, wts["w_qkv"], wts["w_gates"],
                                            gains, name="pre_attn_fwd")
    o_a, lse_a = _attn_fwd(qkvn, 0, 4, 8, 4, None, na_ks, na_cls, NA_NK, name="attn_a_fwd", na=na)
    arrs, o_p, l_p, o_g, l_g = [], [], [], [], []
    for g, (_, dd) in enumerate(DIL_CONFIGS):
        ks_g, cls_g = jnp.asarray(dil[g][1]), jnp.asarray(dil[g][2])
        if dd == 1:
            arr, cb = qkvn, (12, 18, 24)
        else:
            col = lambda base: qkvn[:, base + WB_OUT * g: base + WB_OUT * (g + 1)]
            arr = _perm_rows(jnp.concatenate([col(3 * WA), col(3 * WA + WB), col(3 * WA + 2 * WB)], axis=1), dd)
            cb = (0, 2, 4)
        op, lp = _attn_fwd(arr, cb[0], cb[1], cb[2], 2, tab_d, ks_g, cls_g, DIL_NK, name=f"attn_d{g}_fwd")
        arrs.append((arr, cb, ks_g, cls_g))
        o_p.append(op)
        l_p.append(lp)
        o_g.append(op if dd == 1 else _unperm_rows(op, dd))
        l_g.append(lp if dd == 1 else _unperm_rows(lp, dd))
    wts = dict(wts, **late_weights(o_a))
    o_b, merged, mo, x1, h2 = _post_attn_fwd(o_a, o_g, l_g, gates, x2, wts["w_pa"], wts["w_pb"], wts["w_o"], b_gate,
                                             gt1, g_norm2, sc2, sh2, name="post_attn_fwd")
    act, ff = _ffn_fwd(h2, wts["w_ffn_in"], name="ffn_fwd")

    dy, dffo, dff, dgt2, loss_v = _ffn_mid(act, ff, x1, tgt, wts["w_ffn_out"], gt2, name="ffn_mid")
    grads = {}
    g_ffn_out = _wgrad(act, dffo, name="wg_ffn_out", tm=D_FF // 2, tn=d)
    grads["w_ffn_out"] = g_ffn_out.reshape(N_CHIP, D_FF // N_CHIP, d)
    grads["w_ffn_in"] = _wgrad(h2, dff, name="wg_ffn_in", tm=512, tn=2 * FF_CHIP, chips=N_CHIP)
    dx1, dmo, sums2 = _ffn_in_bwd(dff, x1, dy, mo, wts["w_ffn_in"], g_norm2, sc2, gt1, name="ffn_in_bwd")
    grads["w_o"] = _wgrad(merged, dmo, name="wg_o", tm=d, tn=d).reshape(N_CHIP, d // N_CHIP, d)
    pab = _post_attn_bwd(dmo, gates, o_a, o_g, l_g, wts["w_pa"], wts["w_pb"], wts["w_o"], b_gate, name="post_attn_bwd")
    dpa, dpb, dgates, do_a = pab[:4]
    do_g, dl_g, dbg = pab[4:7], pab[7:10], pab[10]
    g_pa = _wgrad(o_a, dpa, name="wg_pa", tm=WA, tn=d)
    g_pb = _wgrad(o_b, dpb, name="wg_pb", tm=WB_OUT, tn=d)
    grads["w_proj_a"] = g_pa.reshape(WA, N_CHIP, d // N_CHIP).transpose(1, 0, 2)
    grads["w_proj_b"] = g_pb.reshape(WB_OUT, N_CHIP, d // N_CHIP).transpose(1, 0, 2)
    early_grads(grads)
    dqa, dka, dva, bank = _attn_bwd(qkvn, 0, 4, 8, 4, None, na_ks, na_cls, NA_NK, do_a, o_a, lse_a,
                                    name="attn_a_bwd", na=na)
    dqs, dks, dvs = [], [], []
    for g, (_, dd) in enumerate(DIL_CONFIGS):
        arr, cb, ks_g, cls_g = arrs[g]
        dog = do_g[g] if dd == 1 else _perm_rows(do_g[g], dd)
        dlg = dl_g[g] if dd == 1 else _perm_rows(dl_g[g], dd)
        dq, dk, dv = _attn_bwd(arr, cb[0], cb[1], cb[2], 2, tab_d, ks_g, cls_g, DIL_NK, dog, o_p[g], l_p[g],
                               name=f"attn_d{g}_bwd", dlse=dlg)
        if dd != 1:
            dq, dk, dv = _unperm_rows(dq, dd), _unperm_rows(dk, dd), _unperm_rows(dv, dd)
        dqs.append(dq)
        dks.append(dk)
        dvs.append(dv)
    dqkv, grad_x, dgains, sums1 = _pre_attn_bwd(qk_pre, [dqa, dka, dva] + dqs + dks + dvs, dgates, x2, dx1, cos_t, sa_t,
                                                sb_t, wts["w_qkv"], wts["w_gates"], gains, g_norm1, sc1,
                                                name="pre_attn_bwd")
    g_qkv = _wgrad(h1, dqkv, name="wg_qkv", tm=d, tn=W_QKV // 2)
    g_gates = _wgrad(h1, dgates, name="wg_gates", tm=d, tn=W_GATES)
    nc, cut = _BIG_SHARD["w_in"][1], 3 * _BIG_SHARD["w_in"][1] - W_QKV
    grads["w_in"] = jnp.stack([g_qkv[:, :nc], g_qkv[:, nc:2 * nc],
                               jnp.concatenate([g_qkv[:, 2 * nc:], g_gates[:, :cut]], axis=1), g_gates[:, cut:]])

    bank = bank.reshape(NA_HEADS, NA_SLOTS, GRID_W, 2, GRID_W).transpose(0, 1, 3, 2, 4)
    bank = jnp.pad(bank.reshape(NA_HEADS, 2 * NA_SLOTS, GRID_W * GRID_W), ((0, 0), (0, _BANK_ROWS - 2 * NA_SLOTS), (0, 0)))
    g2 = _bank_reduce(bank, jnp.asarray(oh_col), name="rpb_reduce")[:, :2 * NA_SLOTS].reshape(NA_HEADS, NA_SLOTS, 2, LANES)
    g_rpb = g2[:, 3:3 + _RPB_RO, 0, :_RPB_CO] + g2[:, 2:2 + _RPB_RO, 1, :_RPB_CO]

    dmod = jnp.concatenate([sums1[0:1], sums1[1:2], sums2[3:4], sums2[0:1], sums2[1:2], dgt2], axis=1)
    small = dict(g_norm1=sums1[2:3], g_norm2=sums2[2:3], b_gate=dbg, g_qa=dgains[0:1, :HEAD_DIM],
                 g_ka=dgains[1:2, :HEAD_DIM], g_qb=dgains[2:3, :HEAD_DIM], g_kb=dgains[3:4, :HEAD_DIM], rpb=g_rpb)
    return loss_v, grad_x, grads, dmod, small


_SMALL = ("b_ada", "g_norm1", "g_norm2", "b_gate", "g_qa", "g_ka", "g_qb", "g_kb", "rpb")
_SMALL_N = {"b_ada": 6 * D_MODEL, "g_norm1": D_MODEL, "g_norm2": D_MODEL, "b_gate": 2 * D_MODEL, "g_qa": HEAD_DIM,
            "g_ka": HEAD_DIM, "g_qb": HEAD_DIM, "g_kb": HEAD_DIM, "rpb": NA_HEADS * _RPB_RO * _RPB_CO}


def _pack_small(parts):
    flat = [parts[n].reshape(1, _SMALL_N[n]) for n in _SMALL]
    used = sum(_SMALL_N.values())
    return jnp.concatenate(flat + [jnp.zeros((1, STATS_W - used), F32)], axis=1)


def _unpack_small(v, shapes):
    out, at = {}, 0
    for n in _SMALL:
        out[n] = v[:, at:at + _SMALL_N[n]].reshape(shapes[n])
        at += _SMALL_N[n]
    return out


def _join_cols(t):
    _, r, c = t.shape
    return t.transpose(1, 0, 2).reshape(r, N_CHIP * c)


def kernel(x, c, w_ada, b_ada, g_norm1, g_norm2, w_in, b_gate, g_qa, g_ka, g_qb, g_kb, rpb, w_proj_a, w_proj_b, w_o, w_ffn_in, w_ffn_out, loss_target, m_w_ada, m_b_ada, m_g_norm1, m_g_norm2, m_w_in, m_b_gate, m_g_qa, m_g_ka, m_g_qb, m_g_kb, m_rpb, m_w_proj_a, m_w_proj_b, m_w_o, m_w_ffn_in, m_w_ffn_out, v_w_ada, v_b_ada, v_g_norm1, v_g_norm2, v_w_in, v_b_gate, v_g_qa, v_g_ka, v_g_qb, v_g_kb, v_rpb, v_w_proj_a, v_w_proj_b, v_w_o, v_w_ffn_in, v_w_ffn_out):
    names = ("w_ada", "b_ada", "g_norm1", "g_norm2", "w_in", "b_gate", "g_qa", "g_ka", "g_qb", "g_kb", "rpb",
             "w_proj_a", "w_proj_b", "w_o", "w_ffn_in", "w_ffn_out")
    w = dict(zip(names, (w_ada, b_ada, g_norm1, g_norm2, w_in, b_gate, g_qa, g_ka, g_qb, g_kb, rpb, w_proj_a, w_proj_b,
                         w_o, w_ffn_in, w_ffn_out)))
    m = dict(zip(names, (m_w_ada, m_b_ada, m_g_norm1, m_g_norm2, m_w_in, m_b_gate, m_g_qa, m_g_ka, m_g_qb, m_g_kb, m_rpb,
                         m_w_proj_a, m_w_proj_b, m_w_o, m_w_ffn_in, m_w_ffn_out)))
    v = dict(zip(names, (v_w_ada, v_b_ada, v_g_norm1, v_g_norm2, v_w_in, v_b_gate, v_g_qa, v_g_ka, v_g_qb, v_g_kb, v_rpb,
                         v_w_proj_a, v_w_proj_b, v_w_o, v_w_ffn_in, v_w_ffn_out)))
    d = D_MODEL
    xi, yi, ci = _pos()
    chip = 2 * xi + yi
    me = 2 * chip + ci
    ada_cols = 6 * d // N_CHIP

    c_arr, chip_arr = ci.reshape(1).astype(jnp.int32), chip.reshape(1).astype(jnp.int32)
    first, rest = _BIG[:1], _BIG[1:]

    c_all = _small_allgather(jnp.broadcast_to(c, (8, d)), name="ag_c")[::8]
    b_sh = lax.dynamic_slice(b_ada, (0, chip * ada_cols), (1, ada_cols))
    mod_part = _ada_fwd(c_all, w_ada[0], b_sh, name="ada_fwd")
    mod_all = _small_allgather(mod_part, name="ag_mod").reshape(N_CHIP, 2, 8, ada_cols)[:, 0]
    mod = lax.dynamic_index_in_dim(mod_all, me, axis=1, keepdims=False).reshape(1, 6 * d)

    halves = {n: (2, _BIG_SHARD[n][0] // 2, _BIG_SHARD[n][1]) for n in _BIG}
    shards = {n: w[n][0].astype(BF16).reshape(halves[n]) for n in _BIG}
    land = lambda n: lax.empty((N_CHIP,) + halves[n], BF16)
    ag1 = _split_start([shards[n] for n in first], [land(n) for n in first], _ag_plan(1), 4, mod, name="ag1_start")
    ag2 = _split_start([shards[n] for n in rest], [land(n) for n in rest], _ag_plan(len(rest)), 4 * len(rest),
                       ag1[4], name="ag2_start")
    rpb_after = rpb[0] + ag2[4][0, 0]

    def first_weights(after):
        full1 = _split_wait(ag1[0], ag1[1], ag1[2], ag1[3], _ag_plan(1), after, name="ag1_wait")
        p_in = _ag_pass(full1, name="ag1_pass")[0].reshape((N_CHIP,) + _BIG_SHARD["w_in"])
        cut = W_QKV - 2 * _BIG_SHARD["w_in"][1]
        return dict(w_qkv=jnp.concatenate([p_in[0], p_in[1], p_in[2][:, :cut]], axis=1),
                    w_gates=jnp.concatenate([p_in[2][:, cut:], p_in[3]], axis=1))

    def late_weights(after):
        full2 = _split_wait(ag2[0], ag2[1], ag2[2], ag2[3], _ag_plan(len(rest)), after, name="ag2_wait")
        full2 = _ag_pass(full2, name="ag2_pass")
        full = {n: fu.reshape((N_CHIP,) + _BIG_SHARD[n]) for n, fu in zip(rest, full2)}
        return dict(w_pa=_join_cols(full["w_proj_a"]), w_pb=_join_cols(full["w_proj_b"]), w_o=full["w_o"].reshape(d, d),
                    w_ffn_in=full["w_ffn_in"], w_ffn_out=full["w_ffn_out"].reshape(D_FF, d))

    def rs_begin(group, grads, tag):
        gps = [grads[n].reshape((N_CHIP,) + halves[n]) for n in group]
        ras = _sibling_send_halves(gps, name=f"rs_sibling_{tag}")
        sums = [_rs_add(gp, ra, c_arr, name=f"rs_add_{n}") for n, gp, ra in zip(group, gps, ras)]
        lands = [lax.empty((3,) + halves[n][1:], BF16) for n in group]
        st = _split_start([sb for _, sb in sums], lands, _rs_plan(len(group)), 3 * len(group), sums[0][0],
                          name=f"rs_{tag}_start")
        return sums, st

    def rs_end(group, begun, after, tag):
        sums, st = begun
        rbs = _split_wait(st[0], st[1], st[2], st[3], _rs_plan(len(group)), after, name=f"rs_{tag}_wait")
        return [_rs_final(sf, rb, chip_arr, name=f"rs_final_{n}") for n, (sf, _), rb in zip(group, sums, rbs)]

    begun = {}

    def early_grads(grads):
        begun["rest"] = rs_begin(rest, grads, "rest")

    loss_v, grad_x, grads, dmod, small = _device_step(
        x[0], loss_target[0], mod, first_weights, late_weights, early_grads, g_norm1, g_norm2, b_gate, g_qa, g_ka, g_qb,
        g_kb, rpb_after)
    begun["first"] = rs_begin(first, grads, "first")
    ts_rest = rs_end(rest, begun["rest"], begun["first"][0][0][1], "rest")

    stats = _pack_small(dict(b_ada=dmod, **small))
    rows = _small_allgather(jnp.broadcast_to(stats, (8, STATS_W)), name="ag_stats")[::8]
    dmod_sh = lax.dynamic_slice(rows, (0, chip * ada_cols), (8, ada_cols))
    g_ada = _ada_bwd(c_all.T, dmod_sh, name="ada_bwd")
    tot = _row_sum(rows, name="stats_sum")
    g_small = _unpack_small(tot, {n: w[n].shape for n in _SMALL})

    ts = rs_end(first, begun["first"], tot, "first") + ts_rest
    others = _sibling_swap(ts, name="rs_pair")

    g, delta, new_m, new_v = {}, {}, {}, {}
    for n, t, o in zip(_BIG, ts, others):
        gg, dl, nm, nv = _adamw_halves(w[n][0], t, o, m[n][0], v[n][0], c_arr, name=f"adamw_{n}")
        g[n], delta[n], new_m[n], new_v[n] = gg[None], dl[None], nm[None], nv[None]
    dl, nm, nv = _adamw(w_ada[0], g_ada, m_w_ada[0], v_w_ada[0], name="adamw_w_ada")
    g["w_ada"], delta["w_ada"], new_m["w_ada"], new_v["w_ada"] = g_ada[None], dl[None], nm[None], nv[None]
    shapes = {n: w[n].shape for n in _SMALL}
    dl, nm, nv = _adamw(_pack_small({n: w[n] for n in _SMALL}), tot, _pack_small({n: m[n] for n in _SMALL}),
                        _pack_small({n: v[n] for n in _SMALL}), name="adamw_small")
    delta.update(_unpack_small(dl, shapes))
    new_m.update(_unpack_small(nm, shapes))
    new_v.update(_unpack_small(nv, shapes))
    g.update(g_small)

    loss = lax.psum(loss_v[0, 0], ("x", "y", "c"))
    return (loss, grad_x[None], *[g[n] for n in names], *[delta[n] for n in names], *[new_m[n] for n in names],
            *[new_v[n] for n in names])
```

```python
import numpy as np

import jax
import jax.numpy as jnp
from jax import lax
from jax.experimental import pallas as pl
from jax.experimental.pallas import tpu as pltpu

F32 = jnp.float32
BF16 = jnp.bfloat16

D_MODEL = 1024
SEQ = 8192
HEAD_DIM = 64
GRID_W = 64
ROWS = SEQ // GRID_W
NA_HEADS = 8
NA_KH = 8
NA_KW = 16
DIL_CONFIGS = ((128, 1), (512, 4), (2048, 16))
ROT_DIM = 16
ROPE_THETA = 500000.0
D_FF = 2816
EPS = 1e-6
NEG = -1e30
WA = 512
WB = 768
WB_OUT = 256
W_QKV = 3 * WA + 3 * WB
W_QK = 2 * WA + 2 * WB
W_GATES = 2 * D_MODEL
SCALE = HEAD_DIM ** -0.5

ADAM_LR = 0.001
ADAM_B1 = 0.9
ADAM_B2 = 0.999
ADAM_EPS = 1e-08
ADAM_WD = 0.01
ADAM_STEP = 10

LANES = 128
ROW_TILE = 256
ROW_TILES = {"ffn_fwd": 512, "post_attn_fwd": 512, "post_attn_bwd": 512, "ffn_in_bwd": 512,
             "pre_attn_fwd": 512}
Q_BLOCK = 256
NA_QROWS = Q_BLOCK // GRID_W
NA_KROWS = NA_QROWS + NA_KH - 1
NA_NK = NA_KROWS * GRID_W
NA_PAIRS = (NA_KROWS + 1) // 2
NA_W = NA_PAIRS * LANES
NA_RO_NONE = 15
NA_SLOTS = 21
RP_LANE0 = GRID_W - NA_KW
DIL_HALF = 64
DIL_NK = Q_BLOCK + 2 * DIL_HALF
N_QBLK = SEQ // Q_BLOCK

N_DEV = 8
N_CHIP = 4
FF_CHIP = 2 * D_FF // N_CHIP
STATS_W = 14336


def _pcall(body, *, name, **kw):
    return pl.pallas_call(body, name=name, **kw)


_NT = (((1,), (1,)), ((), ()))
_TN = (((0,), (0,)), ((), ()))
_ARB = pltpu.CompilerParams(dimension_semantics=("arbitrary",))
_PAR = pltpu.CompilerParams(dimension_semantics=("parallel",))


def _dot(a, b):
    return jnp.dot(a, b, preferred_element_type=F32)


def _dot_nt(a, b):
    return lax.dot_general(a, b, _NT, preferred_element_type=F32)


def _wgrad(a, b, *, name, tm, tn, tk=1024, chips=None):
    s, ma = a.shape
    nb = b.shape[1]
    nk = s // tk
    nc = nb // chips if chips else tn
    cpb = tn // nc

    def body(a_ref, b_ref, o_ref, acc):
        k = pl.program_id(2)
        r = lax.dot_general(a_ref[...].astype(BF16), b_ref[...].astype(BF16), _TN, preferred_element_type=F32)

        @pl.when(k == 0)
        def _():
            acc[...] = r

        @pl.when(k > 0)
        def _():
            acc[...] += r

        @pl.when(k == nk - 1)
        def _():
            if chips:
                for q in range(cpb):
                    o_ref[q] = acc[:, q * nc:(q + 1) * nc]
            else:
                o_ref[...] = acc[...]

    if chips:
        o_spec = pl.BlockSpec((cpb, tm, nc), lambda i, j, k: (j, i, 0))
        out_shape = jax.ShapeDtypeStruct((chips, ma, nc), F32)
    else:
        o_spec = pl.BlockSpec((tm, tn), lambda i, j, k: (i, j))
        out_shape = jax.ShapeDtypeStruct((ma, nb), F32)
    return _pcall(
        body, name=name, grid=(ma // tm, nb // tn, nk),
        in_specs=[pl.BlockSpec((tk, tm), lambda i, j, k: (k, i)), pl.BlockSpec((tk, tn), lambda i, j, k: (k, j))],
        out_specs=o_spec, out_shape=out_shape, scratch_shapes=[pltpu.VMEM((tm, tn), F32)],
        compiler_params=pltpu.CompilerParams(dimension_semantics=("parallel", "parallel", "arbitrary")),
    )(a, b)


def _row_call(body, *, name, row_ins, res_ins, row_outs, acc_outs=(), scratch=()):
    s = row_ins[0].shape[0]
    tile = ROW_TILES.get(name, ROW_TILE)
    n = s // tile
    nri, nre, nro, nao = len(row_ins), len(res_ins), len(row_outs), len(acc_outs)

    def whole(shape):
        nd = len(shape)
        return pl.BlockSpec(tuple(shape), lambda i: (0,) * nd, pipeline_mode=pl.Buffered(1))

    def whole_out(shape):
        nd = len(shape)
        return pl.BlockSpec(tuple(shape), lambda i: (0,) * nd)

    in_specs = [pl.BlockSpec((tile, a.shape[1]), lambda i: (i, 0)) for a in row_ins]
    in_specs += [whole(a.shape) for a in res_ins]
    out_specs = [pl.BlockSpec((tile, w), lambda i: (i, 0)) for w, _ in row_outs]
    out_specs += [whole_out(shp) for shp, _ in acc_outs]
    out_shape = [jax.ShapeDtypeStruct((s, w), dt) for w, dt in row_outs]
    out_shape += [jax.ShapeDtypeStruct(tuple(shp), dt) for shp, dt in acc_outs]

    def wrapped(*refs):
        at = [0, nri, nri + nre, nri + nre + nro, nri + nre + nro + nao]
        body(pl.program_id(0), n, refs[at[0]:at[1]], refs[at[1]:at[2]], refs[at[2]:at[3]], refs[at[3]:at[4]],
             refs[at[4]:])

    return _pcall(wrapped, name=name, grid=(n,), in_specs=in_specs, out_specs=out_specs, out_shape=out_shape,
                  scratch_shapes=list(scratch), compiler_params=_ARB)(*row_ins, *res_ins)


def _fold8(t):
    r, w = t.shape
    return jnp.sum(t.reshape(r // 8, 8, w), axis=0)


def _sigmoid(t):
    return 1.0 / (1.0 + jnp.exp(-t))


def _head_lanes():
    return lax.broadcasted_iota(jnp.int32, (1, LANES), 1) < HEAD_DIM


def _head_mean(t, lo):
    s_lo = jnp.sum(jnp.where(lo, t, 0.0), axis=1, keepdims=True)
    s_hi = jnp.sum(jnp.where(lo, 0.0, t), axis=1, keepdims=True)
    return jnp.where(lo, s_lo, s_hi) * (1.0 / HEAD_DIM)


def _rms_mod(xv, g, sc, sh):
    rstd = lax.rsqrt(jnp.mean(xv * xv, axis=1, keepdims=True) + EPS)
    return (xv * rstd * g) * (1.0 + sc) + sh


def _rms_mod_bwd(xv, dh, g, sc):
    rstd = lax.rsqrt(jnp.mean(xv * xv, axis=1, keepdims=True) + EPS)
    xhat = xv * rstd
    dn = dh * (1.0 + sc)
    dxhat = dn * g
    dx = rstd * (dxhat - xhat * jnp.mean(dxhat * xhat, axis=1, keepdims=True))
    return dx, dh, dh * (xhat * g), dn * xhat


def _mix_weights(ls):
    m = jnp.maximum(jnp.maximum(ls[0], ls[1]), ls[2])
    es = [jnp.exp(t - m) for t in ls]
    den = es[0] + es[1] + es[2]
    return [e / den for e in es]


def _rope_tables():
    half = ROT_DIM // 2
    inv_freq = ROPE_THETA ** (-(jnp.arange(half, dtype=F32) * 2.0) / ROT_DIM)
    lane = np.arange(LANES) % HEAD_DIM
    ang = jnp.arange(SEQ).astype(F32)[:, None] * jnp.tile(inv_freq, LANES // half)[None, :]
    cos, sin = jnp.cos(ang), jnp.sin(ang)
    first, second = jnp.asarray(lane < half)[None, :], jnp.asarray((lane >= half) & (lane < ROT_DIM))[None, :]
    cos_t = jnp.where(first | second, cos, 1.0)
    return cos_t, jnp.where(second, sin, 0.0), jnp.where(first, -sin, 0.0)


_SECTIONS = ((0, WA, 0, False), (WA, 2 * WA, 1, False), (2 * WA, 3 * WA, -1, False),
             (3 * WA, 3 * WA + WB, 2, True), (3 * WA + WB, 3 * WA + 2 * WB, 3, True), (3 * WA + 2 * WB, W_QKV, -1, False))


def _pre_attn_fwd(x, cos_t, sa_t, sb_t, g1, sc1, sh1, w_qkv, w_gates, gains, *, name):
    half = ROT_DIM // 2

    def body(i, n, rin, res, rout, aout, scr):
        x_ref, cos_ref, sa_ref, sb_ref = rin
        g_ref, sc_ref, sh_ref, wq_ref, wg_ref, gains_ref = res
        h1_ref, qkvn_ref, pre_ref, gates_ref = rout
        hb = _rms_mod(x_ref[...], g_ref[...], sc_ref[...], sh_ref[...]).astype(BF16)
        h1_ref[...] = hb
        gates_ref[...] = _dot(hb, wg_ref[...]).astype(BF16)
        lo = _head_lanes()
        cosv, sav, sbv = cos_ref[...], sa_ref[...], sb_ref[...]
        pre_at = 0
        for c0, c1, kind, rot in _SECTIONS:
            sec = _dot(hb, wq_ref[:, c0:c1])
            for ch in range((c1 - c0) // LANES):
                t = sec[:, ch * LANES:(ch + 1) * LANES]
                if kind >= 0:
                    pre_ref[:, pre_at:pre_at + LANES] = t.astype(BF16)
                    pre_at += LANES
                    t = t * lax.rsqrt(_head_mean(t * t, lo) + EPS) * gains_ref[kind:kind + 1, :]
                    if rot:
                        t = t * cosv + pltpu.roll(t, half, 1) * sav + pltpu.roll(t, LANES - half, 1) * sbv
                qkvn_ref[:, c0 + ch * LANES:c0 + (ch + 1) * LANES] = t.astype(BF16)

    return _row_call(body, name=name, row_ins=[x, cos_t, sa_t, sb_t], res_ins=[g1, sc1, sh1, w_qkv, w_gates, gains],
                     row_outs=[(D_MODEL, BF16), (W_QKV, BF16), (W_QK, BF16), (W_GATES, BF16)])


def _pre_attn_bwd(qk_pre, d_parts, dgates, x, dx1, cos_t, sa_t, sb_t, w_qkv, w_gates, gains, g1, sc1, *, name):
    half = ROT_DIM // 2
    nparts = len(d_parts)
    where = []
    for pi, part in enumerate(d_parts):
        where += [(pi, cj) for cj in range(part.shape[1] // LANES)]
    assert len(where) == W_QKV // LANES

    def body(i, n, rin, res, rout, aout, scr):
        pre_ref, d_refs = rin[0], rin[1:1 + nparts]
        dgates_ref, x_ref, dx1_ref, cos_ref, sa_ref, sb_ref = rin[1 + nparts:]
        wq_ref, wg_ref, gains_ref, g_ref, sc_ref = res
        dqkv_ref, gx_ref = rout
        dgains_ref, sums_ref = aout
        accg, accs = scr

        @pl.when(i == 0)
        def _():
            accg[...] = jnp.zeros_like(accg)
            accs[...] = jnp.zeros_like(accs)

        lo = _head_lanes()
        cosv, sav, sbv = cos_ref[...], sa_ref[...], sb_ref[...]
        dh = _dot_nt(dgates_ref[...], wg_ref[...])
        pre_at = 0
        for c0, c1, kind, rot in _SECTIONS:
            for ch in range((c1 - c0) // LANES):
                pi, cj = where[c0 // LANES + ch]
                dt = d_refs[pi][:, cj * LANES:(cj + 1) * LANES]
                if kind >= 0:
                    if rot:
                        dt = dt * cosv + pltpu.roll(dt * sav, LANES - half, 1) + pltpu.roll(dt * sbv, half, 1)
                    t = pre_ref[:, pre_at:pre_at + LANES].astype(F32)
                    pre_at += LANES
                    rstd = lax.rsqrt(_head_mean(t * t, lo) + EPS)
                    xhat = t * rstd
                    accg[kind] += _fold8(dt * xhat)
                    dxhat = dt * gains_ref[kind:kind + 1, :]
                    dt = rstd * (dxhat - xhat * _head_mean(dxhat * xhat, lo))
                dqkv_ref[:, c0 + ch * LANES:c0 + (ch + 1) * LANES] = dt.astype(BF16)
            dh = dh + _dot_nt(dqkv_ref[:, c0:c1], wq_ref[:, c0:c1])
        dx, t_sh, t_sc, t_g = _rms_mod_bwd(x_ref[...], dh, g_ref[...], sc_ref[...])
        gx_ref[...] = dx1_ref[...] + dx
        accs[0] += _fold8(t_sh)
        accs[1] += _fold8(t_sc)
        accs[2] += _fold8(t_g)

        @pl.when(i == n - 1)
        def _():
            t = jnp.sum(accg[...], axis=1)
            dgains_ref[...] = t + pltpu.roll(t, HEAD_DIM, 1)
            sums_ref[...] = jnp.sum(accs[...], axis=1)

    return _row_call(
        body, name=name, row_ins=[qk_pre, *d_parts, dgates, x, dx1, cos_t, sa_t, sb_t],
        res_ins=[w_qkv, w_gates, gains, g1, sc1], row_outs=[(W_QKV, BF16), (D_MODEL, F32)],
        acc_outs=[((4, LANES), F32), ((3, D_MODEL), F32)],
        scratch=[pltpu.VMEM((4, 8, LANES), F32), pltpu.VMEM((3, 8, D_MODEL), F32)])


def _post_attn_fwd(o_a, o_g, l_g, gates, x, w_pa, w_pb, w_o, b_gate, gt1, g2, sc2, sh2, *, name):
    d = D_MODEL

    def body(i, n, rin, res, rout, aout, scr):
        oa_ref, o0, o1, o2, l0, l1, l2, gates_ref, x_ref = rin
        wpa_ref, wpb_ref, wo_ref, b_ref, gt_ref, g_ref, sc_ref, sh_ref = res
        ob_ref, merged_ref, mo_ref, x1_ref, h2_ref = rout
        ws = _mix_weights([l0[...], l1[...], l2[...]])
        obb = (ws[0] * o0[...] + ws[1] * o1[...] + ws[2] * o2[...]).astype(BF16)
        ob_ref[...] = obb
        pa = _dot(oa_ref[...].astype(BF16), wpa_ref[...])
        pb = _dot(obb, wpb_ref[...])
        ga = _sigmoid(gates_ref[:, :d].astype(F32) + b_ref[:, :d])
        gb = _sigmoid(gates_ref[:, d:].astype(F32) + b_ref[:, d:])
        merged = (ga * pa + gb * pb).astype(BF16)
        merged_ref[...] = merged
        mo = _dot(merged, wo_ref[...])
        mo_ref[...] = mo.astype(BF16)
        x1 = x_ref[...] + gt_ref[...] * mo
        x1_ref[...] = x1
        h2_ref[...] = _rms_mod(x1, g_ref[...], sc_ref[...], sh_ref[...]).astype(BF16)

    return _row_call(body, name=name, row_ins=[o_a, *o_g, *l_g, gates, x],
                     res_ins=[w_pa, w_pb, w_o, b_gate, gt1, g2, sc2, sh2],
                     row_outs=[(WB_OUT, BF16), (d, BF16), (d, BF16), (d, F32), (d, BF16)])


def _ffn_fwd(h2, w_ffn_in, *, name):
    def body(i, n, rin, res, rout, aout, scr):
        (h_ref,), (w_ref,), (act_ref, ff_ref) = rin, res, rout
        hv = h_ref[...]
        for q in range(2):
            a = _dot(hv, w_ref[:, q * FF_CHIP:(q + 1) * FF_CHIP])
            up = _dot(hv, w_ref[:, D_FF + q * FF_CHIP:D_FF + (q + 1) * FF_CHIP])
            sl = slice(q * FF_CHIP, (q + 1) * FF_CHIP)
            act_ref[:, sl] = (a * _sigmoid(a) * up).astype(BF16)
            ff_ref[:, sl] = a.astype(BF16)
            ff_ref[:, D_FF + q * FF_CHIP:D_FF + (q + 1) * FF_CHIP] = up.astype(BF16)

    return _row_call(body, name=name, row_ins=[h2], res_ins=[w_ffn_in], row_outs=[(D_FF, BF16), (2 * D_FF, BF16)])


def _ffn_mid(act, ff, x1, tgt, w_ffn_out, gt2, *, name):
    d = D_MODEL

    def body(i, n, rin, res, rout, aout, scr):
        act_ref, ff_ref, x1_ref, tgt_ref = rin
        wo_ref, gt_ref = res
        dy_ref, dffo_ref, dff_ref = rout
        dgt_ref, loss_ref = aout
        (acc,) = scr

        @pl.when(i == 0)
        def _():
            acc[...] = jnp.zeros_like(acc)

        ffo = _dot(act_ref[...], wo_ref[...])
        gtv = gt_ref[...]
        e = x1_ref[...] + gtv * ffo - tgt_ref[...]
        dy = e * (1.0 / d)
        dy_ref[...] = dy
        dffo = (gtv * dy).astype(BF16)
        dffo_ref[...] = dffo
        acc[0] += _fold8(dy * ffo)
        acc[1] += _fold8(e * e)
        for q in range(2):
            sl = slice(q * FF_CHIP, (q + 1) * FF_CHIP)
            su = slice(D_FF + q * FF_CHIP, D_FF + (q + 1) * FF_CHIP)
            dact = _dot_nt(dffo, wo_ref[sl, :])
            a = ff_ref[:, sl].astype(F32)
            up = ff_ref[:, su].astype(F32)
            sg = _sigmoid(a)
            dff_ref[:, sl] = (dact * up * (sg * (1.0 + a * (1.0 - sg)))).astype(BF16)
            dff_ref[:, su] = (dact * (a * sg)).astype(BF16)

        @pl.when(i == n - 1)
        def _():
            dgt_ref[...] = jnp.sum(acc[0], axis=0, keepdims=True)
            tot = jnp.sum(jnp.sum(acc[1], axis=0, keepdims=True), axis=1, keepdims=True)
            loss_ref[...] = jnp.broadcast_to(tot * (0.5 / d), (1, LANES))

    return _row_call(body, name=name, row_ins=[act, ff, x1, tgt], res_ins=[w_ffn_out, gt2],
                     row_outs=[(d, F32), (d, BF16), (2 * D_FF, BF16)], acc_outs=[((1, d), F32), ((1, LANES), F32)],
                     scratch=[pltpu.VMEM((2, 8, d), F32)])


def _ffn_in_bwd(dff, x1, dy, mo, w_ffn_in, g2, sc2, gt1, *, name):
    d = D_MODEL

    def body(i, n, rin, res, rout, aout, scr):
        dff_ref, x1_ref, dy_ref, mo_ref = rin
        w_ref, g_ref, sc_ref, gt_ref = res
        dx1_ref, dmo_ref = rout
        (sums_ref,) = aout
        (acc,) = scr

        @pl.when(i == 0)
        def _():
            acc[...] = jnp.zeros_like(acc)

        dh = _dot_nt(dff_ref[...], w_ref[...])
        dx, t_sh, t_sc, t_g = _rms_mod_bwd(x1_ref[...], dh, g_ref[...], sc_ref[...])
        dx1 = dy_ref[...] + dx
        dx1_ref[...] = dx1
        dmo_ref[...] = (gt_ref[...] * dx1).astype(BF16)
        acc[0] += _fold8(t_sh)
        acc[1] += _fold8(t_sc)
        acc[2] += _fold8(t_g)
        acc[3] += _fold8(dx1 * mo_ref[...].astype(F32))

        @pl.when(i == n - 1)
        def _():
            sums_ref[...] = jnp.sum(acc[...], axis=1)

    return _row_call(body, name=name, row_ins=[dff, x1, dy, mo], res_ins=[w_ffn_in, g2, sc2, gt1],
                     row_outs=[(d, F32), (d, BF16)], acc_outs=[((4, d), F32)], scratch=[pltpu.VMEM((4, 8, d), F32)])


def _post_attn_bwd(dmo, gates, o_a, o_g, l_g, w_pa, w_pb, w_o, b_gate, *, name):
    d = D_MODEL

    def body(i, n, rin, res, rout, aout, scr):
        dmo_ref, gates_ref, oa_ref, o0, o1, o2, l0, l1, l2 = rin
        wpa_ref, wpb_ref, wo_ref, b_ref = res
        dpa_ref, dpb_ref, dgates_ref, doa_ref = rout[:4]
        do_refs, dl_refs = rout[4:7], rout[7:10]
        (dbg_ref,) = aout
        (acc,) = scr

        @pl.when(i == 0)
        def _():
            acc[...] = jnp.zeros_like(acc)

        ogs = [o0[...], o1[...], o2[...]]
        ws = _mix_weights([l0[...], l1[...], l2[...]])
        obb = (ws[0] * ogs[0] + ws[1] * ogs[1] + ws[2] * ogs[2]).astype(BF16)
        pa = _dot(oa_ref[...].astype(BF16), wpa_ref[...])
        pb = _dot(obb, wpb_ref[...])
        ga = _sigmoid(gates_ref[:, :d].astype(F32) + b_ref[:, :d])
        gb = _sigmoid(gates_ref[:, d:].astype(F32) + b_ref[:, d:])
        dm = _dot_nt(dmo_ref[...], wo_ref[...])
        dpa = (dm * ga).astype(BF16)
        dpb = (dm * gb).astype(BF16)
        dpa_ref[...] = dpa
        dpb_ref[...] = dpb
        dga = dm * pa * ga * (1.0 - ga)
        dgb = dm * pb * gb * (1.0 - gb)
        dgates_ref[:, :d] = dga.astype(BF16)
        dgates_ref[:, d:] = dgb.astype(BF16)
        acc[:, :d] += _fold8(dga)
        acc[:, d:] += _fold8(dgb)
        doa_ref[...] = _dot_nt(dpa, wpa_ref[...])
        dob = _dot_nt(dpb, wpb_ref[...])
        lo = _head_lanes()
        for ch in range(WB_OUT // LANES):
            sl = slice(ch * LANES, (ch + 1) * LANES)
            dv = dob[:, sl]
            wc = [w[:, sl] for w in ws]
            ts = [_head_mean(dv * og[:, sl], lo) * float(HEAD_DIM) for og in ogs]
            tbar = wc[0] * ts[0] + wc[1] * ts[1] + wc[2] * ts[2]
            for g in range(3):
                do_refs[g][:, sl] = wc[g] * dv
                dl_refs[g][:, sl] = wc[g] * (ts[g] - tbar)

        @pl.when(i == n - 1)
        def _():
            dbg_ref[...] = jnp.sum(acc[...], axis=0, keepdims=True)

    return _row_call(body, name=name, row_ins=[dmo, gates, o_a, *o_g, *l_g], res_ins=[w_pa, w_pb, w_o, b_gate],
                     row_outs=[(d, BF16), (d, BF16), (2 * d, BF16), (WA, F32)] + [(WB_OUT, F32)] * 6,
                     acc_outs=[((1, 2 * d), F32)], scratch=[pltpu.VMEM((8, 2 * d), F32)])


def _na_class_tables():
    ro = np.full((3, NA_QROWS, 2 * NA_PAIRS), NA_RO_NONE, np.int64)
    slot = np.zeros((3, NA_QROWS, NA_PAIRS), np.int64)
    for t in range(3):
        for a in range(NA_QROWS):
            qr = _NA_CLASS_R0[t] + a
            rs = min(max(qr - NA_KH // 2, 0), ROWS - NA_KH)
            for b in range(NA_KROWS):
                kr = _NA_CLASS_K0[t] + b
                if rs <= kr < rs + NA_KH:
                    ro[t, a, b] = kr - qr + (NA_KH - 1)
            for j in range(NA_PAIRS):
                slot[t, a, j] = 2 * j - a + (_NA_CLASS_K0[t] - _NA_CLASS_R0[t] + NA_KH - 1) + (NA_QROWS - 1)
    assert slot.min() >= 0 and slot.max() < NA_SLOTS
    return ro, slot


def _na_build_bias(i, cls_ref, rp_ref, cm_ref, bias_scr):
    ro, _ = _na_class_tables()
    lo = _head_lanes()
    first = jnp.logical_or(i == 0, cls_ref[i] != cls_ref[jnp.maximum(i - 1, 0)])
    for t in range(3):
        @pl.when(jnp.logical_and(first, cls_ref[i] == t))
        def _():
            for hh in range(2):
                for a in range(NA_QROWS):
                    for j in range(NA_PAIRS):
                        r0, r1 = int(ro[t, a, 2 * j]), int(ro[t, a, 2 * j + 1])
                        x0 = jnp.broadcast_to(rp_ref[hh, r0:r0 + 1, :], (GRID_W, LANES))
                        x1 = jnp.broadcast_to(rp_ref[hh, r1:r1 + 1, :], (GRID_W, LANES))
                        blk = jnp.where(lo, pltpu.roll(x0, GRID_W + 1, 1, stride=1, stride_axis=0),
                                        pltpu.roll(x1, 1, 1, stride=1, stride_axis=0))
                        bias_scr[hh, a * GRID_W:(a + 1) * GRID_W, j * LANES:(j + 1) * LANES] = blk + cm_ref[...]
    return first


def _attn_fwd(qkv, qc0, kc0, vc0, npairs, table, kstart, cls, nk, *, name, na=None):
    s = qkv.shape[0]

    def body(ks_ref, cls_ref, q_ref, k_ref, v_ref, b_ref, *rest):
        if na:
            cm_ref, o_ref, lse_ref, bias_scr = rest
        else:
            o_ref, lse_ref = rest
        i = pl.program_id(1)
        if na:
            _na_build_bias(i, cls_ref, b_ref, cm_ref, bias_scr)
        ks = pl.multiple_of(ks_ref[i], 64)
        q2 = q_ref[...]
        k2 = k_ref[pl.ds(ks, nk), :]
        v2 = v_ref[pl.ds(ks, nk), :]
        lo = _head_lanes()
        outs, lses = [], []
        for h in range(2):
            qm = jnp.where(lo if h == 0 else jnp.logical_not(lo), q2, jnp.zeros_like(q2))
            sc = _dot_nt(qm, k2) * SCALE + (bias_scr[h, :, :nk] if na else b_ref[0, 0])
            m = jnp.max(sc, axis=1, keepdims=True)
            p = jnp.exp(sc - m)
            l = jnp.sum(p, axis=1, keepdims=True)
            pv = _dot(p.astype(BF16), v2)
            outs.append(pv / l)
            lses.append(m + jnp.log(l))
        o_ref[...] = jnp.where(lo, outs[0], outs[1])
        lse_ref[...] = jnp.where(lo, lses[0], lses[1])

    w = npairs * LANES
    in_specs = [
        pl.BlockSpec((Q_BLOCK, LANES), lambda p, i, ks, cl: (i, qc0 + p)),
        pl.BlockSpec((s, LANES), lambda p, i, ks, cl: (0, kc0 + p)),
        pl.BlockSpec((s, LANES), lambda p, i, ks, cl: (0, vc0 + p)),
    ]
    if na:
        in_specs += _na_bias_specs()
        args, scratch = (kstart, cls, qkv, qkv, qkv, *na), [pltpu.VMEM((2, Q_BLOCK, NA_W), F32)]
    else:
        in_specs.append(pl.BlockSpec((1, 1, Q_BLOCK, nk), lambda p, i, ks, cl: (cl[i], 0, 0, 0)))
        args, scratch = (kstart, cls, qkv, qkv, qkv, table), []
    grid_spec = pltpu.PrefetchScalarGridSpec(
        num_scalar_prefetch=2, grid=(npairs, N_QBLK), in_specs=in_specs,
        out_specs=[pl.BlockSpec((Q_BLOCK, LANES), lambda p, i, ks, cl: (i, p)),
                   pl.BlockSpec((Q_BLOCK, LANES), lambda p, i, ks, cl: (i, p))],
        scratch_shapes=scratch,
    )
    return _pcall(body, name=name, grid_spec=grid_spec,
                  out_shape=[jax.ShapeDtypeStruct((s, w), F32), jax.ShapeDtypeStruct((s, w), F32)],
                  compiler_params=pltpu.CompilerParams(dimension_semantics=("parallel", "arbitrary")),
                  )(*args)


def _na_bias_specs():
    return [pl.BlockSpec((2, 16, LANES), lambda p, i, ks, cl: (p, 0, 0)),
            pl.BlockSpec((GRID_W, LANES), lambda p, i, ks, cl: (0, 0))]


def _attn_bwd(qkv, qc0, kc0, vc0, npairs, table, kstart, cls, nk, do, o, lse, *, name, dlse=None, na=None):
    s = qkv.shape[0]
    has_dlse = dlse is not None
    _, slot = _na_class_tables()

    def body(ks_ref, cls_ref, q_ref, k_ref, v_ref, b_ref, *rest):
        if na:
            cm_ref, rest = rest[0], rest[1:]
        do_ref, o_ref, lse_ref, rest = rest[0], rest[1], rest[2], rest[3:]
        if has_dlse:
            dlse_ref, rest = rest[0], rest[1:]
        dq_ref, dk_ref, dv_ref = rest[0], rest[1], rest[2]
        if na:
            bank_ref, bias_scr, dbias_scr, bank_scr = rest[3:]
        i = pl.program_id(1)
        if na:
            first = _na_build_bias(i, cls_ref, b_ref, cm_ref, bias_scr)

        @pl.when(i == 0)
        def _():
            dk_ref[...] = jnp.zeros_like(dk_ref)
            dv_ref[...] = jnp.zeros_like(dv_ref)
            if na:
                dbias_scr[...] = jnp.zeros_like(dbias_scr)
                bank_scr[...] = jnp.zeros_like(bank_scr)

        ks = pl.multiple_of(ks_ref[i], 64)
        q2 = q_ref[...]
        k2 = k_ref[pl.ds(ks, nk), :]
        v2 = v_ref[pl.ds(ks, nk), :]
        do2 = do_ref[...]
        lse2 = lse_ref[...]
        doo = do2 * o_ref[...]
        do2b = do2.astype(BF16)
        lo = _head_lanes()
        lane = lax.broadcasted_iota(jnp.int32, (1, LANES), 1)
        dqs, dks, dvs = [], [], []
        for h in range(2):
            mh = lo if h == 0 else jnp.logical_not(lo)
            qm = jnp.where(mh, q2, jnp.zeros_like(q2))
            sc = _dot_nt(qm, k2) * SCALE + (bias_scr[h, :, :nk] if na else b_ref[0, 0])
            lse_h = jnp.max(jnp.where(mh, lse2, NEG), axis=1, keepdims=True)
            p = jnp.exp(sc - lse_h)
            delta = jnp.sum(jnp.where(mh, doo, 0.0), axis=1, keepdims=True)
            dom = jnp.where(mh, do2b, jnp.zeros_like(do2b))
            dp = _dot_nt(dom, v2)
            t = dp - delta
            if has_dlse:
                t = t + jnp.sum(jnp.where(lane == h * HEAD_DIM, dlse_ref[...], 0.0), axis=1, keepdims=True)
            ds = p * t
            if na:
                @pl.when(first)
                def _():
                    dbias_scr[h, :, :nk] = ds

                @pl.when(jnp.logical_not(first))
                def _():
                    dbias_scr[h, :, :nk] += ds
            dsb = ds.astype(BF16)
            dqs.append(_dot(dsb, k2))
            dks.append(lax.dot_general(dsb, q2, _TN, preferred_element_type=F32))
            dvs.append(lax.dot_general(p.astype(BF16), do2b, _TN, preferred_element_type=F32))
        dq_ref[...] = jnp.where(lo, dqs[0], dqs[1]) * SCALE
        dk_ref[pl.ds(ks, nk), :] += jnp.where(lo, dks[0], dks[1]) * SCALE
        dv_ref[pl.ds(ks, nk), :] += jnp.where(lo, dvs[0], dvs[1])
        if na:
            last = jnp.logical_or(i == N_QBLK - 1, cls_ref[i] != cls_ref[jnp.minimum(i + 1, N_QBLK - 1)])
            for t in range(3):
                @pl.when(jnp.logical_and(last, cls_ref[i] == t))
                def _():
                    for hh in range(2):
                        for a in range(NA_QROWS):
                            for j in range(NA_PAIRS):
                                bank_scr[hh, int(slot[t, a, j])] += dbias_scr[
                                    hh, a * GRID_W:(a + 1) * GRID_W, j * LANES:(j + 1) * LANES]

            @pl.when(i == N_QBLK - 1)
            def _():
                bank_ref[...] = bank_scr[...]

    w = npairs * LANES
    blk = lambda: pl.BlockSpec((Q_BLOCK, LANES), lambda p, i, ks, cl: (i, p))
    full = lambda: pl.BlockSpec((s, LANES), lambda p, i, ks, cl: (0, p))
    in_specs = [
        pl.BlockSpec((Q_BLOCK, LANES), lambda p, i, ks, cl: (i, qc0 + p)),
        pl.BlockSpec((s, LANES), lambda p, i, ks, cl: (0, kc0 + p)),
        pl.BlockSpec((s, LANES), lambda p, i, ks, cl: (0, vc0 + p)),
    ]
    if na:
        in_specs += _na_bias_specs()
        args = [kstart, cls, qkv, qkv, qkv, *na]
    else:
        in_specs.append(pl.BlockSpec((1, 1, Q_BLOCK, nk), lambda p, i, ks, cl: (cl[i], 0, 0, 0)))
        args = [kstart, cls, qkv, qkv, qkv, table]
    in_specs += [blk(), blk(), blk()]
    args += [do, o, lse]
    if has_dlse:
        in_specs.append(blk())
        args.append(dlse)
    out_specs = [blk(), full(), full()]
    out_shape = [jax.ShapeDtypeStruct((s, w), F32)] * 3
    scratch = []
    if na:
        bank_shape = (2, NA_SLOTS, GRID_W, LANES)
        out_specs.append(pl.BlockSpec(bank_shape, lambda p, i, ks, cl: (p, 0, 0, 0)))
        out_shape.append(jax.ShapeDtypeStruct((2 * npairs,) + bank_shape[1:], F32))
        scratch = [pltpu.VMEM((2, Q_BLOCK, NA_W), F32), pltpu.VMEM((2, Q_BLOCK, NA_W), F32), pltpu.VMEM(bank_shape, F32)]
    grid_spec = pltpu.PrefetchScalarGridSpec(num_scalar_prefetch=2, grid=(npairs, N_QBLK), in_specs=in_specs,
                                             out_specs=out_specs, scratch_shapes=scratch)
    return _pcall(body, name=name, grid_spec=grid_spec, out_shape=out_shape,
                  compiler_params=pltpu.CompilerParams(dimension_semantics=("arbitrary", "arbitrary")))(*args)


_NA_CLASS_R0 = (0, NA_QROWS, ROWS - NA_QROWS)
_NA_CLASS_K0 = (0, 0, ROWS - NA_KROWS)
_RPB_RO = 2 * NA_KH - 1
_RPB_CO = 2 * NA_KW - 1
_BANK_ROWS = 48


def _na_constants():
    col = np.arange(GRID_W)
    cs = np.clip(col - NA_KW // 2, 0, GRID_W - NA_KW)
    vcol = (col[None, :] >= cs[:, None]) & (col[None, :] < cs[:, None] + NA_KW)
    colmask = np.where(np.concatenate([vcol, vcol], axis=1), 0.0, NEG).astype(np.float32)
    co = col[None, :] - col[:, None] + (NA_KW - 1)
    oh_col = np.zeros((GRID_W * GRID_W, LANES), np.float32)
    for qc in range(GRID_W):
        for kc in range(GRID_W):
            if vcol[qc, kc]:
                oh_col[qc * GRID_W + kc, co[qc, kc]] = 1.0
    ks = np.clip(np.arange(N_QBLK) * NA_QROWS - NA_KH // 2, 0, ROWS - NA_KROWS) * GRID_W
    cls = np.ones(N_QBLK, np.int32)
    cls[0], cls[-1] = 0, 2
    return colmask, oh_col, ks.astype(np.int32), cls


def _bank_reduce(bank, oh_col, *, name):
    def body(d_ref, ohc_ref, o_ref):
        o_ref[0] = jnp.dot(d_ref[0], ohc_ref[...], preferred_element_type=F32, precision=lax.Precision.HIGHEST)

    return _pcall(
        body, name=name, grid=(NA_HEADS,),
        in_specs=[pl.BlockSpec((1, _BANK_ROWS, GRID_W * GRID_W), lambda h: (h, 0, 0)),
                  pl.BlockSpec((GRID_W * GRID_W, LANES), lambda h: (0, 0))],
        out_specs=pl.BlockSpec((1, _BANK_ROWS, LANES), lambda h: (h, 0, 0)),
        out_shape=jax.ShapeDtypeStruct((NA_HEADS, _BANK_ROWS, LANES), F32), compiler_params=_PAR,
    )(bank, oh_col)


def _dil_constants(dilation):
    seg = SEQ // dilation
    nb = seg // Q_BLOCK
    shift = (0, -DIL_HALF, -2 * DIL_HALF)
    qi = np.arange(Q_BLOCK)[:, None]
    ki = np.arange(DIL_NK)[None, :]
    mask = np.stack([np.where(np.abs(ki + sh - qi) <= DIL_HALF, 0.0, NEG) for sh in shift]).astype(np.float32)
    ks, cls = [], []
    for i in range(N_QBLK):
        sub, blk = divmod(i, nb)
        t = 0 if blk == 0 else (2 if blk == nb - 1 else 1)
        cls.append(t)
        ks.append(sub * seg + blk * Q_BLOCK + shift[t])
    return mask.reshape(3, 1, Q_BLOCK, DIL_NK), np.asarray(ks, np.int32), np.asarray(cls, np.int32)


_VM = pl.BlockSpec(memory_space=pltpu.VMEM)


def _ada_fwd(c_all, w, b, *, name):
    def body(c_ref, w_ref, b_ref, o_ref):
        cv = c_ref[...]
        o_ref[...] = jnp.dot(cv * _sigmoid(cv), w_ref[...], preferred_element_type=F32,
                             precision=lax.Precision.HIGHEST) + b_ref[...]

    return _pcall(body, name=name, in_specs=[_VM, _VM, _VM], out_specs=_VM,
                  out_shape=jax.ShapeDtypeStruct((c_all.shape[0], w.shape[1]), F32))(c_all, w, b)


def _ada_bwd(c_all_t, dmod, *, name):
    def body(c_ref, d_ref, o_ref):
        cv = c_ref[...]
        o_ref[...] = jnp.dot(cv * _sigmoid(cv), d_ref[...], preferred_element_type=F32,
                             precision=lax.Precision.HIGHEST)

    return _pcall(body, name=name, in_specs=[_VM, _VM], out_specs=_VM,
                  out_shape=jax.ShapeDtypeStruct((c_all_t.shape[0], dmod.shape[1]), F32))(c_all_t, dmod)


def _row_sum(t, *, name):
    def body(t_ref, o_ref):
        o_ref[...] = jnp.sum(t_ref[...], axis=0, keepdims=True)

    return _pcall(body, name=name, in_specs=[_VM], out_specs=_VM,
                  out_shape=jax.ShapeDtypeStruct((1, t.shape[1]), F32))(t)


def _row_tile(rows):
    tr = rows
    for cand in range(8, 513, 8):
        if rows % cand == 0:
            tr = cand
    return tr


def _adamw_math(wv, gv, mv, vv):
    nm = ADAM_B1 * mv + (1.0 - ADAM_B1) * gv
    nv = ADAM_B2 * vv + (1.0 - ADAM_B2) * (gv * gv)
    m_hat = nm / (1.0 - ADAM_B1 ** ADAM_STEP)
    v_hat = nv / (1.0 - ADAM_B2 ** ADAM_STEP)
    return -ADAM_LR * (m_hat / (jnp.sqrt(v_hat) + ADAM_EPS) + ADAM_WD * wv), nm, nv


def _adamw(w, g, m, v, *, name):
    rows, cols = w.shape
    tr = _row_tile(rows)

    def body(w_ref, g_ref, m_ref, v_ref, d_ref, nm_ref, nv_ref):
        d_ref[...], nm_ref[...], nv_ref[...] = _adamw_math(w_ref[...], g_ref[...], m_ref[...], v_ref[...])

    spec = pl.BlockSpec((tr, cols), lambda i: (i, 0))
    return _pcall(body, name=name, grid=(rows // tr,), in_specs=[spec] * 4, out_specs=[spec] * 3,
                  out_shape=[jax.ShapeDtypeStruct((rows, cols), F32)] * 3, compiler_params=_PAR)(w, g, m, v)


def _adamw_halves(w, g_mine, g_other, m, v, c_arr, *, name):
    rows, cols = w.shape
    hr = rows // 2
    tr = _row_tile(hr)
    nt = hr // tr

    def body(c_ref, w_ref, t_ref, o_ref, m_ref, v_ref, g_ref, d_ref, nm_ref, nv_ref):
        gv = jnp.where(pl.program_id(0) == c_ref[0], t_ref[...], o_ref[...])
        g_ref[...] = gv
        d_ref[...], nm_ref[...], nv_ref[...] = _adamw_math(w_ref[...], gv, m_ref[...], v_ref[...])

    full = pl.BlockSpec((tr, cols), lambda h, i, c: (h * nt + i, 0))
    half = pl.BlockSpec((tr, cols), lambda h, i, c: (i, 0))
    grid_spec = pltpu.PrefetchScalarGridSpec(num_scalar_prefetch=1, grid=(2, nt),
                                             in_specs=[full, half, half, full, full], out_specs=[full] * 4)
    return _pcall(body, name=name, grid_spec=grid_spec, out_shape=[jax.ShapeDtypeStruct((rows, cols), F32)] * 4,
                  compiler_params=pltpu.CompilerParams(dimension_semantics=("parallel", "parallel")),
                  )(c_arr, w, g_mine, g_other, m, v)


_MESH = pl.DeviceIdType.MESH
_ANY = pl.BlockSpec(memory_space=pl.ANY)
_CHIP_FLIPS = ((1, 0), (0, 1), (1, 1))


def _pos():
    return lax.axis_index("x"), lax.axis_index("y"), lax.axis_index("c")


def _flip(v, f):
    return 1 - v if f else v


def _sem_pairs(n):
    return [pltpu.SemaphoreType.DMA((n,)), pltpu.SemaphoreType.DMA((n,))]


def _small_allgather(blk, *, name):
    m_per, n = blk.shape

    def body(x_ref, out_ref, send_sems, recv_sems, local_sem):
        x, y, c = _pos()
        me, sibling = (x, y, c), (x, y, 1 - c)
        chips = [(_flip(x, fx), _flip(y, fy)) for fx, fy in _CHIP_FLIPS]

        def rows(px, py, pc):
            return out_ref.at[pl.ds((4 * px + 2 * py + pc) * m_per, m_per), :]

        def copy(k, block, to, src=None):
            return pltpu.make_async_remote_copy(
                src_ref=rows(*block) if src is None else src, dst_ref=rows(*block),
                send_sem=send_sems.at[k], recv_sem=recv_sems.at[k], device_id=to, device_id_type=_MESH)

        mine = pltpu.make_async_copy(x_ref, rows(*me), local_sem)
        mine.start()
        first = [copy(0, me, sibling, src=x_ref)]
        first += [copy(1 + j, me, (*chip, c), src=x_ref) for j, chip in enumerate(chips)]
        for cp in first:
            cp.start()
        passed = [copy(4 + j, (*chip, c), sibling) for j, chip in enumerate(chips)]
        for j, chip in enumerate(chips):
            copy(1 + j, (*chip, c), me).wait_recv()
            passed[j].start()
        copy(0, sibling, me).wait_recv()
        for j, chip in enumerate(chips):
            copy(4 + j, (*chip, 1 - c), me).wait_recv()
        for cp in first + passed:
            cp.wait_send()
        mine.wait()

    return _pcall(
        body, name=name, out_shape=jax.ShapeDtypeStruct((N_DEV * m_per, n), blk.dtype),
        in_specs=[_VM], out_specs=_VM,
        scratch_shapes=_sem_pairs(7) + [pltpu.SemaphoreType.DMA],
    )(blk)


_HBM = pl.BlockSpec(memory_space=pltpu.HBM)
_SEM = pl.BlockSpec(memory_space=pltpu.SEMAPHORE)
_EFFECT = pltpu.SideEffectType.DATAFLOW_SIDE_EFFECTING


def _split_start(srcs, lands, plan, ncopies, after, *, name):
    ns, nl = len(srcs), len(lands)

    def body(*refs):
        src_refs, land_refs = refs[:ns], refs[ns:ns + nl]
        send_sems, recv_sems = refs[ns + nl + 1], refs[ns + nl + 2]
        token = refs[-1]
        x, y, c = _pos()
        for k, (src, dst, to, _) in enumerate(plan(x, y, c, src_refs, land_refs)):
            pltpu.make_async_remote_copy(src_ref=src, dst_ref=dst, send_sem=send_sems.at[k], recv_sem=recv_sems.at[k],
                                         device_id=to, device_id_type=_MESH).start()
        token[...] = jnp.zeros_like(token)

    hbm = lambda a: pltpu.HBM(a.shape, a.dtype)
    out = _pcall(
        body, name=name,
        out_shape=(pltpu.SemaphoreType.DMA((ncopies,)), pltpu.SemaphoreType.DMA((ncopies,)),
                   *[hbm(a) for a in srcs], *[hbm(a) for a in lands], jax.ShapeDtypeStruct((8, LANES), F32)),
        in_specs=[_HBM] * (ns + nl) + [_ANY], out_specs=(_SEM, _SEM, *[_HBM] * (ns + nl), _VM),
        input_output_aliases={i: 2 + i for i in range(ns + nl)},
        compiler_params=pltpu.CompilerParams(has_side_effects=_EFFECT),
    )(*[pltpu.with_memory_space_constraint(a, pltpu.HBM) for a in (*srcs, *lands)], after)
    return out[0], out[1], list(out[2:2 + ns]), list(out[2 + ns:2 + ns + nl]), out[-1]


def _split_wait(send_sems, recv_sems, srcs, lands, plan, after, *, name):
    ns, nl = len(srcs), len(lands)

    def body(*refs):
        src_refs, land_refs = refs[:ns], refs[ns:ns + nl]
        send_sems, recv_sems = refs[ns + nl], refs[ns + nl + 1]
        x, y, c = _pos()
        for k, (src, _, _, mine) in enumerate(plan(x, y, c, src_refs, land_refs)):
            cp = pltpu.make_async_remote_copy(src_ref=src, dst_ref=mine, send_sem=send_sems.at[k],
                                              recv_sem=recv_sems.at[k], device_id=(x, y, c), device_id_type=_MESH)
            cp.wait_send()
            cp.wait_recv()

    hbm = lambda a: pltpu.HBM(a.shape, a.dtype)
    out = _pcall(
        body, name=name, out_shape=tuple(hbm(a) for a in (*srcs, *lands)),
        in_specs=[_HBM] * (ns + nl) + [_SEM, _SEM, _ANY], out_specs=tuple([_HBM] * (ns + nl)),
        input_output_aliases={i: i for i in range(ns + nl)},
        compiler_params=pltpu.CompilerParams(has_side_effects=_EFFECT),
    )(*srcs, *lands, send_sems, recv_sems, after)
    return list(out[ns:])


def _ag_plan(nw):
    def plan(x, y, c, sh_refs, full_refs):
        j = 2 * x + y
        out = []
        for wi in range(nw):
            for fx, fy in _CHIP_FLIPS:
                px, py = _flip(x, fx), _flip(y, fy)
                out.append((sh_refs[wi].at[c], full_refs[wi].at[j, c], (px, py, c), full_refs[wi].at[2 * px + py, c]))
            out.append((sh_refs[wi], full_refs[wi].at[j], (x, y, 1 - c), full_refs[wi].at[j]))
        return out
    return plan


def _ag_pass(fulls, *, name):
    nw = len(fulls)

    def body(*refs):
        in_refs, out_refs = refs[:nw], refs[nw:2 * nw]
        send_sems, recv_sems = refs[2 * nw:]
        x, y, c = _pos()
        cps = []
        for wi in range(nw):
            for k, (fx, fy) in enumerate(_CHIP_FLIPS):
                jp = 2 * _flip(x, fx) + _flip(y, fy)
                sems = dict(send_sem=send_sems.at[3 * wi + k], recv_sem=recv_sems.at[3 * wi + k], device_id_type=_MESH)
                send = pltpu.make_async_remote_copy(src_ref=in_refs[wi].at[jp, c], dst_ref=out_refs[wi].at[jp, c],
                                                    device_id=(x, y, 1 - c), **sems)
                recv = pltpu.make_async_remote_copy(src_ref=in_refs[wi].at[jp, c], dst_ref=out_refs[wi].at[jp, 1 - c],
                                                    device_id=(x, y, c), **sems)
                cps.append((send, recv))
        for send, _ in cps:
            send.start()
        for send, recv in cps:
            send.wait_send()
            recv.wait_recv()

    return _pcall(body, name=name, out_shape=[jax.ShapeDtypeStruct(f.shape, f.dtype) for f in fulls],
                  in_specs=[_ANY] * nw, out_specs=[_ANY] * nw, input_output_aliases={i: i for i in range(nw)},
                  scratch_shapes=_sem_pairs(3 * nw))(*fulls)


def _rs_plan(nw):
    def plan(x, y, c, s_refs, rb_refs):
        out = []
        for wi in range(nw):
            for k, (fx, fy) in enumerate(_CHIP_FLIPS):
                px, py = _flip(x, fx), _flip(y, fy)
                out.append((s_refs[wi].at[2 * px + py], rb_refs[wi].at[k], (px, py, c), rb_refs[wi].at[k]))
        return out
    return plan


def _sibling_send_halves(gs, *, name):
    nw = len(gs)

    def body(*refs):
        g_refs, out_refs = refs[:nw], refs[nw:2 * nw]
        send_sems, recv_sems = refs[2 * nw:]
        x, y, c = _pos()
        cps = [pltpu.make_async_remote_copy(src_ref=g_refs[wi].at[k, 1 - c], dst_ref=out_refs[wi].at[k],
                                            send_sem=send_sems.at[4 * wi + k], recv_sem=recv_sems.at[4 * wi + k],
                                            device_id=(x, y, 1 - c), device_id_type=_MESH)
               for wi in range(nw) for k in range(N_CHIP)]
        for cp in cps:
            cp.start()
        for cp in cps:
            cp.wait()

    return _pcall(body, name=name,
                  out_shape=[jax.ShapeDtypeStruct((g.shape[0],) + g.shape[2:], g.dtype) for g in gs],
                  in_specs=[_ANY] * nw, out_specs=[_ANY] * nw, scratch_shapes=_sem_pairs(N_CHIP * nw))(*gs)


def _sibling_swap(ts, *, name):
    nw = len(ts)

    def body(*refs):
        t_refs, out_refs = refs[:nw], refs[nw:2 * nw]
        send_sems, recv_sems = refs[2 * nw:]
        x, y, c = _pos()
        cps = [pltpu.make_async_remote_copy(src_ref=t_refs[wi], dst_ref=out_refs[wi], send_sem=send_sems.at[wi],
                                            recv_sem=recv_sems.at[wi], device_id=(x, y, 1 - c), device_id_type=_MESH)
               for wi in range(nw)]
        for cp in cps:
            cp.start()
        for cp in cps:
            cp.wait()

    return _pcall(body, name=name, out_shape=[jax.ShapeDtypeStruct(t.shape, t.dtype) for t in ts],
                  in_specs=[_ANY] * nw, out_specs=[_ANY] * nw, scratch_shapes=_sem_pairs(nw))(*ts)


def _rs_add(g, ra, c_arr, *, name):
    n, _, r, w = g.shape

    def body(c_ref, g_ref, ra_ref, s_ref, sb_ref):
        t = g_ref[...] + ra_ref[...]
        s_ref[...] = t
        sb_ref[...] = t.astype(BF16)

    grid_spec = pltpu.PrefetchScalarGridSpec(
        num_scalar_prefetch=1, grid=(n,),
        in_specs=[pl.BlockSpec((None, None, r, w), lambda k, c: (k, c[0], 0, 0)),
                  pl.BlockSpec((None, r, w), lambda k, c: (k, 0, 0))],
        out_specs=[pl.BlockSpec((None, r, w), lambda k, c: (k, 0, 0))] * 2)
    return _pcall(body, name=name, grid_spec=grid_spec,
                  out_shape=[jax.ShapeDtypeStruct((n, r, w), F32), jax.ShapeDtypeStruct((n, r, w), BF16)],
                  compiler_params=_PAR)(c_arr, g, ra)


def _rs_final(s, rb, j_arr, *, name):
    _, r, w = s.shape

    def body(j_ref, s_ref, rb_ref, t_ref):
        t_ref[...] = ((s_ref[...] + rb_ref[0].astype(F32)) + rb_ref[1].astype(F32)) + rb_ref[2].astype(F32)

    grid_spec = pltpu.PrefetchScalarGridSpec(
        num_scalar_prefetch=1, grid=(1,),
        in_specs=[pl.BlockSpec((None, r, w), lambda i, j: (j[0], 0, 0)),
                  pl.BlockSpec((3, r, w), lambda i, j: (0, 0, 0))],
        out_specs=pl.BlockSpec((r, w), lambda i, j: (0, 0)))
    return _pcall(body, name=name, grid_spec=grid_spec, out_shape=jax.ShapeDtypeStruct((r, w), F32),
                  compiler_params=_ARB)(j_arr, s, rb)


def _perm_rows(t, d):
    s, w = t.shape
    return t.reshape(s // d, d, w).transpose(1, 0, 2).reshape(s, w)


def _unperm_rows(t, d):
    s, w = t.shape
    return t.reshape(d, s // d, w).transpose(1, 0, 2).reshape(s, w)


def _tile2(g):
    return jnp.concatenate([g, g], axis=1)


_BIG = ("w_in", "w_ffn_in", "w_ffn_out", "w_o", "w_proj_a", "w_proj_b")
_BIG_SHARD = {"w_in": (1024, 1472), "w_ffn_in": (1024, 1408), "w_ffn_out": (704, 1024), "w_o": (256, 1024),
              "w_proj_a": (512, 256), "w_proj_b": (256, 256)}


def _device_step(x2, tgt, mod, first_weights, late_weights, early_grads, g_norm1, g_norm2, b_gate, g_qa, g_ka, g_qb,
                 g_kb, rpb):
    d = D_MODEL
    sh1, sc1, gt1, sh2, sc2, gt2 = [mod[:, k * d:(k + 1) * d] for k in range(6)]

    colmask, oh_col, na_ks, na_cls = _na_constants()
    rp = jnp.pad(rpb, ((0, 0), (0, 16 - _RPB_RO), (RP_LANE0, LANES - RP_LANE0 - _RPB_CO)), constant_values=NEG)
    na = (rp, jnp.asarray(colmask))
    na_ks, na_cls = jnp.asarray(na_ks), jnp.asarray(na_cls)
    dil = [_dil_constants(dd) for _, dd in DIL_CONFIGS]
    tab_d = jnp.asarray(dil[0][0])
    gains = jnp.concatenate([_tile2(g_qa), _tile2(g_ka), _tile2(g_qb), _tile2(g_kb)], axis=0)
    cos_t, sa_t, sb_t = _rope_tables()

    wts = first_weights(cos_t)
    h1, qkvn, qk_pre, gates = _pre_attn_fwd(x2, cos_t, sa_t, sb_t, g_norm1, sc1, sh1, wts["w_qkv"], wts["w_gates"],
                                            gains, name="pre_attn_fwd")
    o_a, lse_a = _attn_fwd(qkvn, 0, 4, 8, 4, None, na_ks, na_cls, NA_NK, name="attn_a_fwd", na=na)
    arrs, o_p, l_p, o_g, l_g = [], [], [], [], []
    for g, (_, dd) in enumerate(DIL_CONFIGS):
        ks_g, cls_g = jnp.asarray(dil[g][1]), jnp.asarray(dil[g][2])
        if dd == 1:
            arr, cb = qkvn, (12, 18, 24)
        else:
            col = lambda base: qkvn[:, base + WB_OUT * g: base + WB_OUT * (g + 1)]
            arr = _perm_rows(jnp.concatenate([col(3 * WA), col(3 * WA + WB), col(3 * WA + 2 * WB)], axis=1), dd)
            cb = (0, 2, 4)
        op, lp = _attn_fwd(arr, cb[0], cb[1], cb[2], 2, tab_d, ks_g, cls_g, DIL_NK, name=f"attn_d{g}_fwd")
        arrs.append((arr, cb, ks_g, cls_g))
        o_p.append(op)
        l_p.append(lp)
        o_g.append(op if dd == 1 else _unperm_rows(op, dd))
        l_g.append(lp if dd == 1 else _unperm_rows(lp, dd))
    wts = dict(wts, **late_weights(o_a))
    o_b, merged, mo, x1, h2 = _post_attn_fwd(o_a, o_g, l_g, gates, x2, wts["w_pa"], wts["w_pb"], wts["w_o"], b_gate,
                                             gt1, g_norm2, sc2, sh2, name="post_attn_fwd")
    act, ff = _ffn_fwd(h2, wts["w_ffn_in"], name="ffn_fwd")

    dy, dffo, dff, dgt2, loss_v = _ffn_mid(act, ff, x1, tgt, wts["w_ffn_out"], gt2, name="ffn_mid")
    grads = {}
    g_ffn_out = _wgrad(act, dffo, name="wg_ffn_out", tm=D_FF // 2, tn=d)
    grads["w_ffn_out"] = g_ffn_out.reshape(N_CHIP, D_FF // N_CHIP, d)
    grads["w_ffn_in"] = _wgrad(h2, dff, name="wg_ffn_in", tm=512, tn=2 * FF_CHIP, chips=N_CHIP)
    dx1, dmo, sums2 = _ffn_in_bwd(dff, x1, dy, mo, wts["w_ffn_in"], g_norm2, sc2, gt1, name="ffn_in_bwd")
    grads["w_o"] = _wgrad(merged, dmo, name="wg_o", tm=d, tn=d).reshape(N_CHIP, d // N_CHIP, d)
    pab = _post_attn_bwd(dmo, gates, o_a, o_g, l_g, wts["w_pa"], wts["w_pb"], wts["w_o"], b_gate, name="post_attn_bwd")
    dpa, dpb, dgates, do_a = pab[:4]
    do_g, dl_g, dbg = pab[4:7], pab[7:10], pab[10]
    g_pa = _wgrad(o_a, dpa, name="wg_pa", tm=WA, tn=d)
    g_pb = _wgrad(o_b, dpb, name="wg_pb", tm=WB_OUT, tn=d)
    grads["w_proj_a"] = g_pa.reshape(WA, N_CHIP, d // N_CHIP).transpose(1, 0, 2)
    grads["w_proj_b"] = g_pb.reshape(WB_OUT, N_CHIP, d // N_CHIP).transpose(1, 0, 2)
    early_grads(grads)
    dqa, dka, dva, bank = _attn_bwd(qkvn, 0, 4, 8, 4, None, na_ks, na_cls, NA_NK, do_a, o_a, lse_a,
                                    name="attn_a_bwd", na=na)
    dqs, dks, dvs = [], [], []
    for g, (_, dd) in enumerate(DIL_CONFIGS):
        arr, cb, ks_g, cls_g = arrs[g]
        dog = do_g[g] if dd == 1 else _perm_rows(do_g[g], dd)
        dlg = dl_g[g] if dd == 1 else _perm_rows(dl_g[g], dd)
        dq, dk, dv = _attn_bwd(arr, cb[0], cb[1], cb[2], 2, tab_d, ks_g, cls_g, DIL_NK, dog, o_p[g], l_p[g],
                               name=f"attn_d{g}_bwd", dlse=dlg)
        if dd != 1:
            dq, dk, dv = _unperm_rows(dq, dd), _unperm_rows(dk, dd), _unperm_rows(dv, dd)
        dqs.append(dq)
        dks.append(dk)
        dvs.append(dv)
    dqkv, grad_x, dgains, sums1 = _pre_attn_bwd(qk_pre, [dqa, dka, dva] + dqs + dks + dvs, dgates, x2, dx1, cos_t, sa_t,
                                                sb_t, wts["w_qkv"], wts["w_gates"], gains, g_norm1, sc1,
                                                name="pre_attn_bwd")
    g_qkv = _wgrad(h1, dqkv, name="wg_qkv", tm=d, tn=W_QKV // 2)
    g_gates = _wgrad(h1, dgates, name="wg_gates", tm=d, tn=W_GATES)
    nc, cut = _BIG_SHARD["w_in"][1], 3 * _BIG_SHARD["w_in"][1] - W_QKV
    grads["w_in"] = jnp.stack([g_qkv[:, :nc], g_qkv[:, nc:2 * nc],
                               jnp.concatenate([g_qkv[:, 2 * nc:], g_gates[:, :cut]], axis=1), g_gates[:, cut:]])

    bank = bank.reshape(NA_HEADS, NA_SLOTS, GRID_W, 2, GRID_W).transpose(0, 1, 3, 2, 4)
    bank = jnp.pad(bank.reshape(NA_HEADS, 2 * NA_SLOTS, GRID_W * GRID_W), ((0, 0), (0, _BANK_ROWS - 2 * NA_SLOTS), (0, 0)))
    g2 = _bank_reduce(bank, jnp.asarray(oh_col), name="rpb_reduce")[:, :2 * NA_SLOTS].reshape(NA_HEADS, NA_SLOTS, 2, LANES)
    g_rpb = g2[:, 3:3 + _RPB_RO, 0, :_RPB_CO] + g2[:, 2:2 + _RPB_RO, 1, :_RPB_CO]

    dmod = jnp.concatenate([sums1[0:1], sums1[1:2], sums2[3:4], sums2[0:1], sums2[1:2], dgt2], axis=1)
    small = dict(g_norm1=sums1[2:3], g_norm2=sums2[2:3], b_gate=dbg, g_qa=dgains[0:1, :HEAD_DIM],
                 g_ka=dgains[1:2, :HEAD_DIM], g_qb=dgains[2:3, :HEAD_DIM], g_kb=dgains[3:4, :HEAD_DIM], rpb=g_rpb)
    return loss_v, grad_x, grads, dmod, small


_SMALL = ("b_ada", "g_norm1", "g_norm2", "b_gate", "g_qa", "g_ka", "g_qb", "g_kb", "rpb")
_SMALL_N = {"b_ada": 6 * D_MODEL, "g_norm1": D_MODEL, "g_norm2": D_MODEL, "b_gate": 2 * D_MODEL, "g_qa": HEAD_DIM,
            "g_ka": HEAD_DIM, "g_qb": HEAD_DIM, "g_kb": HEAD_DIM, "rpb": NA_HEADS * _RPB_RO * _RPB_CO}


def _pack_small(parts):
    flat = [parts[n].reshape(1, _SMALL_N[n]) for n in _SMALL]
    used = sum(_SMALL_N.values())
    return jnp.concatenate(flat + [jnp.zeros((1, STATS_W - used), F32)], axis=1)


def _unpack_small(v, shapes):
    out, at = {}, 0
    for n in _SMALL:
        out[n] = v[:, at:at + _SMALL_N[n]].reshape(shapes[n])
        at += _SMALL_N[n]
    return out


def _join_cols(t):
    _, r, c = t.shape
    return t.transpose(1, 0, 2).reshape(r, N_CHIP * c)


def kernel(x, c, w_ada, b_ada, g_norm1, g_norm2, w_in, b_gate, g_qa, g_ka, g_qb, g_kb, rpb, w_proj_a, w_proj_b, w_o, w_ffn_in, w_ffn_out, loss_target, m_w_ada, m_b_ada, m_g_norm1, m_g_norm2, m_w_in, m_b_gate, m_g_qa, m_g_ka, m_g_qb, m_g_kb, m_rpb, m_w_proj_a, m_w_proj_b, m_w_o, m_w_ffn_in, m_w_ffn_out, v_w_ada, v_b_ada, v_g_norm1, v_g_norm2, v_w_in, v_b_gate, v_g_qa, v_g_ka, v_g_qb, v_g_kb, v_rpb, v_w_proj_a, v_w_proj_b, v_w_o, v_w_ffn_in, v_w_ffn_out):
    names = ("w_ada", "b_ada", "g_norm1", "g_norm2", "w_in", "b_gate", "g_qa", "g_ka", "g_qb", "g_kb", "rpb",
             "w_proj_a", "w_proj_b", "w_o", "w_ffn_in", "w_ffn_out")
    w = dict(zip(names, (w_ada, b_ada, g_norm1, g_norm2, w_in, b_gate, g_qa, g_ka, g_qb, g_kb, rpb, w_proj_a, w_proj_b,
                         w_o, w_ffn_in, w_ffn_out)))
    m = dict(zip(names, (m_w_ada, m_b_ada, m_g_norm1, m_g_norm2, m_w_in, m_b_gate, m_g_qa, m_g_ka, m_g_qb, m_g_kb, m_rpb,
                         m_w_proj_a, m_w_proj_b, m_w_o, m_w_ffn_in, m_w_ffn_out)))
    v = dict(zip(names, (v_w_ada, v_b_ada, v_g_norm1, v_g_norm2, v_w_in, v_b_gate, v_g_qa, v_g_ka, v_g_qb, v_g_kb, v_rpb,
                         v_w_proj_a, v_w_proj_b, v_w_o, v_w_ffn_in, v_w_ffn_out)))
    d = D_MODEL
    xi, yi, ci = _pos()
    chip = 2 * xi + yi
    me = 2 * chip + ci
    ada_cols = 6 * d // N_CHIP

    c_arr, chip_arr = ci.reshape(1).astype(jnp.int32), chip.reshape(1).astype(jnp.int32)
    first, rest = _BIG[:1], _BIG[1:]

    c_all = _small_allgather(jnp.broadcast_to(c, (8, d)), name="ag_c")[::8]
    b_sh = lax.dynamic_slice(b_ada, (0, chip * ada_cols), (1, ada_cols))
    mod_part = _ada_fwd(c_all, w_ada[0], b_sh, name="ada_fwd")
    mod_all = _small_allgather(mod_part, name="ag_mod").reshape(N_CHIP, 2, 8, ada_cols)[:, 0]
    mod = lax.dynamic_index_in_dim(mod_all, me, axis=1, keepdims=False).reshape(1, 6 * d)

    halves = {n: (2, _BIG_SHARD[n][0] // 2, _BIG_SHARD[n][1]) for n in _BIG}
    shards = {n: w[n][0].astype(BF16).reshape(halves[n]) for n in _BIG}
    land = lambda n: lax.empty((N_CHIP,) + halves[n], BF16)
    ag1 = _split_start([shards[n] for n in first], [land(n) for n in first], _ag_plan(1), 4, mod, name="ag1_start")
    ag2 = _split_start([shards[n] for n in rest], [land(n) for n in rest], _ag_plan(len(rest)), 4 * len(rest),
                       ag1[4], name="ag2_start")
    rpb_after = rpb[0] + ag2[4][0, 0]

    def first_weights(after):
        full1 = _split_wait(ag1[0], ag1[1], ag1[2], ag1[3], _ag_plan(1), after, name="ag1_wait")
        p_in = _ag_pass(full1, name="ag1_pass")[0].reshape((N_CHIP,) + _BIG_SHARD["w_in"])
        cut = W_QKV - 2 * _BIG_SHARD["w_in"][1]
        return dict(w_qkv=jnp.concatenate([p_in[0], p_in[1], p_in[2][:, :cut]], axis=1),
                    w_gates=jnp.concatenate([p_in[2][:, cut:], p_in[3]], axis=1))

    def late_weights(after):
        full2 = _split_wait(ag2[0], ag2[1], ag2[2], ag2[3], _ag_plan(len(rest)), after, name="ag2_wait")
        full2 = _ag_pass(full2, name="ag2_pass")
        full = {n: fu.reshape((N_CHIP,) + _BIG_SHARD[n]) for n, fu in zip(rest, full2)}
        return dict(w_pa=_join_cols(full["w_proj_a"]), w_pb=_join_cols(full["w_proj_b"]), w_o=full["w_o"].reshape(d, d),
                    w_ffn_in=_join_cols(full["w_ffn_in"]), w_ffn_out=full["w_ffn_out"].reshape(D_FF, d))

    def rs_begin(group, grads, tag):
        gps = [grads[n].reshape((N_CHIP,) + halves[n]) for n in group]
        ras = _sibling_send_halves(gps, name=f"rs_sibling_{tag}")
        sums = [_rs_add(gp, ra, c_arr, name=f"rs_add_{n}") for n, gp, ra in zip(group, gps, ras)]
        lands = [lax.empty((3,) + halves[n][1:], BF16) for n in group]
        st = _split_start([sb for _, sb in sums], lands, _rs_plan(len(group)), 3 * len(group), sums[0][0],
                          name=f"rs_{tag}_start")
        return sums, st

    def rs_end(group, begun, after, tag):
        sums, st = begun
        rbs = _split_wait(st[0], st[1], st[2], st[3], _rs_plan(len(group)), after, name=f"rs_{tag}_wait")
        return [_rs_final(sf, rb, chip_arr, name=f"rs_final_{n}") for n, (sf, _), rb in zip(group, sums, rbs)]

    begun = {}

    def early_grads(grads):
        begun["rest"] = rs_begin(rest, grads, "rest")

    loss_v, grad_x, grads, dmod, small = _device_step(
        x[0], loss_target[0], mod, first_weights, late_weights, early_grads, g_norm1, g_norm2, b_gate, g_qa, g_ka, g_qb,
        g_kb, rpb_after)
    begun["first"] = rs_begin(first, grads, "first")

    g, delta, new_m, new_v = {}, {}, {}, {}

    def finish(group, ts, tag):
        others = _sibling_swap(ts, name=f"rs_pair_{tag}")
        for n, t, o in zip(group, ts, others):
            gg, dl, nm, nv = _adamw_halves(w[n][0], t, o, m[n][0], v[n][0], c_arr, name=f"adamw_{n}")
            g[n], delta[n], new_m[n], new_v[n] = gg[None], dl[None], nm[None], nv[None]

    finish(rest, rs_end(rest, begun["rest"], begun["first"][0][0][1], "rest"), "rest")

    stats = _pack_small(dict(b_ada=dmod, **small))
    rows = _small_allgather(jnp.broadcast_to(stats, (8, STATS_W)), name="ag_stats")[::8]
    dmod_sh = lax.dynamic_slice(rows, (0, chip * ada_cols), (8, ada_cols))
    g_ada = _ada_bwd(c_all.T, dmod_sh, name="ada_bwd")
    tot = _row_sum(rows, name="stats_sum")
    g_small = _unpack_small(tot, {n: w[n].shape for n in _SMALL})

    finish(first, rs_end(first, begun["first"], new_v[rest[0]], "first"), "first")

    dl, nm, nv = _adamw(w_ada[0], g_ada, m_w_ada[0], v_w_ada[0], name="adamw_w_ada")
    g["w_ada"], delta["w_ada"], new_m["w_ada"], new_v["w_ada"] = g_ada[None], dl[None], nm[None], nv[None]
    shapes = {n: w[n].shape for n in _SMALL}
    dl, nm, nv = _adamw(_pack_small({n: w[n] for n in _SMALL}), tot, _pack_small({n: m[n] for n in _SMALL}),
                        _pack_small({n: v[n] for n in _SMALL}), name="adamw_small")
    delta.update(_unpack_small(dl, shapes))
    new_m.update(_unpack_small(nm, shapes))
    new_v.update(_unpack_small(nv, shapes))
    g.update(g_small)

    loss = lax.psum(loss_v[0, 0], ("x", "y", "c"))
    return (loss, grad_x[None], *[g[n] for n in names], *[delta[n] for n in names], *[new_m[n] for n in names],
            *[new_v[n] for n in names])
```

```python
import numpy as np

import jax
import jax.numpy as jnp
from jax import lax
from jax.experimental import pallas as pl
from jax.experimental.pallas import tpu as pltpu

F32 = jnp.float32
BF16 = jnp.bfloat16

D_MODEL = 1024
SEQ = 8192
HEAD_DIM = 64
GRID_W = 64
ROWS = SEQ // GRID_W
NA_HEADS = 8
NA_KH = 8
NA_KW = 16
DIL_CONFIGS = ((128, 1), (512, 4), (2048, 16))
ROT_DIM = 16
ROPE_THETA = 500000.0
D_FF = 2816
EPS = 1e-6
NEG = -1e30
WA = 512
WB = 768
WB_OUT = 256
W_QKV = 3 * WA + 3 * WB
W_QK = 2 * WA + 2 * WB
W_GATES = 2 * D_MODEL
SCALE = HEAD_DIM ** -0.5

ADAM_LR = 0.001
ADAM_B1 = 0.9
ADAM_B2 = 0.999
ADAM_EPS = 1e-08
ADAM_WD = 0.01
ADAM_STEP = 10

LANES = 128
ROW_TILE = 256
ROW_TILES = {"ffn_fwd": 512, "post_attn_fwd": 512, "post_attn_bwd": 512,
             "ffn_in_bwd": 512}
Q_BLOCK = 256
NA_QROWS = Q_BLOCK // GRID_W
NA_KROWS = NA_QROWS + NA_KH - 1
NA_NK = NA_KROWS * GRID_W
NA_PAIRS = (NA_KROWS + 1) // 2
NA_W = NA_PAIRS * LANES
NA_RO_NONE = 15
NA_SLOTS = 21
RP_LANE0 = GRID_W - NA_KW
DIL_HALF = 64
DIL_NK = Q_BLOCK + 2 * DIL_HALF
N_QBLK = SEQ // Q_BLOCK

N_DEV = 8
N_CHIP = 4
FF_CHIP = 2 * D_FF // N_CHIP
STATS_W = 14336


def _pcall(body, *, name, **kw):
    return pl.pallas_call(body, name=name, **kw)


_NT = (((1,), (1,)), ((), ()))
_TN = (((0,), (0,)), ((), ()))
_ARB = pltpu.CompilerParams(dimension_semantics=("arbitrary",))
_PAR = pltpu.CompilerParams(dimension_semantics=("parallel",))


def _dot(a, b):
    return jnp.dot(a, b, preferred_element_type=F32)


def _dot_nt(a, b):
    return lax.dot_general(a, b, _NT, preferred_element_type=F32)


def _wgrad(a, b, *, name, tm, tn, tk=1024, chips=None):
    s, ma = a.shape
    nb = b.shape[1]
    nk = s // tk
    nc = nb // chips if chips else tn
    cpb = tn // nc

    def body(a_ref, b_ref, o_ref, acc):
        k = pl.program_id(2)
        r = lax.dot_general(a_ref[...].astype(BF16), b_ref[...].astype(BF16), _TN, preferred_element_type=F32)

        @pl.when(k == 0)
        def _():
            acc[...] = r

        @pl.when(k > 0)
        def _():
            acc[...] += r

        @pl.when(k == nk - 1)
        def _():
            if chips:
                for q in range(cpb):
                    o_ref[q] = acc[:, q * nc:(q + 1) * nc]
            else:
                o_ref[...] = acc[...]

    if chips:
        o_spec = pl.BlockSpec((cpb, tm, nc), lambda i, j, k: (j, i, 0))
        out_shape = jax.ShapeDtypeStruct((chips, ma, nc), F32)
    else:
        o_spec = pl.BlockSpec((tm, tn), lambda i, j, k: (i, j))
        out_shape = jax.ShapeDtypeStruct((ma, nb), F32)
    return _pcall(
        body, name=name, grid=(ma // tm, nb // tn, nk),
        in_specs=[pl.BlockSpec((tk, tm), lambda i, j, k: (k, i)), pl.BlockSpec((tk, tn), lambda i, j, k: (k, j))],
        out_specs=o_spec, out_shape=out_shape, scratch_shapes=[pltpu.VMEM((tm, tn), F32)],
        compiler_params=pltpu.CompilerParams(dimension_semantics=("parallel", "parallel", "arbitrary")),
    )(a, b)


def _row_call(body, *, name, row_ins, res_ins, row_outs, acc_outs=(), scratch=()):
    row_ins = [a if isinstance(a, tuple) else (a, 1) for a in row_ins]
    row_outs = [o if len(o) == 3 else (*o, 1) for o in row_outs]
    s = row_ins[0][0].shape[0]
    tile = ROW_TILES.get(name, ROW_TILE)
    n = s // tile
    nri, nre, nro, nao = len(row_ins), len(res_ins), len(row_outs), len(acc_outs)

    def whole(shape):
        nd = len(shape)
        return pl.BlockSpec(tuple(shape), lambda i: (0,) * nd, pipeline_mode=pl.Buffered(1))

    def whole_out(shape):
        nd = len(shape)
        return pl.BlockSpec(tuple(shape), lambda i: (0,) * nd)

    def rows(w, d):
        if d == 1:
            return pl.BlockSpec((tile, w), lambda i: (i, 0))
        return pl.BlockSpec((d, tile // d, w), lambda i: (0, i, 0))

    in_specs = [rows(a.shape[1], d) for a, d in row_ins]
    in_specs += [whole(a.shape) for a in res_ins]
    out_specs = [rows(w, d) for w, _, d in row_outs]
    out_specs += [whole_out(shp) for shp, _ in acc_outs]
    out_shape = [jax.ShapeDtypeStruct((s, w) if d == 1 else (d, s // d, w), dt) for w, dt, d in row_outs]
    out_shape += [jax.ShapeDtypeStruct(tuple(shp), dt) for shp, dt in acc_outs]

    def wrapped(*refs):
        at = [0, nri, nri + nre, nri + nre + nro, nri + nre + nro + nao]
        body(pl.program_id(0), n, refs[at[0]:at[1]], refs[at[1]:at[2]], refs[at[2]:at[3]], refs[at[3]:at[4]],
             refs[at[4]:])

    args = [a if d == 1 else a.reshape(d, s // d, a.shape[1]) for a, d in row_ins]
    outs = _pcall(wrapped, name=name, grid=(n,), in_specs=in_specs, out_specs=out_specs, out_shape=out_shape,
                  scratch_shapes=list(scratch), compiler_params=_ARB)(*args, *res_ins)
    return [o.reshape(s, o.shape[-1]) if k < nro and row_outs[k][2] != 1 else o for k, o in enumerate(outs)]


def _stage_shape(name):
    return pltpu.VMEM((4, ROW_TILES.get(name, ROW_TILE), LANES), F32)


def _from_residue(ref, col, stage, slot):
    d, n = ref.shape[0], ref.shape[1]
    for r in range(d):
        stage.at[slot][pl.ds(r, n, stride=d), :] = ref[r, :, col:col + LANES].astype(F32)
    return stage[slot]


def _natural(ref, stage, slot0):
    if len(ref.shape) == 2:
        return ref[...]
    return jnp.concatenate([_from_residue(ref, c * LANES, stage, (slot0 + c) % 4)
                            for c in range(ref.shape[2] // LANES)], axis=1)


def _to_residue(val, ref, col, stage, slot):
    d, n = ref.shape[0], ref.shape[1]
    stage[slot] = val
    for r in range(d):
        ref[r, :, col:col + LANES] = stage.at[slot][pl.ds(r, n, stride=d), :].astype(ref.dtype)


def _fold8(t):
    r, w = t.shape
    return jnp.sum(t.reshape(r // 8, 8, w), axis=0)


def _sigmoid(t):
    return 1.0 / (1.0 + jnp.exp(-t))


def _head_lanes():
    return lax.broadcasted_iota(jnp.int32, (1, LANES), 1) < HEAD_DIM


def _head_mean(t, lo):
    s_lo = jnp.sum(jnp.where(lo, t, 0.0), axis=1, keepdims=True)
    s_hi = jnp.sum(jnp.where(lo, 0.0, t), axis=1, keepdims=True)
    return jnp.where(lo, s_lo, s_hi) * (1.0 / HEAD_DIM)


def _rms_mod(xv, g, sc, sh):
    rstd = lax.rsqrt(jnp.mean(xv * xv, axis=1, keepdims=True) + EPS)
    return (xv * rstd * g) * (1.0 + sc) + sh


def _rms_mod_bwd(xv, dh, g, sc):
    rstd = lax.rsqrt(jnp.mean(xv * xv, axis=1, keepdims=True) + EPS)
    xhat = xv * rstd
    dn = dh * (1.0 + sc)
    dxhat = dn * g
    dx = rstd * (dxhat - xhat * jnp.mean(dxhat * xhat, axis=1, keepdims=True))
    return dx, dh, dh * (xhat * g), dn * xhat


def _mix_weights(ls):
    m = jnp.maximum(jnp.maximum(ls[0], ls[1]), ls[2])
    es = [jnp.exp(t - m) for t in ls]
    den = es[0] + es[1] + es[2]
    return [e / den for e in es]


def _rope_tables():
    half = ROT_DIM // 2
    inv_freq = ROPE_THETA ** (-(jnp.arange(half, dtype=F32) * 2.0) / ROT_DIM)
    lane = np.arange(LANES) % HEAD_DIM
    ang = jnp.arange(SEQ).astype(F32)[:, None] * jnp.tile(inv_freq, LANES // half)[None, :]
    cos, sin = jnp.cos(ang), jnp.sin(ang)
    first, second = jnp.asarray(lane < half)[None, :], jnp.asarray((lane >= half) & (lane < ROT_DIM))[None, :]
    cos_t = jnp.where(first | second, cos, 1.0)
    return cos_t, jnp.where(second, sin, 0.0), jnp.where(first, -sin, 0.0)


_SECTIONS = ((0, WA, 0, False), (WA, 2 * WA, 1, False), (2 * WA, 3 * WA, -1, False),
             (3 * WA, 3 * WA + WB, 2, True), (3 * WA + WB, 3 * WA + 2 * WB, 3, True), (3 * WA + 2 * WB, W_QKV, -1, False))


def _pre_attn_fwd(x, cos_t, sa_t, sb_t, g1, sc1, sh1, w_qkv, w_gates, gains, *, name):
    half = ROT_DIM // 2
    dilated = [(g, dd) for g, (_, dd) in enumerate(DIL_CONFIGS) if dd > 1]

    def body(i, n, rin, res, rout, aout, scr):
        x_ref, cos_ref, sa_ref, sb_ref = rin
        g_ref, sc_ref, sh_ref, wq_ref, wg_ref, gains_ref = res
        h1_ref, qkvn_ref, pre_ref, gates_ref = rout[:4]
        group_ref = {g: rout[4 + k] for k, (g, _) in enumerate(dilated)}
        (stage,) = scr
        staged = 0
        hb = _rms_mod(x_ref[...], g_ref[...], sc_ref[...], sh_ref[...]).astype(BF16)
        h1_ref[...] = hb
        gates_ref[...] = _dot(hb, wg_ref[...]).astype(BF16)
        lo = _head_lanes()
        cosv, sav, sbv = cos_ref[...], sa_ref[...], sb_ref[...]
        pre_at = 0
        for si, (c0, c1, kind, rot) in enumerate(_SECTIONS):
            sec = _dot(hb, wq_ref[:, c0:c1])
            for ch in range((c1 - c0) // LANES):
                t = sec[:, ch * LANES:(ch + 1) * LANES]
                if kind >= 0:
                    pre_ref[:, pre_at:pre_at + LANES] = t.astype(BF16)
                    pre_at += LANES
                    t = t * lax.rsqrt(_head_mean(t * t, lo) + EPS) * gains_ref[kind:kind + 1, :]
                    if rot:
                        t = t * cosv + pltpu.roll(t, half, 1) * sav + pltpu.roll(t, LANES - half, 1) * sbv
                qkvn_ref[:, c0 + ch * LANES:c0 + (ch + 1) * LANES] = t.astype(BF16)
                group = ch * LANES // WB_OUT if si >= 3 else 0
                if group in group_ref:
                    col = (si - 3) * WB_OUT + ch * LANES % WB_OUT
                    _to_residue(t, group_ref[group], col, stage, staged % 4)
                    staged += 1

    return _row_call(body, name=name, row_ins=[x, cos_t, sa_t, sb_t], res_ins=[g1, sc1, sh1, w_qkv, w_gates, gains],
                     row_outs=[(D_MODEL, BF16), (W_QKV, BF16), (W_QK, BF16), (W_GATES, BF16)]
                     + [(3 * WB_OUT, BF16, dd) for _, dd in dilated], scratch=[_stage_shape(name)])


def _pre_attn_bwd(qk_pre, d_parts, dgates, x, dx1, cos_t, sa_t, sb_t, w_qkv, w_gates, gains, g1, sc1, *, name):
    half = ROT_DIM // 2
    nparts = len(d_parts)
    where = []
    residue = [isinstance(part, tuple) for part in d_parts]
    for pi, part in enumerate(d_parts):
        width = (part[0] if residue[pi] else part).shape[1]
        where += [(pi, cj) for cj in range(width // LANES)]
    assert len(where) == W_QKV // LANES

    def body(i, n, rin, res, rout, aout, scr):
        pre_ref, d_refs = rin[0], rin[1:1 + nparts]
        dgates_ref, x_ref, dx1_ref, cos_ref, sa_ref, sb_ref = rin[1 + nparts:]
        wq_ref, wg_ref, gains_ref, g_ref, sc_ref = res
        dqkv_ref, gx_ref = rout
        dgains_ref, sums_ref = aout
        accg, accs, stage = scr
        staged = 0

        @pl.when(i == 0)
        def _():
            accg[...] = jnp.zeros_like(accg)
            accs[...] = jnp.zeros_like(accs)

        lo = _head_lanes()
        cosv, sav, sbv = cos_ref[...], sa_ref[...], sb_ref[...]
        dh = _dot_nt(dgates_ref[...], wg_ref[...])
        pre_at = 0
        for c0, c1, kind, rot in _SECTIONS:
            for ch in range((c1 - c0) // LANES):
                pi, cj = where[c0 // LANES + ch]
                if residue[pi]:
                    dt = _from_residue(d_refs[pi], cj * LANES, stage, staged % 4)
                    staged += 1
                else:
                    dt = d_refs[pi][:, cj * LANES:(cj + 1) * LANES]
                if kind >= 0:
                    if rot:
                        dt = dt * cosv + pltpu.roll(dt * sav, LANES - half, 1) + pltpu.roll(dt * sbv, half, 1)
                    t = pre_ref[:, pre_at:pre_at + LANES].astype(F32)
                    pre_at += LANES
                    rstd = lax.rsqrt(_head_mean(t * t, lo) + EPS)
                    xhat = t * rstd
                    accg[kind] += _fold8(dt * xhat)
                    dxhat = dt * gains_ref[kind:kind + 1, :]
                    dt = rstd * (dxhat - xhat * _head_mean(dxhat * xhat, lo))
                dqkv_ref[:, c0 + ch * LANES:c0 + (ch + 1) * LANES] = dt.astype(BF16)
            dh = dh + _dot_nt(dqkv_ref[:, c0:c1], wq_ref[:, c0:c1])
        dx, t_sh, t_sc, t_g = _rms_mod_bwd(x_ref[...], dh, g_ref[...], sc_ref[...])
        gx_ref[...] = dx1_ref[...] + dx
        accs[0] += _fold8(t_sh)
        accs[1] += _fold8(t_sc)
        accs[2] += _fold8(t_g)

        @pl.when(i == n - 1)
        def _():
            t = jnp.sum(accg[...], axis=1)
            dgains_ref[...] = t + pltpu.roll(t, HEAD_DIM, 1)
            sums_ref[...] = jnp.sum(accs[...], axis=1)

    return _row_call(
        body, name=name, row_ins=[qk_pre, *d_parts, dgates, x, dx1, cos_t, sa_t, sb_t],
        res_ins=[w_qkv, w_gates, gains, g1, sc1], row_outs=[(W_QKV, BF16), (D_MODEL, F32)],
        acc_outs=[((4, LANES), F32), ((3, D_MODEL), F32)],
        scratch=[pltpu.VMEM((4, 8, LANES), F32), pltpu.VMEM((3, 8, D_MODEL), F32), _stage_shape(name)])


def _post_attn_fwd(o_a, o_g, l_g, gates, x, w_pa, w_pb, w_o, b_gate, gt1, g2, sc2, sh2, *, name):
    d = D_MODEL

    def body(i, n, rin, res, rout, aout, scr):
        oa_ref, o0, o1, o2, l0, l1, l2, gates_ref, x_ref = rin
        wpa_ref, wpb_ref, wo_ref, b_ref, gt_ref, g_ref, sc_ref, sh_ref = res
        ob_ref, merged_ref, mo_ref, x1_ref, h2_ref = rout
        (stage,) = scr
        ogs = [_natural(r, stage, 0) for r in (o0, o1, o2)]
        ws = _mix_weights([_natural(r, stage, 2) for r in (l0, l1, l2)])
        obb = (ws[0] * ogs[0] + ws[1] * ogs[1] + ws[2] * ogs[2]).astype(BF16)
        ob_ref[...] = obb
        pa = _dot(oa_ref[...].astype(BF16), wpa_ref[...])
        pb = _dot(obb, wpb_ref[...])
        ga = _sigmoid(gates_ref[:, :d].astype(F32) + b_ref[:, :d])
        gb = _sigmoid(gates_ref[:, d:].astype(F32) + b_ref[:, d:])
        merged = (ga * pa + gb * pb).astype(BF16)
        merged_ref[...] = merged
        mo = _dot(merged, wo_ref[...])
        mo_ref[...] = mo.astype(BF16)
        x1 = x_ref[...] + gt_ref[...] * mo
        x1_ref[...] = x1
        h2_ref[...] = _rms_mod(x1, g_ref[...], sc_ref[...], sh_ref[...]).astype(BF16)

    return _row_call(body, name=name, row_ins=[o_a, *o_g, *l_g, gates, x],
                     res_ins=[w_pa, w_pb, w_o, b_gate, gt1, g2, sc2, sh2],
                     row_outs=[(WB_OUT, BF16), (d, BF16), (d, BF16), (d, F32), (d, BF16)], scratch=[_stage_shape(name)])


def _ffn_fwd(h2, w_ffn_in, *, name):
    def body(i, n, rin, res, rout, aout, scr):
        (h_ref,), (w_ref,), (act_ref, ff_ref) = rin, res, rout
        hv = h_ref[...]
        for q in range(2):
            a = _dot(hv, w_ref[:, q * FF_CHIP:(q + 1) * FF_CHIP])
            up = _dot(hv, w_ref[:, D_FF + q * FF_CHIP:D_FF + (q + 1) * FF_CHIP])
            sl = slice(q * FF_CHIP, (q + 1) * FF_CHIP)
            act_ref[:, sl] = (a * _sigmoid(a) * up).astype(BF16)
            ff_ref[:, sl] = a.astype(BF16)
            ff_ref[:, D_FF + q * FF_CHIP:D_FF + (q + 1) * FF_CHIP] = up.astype(BF16)

    return _row_call(body, name=name, row_ins=[h2], res_ins=[w_ffn_in], row_outs=[(D_FF, BF16), (2 * D_FF, BF16)])


def _ffn_mid(act, ff, x1, tgt, w_ffn_out, gt2, *, name):
    d = D_MODEL

    def body(i, n, rin, res, rout, aout, scr):
        act_ref, ff_ref, x1_ref, tgt_ref = rin
        wo_ref, gt_ref = res
        dy_ref, dffo_ref, dff_ref = rout
        dgt_ref, loss_ref = aout
        (acc,) = scr

        @pl.when(i == 0)
        def _():
            acc[...] = jnp.zeros_like(acc)

        ffo = _dot(act_ref[...], wo_ref[...])
        gtv = gt_ref[...]
        e = x1_ref[...] + gtv * ffo - tgt_ref[...]
        dy = e * (1.0 / d)
        dy_ref[...] = dy
        dffo = (gtv * dy).astype(BF16)
        dffo_ref[...] = dffo
        acc[0] += _fold8(dy * ffo)
        acc[1] += _fold8(e * e)
        for q in range(2):
            sl = slice(q * FF_CHIP, (q + 1) * FF_CHIP)
            su = slice(D_FF + q * FF_CHIP, D_FF + (q + 1) * FF_CHIP)
            dact = _dot_nt(dffo, wo_ref[sl, :])
            a = ff_ref[:, sl].astype(F32)
            up = ff_ref[:, su].astype(F32)
            sg = _sigmoid(a)
            dff_ref[:, sl] = (dact * up * (sg * (1.0 + a * (1.0 - sg)))).astype(BF16)
            dff_ref[:, su] = (dact * (a * sg)).astype(BF16)

        @pl.when(i == n - 1)
        def _():
            dgt_ref[...] = jnp.sum(acc[0], axis=0, keepdims=True)
            tot = jnp.sum(jnp.sum(acc[1], axis=0, keepdims=True), axis=1, keepdims=True)
            loss_ref[...] = jnp.broadcast_to(tot * (0.5 / d), (1, LANES))

    return _row_call(body, name=name, row_ins=[act, ff, x1, tgt], res_ins=[w_ffn_out, gt2],
                     row_outs=[(d, F32), (d, BF16), (2 * D_FF, BF16)], acc_outs=[((1, d), F32), ((1, LANES), F32)],
                     scratch=[pltpu.VMEM((2, 8, d), F32)])


def _ffn_in_bwd(dff, x1, dy, mo, w_ffn_in, g2, sc2, gt1, *, name):
    d = D_MODEL

    def body(i, n, rin, res, rout, aout, scr):
        dff_ref, x1_ref, dy_ref, mo_ref = rin
        w_ref, g_ref, sc_ref, gt_ref = res
        dx1_ref, dmo_ref = rout
        (sums_ref,) = aout
        (acc,) = scr

        @pl.when(i == 0)
        def _():
            acc[...] = jnp.zeros_like(acc)

        dh = _dot_nt(dff_ref[...], w_ref[...])
        dx, t_sh, t_sc, t_g = _rms_mod_bwd(x1_ref[...], dh, g_ref[...], sc_ref[...])
        dx1 = dy_ref[...] + dx
        dx1_ref[...] = dx1
        dmo_ref[...] = (gt_ref[...] * dx1).astype(BF16)
        acc[0] += _fold8(t_sh)
        acc[1] += _fold8(t_sc)
        acc[2] += _fold8(t_g)
        acc[3] += _fold8(dx1 * mo_ref[...].astype(F32))

        @pl.when(i == n - 1)
        def _():
            sums_ref[...] = jnp.sum(acc[...], axis=1)

    return _row_call(body, name=name, row_ins=[dff, x1, dy, mo], res_ins=[w_ffn_in, g2, sc2, gt1],
                     row_outs=[(d, F32), (d, BF16)], acc_outs=[((4, d), F32)], scratch=[pltpu.VMEM((4, 8, d), F32)])


def _post_attn_bwd(dmo, gates, o_a, o_g, l_g, w_pa, w_pb, w_o, b_gate, *, name):
    d = D_MODEL

    def body(i, n, rin, res, rout, aout, scr):
        dmo_ref, gates_ref, oa_ref, o0, o1, o2, l0, l1, l2 = rin
        wpa_ref, wpb_ref, wo_ref, b_ref = res
        dpa_ref, dpb_ref, dgates_ref, doa_ref = rout[:4]
        do_refs, dl_refs = rout[4:7], rout[7:10]
        (dbg_ref,) = aout
        acc, stage = scr

        @pl.when(i == 0)
        def _():
            acc[...] = jnp.zeros_like(acc)

        ogs = [_natural(r, stage, 0) for r in (o0, o1, o2)]
        ws = _mix_weights([_natural(r, stage, 2) for r in (l0, l1, l2)])
        obb = (ws[0] * ogs[0] + ws[1] * ogs[1] + ws[2] * ogs[2]).astype(BF16)
        pa = _dot(oa_ref[...].astype(BF16), wpa_ref[...])
        pb = _dot(obb, wpb_ref[...])
        ga = _sigmoid(gates_ref[:, :d].astype(F32) + b_ref[:, :d])
        gb = _sigmoid(gates_ref[:, d:].astype(F32) + b_ref[:, d:])
        dm = _dot_nt(dmo_ref[...], wo_ref[...])
        dpa = (dm * ga).astype(BF16)
        dpb = (dm * gb).astype(BF16)
        dpa_ref[...] = dpa
        dpb_ref[...] = dpb
        dga = dm * pa * ga * (1.0 - ga)
        dgb = dm * pb * gb * (1.0 - gb)
        dgates_ref[:, :d] = dga.astype(BF16)
        dgates_ref[:, d:] = dgb.astype(BF16)
        acc[:, :d] += _fold8(dga)
        acc[:, d:] += _fold8(dgb)
        doa_ref[...] = _dot_nt(dpa, wpa_ref[...])
        dob = _dot_nt(dpb, wpb_ref[...])
        lo = _head_lanes()
        for ch in range(WB_OUT // LANES):
            sl = slice(ch * LANES, (ch + 1) * LANES)
            dv = dob[:, sl]
            wc = [w[:, sl] for w in ws]
            ts = [_head_mean(dv * og[:, sl], lo) * float(HEAD_DIM) for og in ogs]
            tbar = wc[0] * ts[0] + wc[1] * ts[1] + wc[2] * ts[2]
            for g in range(3):
                for k, (ref, val) in enumerate(((do_refs[g], wc[g] * dv), (dl_refs[g], wc[g] * (ts[g] - tbar)))):
                    if len(ref.shape) == 2:
                        ref[:, sl] = val
                    else:
                        _to_residue(val, ref, ch * LANES, stage, (2 * g + k) % 4)

        @pl.when(i == n - 1)
        def _():
            dbg_ref[...] = jnp.sum(acc[...], axis=0, keepdims=True)

    return _row_call(body, name=name, row_ins=[dmo, gates, o_a, *o_g, *l_g], res_ins=[w_pa, w_pb, w_o, b_gate],
                     row_outs=[(d, BF16), (d, BF16), (2 * d, BF16), (WA, F32)]
                     + 2 * [(WB_OUT, F32, dd) for _, dd in DIL_CONFIGS],
                     acc_outs=[((1, 2 * d), F32)], scratch=[pltpu.VMEM((8, 2 * d), F32), _stage_shape(name)])


def _na_class_tables():
    ro = np.full((3, NA_QROWS, 2 * NA_PAIRS), NA_RO_NONE, np.int64)
    slot = np.zeros((3, NA_QROWS, NA_PAIRS), np.int64)
    for t in range(3):
        for a in range(NA_QROWS):
            qr = _NA_CLASS_R0[t] + a
            rs = min(max(qr - NA_KH // 2, 0), ROWS - NA_KH)
            for b in range(NA_KROWS):
                kr = _NA_CLASS_K0[t] + b
                if rs <= kr < rs + NA_KH:
                    ro[t, a, b] = kr - qr + (NA_KH - 1)
            for j in range(NA_PAIRS):
                slot[t, a, j] = 2 * j - a + (_NA_CLASS_K0[t] - _NA_CLASS_R0[t] + NA_KH - 1) + (NA_QROWS - 1)
    assert slot.min() >= 0 and slot.max() < NA_SLOTS
    return ro, slot


def _na_build_bias(i, cls_ref, rp_ref, cm_ref, bias_scr):
    ro, _ = _na_class_tables()
    lo = _head_lanes()
    first = jnp.logical_or(i == 0, cls_ref[i] != cls_ref[jnp.maximum(i - 1, 0)])
    for t in range(3):
        @pl.when(jnp.logical_and(first, cls_ref[i] == t))
        def _():
            for hh in range(2):
                for a in range(NA_QROWS):
                    for j in range(NA_PAIRS):
                        r0, r1 = int(ro[t, a, 2 * j]), int(ro[t, a, 2 * j + 1])
                        x0 = jnp.broadcast_to(rp_ref[hh, r0:r0 + 1, :], (GRID_W, LANES))
                        x1 = jnp.broadcast_to(rp_ref[hh, r1:r1 + 1, :], (GRID_W, LANES))
                        blk = jnp.where(lo, pltpu.roll(x0, GRID_W + 1, 1, stride=1, stride_axis=0),
                                        pltpu.roll(x1, 1, 1, stride=1, stride_axis=0))
                        bias_scr[hh, a * GRID_W:(a + 1) * GRID_W, j * LANES:(j + 1) * LANES] = blk + cm_ref[...]
    return first


def _attn_fwd(qkv, qc0, kc0, vc0, npairs, table, kstart, cls, nk, *, name, na=None):
    s = qkv.shape[0]

    def body(ks_ref, cls_ref, q_ref, k_ref, v_ref, b_ref, *rest):
        if na:
            cm_ref, o_ref, lse_ref, bias_scr = rest
        else:
            o_ref, lse_ref = rest
        i = pl.program_id(1)
        if na:
            _na_build_bias(i, cls_ref, b_ref, cm_ref, bias_scr)
        ks = pl.multiple_of(ks_ref[i], 64)
        q2 = q_ref[...]
        k2 = k_ref[pl.ds(ks, nk), :]
        v2 = v_ref[pl.ds(ks, nk), :]
        lo = _head_lanes()
        outs, lses = [], []
        for h in range(2):
            qm = jnp.where(lo if h == 0 else jnp.logical_not(lo), q2, jnp.zeros_like(q2))
            sc = _dot_nt(qm, k2) * SCALE + (bias_scr[h, :, :nk] if na else b_ref[0, 0])
            m = jnp.max(sc, axis=1, keepdims=True)
            p = jnp.exp(sc - m)
            l = jnp.sum(p, axis=1, keepdims=True)
            pv = _dot(p.astype(BF16), v2)
            outs.append(pv / l)
            lses.append(m + jnp.log(l))
        o_ref[...] = jnp.where(lo, outs[0], outs[1])
        lse_ref[...] = jnp.where(lo, lses[0], lses[1])

    w = npairs * LANES
    in_specs = [
        pl.BlockSpec((Q_BLOCK, LANES), lambda p, i, ks, cl: (i, qc0 + p)),
        pl.BlockSpec((s, LANES), lambda p, i, ks, cl: (0, kc0 + p)),
        pl.BlockSpec((s, LANES), lambda p, i, ks, cl: (0, vc0 + p)),
    ]
    if na:
        in_specs += _na_bias_specs()
        args, scratch = (kstart, cls, qkv, qkv, qkv, *na), [pltpu.VMEM((2, Q_BLOCK, NA_W), F32)]
    else:
        in_specs.append(pl.BlockSpec((1, 1, Q_BLOCK, nk), lambda p, i, ks, cl: (cl[i], 0, 0, 0)))
        args, scratch = (kstart, cls, qkv, qkv, qkv, table), []
    grid_spec = pltpu.PrefetchScalarGridSpec(
        num_scalar_prefetch=2, grid=(npairs, N_QBLK), in_specs=in_specs,
        out_specs=[pl.BlockSpec((Q_BLOCK, LANES), lambda p, i, ks, cl: (i, p)),
                   pl.BlockSpec((Q_BLOCK, LANES), lambda p, i, ks, cl: (i, p))],
        scratch_shapes=scratch,
    )
    return _pcall(body, name=name, grid_spec=grid_spec,
                  out_shape=[jax.ShapeDtypeStruct((s, w), F32), jax.ShapeDtypeStruct((s, w), F32)],
                  compiler_params=pltpu.CompilerParams(dimension_semantics=("parallel", "arbitrary")),
                  )(*args)


def _na_bias_specs():
    return [pl.BlockSpec((2, 16, LANES), lambda p, i, ks, cl: (p, 0, 0)),
            pl.BlockSpec((GRID_W, LANES), lambda p, i, ks, cl: (0, 0))]


def _attn_bwd(qkv, qc0, kc0, vc0, npairs, table, kstart, cls, nk, do, o, lse, *, name, dlse=None, na=None):
    s = qkv.shape[0]
    has_dlse = dlse is not None
    _, slot = _na_class_tables()

    def body(ks_ref, cls_ref, q_ref, k_ref, v_ref, b_ref, *rest):
        if na:
            cm_ref, rest = rest[0], rest[1:]
        do_ref, o_ref, lse_ref, rest = rest[0], rest[1], rest[2], rest[3:]
        if has_dlse:
            dlse_ref, rest = rest[0], rest[1:]
        dq_ref, dk_ref, dv_ref = rest[0], rest[1], rest[2]
        if na:
            bank_ref, bias_scr, dbias_scr, bank_scr = rest[3:]
        i = pl.program_id(1)
        if na:
            first = _na_build_bias(i, cls_ref, b_ref, cm_ref, bias_scr)

        @pl.when(i == 0)
        def _():
            dk_ref[...] = jnp.zeros_like(dk_ref)
            dv_ref[...] = jnp.zeros_like(dv_ref)
            if na:
                dbias_scr[...] = jnp.zeros_like(dbias_scr)
                bank_scr[...] = jnp.zeros_like(bank_scr)

        ks = pl.multiple_of(ks_ref[i], 64)
        q2 = q_ref[...]
        k2 = k_ref[pl.ds(ks, nk), :]
        v2 = v_ref[pl.ds(ks, nk), :]
        do2 = do_ref[...]
        lse2 = lse_ref[...]
        doo = do2 * o_ref[...]
        do2b = do2.astype(BF16)
        lo = _head_lanes()
        lane = lax.broadcasted_iota(jnp.int32, (1, LANES), 1)
        dqs, dks, dvs = [], [], []
        for h in range(2):
            mh = lo if h == 0 else jnp.logical_not(lo)
            qm = jnp.where(mh, q2, jnp.zeros_like(q2))
            sc = _dot_nt(qm, k2) * SCALE + (bias_scr[h, :, :nk] if na else b_ref[0, 0])
            lse_h = jnp.max(jnp.where(mh, lse2, NEG), axis=1, keepdims=True)
            p = jnp.exp(sc - lse_h)
            delta = jnp.sum(jnp.where(mh, doo, 0.0), axis=1, keepdims=True)
            dom = jnp.where(mh, do2b, jnp.zeros_like(do2b))
            dp = _dot_nt(dom, v2)
            t = dp - delta
            if has_dlse:
                t = t + jnp.sum(jnp.where(lane == h * HEAD_DIM, dlse_ref[...], 0.0), axis=1, keepdims=True)
            ds = p * t
            if na:
                @pl.when(first)
                def _():
                    dbias_scr[h, :, :nk] = ds

                @pl.when(jnp.logical_not(first))
                def _():
                    dbias_scr[h, :, :nk] += ds
            dsb = ds.astype(BF16)
            dqs.append(_dot(dsb, k2))
            dks.append(lax.dot_general(dsb, q2, _TN, preferred_element_type=F32))
            dvs.append(lax.dot_general(p.astype(BF16), do2b, _TN, preferred_element_type=F32))
        dq_ref[...] = jnp.where(lo, dqs[0], dqs[1]) * SCALE
        dk_ref[pl.ds(ks, nk), :] += jnp.where(lo, dks[0], dks[1]) * SCALE
        dv_ref[pl.ds(ks, nk), :] += jnp.where(lo, dvs[0], dvs[1])
        if na:
            last = jnp.logical_or(i == N_QBLK - 1, cls_ref[i] != cls_ref[jnp.minimum(i + 1, N_QBLK - 1)])
            for t in range(3):
                @pl.when(jnp.logical_and(last, cls_ref[i] == t))
                def _():
                    for hh in range(2):
                        for a in range(NA_QROWS):
                            for j in range(NA_PAIRS):
                                bank_scr[hh, int(slot[t, a, j])] += dbias_scr[
                                    hh, a * GRID_W:(a + 1) * GRID_W, j * LANES:(j + 1) * LANES]

            @pl.when(i == N_QBLK - 1)
            def _():
                bank_ref[...] = bank_scr[...]

    w = npairs * LANES
    blk = lambda: pl.BlockSpec((Q_BLOCK, LANES), lambda p, i, ks, cl: (i, p))
    full = lambda: pl.BlockSpec((s, LANES), lambda p, i, ks, cl: (0, p))
    in_specs = [
        pl.BlockSpec((Q_BLOCK, LANES), lambda p, i, ks, cl: (i, qc0 + p)),
        pl.BlockSpec((s, LANES), lambda p, i, ks, cl: (0, kc0 + p)),
        pl.BlockSpec((s, LANES), lambda p, i, ks, cl: (0, vc0 + p)),
    ]
    if na:
        in_specs += _na_bias_specs()
        args = [kstart, cls, qkv, qkv, qkv, *na]
    else:
        in_specs.append(pl.BlockSpec((1, 1, Q_BLOCK, nk), lambda p, i, ks, cl: (cl[i], 0, 0, 0)))
        args = [kstart, cls, qkv, qkv, qkv, table]
    in_specs += [blk(), blk(), blk()]
    args += [do, o, lse]
    if has_dlse:
        in_specs.append(blk())
        args.append(dlse)
    out_specs = [blk(), full(), full()]
    out_shape = [jax.ShapeDtypeStruct((s, w), F32)] * 3
    scratch = []
    if na:
        bank_shape = (2, NA_SLOTS, GRID_W, LANES)
        out_specs.append(pl.BlockSpec(bank_shape, lambda p, i, ks, cl: (p, 0, 0, 0)))
        out_shape.append(jax.ShapeDtypeStruct((2 * npairs,) + bank_shape[1:], F32))
        scratch = [pltpu.VMEM((2, Q_BLOCK, NA_W), F32), pltpu.VMEM((2, Q_BLOCK, NA_W), F32), pltpu.VMEM(bank_shape, F32)]
    grid_spec = pltpu.PrefetchScalarGridSpec(num_scalar_prefetch=2, grid=(npairs, N_QBLK), in_specs=in_specs,
                                             out_specs=out_specs, scratch_shapes=scratch)
    return _pcall(body, name=name, grid_spec=grid_spec, out_shape=out_shape,
                  compiler_params=pltpu.CompilerParams(dimension_semantics=("arbitrary", "arbitrary")))(*args)


_NA_CLASS_R0 = (0, NA_QROWS, ROWS - NA_QROWS)
_NA_CLASS_K0 = (0, 0, ROWS - NA_KROWS)
_RPB_RO = 2 * NA_KH - 1
_RPB_CO = 2 * NA_KW - 1
_BANK_ROWS = 48


def _na_constants():
    col = np.arange(GRID_W)
    cs = np.clip(col - NA_KW // 2, 0, GRID_W - NA_KW)
    vcol = (col[None, :] >= cs[:, None]) & (col[None, :] < cs[:, None] + NA_KW)
    colmask = np.where(np.concatenate([vcol, vcol], axis=1), 0.0, NEG).astype(np.float32)
    co = col[None, :] - col[:, None] + (NA_KW - 1)
    oh_col = np.zeros((GRID_W * GRID_W, LANES), np.float32)
    for qc in range(GRID_W):
        for kc in range(GRID_W):
            if vcol[qc, kc]:
                oh_col[qc * GRID_W + kc, co[qc, kc]] = 1.0
    ks = np.clip(np.arange(N_QBLK) * NA_QROWS - NA_KH // 2, 0, ROWS - NA_KROWS) * GRID_W
    cls = np.ones(N_QBLK, np.int32)
    cls[0], cls[-1] = 0, 2
    return colmask, oh_col, ks.astype(np.int32), cls


def _bank_reduce(bank, oh_col, *, name):
    def body(d_ref, ohc_ref, o_ref):
        o_ref[0] = jnp.dot(d_ref[0], ohc_ref[...], preferred_element_type=F32, precision=lax.Precision.HIGHEST)

    return _pcall(
        body, name=name, grid=(NA_HEADS,),
        in_specs=[pl.BlockSpec((1, _BANK_ROWS, GRID_W * GRID_W), lambda h: (h, 0, 0)),
                  pl.BlockSpec((GRID_W * GRID_W, LANES), lambda h: (0, 0))],
        out_specs=pl.BlockSpec((1, _BANK_ROWS, LANES), lambda h: (h, 0, 0)),
        out_shape=jax.ShapeDtypeStruct((NA_HEADS, _BANK_ROWS, LANES), F32), compiler_params=_PAR,
    )(bank, oh_col)


def _dil_constants(dilation):
    seg = SEQ // dilation
    nb = seg // Q_BLOCK
    shift = (0, -DIL_HALF, -2 * DIL_HALF)
    qi = np.arange(Q_BLOCK)[:, None]
    ki = np.arange(DIL_NK)[None, :]
    mask = np.stack([np.where(np.abs(ki + sh - qi) <= DIL_HALF, 0.0, NEG) for sh in shift]).astype(np.float32)
    ks, cls = [], []
    for i in range(N_QBLK):
        sub, blk = divmod(i, nb)
        t = 0 if blk == 0 else (2 if blk == nb - 1 else 1)
        cls.append(t)
        ks.append(sub * seg + blk * Q_BLOCK + shift[t])
    return mask.reshape(3, 1, Q_BLOCK, DIL_NK), np.asarray(ks, np.int32), np.asarray(cls, np.int32)


_VM = pl.BlockSpec(memory_space=pltpu.VMEM)


def _ada_fwd(c_all, w, b, *, name):
    def body(c_ref, w_ref, b_ref, o_ref):
        cv = c_ref[...]
        o_ref[...] = jnp.dot(cv * _sigmoid(cv), w_ref[...], preferred_element_type=F32,
                             precision=lax.Precision.HIGHEST) + b_ref[...]

    return _pcall(body, name=name, in_specs=[_VM, _VM, _VM], out_specs=_VM,
                  out_shape=jax.ShapeDtypeStruct((c_all.shape[0], w.shape[1]), F32))(c_all, w, b)


def _ada_bwd(c_all_t, dmod, *, name):
    def body(c_ref, d_ref, o_ref):
        cv = c_ref[...]
        o_ref[...] = jnp.dot(cv * _sigmoid(cv), d_ref[...], preferred_element_type=F32,
                             precision=lax.Precision.HIGHEST)

    return _pcall(body, name=name, in_specs=[_VM, _VM], out_specs=_VM,
                  out_shape=jax.ShapeDtypeStruct((c_all_t.shape[0], dmod.shape[1]), F32))(c_all_t, dmod)


def _row_sum(t, *, name):
    def body(t_ref, o_ref):
        o_ref[...] = jnp.sum(t_ref[...], axis=0, keepdims=True)

    return _pcall(body, name=name, in_specs=[_VM], out_specs=_VM,
                  out_shape=jax.ShapeDtypeStruct((1, t.shape[1]), F32))(t)


def _row_tile(rows):
    tr = rows
    for cand in range(8, 513, 8):
        if rows % cand == 0:
            tr = cand
    return tr


def _adamw_math(wv, gv, mv, vv):
    nm = ADAM_B1 * mv + (1.0 - ADAM_B1) * gv
    nv = ADAM_B2 * vv + (1.0 - ADAM_B2) * (gv * gv)
    m_hat = nm / (1.0 - ADAM_B1 ** ADAM_STEP)
    v_hat = nv / (1.0 - ADAM_B2 ** ADAM_STEP)
    return -ADAM_LR * (m_hat / (jnp.sqrt(v_hat) + ADAM_EPS) + ADAM_WD * wv), nm, nv


def _adamw(w, g, m, v, *, name):
    rows, cols = w.shape
    tr = _row_tile(rows)

    def body(w_ref, g_ref, m_ref, v_ref, d_ref, nm_ref, nv_ref):
        d_ref[...], nm_ref[...], nv_ref[...] = _adamw_math(w_ref[...], g_ref[...], m_ref[...], v_ref[...])

    spec = pl.BlockSpec((tr, cols), lambda i: (i, 0))
    return _pcall(body, name=name, grid=(rows // tr,), in_specs=[spec] * 4, out_specs=[spec] * 3,
                  out_shape=[jax.ShapeDtypeStruct((rows, cols), F32)] * 3, compiler_params=_PAR)(w, g, m, v)


def _adamw_halves(w, g_mine, g_other, m, v, c_arr, *, name):
    rows, cols = w.shape
    hr = rows // 2
    tr = _row_tile(hr)
    nt = hr // tr

    def body(c_ref, w_ref, t_ref, o_ref, m_ref, v_ref, g_ref, d_ref, nm_ref, nv_ref):
        gv = jnp.where(pl.program_id(0) == c_ref[0], t_ref[...], o_ref[...])
        g_ref[...] = gv
        d_ref[...], nm_ref[...], nv_ref[...] = _adamw_math(w_ref[...], gv, m_ref[...], v_ref[...])

    full = pl.BlockSpec((tr, cols), lambda h, i, c: (h * nt + i, 0))
    half = pl.BlockSpec((tr, cols), lambda h, i, c: (i, 0))
    grid_spec = pltpu.PrefetchScalarGridSpec(num_scalar_prefetch=1, grid=(2, nt),
                                             in_specs=[full, half, half, full, full], out_specs=[full] * 4)
    return _pcall(body, name=name, grid_spec=grid_spec, out_shape=[jax.ShapeDtypeStruct((rows, cols), F32)] * 4,
                  compiler_params=pltpu.CompilerParams(dimension_semantics=("parallel", "parallel")),
                  )(c_arr, w, g_mine, g_other, m, v)


_MESH = pl.DeviceIdType.MESH
_ANY = pl.BlockSpec(memory_space=pl.ANY)
_CHIP_FLIPS = ((1, 0), (0, 1), (1, 1))


def _pos():
    return lax.axis_index("x"), lax.axis_index("y"), lax.axis_index("c")


def _flip(v, f):
    return 1 - v if f else v


def _sem_pairs(n):
    return [pltpu.SemaphoreType.DMA((n,)), pltpu.SemaphoreType.DMA((n,))]


def _small_allgather(blk, *, name):
    m_per, n = blk.shape

    def body(x_ref, out_ref, send_sems, recv_sems, local_sem):
        x, y, c = _pos()
        me, sibling = (x, y, c), (x, y, 1 - c)
        chips = [(_flip(x, fx), _flip(y, fy)) for fx, fy in _CHIP_FLIPS]

        def rows(px, py, pc):
            return out_ref.at[pl.ds((4 * px + 2 * py + pc) * m_per, m_per), :]

        def copy(k, block, to, src=None):
            return pltpu.make_async_remote_copy(
                src_ref=rows(*block) if src is None else src, dst_ref=rows(*block),
                send_sem=send_sems.at[k], recv_sem=recv_sems.at[k], device_id=to, device_id_type=_MESH)

        mine = pltpu.make_async_copy(x_ref, rows(*me), local_sem)
        mine.start()
        first = [copy(0, me, sibling, src=x_ref)]
        first += [copy(1 + j, me, (*chip, c), src=x_ref) for j, chip in enumerate(chips)]
        for cp in first:
            cp.start()
        passed = [copy(4 + j, (*chip, c), sibling) for j, chip in enumerate(chips)]
        for j, chip in enumerate(chips):
            copy(1 + j, (*chip, c), me).wait_recv()
            passed[j].start()
        copy(0, sibling, me).wait_recv()
        for j, chip in enumerate(chips):
            copy(4 + j, (*chip, 1 - c), me).wait_recv()
        for cp in first + passed:
            cp.wait_send()
        mine.wait()

    return _pcall(
        body, name=name, out_shape=jax.ShapeDtypeStruct((N_DEV * m_per, n), blk.dtype),
        in_specs=[_VM], out_specs=_VM,
        scratch_shapes=_sem_pairs(7) + [pltpu.SemaphoreType.DMA],
    )(blk)


_HBM = pl.BlockSpec(memory_space=pltpu.HBM)
_SEM = pl.BlockSpec(memory_space=pltpu.SEMAPHORE)
_EFFECT = pltpu.SideEffectType.DATAFLOW_SIDE_EFFECTING


def _split_start(srcs, lands, plan, ncopies, after, *, name):
    ns, nl = len(srcs), len(lands)

    def body(*refs):
        src_refs, land_refs = refs[:ns], refs[ns:ns + nl]
        send_sems, recv_sems = refs[ns + nl + 1], refs[ns + nl + 2]
        token = refs[-1]
        x, y, c = _pos()
        for k, (src, dst, to, _) in enumerate(plan(x, y, c, src_refs, land_refs)):
            pltpu.make_async_remote_copy(src_ref=src, dst_ref=dst, send_sem=send_sems.at[k], recv_sem=recv_sems.at[k],
                                         device_id=to, device_id_type=_MESH).start()
        token[...] = jnp.zeros_like(token)

    hbm = lambda a: pltpu.HBM(a.shape, a.dtype)
    out = _pcall(
        body, name=name,
        out_shape=(pltpu.SemaphoreType.DMA((ncopies,)), pltpu.SemaphoreType.DMA((ncopies,)),
                   *[hbm(a) for a in srcs], *[hbm(a) for a in lands], jax.ShapeDtypeStruct((8, LANES), F32)),
        in_specs=[_HBM] * (ns + nl) + [_ANY], out_specs=(_SEM, _SEM, *[_HBM] * (ns + nl), _VM),
        input_output_aliases={i: 2 + i for i in range(ns + nl)},
        compiler_params=pltpu.CompilerParams(has_side_effects=_EFFECT),
    )(*[pltpu.with_memory_space_constraint(a, pltpu.HBM) for a in (*srcs, *lands)], after)
    return out[0], out[1], list(out[2:2 + ns]), list(out[2 + ns:2 + ns + nl]), out[-1]


def _split_wait(send_sems, recv_sems, srcs, lands, plan, after, *, name):
    ns, nl = len(srcs), len(lands)

    def body(*refs):
        src_refs, land_refs = refs[:ns], refs[ns:ns + nl]
        send_sems, recv_sems = refs[ns + nl], refs[ns + nl + 1]
        x, y, c = _pos()
        for k, (src, _, _, mine) in enumerate(plan(x, y, c, src_refs, land_refs)):
            cp = pltpu.make_async_remote_copy(src_ref=src, dst_ref=mine, send_sem=send_sems.at[k],
                                              recv_sem=recv_sems.at[k], device_id=(x, y, c), device_id_type=_MESH)
            cp.wait_send()
            cp.wait_recv()

    hbm = lambda a: pltpu.HBM(a.shape, a.dtype)
    out = _pcall(
        body, name=name, out_shape=tuple(hbm(a) for a in (*srcs, *lands)),
        in_specs=[_HBM] * (ns + nl) + [_SEM, _SEM, _ANY], out_specs=tuple([_HBM] * (ns + nl)),
        input_output_aliases={i: i for i in range(ns + nl)},
        compiler_params=pltpu.CompilerParams(has_side_effects=_EFFECT),
    )(*srcs, *lands, send_sems, recv_sems, after)
    return list(out[ns:])


def _ag_plan(nw):
    def plan(x, y, c, sh_refs, full_refs):
        j = 2 * x + y
        out = []
        for wi in range(nw):
            for fx, fy in _CHIP_FLIPS:
                px, py = _flip(x, fx), _flip(y, fy)
                out.append((sh_refs[wi].at[c], full_refs[wi].at[j, c], (px, py, c), full_refs[wi].at[2 * px + py, c]))
            out.append((sh_refs[wi], full_refs[wi].at[j], (x, y, 1 - c), full_refs[wi].at[j]))
        return out
    return plan


def _ag_pass(fulls, *, name):
    nw = len(fulls)

    def body(*refs):
        in_refs, out_refs = refs[:nw], refs[nw:2 * nw]
        send_sems, recv_sems = refs[2 * nw:]
        x, y, c = _pos()
        cps = []
        for wi in range(nw):
            for k, (fx, fy) in enumerate(_CHIP_FLIPS):
                jp = 2 * _flip(x, fx) + _flip(y, fy)
                sems = dict(send_sem=send_sems.at[3 * wi + k], recv_sem=recv_sems.at[3 * wi + k], device_id_type=_MESH)
                send = pltpu.make_async_remote_copy(src_ref=in_refs[wi].at[jp, c], dst_ref=out_refs[wi].at[jp, c],
                                                    device_id=(x, y, 1 - c), **sems)
                recv = pltpu.make_async_remote_copy(src_ref=in_refs[wi].at[jp, c], dst_ref=out_refs[wi].at[jp, 1 - c],
                                                    device_id=(x, y, c), **sems)
                cps.append((send, recv))
        for send, _ in cps:
            send.start()
        for send, recv in cps:
            send.wait_send()
            recv.wait_recv()

    return _pcall(body, name=name, out_shape=[jax.ShapeDtypeStruct(f.shape, f.dtype) for f in fulls],
                  in_specs=[_ANY] * nw, out_specs=[_ANY] * nw, input_output_aliases={i: i for i in range(nw)},
                  scratch_shapes=_sem_pairs(3 * nw))(*fulls)


def _rs_plan(nw):
    def plan(x, y, c, s_refs, rb_refs):
        out = []
        for wi in range(nw):
            for k, (fx, fy) in enumerate(_CHIP_FLIPS):
                px, py = _flip(x, fx), _flip(y, fy)
                out.append((s_refs[wi].at[2 * px + py], rb_refs[wi].at[k], (px, py, c), rb_refs[wi].at[k]))
        return out
    return plan


def _sibling_send_halves(gs, *, name):
    nw = len(gs)

    def body(*refs):
        g_refs, out_refs = refs[:nw], refs[nw:2 * nw]
        send_sems, recv_sems = refs[2 * nw:]
        x, y, c = _pos()
        cps = [pltpu.make_async_remote_copy(src_ref=g_refs[wi].at[k, 1 - c], dst_ref=out_refs[wi].at[k],
                                            send_sem=send_sems.at[4 * wi + k], recv_sem=recv_sems.at[4 * wi + k],
                                            device_id=(x, y, 1 - c), device_id_type=_MESH)
               for wi in range(nw) for k in range(N_CHIP)]
        for cp in cps:
            cp.start()
        for cp in cps:
            cp.wait()

    return _pcall(body, name=name,
                  out_shape=[jax.ShapeDtypeStruct((g.shape[0],) + g.shape[2:], g.dtype) for g in gs],
                  in_specs=[_ANY] * nw, out_specs=[_ANY] * nw, scratch_shapes=_sem_pairs(N_CHIP * nw))(*gs)


def _sibling_swap(ts, *, name):
    nw = len(ts)

    def body(*refs):
        t_refs, out_refs = refs[:nw], refs[nw:2 * nw]
        send_sems, recv_sems = refs[2 * nw:]
        x, y, c = _pos()
        cps = [pltpu.make_async_remote_copy(src_ref=t_refs[wi], dst_ref=out_refs[wi], send_sem=send_sems.at[wi],
                                            recv_sem=recv_sems.at[wi], device_id=(x, y, 1 - c), device_id_type=_MESH)
               for wi in range(nw)]
        for cp in cps:
            cp.start()
        for cp in cps:
            cp.wait()

    return _pcall(body, name=name, out_shape=[jax.ShapeDtypeStruct(t.shape, t.dtype) for t in ts],
                  in_specs=[_ANY] * nw, out_specs=[_ANY] * nw, scratch_shapes=_sem_pairs(nw))(*ts)


def _rs_add(g, ra, c_arr, *, name):
    n, _, r, w = g.shape

    def body(c_ref, g_ref, ra_ref, s_ref, sb_ref):
        t = g_ref[...] + ra_ref[...]
        s_ref[...] = t
        sb_ref[...] = t.astype(BF16)

    grid_spec = pltpu.PrefetchScalarGridSpec(
        num_scalar_prefetch=1, grid=(n,),
        in_specs=[pl.BlockSpec((None, None, r, w), lambda k, c: (k, c[0], 0, 0)),
                  pl.BlockSpec((None, r, w), lambda k, c: (k, 0, 0))],
        out_specs=[pl.BlockSpec((None, r, w), lambda k, c: (k, 0, 0))] * 2)
    return _pcall(body, name=name, grid_spec=grid_spec,
                  out_shape=[jax.ShapeDtypeStruct((n, r, w), F32), jax.ShapeDtypeStruct((n, r, w), BF16)],
                  compiler_params=_PAR)(c_arr, g, ra)


def _rs_final(s, rb, j_arr, *, name):
    _, r, w = s.shape

    def body(j_ref, s_ref, rb_ref, t_ref):
        t_ref[...] = ((s_ref[...] + rb_ref[0].astype(F32)) + rb_ref[1].astype(F32)) + rb_ref[2].astype(F32)

    grid_spec = pltpu.PrefetchScalarGridSpec(
        num_scalar_prefetch=1, grid=(1,),
        in_specs=[pl.BlockSpec((None, r, w), lambda i, j: (j[0], 0, 0)),
                  pl.BlockSpec((3, r, w), lambda i, j: (0, 0, 0))],
        out_specs=pl.BlockSpec((r, w), lambda i, j: (0, 0)))
    return _pcall(body, name=name, grid_spec=grid_spec, out_shape=jax.ShapeDtypeStruct((r, w), F32),
                  compiler_params=_ARB)(j_arr, s, rb)


def _tile2(g):
    return jnp.concatenate([g, g], axis=1)


_BIG = ("w_in", "w_ffn_in", "w_ffn_out", "w_o", "w_proj_a", "w_proj_b")
_BIG_SHARD = {"w_in": (1024, 1472), "w_ffn_in": (1024, 1408), "w_ffn_out": (704, 1024), "w_o": (256, 1024),
              "w_proj_a": (512, 256), "w_proj_b": (256, 256)}


def _device_step(x2, tgt, mod, first_weights, late_weights, early_grads, g_norm1, g_norm2, b_gate, g_qa, g_ka, g_qb,
                 g_kb, rpb):
    d = D_MODEL
    sh1, sc1, gt1, sh2, sc2, gt2 = [mod[:, k * d:(k + 1) * d] for k in range(6)]

    colmask, oh_col, na_ks, na_cls = _na_constants()
    rp = jnp.pad(rpb, ((0, 0), (0, 16 - _RPB_RO), (RP_LANE0, LANES - RP_LANE0 - _RPB_CO)), constant_values=NEG)
    na = (rp, jnp.asarray(colmask))
    na_ks, na_cls = jnp.asarray(na_ks), jnp.asarray(na_cls)
    dil = [_dil_constants(dd) for _, dd in DIL_CONFIGS]
    tab_d = jnp.asarray(dil[0][0])
    gains = jnp.concatenate([_tile2(g_qa), _tile2(g_ka), _tile2(g_qb), _tile2(g_kb)], axis=0)
    cos_t, sa_t, sb_t = _rope_tables()

    wts = first_weights(cos_t)
    h1, qkvn, qk_pre, gates, *qkv_dil = _pre_attn_fwd(x2, cos_t, sa_t, sb_t, g_norm1, sc1, sh1, wts["w_qkv"],
                                                      wts["w_gates"], gains, name="pre_attn_fwd")
    o_a, lse_a = _attn_fwd(qkvn, 0, 4, 8, 4, None, na_ks, na_cls, NA_NK, name="attn_a_fwd", na=na)
    arrs, o_g, l_g = [], [], []
    res = lambda t, dd: t if dd == 1 else (t, dd)
    for g, (_, dd) in enumerate(DIL_CONFIGS):
        ks_g, cls_g = jnp.asarray(dil[g][1]), jnp.asarray(dil[g][2])
        arr, cb = (qkvn, (12, 18, 24)) if dd == 1 else (qkv_dil.pop(0), (0, 2, 4))
        op, lp = _attn_fwd(arr, cb[0], cb[1], cb[2], 2, tab_d, ks_g, cls_g, DIL_NK, name=f"attn_d{g}_fwd")
        arrs.append((arr, cb, ks_g, cls_g))
        o_g.append(res(op, dd))
        l_g.append(res(lp, dd))
    wts = dict(wts, **late_weights(o_a))
    o_b, merged, mo, x1, h2 = _post_attn_fwd(o_a, o_g, l_g, gates, x2, wts["w_pa"], wts["w_pb"], wts["w_o"], b_gate,
                                             gt1, g_norm2, sc2, sh2, name="post_attn_fwd")
    act, ff = _ffn_fwd(h2, wts["w_ffn_in"], name="ffn_fwd")

    dy, dffo, dff, dgt2, loss_v = _ffn_mid(act, ff, x1, tgt, wts["w_ffn_out"], gt2, name="ffn_mid")
    grads = {}
    g_ffn_out = _wgrad(act, dffo, name="wg_ffn_out", tm=D_FF // 2, tn=d)
    grads["w_ffn_out"] = g_ffn_out.reshape(N_CHIP, D_FF // N_CHIP, d)
    grads["w_ffn_in"] = _wgrad(h2, dff, name="wg_ffn_in", tm=512, tn=2 * FF_CHIP, chips=N_CHIP)
    dx1, dmo, sums2 = _ffn_in_bwd(dff, x1, dy, mo, wts["w_ffn_in"], g_norm2, sc2, gt1, name="ffn_in_bwd")
    grads["w_o"] = _wgrad(merged, dmo, name="wg_o", tm=d, tn=d).reshape(N_CHIP, d // N_CHIP, d)
    pab = _post_attn_bwd(dmo, gates, o_a, o_g, l_g, wts["w_pa"], wts["w_pb"], wts["w_o"], b_gate, name="post_attn_bwd")
    dpa, dpb, dgates, do_a = pab[:4]
    do_g, dl_g, dbg = pab[4:7], pab[7:10], pab[10]
    g_pa = _wgrad(o_a, dpa, name="wg_pa", tm=WA, tn=d)
    g_pb = _wgrad(o_b, dpb, name="wg_pb", tm=WB_OUT, tn=d)
    grads["w_proj_a"] = g_pa.reshape(WA, N_CHIP, d // N_CHIP).transpose(1, 0, 2)
    grads["w_proj_b"] = g_pb.reshape(WB_OUT, N_CHIP, d // N_CHIP).transpose(1, 0, 2)
    early_grads(grads)
    dqa, dka, dva, bank = _attn_bwd(qkvn, 0, 4, 8, 4, None, na_ks, na_cls, NA_NK, do_a, o_a, lse_a,
                                    name="attn_a_bwd", na=na)
    dqs, dks, dvs = [], [], []
    for g, (_, dd) in enumerate(DIL_CONFIGS):
        arr, cb, ks_g, cls_g = arrs[g]
        plain = lambda t: t[0] if isinstance(t, tuple) else t
        dq, dk, dv = _attn_bwd(arr, cb[0], cb[1], cb[2], 2, tab_d, ks_g, cls_g, DIL_NK, do_g[g], plain(o_g[g]),
                               plain(l_g[g]), name=f"attn_d{g}_bwd", dlse=dl_g[g])
        dqs.append(res(dq, dd))
        dks.append(res(dk, dd))
        dvs.append(res(dv, dd))
    dqkv, grad_x, dgains, sums1 = _pre_attn_bwd(qk_pre, [dqa, dka, dva] + dqs + dks + dvs, dgates, x2, dx1, cos_t, sa_t,
                                                sb_t, wts["w_qkv"], wts["w_gates"], gains, g_norm1, sc1,
                                                name="pre_attn_bwd")
    g_qkv = _wgrad(h1, dqkv, name="wg_qkv", tm=d, tn=W_QKV // 2)
    g_gates = _wgrad(h1, dgates, name="wg_gates", tm=d, tn=W_GATES)
    nc, cut = _BIG_SHARD["w_in"][1], 3 * _BIG_SHARD["w_in"][1] - W_QKV
    grads["w_in"] = jnp.stack([g_qkv[:, :nc], g_qkv[:, nc:2 * nc],
                               jnp.concatenate([g_qkv[:, 2 * nc:], g_gates[:, :cut]], axis=1), g_gates[:, cut:]])

    bank = bank.reshape(NA_HEADS, NA_SLOTS, GRID_W, 2, GRID_W).transpose(0, 1, 3, 2, 4)
    bank = jnp.pad(bank.reshape(NA_HEADS, 2 * NA_SLOTS, GRID_W * GRID_W), ((0, 0), (0, _BANK_ROWS - 2 * NA_SLOTS), (0, 0)))
    g2 = _bank_reduce(bank, jnp.asarray(oh_col), name="rpb_reduce")[:, :2 * NA_SLOTS].reshape(NA_HEADS, NA_SLOTS, 2, LANES)
    g_rpb = g2[:, 3:3 + _RPB_RO, 0, :_RPB_CO] + g2[:, 2:2 + _RPB_RO, 1, :_RPB_CO]

    dmod = jnp.concatenate([sums1[0:1], sums1[1:2], sums2[3:4], sums2[0:1], sums2[1:2], dgt2], axis=1)
    small = dict(g_norm1=sums1[2:3], g_norm2=sums2[2:3], b_gate=dbg, g_qa=dgains[0:1, :HEAD_DIM],
                 g_ka=dgains[1:2, :HEAD_DIM], g_qb=dgains[2:3, :HEAD_DIM], g_kb=dgains[3:4, :HEAD_DIM], rpb=g_rpb)
    return loss_v, grad_x, grads, dmod, small


_SMALL = ("b_ada", "g_norm1", "g_norm2", "b_gate", "g_qa", "g_ka", "g_qb", "g_kb", "rpb")
_SMALL_N = {"b_ada": 6 * D_MODEL, "g_norm1": D_MODEL, "g_norm2": D_MODEL, "b_gate": 2 * D_MODEL, "g_qa": HEAD_DIM,
            "g_ka": HEAD_DIM, "g_qb": HEAD_DIM, "g_kb": HEAD_DIM, "rpb": NA_HEADS * _RPB_RO * _RPB_CO}


def _pack_small(parts):
    flat = [parts[n].reshape(1, _SMALL_N[n]) for n in _SMALL]
    used = sum(_SMALL_N.values())
    return jnp.concatenate(flat + [jnp.zeros((1, STATS_W - used), F32)], axis=1)


def _unpack_small(v, shapes):
    out, at = {}, 0
    for n in _SMALL:
        out[n] = v[:, at:at + _SMALL_N[n]].reshape(shapes[n])
        at += _SMALL_N[n]
    return out


def _join_cols(t):
    _, r, c = t.shape
    return t.transpose(1, 0, 2).reshape(r, N_CHIP * c)


def kernel(x, c, w_ada, b_ada, g_norm1, g_norm2, w_in, b_gate, g_qa, g_ka, g_qb, g_kb, rpb, w_proj_a, w_proj_b, w_o, w_ffn_in, w_ffn_out, loss_target, m_w_ada, m_b_ada, m_g_norm1, m_g_norm2, m_w_in, m_b_gate, m_g_qa, m_g_ka, m_g_qb, m_g_kb, m_rpb, m_w_proj_a, m_w_proj_b, m_w_o, m_w_ffn_in, m_w_ffn_out, v_w_ada, v_b_ada, v_g_norm1, v_g_norm2, v_w_in, v_b_gate, v_g_qa, v_g_ka, v_g_qb, v_g_kb, v_rpb, v_w_proj_a, v_w_proj_b, v_w_o, v_w_ffn_in, v_w_ffn_out):
    names = ("w_ada", "b_ada", "g_norm1", "g_norm2", "w_in", "b_gate", "g_qa", "g_ka", "g_qb", "g_kb", "rpb",
             "w_proj_a", "w_proj_b", "w_o", "w_ffn_in", "w_ffn_out")
    w = dict(zip(names, (w_ada, b_ada, g_norm1, g_norm2, w_in, b_gate, g_qa, g_ka, g_qb, g_kb, rpb, w_proj_a, w_proj_b,
                         w_o, w_ffn_in, w_ffn_out)))
    m = dict(zip(names, (m_w_ada, m_b_ada, m_g_norm1, m_g_norm2, m_w_in, m_b_gate, m_g_qa, m_g_ka, m_g_qb, m_g_kb, m_rpb,
                         m_w_proj_a, m_w_proj_b, m_w_o, m_w_ffn_in, m_w_ffn_out)))
    v = dict(zip(names, (v_w_ada, v_b_ada, v_g_norm1, v_g_norm2, v_w_in, v_b_gate, v_g_qa, v_g_ka, v_g_qb, v_g_kb, v_rpb,
                         v_w_proj_a, v_w_proj_b, v_w_o, v_w_ffn_in, v_w_ffn_out)))
    d = D_MODEL
    xi, yi, ci = _pos()
    chip = 2 * xi + yi
    me = 2 * chip + ci
    ada_cols = 6 * d // N_CHIP

    c_arr, chip_arr = ci.reshape(1).astype(jnp.int32), chip.reshape(1).astype(jnp.int32)
    first, rest = _BIG[:1], _BIG[1:]

    c_all = _small_allgather(jnp.broadcast_to(c, (8, d)), name="ag_c")[::8]
    b_sh = lax.dynamic_slice(b_ada, (0, chip * ada_cols), (1, ada_cols))
    mod_part = _ada_fwd(c_all, w_ada[0], b_sh, name="ada_fwd")
    mod_all = _small_allgather(mod_part, name="ag_mod").reshape(N_CHIP, 2, 8, ada_cols)[:, 0]
    mod = lax.dynamic_index_in_dim(mod_all, me, axis=1, keepdims=False).reshape(1, 6 * d)

    halves = {n: (2, _BIG_SHARD[n][0] // 2, _BIG_SHARD[n][1]) for n in _BIG}
    shards = {n: w[n][0].astype(BF16).reshape(halves[n]) for n in _BIG}
    land = lambda n: lax.empty((N_CHIP,) + halves[n], BF16)
    ag1 = _split_start([shards[n] for n in first], [land(n) for n in first], _ag_plan(1), 4, mod, name="ag1_start")
    ag2 = _split_start([shards[n] for n in rest], [land(n) for n in rest], _ag_plan(len(rest)), 4 * len(rest),
                       ag1[4], name="ag2_start")
    rpb_after = rpb[0] + ag2[4][0, 0]

    def first_weights(after):
        full1 = _split_wait(ag1[0], ag1[1], ag1[2], ag1[3], _ag_plan(1), after, name="ag1_wait")
        p_in = _ag_pass(full1, name="ag1_pass")[0].reshape((N_CHIP,) + _BIG_SHARD["w_in"])
        cut = W_QKV - 2 * _BIG_SHARD["w_in"][1]
        return dict(w_qkv=jnp.concatenate([p_in[0], p_in[1], p_in[2][:, :cut]], axis=1),
                    w_gates=jnp.concatenate([p_in[2][:, cut:], p_in[3]], axis=1))

    def late_weights(after):
        full2 = _split_wait(ag2[0], ag2[1], ag2[2], ag2[3], _ag_plan(len(rest)), after, name="ag2_wait")
        full2 = _ag_pass(full2, name="ag2_pass")
        full = {n: fu.reshape((N_CHIP,) + _BIG_SHARD[n]) for n, fu in zip(rest, full2)}
        return dict(w_pa=_join_cols(full["w_proj_a"]), w_pb=_join_cols(full["w_proj_b"]), w_o=full["w_o"].reshape(d, d),
                    w_ffn_in=_join_cols(full["w_ffn_in"]), w_ffn_out=full["w_ffn_out"].reshape(D_FF, d))

    def rs_begin(group, grads, tag):
        gps = [grads[n].reshape((N_CHIP,) + halves[n]) for n in group]
        ras = _sibling_send_halves(gps, name=f"rs_sibling_{tag}")
        sums = [_rs_add(gp, ra, c_arr, name=f"rs_add_{n}") for n, gp, ra in zip(group, gps, ras)]
        lands = [lax.empty((3,) + halves[n][1:], BF16) for n in group]
        st = _split_start([sb for _, sb in sums], lands, _rs_plan(len(group)), 3 * len(group), sums[0][0],
                          name=f"rs_{tag}_start")
        return sums, st

    def rs_end(group, begun, after, tag):
        sums, st = begun
        rbs = _split_wait(st[0], st[1], st[2], st[3], _rs_plan(len(group)), after, name=f"rs_{tag}_wait")
        return [_rs_final(sf, rb, chip_arr, name=f"rs_final_{n}") for n, (sf, _), rb in zip(group, sums, rbs)]

    begun = {}

    def early_grads(grads):
        begun["rest"] = rs_begin(rest, grads, "rest")

    loss_v, grad_x, grads, dmod, small = _device_step(
        x[0], loss_target[0], mod, first_weights, late_weights, early_grads, g_norm1, g_norm2, b_gate, g_qa, g_ka, g_qb,
        g_kb, rpb_after)
    begun["first"] = rs_begin(first, grads, "first")

    g, delta, new_m, new_v = {}, {}, {}, {}

    def finish(group, ts, tag):
        others = _sibling_swap(ts, name=f"rs_pair_{tag}")
        for n, t, o in zip(group, ts, others):
            gg, dl, nm, nv = _adamw_halves(w[n][0], t, o, m[n][0], v[n][0], c_arr, name=f"adamw_{n}")
            g[n], delta[n], new_m[n], new_v[n] = gg[None], dl[None], nm[None], nv[None]

    finish(rest, rs_end(rest, begun["rest"], begun["first"][0][0][1], "rest"), "rest")

    stats = _pack_small(dict(b_ada=dmod, **small))
    rows = _small_allgather(jnp.broadcast_to(stats, (8, STATS_W)), name="ag_stats")[::8]
    dmod_sh = lax.dynamic_slice(rows, (0, chip * ada_cols), (8, ada_cols))
    g_ada = _ada_bwd(c_all.T, dmod_sh, name="ada_bwd")
    tot = _row_sum(rows, name="stats_sum")
    g_small = _unpack_small(tot, {n: w[n].shape for n in _SMALL})

    finish(first, rs_end(first, begun["first"], new_v[rest[0]], "first"), "first")

    dl, nm, nv = _adamw(w_ada[0], g_ada, m_w_ada[0], v_w_ada[0], name="adamw_w_ada")
    g["w_ada"], delta["w_ada"], new_m["w_ada"], new_v["w_ada"] = g_ada[None], dl[None], nm[None], nv[None]
    shapes = {n: w[n].shape for n in _SMALL}
    dl, nm, nv = _adamw(_pack_small({n: w[n] for n in _SMALL}), tot, _pack_small({n: m[n] for n in _SMALL}),
                        _pack_small({n: v[n] for n in _SMALL}), name="adamw_small")
    delta.update(_unpack_small(dl, shapes))
    new_m.update(_unpack_small(nm, shapes))
    new_v.update(_unpack_small(nv, shapes))
    g.update(g_small)

    loss = lax.psum(loss_v[0, 0], ("x", "y", "c"))
    return (loss, grad_x[None], *[g[n] for n in names], *[delta[n] for n in names], *[new_m[n] for n in names],
            *[new_v[n] for n in names])
```

```python
import numpy as np

import jax
import jax.numpy as jnp
from jax import lax
from jax.experimental import pallas as pl
from jax.experimental.pallas import tpu as pltpu

F32 = jnp.float32
BF16 = jnp.bfloat16

D_MODEL = 1024
SEQ = 8192
HEAD_DIM = 64
GRID_W = 64
ROWS = SEQ // GRID_W
NA_HEADS = 8
NA_KH = 8
NA_KW = 16
DIL_CONFIGS = ((128, 1), (512, 4), (2048, 16))
ROT_DIM = 16
ROPE_THETA = 500000.0
D_FF = 2816
EPS = 1e-6
NEG = -1e30
WA = 512
WB = 768
WB_OUT = 256
W_QKV = 3 * WA + 3 * WB
W_QK = 2 * WA + 2 * WB
W_GATES = 2 * D_MODEL
SCALE = HEAD_DIM ** -0.5

ADAM_LR = 0.001
ADAM_B1 = 0.9
ADAM_B2 = 0.999
ADAM_EPS = 1e-08
ADAM_WD = 0.01
ADAM_STEP = 10

LANES = 128
ROW_TILE = 256
ROW_TILES = {"ffn_fwd": 512, "post_attn_fwd": 512, "post_attn_bwd": 512,
             "ffn_in_bwd": 512}
Q_BLOCK = 256
NA_QROWS = Q_BLOCK // GRID_W
NA_KROWS = NA_QROWS + NA_KH - 1
NA_NK = NA_KROWS * GRID_W
NA_PAIRS = (NA_KROWS + 1) // 2
NA_W = NA_PAIRS * LANES
NA_RO_NONE = 15
NA_SLOTS = 21
RP_LANE0 = GRID_W - NA_KW
DIL_HALF = 64
DIL_NK = Q_BLOCK + 2 * DIL_HALF
N_QBLK = SEQ // Q_BLOCK

N_DEV = 8
N_CHIP = 4
FF_CHIP = 2 * D_FF // N_CHIP
STATS_W = 14336


def _pcall(body, *, name, **kw):
    return pl.pallas_call(body, name=name, **kw)


_NT = (((1,), (1,)), ((), ()))
_TN = (((0,), (0,)), ((), ()))
_ARB = pltpu.CompilerParams(dimension_semantics=("arbitrary",))
_PAR = pltpu.CompilerParams(dimension_semantics=("parallel",))


def _dot(a, b):
    return jnp.dot(a, b, preferred_element_type=F32)


def _dot_nt(a, b):
    return lax.dot_general(a, b, _NT, preferred_element_type=F32)


def _wgrad(a, b, *, name, tm, tn, tk=1024, chips=None):
    s, ma = a.shape
    nb = b.shape[1]
    nk = s // tk
    nc = nb // chips if chips else tn
    cpb = tn // nc

    def body(a_ref, b_ref, o_ref, acc):
        k = pl.program_id(2)
        r = lax.dot_general(a_ref[...].astype(BF16), b_ref[...].astype(BF16), _TN, preferred_element_type=F32)

        @pl.when(k == 0)
        def _():
            acc[...] = r

        @pl.when(k > 0)
        def _():
            acc[...] += r

        @pl.when(k == nk - 1)
        def _():
            if chips:
                for q in range(cpb):
                    o_ref[q] = acc[:, q * nc:(q + 1) * nc]
            else:
                o_ref[...] = acc[...]

    if chips:
        o_spec = pl.BlockSpec((cpb, tm, nc), lambda i, j, k: (j, i, 0))
        out_shape = jax.ShapeDtypeStruct((chips, ma, nc), F32)
    else:
        o_spec = pl.BlockSpec((tm, tn), lambda i, j, k: (i, j))
        out_shape = jax.ShapeDtypeStruct((ma, nb), F32)
    return _pcall(
        body, name=name, grid=(ma // tm, nb // tn, nk),
        in_specs=[pl.BlockSpec((tk, tm), lambda i, j, k: (k, i)), pl.BlockSpec((tk, tn), lambda i, j, k: (k, j))],
        out_specs=o_spec, out_shape=out_shape, scratch_shapes=[pltpu.VMEM((tm, tn), F32)],
        compiler_params=pltpu.CompilerParams(dimension_semantics=("parallel", "parallel", "arbitrary")),
    )(a, b)


def _row_call(body, *, name, row_ins, res_ins, row_outs, acc_outs=(), scratch=()):
    row_ins = [a if isinstance(a, tuple) else (a, 1) for a in row_ins]
    row_outs = [o if len(o) == 3 else (*o, 1) for o in row_outs]
    s = row_ins[0][0].shape[0]
    tile = ROW_TILES.get(name, ROW_TILE)
    n = s // tile
    nri, nre, nro, nao = len(row_ins), len(res_ins), len(row_outs), len(acc_outs)

    def whole(shape):
        nd = len(shape)
        return pl.BlockSpec(tuple(shape), lambda i: (0,) * nd, pipeline_mode=pl.Buffered(1))

    def whole_out(shape):
        nd = len(shape)
        return pl.BlockSpec(tuple(shape), lambda i: (0,) * nd)

    def rows(w, d):
        if d == 1:
            return pl.BlockSpec((tile, w), lambda i: (i, 0))
        return pl.BlockSpec((d, tile // d, w), lambda i: (0, i, 0))

    in_specs = [rows(a.shape[1], d) for a, d in row_ins]
    in_specs += [whole(a.shape) for a in res_ins]
    out_specs = [rows(w, d) for w, _, d in row_outs]
    out_specs += [whole_out(shp) for shp, _ in acc_outs]
    out_shape = [jax.ShapeDtypeStruct((s, w) if d == 1 else (d, s // d, w), dt) for w, dt, d in row_outs]
    out_shape += [jax.ShapeDtypeStruct(tuple(shp), dt) for shp, dt in acc_outs]

    def wrapped(*refs):
        at = [0, nri, nri + nre, nri + nre + nro, nri + nre + nro + nao]
        body(pl.program_id(0), n, refs[at[0]:at[1]], refs[at[1]:at[2]], refs[at[2]:at[3]], refs[at[3]:at[4]],
             refs[at[4]:])

    args = [a if d == 1 else a.reshape(d, s // d, a.shape[1]) for a, d in row_ins]
    outs = _pcall(wrapped, name=name, grid=(n,), in_specs=in_specs, out_specs=out_specs, out_shape=out_shape,
                  scratch_shapes=list(scratch), compiler_params=_ARB)(*args, *res_ins)
    return [o.reshape(s, o.shape[-1]) if k < nro and row_outs[k][2] != 1 else o for k, o in enumerate(outs)]


def _stage_shape(name):
    return pltpu.VMEM((4, ROW_TILES.get(name, ROW_TILE), LANES), F32)


def _from_residue(ref, col, stage, slot):
    d, n = ref.shape[0], ref.shape[1]
    for r in range(d):
        stage.at[slot][pl.ds(r, n, stride=d), :] = ref[r, :, col:col + LANES].astype(F32)
    return stage[slot]


def _natural(ref, stage, slot0):
    if len(ref.shape) == 2:
        return ref[...]
    return jnp.concatenate([_from_residue(ref, c * LANES, stage, (slot0 + c) % 4)
                            for c in range(ref.shape[2] // LANES)], axis=1)


def _to_residue(val, ref, col, stage, slot):
    d, n = ref.shape[0], ref.shape[1]
    stage[slot] = val
    for r in range(d):
        ref[r, :, col:col + LANES] = stage.at[slot][pl.ds(r, n, stride=d), :].astype(ref.dtype)


def _fold8(t):
    r, w = t.shape
    return jnp.sum(t.reshape(r // 8, 8, w), axis=0)


def _sigmoid(t):
    return 1.0 / (1.0 + jnp.exp(-t))


def _head_lanes():
    return lax.broadcasted_iota(jnp.int32, (1, LANES), 1) < HEAD_DIM


def _head_mean(t, lo):
    s_lo = jnp.sum(jnp.where(lo, t, 0.0), axis=1, keepdims=True)
    s_hi = jnp.sum(jnp.where(lo, 0.0, t), axis=1, keepdims=True)
    return jnp.where(lo, s_lo, s_hi) * (1.0 / HEAD_DIM)


def _rms_mod(xv, g, sc, sh):
    rstd = lax.rsqrt(jnp.mean(xv * xv, axis=1, keepdims=True) + EPS)
    return (xv * rstd * g) * (1.0 + sc) + sh


def _rms_mod_bwd(xv, dh, g, sc):
    rstd = lax.rsqrt(jnp.mean(xv * xv, axis=1, keepdims=True) + EPS)
    xhat = xv * rstd
    dn = dh * (1.0 + sc)
    dxhat = dn * g
    dx = rstd * (dxhat - xhat * jnp.mean(dxhat * xhat, axis=1, keepdims=True))
    return dx, dh, dh * (xhat * g), dn * xhat


def _mix_weights(ls):
    m = jnp.maximum(jnp.maximum(ls[0], ls[1]), ls[2])
    es = [jnp.exp(t - m) for t in ls]
    den = es[0] + es[1] + es[2]
    return [e / den for e in es]


def _rope_tables():
    half = ROT_DIM // 2
    inv_freq = ROPE_THETA ** (-(jnp.arange(half, dtype=F32) * 2.0) / ROT_DIM)
    lane = np.arange(LANES) % HEAD_DIM
    ang = jnp.arange(SEQ).astype(F32)[:, None] * jnp.tile(inv_freq, LANES // half)[None, :]
    cos, sin = jnp.cos(ang), jnp.sin(ang)
    first, second = jnp.asarray(lane < half)[None, :], jnp.asarray((lane >= half) & (lane < ROT_DIM))[None, :]
    cos_t = jnp.where(first | second, cos, 1.0)
    return cos_t, jnp.where(second, sin, 0.0), jnp.where(first, -sin, 0.0)


_SECTIONS = ((0, WA, 0, False), (WA, 2 * WA, 1, False), (2 * WA, 3 * WA, -1, False),
             (3 * WA, 3 * WA + WB, 2, True), (3 * WA + WB, 3 * WA + 2 * WB, 3, True), (3 * WA + 2 * WB, W_QKV, -1, False))


def _pre_attn_fwd(x, cos_t, sa_t, sb_t, g1, sc1, sh1, w_qkv, w_gates, gains, *, name):
    half = ROT_DIM // 2
    dilated = [(g, dd) for g, (_, dd) in enumerate(DIL_CONFIGS) if dd > 1]

    def body(i, n, rin, res, rout, aout, scr):
        x_ref, cos_ref, sa_ref, sb_ref = rin
        g_ref, sc_ref, sh_ref, wq_ref, wg_ref, gains_ref = res
        h1_ref, qkvn_ref, pre_ref, gates_ref = rout[:4]
        group_ref = {g: rout[4 + k] for k, (g, _) in enumerate(dilated)}
        (stage,) = scr
        staged = 0
        hb = _rms_mod(x_ref[...], g_ref[...], sc_ref[...], sh_ref[...]).astype(BF16)
        h1_ref[...] = hb
        gates_ref[...] = _dot(hb, wg_ref[...]).astype(BF16)
        lo = _head_lanes()
        cosv, sav, sbv = cos_ref[...], sa_ref[...], sb_ref[...]
        pre_at = 0
        for si, (c0, c1, kind, rot) in enumerate(_SECTIONS):
            sec = _dot(hb, wq_ref[:, c0:c1])
            for ch in range((c1 - c0) // LANES):
                t = sec[:, ch * LANES:(ch + 1) * LANES]
                if kind >= 0:
                    pre_ref[:, pre_at:pre_at + LANES] = t.astype(BF16)
                    pre_at += LANES
                    t = t * lax.rsqrt(_head_mean(t * t, lo) + EPS) * gains_ref[kind:kind + 1, :]
                    if rot:
                        t = t * cosv + pltpu.roll(t, half, 1) * sav + pltpu.roll(t, LANES - half, 1) * sbv
                qkvn_ref[:, c0 + ch * LANES:c0 + (ch + 1) * LANES] = t.astype(BF16)
                group = ch * LANES // WB_OUT if si >= 3 else 0
                if group in group_ref:
                    col = (si - 3) * WB_OUT + ch * LANES % WB_OUT
                    _to_residue(t, group_ref[group], col, stage, staged % 4)
                    staged += 1

    return _row_call(body, name=name, row_ins=[x, cos_t, sa_t, sb_t], res_ins=[g1, sc1, sh1, w_qkv, w_gates, gains],
                     row_outs=[(D_MODEL, BF16), (W_QKV, BF16), (W_QK, BF16), (W_GATES, BF16)]
                     + [(3 * WB_OUT, BF16, dd) for _, dd in dilated], scratch=[_stage_shape(name)])


def _pre_attn_bwd(qk_pre, d_parts, dgates, x, dx1, cos_t, sa_t, sb_t, w_qkv, w_gates, gains, g1, sc1, *, name):
    half = ROT_DIM // 2
    nparts = len(d_parts)
    where = []
    residue = [isinstance(part, tuple) for part in d_parts]
    for pi, part in enumerate(d_parts):
        width = (part[0] if residue[pi] else part).shape[1]
        where += [(pi, cj) for cj in range(width // LANES)]
    assert len(where) == W_QKV // LANES

    def body(i, n, rin, res, rout, aout, scr):
        pre_ref, d_refs = rin[0], rin[1:1 + nparts]
        dgates_ref, x_ref, dx1_ref, cos_ref, sa_ref, sb_ref = rin[1 + nparts:]
        wq_ref, wg_ref, gains_ref, g_ref, sc_ref = res
        dqkv_ref, gx_ref = rout
        dgains_ref, sums_ref = aout
        accg, accs, stage = scr
        staged = 0

        @pl.when(i == 0)
        def _():
            accg[...] = jnp.zeros_like(accg)
            accs[...] = jnp.zeros_like(accs)

        lo = _head_lanes()
        cosv, sav, sbv = cos_ref[...], sa_ref[...], sb_ref[...]
        dh = _dot_nt(dgates_ref[...], wg_ref[...])
        pre_at = 0
        for c0, c1, kind, rot in _SECTIONS:
            for ch in range((c1 - c0) // LANES):
                pi, cj = where[c0 // LANES + ch]
                if residue[pi]:
                    dt = _from_residue(d_refs[pi], cj * LANES, stage, staged % 4)
                    staged += 1
                else:
                    dt = d_refs[pi][:, cj * LANES:(cj + 1) * LANES]
                if kind >= 0:
                    if rot:
                        dt = dt * cosv + pltpu.roll(dt * sav, LANES - half, 1) + pltpu.roll(dt * sbv, half, 1)
                    t = pre_ref[:, pre_at:pre_at + LANES].astype(F32)
                    pre_at += LANES
                    rstd = lax.rsqrt(_head_mean(t * t, lo) + EPS)
                    xhat = t * rstd
                    accg[kind] += _fold8(dt * xhat)
                    dxhat = dt * gains_ref[kind:kind + 1, :]
                    dt = rstd * (dxhat - xhat * _head_mean(dxhat * xhat, lo))
                dqkv_ref[:, c0 + ch * LANES:c0 + (ch + 1) * LANES] = dt.astype(BF16)
            dh = dh + _dot_nt(dqkv_ref[:, c0:c1], wq_ref[:, c0:c1])
        dx, t_sh, t_sc, t_g = _rms_mod_bwd(x_ref[...], dh, g_ref[...], sc_ref[...])
        gx_ref[...] = dx1_ref[...] + dx
        accs[0] += _fold8(t_sh)
        accs[1] += _fold8(t_sc)
        accs[2] += _fold8(t_g)

        @pl.when(i == n - 1)
        def _():
            t = jnp.sum(accg[...], axis=1)
            dgains_ref[...] = t + pltpu.roll(t, HEAD_DIM, 1)
            sums_ref[...] = jnp.sum(accs[...], axis=1)

    return _row_call(
        body, name=name, row_ins=[qk_pre, *d_parts, dgates, x, dx1, cos_t, sa_t, sb_t],
        res_ins=[w_qkv, w_gates, gains, g1, sc1], row_outs=[(W_QKV, BF16), (D_MODEL, F32)],
        acc_outs=[((4, LANES), F32), ((3, D_MODEL), F32)],
        scratch=[pltpu.VMEM((4, 8, LANES), F32), pltpu.VMEM((3, 8, D_MODEL), F32), _stage_shape(name)])


def _post_attn_fwd(o_a, o_g, l_g, gates, x, w_pa, w_pb, w_o, b_gate, gt1, g2, sc2, sh2, *, name):
    d = D_MODEL

    def body(i, n, rin, res, rout, aout, scr):
        oa_ref, o0, o1, o2, l0, l1, l2, gates_ref, x_ref = rin
        wpa_ref, wpb_ref, wo_ref, b_ref, gt_ref, g_ref, sc_ref, sh_ref = res
        ob_ref, merged_ref, mo_ref, x1_ref, h2_ref = rout
        (stage,) = scr
        ogs = [_natural(r, stage, 0) for r in (o0, o1, o2)]
        ws = _mix_weights([_natural(r, stage, 2) for r in (l0, l1, l2)])
        obb = (ws[0] * ogs[0] + ws[1] * ogs[1] + ws[2] * ogs[2]).astype(BF16)
        ob_ref[...] = obb
        pa = _dot(oa_ref[...].astype(BF16), wpa_ref[...])
        pb = _dot(obb, wpb_ref[...])
        ga = _sigmoid(gates_ref[:, :d].astype(F32) + b_ref[:, :d])
        gb = _sigmoid(gates_ref[:, d:].astype(F32) + b_ref[:, d:])
        merged = (ga * pa + gb * pb).astype(BF16)
        merged_ref[...] = merged
        mo = _dot(merged, wo_ref[...])
        mo_ref[...] = mo.astype(BF16)
        x1 = x_ref[...] + gt_ref[...] * mo
        x1_ref[...] = x1
        h2_ref[...] = _rms_mod(x1, g_ref[...], sc_ref[...], sh_ref[...]).astype(BF16)

    return _row_call(body, name=name, row_ins=[o_a, *o_g, *l_g, gates, x],
                     res_ins=[w_pa, w_pb, w_o, b_gate, gt1, g2, sc2, sh2],
                     row_outs=[(WB_OUT, BF16), (d, BF16), (d, BF16), (d, F32), (d, BF16)], scratch=[_stage_shape(name)])


def _ffn_fwd(h2, w_ffn_in, *, name):
    def body(i, n, rin, res, rout, aout, scr):
        (h_ref,), (w_ref,), (act_ref, ff_ref) = rin, res, rout
        hv = h_ref[...]
        for q in range(2):
            a = _dot(hv, w_ref[:, q * FF_CHIP:(q + 1) * FF_CHIP])
            up = _dot(hv, w_ref[:, D_FF + q * FF_CHIP:D_FF + (q + 1) * FF_CHIP])
            sl = slice(q * FF_CHIP, (q + 1) * FF_CHIP)
            act_ref[:, sl] = (a * _sigmoid(a) * up).astype(BF16)
            ff_ref[:, sl] = a.astype(BF16)
            ff_ref[:, D_FF + q * FF_CHIP:D_FF + (q + 1) * FF_CHIP] = up.astype(BF16)

    return _row_call(body, name=name, row_ins=[h2], res_ins=[w_ffn_in], row_outs=[(D_FF, BF16), (2 * D_FF, BF16)])


def _ffn_mid(act, ff, x1, tgt, w_ffn_out, gt2, *, name):
    d = D_MODEL

    def body(i, n, rin, res, rout, aout, scr):
        act_ref, ff_ref, x1_ref, tgt_ref = rin
        wo_ref, gt_ref = res
        dy_ref, dffo_ref, dff_ref = rout
        dgt_ref, loss_ref = aout
        (acc,) = scr

        @pl.when(i == 0)
        def _():
            acc[...] = jnp.zeros_like(acc)

        ffo = _dot(act_ref[...], wo_ref[...])
        gtv = gt_ref[...]
        e = x1_ref[...] + gtv * ffo - tgt_ref[...]
        dy = e * (1.0 / d)
        dy_ref[...] = dy
        dffo = (gtv * dy).astype(BF16)
        dffo_ref[...] = dffo
        acc[0] += _fold8(dy * ffo)
        acc[1] += _fold8(e * e)
        for q in range(2):
            sl = slice(q * FF_CHIP, (q + 1) * FF_CHIP)
            su = slice(D_FF + q * FF_CHIP, D_FF + (q + 1) * FF_CHIP)
            dact = _dot_nt(dffo, wo_ref[sl, :])
            a = ff_ref[:, sl].astype(F32)
            up = ff_ref[:, su].astype(F32)
            sg = _sigmoid(a)
            dff_ref[:, sl] = (dact * up * (sg * (1.0 + a * (1.0 - sg)))).astype(BF16)
            dff_ref[:, su] = (dact * (a * sg)).astype(BF16)

        @pl.when(i == n - 1)
        def _():
            dgt_ref[...] = jnp.sum(acc[0], axis=0, keepdims=True)
            tot = jnp.sum(jnp.sum(acc[1], axis=0, keepdims=True), axis=1, keepdims=True)
            loss_ref[...] = jnp.broadcast_to(tot * (0.5 / d), (1, LANES))

    return _row_call(body, name=name, row_ins=[act, ff, x1, tgt], res_ins=[w_ffn_out, gt2],
                     row_outs=[(d, F32), (d, BF16), (2 * D_FF, BF16)], acc_outs=[((1, d), F32), ((1, LANES), F32)],
                     scratch=[pltpu.VMEM((2, 8, d), F32)])


def _ffn_in_bwd(dff, x1, dy, mo, w_ffn_in, g2, sc2, gt1, *, name):
    d = D_MODEL

    def body(i, n, rin, res, rout, aout, scr):
        dff_ref, x1_ref, dy_ref, mo_ref = rin
        w_ref, g_ref, sc_ref, gt_ref = res
        dx1_ref, dmo_ref = rout
        (sums_ref,) = aout
        (acc,) = scr

        @pl.when(i == 0)
        def _():
            acc[...] = jnp.zeros_like(acc)

        dh = _dot_nt(dff_ref[...], w_ref[...])
        dx, t_sh, t_sc, t_g = _rms_mod_bwd(x1_ref[...], dh, g_ref[...], sc_ref[...])
        dx1 = dy_ref[...] + dx
        dx1_ref[...] = dx1
        dmo_ref[...] = (gt_ref[...] * dx1).astype(BF16)
        acc[0] += _fold8(t_sh)
        acc[1] += _fold8(t_sc)
        acc[2] += _fold8(t_g)
        acc[3] += _fold8(dx1 * mo_ref[...].astype(F32))

        @pl.when(i == n - 1)
        def _():
            sums_ref[...] = jnp.sum(acc[...], axis=1)

    return _row_call(body, name=name, row_ins=[dff, x1, dy, mo], res_ins=[w_ffn_in, g2, sc2, gt1],
                     row_outs=[(d, F32), (d, BF16)], acc_outs=[((4, d), F32)], scratch=[pltpu.VMEM((4, 8, d), F32)])


def _post_attn_bwd(dmo, gates, o_a, o_g, l_g, w_pa, w_pb, w_o, b_gate, *, name):
    d = D_MODEL

    def body(i, n, rin, res, rout, aout, scr):
        dmo_ref, gates_ref, oa_ref, o0, o1, o2, l0, l1, l2 = rin
        wpa_ref, wpb_ref, wo_ref, b_ref = res
        dpa_ref, dpb_ref, dgates_ref, doa_ref = rout[:4]
        do_refs, dl_refs = rout[4:7], rout[7:10]
        (dbg_ref,) = aout
        acc, stage = scr

        @pl.when(i == 0)
        def _():
            acc[...] = jnp.zeros_like(acc)

        ogs = [_natural(r, stage, 0) for r in (o0, o1, o2)]
        ws = _mix_weights([_natural(r, stage, 2) for r in (l0, l1, l2)])
        obb = (ws[0] * ogs[0] + ws[1] * ogs[1] + ws[2] * ogs[2]).astype(BF16)
        pa = _dot(oa_ref[...].astype(BF16), wpa_ref[...])
        pb = _dot(obb, wpb_ref[...])
        ga = _sigmoid(gates_ref[:, :d].astype(F32) + b_ref[:, :d])
        gb = _sigmoid(gates_ref[:, d:].astype(F32) + b_ref[:, d:])
        dm = _dot_nt(dmo_ref[...], wo_ref[...])
        dpa = (dm * ga).astype(BF16)
        dpb = (dm * gb).astype(BF16)
        dpa_ref[...] = dpa
        dpb_ref[...] = dpb
        dga = dm * pa * ga * (1.0 - ga)
        dgb = dm * pb * gb * (1.0 - gb)
        dgates_ref[:, :d] = dga.astype(BF16)
        dgates_ref[:, d:] = dgb.astype(BF16)
        acc[:, :d] += _fold8(dga)
        acc[:, d:] += _fold8(dgb)
        doa_ref[...] = _dot_nt(dpa, wpa_ref[...])
        dob = _dot_nt(dpb, wpb_ref[...])
        lo = _head_lanes()
        for ch in range(WB_OUT // LANES):
            sl = slice(ch * LANES, (ch + 1) * LANES)
            dv = dob[:, sl]
            wc = [w[:, sl] for w in ws]
            ts = [_head_mean(dv * og[:, sl], lo) * float(HEAD_DIM) for og in ogs]
            tbar = wc[0] * ts[0] + wc[1] * ts[1] + wc[2] * ts[2]
            for g in range(3):
                for k, (ref, val) in enumerate(((do_refs[g], wc[g] * dv), (dl_refs[g], wc[g] * (ts[g] - tbar)))):
                    if len(ref.shape) == 2:
                        ref[:, sl] = val
                    else:
                        _to_residue(val, ref, ch * LANES, stage, (2 * g + k) % 4)

        @pl.when(i == n - 1)
        def _():
            dbg_ref[...] = jnp.sum(acc[...], axis=0, keepdims=True)

    return _row_call(body, name=name, row_ins=[dmo, gates, o_a, *o_g, *l_g], res_ins=[w_pa, w_pb, w_o, b_gate],
                     row_outs=[(d, BF16), (d, BF16), (2 * d, BF16), (WA, F32)]
                     + 2 * [(WB_OUT, F32, dd) for _, dd in DIL_CONFIGS],
                     acc_outs=[((1, 2 * d), F32)], scratch=[pltpu.VMEM((8, 2 * d), F32), _stage_shape(name)])


def _na_class_tables():
    ro = np.full((3, NA_QROWS, 2 * NA_PAIRS), NA_RO_NONE, np.int64)
    slot = np.zeros((3, NA_QROWS, NA_PAIRS), np.int64)
    for t in range(3):
        for a in range(NA_QROWS):
            qr = _NA_CLASS_R0[t] + a
            rs = min(max(qr - NA_KH // 2, 0), ROWS - NA_KH)
            for b in range(NA_KROWS):
                kr = _NA_CLASS_K0[t] + b
                if rs <= kr < rs + NA_KH:
                    ro[t, a, b] = kr - qr + (NA_KH - 1)
            for j in range(NA_PAIRS):
                slot[t, a, j] = 2 * j - a + (_NA_CLASS_K0[t] - _NA_CLASS_R0[t] + NA_KH - 1) + (NA_QROWS - 1)
    assert slot.min() >= 0 and slot.max() < NA_SLOTS
    return ro, slot


def _na_build_bias(i, cls_ref, rp_ref, cm_ref, bias_scr):
    ro, _ = _na_class_tables()
    lo = _head_lanes()
    first = jnp.logical_or(i == 0, cls_ref[i] != cls_ref[jnp.maximum(i - 1, 0)])
    for t in range(3):
        @pl.when(jnp.logical_and(first, cls_ref[i] == t))
        def _():
            for hh in range(2):
                for a in range(NA_QROWS):
                    for j in range(NA_PAIRS):
                        r0, r1 = int(ro[t, a, 2 * j]), int(ro[t, a, 2 * j + 1])
                        x0 = jnp.broadcast_to(rp_ref[hh, r0:r0 + 1, :], (GRID_W, LANES))
                        x1 = jnp.broadcast_to(rp_ref[hh, r1:r1 + 1, :], (GRID_W, LANES))
                        blk = jnp.where(lo, pltpu.roll(x0, GRID_W + 1, 1, stride=1, stride_axis=0),
                                        pltpu.roll(x1, 1, 1, stride=1, stride_axis=0))
                        bias_scr[hh, a * GRID_W:(a + 1) * GRID_W, j * LANES:(j + 1) * LANES] = blk + cm_ref[...]
    return first


def _attn_fwd(qkv, qc0, kc0, vc0, npairs, table, kstart, cls, nk, *, name, na=None):
    s = qkv.shape[0]

    def body(ks_ref, cls_ref, q_ref, k_ref, v_ref, b_ref, *rest):
        if na:
            cm_ref, o_ref, lse_ref, bias_scr = rest
        else:
            o_ref, lse_ref = rest
        i = pl.program_id(1)
        if na:
            _na_build_bias(i, cls_ref, b_ref, cm_ref, bias_scr)
        ks = pl.multiple_of(ks_ref[i], 64)
        q2 = q_ref[...]
        k2 = k_ref[pl.ds(ks, nk), :]
        v2 = v_ref[pl.ds(ks, nk), :]
        lo = _head_lanes()
        outs, lses = [], []
        for h in range(2):
            qm = jnp.where(lo if h == 0 else jnp.logical_not(lo), q2, jnp.zeros_like(q2))
            sc = _dot_nt(qm, k2) * SCALE + (bias_scr[h, :, :nk] if na else b_ref[0, 0])
            m = jnp.max(sc, axis=1, keepdims=True)
            p = jnp.exp(sc - m)
            l = jnp.sum(p, axis=1, keepdims=True)
            pv = _dot(p.astype(BF16), v2)
            outs.append(pv / l)
            lses.append(m + jnp.log(l))
        o_ref[...] = jnp.where(lo, outs[0], outs[1])
        lse_ref[...] = jnp.where(lo, lses[0], lses[1])

    w = npairs * LANES
    in_specs = [
        pl.BlockSpec((Q_BLOCK, LANES), lambda p, i, ks, cl: (i, qc0 + p)),
        pl.BlockSpec((s, LANES), lambda p, i, ks, cl: (0, kc0 + p)),
        pl.BlockSpec((s, LANES), lambda p, i, ks, cl: (0, vc0 + p)),
    ]
    if na:
        in_specs += _na_bias_specs()
        args, scratch = (kstart, cls, qkv, qkv, qkv, *na), [pltpu.VMEM((2, Q_BLOCK, NA_W), F32)]
    else:
        in_specs.append(pl.BlockSpec((1, 1, Q_BLOCK, nk), lambda p, i, ks, cl: (cl[i], 0, 0, 0)))
        args, scratch = (kstart, cls, qkv, qkv, qkv, table), []
    grid_spec = pltpu.PrefetchScalarGridSpec(
        num_scalar_prefetch=2, grid=(npairs, N_QBLK), in_specs=in_specs,
        out_specs=[pl.BlockSpec((Q_BLOCK, LANES), lambda p, i, ks, cl: (i, p)),
                   pl.BlockSpec((Q_BLOCK, LANES), lambda p, i, ks, cl: (i, p))],
        scratch_shapes=scratch,
    )
    return _pcall(body, name=name, grid_spec=grid_spec,
                  out_shape=[jax.ShapeDtypeStruct((s, w), F32), jax.ShapeDtypeStruct((s, w), F32)],
                  compiler_params=pltpu.CompilerParams(dimension_semantics=("parallel", "arbitrary")),
                  )(*args)


def _na_bias_specs():
    return [pl.BlockSpec((2, 16, LANES), lambda p, i, ks, cl: (p, 0, 0)),
            pl.BlockSpec((GRID_W, LANES), lambda p, i, ks, cl: (0, 0))]


def _attn_bwd(qkv, qc0, kc0, vc0, npairs, table, kstart, cls, nk, do, o, lse, *, name, dlse=None, na=None):
    s = qkv.shape[0]
    has_dlse = dlse is not None
    _, slot = _na_class_tables()

    def body(ks_ref, cls_ref, q_ref, k_ref, v_ref, b_ref, *rest):
        if na:
            cm_ref, rest = rest[0], rest[1:]
        do_ref, o_ref, lse_ref, rest = rest[0], rest[1], rest[2], rest[3:]
        if has_dlse:
            dlse_ref, rest = rest[0], rest[1:]
        dq_ref, dk_ref, dv_ref = rest[0], rest[1], rest[2]
        if na:
            bank_ref, bias_scr, dbias_scr, bank_scr = rest[3:]
        i = pl.program_id(1)
        if na:
            first = _na_build_bias(i, cls_ref, b_ref, cm_ref, bias_scr)

        @pl.when(i == 0)
        def _():
            dk_ref[...] = jnp.zeros_like(dk_ref)
            dv_ref[...] = jnp.zeros_like(dv_ref)
            if na:
                dbias_scr[...] = jnp.zeros_like(dbias_scr)
                bank_scr[...] = jnp.zeros_like(bank_scr)

        ks = pl.multiple_of(ks_ref[i], 64)
        q2 = q_ref[...]
        k2 = k_ref[pl.ds(ks, nk), :]
        v2 = v_ref[pl.ds(ks, nk), :]
        do2 = do_ref[...]
        lse2 = lse_ref[...]
        doo = do2 * o_ref[...]
        do2b = do2.astype(BF16)
        lo = _head_lanes()
        lane = lax.broadcasted_iota(jnp.int32, (1, LANES), 1)
        dqs, dks, dvs = [], [], []
        for h in range(2):
            mh = lo if h == 0 else jnp.logical_not(lo)
            qm = jnp.where(mh, q2, jnp.zeros_like(q2))
            sc = _dot_nt(qm, k2) * SCALE + (bias_scr[h, :, :nk] if na else b_ref[0, 0])
            lse_h = jnp.max(jnp.where(mh, lse2, NEG), axis=1, keepdims=True)
            p = jnp.exp(sc - lse_h)
            delta = jnp.sum(jnp.where(mh, doo, 0.0), axis=1, keepdims=True)
            dom = jnp.where(mh, do2b, jnp.zeros_like(do2b))
            dp = _dot_nt(dom, v2)
            t = dp - delta
            if has_dlse:
                t = t + jnp.sum(jnp.where(lane == h * HEAD_DIM, dlse_ref[...], 0.0), axis=1, keepdims=True)
            ds = p * t
            if na:
                @pl.when(first)
                def _():
                    dbias_scr[h, :, :nk] = ds

                @pl.when(jnp.logical_not(first))
                def _():
                    dbias_scr[h, :, :nk] += ds
            dsb = ds.astype(BF16)
            dqs.append(_dot(dsb, k2))
            dks.append(lax.dot_general(dsb, q2, _TN, preferred_element_type=F32))
            dvs.append(lax.dot_general(p.astype(BF16), do2b, _TN, preferred_element_type=F32))
        dq_ref[...] = jnp.where(lo, dqs[0], dqs[1]) * SCALE
        dk_ref[pl.ds(ks, nk), :] += jnp.where(lo, dks[0], dks[1]) * SCALE
        dv_ref[pl.ds(ks, nk), :] += jnp.where(lo, dvs[0], dvs[1])
        if na:
            last = jnp.logical_or(i == N_QBLK - 1, cls_ref[i] != cls_ref[jnp.minimum(i + 1, N_QBLK - 1)])
            for t in range(3):
                @pl.when(jnp.logical_and(last, cls_ref[i] == t))
                def _():
                    for hh in range(2):
                        for a in range(NA_QROWS):
                            for j in range(NA_PAIRS):
                                bank_scr[hh, int(slot[t, a, j])] += dbias_scr[
                                    hh, a * GRID_W:(a + 1) * GRID_W, j * LANES:(j + 1) * LANES]

            @pl.when(i == N_QBLK - 1)
            def _():
                bank_ref[...] = bank_scr[...]

    w = npairs * LANES
    blk = lambda: pl.BlockSpec((Q_BLOCK, LANES), lambda p, i, ks, cl: (i, p))
    full = lambda: pl.BlockSpec((s, LANES), lambda p, i, ks, cl: (0, p))
    in_specs = [
        pl.BlockSpec((Q_BLOCK, LANES), lambda p, i, ks, cl: (i, qc0 + p)),
        pl.BlockSpec((s, LANES), lambda p, i, ks, cl: (0, kc0 + p)),
        pl.BlockSpec((s, LANES), lambda p, i, ks, cl: (0, vc0 + p)),
    ]
    if na:
        in_specs += _na_bias_specs()
        args = [kstart, cls, qkv, qkv, qkv, *na]
    else:
        in_specs.append(pl.BlockSpec((1, 1, Q_BLOCK, nk), lambda p, i, ks, cl: (cl[i], 0, 0, 0)))
        args = [kstart, cls, qkv, qkv, qkv, table]
    in_specs += [blk(), blk(), blk()]
    args += [do, o, lse]
    if has_dlse:
        in_specs.append(blk())
        args.append(dlse)
    out_specs = [blk(), full(), full()]
    out_shape = [jax.ShapeDtypeStruct((s, w), F32)] * 3
    scratch = []
    if na:
        bank_shape = (2, NA_SLOTS, GRID_W, LANES)
        out_specs.append(pl.BlockSpec(bank_shape, lambda p, i, ks, cl: (p, 0, 0, 0)))
        out_shape.append(jax.ShapeDtypeStruct((2 * npairs,) + bank_shape[1:], F32))
        scratch = [pltpu.VMEM((2, Q_BLOCK, NA_W), F32), pltpu.VMEM((2, Q_BLOCK, NA_W), F32), pltpu.VMEM(bank_shape, F32)]
    grid_spec = pltpu.PrefetchScalarGridSpec(num_scalar_prefetch=2, grid=(npairs, N_QBLK), in_specs=in_specs,
                                             out_specs=out_specs, scratch_shapes=scratch)
    return _pcall(body, name=name, grid_spec=grid_spec, out_shape=out_shape,
                  compiler_params=pltpu.CompilerParams(dimension_semantics=("arbitrary", "arbitrary")))(*args)


_NA_CLASS_R0 = (0, NA_QROWS, ROWS - NA_QROWS)
_NA_CLASS_K0 = (0, 0, ROWS - NA_KROWS)
_RPB_RO = 2 * NA_KH - 1
_RPB_CO = 2 * NA_KW - 1
_BANK_ROWS = 48


def _na_constants():
    col = np.arange(GRID_W)
    cs = np.clip(col - NA_KW // 2, 0, GRID_W - NA_KW)
    vcol = (col[None, :] >= cs[:, None]) & (col[None, :] < cs[:, None] + NA_KW)
    colmask = np.where(np.concatenate([vcol, vcol], axis=1), 0.0, NEG).astype(np.float32)
    co = col[None, :] - col[:, None] + (NA_KW - 1)
    oh_col = np.zeros((GRID_W * GRID_W, LANES), np.float32)
    for qc in range(GRID_W):
        for kc in range(GRID_W):
            if vcol[qc, kc]:
                oh_col[qc * GRID_W + kc, co[qc, kc]] = 1.0
    ks = np.clip(np.arange(N_QBLK) * NA_QROWS - NA_KH // 2, 0, ROWS - NA_KROWS) * GRID_W
    cls = np.ones(N_QBLK, np.int32)
    cls[0], cls[-1] = 0, 2
    return colmask, oh_col, ks.astype(np.int32), cls


def _bank_reduce(bank, oh_col, *, name):
    def body(d_ref, ohc_ref, o_ref):
        o_ref[0] = jnp.dot(d_ref[0], ohc_ref[...], preferred_element_type=F32, precision=lax.Precision.HIGHEST)

    return _pcall(
        body, name=name, grid=(NA_HEADS,),
        in_specs=[pl.BlockSpec((1, _BANK_ROWS, GRID_W * GRID_W), lambda h: (h, 0, 0)),
                  pl.BlockSpec((GRID_W * GRID_W, LANES), lambda h: (0, 0))],
        out_specs=pl.BlockSpec((1, _BANK_ROWS, LANES), lambda h: (h, 0, 0)),
        out_shape=jax.ShapeDtypeStruct((NA_HEADS, _BANK_ROWS, LANES), F32), compiler_params=_PAR,
    )(bank, oh_col)


def _dil_constants(dilation):
    seg = SEQ // dilation
    nb = seg // Q_BLOCK
    shift = (0, -DIL_HALF, -2 * DIL_HALF)
    qi = np.arange(Q_BLOCK)[:, None]
    ki = np.arange(DIL_NK)[None, :]
    mask = np.stack([np.where(np.abs(ki + sh - qi) <= DIL_HALF, 0.0, NEG) for sh in shift]).astype(np.float32)
    ks, cls = [], []
    for i in range(N_QBLK):
        sub, blk = divmod(i, nb)
        t = 0 if blk == 0 else (2 if blk == nb - 1 else 1)
        cls.append(t)
        ks.append(sub * seg + blk * Q_BLOCK + shift[t])
    return mask.reshape(3, 1, Q_BLOCK, DIL_NK), np.asarray(ks, np.int32), np.asarray(cls, np.int32)


_VM = pl.BlockSpec(memory_space=pltpu.VMEM)


def _ada_fwd(c_all, w, b, *, name):
    def body(c_ref, w_ref, b_ref, o_ref):
        cv = c_ref[...]
        o_ref[...] = jnp.dot(cv * _sigmoid(cv), w_ref[...], preferred_element_type=F32,
                             precision=lax.Precision.HIGHEST) + b_ref[...]

    return _pcall(body, name=name, in_specs=[_VM, _VM, _VM], out_specs=_VM,
                  out_shape=jax.ShapeDtypeStruct((c_all.shape[0], w.shape[1]), F32))(c_all, w, b)


def _ada_bwd(c_all_t, dmod, *, name):
    def body(c_ref, d_ref, o_ref):
        cv = c_ref[...]
        o_ref[...] = jnp.dot(cv * _sigmoid(cv), d_ref[...], preferred_element_type=F32,
                             precision=lax.Precision.HIGHEST)

    return _pcall(body, name=name, in_specs=[_VM, _VM], out_specs=_VM,
                  out_shape=jax.ShapeDtypeStruct((c_all_t.shape[0], dmod.shape[1]), F32))(c_all_t, dmod)


def _row_sum(t, *, name):
    def body(t_ref, o_ref):
        o_ref[...] = jnp.sum(t_ref[...], axis=0, keepdims=True)

    return _pcall(body, name=name, in_specs=[_VM], out_specs=_VM,
                  out_shape=jax.ShapeDtypeStruct((1, t.shape[1]), F32))(t)


def _row_tile(rows):
    tr = rows
    for cand in range(8, 513, 8):
        if rows % cand == 0:
            tr = cand
    return tr


def _adamw_math(wv, gv, mv, vv):
    nm = ADAM_B1 * mv + (1.0 - ADAM_B1) * gv
    nv = ADAM_B2 * vv + (1.0 - ADAM_B2) * (gv * gv)
    m_hat = nm / (1.0 - ADAM_B1 ** ADAM_STEP)
    v_hat = nv / (1.0 - ADAM_B2 ** ADAM_STEP)
    return -ADAM_LR * (m_hat / (jnp.sqrt(v_hat) + ADAM_EPS) + ADAM_WD * wv), nm, nv


def _adamw(w, g, m, v, *, name):
    rows, cols = w.shape
    tr = _row_tile(rows)

    def body(w_ref, g_ref, m_ref, v_ref, d_ref, nm_ref, nv_ref):
        d_ref[...], nm_ref[...], nv_ref[...] = _adamw_math(w_ref[...], g_ref[...], m_ref[...], v_ref[...])

    spec = pl.BlockSpec((tr, cols), lambda i: (i, 0))
    return _pcall(body, name=name, grid=(rows // tr,), in_specs=[spec] * 4, out_specs=[spec] * 3,
                  out_shape=[jax.ShapeDtypeStruct((rows, cols), F32)] * 3, compiler_params=_PAR)(w, g, m, v)


def _adamw_halves(w, g_mine, g_other, m, v, c_arr, *, name):
    rows, cols = w.shape
    hr = rows // 2
    tr = _row_tile(hr)
    nt = hr // tr

    def body(c_ref, w_ref, t_ref, o_ref, m_ref, v_ref, g_ref, d_ref, nm_ref, nv_ref):
        gv = jnp.where(pl.program_id(0) == c_ref[0], t_ref[...], o_ref[...])
        g_ref[...] = gv
        d_ref[...], nm_ref[...], nv_ref[...] = _adamw_math(w_ref[...], gv, m_ref[...], v_ref[...])

    full = pl.BlockSpec((tr, cols), lambda h, i, c: (h * nt + i, 0))
    half = pl.BlockSpec((tr, cols), lambda h, i, c: (i, 0))
    grid_spec = pltpu.PrefetchScalarGridSpec(num_scalar_prefetch=1, grid=(2, nt),
                                             in_specs=[full, half, half, full, full], out_specs=[full] * 4)
    return _pcall(body, name=name, grid_spec=grid_spec, out_shape=[jax.ShapeDtypeStruct((rows, cols), F32)] * 4,
                  compiler_params=pltpu.CompilerParams(dimension_semantics=("parallel", "parallel")),
                  )(c_arr, w, g_mine, g_other, m, v)


_MESH = pl.DeviceIdType.MESH
_ANY = pl.BlockSpec(memory_space=pl.ANY)
_CHIP_FLIPS = ((1, 0), (0, 1), (1, 1))


def _pos():
    return lax.axis_index("x"), lax.axis_index("y"), lax.axis_index("c")


def _flip(v, f):
    return 1 - v if f else v


def _sem_pairs(n):
    return [pltpu.SemaphoreType.DMA((n,)), pltpu.SemaphoreType.DMA((n,))]


def _small_allgather(blk, *, name):
    m_per, n = blk.shape

    def body(x_ref, out_ref, send_sems, recv_sems, local_sem):
        x, y, c = _pos()
        me, sibling = (x, y, c), (x, y, 1 - c)
        chips = [(_flip(x, fx), _flip(y, fy)) for fx, fy in _CHIP_FLIPS]

        def rows(px, py, pc):
            return out_ref.at[pl.ds((4 * px + 2 * py + pc) * m_per, m_per), :]

        def copy(k, block, to, src=None):
            return pltpu.make_async_remote_copy(
                src_ref=rows(*block) if src is None else src, dst_ref=rows(*block),
                send_sem=send_sems.at[k], recv_sem=recv_sems.at[k], device_id=to, device_id_type=_MESH)

        mine = pltpu.make_async_copy(x_ref, rows(*me), local_sem)
        mine.start()
        first = [copy(0, me, sibling, src=x_ref)]
        first += [copy(1 + j, me, (*chip, c), src=x_ref) for j, chip in enumerate(chips)]
        for cp in first:
            cp.start()
        passed = [copy(4 + j, (*chip, c), sibling) for j, chip in enumerate(chips)]
        for j, chip in enumerate(chips):
            copy(1 + j, (*chip, c), me).wait_recv()
            passed[j].start()
        copy(0, sibling, me).wait_recv()
        for j, chip in enumerate(chips):
            copy(4 + j, (*chip, 1 - c), me).wait_recv()
        for cp in first + passed:
            cp.wait_send()
        mine.wait()

    return _pcall(
        body, name=name, out_shape=jax.ShapeDtypeStruct((N_DEV * m_per, n), blk.dtype),
        in_specs=[_VM], out_specs=_VM,
        scratch_shapes=_sem_pairs(7) + [pltpu.SemaphoreType.DMA],
    )(blk)


_HBM = pl.BlockSpec(memory_space=pltpu.HBM)
_SEM = pl.BlockSpec(memory_space=pltpu.SEMAPHORE)
_EFFECT = pltpu.SideEffectType.DATAFLOW_SIDE_EFFECTING


def _split_start(srcs, lands, plan, ncopies, after, *, name):
    ns, nl = len(srcs), len(lands)

    def body(*refs):
        src_refs, land_refs = refs[:ns], refs[ns:ns + nl]
        send_sems, recv_sems = refs[ns + nl + 1], refs[ns + nl + 2]
        token = refs[-1]
        x, y, c = _pos()
        for k, (src, dst, to, _) in enumerate(plan(x, y, c, src_refs, land_refs)):
            pltpu.make_async_remote_copy(src_ref=src, dst_ref=dst, send_sem=send_sems.at[k], recv_sem=recv_sems.at[k],
                                         device_id=to, device_id_type=_MESH).start()
        token[...] = jnp.zeros_like(token)

    hbm = lambda a: pltpu.HBM(a.shape, a.dtype)
    out = _pcall(
        body, name=name,
        out_shape=(pltpu.SemaphoreType.DMA((ncopies,)), pltpu.SemaphoreType.DMA((ncopies,)),
                   *[hbm(a) for a in srcs], *[hbm(a) for a in lands], jax.ShapeDtypeStruct((8, LANES), F32)),
        in_specs=[_HBM] * (ns + nl) + [_ANY], out_specs=(_SEM, _SEM, *[_HBM] * (ns + nl), _VM),
        input_output_aliases={i: 2 + i for i in range(ns + nl)},
        compiler_params=pltpu.CompilerParams(has_side_effects=_EFFECT),
    )(*[pltpu.with_memory_space_constraint(a, pltpu.HBM) for a in (*srcs, *lands)], after)
    return out[0], out[1], list(out[2:2 + ns]), list(out[2 + ns:2 + ns + nl]), out[-1]


def _split_wait(send_sems, recv_sems, srcs, lands, plan, after, *, name):
    ns, nl = len(srcs), len(lands)

    def body(*refs):
        src_refs, land_refs = refs[:ns], refs[ns:ns + nl]
        send_sems, recv_sems = refs[ns + nl], refs[ns + nl + 1]
        x, y, c = _pos()
        for k, (src, _, _, mine) in enumerate(plan(x, y, c, src_refs, land_refs)):
            cp = pltpu.make_async_remote_copy(src_ref=src, dst_ref=mine, send_sem=send_sems.at[k],
                                              recv_sem=recv_sems.at[k], device_id=(x, y, c), device_id_type=_MESH)
            cp.wait_send()
            cp.wait_recv()

    hbm = lambda a: pltpu.HBM(a.shape, a.dtype)
    out = _pcall(
        body, name=name, out_shape=tuple(hbm(a) for a in (*srcs, *lands)),
        in_specs=[_HBM] * (ns + nl) + [_SEM, _SEM, _ANY], out_specs=tuple([_HBM] * (ns + nl)),
        input_output_aliases={i: i for i in range(ns + nl)},
        compiler_params=pltpu.CompilerParams(has_side_effects=_EFFECT),
    )(*srcs, *lands, send_sems, recv_sems, after)
    return list(out[ns:])


def _ag_plan(nw):
    def plan(x, y, c, sh_refs, full_refs):
        j = 2 * x + y
        out = []
        for wi in range(nw):
            for fx, fy in _CHIP_FLIPS:
                px, py = _flip(x, fx), _flip(y, fy)
                out.append((sh_refs[wi].at[c], full_refs[wi].at[j, c], (px, py, c), full_refs[wi].at[2 * px + py, c]))
            out.append((sh_refs[wi], full_refs[wi].at[j], (x, y, 1 - c), full_refs[wi].at[j]))
        return out
    return plan


def _ag_pass(fulls, *, name):
    nw = len(fulls)

    def body(*refs):
        in_refs, out_refs = refs[:nw], refs[nw:2 * nw]
        send_sems, recv_sems = refs[2 * nw:]
        x, y, c = _pos()
        cps = []
        for wi in range(nw):
            for k, (fx, fy) in enumerate(_CHIP_FLIPS):
                jp = 2 * _flip(x, fx) + _flip(y, fy)
                sems = dict(send_sem=send_sems.at[3 * wi + k], recv_sem=recv_sems.at[3 * wi + k], device_id_type=_MESH)
                send = pltpu.make_async_remote_copy(src_ref=in_refs[wi].at[jp, c], dst_ref=out_refs[wi].at[jp, c],
                                                    device_id=(x, y, 1 - c), **sems)
                recv = pltpu.make_async_remote_copy(src_ref=in_refs[wi].at[jp, c], dst_ref=out_refs[wi].at[jp, 1 - c],
                                                    device_id=(x, y, c), **sems)
                cps.append((send, recv))
        for send, _ in cps:
            send.start()
        for send, recv in cps:
            send.wait_send()
            recv.wait_recv()

    return _pcall(body, name=name, out_shape=[jax.ShapeDtypeStruct(f.shape, f.dtype) for f in fulls],
                  in_specs=[_ANY] * nw, out_specs=[_ANY] * nw, input_output_aliases={i: i for i in range(nw)},
                  scratch_shapes=_sem_pairs(3 * nw))(*fulls)


def _rs_plan(nw):
    def plan(x, y, c, s_refs, rb_refs):
        out = []
        for wi in range(nw):
            for k, (fx, fy) in enumerate(_CHIP_FLIPS):
                px, py = _flip(x, fx), _flip(y, fy)
                out.append((s_refs[wi].at[2 * px + py], rb_refs[wi].at[k], (px, py, c), rb_refs[wi].at[k]))
        return out
    return plan


def _sibling_send_halves(gs, *, name):
    nw = len(gs)

    def body(*refs):
        g_refs, out_refs = refs[:nw], refs[nw:2 * nw]
        send_sems, recv_sems = refs[2 * nw:]
        x, y, c = _pos()
        cps = [pltpu.make_async_remote_copy(src_ref=g_refs[wi].at[k, 1 - c], dst_ref=out_refs[wi].at[k],
                                            send_sem=send_sems.at[4 * wi + k], recv_sem=recv_sems.at[4 * wi + k],
                                            device_id=(x, y, 1 - c), device_id_type=_MESH)
               for wi in range(nw) for k in range(N_CHIP)]
        for cp in cps:
            cp.start()
        for cp in cps:
            cp.wait()

    return _pcall(body, name=name,
                  out_shape=[jax.ShapeDtypeStruct((g.shape[0],) + g.shape[2:], g.dtype) for g in gs],
                  in_specs=[_ANY] * nw, out_specs=[_ANY] * nw, scratch_shapes=_sem_pairs(N_CHIP * nw))(*gs)


def _sibling_swap(ts, *, name):
    nw = len(ts)

    def body(*refs):
        t_refs, out_refs = refs[:nw], refs[nw:2 * nw]
        send_sems, recv_sems = refs[2 * nw:]
        x, y, c = _pos()
        cps = [pltpu.make_async_remote_copy(src_ref=t_refs[wi], dst_ref=out_refs[wi], send_sem=send_sems.at[wi],
                                            recv_sem=recv_sems.at[wi], device_id=(x, y, 1 - c), device_id_type=_MESH)
               for wi in range(nw)]
        for cp in cps:
            cp.start()
        for cp in cps:
            cp.wait()

    return _pcall(body, name=name, out_shape=[jax.ShapeDtypeStruct(t.shape, t.dtype) for t in ts],
                  in_specs=[_ANY] * nw, out_specs=[_ANY] * nw, scratch_shapes=_sem_pairs(nw))(*ts)


def _rs_add(g, ra, c_arr, *, name):
    n, _, r, w = g.shape

    def body(c_ref, g_ref, ra_ref, s_ref, sb_ref):
        t = g_ref[...] + ra_ref[...]
        s_ref[...] = t
        sb_ref[...] = t.astype(BF16)

    grid_spec = pltpu.PrefetchScalarGridSpec(
        num_scalar_prefetch=1, grid=(n,),
        in_specs=[pl.BlockSpec((None, None, r, w), lambda k, c: (k, c[0], 0, 0)),
                  pl.BlockSpec((None, r, w), lambda k, c: (k, 0, 0))],
        out_specs=[pl.BlockSpec((None, r, w), lambda k, c: (k, 0, 0))] * 2)
    return _pcall(body, name=name, grid_spec=grid_spec,
                  out_shape=[jax.ShapeDtypeStruct((n, r, w), F32), jax.ShapeDtypeStruct((n, r, w), BF16)],
                  compiler_params=_PAR)(c_arr, g, ra)


def _rs_final(s, rb, j_arr, *, name):
    _, r, w = s.shape

    def body(j_ref, s_ref, rb_ref, t_ref):
        t_ref[...] = ((s_ref[...] + rb_ref[0].astype(F32)) + rb_ref[1].astype(F32)) + rb_ref[2].astype(F32)

    grid_spec = pltpu.PrefetchScalarGridSpec(
        num_scalar_prefetch=1, grid=(1,),
        in_specs=[pl.BlockSpec((None, r, w), lambda i, j: (j[0], 0, 0)),
                  pl.BlockSpec((3, r, w), lambda i, j: (0, 0, 0))],
        out_specs=pl.BlockSpec((r, w), lambda i, j: (0, 0)))
    return _pcall(body, name=name, grid_spec=grid_spec, out_shape=jax.ShapeDtypeStruct((r, w), F32),
                  compiler_params=_ARB)(j_arr, s, rb)


def _tile2(g):
    return jnp.concatenate([g, g], axis=1)


_BIG = ("w_in", "w_ffn_in", "w_ffn_out", "w_o", "w_proj_a", "w_proj_b")
_BIG_SHARD = {"w_in": (1024, 1472), "w_ffn_in": (1024, 1408), "w_ffn_out": (704, 1024), "w_o": (256, 1024),
              "w_proj_a": (512, 256), "w_proj_b": (256, 256)}


def _device_step(x2, tgt, mod, first_weights, late_weights, early_grads, g_norm1, g_norm2, b_gate, g_qa, g_ka, g_qb,
                 g_kb, rpb):
    d = D_MODEL
    sh1, sc1, gt1, sh2, sc2, gt2 = [mod[:, k * d:(k + 1) * d] for k in range(6)]

    colmask, oh_col, na_ks, na_cls = _na_constants()
    rp = jnp.pad(rpb, ((0, 0), (0, 16 - _RPB_RO), (RP_LANE0, LANES - RP_LANE0 - _RPB_CO)), constant_values=NEG)
    na = (rp, jnp.asarray(colmask))
    na_ks, na_cls = jnp.asarray(na_ks), jnp.asarray(na_cls)
    dil = [_dil_constants(dd) for _, dd in DIL_CONFIGS]
    tab_d = jnp.asarray(dil[0][0])
    gains = jnp.concatenate([_tile2(g_qa), _tile2(g_ka), _tile2(g_qb), _tile2(g_kb)], axis=0)
    cos_t, sa_t, sb_t = _rope_tables()

    wts = first_weights(cos_t)
    h1, qkvn, qk_pre, gates, *qkv_dil = _pre_attn_fwd(x2, cos_t, sa_t, sb_t, g_norm1, sc1, sh1, wts["w_qkv"],
                                                      wts["w_gates"], gains, name="pre_attn_fwd")
    o_a, lse_a = _attn_fwd(qkvn, 0, 4, 8, 4, None, na_ks, na_cls, NA_NK, name="attn_a_fwd", na=na)
    arrs, o_g, l_g = [], [], []
    res = lambda t, dd: t if dd == 1 else (t, dd)
    for g, (_, dd) in enumerate(DIL_CONFIGS):
        ks_g, cls_g = jnp.asarray(dil[g][1]), jnp.asarray(dil[g][2])
        arr, cb = (qkvn, (12, 18, 24)) if dd == 1 else (qkv_dil.pop(0), (0, 2, 4))
        op, lp = _attn_fwd(arr, cb[0], cb[1], cb[2], 2, tab_d, ks_g, cls_g, DIL_NK, name=f"attn_d{g}_fwd")
        arrs.append((arr, cb, ks_g, cls_g))
        o_g.append(res(op, dd))
        l_g.append(res(lp, dd))
    wts = dict(wts, **late_weights(o_a))
    o_b, merged, mo, x1, h2 = _post_attn_fwd(o_a, o_g, l_g, gates, x2, wts["w_pa"], wts["w_pb"], wts["w_o"], b_gate,
                                             gt1, g_norm2, sc2, sh2, name="post_attn_fwd")
    act, ff = _ffn_fwd(h2, wts["w_ffn_in"], name="ffn_fwd")

    dy, dffo, dff, dgt2, loss_v = _ffn_mid(act, ff, x1, tgt, wts["w_ffn_out"], gt2, name="ffn_mid")
    grads = {}
    g_ffn_out = _wgrad(act, dffo, name="wg_ffn_out", tm=D_FF // 2, tn=d)
    grads["w_ffn_out"] = g_ffn_out.reshape(N_CHIP, D_FF // N_CHIP, d)
    grads["w_ffn_in"] = _wgrad(h2, dff, name="wg_ffn_in", tm=512, tn=2 * FF_CHIP, chips=N_CHIP)
    dx1, dmo, sums2 = _ffn_in_bwd(dff, x1, dy, mo, wts["w_ffn_in"], g_norm2, sc2, gt1, name="ffn_in_bwd")
    grads["w_o"] = _wgrad(merged, dmo, name="wg_o", tm=d, tn=d).reshape(N_CHIP, d // N_CHIP, d)
    pab = _post_attn_bwd(dmo, gates, o_a, o_g, l_g, wts["w_pa"], wts["w_pb"], wts["w_o"], b_gate, name="post_attn_bwd")
    dpa, dpb, dgates, do_a = pab[:4]
    do_g, dl_g, dbg = pab[4:7], pab[7:10], pab[10]
    g_pa = _wgrad(o_a, dpa, name="wg_pa", tm=WA, tn=d)
    g_pb = _wgrad(o_b, dpb, name="wg_pb", tm=WB_OUT, tn=d)
    grads["w_proj_a"] = g_pa.reshape(WA, N_CHIP, d // N_CHIP).transpose(1, 0, 2)
    grads["w_proj_b"] = g_pb.reshape(WB_OUT, N_CHIP, d // N_CHIP).transpose(1, 0, 2)
    order = early_grads(grads)
    if order is not None:
        tab_d, na = tab_d + order[0, 0], (na[0], na[1] + order[0, 0])
    dqa, dka, dva, bank = _attn_bwd(qkvn, 0, 4, 8, 4, None, na_ks, na_cls, NA_NK, do_a, o_a, lse_a,
                                    name="attn_a_bwd", na=na)
    dqs, dks, dvs = [], [], []
    for g, (_, dd) in enumerate(DIL_CONFIGS):
        arr, cb, ks_g, cls_g = arrs[g]
        plain = lambda t: t[0] if isinstance(t, tuple) else t
        dq, dk, dv = _attn_bwd(arr, cb[0], cb[1], cb[2], 2, tab_d, ks_g, cls_g, DIL_NK, do_g[g], plain(o_g[g]),
                               plain(l_g[g]), name=f"attn_d{g}_bwd", dlse=dl_g[g])
        dqs.append(res(dq, dd))
        dks.append(res(dk, dd))
        dvs.append(res(dv, dd))
    dqkv, grad_x, dgains, sums1 = _pre_attn_bwd(qk_pre, [dqa, dka, dva] + dqs + dks + dvs, dgates, x2, dx1, cos_t, sa_t,
                                                sb_t, wts["w_qkv"], wts["w_gates"], gains, g_norm1, sc1,
                                                name="pre_attn_bwd")
    g_qkv = _wgrad(h1, dqkv, name="wg_qkv", tm=d, tn=W_QKV // 2)
    g_gates = _wgrad(h1, dgates, name="wg_gates", tm=d, tn=W_GATES)
    nc, cut = _BIG_SHARD["w_in"][1], 3 * _BIG_SHARD["w_in"][1] - W_QKV
    grads["w_in"] = jnp.stack([g_qkv[:, :nc], g_qkv[:, nc:2 * nc],
                               jnp.concatenate([g_qkv[:, 2 * nc:], g_gates[:, :cut]], axis=1), g_gates[:, cut:]])

    bank = bank.reshape(NA_HEADS, NA_SLOTS, GRID_W, 2, GRID_W).transpose(0, 1, 3, 2, 4)
    bank = jnp.pad(bank.reshape(NA_HEADS, 2 * NA_SLOTS, GRID_W * GRID_W), ((0, 0), (0, _BANK_ROWS - 2 * NA_SLOTS), (0, 0)))
    g2 = _bank_reduce(bank, jnp.asarray(oh_col), name="rpb_reduce")[:, :2 * NA_SLOTS].reshape(NA_HEADS, NA_SLOTS, 2, LANES)
    g_rpb = g2[:, 3:3 + _RPB_RO, 0, :_RPB_CO] + g2[:, 2:2 + _RPB_RO, 1, :_RPB_CO]

    dmod = jnp.concatenate([sums1[0:1], sums1[1:2], sums2[3:4], sums2[0:1], sums2[1:2], dgt2], axis=1)
    small = dict(g_norm1=sums1[2:3], g_norm2=sums2[2:3], b_gate=dbg, g_qa=dgains[0:1, :HEAD_DIM],
                 g_ka=dgains[1:2, :HEAD_DIM], g_qb=dgains[2:3, :HEAD_DIM], g_kb=dgains[3:4, :HEAD_DIM], rpb=g_rpb)
    return loss_v, grad_x, grads, dmod, small


_SMALL = ("b_ada", "g_norm1", "g_norm2", "b_gate", "g_qa", "g_ka", "g_qb", "g_kb", "rpb")
_SMALL_N = {"b_ada": 6 * D_MODEL, "g_norm1": D_MODEL, "g_norm2": D_MODEL, "b_gate": 2 * D_MODEL, "g_qa": HEAD_DIM,
            "g_ka": HEAD_DIM, "g_qb": HEAD_DIM, "g_kb": HEAD_DIM, "rpb": NA_HEADS * _RPB_RO * _RPB_CO}


def _pack_small(parts):
    flat = [parts[n].reshape(1, _SMALL_N[n]) for n in _SMALL]
    used = sum(_SMALL_N.values())
    return jnp.concatenate(flat + [jnp.zeros((1, STATS_W - used), F32)], axis=1)


def _unpack_small(v, shapes):
    out, at = {}, 0
    for n in _SMALL:
        out[n] = v[:, at:at + _SMALL_N[n]].reshape(shapes[n])
        at += _SMALL_N[n]
    return out


def _join_cols(t):
    _, r, c = t.shape
    return t.transpose(1, 0, 2).reshape(r, N_CHIP * c)


def kernel(x, c, w_ada, b_ada, g_norm1, g_norm2, w_in, b_gate, g_qa, g_ka, g_qb, g_kb, rpb, w_proj_a, w_proj_b, w_o, w_ffn_in, w_ffn_out, loss_target, m_w_ada, m_b_ada, m_g_norm1, m_g_norm2, m_w_in, m_b_gate, m_g_qa, m_g_ka, m_g_qb, m_g_kb, m_rpb, m_w_proj_a, m_w_proj_b, m_w_o, m_w_ffn_in, m_w_ffn_out, v_w_ada, v_b_ada, v_g_norm1, v_g_norm2, v_w_in, v_b_gate, v_g_qa, v_g_ka, v_g_qb, v_g_kb, v_rpb, v_w_proj_a, v_w_proj_b, v_w_o, v_w_ffn_in, v_w_ffn_out):
    names = ("w_ada", "b_ada", "g_norm1", "g_norm2", "w_in", "b_gate", "g_qa", "g_ka", "g_qb", "g_kb", "rpb",
             "w_proj_a", "w_proj_b", "w_o", "w_ffn_in", "w_ffn_out")
    w = dict(zip(names, (w_ada, b_ada, g_norm1, g_norm2, w_in, b_gate, g_qa, g_ka, g_qb, g_kb, rpb, w_proj_a, w_proj_b,
                         w_o, w_ffn_in, w_ffn_out)))
    m = dict(zip(names, (m_w_ada, m_b_ada, m_g_norm1, m_g_norm2, m_w_in, m_b_gate, m_g_qa, m_g_ka, m_g_qb, m_g_kb, m_rpb,
                         m_w_proj_a, m_w_proj_b, m_w_o, m_w_ffn_in, m_w_ffn_out)))
    v = dict(zip(names, (v_w_ada, v_b_ada, v_g_norm1, v_g_norm2, v_w_in, v_b_gate, v_g_qa, v_g_ka, v_g_qb, v_g_kb, v_rpb,
                         v_w_proj_a, v_w_proj_b, v_w_o, v_w_ffn_in, v_w_ffn_out)))
    d = D_MODEL
    xi, yi, ci = _pos()
    chip = 2 * xi + yi
    me = 2 * chip + ci
    ada_cols = 6 * d // N_CHIP

    c_arr, chip_arr = ci.reshape(1).astype(jnp.int32), chip.reshape(1).astype(jnp.int32)
    first, rest = _BIG[:1], _BIG[1:]

    c_all = _small_allgather(jnp.broadcast_to(c, (8, d)), name="ag_c")[::8]
    b_sh = lax.dynamic_slice(b_ada, (0, chip * ada_cols), (1, ada_cols))
    mod_part = _ada_fwd(c_all, w_ada[0], b_sh, name="ada_fwd")
    mod_all = _small_allgather(mod_part, name="ag_mod").reshape(N_CHIP, 2, 8, ada_cols)[:, 0]
    mod = lax.dynamic_index_in_dim(mod_all, me, axis=1, keepdims=False).reshape(1, 6 * d)

    halves = {n: (2, _BIG_SHARD[n][0] // 2, _BIG_SHARD[n][1]) for n in _BIG}
    shards = {n: w[n][0].astype(BF16).reshape(halves[n]) for n in _BIG}
    land = lambda n: lax.empty((N_CHIP,) + halves[n], BF16)
    ag1 = _split_start([shards[n] for n in first], [land(n) for n in first], _ag_plan(1), 4, mod, name="ag1_start")
    ag2 = _split_start([shards[n] for n in rest], [land(n) for n in rest], _ag_plan(len(rest)), 4 * len(rest),
                       ag1[4], name="ag2_start")
    rpb_after = rpb[0] + ag2[4][0, 0]

    def first_weights(after):
        full1 = _split_wait(ag1[0], ag1[1], ag1[2], ag1[3], _ag_plan(1), after, name="ag1_wait")
        p_in = _ag_pass(full1, name="ag1_pass")[0].reshape((N_CHIP,) + _BIG_SHARD["w_in"])
        cut = W_QKV - 2 * _BIG_SHARD["w_in"][1]
        return dict(w_qkv=jnp.concatenate([p_in[0], p_in[1], p_in[2][:, :cut]], axis=1),
                    w_gates=jnp.concatenate([p_in[2][:, cut:], p_in[3]], axis=1))

    def late_weights(after):
        full2 = _split_wait(ag2[0], ag2[1], ag2[2], ag2[3], _ag_plan(len(rest)), after, name="ag2_wait")
        full2 = _ag_pass(full2, name="ag2_pass")
        full = {n: fu.reshape((N_CHIP,) + _BIG_SHARD[n]) for n, fu in zip(rest, full2)}
        return dict(w_pa=_join_cols(full["w_proj_a"]), w_pb=_join_cols(full["w_proj_b"]), w_o=full["w_o"].reshape(d, d),
                    w_ffn_in=_join_cols(full["w_ffn_in"]), w_ffn_out=full["w_ffn_out"].reshape(D_FF, d))

    def rs_begin(group, grads, tag):
        gps = [grads[n].reshape((N_CHIP,) + halves[n]) for n in group]
        ras = _sibling_send_halves(gps, name=f"rs_sibling_{tag}")
        sums = [_rs_add(gp, ra, c_arr, name=f"rs_add_{n}") for n, gp, ra in zip(group, gps, ras)]
        lands = [lax.empty((3,) + halves[n][1:], BF16) for n in group]
        st = _split_start([sb for _, sb in sums], lands, _rs_plan(len(group)), 3 * len(group), sums[0][0],
                          name=f"rs_{tag}_start")
        return sums, st

    def rs_end(group, begun, after, tag):
        sums, st = begun
        rbs = _split_wait(st[0], st[1], st[2], st[3], _rs_plan(len(group)), after, name=f"rs_{tag}_wait")
        return [_rs_final(sf, rb, chip_arr, name=f"rs_final_{n}") for n, (sf, _), rb in zip(group, sums, rbs)]

    begun = {}

    def early_grads(grads):
        begun["rest"] = rs_begin(rest, grads, "rest")
        return begun["rest"][1][4]

    loss_v, grad_x, grads, dmod, small = _device_step(
        x[0], loss_target[0], mod, first_weights, late_weights, early_grads, g_norm1, g_norm2, b_gate, g_qa, g_ka, g_qb,
        g_kb, rpb_after)
    begun["first"] = rs_begin(first, grads, "first")

    g, delta, new_m, new_v = {}, {}, {}, {}

    def finish(group, ts, tag):
        others = _sibling_swap(ts, name=f"rs_pair_{tag}")
        for n, t, o in zip(group, ts, others):
            gg, dl, nm, nv = _adamw_halves(w[n][0], t, o, m[n][0], v[n][0], c_arr, name=f"adamw_{n}")
            g[n], delta[n], new_m[n], new_v[n] = gg[None], dl[None], nm[None], nv[None]

    finish(rest, rs_end(rest, begun["rest"], begun["first"][1][4], "rest"), "rest")

    stats = _pack_small(dict(b_ada=dmod, **small))
    rows = _small_allgather(jnp.broadcast_to(stats, (8, STATS_W)), name="ag_stats")[::8]
    dmod_sh = lax.dynamic_slice(rows, (0, chip * ada_cols), (8, ada_cols))
    g_ada = _ada_bwd(c_all.T, dmod_sh, name="ada_bwd")
    tot = _row_sum(rows, name="stats_sum")
    g_small = _unpack_small(tot, {n: w[n].shape for n in _SMALL})

    done_rest = sum(new_v[n][0, :1, :1] for n in rest) + tot[:, :1]
    finish(first, rs_end(first, begun["first"], done_rest, "first"), "first")

    dl, nm, nv = _adamw(w_ada[0], g_ada, m_w_ada[0], v_w_ada[0], name="adamw_w_ada")
    g["w_ada"], delta["w_ada"], new_m["w_ada"], new_v["w_ada"] = g_ada[None], dl[None], nm[None], nv[None]
    shapes = {n: w[n].shape for n in _SMALL}
    dl, nm, nv = _adamw(_pack_small({n: w[n] for n in _SMALL}), tot, _pack_small({n: m[n] for n in _SMALL}),
                        _pack_small({n: v[n] for n in _SMALL}), name="adamw_small")
    delta.update(_unpack_small(dl, shapes))
    new_m.update(_unpack_small(nm, shapes))
    new_v.update(_unpack_small(nv, shapes))
    g.update(g_small)

    loss = lax.psum(loss_v[0, 0], ("x", "y", "c"))
    return (loss, grad_x[None], *[g[n] for n in names], *[delta[n] for n in names], *[new_m[n] for n in names],
            *[new_v[n] for n in names])
```

```python
import numpy as np

import jax
import jax.numpy as jnp
from jax import lax
from jax.experimental import pallas as pl
from jax.experimental.pallas import tpu as pltpu

F32 = jnp.float32
BF16 = jnp.bfloat16

D_MODEL = 1024
SEQ = 8192
HEAD_DIM = 64
GRID_W = 64
ROWS = SEQ // GRID_W
NA_HEADS = 8
NA_KH = 8
NA_KW = 16
DIL_CONFIGS = ((128, 1), (512, 4), (2048, 16))
ROT_DIM = 16
ROPE_THETA = 500000.0
D_FF = 2816
EPS = 1e-6
NEG = -1e30
WA = 512
WB = 768
WB_OUT = 256
W_QKV = 3 * WA + 3 * WB
W_QK = 2 * WA + 2 * WB
W_GATES = 2 * D_MODEL
SCALE = HEAD_DIM ** -0.5

ADAM_LR = 0.001
ADAM_B1 = 0.9
ADAM_B2 = 0.999
ADAM_EPS = 1e-08
ADAM_WD = 0.01
ADAM_STEP = 10

LANES = 128
ROW_TILE = 256
ROW_TILES = {"ffn_fwd": 512, "post_attn_fwd": 512, "post_attn_bwd": 512,
             "ffn_in_bwd": 512}
Q_BLOCK = 256
NA_QROWS = Q_BLOCK // GRID_W
NA_KROWS = NA_QROWS + NA_KH - 1
NA_NK = NA_KROWS * GRID_W
NA_PAIRS = (NA_KROWS + 1) // 2
NA_W = NA_PAIRS * LANES
NA_RO_NONE = 15
NA_SLOTS = 21
RP_LANE0 = GRID_W - NA_KW
DIL_HALF = 64
DIL_QB = 512
N_QBLK = SEQ // Q_BLOCK

N_DEV = 8
N_CHIP = 4
FF_CHIP = 2 * D_FF // N_CHIP
STATS_W = 14336


def _pcall(body, *, name, **kw):
    return pl.pallas_call(body, name=name, **kw)


_NT = (((1,), (1,)), ((), ()))
_TN = (((0,), (0,)), ((), ()))
_ARB = pltpu.CompilerParams(dimension_semantics=("arbitrary",))
_PAR = pltpu.CompilerParams(dimension_semantics=("parallel",))


def _dot(a, b):
    return jnp.dot(a, b, preferred_element_type=F32)


def _dot_nt(a, b):
    return lax.dot_general(a, b, _NT, preferred_element_type=F32)


def _wgrad(a, b, *, name, tm, tn, tk=1024, chips=None):
    s, ma = a.shape
    nb = b.shape[1]
    nk = s // tk
    nc = nb // chips if chips else tn
    cpb = tn // nc

    def body(a_ref, b_ref, o_ref, acc):
        k = pl.program_id(2)
        r = lax.dot_general(a_ref[...].astype(BF16), b_ref[...].astype(BF16), _TN, preferred_element_type=F32)

        @pl.when(k == 0)
        def _():
            acc[...] = r

        @pl.when(k > 0)
        def _():
            acc[...] += r

        @pl.when(k == nk - 1)
        def _():
            if chips:
                for q in range(cpb):
                    o_ref[q] = acc[:, q * nc:(q + 1) * nc]
            else:
                o_ref[...] = acc[...]

    if chips:
        o_spec = pl.BlockSpec((cpb, tm, nc), lambda i, j, k: (j, i, 0))
        out_shape = jax.ShapeDtypeStruct((chips, ma, nc), F32)
    else:
        o_spec = pl.BlockSpec((tm, tn), lambda i, j, k: (i, j))
        out_shape = jax.ShapeDtypeStruct((ma, nb), F32)
    return _pcall(
        body, name=name, grid=(ma // tm, nb // tn, nk),
        in_specs=[pl.BlockSpec((tk, tm), lambda i, j, k: (k, i)), pl.BlockSpec((tk, tn), lambda i, j, k: (k, j))],
        out_specs=o_spec, out_shape=out_shape, scratch_shapes=[pltpu.VMEM((tm, tn), F32)],
        compiler_params=pltpu.CompilerParams(dimension_semantics=("parallel", "parallel", "arbitrary")),
    )(a, b)


def _row_call(body, *, name, row_ins, res_ins, row_outs, acc_outs=(), scratch=()):
    row_ins = [a if isinstance(a, tuple) else (a, 1) for a in row_ins]
    row_outs = [o if len(o) == 3 else (*o, 1) for o in row_outs]
    s = row_ins[0][0].shape[0]
    tile = ROW_TILES.get(name, ROW_TILE)
    n = s // tile
    nri, nre, nro, nao = len(row_ins), len(res_ins), len(row_outs), len(acc_outs)

    def whole(shape):
        nd = len(shape)
        return pl.BlockSpec(tuple(shape), lambda i: (0,) * nd, pipeline_mode=pl.Buffered(1))

    def whole_out(shape):
        nd = len(shape)
        return pl.BlockSpec(tuple(shape), lambda i: (0,) * nd)

    def rows(w, d):
        if d == 1:
            return pl.BlockSpec((tile, w), lambda i: (i, 0))
        return pl.BlockSpec((d, tile // d, w), lambda i: (0, i, 0))

    in_specs = [rows(a.shape[1], d) for a, d in row_ins]
    in_specs += [whole(a.shape) for a in res_ins]
    out_specs = [rows(w, d) for w, _, d in row_outs]
    out_specs += [whole_out(shp) for shp, _ in acc_outs]
    out_shape = [jax.ShapeDtypeStruct((s, w) if d == 1 else (d, s // d, w), dt) for w, dt, d in row_outs]
    out_shape += [jax.ShapeDtypeStruct(tuple(shp), dt) for shp, dt in acc_outs]

    def wrapped(*refs):
        at = [0, nri, nri + nre, nri + nre + nro, nri + nre + nro + nao]
        body(pl.program_id(0), n, refs[at[0]:at[1]], refs[at[1]:at[2]], refs[at[2]:at[3]], refs[at[3]:at[4]],
             refs[at[4]:])

    args = [a if d == 1 else a.reshape(d, s // d, a.shape[1]) for a, d in row_ins]
    outs = _pcall(wrapped, name=name, grid=(n,), in_specs=in_specs, out_specs=out_specs, out_shape=out_shape,
                  scratch_shapes=list(scratch), compiler_params=_ARB)(*args, *res_ins)
    return [o.reshape(s, o.shape[-1]) if k < nro and row_outs[k][2] != 1 else o for k, o in enumerate(outs)]


def _stage_shape(name):
    return pltpu.VMEM((4, ROW_TILES.get(name, ROW_TILE), LANES), F32)


def _from_residue(ref, col, stage, slot):
    d, n = ref.shape[0], ref.shape[1]
    for r in range(d):
        stage.at[slot][pl.ds(r, n, stride=d), :] = ref[r, :, col:col + LANES].astype(F32)
    return stage[slot]


def _natural(ref, stage, slot0):
    if len(ref.shape) == 2:
        return ref[...]
    return jnp.concatenate([_from_residue(ref, c * LANES, stage, (slot0 + c) % 4)
                            for c in range(ref.shape[2] // LANES)], axis=1)


def _to_residue(val, ref, col, stage, slot):
    d, n = ref.shape[0], ref.shape[1]
    stage[slot] = val
    for r in range(d):
        ref[r, :, col:col + LANES] = stage.at[slot][pl.ds(r, n, stride=d), :].astype(ref.dtype)


def _fold8(t):
    r, w = t.shape
    return jnp.sum(t.reshape(r // 8, 8, w), axis=0)


def _sigmoid(t):
    return 0.5 * (jnp.tanh(0.5 * t) + 1.0)


def _head_lanes():
    return lax.broadcasted_iota(jnp.int32, (1, LANES), 1) < HEAD_DIM


def _head_mean(t, lo):
    s_lo = jnp.sum(jnp.where(lo, t, 0.0), axis=1, keepdims=True)
    s_hi = jnp.sum(jnp.where(lo, 0.0, t), axis=1, keepdims=True)
    return jnp.where(lo, s_lo, s_hi) * (1.0 / HEAD_DIM)


def _rms_mod(xv, g, sc, sh):
    rstd = lax.rsqrt(jnp.mean(xv * xv, axis=1, keepdims=True) + EPS)
    return (xv * rstd * g) * (1.0 + sc) + sh


def _rms_mod_bwd(xv, dh, g, sc):
    rstd = lax.rsqrt(jnp.mean(xv * xv, axis=1, keepdims=True) + EPS)
    xhat = xv * rstd
    dn = dh * (1.0 + sc)
    dxhat = dn * g
    dx = rstd * (dxhat - xhat * jnp.mean(dxhat * xhat, axis=1, keepdims=True))
    return dx, dh, dh * (xhat * g), dn * xhat


def _mix_weights(ls):
    m = jnp.maximum(jnp.maximum(ls[0], ls[1]), ls[2])
    es = [jnp.exp(t - m) for t in ls]
    den = es[0] + es[1] + es[2]
    return [e / den for e in es]


def _rope_tables():
    half = ROT_DIM // 2
    inv_freq = ROPE_THETA ** (-(jnp.arange(half, dtype=F32) * 2.0) / ROT_DIM)
    lane = np.arange(LANES) % HEAD_DIM
    ang = jnp.arange(SEQ).astype(F32)[:, None] * jnp.tile(inv_freq, LANES // half)[None, :]
    cos, sin = jnp.cos(ang), jnp.sin(ang)
    first, second = jnp.asarray(lane < half)[None, :], jnp.asarray((lane >= half) & (lane < ROT_DIM))[None, :]
    cos_t = jnp.where(first | second, cos, 1.0)
    return cos_t, jnp.where(second, sin, 0.0), jnp.where(first, -sin, 0.0)


_SECTIONS = ((0, WA, 0, False), (WA, 2 * WA, 1, False), (2 * WA, 3 * WA, -1, False),
             (3 * WA, 3 * WA + WB, 2, True), (3 * WA + WB, 3 * WA + 2 * WB, 3, True), (3 * WA + 2 * WB, W_QKV, -1, False))


def _pre_attn_fwd(x, cos_t, sa_t, sb_t, g1, sc1, sh1, w_qkv, w_gates, gains, *, name):
    half = ROT_DIM // 2
    dilated = [(g, dd) for g, (_, dd) in enumerate(DIL_CONFIGS) if dd > 1]

    def body(i, n, rin, res, rout, aout, scr):
        x_ref, cos_ref, sa_ref, sb_ref = rin
        g_ref, sc_ref, sh_ref, wq_ref, wg_ref, gains_ref = res
        h1_ref, qkvn_ref, pre_ref, gates_ref = rout[:4]
        group_ref = {g: rout[4 + k] for k, (g, _) in enumerate(dilated)}
        (stage,) = scr
        staged = 0
        hb = _rms_mod(x_ref[...], g_ref[...], sc_ref[...], sh_ref[...]).astype(BF16)
        h1_ref[...] = hb
        gates_ref[...] = _dot(hb, wg_ref[...]).astype(BF16)
        lo = _head_lanes()
        cosv, sav, sbv = cos_ref[...], sa_ref[...], sb_ref[...]
        pre_at = 0
        for si, (c0, c1, kind, rot) in enumerate(_SECTIONS):
            sec = _dot(hb, wq_ref[:, c0:c1])
            for ch in range((c1 - c0) // LANES):
                t = sec[:, ch * LANES:(ch + 1) * LANES]
                if kind >= 0:
                    pre_ref[:, pre_at:pre_at + LANES] = t.astype(BF16)
                    pre_at += LANES
                    t = t * lax.rsqrt(_head_mean(t * t, lo) + EPS) * gains_ref[kind:kind + 1, :]
                    if rot:
                        t = t * cosv + pltpu.roll(t, half, 1) * sav + pltpu.roll(t, LANES - half, 1) * sbv
                qkvn_ref[:, c0 + ch * LANES:c0 + (ch + 1) * LANES] = t.astype(BF16)
                group = ch * LANES // WB_OUT if si >= 3 else 0
                if group in group_ref:
                    col = (si - 3) * WB_OUT + ch * LANES % WB_OUT
                    _to_residue(t, group_ref[group], col, stage, staged % 4)
                    staged += 1

    return _row_call(body, name=name, row_ins=[x, cos_t, sa_t, sb_t], res_ins=[g1, sc1, sh1, w_qkv, w_gates, gains],
                     row_outs=[(D_MODEL, BF16), (W_QKV, BF16), (W_QK, BF16), (W_GATES, BF16)]
                     + [(3 * WB_OUT, BF16, dd) for _, dd in dilated], scratch=[_stage_shape(name)])


def _pre_attn_bwd(qk_pre, d_parts, dgates, x, dx1, cos_t, sa_t, sb_t, w_qkv, w_gates, gains, g1, sc1, *, name):
    half = ROT_DIM // 2
    nparts = len(d_parts)
    where = []
    residue = [isinstance(part, tuple) for part in d_parts]
    for pi, part in enumerate(d_parts):
        width = (part[0] if residue[pi] else part).shape[1]
        where += [(pi, cj) for cj in range(width // LANES)]
    assert len(where) == W_QKV // LANES

    def body(i, n, rin, res, rout, aout, scr):
        pre_ref, d_refs = rin[0], rin[1:1 + nparts]
        dgates_ref, x_ref, dx1_ref, cos_ref, sa_ref, sb_ref = rin[1 + nparts:]
        wq_ref, wg_ref, gains_ref, g_ref, sc_ref = res
        dqkv_ref, gx_ref = rout
        dgains_ref, sums_ref = aout
        accg, accs, stage = scr
        staged = 0

        @pl.when(i == 0)
        def _():
            accg[...] = jnp.zeros_like(accg)
            accs[...] = jnp.zeros_like(accs)

        lo = _head_lanes()
        cosv, sav, sbv = cos_ref[...], sa_ref[...], sb_ref[...]
        dh = _dot_nt(dgates_ref[...], wg_ref[...])
        pre_at = 0
        for c0, c1, kind, rot in _SECTIONS:
            for ch in range((c1 - c0) // LANES):
                pi, cj = where[c0 // LANES + ch]
                if residue[pi]:
                    dt = _from_residue(d_refs[pi], cj * LANES, stage, staged % 4)
                    staged += 1
                else:
                    dt = d_refs[pi][:, cj * LANES:(cj + 1) * LANES]
                if kind >= 0:
                    if rot:
                        dt = dt * cosv + pltpu.roll(dt * sav, LANES - half, 1) + pltpu.roll(dt * sbv, half, 1)
                    t = pre_ref[:, pre_at:pre_at + LANES].astype(F32)
                    pre_at += LANES
                    rstd = lax.rsqrt(_head_mean(t * t, lo) + EPS)
                    xhat = t * rstd
                    accg[kind] += _fold8(dt * xhat)
                    dxhat = dt * gains_ref[kind:kind + 1, :]
                    dt = rstd * (dxhat - xhat * _head_mean(dxhat * xhat, lo))
                dqkv_ref[:, c0 + ch * LANES:c0 + (ch + 1) * LANES] = dt.astype(BF16)
            dh = dh + _dot_nt(dqkv_ref[:, c0:c1], wq_ref[:, c0:c1])
        dx, t_sh, t_sc, t_g = _rms_mod_bwd(x_ref[...], dh, g_ref[...], sc_ref[...])
        gx_ref[...] = dx1_ref[...] + dx
        accs[0] += _fold8(t_sh)
        accs[1] += _fold8(t_sc)
        accs[2] += _fold8(t_g)

        @pl.when(i == n - 1)
        def _():
            t = jnp.sum(accg[...], axis=1)
            dgains_ref[...] = t + pltpu.roll(t, HEAD_DIM, 1)
            sums_ref[...] = jnp.sum(accs[...], axis=1)

    return _row_call(
        body, name=name, row_ins=[qk_pre, *d_parts, dgates, x, dx1, cos_t, sa_t, sb_t],
        res_ins=[w_qkv, w_gates, gains, g1, sc1], row_outs=[(W_QKV, BF16), (D_MODEL, F32)],
        acc_outs=[((4, LANES), F32), ((3, D_MODEL), F32)],
        scratch=[pltpu.VMEM((4, 8, LANES), F32), pltpu.VMEM((3, 8, D_MODEL), F32), _stage_shape(name)])


def _post_attn_fwd(o_a, o_g, l_g, gates, x, w_pa, w_pb, w_o, b_gate, gt1, g2, sc2, sh2, *, name):
    d = D_MODEL

    def body(i, n, rin, res, rout, aout, scr):
        oa_ref, o0, o1, o2, l0, l1, l2, gates_ref, x_ref = rin
        wpa_ref, wpb_ref, wo_ref, b_ref, gt_ref, g_ref, sc_ref, sh_ref = res
        ob_ref, merged_ref, mo_ref, x1_ref, h2_ref = rout
        (stage,) = scr
        ogs = [_natural(r, stage, 0) for r in (o0, o1, o2)]
        ws = _mix_weights([_natural(r, stage, 2) for r in (l0, l1, l2)])
        obb = (ws[0] * ogs[0] + ws[1] * ogs[1] + ws[2] * ogs[2]).astype(BF16)
        ob_ref[...] = obb
        pa = _dot(oa_ref[...].astype(BF16), wpa_ref[...])
        pb = _dot(obb, wpb_ref[...])
        ga = _sigmoid(gates_ref[:, :d].astype(F32) + b_ref[:, :d])
        gb = _sigmoid(gates_ref[:, d:].astype(F32) + b_ref[:, d:])
        merged = (ga * pa + gb * pb).astype(BF16)
        merged_ref[...] = merged
        mo = _dot(merged, wo_ref[...])
        mo_ref[...] = mo.astype(BF16)
        x1 = x_ref[...] + gt_ref[...] * mo
        x1_ref[...] = x1
        h2_ref[...] = _rms_mod(x1, g_ref[...], sc_ref[...], sh_ref[...]).astype(BF16)

    return _row_call(body, name=name, row_ins=[o_a, *o_g, *l_g, gates, x],
                     res_ins=[w_pa, w_pb, w_o, b_gate, gt1, g2, sc2, sh2],
                     row_outs=[(WB_OUT, BF16), (d, BF16), (d, BF16), (d, F32), (d, BF16)], scratch=[_stage_shape(name)])


def _ffn_fwd(h2, w_ffn_in, *, name):
    def body(i, n, rin, res, rout, aout, scr):
        (h_ref,), (w_ref,), (act_ref, ff_ref) = rin, res, rout
        hv = h_ref[...]
        for q in range(2):
            a = _dot(hv, w_ref[:, q * FF_CHIP:(q + 1) * FF_CHIP])
            up = _dot(hv, w_ref[:, D_FF + q * FF_CHIP:D_FF + (q + 1) * FF_CHIP])
            sl = slice(q * FF_CHIP, (q + 1) * FF_CHIP)
            act_ref[:, sl] = (a * _sigmoid(a) * up).astype(BF16)
            ff_ref[:, sl] = a.astype(BF16)
            ff_ref[:, D_FF + q * FF_CHIP:D_FF + (q + 1) * FF_CHIP] = up.astype(BF16)

    return _row_call(body, name=name, row_ins=[h2], res_ins=[w_ffn_in], row_outs=[(D_FF, BF16), (2 * D_FF, BF16)])


def _ffn_mid(act, ff, x1, tgt, w_ffn_out, gt2, *, name):
    d = D_MODEL

    def body(i, n, rin, res, rout, aout, scr):
        act_ref, ff_ref, x1_ref, tgt_ref = rin
        wo_ref, gt_ref = res
        dy_ref, dffo_ref, dff_ref = rout
        dgt_ref, loss_ref = aout
        (acc,) = scr

        @pl.when(i == 0)
        def _():
            acc[...] = jnp.zeros_like(acc)

        ffo = _dot(act_ref[...], wo_ref[...])
        gtv = gt_ref[...]
        e = x1_ref[...] + gtv * ffo - tgt_ref[...]
        dy = e * (1.0 / d)
        dy_ref[...] = dy
        dffo = (gtv * dy).astype(BF16)
        dffo_ref[...] = dffo
        acc[0] += _fold8(dy * ffo)
        acc[1] += _fold8(e * e)
        for q in range(2):
            sl = slice(q * FF_CHIP, (q + 1) * FF_CHIP)
            su = slice(D_FF + q * FF_CHIP, D_FF + (q + 1) * FF_CHIP)
            dact = _dot_nt(dffo, wo_ref[sl, :])
            a = ff_ref[:, sl].astype(F32)
            up = ff_ref[:, su].astype(F32)
            sg = _sigmoid(a)
            dff_ref[:, sl] = (dact * up * (sg * (1.0 + a * (1.0 - sg)))).astype(BF16)
            dff_ref[:, su] = (dact * (a * sg)).astype(BF16)

        @pl.when(i == n - 1)
        def _():
            dgt_ref[...] = jnp.sum(acc[0], axis=0, keepdims=True)
            tot = jnp.sum(jnp.sum(acc[1], axis=0, keepdims=True), axis=1, keepdims=True)
            loss_ref[...] = jnp.broadcast_to(tot * (0.5 / d), (1, LANES))

    return _row_call(body, name=name, row_ins=[act, ff, x1, tgt], res_ins=[w_ffn_out, gt2],
                     row_outs=[(d, F32), (d, BF16), (2 * D_FF, BF16)], acc_outs=[((1, d), F32), ((1, LANES), F32)],
                     scratch=[pltpu.VMEM((2, 8, d), F32)])


def _ffn_in_bwd(dff, x1, dy, mo, w_ffn_in, g2, sc2, gt1, *, name):
    d = D_MODEL

    def body(i, n, rin, res, rout, aout, scr):
        dff_ref, x1_ref, dy_ref, mo_ref = rin
        w_ref, g_ref, sc_ref, gt_ref = res
        dx1_ref, dmo_ref = rout
        (sums_ref,) = aout
        (acc,) = scr

        @pl.when(i == 0)
        def _():
            acc[...] = jnp.zeros_like(acc)

        dh = _dot_nt(dff_ref[...], w_ref[...])
        dx, t_sh, t_sc, t_g = _rms_mod_bwd(x1_ref[...], dh, g_ref[...], sc_ref[...])
        dx1 = dy_ref[...] + dx
        dx1_ref[...] = dx1
        dmo_ref[...] = (gt_ref[...] * dx1).astype(BF16)
        acc[0] += _fold8(t_sh)
        acc[1] += _fold8(t_sc)
        acc[2] += _fold8(t_g)
        acc[3] += _fold8(dx1 * mo_ref[...].astype(F32))

        @pl.when(i == n - 1)
        def _():
            sums_ref[...] = jnp.sum(acc[...], axis=1)

    return _row_call(body, name=name, row_ins=[dff, x1, dy, mo], res_ins=[w_ffn_in, g2, sc2, gt1],
                     row_outs=[(d, F32), (d, BF16)], acc_outs=[((4, d), F32)], scratch=[pltpu.VMEM((4, 8, d), F32)])


def _post_attn_bwd(dmo, gates, o_a, o_g, l_g, w_pa, w_pb, w_o, b_gate, *, name):
    d = D_MODEL

    def body(i, n, rin, res, rout, aout, scr):
        dmo_ref, gates_ref, oa_ref, o0, o1, o2, l0, l1, l2 = rin
        wpa_ref, wpb_ref, wo_ref, b_ref = res
        dpa_ref, dpb_ref, dgates_ref, doa_ref = rout[:4]
        do_refs, dl_refs = rout[4:7], rout[7:10]
        (dbg_ref,) = aout
        acc, stage = scr

        @pl.when(i == 0)
        def _():
            acc[...] = jnp.zeros_like(acc)

        ogs = [_natural(r, stage, 0) for r in (o0, o1, o2)]
        ws = _mix_weights([_natural(r, stage, 2) for r in (l0, l1, l2)])
        obb = (ws[0] * ogs[0] + ws[1] * ogs[1] + ws[2] * ogs[2]).astype(BF16)
        pa = _dot(oa_ref[...].astype(BF16), wpa_ref[...])
        pb = _dot(obb, wpb_ref[...])
        ga = _sigmoid(gates_ref[:, :d].astype(F32) + b_ref[:, :d])
        gb = _sigmoid(gates_ref[:, d:].astype(F32) + b_ref[:, d:])
        dm = _dot_nt(dmo_ref[...], wo_ref[...])
        dpa = (dm * ga).astype(BF16)
        dpb = (dm * gb).astype(BF16)
        dpa_ref[...] = dpa
        dpb_ref[...] = dpb
        dga = dm * pa * ga * (1.0 - ga)
        dgb = dm * pb * gb * (1.0 - gb)
        dgates_ref[:, :d] = dga.astype(BF16)
        dgates_ref[:, d:] = dgb.astype(BF16)
        acc[:, :d] += _fold8(dga)
        acc[:, d:] += _fold8(dgb)
        doa_ref[...] = _dot_nt(dpa, wpa_ref[...])
        dob = _dot_nt(dpb, wpb_ref[...])
        lo = _head_lanes()
        for ch in range(WB_OUT // LANES):
            sl = slice(ch * LANES, (ch + 1) * LANES)
            dv = dob[:, sl]
            wc = [w[:, sl] for w in ws]
            ts = [_head_mean(dv * og[:, sl], lo) * float(HEAD_DIM) for og in ogs]
            tbar = wc[0] * ts[0] + wc[1] * ts[1] + wc[2] * ts[2]
            for g in range(3):
                for k, (ref, val) in enumerate(((do_refs[g], wc[g] * dv), (dl_refs[g], wc[g] * (ts[g] - tbar)))):
                    if len(ref.shape) == 2:
                        ref[:, sl] = val
                    else:
                        _to_residue(val, ref, ch * LANES, stage, (2 * g + k) % 4)

        @pl.when(i == n - 1)
        def _():
            dbg_ref[...] = jnp.sum(acc[...], axis=0, keepdims=True)

    return _row_call(body, name=name, row_ins=[dmo, gates, o_a, *o_g, *l_g], res_ins=[w_pa, w_pb, w_o, b_gate],
                     row_outs=[(d, BF16), (d, BF16), (2 * d, BF16), (WA, F32)]
                     + 2 * [(WB_OUT, F32, dd) for _, dd in DIL_CONFIGS],
                     acc_outs=[((1, 2 * d), F32)], scratch=[pltpu.VMEM((8, 2 * d), F32), _stage_shape(name)])


def _na_class_tables():
    ro = np.full((3, NA_QROWS, 2 * NA_PAIRS), NA_RO_NONE, np.int64)
    slot = np.zeros((3, NA_QROWS, NA_PAIRS), np.int64)
    for t in range(3):
        for a in range(NA_QROWS):
            qr = _NA_CLASS_R0[t] + a
            rs = min(max(qr - NA_KH // 2, 0), ROWS - NA_KH)
            for b in range(NA_KROWS):
                kr = _NA_CLASS_K0[t] + b
                if rs <= kr < rs + NA_KH:
                    ro[t, a, b] = kr - qr + (NA_KH - 1)
            for j in range(NA_PAIRS):
                slot[t, a, j] = 2 * j - a + (_NA_CLASS_K0[t] - _NA_CLASS_R0[t] + NA_KH - 1) + (NA_QROWS - 1)
    assert slot.min() >= 0 and slot.max() < NA_SLOTS
    return ro, slot


def _na_build_bias(i, cls_ref, rp_ref, cm_ref, bias_scr):
    ro, _ = _na_class_tables()
    lo = _head_lanes()
    first = jnp.logical_or(i == 0, cls_ref[i] != cls_ref[jnp.maximum(i - 1, 0)])
    for t in range(3):
        @pl.when(jnp.logical_and(first, cls_ref[i] == t))
        def _():
            for hh in range(2):
                for a in range(NA_QROWS):
                    for j in range(NA_PAIRS):
                        r0, r1 = int(ro[t, a, 2 * j]), int(ro[t, a, 2 * j + 1])
                        x0 = jnp.broadcast_to(rp_ref[hh, r0:r0 + 1, :], (GRID_W, LANES))
                        x1 = jnp.broadcast_to(rp_ref[hh, r1:r1 + 1, :], (GRID_W, LANES))
                        blk = jnp.where(lo, pltpu.roll(x0, GRID_W + 1, 1, stride=1, stride_axis=0),
                                        pltpu.roll(x1, 1, 1, stride=1, stride_axis=0))
                        bias_scr[hh, a * GRID_W:(a + 1) * GRID_W, j * LANES:(j + 1) * LANES] = blk + cm_ref[...]
    return first


def _attn_fwd(qkv, qc0, kc0, vc0, npairs, table, kstart, cls, nk, *, name, na=None, qb=Q_BLOCK):
    s = qkv.shape[0]

    def body(ks_ref, cls_ref, q_ref, k_ref, v_ref, b_ref, *rest):
        if na:
            cm_ref, o_ref, lse_ref, bias_scr = rest
        else:
            o_ref, lse_ref = rest
        i = pl.program_id(1)
        if na:
            _na_build_bias(i, cls_ref, b_ref, cm_ref, bias_scr)
        ks = pl.multiple_of(ks_ref[i], 64)
        q2 = q_ref[...]
        k2 = k_ref[pl.ds(ks, nk), :]
        v2 = v_ref[pl.ds(ks, nk), :]
        lo = _head_lanes()
        outs, lses = [], []
        for h in range(2):
            qm = jnp.where(lo if h == 0 else jnp.logical_not(lo), q2, jnp.zeros_like(q2))
            sc = _dot_nt(qm, k2) * SCALE + (bias_scr[h, :, :nk] if na else b_ref[0, 0])
            m = jnp.max(sc, axis=1, keepdims=True)
            p = jnp.exp(sc - m)
            l = jnp.sum(p, axis=1, keepdims=True)
            pv = _dot(p.astype(BF16), v2)
            outs.append(pv / l)
            lses.append(m + jnp.log(l))
        o_ref[...] = jnp.where(lo, outs[0], outs[1])
        lse_ref[...] = jnp.where(lo, lses[0], lses[1])

    w = npairs * LANES
    in_specs = [
        pl.BlockSpec((qb, LANES), lambda p, i, ks, cl: (i, qc0 + p)),
        pl.BlockSpec((s, LANES), lambda p, i, ks, cl: (0, kc0 + p)),
        pl.BlockSpec((s, LANES), lambda p, i, ks, cl: (0, vc0 + p)),
    ]
    if na:
        in_specs += _na_bias_specs()
        args, scratch = (kstart, cls, qkv, qkv, qkv, *na), [pltpu.VMEM((2, Q_BLOCK, NA_W), F32)]
    else:
        in_specs.append(pl.BlockSpec((1, 1, qb, nk), lambda p, i, ks, cl: (cl[i], 0, 0, 0)))
        args, scratch = (kstart, cls, qkv, qkv, qkv, table), []
    grid_spec = pltpu.PrefetchScalarGridSpec(
        num_scalar_prefetch=2, grid=(npairs, s // qb), in_specs=in_specs,
        out_specs=[pl.BlockSpec((qb, LANES), lambda p, i, ks, cl: (i, p)),
                   pl.BlockSpec((qb, LANES), lambda p, i, ks, cl: (i, p))],
        scratch_shapes=scratch,
    )
    return _pcall(body, name=name, grid_spec=grid_spec,
                  out_shape=[jax.ShapeDtypeStruct((s, w), F32), jax.ShapeDtypeStruct((s, w), F32)],
                  compiler_params=pltpu.CompilerParams(dimension_semantics=("parallel", "arbitrary")),
                  )(*args)


def _na_bias_specs():
    return [pl.BlockSpec((2, 16, LANES), lambda p, i, ks, cl: (p, 0, 0)),
            pl.BlockSpec((GRID_W, LANES), lambda p, i, ks, cl: (0, 0))]


def _attn_bwd(qkv, qc0, kc0, vc0, npairs, table, kstart, cls, nk, do, o, lse, *, name, dlse=None, na=None,
              qb=Q_BLOCK, order=None):
    s = qkv.shape[0]
    has_dlse = dlse is not None
    _, slot = _na_class_tables()

    def body(ks_ref, cls_ref, q_ref, k_ref, v_ref, b_ref, *rest):
        if na:
            cm_ref, rest = rest[0], rest[1:]
        do_ref, o_ref, lse_ref, rest = rest[0], rest[1], rest[2], rest[3:]
        if has_dlse:
            dlse_ref, rest = rest[0], rest[1:]
        if order is not None:
            rest = rest[1:]
        dq_ref, dk_ref, dv_ref = rest[0], rest[1], rest[2]
        if na:
            bank_ref, bias_scr, dbias_scr, bank_scr = rest[3:]
        i = pl.program_id(1)
        if na:
            first = _na_build_bias(i, cls_ref, b_ref, cm_ref, bias_scr)

        @pl.when(i == 0)
        def _():
            dk_ref[...] = jnp.zeros_like(dk_ref)
            dv_ref[...] = jnp.zeros_like(dv_ref)
            if na:
                dbias_scr[...] = jnp.zeros_like(dbias_scr)
                bank_scr[...] = jnp.zeros_like(bank_scr)

        ks = pl.multiple_of(ks_ref[i], 64)
        q2 = q_ref[...]
        k2 = k_ref[pl.ds(ks, nk), :]
        v2 = v_ref[pl.ds(ks, nk), :]
        do2 = do_ref[...]
        lse2 = lse_ref[...]
        doo = do2 * o_ref[...]
        do2b = do2.astype(BF16)
        lo = _head_lanes()
        lane = lax.broadcasted_iota(jnp.int32, (1, LANES), 1)
        dqs, dks, dvs = [], [], []
        for h in range(2):
            mh = lo if h == 0 else jnp.logical_not(lo)
            qm = jnp.where(mh, q2, jnp.zeros_like(q2))
            sc = _dot_nt(qm, k2) * SCALE + (bias_scr[h, :, :nk] if na else b_ref[0, 0])
            lse_h = jnp.max(jnp.where(mh, lse2, NEG), axis=1, keepdims=True)
            p = jnp.exp(sc - lse_h)
            delta = jnp.sum(jnp.where(mh, doo, 0.0), axis=1, keepdims=True)
            dom = jnp.where(mh, do2b, jnp.zeros_like(do2b))
            dp = _dot_nt(dom, v2)
            t = dp - delta
            if has_dlse:
                t = t + jnp.sum(jnp.where(lane == h * HEAD_DIM, dlse_ref[...], 0.0), axis=1, keepdims=True)
            ds = p * t
            if na:
                @pl.when(first)
                def _():
                    dbias_scr[h, :, :nk] = ds

                @pl.when(jnp.logical_not(first))
                def _():
                    dbias_scr[h, :, :nk] += ds
            dsb = ds.astype(BF16)
            dqs.append(_dot(dsb, k2))
            dks.append(lax.dot_general(dsb, q2, _TN, preferred_element_type=F32))
            dvs.append(lax.dot_general(p.astype(BF16), do2b, _TN, preferred_element_type=F32))
        dq_ref[...] = jnp.where(lo, dqs[0], dqs[1]) * SCALE
        dk_ref[pl.ds(ks, nk), :] += jnp.where(lo, dks[0], dks[1]) * SCALE
        dv_ref[pl.ds(ks, nk), :] += jnp.where(lo, dvs[0], dvs[1])
        if na:
            last = jnp.logical_or(i == N_QBLK - 1, cls_ref[i] != cls_ref[jnp.minimum(i + 1, N_QBLK - 1)])
            for t in range(3):
                @pl.when(jnp.logical_and(last, cls_ref[i] == t))
                def _():
                    for hh in range(2):
                        for a in range(NA_QROWS):
                            for j in range(NA_PAIRS):
                                bank_scr[hh, int(slot[t, a, j])] += dbias_scr[
                                    hh, a * GRID_W:(a + 1) * GRID_W, j * LANES:(j + 1) * LANES]

            @pl.when(i == N_QBLK - 1)
            def _():
                bank_ref[...] = bank_scr[...]

    w = npairs * LANES
    blk = lambda: pl.BlockSpec((qb, LANES), lambda p, i, ks, cl: (i, p))
    full = lambda: pl.BlockSpec((s, LANES), lambda p, i, ks, cl: (0, p))
    in_specs = [
        pl.BlockSpec((qb, LANES), lambda p, i, ks, cl: (i, qc0 + p)),
        pl.BlockSpec((s, LANES), lambda p, i, ks, cl: (0, kc0 + p)),
        pl.BlockSpec((s, LANES), lambda p, i, ks, cl: (0, vc0 + p)),
    ]
    if na:
        in_specs += _na_bias_specs()
        args = [kstart, cls, qkv, qkv, qkv, *na]
    else:
        in_specs.append(pl.BlockSpec((1, 1, qb, nk), lambda p, i, ks, cl: (cl[i], 0, 0, 0)))
        args = [kstart, cls, qkv, qkv, qkv, table]
    in_specs += [blk(), blk(), blk()]
    args += [do, o, lse]
    if has_dlse:
        in_specs.append(blk())
        args.append(dlse)
    out_specs = [blk(), full(), full()]
    out_shape = [jax.ShapeDtypeStruct((s, w), F32)] * 3
    scratch = []
    if na:
        bank_shape = (2, NA_SLOTS, GRID_W, LANES)
        out_specs.append(pl.BlockSpec(bank_shape, lambda p, i, ks, cl: (p, 0, 0, 0)))
        out_shape.append(jax.ShapeDtypeStruct((2 * npairs,) + bank_shape[1:], F32))
        scratch = [pltpu.VMEM((2, Q_BLOCK, NA_W), F32), pltpu.VMEM((2, Q_BLOCK, NA_W), F32), pltpu.VMEM(bank_shape, F32)]
    if order is not None:
        in_specs.append(pl.BlockSpec(order.shape, lambda p, i, ks, cl: (0, 0)))
        args.append(order)
    grid_spec = pltpu.PrefetchScalarGridSpec(num_scalar_prefetch=2, grid=(npairs, s // qb), in_specs=in_specs,
                                             out_specs=out_specs, scratch_shapes=scratch)
    return _pcall(body, name=name, grid_spec=grid_spec, out_shape=out_shape,
                  compiler_params=pltpu.CompilerParams(dimension_semantics=("arbitrary", "arbitrary")))(*args)


_NA_CLASS_R0 = (0, NA_QROWS, ROWS - NA_QROWS)
_NA_CLASS_K0 = (0, 0, ROWS - NA_KROWS)
_RPB_RO = 2 * NA_KH - 1
_RPB_CO = 2 * NA_KW - 1
_BANK_ROWS = 48


def _na_constants():
    col = np.arange(GRID_W)
    cs = np.clip(col - NA_KW // 2, 0, GRID_W - NA_KW)
    vcol = (col[None, :] >= cs[:, None]) & (col[None, :] < cs[:, None] + NA_KW)
    colmask = np.where(np.concatenate([vcol, vcol], axis=1), 0.0, NEG).astype(np.float32)
    co = col[None, :] - col[:, None] + (NA_KW - 1)
    oh_col = np.zeros((GRID_W * GRID_W, LANES), np.float32)
    for qc in range(GRID_W):
        for kc in range(GRID_W):
            if vcol[qc, kc]:
                oh_col[qc * GRID_W + kc, co[qc, kc]] = 1.0
    ks = np.clip(np.arange(N_QBLK) * NA_QROWS - NA_KH // 2, 0, ROWS - NA_KROWS) * GRID_W
    cls = np.ones(N_QBLK, np.int32)
    cls[0], cls[-1] = 0, 2
    return colmask, oh_col, ks.astype(np.int32), cls


def _bank_reduce(bank, oh_col, *, name):
    def body(d_ref, ohc_ref, o_ref):
        o_ref[0] = jnp.dot(d_ref[0], ohc_ref[...], preferred_element_type=F32, precision=lax.Precision.HIGHEST)

    return _pcall(
        body, name=name, grid=(NA_HEADS,),
        in_specs=[pl.BlockSpec((1, _BANK_ROWS, GRID_W * GRID_W), lambda h: (h, 0, 0)),
                  pl.BlockSpec((GRID_W * GRID_W, LANES), lambda h: (0, 0))],
        out_specs=pl.BlockSpec((1, _BANK_ROWS, LANES), lambda h: (h, 0, 0)),
        out_shape=jax.ShapeDtypeStruct((NA_HEADS, _BANK_ROWS, LANES), F32), compiler_params=_PAR,
    )(bank, oh_col)


def _dil_constants(dilation):
    seg = SEQ // dilation
    nb = seg // DIL_QB
    nk = min(DIL_QB + 2 * DIL_HALF, seg)
    starts = [min(max(blk * DIL_QB - DIL_HALF, 0), seg - nk) for blk in range(nb)]
    shifts = sorted({w0 - blk * DIL_QB for blk, w0 in enumerate(starts)}, reverse=True)
    qi = np.arange(DIL_QB)[:, None]
    ki = np.arange(nk)[None, :]
    mask = np.stack([np.where(np.abs(ki + sh - qi) <= DIL_HALF, 0.0, NEG) for sh in shifts]).astype(np.float32)
    ks, cls = [], []
    for i in range(SEQ // DIL_QB):
        sub, blk = divmod(i, nb)
        cls.append(shifts.index(starts[blk] - blk * DIL_QB))
        ks.append(sub * seg + starts[blk])
    return mask.reshape(len(shifts), 1, DIL_QB, nk), np.asarray(ks, np.int32), np.asarray(cls, np.int32), nk


_VM = pl.BlockSpec(memory_space=pltpu.VMEM)


def _ada_fwd(c_all, w, b, *, name):
    def body(c_ref, w_ref, b_ref, o_ref):
        cv = c_ref[...]
        o_ref[...] = jnp.dot(cv * _sigmoid(cv), w_ref[...], preferred_element_type=F32,
                             precision=lax.Precision.HIGHEST) + b_ref[...]

    return _pcall(body, name=name, in_specs=[_VM, _VM, _VM], out_specs=_VM,
                  out_shape=jax.ShapeDtypeStruct((c_all.shape[0], w.shape[1]), F32))(c_all, w, b)


def _ada_bwd(c_all_t, dmod, *, name):
    def body(c_ref, d_ref, o_ref):
        cv = c_ref[...]
        o_ref[...] = jnp.dot(cv * _sigmoid(cv), d_ref[...], preferred_element_type=F32,
                             precision=lax.Precision.HIGHEST)

    return _pcall(body, name=name, in_specs=[_VM, _VM], out_specs=_VM,
                  out_shape=jax.ShapeDtypeStruct((c_all_t.shape[0], dmod.shape[1]), F32))(c_all_t, dmod)


def _row_sum(t, *, name):
    def body(t_ref, o_ref):
        o_ref[...] = jnp.sum(t_ref[...], axis=0, keepdims=True)

    return _pcall(body, name=name, in_specs=[_VM], out_specs=_VM,
                  out_shape=jax.ShapeDtypeStruct((1, t.shape[1]), F32))(t)


def _row_tile(rows):
    tr = rows
    for cand in range(8, 513, 8):
        if rows % cand == 0:
            tr = cand
    return tr


def _adamw_math(wv, gv, mv, vv):
    nm = ADAM_B1 * mv + (1.0 - ADAM_B1) * gv
    nv = ADAM_B2 * vv + (1.0 - ADAM_B2) * (gv * gv)
    m_hat = nm / (1.0 - ADAM_B1 ** ADAM_STEP)
    v_hat = nv / (1.0 - ADAM_B2 ** ADAM_STEP)
    return -ADAM_LR * (m_hat / (jnp.sqrt(v_hat) + ADAM_EPS) + ADAM_WD * wv), nm, nv


def _adamw(w, g, m, v, *, name):
    rows, cols = w.shape
    tr = _row_tile(rows)

    def body(w_ref, g_ref, m_ref, v_ref, d_ref, nm_ref, nv_ref):
        d_ref[...], nm_ref[...], nv_ref[...] = _adamw_math(w_ref[...], g_ref[...], m_ref[...], v_ref[...])

    spec = pl.BlockSpec((tr, cols), lambda i: (i, 0))
    return _pcall(body, name=name, grid=(rows // tr,), in_specs=[spec] * 4, out_specs=[spec] * 3,
                  out_shape=[jax.ShapeDtypeStruct((rows, cols), F32)] * 3, compiler_params=_PAR)(w, g, m, v)


def _adamw_halves(w, g_mine, g_other, m, v, c_arr, *, name):
    rows, cols = w.shape
    hr = rows // 2
    tr = _row_tile(hr)
    nt = hr // tr

    def body(c_ref, w_ref, t_ref, o_ref, m_ref, v_ref, g_ref, d_ref, nm_ref, nv_ref):
        gv = jnp.where(pl.program_id(0) == c_ref[0], t_ref[...], o_ref[...])
        g_ref[...] = gv
        d_ref[...], nm_ref[...], nv_ref[...] = _adamw_math(w_ref[...], gv, m_ref[...], v_ref[...])

    full = pl.BlockSpec((tr, cols), lambda h, i, c: (h * nt + i, 0))
    half = pl.BlockSpec((tr, cols), lambda h, i, c: (i, 0))
    grid_spec = pltpu.PrefetchScalarGridSpec(num_scalar_prefetch=1, grid=(2, nt),
                                             in_specs=[full, half, half, full, full], out_specs=[full] * 4)
    return _pcall(body, name=name, grid_spec=grid_spec, out_shape=[jax.ShapeDtypeStruct((rows, cols), F32)] * 4,
                  compiler_params=pltpu.CompilerParams(dimension_semantics=("parallel", "parallel")),
                  )(c_arr, w, g_mine, g_other, m, v)


_MESH = pl.DeviceIdType.MESH
_ANY = pl.BlockSpec(memory_space=pl.ANY)
_CHIP_FLIPS = ((1, 0), (0, 1), (1, 1))


def _pos():
    return lax.axis_index("x"), lax.axis_index("y"), lax.axis_index("c")


def _flip(v, f):
    return 1 - v if f else v


def _sem_pairs(n):
    return [pltpu.SemaphoreType.DMA((n,)), pltpu.SemaphoreType.DMA((n,))]


def _small_allgather(blk, *, name):
    m_per, n = blk.shape

    def body(x_ref, out_ref, send_sems, recv_sems, local_sem):
        x, y, c = _pos()
        me, sibling = (x, y, c), (x, y, 1 - c)
        chips = [(_flip(x, fx), _flip(y, fy)) for fx, fy in _CHIP_FLIPS]

        def rows(px, py, pc):
            return out_ref.at[pl.ds((4 * px + 2 * py + pc) * m_per, m_per), :]

        def copy(k, block, to, src=None):
            return pltpu.make_async_remote_copy(
                src_ref=rows(*block) if src is None else src, dst_ref=rows(*block),
                send_sem=send_sems.at[k], recv_sem=recv_sems.at[k], device_id=to, device_id_type=_MESH)

        mine = pltpu.make_async_copy(x_ref, rows(*me), local_sem)
        mine.start()
        first = [copy(0, me, sibling, src=x_ref)]
        first += [copy(1 + j, me, (*chip, c), src=x_ref) for j, chip in enumerate(chips)]
        for cp in first:
            cp.start()
        passed = [copy(4 + j, (*chip, c), sibling) for j, chip in enumerate(chips)]
        for j, chip in enumerate(chips):
            copy(1 + j, (*chip, c), me).wait_recv()
            passed[j].start()
        copy(0, sibling, me).wait_recv()
        for j, chip in enumerate(chips):
            copy(4 + j, (*chip, 1 - c), me).wait_recv()
        for cp in first + passed:
            cp.wait_send()
        mine.wait()

    return _pcall(
        body, name=name, out_shape=jax.ShapeDtypeStruct((N_DEV * m_per, n), blk.dtype),
        in_specs=[_VM], out_specs=_VM,
        scratch_shapes=_sem_pairs(7) + [pltpu.SemaphoreType.DMA],
    )(blk)


_HBM = pl.BlockSpec(memory_space=pltpu.HBM)
_SEM = pl.BlockSpec(memory_space=pltpu.SEMAPHORE)
_EFFECT = pltpu.SideEffectType.DATAFLOW_SIDE_EFFECTING


def _split_start(srcs, lands, plan, ncopies, after, *, name):
    ns, nl = len(srcs), len(lands)

    def body(*refs):
        src_refs, land_refs = refs[:ns], refs[ns:ns + nl]
        send_sems, recv_sems = refs[ns + nl + 1], refs[ns + nl + 2]
        token = refs[-1]
        x, y, c = _pos()
        for k, (src, dst, to, _) in enumerate(plan(x, y, c, src_refs, land_refs)):
            pltpu.make_async_remote_copy(src_ref=src, dst_ref=dst, send_sem=send_sems.at[k], recv_sem=recv_sems.at[k],
                                         device_id=to, device_id_type=_MESH).start()
        token[...] = jnp.zeros_like(token)

    hbm = lambda a: pltpu.HBM(a.shape, a.dtype)
    out = _pcall(
        body, name=name,
        out_shape=(pltpu.SemaphoreType.DMA((ncopies,)), pltpu.SemaphoreType.DMA((ncopies,)),
                   *[hbm(a) for a in srcs], *[hbm(a) for a in lands], jax.ShapeDtypeStruct((8, LANES), F32)),
        in_specs=[_HBM] * (ns + nl) + [_ANY], out_specs=(_SEM, _SEM, *[_HBM] * (ns + nl), _VM),
        input_output_aliases={i: 2 + i for i in range(ns + nl)},
        compiler_params=pltpu.CompilerParams(has_side_effects=_EFFECT),
    )(*[pltpu.with_memory_space_constraint(a, pltpu.HBM) for a in (*srcs, *lands)], after)
    return out[0], out[1], list(out[2:2 + ns]), list(out[2 + ns:2 + ns + nl]), out[-1]


def _split_wait(send_sems, recv_sems, srcs, lands, plan, after, *, name):
    ns, nl = len(srcs), len(lands)

    def body(*refs):
        src_refs, land_refs = refs[:ns], refs[ns:ns + nl]
        send_sems, recv_sems = refs[ns + nl], refs[ns + nl + 1]
        x, y, c = _pos()
        for k, (src, _, _, mine) in enumerate(plan(x, y, c, src_refs, land_refs)):
            cp = pltpu.make_async_remote_copy(src_ref=src, dst_ref=mine, send_sem=send_sems.at[k],
                                              recv_sem=recv_sems.at[k], device_id=(x, y, c), device_id_type=_MESH)
            cp.wait_send()
            cp.wait_recv()

    hbm = lambda a: pltpu.HBM(a.shape, a.dtype)
    out = _pcall(
        body, name=name, out_shape=tuple(hbm(a) for a in (*srcs, *lands)),
        in_specs=[_HBM] * (ns + nl) + [_SEM, _SEM, _ANY], out_specs=tuple([_HBM] * (ns + nl)),
        input_output_aliases={i: i for i in range(ns + nl)},
        compiler_params=pltpu.CompilerParams(has_side_effects=_EFFECT),
    )(*srcs, *lands, send_sems, recv_sems, after)
    return list(out[ns:])


def _ag_plan(nw):
    def plan(x, y, c, sh_refs, full_refs):
        j = 2 * x + y
        out = []
        for wi in range(nw):
            for fx, fy in _CHIP_FLIPS:
                px, py = _flip(x, fx), _flip(y, fy)
                out.append((sh_refs[wi].at[c], full_refs[wi].at[j, c], (px, py, c), full_refs[wi].at[2 * px + py, c]))
            out.append((sh_refs[wi], full_refs[wi].at[j], (x, y, 1 - c), full_refs[wi].at[j]))
        return out
    return plan


def _ag_pass(fulls, *, name):
    nw = len(fulls)

    def body(*refs):
        in_refs, out_refs = refs[:nw], refs[nw:2 * nw]
        send_sems, recv_sems = refs[2 * nw:]
        x, y, c = _pos()
        cps = []
        for wi in range(nw):
            for k, (fx, fy) in enumerate(_CHIP_FLIPS):
                jp = 2 * _flip(x, fx) + _flip(y, fy)
                sems = dict(send_sem=send_sems.at[3 * wi + k], recv_sem=recv_sems.at[3 * wi + k], device_id_type=_MESH)
                send = pltpu.make_async_remote_copy(src_ref=in_refs[wi].at[jp, c], dst_ref=out_refs[wi].at[jp, c],
                                                    device_id=(x, y, 1 - c), **sems)
                recv = pltpu.make_async_remote_copy(src_ref=in_refs[wi].at[jp, c], dst_ref=out_refs[wi].at[jp, 1 - c],
                                                    device_id=(x, y, c), **sems)
                cps.append((send, recv))
        for send, _ in cps:
            send.start()
        for send, recv in cps:
            send.wait_send()
            recv.wait_recv()

    return _pcall(body, name=name, out_shape=[jax.ShapeDtypeStruct(f.shape, f.dtype) for f in fulls],
                  in_specs=[_ANY] * nw, out_specs=[_ANY] * nw, input_output_aliases={i: i for i in range(nw)},
                  scratch_shapes=_sem_pairs(3 * nw))(*fulls)


def _rs_plan(nw):
    def plan(x, y, c, s_refs, rb_refs):
        out = []
        for wi in range(nw):
            for k, (fx, fy) in enumerate(_CHIP_FLIPS):
                px, py = _flip(x, fx), _flip(y, fy)
                out.append((s_refs[wi].at[2 * px + py], rb_refs[wi].at[k], (px, py, c), rb_refs[wi].at[k]))
        return out
    return plan


def _sibling_send_halves(gs, *, name):
    nw = len(gs)

    def body(*refs):
        g_refs, out_refs = refs[:nw], refs[nw:2 * nw]
        send_sems, recv_sems = refs[2 * nw:]
        x, y, c = _pos()
        cps = [pltpu.make_async_remote_copy(src_ref=g_refs[wi].at[k, 1 - c], dst_ref=out_refs[wi].at[k],
                                            send_sem=send_sems.at[4 * wi + k], recv_sem=recv_sems.at[4 * wi + k],
                                            device_id=(x, y, 1 - c), device_id_type=_MESH)
               for wi in range(nw) for k in range(N_CHIP)]
        for cp in cps:
            cp.start()
        for cp in cps:
            cp.wait()

    return _pcall(body, name=name,
                  out_shape=[jax.ShapeDtypeStruct((g.shape[0],) + g.shape[2:], g.dtype) for g in gs],
                  in_specs=[_ANY] * nw, out_specs=[_ANY] * nw, scratch_shapes=_sem_pairs(N_CHIP * nw))(*gs)


def _sibling_swap(ts, *, name):
    nw = len(ts)

    def body(*refs):
        t_refs, out_refs = refs[:nw], refs[nw:2 * nw]
        send_sems, recv_sems = refs[2 * nw:]
        x, y, c = _pos()
        cps = [pltpu.make_async_remote_copy(src_ref=t_refs[wi], dst_ref=out_refs[wi], send_sem=send_sems.at[wi],
                                            recv_sem=recv_sems.at[wi], device_id=(x, y, 1 - c), device_id_type=_MESH)
               for wi in range(nw)]
        for cp in cps:
            cp.start()
        for cp in cps:
            cp.wait()

    return _pcall(body, name=name, out_shape=[jax.ShapeDtypeStruct(t.shape, t.dtype) for t in ts],
                  in_specs=[_ANY] * nw, out_specs=[_ANY] * nw, scratch_shapes=_sem_pairs(nw))(*ts)


def _rs_add(g, ra, c_arr, *, name):
    n, _, r, w = g.shape

    def body(c_ref, g_ref, ra_ref, s_ref, sb_ref):
        t = g_ref[...] + ra_ref[...]
        s_ref[...] = t
        sb_ref[...] = t.astype(BF16)

    grid_spec = pltpu.PrefetchScalarGridSpec(
        num_scalar_prefetch=1, grid=(n,),
        in_specs=[pl.BlockSpec((None, None, r, w), lambda k, c: (k, c[0], 0, 0)),
                  pl.BlockSpec((None, r, w), lambda k, c: (k, 0, 0))],
        out_specs=[pl.BlockSpec((None, r, w), lambda k, c: (k, 0, 0))] * 2)
    return _pcall(body, name=name, grid_spec=grid_spec,
                  out_shape=[jax.ShapeDtypeStruct((n, r, w), F32), jax.ShapeDtypeStruct((n, r, w), BF16)],
                  compiler_params=_PAR)(c_arr, g, ra)


def _rs_final(s, rb, j_arr, *, name):
    _, r, w = s.shape

    def body(j_ref, s_ref, rb_ref, t_ref):
        t_ref[...] = ((s_ref[...] + rb_ref[0].astype(F32)) + rb_ref[1].astype(F32)) + rb_ref[2].astype(F32)

    grid_spec = pltpu.PrefetchScalarGridSpec(
        num_scalar_prefetch=1, grid=(1,),
        in_specs=[pl.BlockSpec((None, r, w), lambda i, j: (j[0], 0, 0)),
                  pl.BlockSpec((3, r, w), lambda i, j: (0, 0, 0))],
        out_specs=pl.BlockSpec((r, w), lambda i, j: (0, 0)))
    return _pcall(body, name=name, grid_spec=grid_spec, out_shape=jax.ShapeDtypeStruct((r, w), F32),
                  compiler_params=_ARB)(j_arr, s, rb)


def _tile2(g):
    return jnp.concatenate([g, g], axis=1)


_BIG = ("w_in", "w_ffn_in", "w_ffn_out", "w_o", "w_proj_a", "w_proj_b")
_BIG_SHARD = {"w_in": (1024, 1472), "w_ffn_in": (1024, 1408), "w_ffn_out": (704, 1024), "w_o": (256, 1024),
              "w_proj_a": (512, 256), "w_proj_b": (256, 256)}


def _device_step(x2, tgt, mod, first_weights, late_weights, early_grads, g_norm1, g_norm2, b_gate, g_qa, g_ka, g_qb,
                 g_kb, rpb):
    d = D_MODEL
    sh1, sc1, gt1, sh2, sc2, gt2 = [mod[:, k * d:(k + 1) * d] for k in range(6)]

    colmask, oh_col, na_ks, na_cls = _na_constants()
    rp = jnp.pad(rpb, ((0, 0), (0, 16 - _RPB_RO), (RP_LANE0, LANES - RP_LANE0 - _RPB_CO)), constant_values=NEG)
    na = (rp, jnp.asarray(colmask))
    na_ks, na_cls = jnp.asarray(na_ks), jnp.asarray(na_cls)
    dil = [_dil_constants(dd) for _, dd in DIL_CONFIGS]
    gains = jnp.concatenate([_tile2(g_qa), _tile2(g_ka), _tile2(g_qb), _tile2(g_kb)], axis=0)
    cos_t, sa_t, sb_t = _rope_tables()

    wts = first_weights(cos_t)
    h1, qkvn, qk_pre, gates, *qkv_dil = _pre_attn_fwd(x2, cos_t, sa_t, sb_t, g_norm1, sc1, sh1, wts["w_qkv"],
                                                      wts["w_gates"], gains, name="pre_attn_fwd")
    o_a, lse_a = _attn_fwd(qkvn, 0, 4, 8, 4, None, na_ks, na_cls, NA_NK, name="attn_a_fwd", na=na)
    arrs, o_g, l_g = [], [], []
    res = lambda t, dd: t if dd == 1 else (t, dd)
    for g, (_, dd) in enumerate(DIL_CONFIGS):
        tab_g, ks_g, cls_g, nk_g = jnp.asarray(dil[g][0]), jnp.asarray(dil[g][1]), jnp.asarray(dil[g][2]), dil[g][3]
        arr, cb = (qkvn, (12, 18, 24)) if dd == 1 else (qkv_dil.pop(0), (0, 2, 4))
        op, lp = _attn_fwd(arr, cb[0], cb[1], cb[2], 2, tab_g, ks_g, cls_g, nk_g, name=f"attn_d{g}_fwd", qb=DIL_QB)
        arrs.append((arr, cb, tab_g, ks_g, cls_g, nk_g))
        o_g.append(res(op, dd))
        l_g.append(res(lp, dd))
    wts = dict(wts, **late_weights(o_a))
    o_b, merged, mo, x1, h2 = _post_attn_fwd(o_a, o_g, l_g, gates, x2, wts["w_pa"], wts["w_pb"], wts["w_o"], b_gate,
                                             gt1, g_norm2, sc2, sh2, name="post_attn_fwd")
    act, ff = _ffn_fwd(h2, wts["w_ffn_in"], name="ffn_fwd")

    dy, dffo, dff, dgt2, loss_v = _ffn_mid(act, ff, x1, tgt, wts["w_ffn_out"], gt2, name="ffn_mid")
    grads = {}
    g_ffn_out = _wgrad(act, dffo, name="wg_ffn_out", tm=D_FF // 2, tn=d)
    grads["w_ffn_out"] = g_ffn_out.reshape(N_CHIP, D_FF // N_CHIP, d)
    grads["w_ffn_in"] = _wgrad(h2, dff, name="wg_ffn_in", tm=512, tn=2 * FF_CHIP, chips=N_CHIP)
    dx1, dmo, sums2 = _ffn_in_bwd(dff, x1, dy, mo, wts["w_ffn_in"], g_norm2, sc2, gt1, name="ffn_in_bwd")
    grads["w_o"] = _wgrad(merged, dmo, name="wg_o", tm=d, tn=d).reshape(N_CHIP, d // N_CHIP, d)
    pab = _post_attn_bwd(dmo, gates, o_a, o_g, l_g, wts["w_pa"], wts["w_pb"], wts["w_o"], b_gate, name="post_attn_bwd")
    dpa, dpb, dgates, do_a = pab[:4]
    do_g, dl_g, dbg = pab[4:7], pab[7:10], pab[10]
    g_pa = _wgrad(o_a, dpa, name="wg_pa", tm=WA, tn=d)
    g_pb = _wgrad(o_b, dpb, name="wg_pb", tm=WB_OUT, tn=d)
    grads["w_proj_a"] = g_pa.reshape(WA, N_CHIP, d // N_CHIP).transpose(1, 0, 2)
    grads["w_proj_b"] = g_pb.reshape(WB_OUT, N_CHIP, d // N_CHIP).transpose(1, 0, 2)
    order = early_grads(grads)
    dqa, dka, dva, bank = _attn_bwd(qkvn, 0, 4, 8, 4, None, na_ks, na_cls, NA_NK, do_a, o_a, lse_a,
                                    name="attn_a_bwd", na=na, order=order)
    dqs, dks, dvs = [], [], []
    for g, (_, dd) in enumerate(DIL_CONFIGS):
        arr, cb, tab_g, ks_g, cls_g, nk_g = arrs[g]
        plain = lambda t: t[0] if isinstance(t, tuple) else t
        dq, dk, dv = _attn_bwd(arr, cb[0], cb[1], cb[2], 2, tab_g, ks_g, cls_g, nk_g, do_g[g], plain(o_g[g]),
                               plain(l_g[g]), name=f"attn_d{g}_bwd", dlse=dl_g[g], qb=DIL_QB, order=order)
        dqs.append(res(dq, dd))
        dks.append(res(dk, dd))
        dvs.append(res(dv, dd))
    dqkv, grad_x, dgains, sums1 = _pre_attn_bwd(qk_pre, [dqa, dka, dva] + dqs + dks + dvs, dgates, x2, dx1, cos_t, sa_t,
                                                sb_t, wts["w_qkv"], wts["w_gates"], gains, g_norm1, sc1,
                                                name="pre_attn_bwd")
    g_qkv = _wgrad(h1, dqkv, name="wg_qkv", tm=d, tn=W_QKV // 2)
    g_gates = _wgrad(h1, dgates, name="wg_gates", tm=d, tn=W_GATES)
    nc, cut = _BIG_SHARD["w_in"][1], 3 * _BIG_SHARD["w_in"][1] - W_QKV
    grads["w_in"] = jnp.stack([g_qkv[:, :nc], g_qkv[:, nc:2 * nc],
                               jnp.concatenate([g_qkv[:, 2 * nc:], g_gates[:, :cut]], axis=1), g_gates[:, cut:]])

    bank = bank.reshape(NA_HEADS, NA_SLOTS, GRID_W, 2, GRID_W).transpose(0, 1, 3, 2, 4)
    bank = jnp.pad(bank.reshape(NA_HEADS, 2 * NA_SLOTS, GRID_W * GRID_W), ((0, 0), (0, _BANK_ROWS - 2 * NA_SLOTS), (0, 0)))
    g2 = _bank_reduce(bank, jnp.asarray(oh_col), name="rpb_reduce")[:, :2 * NA_SLOTS].reshape(NA_HEADS, NA_SLOTS, 2, LANES)
    g_rpb = g2[:, 3:3 + _RPB_RO, 0, :_RPB_CO] + g2[:, 2:2 + _RPB_RO, 1, :_RPB_CO]

    dmod = jnp.concatenate([sums1[0:1], sums1[1:2], sums2[3:4], sums2[0:1], sums2[1:2], dgt2], axis=1)
    small = dict(g_norm1=sums1[2:3], g_norm2=sums2[2:3], b_gate=dbg, g_qa=dgains[0:1, :HEAD_DIM],
                 g_ka=dgains[1:2, :HEAD_DIM], g_qb=dgains[2:3, :HEAD_DIM], g_kb=dgains[3:4, :HEAD_DIM], rpb=g_rpb)
    return loss_v, grad_x, grads, dmod, small


_SMALL = ("b_ada", "g_norm1", "g_norm2", "b_gate", "g_qa", "g_ka", "g_qb", "g_kb", "rpb")
_SMALL_N = {"b_ada": 6 * D_MODEL, "g_norm1": D_MODEL, "g_norm2": D_MODEL, "b_gate": 2 * D_MODEL, "g_qa": HEAD_DIM,
            "g_ka": HEAD_DIM, "g_qb": HEAD_DIM, "g_kb": HEAD_DIM, "rpb": NA_HEADS * _RPB_RO * _RPB_CO}


def _pack_small(parts):
    flat = [parts[n].reshape(1, _SMALL_N[n]) for n in _SMALL]
    used = sum(_SMALL_N.values())
    return jnp.concatenate(flat + [jnp.zeros((1, STATS_W - used), F32)], axis=1)


def _unpack_small(v, shapes):
    out, at = {}, 0
    for n in _SMALL:
        out[n] = v[:, at:at + _SMALL_N[n]].reshape(shapes[n])
        at += _SMALL_N[n]
    return out


def _join_cols(t):
    _, r, c = t.shape
    return t.transpose(1, 0, 2).reshape(r, N_CHIP * c)


def kernel(x, c, w_ada, b_ada, g_norm1, g_norm2, w_in, b_gate, g_qa, g_ka, g_qb, g_kb, rpb, w_proj_a, w_proj_b, w_o, w_ffn_in, w_ffn_out, loss_target, m_w_ada, m_b_ada, m_g_norm1, m_g_norm2, m_w_in, m_b_gate, m_g_qa, m_g_ka, m_g_qb, m_g_kb, m_rpb, m_w_proj_a, m_w_proj_b, m_w_o, m_w_ffn_in, m_w_ffn_out, v_w_ada, v_b_ada, v_g_norm1, v_g_norm2, v_w_in, v_b_gate, v_g_qa, v_g_ka, v_g_qb, v_g_kb, v_rpb, v_w_proj_a, v_w_proj_b, v_w_o, v_w_ffn_in, v_w_ffn_out):
    names = ("w_ada", "b_ada", "g_norm1", "g_norm2", "w_in", "b_gate", "g_qa", "g_ka", "g_qb", "g_kb", "rpb",
             "w_proj_a", "w_proj_b", "w_o", "w_ffn_in", "w_ffn_out")
    w = dict(zip(names, (w_ada, b_ada, g_norm1, g_norm2, w_in, b_gate, g_qa, g_ka, g_qb, g_kb, rpb, w_proj_a, w_proj_b,
                         w_o, w_ffn_in, w_ffn_out)))
    m = dict(zip(names, (m_w_ada, m_b_ada, m_g_norm1, m_g_norm2, m_w_in, m_b_gate, m_g_qa, m_g_ka, m_g_qb, m_g_kb, m_rpb,
                         m_w_proj_a, m_w_proj_b, m_w_o, m_w_ffn_in, m_w_ffn_out)))
    v = dict(zip(names, (v_w_ada, v_b_ada, v_g_norm1, v_g_norm2, v_w_in, v_b_gate, v_g_qa, v_g_ka, v_g_qb, v_g_kb, v_rpb,
                         v_w_proj_a, v_w_proj_b, v_w_o, v_w_ffn_in, v_w_ffn_out)))
    d = D_MODEL
    xi, yi, ci = _pos()
    chip = 2 * xi + yi
    me = 2 * chip + ci
    ada_cols = 6 * d // N_CHIP

    c_arr, chip_arr = ci.reshape(1).astype(jnp.int32), chip.reshape(1).astype(jnp.int32)
    first, rest = _BIG[:1], _BIG[1:]

    c_all = _small_allgather(jnp.broadcast_to(c, (8, d)), name="ag_c")[::8]
    b_sh = lax.dynamic_slice(b_ada, (0, chip * ada_cols), (1, ada_cols))
    mod_part = _ada_fwd(c_all, w_ada[0], b_sh, name="ada_fwd")
    mod_all = _small_allgather(mod_part, name="ag_mod").reshape(N_CHIP, 2, 8, ada_cols)[:, 0]
    mod = lax.dynamic_index_in_dim(mod_all, me, axis=1, keepdims=False).reshape(1, 6 * d)

    halves = {n: (2, _BIG_SHARD[n][0] // 2, _BIG_SHARD[n][1]) for n in _BIG}
    shards = {n: w[n][0].astype(BF16).reshape(halves[n]) for n in _BIG}
    land = lambda n: lax.empty((N_CHIP,) + halves[n], BF16)
    ag1 = _split_start([shards[n] for n in first], [land(n) for n in first], _ag_plan(1), 4, mod, name="ag1_start")
    ag2 = _split_start([shards[n] for n in rest], [land(n) for n in rest], _ag_plan(len(rest)), 4 * len(rest),
                       ag1[4], name="ag2_start")
    rpb_after = rpb[0] + ag2[4][0, 0]

    def first_weights(after):
        full1 = _split_wait(ag1[0], ag1[1], ag1[2], ag1[3], _ag_plan(1), after, name="ag1_wait")
        p_in = _ag_pass(full1, name="ag1_pass")[0].reshape((N_CHIP,) + _BIG_SHARD["w_in"])
        cut = W_QKV - 2 * _BIG_SHARD["w_in"][1]
        return dict(w_qkv=jnp.concatenate([p_in[0], p_in[1], p_in[2][:, :cut]], axis=1),
                    w_gates=jnp.concatenate([p_in[2][:, cut:], p_in[3]], axis=1))

    def late_weights(after):
        full2 = _split_wait(ag2[0], ag2[1], ag2[2], ag2[3], _ag_plan(len(rest)), after, name="ag2_wait")
        full2 = _ag_pass(full2, name="ag2_pass")
        full = {n: fu.reshape((N_CHIP,) + _BIG_SHARD[n]) for n, fu in zip(rest, full2)}
        return dict(w_pa=_join_cols(full["w_proj_a"]), w_pb=_join_cols(full["w_proj_b"]), w_o=full["w_o"].reshape(d, d),
                    w_ffn_in=_join_cols(full["w_ffn_in"]), w_ffn_out=full["w_ffn_out"].reshape(D_FF, d))

    def rs_begin(group, grads, tag):
        gps = [grads[n].reshape((N_CHIP,) + halves[n]) for n in group]
        ras = _sibling_send_halves(gps, name=f"rs_sibling_{tag}")
        sums = [_rs_add(gp, ra, c_arr, name=f"rs_add_{n}") for n, gp, ra in zip(group, gps, ras)]
        lands = [lax.empty((3,) + halves[n][1:], BF16) for n in group]
        st = _split_start([sb for _, sb in sums], lands, _rs_plan(len(group)), 3 * len(group), sums[0][0],
                          name=f"rs_{tag}_start")
        return sums, st

    def rs_end(group, begun, after, tag):
        sums, st = begun
        rbs = _split_wait(st[0], st[1], st[2], st[3], _rs_plan(len(group)), after, name=f"rs_{tag}_wait")
        return [_rs_final(sf, rb, chip_arr, name=f"rs_final_{n}") for n, (sf, _), rb in zip(group, sums, rbs)]

    begun = {}

    def early_grads(grads):
        begun["rest"] = rs_begin(rest, grads, "rest")
        return begun["rest"][1][4]

    loss_v, grad_x, grads, dmod, small = _device_step(
        x[0], loss_target[0], mod, first_weights, late_weights, early_grads, g_norm1, g_norm2, b_gate, g_qa, g_ka, g_qb,
        g_kb, rpb_after)
    begun["first"] = rs_begin(first, grads, "first")

    g, delta, new_m, new_v = {}, {}, {}, {}

    def finish(group, ts, tag):
        others = _sibling_swap(ts, name=f"rs_pair_{tag}")
        for n, t, o in zip(group, ts, others):
            gg, dl, nm, nv = _adamw_halves(w[n][0], t, o, m[n][0], v[n][0], c_arr, name=f"adamw_{n}")
            g[n], delta[n], new_m[n], new_v[n] = gg[None], dl[None], nm[None], nv[None]

    finish(rest, rs_end(rest, begun["rest"], begun["first"][1][4], "rest"), "rest")

    stats = _pack_small(dict(b_ada=dmod, **small))
    rows = _small_allgather(jnp.broadcast_to(stats, (8, STATS_W)), name="ag_stats")[::8]
    dmod_sh = lax.dynamic_slice(rows, (0, chip * ada_cols), (8, ada_cols))
    g_ada = _ada_bwd(c_all.T, dmod_sh, name="ada_bwd")
    tot = _row_sum(rows, name="stats_sum")
    g_small = _unpack_small(tot, {n: w[n].shape for n in _SMALL})

    done_rest = sum(new_v[n][0, :1, :1] for n in rest) + tot[:, :1]
    finish(first, rs_end(first, begun["first"], done_rest, "first"), "first")

    dl, nm, nv = _adamw(w_ada[0], g_ada, m_w_ada[0], v_w_ada[0], name="adamw_w_ada")
    g["w_ada"], delta["w_ada"], new_m["w_ada"], new_v["w_ada"] = g_ada[None], dl[None], nm[None], nv[None]
    shapes = {n: w[n].shape for n in _SMALL}
    dl, nm, nv = _adamw(_pack_small({n: w[n] for n in _SMALL}), tot, _pack_small({n: m[n] for n in _SMALL}),
                        _pack_small({n: v[n] for n in _SMALL}), name="adamw_small")
    delta.update(_unpack_small(dl, shapes))
    new_m.update(_unpack_small(nm, shapes))
    new_v.update(_unpack_small(nv, shapes))
    g.update(g_small)

    loss = lax.psum(loss_v[0, 0], ("x", "y", "c"))
    return (loss, grad_x[None], *[g[n] for n in names], *[delta[n] for n in names], *[new_m[n] for n in names],
            *[new_v[n] for n in names])
```

```python
import numpy as np

import jax
import jax.numpy as jnp
from jax import lax
from jax.experimental import pallas as pl
from jax.experimental.pallas import tpu as pltpu

F32 = jnp.float32
BF16 = jnp.bfloat16

D_MODEL = 1024
SEQ = 8192
HEAD_DIM = 64
GRID_W = 64
ROWS = SEQ // GRID_W
NA_HEADS = 8
NA_KH = 8
NA_KW = 16
DIL_CONFIGS = ((128, 1), (512, 4), (2048, 16))
ROT_DIM = 16
ROPE_THETA = 500000.0
D_FF = 2816
EPS = 1e-6
NEG = -1e30
WA = 512
WB = 768
WB_OUT = 256
W_QKV = 3 * WA + 3 * WB
W_QK = 2 * WA + 2 * WB
W_GATES = 2 * D_MODEL
SCALE = HEAD_DIM ** -0.5

ADAM_LR = 0.001
ADAM_B1 = 0.9
ADAM_B2 = 0.999
ADAM_EPS = 1e-08
ADAM_WD = 0.01
ADAM_STEP = 10

LANES = 128
ROW_TILE = 256
ROW_TILES = {"ffn_fwd": 512, "post_attn_fwd": 512, "post_attn_bwd": 512,
             "ffn_in_bwd": 512}
Q_BLOCK = 256
NA_QROWS = Q_BLOCK // GRID_W
NA_KROWS = NA_QROWS + NA_KH - 1
NA_NK = NA_KROWS * GRID_W
NA_PAIRS = (NA_KROWS + 1) // 2
NA_W = NA_PAIRS * LANES
NA_RO_NONE = 15
NA_SLOTS = 21
RP_LANE0 = GRID_W - NA_KW
DIL_HALF = 64
DIL_QB = 512
N_QBLK = SEQ // Q_BLOCK

N_DEV = 8
N_CHIP = 4
FF_CHIP = 2 * D_FF // N_CHIP
STATS_W = 14336


def _pcall(body, *, name, **kw):
    return pl.pallas_call(body, name=name, **kw)


_NT = (((1,), (1,)), ((), ()))
_TN = (((0,), (0,)), ((), ()))
_ARB = pltpu.CompilerParams(dimension_semantics=("arbitrary",))
_PAR = pltpu.CompilerParams(dimension_semantics=("parallel",))


def _dot(a, b):
    return jnp.dot(a, b, preferred_element_type=F32)


def _dot_nt(a, b):
    return lax.dot_general(a, b, _NT, preferred_element_type=F32)


def _wgrad(a, b, *, name, tm, tn, tk=1024, chips=None):
    s, ma = a.shape
    nb = b.shape[1]
    nk = s // tk
    nc = nb // chips if chips else tn
    cpb = tn // nc

    def body(a_ref, b_ref, o_ref, acc):
        k = pl.program_id(2)
        r = lax.dot_general(a_ref[...].astype(BF16), b_ref[...].astype(BF16), _TN, preferred_element_type=F32)

        @pl.when(k == 0)
        def _():
            acc[...] = r

        @pl.when(k > 0)
        def _():
            acc[...] += r

        @pl.when(k == nk - 1)
        def _():
            if chips:
                for q in range(cpb):
                    o_ref[q] = acc[:, q * nc:(q + 1) * nc]
            else:
                o_ref[...] = acc[...]

    if chips:
        o_spec = pl.BlockSpec((cpb, tm, nc), lambda i, j, k: (j, i, 0))
        out_shape = jax.ShapeDtypeStruct((chips, ma, nc), F32)
    else:
        o_spec = pl.BlockSpec((tm, tn), lambda i, j, k: (i, j))
        out_shape = jax.ShapeDtypeStruct((ma, nb), F32)
    return _pcall(
        body, name=name, grid=(ma // tm, nb // tn, nk),
        in_specs=[pl.BlockSpec((tk, tm), lambda i, j, k: (k, i)), pl.BlockSpec((tk, tn), lambda i, j, k: (k, j))],
        out_specs=o_spec, out_shape=out_shape, scratch_shapes=[pltpu.VMEM((tm, tn), F32)],
        compiler_params=pltpu.CompilerParams(dimension_semantics=("parallel", "parallel", "arbitrary")),
    )(a, b)


def _row_call(body, *, name, row_ins, res_ins, row_outs, acc_outs=(), scratch=()):
    row_ins = [a if isinstance(a, tuple) else (a, 1) for a in row_ins]
    row_outs = [o if len(o) == 3 else (*o, 1) for o in row_outs]
    s = row_ins[0][0].shape[0]
    tile = ROW_TILES.get(name, ROW_TILE)
    n = s // tile
    nri, nre, nro, nao = len(row_ins), len(res_ins), len(row_outs), len(acc_outs)

    def whole(shape):
        nd = len(shape)
        return pl.BlockSpec(tuple(shape), lambda i: (0,) * nd, pipeline_mode=pl.Buffered(1))

    def whole_out(shape):
        nd = len(shape)
        return pl.BlockSpec(tuple(shape), lambda i: (0,) * nd)

    def rows(w, d):
        if d == 1:
            return pl.BlockSpec((tile, w), lambda i: (i, 0))
        return pl.BlockSpec((d, tile // d, w), lambda i: (0, i, 0))

    in_specs = [rows(a.shape[1], d) for a, d in row_ins]
    in_specs += [whole(a.shape) for a in res_ins]
    out_specs = [rows(w, d) for w, _, d in row_outs]
    out_specs += [whole_out(shp) for shp, _ in acc_outs]
    out_shape = [jax.ShapeDtypeStruct((s, w) if d == 1 else (d, s // d, w), dt) for w, dt, d in row_outs]
    out_shape += [jax.ShapeDtypeStruct(tuple(shp), dt) for shp, dt in acc_outs]

    def wrapped(*refs):
        at = [0, nri, nri + nre, nri + nre + nro, nri + nre + nro + nao]
        body(pl.program_id(0), n, refs[at[0]:at[1]], refs[at[1]:at[2]], refs[at[2]:at[3]], refs[at[3]:at[4]],
             refs[at[4]:])

    args = [a if d == 1 else a.reshape(d, s // d, a.shape[1]) for a, d in row_ins]
    outs = _pcall(wrapped, name=name, grid=(n,), in_specs=in_specs, out_specs=out_specs, out_shape=out_shape,
                  scratch_shapes=list(scratch), compiler_params=_ARB)(*args, *res_ins)
    return [o.reshape(s, o.shape[-1]) if k < nro and row_outs[k][2] != 1 else o for k, o in enumerate(outs)]


def _stage_shape(name):
    return pltpu.VMEM((4, ROW_TILES.get(name, ROW_TILE), LANES), F32)


def _from_residue(ref, col, stage, slot):
    d, n = ref.shape[0], ref.shape[1]
    for r in range(d):
        stage.at[slot][pl.ds(r, n, stride=d), :] = ref[r, :, col:col + LANES].astype(F32)
    return stage[slot]


def _natural(ref, stage, slot0):
    if len(ref.shape) == 2:
        return ref[...]
    return jnp.concatenate([_from_residue(ref, c * LANES, stage, (slot0 + c) % 4)
                            for c in range(ref.shape[2] // LANES)], axis=1)


def _to_residue(val, ref, col, stage, slot):
    d, n = ref.shape[0], ref.shape[1]
    stage[slot] = val
    for r in range(d):
        ref[r, :, col:col + LANES] = stage.at[slot][pl.ds(r, n, stride=d), :].astype(ref.dtype)


def _fold8(t):
    r, w = t.shape
    return jnp.sum(t.reshape(r // 8, 8, w), axis=0)


def _sigmoid(t):
    return 0.5 * (jnp.tanh(0.5 * t) + 1.0)


def _head_lanes():
    return lax.broadcasted_iota(jnp.int32, (1, LANES), 1) < HEAD_DIM


def _head_mean(t, lo):
    s_lo = jnp.sum(jnp.where(lo, t, 0.0), axis=1, keepdims=True)
    s_hi = jnp.sum(jnp.where(lo, 0.0, t), axis=1, keepdims=True)
    return jnp.where(lo, s_lo, s_hi) * (1.0 / HEAD_DIM)


def _rms_mod(xv, g, sc, sh):
    rstd = lax.rsqrt(jnp.mean(xv * xv, axis=1, keepdims=True) + EPS)
    return (xv * rstd * g) * (1.0 + sc) + sh


def _rms_mod_bwd(xv, dh, g, sc):
    rstd = lax.rsqrt(jnp.mean(xv * xv, axis=1, keepdims=True) + EPS)
    xhat = xv * rstd
    dn = dh * (1.0 + sc)
    dxhat = dn * g
    dx = rstd * (dxhat - xhat * jnp.mean(dxhat * xhat, axis=1, keepdims=True))
    return dx, dh, dh * (xhat * g), dn * xhat


def _mix_weights(ls):
    m = jnp.maximum(jnp.maximum(ls[0], ls[1]), ls[2])
    es = [jnp.exp(t - m) for t in ls]
    den = es[0] + es[1] + es[2]
    return [e / den for e in es]


def _rope_tables():
    half = ROT_DIM // 2
    inv_freq = ROPE_THETA ** (-(jnp.arange(half, dtype=F32) * 2.0) / ROT_DIM)
    lane = np.arange(LANES) % HEAD_DIM
    ang = jnp.arange(SEQ).astype(F32)[:, None] * jnp.tile(inv_freq, LANES // half)[None, :]
    cos, sin = jnp.cos(ang), jnp.sin(ang)
    first, second = jnp.asarray(lane < half)[None, :], jnp.asarray((lane >= half) & (lane < ROT_DIM))[None, :]
    cos_t = jnp.where(first | second, cos, 1.0)
    return cos_t, jnp.where(second, sin, 0.0), jnp.where(first, -sin, 0.0)


_SECTIONS = ((0, WA, 0, False), (WA, 2 * WA, 1, False), (2 * WA, 3 * WA, -1, False),
             (3 * WA, 3 * WA + WB, 2, True), (3 * WA + WB, 3 * WA + 2 * WB, 3, True), (3 * WA + 2 * WB, W_QKV, -1, False))


def _pre_attn_fwd(x, cos_t, sa_t, sb_t, g1, sc1, sh1, w_qkv, w_gates, gains, *, name):
    half = ROT_DIM // 2
    dilated = [(g, dd) for g, (_, dd) in enumerate(DIL_CONFIGS) if dd > 1]

    def body(i, n, rin, res, rout, aout, scr):
        x_ref, cos_ref, sa_ref, sb_ref = rin
        g_ref, sc_ref, sh_ref, wq_ref, wg_ref, gains_ref = res
        h1_ref, qkvn_ref, pre_ref, gates_ref = rout[:4]
        group_ref = {g: rout[4 + k] for k, (g, _) in enumerate(dilated)}
        (stage,) = scr
        staged = 0
        hb = _rms_mod(x_ref[...], g_ref[...], sc_ref[...], sh_ref[...]).astype(BF16)
        h1_ref[...] = hb
        gates_ref[...] = _dot(hb, wg_ref[...]).astype(BF16)
        lo = _head_lanes()
        cosv, sav, sbv = cos_ref[...], sa_ref[...], sb_ref[...]
        pre_at = 0
        for si, (c0, c1, kind, rot) in enumerate(_SECTIONS):
            sec = _dot(hb, wq_ref[:, c0:c1])
            for ch in range((c1 - c0) // LANES):
                t = sec[:, ch * LANES:(ch + 1) * LANES]
                if kind >= 0:
                    pre_ref[:, pre_at:pre_at + LANES] = t.astype(BF16)
                    pre_at += LANES
                    t = t * lax.rsqrt(_head_mean(t * t, lo) + EPS) * gains_ref[kind:kind + 1, :]
                    if rot:
                        t = t * cosv + pltpu.roll(t, half, 1) * sav + pltpu.roll(t, LANES - half, 1) * sbv
                qkvn_ref[:, c0 + ch * LANES:c0 + (ch + 1) * LANES] = t.astype(BF16)
                group = ch * LANES // WB_OUT if si >= 3 else 0
                if group in group_ref:
                    col = (si - 3) * WB_OUT + ch * LANES % WB_OUT
                    _to_residue(t, group_ref[group], col, stage, staged % 4)
                    staged += 1

    return _row_call(body, name=name, row_ins=[x, cos_t, sa_t, sb_t], res_ins=[g1, sc1, sh1, w_qkv, w_gates, gains],
                     row_outs=[(D_MODEL, BF16), (W_QKV, BF16), (W_QK, BF16), (W_GATES, BF16)]
                     + [(3 * WB_OUT, BF16, dd) for _, dd in dilated], scratch=[_stage_shape(name)])


def _pre_attn_bwd(qk_pre, d_parts, dgates, x, dx1, cos_t, sa_t, sb_t, w_qkv, w_gates, gains, g1, sc1, *, name):
    half = ROT_DIM // 2
    nparts = len(d_parts)
    where = []
    residue = [isinstance(part, tuple) for part in d_parts]
    for pi, part in enumerate(d_parts):
        width = (part[0] if residue[pi] else part).shape[1]
        where += [(pi, cj) for cj in range(width // LANES)]
    assert len(where) == W_QKV // LANES

    def body(i, n, rin, res, rout, aout, scr):
        pre_ref, d_refs = rin[0], rin[1:1 + nparts]
        dgates_ref, x_ref, dx1_ref, cos_ref, sa_ref, sb_ref = rin[1 + nparts:]
        wq_ref, wg_ref, gains_ref, g_ref, sc_ref = res
        dqkv_ref, gx_ref = rout
        dgains_ref, sums_ref = aout
        accg, accs, stage = scr
        staged = 0

        @pl.when(i == 0)
        def _():
            accg[...] = jnp.zeros_like(accg)
            accs[...] = jnp.zeros_like(accs)

        lo = _head_lanes()
        cosv, sav, sbv = cos_ref[...], sa_ref[...], sb_ref[...]
        dh = _dot_nt(dgates_ref[...], wg_ref[...])
        pre_at = 0
        for c0, c1, kind, rot in _SECTIONS:
            for ch in range((c1 - c0) // LANES):
                pi, cj = where[c0 // LANES + ch]
                if residue[pi]:
                    dt = _from_residue(d_refs[pi], cj * LANES, stage, staged % 4)
                    staged += 1
                else:
                    dt = d_refs[pi][:, cj * LANES:(cj + 1) * LANES]
                if kind >= 0:
                    if rot:
                        dt = dt * cosv + pltpu.roll(dt * sav, LANES - half, 1) + pltpu.roll(dt * sbv, half, 1)
                    t = pre_ref[:, pre_at:pre_at + LANES].astype(F32)
                    pre_at += LANES
                    rstd = lax.rsqrt(_head_mean(t * t, lo) + EPS)
                    xhat = t * rstd
                    accg[kind] += _fold8(dt * xhat)
                    dxhat = dt * gains_ref[kind:kind + 1, :]
                    dt = rstd * (dxhat - xhat * _head_mean(dxhat * xhat, lo))
                dqkv_ref[:, c0 + ch * LANES:c0 + (ch + 1) * LANES] = dt.astype(BF16)
            dh = dh + _dot_nt(dqkv_ref[:, c0:c1], wq_ref[:, c0:c1])
        dx, t_sh, t_sc, t_g = _rms_mod_bwd(x_ref[...], dh, g_ref[...], sc_ref[...])
        gx_ref[...] = dx1_ref[...] + dx
        accs[0] += _fold8(t_sh)
        accs[1] += _fold8(t_sc)
        accs[2] += _fold8(t_g)

        @pl.when(i == n - 1)
        def _():
            t = jnp.sum(accg[...], axis=1)
            dgains_ref[...] = t + pltpu.roll(t, HEAD_DIM, 1)
            sums_ref[...] = jnp.sum(accs[...], axis=1)

    return _row_call(
        body, name=name, row_ins=[qk_pre, *d_parts, dgates, x, dx1, cos_t, sa_t, sb_t],
        res_ins=[w_qkv, w_gates, gains, g1, sc1], row_outs=[(W_QKV, BF16), (D_MODEL, F32)],
        acc_outs=[((4, LANES), F32), ((3, D_MODEL), F32)],
        scratch=[pltpu.VMEM((4, 8, LANES), F32), pltpu.VMEM((3, 8, D_MODEL), F32), _stage_shape(name)])


def _post_attn_fwd(o_a, o_g, l_g, gates, x, w_pa, w_pb, w_o, b_gate, gt1, g2, sc2, sh2, *, name):
    d = D_MODEL

    def body(i, n, rin, res, rout, aout, scr):
        oa_ref, o0, o1, o2, l0, l1, l2, gates_ref, x_ref = rin
        wpa_ref, wpb_ref, wo_ref, b_ref, gt_ref, g_ref, sc_ref, sh_ref = res
        ob_ref, merged_ref, mo_ref, x1_ref, h2_ref = rout
        (stage,) = scr
        ogs = [_natural(r, stage, 0) for r in (o0, o1, o2)]
        ws = _mix_weights([_natural(r, stage, 2) for r in (l0, l1, l2)])
        obb = (ws[0] * ogs[0] + ws[1] * ogs[1] + ws[2] * ogs[2]).astype(BF16)
        ob_ref[...] = obb
        pa = _dot(oa_ref[...].astype(BF16), wpa_ref[...])
        pb = _dot(obb, wpb_ref[...])
        ga = _sigmoid(gates_ref[:, :d].astype(F32) + b_ref[:, :d])
        gb = _sigmoid(gates_ref[:, d:].astype(F32) + b_ref[:, d:])
        merged = (ga * pa + gb * pb).astype(BF16)
        merged_ref[...] = merged
        mo = _dot(merged, wo_ref[...])
        mo_ref[...] = mo.astype(BF16)
        x1 = x_ref[...] + gt_ref[...] * mo
        x1_ref[...] = x1
        h2_ref[...] = _rms_mod(x1, g_ref[...], sc_ref[...], sh_ref[...]).astype(BF16)

    return _row_call(body, name=name, row_ins=[o_a, *o_g, *l_g, gates, x],
                     res_ins=[w_pa, w_pb, w_o, b_gate, gt1, g2, sc2, sh2],
                     row_outs=[(WB_OUT, BF16), (d, BF16), (d, BF16), (d, F32), (d, BF16)], scratch=[_stage_shape(name)])


def _ffn_fwd(h2, w_ffn_in, *, name):
    def body(i, n, rin, res, rout, aout, scr):
        (h_ref,), (w_ref,), (act_ref, ff_ref) = rin, res, rout
        hv = h_ref[...]
        for q in range(2):
            a = _dot(hv, w_ref[:, q * FF_CHIP:(q + 1) * FF_CHIP])
            up = _dot(hv, w_ref[:, D_FF + q * FF_CHIP:D_FF + (q + 1) * FF_CHIP])
            sl = slice(q * FF_CHIP, (q + 1) * FF_CHIP)
            act_ref[:, sl] = (a * _sigmoid(a) * up).astype(BF16)
            ff_ref[:, sl] = a.astype(BF16)
            ff_ref[:, D_FF + q * FF_CHIP:D_FF + (q + 1) * FF_CHIP] = up.astype(BF16)

    return _row_call(body, name=name, row_ins=[h2], res_ins=[w_ffn_in], row_outs=[(D_FF, BF16), (2 * D_FF, BF16)])


def _ffn_mid(act, ff, x1, tgt, w_ffn_out, gt2, *, name):
    d = D_MODEL

    def body(i, n, rin, res, rout, aout, scr):
        act_ref, ff_ref, x1_ref, tgt_ref = rin
        wo_ref, gt_ref = res
        dy_ref, dffo_ref, dff_ref = rout
        dgt_ref, loss_ref = aout
        (acc,) = scr

        @pl.when(i == 0)
        def _():
            acc[...] = jnp.zeros_like(acc)

        ffo = _dot(act_ref[...], wo_ref[...])
        gtv = gt_ref[...]
        e = x1_ref[...] + gtv * ffo - tgt_ref[...]
        dy = e * (1.0 / d)
        dy_ref[...] = dy
        dffo = (gtv * dy).astype(BF16)
        dffo_ref[...] = dffo
        acc[0] += _fold8(dy * ffo)
        acc[1] += _fold8(e * e)
        for q in range(2):
            sl = slice(q * FF_CHIP, (q + 1) * FF_CHIP)
            su = slice(D_FF + q * FF_CHIP, D_FF + (q + 1) * FF_CHIP)
            dact = _dot_nt(dffo, wo_ref[sl, :])
            a = ff_ref[:, sl].astype(F32)
            up = ff_ref[:, su].astype(F32)
            sg = _sigmoid(a)
            dff_ref[:, sl] = (dact * up * (sg * (1.0 + a * (1.0 - sg)))).astype(BF16)
            dff_ref[:, su] = (dact * (a * sg)).astype(BF16)

        @pl.when(i == n - 1)
        def _():
            dgt_ref[...] = jnp.sum(acc[0], axis=0, keepdims=True)
            tot = jnp.sum(jnp.sum(acc[1], axis=0, keepdims=True), axis=1, keepdims=True)
            loss_ref[...] = jnp.broadcast_to(tot * (0.5 / d), (1, LANES))

    return _row_call(body, name=name, row_ins=[act, ff, x1, tgt], res_ins=[w_ffn_out, gt2],
                     row_outs=[(d, F32), (d, BF16), (2 * D_FF, BF16)], acc_outs=[((1, d), F32), ((1, LANES), F32)],
                     scratch=[pltpu.VMEM((2, 8, d), F32)])


def _ffn_in_bwd(dff, x1, dy, mo, w_ffn_in, g2, sc2, gt1, *, name):
    d = D_MODEL

    def body(i, n, rin, res, rout, aout, scr):
        dff_ref, x1_ref, dy_ref, mo_ref = rin
        w_ref, g_ref, sc_ref, gt_ref = res
        dx1_ref, dmo_ref = rout
        (sums_ref,) = aout
        (acc,) = scr

        @pl.when(i == 0)
        def _():
            acc[...] = jnp.zeros_like(acc)

        dh = _dot_nt(dff_ref[...], w_ref[...])
        dx, t_sh, t_sc, t_g = _rms_mod_bwd(x1_ref[...], dh, g_ref[...], sc_ref[...])
        dx1 = dy_ref[...] + dx
        dx1_ref[...] = dx1
        dmo_ref[...] = (gt_ref[...] * dx1).astype(BF16)
        acc[0] += _fold8(t_sh)
        acc[1] += _fold8(t_sc)
        acc[2] += _fold8(t_g)
        acc[3] += _fold8(dx1 * mo_ref[...].astype(F32))

        @pl.when(i == n - 1)
        def _():
            sums_ref[...] = jnp.sum(acc[...], axis=1)

    return _row_call(body, name=name, row_ins=[dff, x1, dy, mo], res_ins=[w_ffn_in, g2, sc2, gt1],
                     row_outs=[(d, F32), (d, BF16)], acc_outs=[((4, d), F32)], scratch=[pltpu.VMEM((4, 8, d), F32)])


def _post_attn_bwd(dmo, gates, o_a, o_g, l_g, w_pa, w_pb, w_o, b_gate, *, name):
    d = D_MODEL

    def body(i, n, rin, res, rout, aout, scr):
        dmo_ref, gates_ref, oa_ref, o0, o1, o2, l0, l1, l2 = rin
        wpa_ref, wpb_ref, wo_ref, b_ref = res
        dpa_ref, dpb_ref, dgates_ref, doa_ref = rout[:4]
        do_refs, dl_refs = rout[4:7], rout[7:10]
        (dbg_ref,) = aout
        acc, stage = scr

        @pl.when(i == 0)
        def _():
            acc[...] = jnp.zeros_like(acc)

        ogs = [_natural(r, stage, 0) for r in (o0, o1, o2)]
        ws = _mix_weights([_natural(r, stage, 2) for r in (l0, l1, l2)])
        obb = (ws[0] * ogs[0] + ws[1] * ogs[1] + ws[2] * ogs[2]).astype(BF16)
        pa = _dot(oa_ref[...].astype(BF16), wpa_ref[...])
        pb = _dot(obb, wpb_ref[...])
        ga = _sigmoid(gates_ref[:, :d].astype(F32) + b_ref[:, :d])
        gb = _sigmoid(gates_ref[:, d:].astype(F32) + b_ref[:, d:])
        dm = _dot_nt(dmo_ref[...], wo_ref[...])
        dpa = (dm * ga).astype(BF16)
        dpb = (dm * gb).astype(BF16)
        dpa_ref[...] = dpa
        dpb_ref[...] = dpb
        dga = dm * pa * ga * (1.0 - ga)
        dgb = dm * pb * gb * (1.0 - gb)
        dgates_ref[:, :d] = dga.astype(BF16)
        dgates_ref[:, d:] = dgb.astype(BF16)
        acc[:, :d] += _fold8(dga)
        acc[:, d:] += _fold8(dgb)
        doa_ref[...] = _dot_nt(dpa, wpa_ref[...])
        dob = _dot_nt(dpb, wpb_ref[...])
        lo = _head_lanes()
        for ch in range(WB_OUT // LANES):
            sl = slice(ch * LANES, (ch + 1) * LANES)
            dv = dob[:, sl]
            wc = [w[:, sl] for w in ws]
            ts = [_head_mean(dv * og[:, sl], lo) * float(HEAD_DIM) for og in ogs]
            tbar = wc[0] * ts[0] + wc[1] * ts[1] + wc[2] * ts[2]
            for g in range(3):
                for k, (ref, val) in enumerate(((do_refs[g], wc[g] * dv), (dl_refs[g], wc[g] * (ts[g] - tbar)))):
                    if len(ref.shape) == 2:
                        ref[:, sl] = val
                    else:
                        _to_residue(val, ref, ch * LANES, stage, (2 * g + k) % 4)

        @pl.when(i == n - 1)
        def _():
            dbg_ref[...] = jnp.sum(acc[...], axis=0, keepdims=True)

    return _row_call(body, name=name, row_ins=[dmo, gates, o_a, *o_g, *l_g], res_ins=[w_pa, w_pb, w_o, b_gate],
                     row_outs=[(d, BF16), (d, BF16), (2 * d, BF16), (WA, F32)]
                     + 2 * [(WB_OUT, F32, dd) for _, dd in DIL_CONFIGS],
                     acc_outs=[((1, 2 * d), F32)], scratch=[pltpu.VMEM((8, 2 * d), F32), _stage_shape(name)])


def _na_class_tables():
    ro = np.full((3, NA_QROWS, 2 * NA_PAIRS), NA_RO_NONE, np.int64)
    slot = np.zeros((3, NA_QROWS, NA_PAIRS), np.int64)
    for t in range(3):
        for a in range(NA_QROWS):
            qr = _NA_CLASS_R0[t] + a
            rs = min(max(qr - NA_KH // 2, 0), ROWS - NA_KH)
            for b in range(NA_KROWS):
                kr = _NA_CLASS_K0[t] + b
                if rs <= kr < rs + NA_KH:
                    ro[t, a, b] = kr - qr + (NA_KH - 1)
            for j in range(NA_PAIRS):
                slot[t, a, j] = 2 * j - a + (_NA_CLASS_K0[t] - _NA_CLASS_R0[t] + NA_KH - 1) + (NA_QROWS - 1)
    assert slot.min() >= 0 and slot.max() < NA_SLOTS
    return ro, slot


def _na_build_bias(i, cls_ref, rp_ref, cm_ref, bias_scr):
    ro, _ = _na_class_tables()
    lo = _head_lanes()
    first = jnp.logical_or(i == 0, cls_ref[i] != cls_ref[jnp.maximum(i - 1, 0)])
    for t in range(3):
        @pl.when(jnp.logical_and(first, cls_ref[i] == t))
        def _():
            for hh in range(2):
                for a in range(NA_QROWS):
                    for j in range(NA_PAIRS):
                        r0, r1 = int(ro[t, a, 2 * j]), int(ro[t, a, 2 * j + 1])
                        x0 = jnp.broadcast_to(rp_ref[hh, r0:r0 + 1, :], (GRID_W, LANES))
                        x1 = jnp.broadcast_to(rp_ref[hh, r1:r1 + 1, :], (GRID_W, LANES))
                        blk = jnp.where(lo, pltpu.roll(x0, GRID_W + 1, 1, stride=1, stride_axis=0),
                                        pltpu.roll(x1, 1, 1, stride=1, stride_axis=0))
                        bias_scr[hh, a * GRID_W:(a + 1) * GRID_W, j * LANES:(j + 1) * LANES] = blk + cm_ref[...]
    return first


def _attn_fwd(qkv, qc0, kc0, vc0, npairs, table, kstart, cls, nk, *, name, na=None, qb=Q_BLOCK):
    s = qkv.shape[0]

    def body(ks_ref, cls_ref, q_ref, k_ref, v_ref, b_ref, *rest):
        if na:
            cm_ref, o_ref, lse_ref, bias_scr = rest
        else:
            o_ref, lse_ref = rest
        i = pl.program_id(1)
        if na:
            _na_build_bias(i, cls_ref, b_ref, cm_ref, bias_scr)
        ks = pl.multiple_of(ks_ref[i], 64)
        q2 = q_ref[...]
        k2 = k_ref[pl.ds(ks, nk), :]
        v2 = v_ref[pl.ds(ks, nk), :]
        lo = _head_lanes()
        outs, lses = [], []
        for h in range(2):
            qm = jnp.where(lo if h == 0 else jnp.logical_not(lo), q2, jnp.zeros_like(q2))
            sc = _dot_nt(qm, k2) * SCALE + (bias_scr[h, :, :nk] if na else b_ref[0, 0])
            m = jnp.max(sc, axis=1, keepdims=True)
            p = jnp.exp(sc - m)
            l = jnp.sum(p, axis=1, keepdims=True)
            pv = _dot(p.astype(BF16), v2)
            outs.append(pv / l)
            lses.append(m + jnp.log(l))
        o_ref[...] = jnp.where(lo, outs[0], outs[1])
        lse_ref[...] = jnp.where(lo, lses[0], lses[1])

    w = npairs * LANES
    in_specs = [
        pl.BlockSpec((qb, LANES), lambda p, i, ks, cl: (i, qc0 + p)),
        pl.BlockSpec((s, LANES), lambda p, i, ks, cl: (0, kc0 + p)),
        pl.BlockSpec((s, LANES), lambda p, i, ks, cl: (0, vc0 + p)),
    ]
    if na:
        in_specs += _na_bias_specs()
        args, scratch = (kstart, cls, qkv, qkv, qkv, *na), [pltpu.VMEM((2, Q_BLOCK, NA_W), F32)]
    else:
        in_specs.append(pl.BlockSpec((1, 1, qb, nk), lambda p, i, ks, cl: (cl[i], 0, 0, 0)))
        args, scratch = (kstart, cls, qkv, qkv, qkv, table), []
    grid_spec = pltpu.PrefetchScalarGridSpec(
        num_scalar_prefetch=2, grid=(npairs, s // qb), in_specs=in_specs,
        out_specs=[pl.BlockSpec((qb, LANES), lambda p, i, ks, cl: (i, p)),
                   pl.BlockSpec((qb, LANES), lambda p, i, ks, cl: (i, p))],
        scratch_shapes=scratch,
    )
    return _pcall(body, name=name, grid_spec=grid_spec,
                  out_shape=[jax.ShapeDtypeStruct((s, w), F32), jax.ShapeDtypeStruct((s, w), F32)],
                  compiler_params=pltpu.CompilerParams(dimension_semantics=("parallel", "arbitrary")),
                  )(*args)


def _na_bias_specs():
    return [pl.BlockSpec((2, 16, LANES), lambda p, i, ks, cl: (p, 0, 0)),
            pl.BlockSpec((GRID_W, LANES), lambda p, i, ks, cl: (0, 0))]


def _attn_bwd(qkv, qc0, kc0, vc0, npairs, table, kstart, cls, nk, do, o, lse, *, name, dlse=None, na=None,
              qb=Q_BLOCK, order=None):
    s = qkv.shape[0]
    has_dlse = dlse is not None
    _, slot = _na_class_tables()

    def body(ks_ref, cls_ref, q_ref, k_ref, v_ref, b_ref, *rest):
        if na:
            cm_ref, rest = rest[0], rest[1:]
        do_ref, o_ref, lse_ref, rest = rest[0], rest[1], rest[2], rest[3:]
        if has_dlse:
            dlse_ref, rest = rest[0], rest[1:]
        if order is not None:
            rest = rest[1:]
        dq_ref, dk_ref, dv_ref = rest[0], rest[1], rest[2]
        if na:
            bank_ref, bias_scr, dbias_scr, bank_scr = rest[3:]
        i = pl.program_id(1)
        if na:
            first = _na_build_bias(i, cls_ref, b_ref, cm_ref, bias_scr)

        @pl.when(i == 0)
        def _():
            dk_ref[...] = jnp.zeros_like(dk_ref)
            dv_ref[...] = jnp.zeros_like(dv_ref)
            if na:
                dbias_scr[...] = jnp.zeros_like(dbias_scr)
                bank_scr[...] = jnp.zeros_like(bank_scr)

        ks = pl.multiple_of(ks_ref[i], 64)
        q2 = q_ref[...]
        k2 = k_ref[pl.ds(ks, nk), :]
        v2 = v_ref[pl.ds(ks, nk), :]
        do2 = do_ref[...]
        lse2 = lse_ref[...]
        doo = do2 * o_ref[...]
        do2b = do2.astype(BF16)
        lo = _head_lanes()
        lane = lax.broadcasted_iota(jnp.int32, (1, LANES), 1)
        dqs, dks, dvs = [], [], []
        for h in range(2):
            mh = lo if h == 0 else jnp.logical_not(lo)
            qm = jnp.where(mh, q2, jnp.zeros_like(q2))
            sc = _dot_nt(qm, k2) * SCALE + (bias_scr[h, :, :nk] if na else b_ref[0, 0])
            lse_h = jnp.max(jnp.where(mh, lse2, NEG), axis=1, keepdims=True)
            p = jnp.exp(sc - lse_h)
            delta = jnp.sum(jnp.where(mh, doo, 0.0), axis=1, keepdims=True)
            dom = jnp.where(mh, do2b, jnp.zeros_like(do2b))
            dp = _dot_nt(dom, v2)
            t = dp - delta
            if has_dlse:
                t = t + jnp.sum(jnp.where(lane == h * HEAD_DIM, dlse_ref[...], 0.0), axis=1, keepdims=True)
            ds = p * t
            if na:
                @pl.when(first)
                def _():
                    dbias_scr[h, :, :nk] = ds

                @pl.when(jnp.logical_not(first))
                def _():
                    dbias_scr[h, :, :nk] += ds
            dsb = ds.astype(BF16)
            dqs.append(_dot(dsb, k2))
            dks.append(lax.dot_general(dsb, q2, _TN, preferred_element_type=F32))
            dvs.append(lax.dot_general(p.astype(BF16), do2b, _TN, preferred_element_type=F32))
        dq_ref[...] = jnp.where(lo, dqs[0], dqs[1]) * SCALE
        dk_ref[pl.ds(ks, nk), :] += jnp.where(lo, dks[0], dks[1]) * SCALE
        dv_ref[pl.ds(ks, nk), :] += jnp.where(lo, dvs[0], dvs[1])
        if na:
            last = jnp.logical_or(i == N_QBLK - 1, cls_ref[i] != cls_ref[jnp.minimum(i + 1, N_QBLK - 1)])
            for t in range(3):
                @pl.when(jnp.logical_and(last, cls_ref[i] == t))
                def _():
                    for hh in range(2):
                        for a in range(NA_QROWS):
                            for j in range(NA_PAIRS):
                                bank_scr[hh, int(slot[t, a, j])] += dbias_scr[
                                    hh, a * GRID_W:(a + 1) * GRID_W, j * LANES:(j + 1) * LANES]

            @pl.when(i == N_QBLK - 1)
            def _():
                bank_ref[...] = bank_scr[...]

    w = npairs * LANES
    blk = lambda: pl.BlockSpec((qb, LANES), lambda p, i, ks, cl: (i, p))
    full = lambda: pl.BlockSpec((s, LANES), lambda p, i, ks, cl: (0, p))
    in_specs = [
        pl.BlockSpec((qb, LANES), lambda p, i, ks, cl: (i, qc0 + p)),
        pl.BlockSpec((s, LANES), lambda p, i, ks, cl: (0, kc0 + p)),
        pl.BlockSpec((s, LANES), lambda p, i, ks, cl: (0, vc0 + p)),
    ]
    if na:
        in_specs += _na_bias_specs()
        args = [kstart, cls, qkv, qkv, qkv, *na]
    else:
        in_specs.append(pl.BlockSpec((1, 1, qb, nk), lambda p, i, ks, cl: (cl[i], 0, 0, 0)))
        args = [kstart, cls, qkv, qkv, qkv, table]
    in_specs += [blk(), blk(), blk()]
    args += [do, o, lse]
    if has_dlse:
        in_specs.append(blk())
        args.append(dlse)
    out_specs = [blk(), full(), full()]
    out_shape = [jax.ShapeDtypeStruct((s, w), F32)] * 3
    scratch = []
    if na:
        bank_shape = (2, NA_SLOTS, GRID_W, LANES)
        out_specs.append(pl.BlockSpec(bank_shape, lambda p, i, ks, cl: (p, 0, 0, 0)))
        out_shape.append(jax.ShapeDtypeStruct((2 * npairs,) + bank_shape[1:], F32))
        scratch = [pltpu.VMEM((2, Q_BLOCK, NA_W), F32), pltpu.VMEM((2, Q_BLOCK, NA_W), F32), pltpu.VMEM(bank_shape, F32)]
    if order is not None:
        in_specs.append(pl.BlockSpec(order.shape, lambda p, i, ks, cl: (0, 0)))
        args.append(order)
    grid_spec = pltpu.PrefetchScalarGridSpec(num_scalar_prefetch=2, grid=(npairs, s // qb), in_specs=in_specs,
                                             out_specs=out_specs, scratch_shapes=scratch)
    return _pcall(body, name=name, grid_spec=grid_spec, out_shape=out_shape,
                  compiler_params=pltpu.CompilerParams(dimension_semantics=("arbitrary", "arbitrary")))(*args)


_NA_CLASS_R0 = (0, NA_QROWS, ROWS - NA_QROWS)
_NA_CLASS_K0 = (0, 0, ROWS - NA_KROWS)
_RPB_RO = 2 * NA_KH - 1
_RPB_CO = 2 * NA_KW - 1
_BANK_ROWS = 48


def _na_constants():
    col = np.arange(GRID_W)
    cs = np.clip(col - NA_KW // 2, 0, GRID_W - NA_KW)
    vcol = (col[None, :] >= cs[:, None]) & (col[None, :] < cs[:, None] + NA_KW)
    colmask = np.where(np.concatenate([vcol, vcol], axis=1), 0.0, NEG).astype(np.float32)
    co = col[None, :] - col[:, None] + (NA_KW - 1)
    oh_col = np.zeros((GRID_W * GRID_W, LANES), np.float32)
    for qc in range(GRID_W):
        for kc in range(GRID_W):
            if vcol[qc, kc]:
                oh_col[qc * GRID_W + kc, co[qc, kc]] = 1.0
    ks = np.clip(np.arange(N_QBLK) * NA_QROWS - NA_KH // 2, 0, ROWS - NA_KROWS) * GRID_W
    cls = np.ones(N_QBLK, np.int32)
    cls[0], cls[-1] = 0, 2
    return colmask, oh_col, ks.astype(np.int32), cls


def _bank_reduce(bank, oh_col, *, name):
    def body(d_ref, ohc_ref, o_ref):
        o_ref[0] = jnp.dot(d_ref[0], ohc_ref[...], preferred_element_type=F32, precision=lax.Precision.HIGHEST)

    return _pcall(
        body, name=name, grid=(NA_HEADS,),
        in_specs=[pl.BlockSpec((1, _BANK_ROWS, GRID_W * GRID_W), lambda h: (h, 0, 0)),
                  pl.BlockSpec((GRID_W * GRID_W, LANES), lambda h: (0, 0))],
        out_specs=pl.BlockSpec((1, _BANK_ROWS, LANES), lambda h: (h, 0, 0)),
        out_shape=jax.ShapeDtypeStruct((NA_HEADS, _BANK_ROWS, LANES), F32), compiler_params=_PAR,
    )(bank, oh_col)


def _dil_constants(dilation):
    seg = SEQ // dilation
    nb = seg // DIL_QB
    nk = min(DIL_QB + 2 * DIL_HALF, seg)
    starts = [min(max(blk * DIL_QB - DIL_HALF, 0), seg - nk) for blk in range(nb)]
    shifts = sorted({w0 - blk * DIL_QB for blk, w0 in enumerate(starts)}, reverse=True)
    qi = np.arange(DIL_QB)[:, None]
    ki = np.arange(nk)[None, :]
    mask = np.stack([np.where(np.abs(ki + sh - qi) <= DIL_HALF, 0.0, NEG) for sh in shifts]).astype(np.float32)
    ks, cls = [], []
    for i in range(SEQ // DIL_QB):
        sub, blk = divmod(i, nb)
        cls.append(shifts.index(starts[blk] - blk * DIL_QB))
        ks.append(sub * seg + starts[blk])
    return mask.reshape(len(shifts), 1, DIL_QB, nk), np.asarray(ks, np.int32), np.asarray(cls, np.int32), nk


_VM = pl.BlockSpec(memory_space=pltpu.VMEM)


def _ada_fwd(c_all, w, b, *, name):
    def body(c_ref, w_ref, b_ref, o_ref):
        cv = c_ref[...]
        o_ref[...] = jnp.dot(cv * _sigmoid(cv), w_ref[...], preferred_element_type=F32,
                             precision=lax.Precision.HIGHEST) + b_ref[...]

    return _pcall(body, name=name, in_specs=[_VM, _VM, _VM], out_specs=_VM,
                  out_shape=jax.ShapeDtypeStruct((c_all.shape[0], w.shape[1]), F32))(c_all, w, b)


def _ada_bwd(c_all_t, dmod, *, name):
    def body(c_ref, d_ref, o_ref):
        cv = c_ref[...]
        o_ref[...] = jnp.dot(cv * _sigmoid(cv), d_ref[...], preferred_element_type=F32,
                             precision=lax.Precision.HIGHEST)

    return _pcall(body, name=name, in_specs=[_VM, _VM], out_specs=_VM,
                  out_shape=jax.ShapeDtypeStruct((c_all_t.shape[0], dmod.shape[1]), F32))(c_all_t, dmod)


def _row_sum(t, *, name):
    def body(t_ref, o_ref):
        o_ref[...] = jnp.sum(t_ref[...], axis=0, keepdims=True)

    return _pcall(body, name=name, in_specs=[_VM], out_specs=_VM,
                  out_shape=jax.ShapeDtypeStruct((1, t.shape[1]), F32))(t)


def _row_tile(rows):
    tr = rows
    for cand in range(8, 513, 8):
        if rows % cand == 0:
            tr = cand
    return tr


def _adamw_math(wv, gv, mv, vv):
    nm = ADAM_B1 * mv + (1.0 - ADAM_B1) * gv
    nv = ADAM_B2 * vv + (1.0 - ADAM_B2) * (gv * gv)
    m_hat = nm / (1.0 - ADAM_B1 ** ADAM_STEP)
    v_hat = nv / (1.0 - ADAM_B2 ** ADAM_STEP)
    return -ADAM_LR * (m_hat / (jnp.sqrt(v_hat) + ADAM_EPS) + ADAM_WD * wv), nm, nv


def _adamw(w, g, m, v, *, name):
    rows, cols = w.shape
    tr = _row_tile(rows)

    def body(w_ref, g_ref, m_ref, v_ref, d_ref, nm_ref, nv_ref):
        d_ref[...], nm_ref[...], nv_ref[...] = _adamw_math(w_ref[...], g_ref[...], m_ref[...], v_ref[...])

    spec = pl.BlockSpec((tr, cols), lambda i: (i, 0))
    return _pcall(body, name=name, grid=(rows // tr,), in_specs=[spec] * 4, out_specs=[spec] * 3,
                  out_shape=[jax.ShapeDtypeStruct((rows, cols), F32)] * 3, compiler_params=_PAR)(w, g, m, v)


def _adamw_halves(w, g_mine, g_other, m, v, c_arr, *, name):
    rows, cols = w.shape
    hr = rows // 2
    tr = _row_tile(hr)
    nt = hr // tr

    def body(c_ref, w_ref, t_ref, o_ref, m_ref, v_ref, g_ref, d_ref, nm_ref, nv_ref):
        gv = jnp.where(pl.program_id(0) == c_ref[0], t_ref[...], o_ref[...])
        g_ref[...] = gv
        d_ref[...], nm_ref[...], nv_ref[...] = _adamw_math(w_ref[...], gv, m_ref[...], v_ref[...])

    full = pl.BlockSpec((tr, cols), lambda h, i, c: (h * nt + i, 0))
    half = pl.BlockSpec((tr, cols), lambda h, i, c: (i, 0))
    grid_spec = pltpu.PrefetchScalarGridSpec(num_scalar_prefetch=1, grid=(2, nt),
                                             in_specs=[full, half, half, full, full], out_specs=[full] * 4)
    return _pcall(body, name=name, grid_spec=grid_spec, out_shape=[jax.ShapeDtypeStruct((rows, cols), F32)] * 4,
                  compiler_params=pltpu.CompilerParams(dimension_semantics=("parallel", "parallel")),
                  )(c_arr, w, g_mine, g_other, m, v)


_MESH = pl.DeviceIdType.MESH
_ANY = pl.BlockSpec(memory_space=pl.ANY)
_CHIP_FLIPS = ((1, 0), (0, 1), (1, 1))


def _pos():
    return lax.axis_index("x"), lax.axis_index("y"), lax.axis_index("c")


def _flip(v, f):
    return 1 - v if f else v


def _sem_pairs(n):
    return [pltpu.SemaphoreType.DMA((n,)), pltpu.SemaphoreType.DMA((n,))]


def _small_allgather(blk, *, name):
    m_per, n = blk.shape

    def body(x_ref, out_ref, send_sems, recv_sems, local_sem):
        x, y, c = _pos()
        me, sibling = (x, y, c), (x, y, 1 - c)
        chips = [(_flip(x, fx), _flip(y, fy)) for fx, fy in _CHIP_FLIPS]

        def rows(px, py, pc):
            return out_ref.at[pl.ds((4 * px + 2 * py + pc) * m_per, m_per), :]

        def copy(k, block, to, src=None):
            return pltpu.make_async_remote_copy(
                src_ref=rows(*block) if src is None else src, dst_ref=rows(*block),
                send_sem=send_sems.at[k], recv_sem=recv_sems.at[k], device_id=to, device_id_type=_MESH)

        mine = pltpu.make_async_copy(x_ref, rows(*me), local_sem)
        mine.start()
        first = [copy(0, me, sibling, src=x_ref)]
        first += [copy(1 + j, me, (*chip, c), src=x_ref) for j, chip in enumerate(chips)]
        for cp in first:
            cp.start()
        passed = [copy(4 + j, (*chip, c), sibling) for j, chip in enumerate(chips)]
        for j, chip in enumerate(chips):
            copy(1 + j, (*chip, c), me).wait_recv()
            passed[j].start()
        copy(0, sibling, me).wait_recv()
        for j, chip in enumerate(chips):
            copy(4 + j, (*chip, 1 - c), me).wait_recv()
        for cp in first + passed:
            cp.wait_send()
        mine.wait()

    return _pcall(
        body, name=name, out_shape=jax.ShapeDtypeStruct((N_DEV * m_per, n), blk.dtype),
        in_specs=[_VM], out_specs=_VM,
        scratch_shapes=_sem_pairs(7) + [pltpu.SemaphoreType.DMA],
    )(blk)


_HBM = pl.BlockSpec(memory_space=pltpu.HBM)
_SEM = pl.BlockSpec(memory_space=pltpu.SEMAPHORE)
_EFFECT = pltpu.SideEffectType.DATAFLOW_SIDE_EFFECTING


def _split_start(srcs, lands, plan, ncopies, after, *, name):
    ns, nl = len(srcs), len(lands)

    def body(*refs):
        src_refs, land_refs = refs[:ns], refs[ns:ns + nl]
        send_sems, recv_sems = refs[ns + nl + 1], refs[ns + nl + 2]
        token = refs[-1]
        x, y, c = _pos()
        for k, (src, dst, to, _) in enumerate(plan(x, y, c, src_refs, land_refs)):
            pltpu.make_async_remote_copy(src_ref=src, dst_ref=dst, send_sem=send_sems.at[k], recv_sem=recv_sems.at[k],
                                         device_id=to, device_id_type=_MESH).start()
        token[...] = jnp.zeros_like(token)

    hbm = lambda a: pltpu.HBM(a.shape, a.dtype)
    out = _pcall(
        body, name=name,
        out_shape=(pltpu.SemaphoreType.DMA((ncopies,)), pltpu.SemaphoreType.DMA((ncopies,)),
                   *[hbm(a) for a in srcs], *[hbm(a) for a in lands], jax.ShapeDtypeStruct((8, LANES), F32)),
        in_specs=[_HBM] * (ns + nl) + [_ANY], out_specs=(_SEM, _SEM, *[_HBM] * (ns + nl), _VM),
        input_output_aliases={i: 2 + i for i in range(ns + nl)},
        compiler_params=pltpu.CompilerParams(has_side_effects=_EFFECT),
    )(*[pltpu.with_memory_space_constraint(a, pltpu.HBM) for a in (*srcs, *lands)], after)
    return out[0], out[1], list(out[2:2 + ns]), list(out[2 + ns:2 + ns + nl]), out[-1]


def _split_wait(send_sems, recv_sems, srcs, lands, plan, after, *, name, with_sources=False):
    ns, nl = len(srcs), len(lands)

    def body(*refs):
        src_refs, land_refs = refs[:ns], refs[ns:ns + nl]
        send_sems, recv_sems = refs[ns + nl], refs[ns + nl + 1]
        x, y, c = _pos()
        for k, (src, _, _, mine) in enumerate(plan(x, y, c, src_refs, land_refs)):
            cp = pltpu.make_async_remote_copy(src_ref=src, dst_ref=mine, send_sem=send_sems.at[k],
                                              recv_sem=recv_sems.at[k], device_id=(x, y, c), device_id_type=_MESH)
            cp.wait_send()
            cp.wait_recv()

    hbm = lambda a: pltpu.HBM(a.shape, a.dtype)
    out = _pcall(
        body, name=name, out_shape=tuple(hbm(a) for a in (*srcs, *lands)),
        in_specs=[_HBM] * (ns + nl) + [_SEM, _SEM, _ANY], out_specs=tuple([_HBM] * (ns + nl)),
        input_output_aliases={i: i for i in range(ns + nl)},
        compiler_params=pltpu.CompilerParams(has_side_effects=_EFFECT),
    )(*srcs, *lands, send_sems, recv_sems, after)
    return (list(out[:ns]), list(out[ns:])) if with_sources else list(out[ns:])


def _ag_plan(nw):
    def plan(x, y, c, sh_refs, full_refs):
        j = 2 * x + y
        out = []
        for wi in range(nw):
            for fx, fy in _CHIP_FLIPS:
                px, py = _flip(x, fx), _flip(y, fy)
                out.append((sh_refs[wi].at[c], full_refs[wi].at[j, c], (px, py, c), full_refs[wi].at[2 * px + py, c]))
            out.append((sh_refs[wi], full_refs[wi].at[j], (x, y, 1 - c), full_refs[wi].at[j]))
        return out
    return plan


def _ag_pass(fulls, *, name):
    nw = len(fulls)

    def body(*refs):
        in_refs, out_refs = refs[:nw], refs[nw:2 * nw]
        send_sems, recv_sems = refs[2 * nw:]
        x, y, c = _pos()
        cps = []
        for wi in range(nw):
            for k, (fx, fy) in enumerate(_CHIP_FLIPS):
                jp = 2 * _flip(x, fx) + _flip(y, fy)
                sems = dict(send_sem=send_sems.at[3 * wi + k], recv_sem=recv_sems.at[3 * wi + k], device_id_type=_MESH)
                send = pltpu.make_async_remote_copy(src_ref=in_refs[wi].at[jp, c], dst_ref=out_refs[wi].at[jp, c],
                                                    device_id=(x, y, 1 - c), **sems)
                recv = pltpu.make_async_remote_copy(src_ref=in_refs[wi].at[jp, c], dst_ref=out_refs[wi].at[jp, 1 - c],
                                                    device_id=(x, y, c), **sems)
                cps.append((send, recv))
        for send, _ in cps:
            send.start()
        for send, recv in cps:
            send.wait_send()
            recv.wait_recv()

    return _pcall(body, name=name, out_shape=[jax.ShapeDtypeStruct(f.shape, f.dtype) for f in fulls],
                  in_specs=[_ANY] * nw, out_specs=[_ANY] * nw, input_output_aliases={i: i for i in range(nw)},
                  scratch_shapes=_sem_pairs(3 * nw))(*fulls)


def _sib_plan(nw):
    def plan(x, y, c, g_refs, ra_refs):
        return [(g_refs[wi].at[k, 1 - c], ra_refs[wi].at[k], (x, y, 1 - c), ra_refs[wi].at[k])
                for wi in range(nw) for k in range(N_CHIP)]
    return plan


def _rs_plan(nw):
    def plan(x, y, c, s_refs, rb_refs):
        out = []
        for wi in range(nw):
            for k, (fx, fy) in enumerate(_CHIP_FLIPS):
                px, py = _flip(x, fx), _flip(y, fy)
                out.append((s_refs[wi].at[2 * px + py], rb_refs[wi].at[k], (px, py, c), rb_refs[wi].at[k]))
        return out
    return plan


def _sibling_swap(ts, *, name):
    nw = len(ts)

    def body(*refs):
        t_refs, out_refs = refs[:nw], refs[nw:2 * nw]
        send_sems, recv_sems = refs[2 * nw:]
        x, y, c = _pos()
        cps = [pltpu.make_async_remote_copy(src_ref=t_refs[wi], dst_ref=out_refs[wi], send_sem=send_sems.at[wi],
                                            recv_sem=recv_sems.at[wi], device_id=(x, y, 1 - c), device_id_type=_MESH)
               for wi in range(nw)]
        for cp in cps:
            cp.start()
        for cp in cps:
            cp.wait()

    return _pcall(body, name=name, out_shape=[jax.ShapeDtypeStruct(t.shape, t.dtype) for t in ts],
                  in_specs=[_ANY] * nw, out_specs=[_ANY] * nw, scratch_shapes=_sem_pairs(nw))(*ts)


def _rs_add(g, ra, c_arr, *, name):
    n, _, r, w = g.shape

    def body(c_ref, g_ref, ra_ref, s_ref, sb_ref):
        t = g_ref[...] + ra_ref[...]
        s_ref[...] = t
        sb_ref[...] = t.astype(BF16)

    grid_spec = pltpu.PrefetchScalarGridSpec(
        num_scalar_prefetch=1, grid=(n,),
        in_specs=[pl.BlockSpec((None, None, r, w), lambda k, c: (k, c[0], 0, 0)),
                  pl.BlockSpec((None, r, w), lambda k, c: (k, 0, 0))],
        out_specs=[pl.BlockSpec((None, r, w), lambda k, c: (k, 0, 0))] * 2)
    return _pcall(body, name=name, grid_spec=grid_spec,
                  out_shape=[jax.ShapeDtypeStruct((n, r, w), F32), jax.ShapeDtypeStruct((n, r, w), BF16)],
                  compiler_params=_PAR)(c_arr, g, ra)


def _rs_final(s, rb, j_arr, *, name):
    _, r, w = s.shape

    def body(j_ref, s_ref, rb_ref, t_ref):
        t_ref[...] = ((s_ref[...] + rb_ref[0].astype(F32)) + rb_ref[1].astype(F32)) + rb_ref[2].astype(F32)

    grid_spec = pltpu.PrefetchScalarGridSpec(
        num_scalar_prefetch=1, grid=(1,),
        in_specs=[pl.BlockSpec((None, r, w), lambda i, j: (j[0], 0, 0)),
                  pl.BlockSpec((3, r, w), lambda i, j: (0, 0, 0))],
        out_specs=pl.BlockSpec((r, w), lambda i, j: (0, 0)))
    return _pcall(body, name=name, grid_spec=grid_spec, out_shape=jax.ShapeDtypeStruct((r, w), F32),
                  compiler_params=_ARB)(j_arr, s, rb)


def _tile2(g):
    return jnp.concatenate([g, g], axis=1)


_BIG = ("w_in", "w_ffn_in", "w_ffn_out", "w_o", "w_proj_a", "w_proj_b")
_BIG_SHARD = {"w_in": (1024, 1472), "w_ffn_in": (1024, 1408), "w_ffn_out": (704, 1024), "w_o": (256, 1024),
              "w_proj_a": (512, 256), "w_proj_b": (256, 256)}


def _device_step(x2, tgt, mod, first_weights, late_weights, early_grads, mid_grads, g_norm1, g_norm2, b_gate, g_qa, g_ka,
                 g_qb, g_kb, rpb):
    d = D_MODEL
    sh1, sc1, gt1, sh2, sc2, gt2 = [mod[:, k * d:(k + 1) * d] for k in range(6)]

    colmask, oh_col, na_ks, na_cls = _na_constants()
    rp = jnp.pad(rpb, ((0, 0), (0, 16 - _RPB_RO), (RP_LANE0, LANES - RP_LANE0 - _RPB_CO)), constant_values=NEG)
    na = (rp, jnp.asarray(colmask))
    na_ks, na_cls = jnp.asarray(na_ks), jnp.asarray(na_cls)
    dil = [_dil_constants(dd) for _, dd in DIL_CONFIGS]
    gains = jnp.concatenate([_tile2(g_qa), _tile2(g_ka), _tile2(g_qb), _tile2(g_kb)], axis=0)
    cos_t, sa_t, sb_t = _rope_tables()

    wts = first_weights(cos_t)
    h1, qkvn, qk_pre, gates, *qkv_dil = _pre_attn_fwd(x2, cos_t, sa_t, sb_t, g_norm1, sc1, sh1, wts["w_qkv"],
                                                      wts["w_gates"], gains, name="pre_attn_fwd")
    o_a, lse_a = _attn_fwd(qkvn, 0, 4, 8, 4, None, na_ks, na_cls, NA_NK, name="attn_a_fwd", na=na)
    arrs, o_g, l_g = [], [], []
    res = lambda t, dd: t if dd == 1 else (t, dd)
    for g, (_, dd) in enumerate(DIL_CONFIGS):
        tab_g, ks_g, cls_g, nk_g = jnp.asarray(dil[g][0]), jnp.asarray(dil[g][1]), jnp.asarray(dil[g][2]), dil[g][3]
        arr, cb = (qkvn, (12, 18, 24)) if dd == 1 else (qkv_dil.pop(0), (0, 2, 4))
        op, lp = _attn_fwd(arr, cb[0], cb[1], cb[2], 2, tab_g, ks_g, cls_g, nk_g, name=f"attn_d{g}_fwd", qb=DIL_QB)
        arrs.append((arr, cb, tab_g, ks_g, cls_g, nk_g))
        o_g.append(res(op, dd))
        l_g.append(res(lp, dd))
    wts = dict(wts, **late_weights(o_a))
    o_b, merged, mo, x1, h2 = _post_attn_fwd(o_a, o_g, l_g, gates, x2, wts["w_pa"], wts["w_pb"], wts["w_o"], b_gate,
                                             gt1, g_norm2, sc2, sh2, name="post_attn_fwd")
    act, ff = _ffn_fwd(h2, wts["w_ffn_in"], name="ffn_fwd")

    dy, dffo, dff, dgt2, loss_v = _ffn_mid(act, ff, x1, tgt, wts["w_ffn_out"], gt2, name="ffn_mid")
    grads = {}
    g_ffn_out = _wgrad(act, dffo, name="wg_ffn_out", tm=D_FF // 2, tn=d)
    grads["w_ffn_out"] = g_ffn_out.reshape(N_CHIP, D_FF // N_CHIP, d)
    grads["w_ffn_in"] = _wgrad(h2, dff, name="wg_ffn_in", tm=d, tn=FF_CHIP, chips=N_CHIP)
    dx1, dmo, sums2 = _ffn_in_bwd(dff, x1, dy, mo, wts["w_ffn_in"], g_norm2, sc2, gt1, name="ffn_in_bwd")
    grads["w_o"] = _wgrad(merged, dmo, name="wg_o", tm=d, tn=d).reshape(N_CHIP, d // N_CHIP, d)
    pab = _post_attn_bwd(dmo, gates, o_a, o_g, l_g, wts["w_pa"], wts["w_pb"], wts["w_o"], b_gate, name="post_attn_bwd")
    dpa, dpb, dgates, do_a = pab[:4]
    do_g, dl_g, dbg = pab[4:7], pab[7:10], pab[10]
    g_pa = _wgrad(o_a, dpa, name="wg_pa", tm=WA, tn=d)
    g_pb = _wgrad(o_b, dpb, name="wg_pb", tm=WB_OUT, tn=d)
    grads["w_proj_a"] = g_pa.reshape(WA, N_CHIP, d // N_CHIP).transpose(1, 0, 2)
    grads["w_proj_b"] = g_pb.reshape(WB_OUT, N_CHIP, d // N_CHIP).transpose(1, 0, 2)
    order = early_grads(grads)
    dqs, dks, dvs = [], [], []
    for g, (_, dd) in enumerate(DIL_CONFIGS):
        arr, cb, tab_g, ks_g, cls_g, nk_g = arrs[g]
        plain = lambda t: t[0] if isinstance(t, tuple) else t
        dq, dk, dv = _attn_bwd(arr, cb[0], cb[1], cb[2], 2, tab_g, ks_g, cls_g, nk_g, do_g[g], plain(o_g[g]),
                               plain(l_g[g]), name=f"attn_d{g}_bwd", dlse=dl_g[g], qb=DIL_QB, order=order)
        dqs.append(res(dq, dd))
        dks.append(res(dk, dd))
        dvs.append(res(dv, dd))
    order = mid_grads(dv)
    dqa, dka, dva, bank = _attn_bwd(qkvn, 0, 4, 8, 4, None, na_ks, na_cls, NA_NK, do_a, o_a, lse_a,
                                    name="attn_a_bwd", na=na, order=order)
    dqkv, grad_x, dgains, sums1 = _pre_attn_bwd(qk_pre, [dqa, dka, dva] + dqs + dks + dvs, dgates, x2, dx1, cos_t, sa_t,
                                                sb_t, wts["w_qkv"], wts["w_gates"], gains, g_norm1, sc1,
                                                name="pre_attn_bwd")
    g_qkv = _wgrad(h1, dqkv, name="wg_qkv", tm=d, tn=W_QKV // 2)
    g_gates = _wgrad(h1, dgates, name="wg_gates", tm=d, tn=W_GATES)
    nc, cut = _BIG_SHARD["w_in"][1], 3 * _BIG_SHARD["w_in"][1] - W_QKV
    grads["w_in"] = jnp.stack([g_qkv[:, :nc], g_qkv[:, nc:2 * nc],
                               jnp.concatenate([g_qkv[:, 2 * nc:], g_gates[:, :cut]], axis=1), g_gates[:, cut:]])

    bank = bank.reshape(NA_HEADS, NA_SLOTS, GRID_W, 2, GRID_W).transpose(0, 1, 3, 2, 4)
    bank = jnp.pad(bank.reshape(NA_HEADS, 2 * NA_SLOTS, GRID_W * GRID_W), ((0, 0), (0, _BANK_ROWS - 2 * NA_SLOTS), (0, 0)))
    g2 = _bank_reduce(bank, jnp.asarray(oh_col), name="rpb_reduce")[:, :2 * NA_SLOTS].reshape(NA_HEADS, NA_SLOTS, 2, LANES)
    g_rpb = g2[:, 3:3 + _RPB_RO, 0, :_RPB_CO] + g2[:, 2:2 + _RPB_RO, 1, :_RPB_CO]

    dmod = jnp.concatenate([sums1[0:1], sums1[1:2], sums2[3:4], sums2[0:1], sums2[1:2], dgt2], axis=1)
    small = dict(g_norm1=sums1[2:3], g_norm2=sums2[2:3], b_gate=dbg, g_qa=dgains[0:1, :HEAD_DIM],
                 g_ka=dgains[1:2, :HEAD_DIM], g_qb=dgains[2:3, :HEAD_DIM], g_kb=dgains[3:4, :HEAD_DIM], rpb=g_rpb)
    return loss_v, grad_x, grads, dmod, small


_SMALL = ("b_ada", "g_norm1", "g_norm2", "b_gate", "g_qa", "g_ka", "g_qb", "g_kb", "rpb")
_SMALL_N = {"b_ada": 6 * D_MODEL, "g_norm1": D_MODEL, "g_norm2": D_MODEL, "b_gate": 2 * D_MODEL, "g_qa": HEAD_DIM,
            "g_ka": HEAD_DIM, "g_qb": HEAD_DIM, "g_kb": HEAD_DIM, "rpb": NA_HEADS * _RPB_RO * _RPB_CO}


def _pack_small(parts):
    flat = [parts[n].reshape(1, _SMALL_N[n]) for n in _SMALL]
    used = sum(_SMALL_N.values())
    return jnp.concatenate(flat + [jnp.zeros((1, STATS_W - used), F32)], axis=1)


def _unpack_small(v, shapes):
    out, at = {}, 0
    for n in _SMALL:
        out[n] = v[:, at:at + _SMALL_N[n]].reshape(shapes[n])
        at += _SMALL_N[n]
    return out


def _join_cols(t):
    _, r, c = t.shape
    return t.transpose(1, 0, 2).reshape(r, N_CHIP * c)


def kernel(x, c, w_ada, b_ada, g_norm1, g_norm2, w_in, b_gate, g_qa, g_ka, g_qb, g_kb, rpb, w_proj_a, w_proj_b, w_o, w_ffn_in, w_ffn_out, loss_target, m_w_ada, m_b_ada, m_g_norm1, m_g_norm2, m_w_in, m_b_gate, m_g_qa, m_g_ka, m_g_qb, m_g_kb, m_rpb, m_w_proj_a, m_w_proj_b, m_w_o, m_w_ffn_in, m_w_ffn_out, v_w_ada, v_b_ada, v_g_norm1, v_g_norm2, v_w_in, v_b_gate, v_g_qa, v_g_ka, v_g_qb, v_g_kb, v_rpb, v_w_proj_a, v_w_proj_b, v_w_o, v_w_ffn_in, v_w_ffn_out):
    names = ("w_ada", "b_ada", "g_norm1", "g_norm2", "w_in", "b_gate", "g_qa", "g_ka", "g_qb", "g_kb", "rpb",
             "w_proj_a", "w_proj_b", "w_o", "w_ffn_in", "w_ffn_out")
    w = dict(zip(names, (w_ada, b_ada, g_norm1, g_norm2, w_in, b_gate, g_qa, g_ka, g_qb, g_kb, rpb, w_proj_a, w_proj_b,
                         w_o, w_ffn_in, w_ffn_out)))
    m = dict(zip(names, (m_w_ada, m_b_ada, m_g_norm1, m_g_norm2, m_w_in, m_b_gate, m_g_qa, m_g_ka, m_g_qb, m_g_kb, m_rpb,
                         m_w_proj_a, m_w_proj_b, m_w_o, m_w_ffn_in, m_w_ffn_out)))
    v = dict(zip(names, (v_w_ada, v_b_ada, v_g_norm1, v_g_norm2, v_w_in, v_b_gate, v_g_qa, v_g_ka, v_g_qb, v_g_kb, v_rpb,
                         v_w_proj_a, v_w_proj_b, v_w_o, v_w_ffn_in, v_w_ffn_out)))
    d = D_MODEL
    xi, yi, ci = _pos()
    chip = 2 * xi + yi
    me = 2 * chip + ci
    ada_cols = 6 * d // N_CHIP

    c_arr, chip_arr = ci.reshape(1).astype(jnp.int32), chip.reshape(1).astype(jnp.int32)
    first, rest = _BIG[:1], _BIG[1:]

    c_all = _small_allgather(c.reshape(8, d // 8), name="ag_c").reshape(N_DEV, d)
    b_sh = lax.dynamic_slice(b_ada, (0, chip * ada_cols), (1, ada_cols))
    mod_part = _ada_fwd(c_all, w_ada[0], b_sh, name="ada_fwd")
    mod_all = _small_allgather(mod_part, name="ag_mod").reshape(N_CHIP, 2, 8, ada_cols)[:, 0]
    mod = lax.dynamic_index_in_dim(mod_all, me, axis=1, keepdims=False).reshape(1, 6 * d)

    halves = {n: (2, _BIG_SHARD[n][0] // 2, _BIG_SHARD[n][1]) for n in _BIG}
    shards = {n: w[n][0].astype(BF16).reshape(halves[n]) for n in _BIG}
    land = lambda n: lax.empty((N_CHIP,) + halves[n], BF16)
    ag1 = _split_start([shards[n] for n in first], [land(n) for n in first], _ag_plan(1), 4, mod, name="ag1_start")
    ag2 = _split_start([shards[n] for n in rest], [land(n) for n in rest], _ag_plan(len(rest)), 4 * len(rest),
                       ag1[4], name="ag2_start")
    rpb_after = rpb[0] + ag2[4][0, 0]

    def first_weights(after):
        full1 = _split_wait(ag1[0], ag1[1], ag1[2], ag1[3], _ag_plan(1), after, name="ag1_wait")
        p_in = _ag_pass(full1, name="ag1_pass")[0].reshape((N_CHIP,) + _BIG_SHARD["w_in"])
        cut = W_QKV - 2 * _BIG_SHARD["w_in"][1]
        return dict(w_qkv=jnp.concatenate([p_in[0], p_in[1], p_in[2][:, :cut]], axis=1),
                    w_gates=jnp.concatenate([p_in[2][:, cut:], p_in[3]], axis=1))

    def late_weights(after):
        full2 = _split_wait(ag2[0], ag2[1], ag2[2], ag2[3], _ag_plan(len(rest)), after, name="ag2_wait")
        full2 = _ag_pass(full2, name="ag2_pass")
        full = {n: fu.reshape((N_CHIP,) + _BIG_SHARD[n]) for n, fu in zip(rest, full2)}
        return dict(w_pa=_join_cols(full["w_proj_a"]), w_pb=_join_cols(full["w_proj_b"]), w_o=full["w_o"].reshape(d, d),
                    w_ffn_in=_join_cols(full["w_ffn_in"]), w_ffn_out=full["w_ffn_out"].reshape(D_FF, d))

    def sib_begin(group, grads, tag):
        gps = [grads[n].reshape((N_CHIP,) + halves[n]) for n in group]
        lands = [lax.empty((N_CHIP,) + halves[n][1:], F32) for n in group]
        return _split_start(gps, lands, _sib_plan(len(group)), N_CHIP * len(group), gps[0], name=f"rs_sib_{tag}_start")

    def rs_begin(group, sib, after, tag):
        gps, ras = _split_wait(sib[0], sib[1], sib[2], sib[3], _sib_plan(len(group)), after,
                               name=f"rs_sib_{tag}_wait", with_sources=True)
        sums = [_rs_add(gp, ra, c_arr, name=f"rs_add_{n}") for n, gp, ra in zip(group, gps, ras)]
        lands = [lax.empty((3,) + halves[n][1:], BF16) for n in group]
        st = _split_start([sb for _, sb in sums], lands, _rs_plan(len(group)), 3 * len(group), sums[0][0],
                          name=f"rs_{tag}_start")
        return sums, st

    def rs_end(group, begun, after, tag):
        sums, st = begun
        rbs = _split_wait(st[0], st[1], st[2], st[3], _rs_plan(len(group)), after, name=f"rs_{tag}_wait")
        return [_rs_final(sf, rb, chip_arr, name=f"rs_final_{n}") for n, (sf, _), rb in zip(group, sums, rbs)]

    begun = {}

    def early_grads(grads):
        begun["sib_rest"] = sib_begin(rest, grads, "rest")
        return begun["sib_rest"][4]

    def mid_grads(after):
        begun["rest"] = rs_begin(rest, begun["sib_rest"], after, "rest")
        return begun["rest"][1][4]

    loss_v, grad_x, grads, dmod, small = _device_step(
        x[0], loss_target[0], mod, first_weights, late_weights, early_grads, mid_grads, g_norm1, g_norm2, b_gate, g_qa,
        g_ka, g_qb, g_kb, rpb_after)
    sib_first = sib_begin(first, grads, "first")

    g, delta, new_m, new_v = {}, {}, {}, {}

    def finish(group, ts, tag):
        others = _sibling_swap(ts, name=f"rs_pair_{tag}")
        for n, t, o in zip(group, ts, others):
            gg, dl, nm, nv = _adamw_halves(w[n][0], t, o, m[n][0], v[n][0], c_arr, name=f"adamw_{n}")
            g[n], delta[n], new_m[n], new_v[n] = gg[None], dl[None], nm[None], nv[None]

    finish(rest, rs_end(rest, begun["rest"], sib_first[4], "rest"), "rest")
    done_rest = sum(new_v[n][0, :1, :1] for n in rest)
    begun["first"] = rs_begin(first, sib_first, done_rest, "first")

    stats = _pack_small(dict(b_ada=dmod, **small)) + begun["first"][1][4][0, 0]
    rows = _small_allgather(stats.reshape(8, STATS_W // 8), name="ag_stats").reshape(N_DEV, STATS_W)
    dmod_sh = lax.dynamic_slice(rows, (0, chip * ada_cols), (8, ada_cols))
    g_ada = _ada_bwd(c_all.T, dmod_sh, name="ada_bwd")
    tot = _row_sum(rows, name="stats_sum")
    g_small = _unpack_small(tot, {n: w[n].shape for n in _SMALL})

    finish(first, rs_end(first, begun["first"], tot, "first"), "first")

    dl, nm, nv = _adamw(w_ada[0], g_ada, m_w_ada[0], v_w_ada[0], name="adamw_w_ada")
    g["w_ada"], delta["w_ada"], new_m["w_ada"], new_v["w_ada"] = g_ada[None], dl[None], nm[None], nv[None]
    shapes = {n: w[n].shape for n in _SMALL}
    dl, nm, nv = _adamw(_pack_small({n: w[n] for n in _SMALL}), tot, _pack_small({n: m[n] for n in _SMALL}),
                        _pack_small({n: v[n] for n in _SMALL}), name="adamw_small")
    delta.update(_unpack_small(dl, shapes))
    new_m.update(_unpack_small(nm, shapes))
    new_v.update(_unpack_small(nv, shapes))
    g.update(g_small)

    loss = lax.psum(loss_v[0, 0], ("x", "y", "c"))
    return (loss, grad_x[None], *[g[n] for n in names], *[delta[n] for n in names], *[new_m[n] for n in names],
            *[new_v[n] for n in names])
```

```python
import numpy as np

import jax
import jax.numpy as jnp
from jax import lax
from jax.experimental import pallas as pl
from jax.experimental.pallas import tpu as pltpu

F32 = jnp.float32
BF16 = jnp.bfloat16

D_MODEL = 1024
SEQ = 8192
HEAD_DIM = 64
GRID_W = 64
ROWS = SEQ // GRID_W
NA_HEADS = 8
NA_KH = 8
NA_KW = 16
DIL_CONFIGS = ((128, 1), (512, 4), (2048, 16))
ROT_DIM = 16
ROPE_THETA = 500000.0
D_FF = 2816
EPS = 1e-6
NEG = -1e30
WA = 512
WB = 768
WB_OUT = 256
W_QKV = 3 * WA + 3 * WB
W_QK = 2 * WA + 2 * WB
W_GATES = 2 * D_MODEL
SCALE = HEAD_DIM ** -0.5

ADAM_LR = 0.001
ADAM_B1 = 0.9
ADAM_B2 = 0.999
ADAM_EPS = 1e-08
ADAM_WD = 0.01
ADAM_STEP = 10

LANES = 128
ROW_TILE = 256
ROW_TILES = {"ffn_fwd": 512, "post_attn_fwd": 512, "post_attn_bwd": 512,
             "ffn_in_bwd": 512}
Q_BLOCK = 256
NA_QROWS = Q_BLOCK // GRID_W
NA_KROWS = NA_QROWS + NA_KH - 1
NA_NK = NA_KROWS * GRID_W
NA_PAIRS = (NA_KROWS + 1) // 2
NA_W = NA_PAIRS * LANES
NA_RO_NONE = 15
NA_SLOTS = 21
RP_LANE0 = GRID_W - NA_KW
DIL_HALF = 64
DIL_QB = 512
N_QBLK = SEQ // Q_BLOCK

N_DEV = 8
N_CHIP = 4
FF_CHIP = 2 * D_FF // N_CHIP
STATS_W = 14336


def _pcall(body, *, name, **kw):
    return pl.pallas_call(body, name=name, **kw)


_NT = (((1,), (1,)), ((), ()))
_TN = (((0,), (0,)), ((), ()))
_ARB = pltpu.CompilerParams(dimension_semantics=("arbitrary",))
_PAR = pltpu.CompilerParams(dimension_semantics=("parallel",))


def _dot(a, b):
    return jnp.dot(a, b, preferred_element_type=F32)


def _dot_nt(a, b):
    return lax.dot_general(a, b, _NT, preferred_element_type=F32)


def _wgrad(a, b, *, name, tm, tn, tk=1024, chips=None):
    s, ma = a.shape
    nb = b.shape[1]
    nk = s // tk
    nc = nb // chips if chips else tn
    cpb = tn // nc

    def body(a_ref, b_ref, o_ref, acc):
        k = pl.program_id(2)
        r = lax.dot_general(a_ref[...].astype(BF16), b_ref[...].astype(BF16), _TN, preferred_element_type=F32)

        @pl.when(k == 0)
        def _():
            acc[...] = r

        @pl.when(k > 0)
        def _():
            acc[...] += r

        @pl.when(k == nk - 1)
        def _():
            if chips:
                for q in range(cpb):
                    o_ref[q] = acc[:, q * nc:(q + 1) * nc]
            else:
                o_ref[...] = acc[...]

    if chips:
        o_spec = pl.BlockSpec((cpb, tm, nc), lambda i, j, k: (j, i, 0))
        out_shape = jax.ShapeDtypeStruct((chips, ma, nc), F32)
    else:
        o_spec = pl.BlockSpec((tm, tn), lambda i, j, k: (i, j))
        out_shape = jax.ShapeDtypeStruct((ma, nb), F32)
    return _pcall(
        body, name=name, grid=(ma // tm, nb // tn, nk),
        in_specs=[pl.BlockSpec((tk, tm), lambda i, j, k: (k, i)), pl.BlockSpec((tk, tn), lambda i, j, k: (k, j))],
        out_specs=o_spec, out_shape=out_shape, scratch_shapes=[pltpu.VMEM((tm, tn), F32)],
        compiler_params=pltpu.CompilerParams(dimension_semantics=("parallel", "parallel", "arbitrary")),
    )(a, b)


def _row_call(body, *, name, row_ins, res_ins, row_outs, acc_outs=(), scratch=()):
    row_ins = [a if isinstance(a, tuple) else (a, 1) for a in row_ins]
    row_outs = [o if len(o) == 3 else (*o, 1) for o in row_outs]
    s = row_ins[0][0].shape[0]
    tile = ROW_TILES.get(name, ROW_TILE)
    n = s // tile
    nri, nre, nro, nao = len(row_ins), len(res_ins), len(row_outs), len(acc_outs)

    def whole(shape):
        nd = len(shape)
        return pl.BlockSpec(tuple(shape), lambda i: (0,) * nd, pipeline_mode=pl.Buffered(1))

    def whole_out(shape):
        nd = len(shape)
        return pl.BlockSpec(tuple(shape), lambda i: (0,) * nd)

    def rows(w, d):
        if d == 1:
            return pl.BlockSpec((tile, w), lambda i: (i, 0))
        return pl.BlockSpec((d, tile // d, w), lambda i: (0, i, 0))

    in_specs = [rows(a.shape[1], d) for a, d in row_ins]
    in_specs += [whole(a.shape) for a in res_ins]
    out_specs = [rows(w, d) for w, _, d in row_outs]
    out_specs += [whole_out(shp) for shp, _ in acc_outs]
    out_shape = [jax.ShapeDtypeStruct((s, w) if d == 1 else (d, s // d, w), dt) for w, dt, d in row_outs]
    out_shape += [jax.ShapeDtypeStruct(tuple(shp), dt) for shp, dt in acc_outs]

    def wrapped(*refs):
        at = [0, nri, nri + nre, nri + nre + nro, nri + nre + nro + nao]
        body(pl.program_id(0), n, refs[at[0]:at[1]], refs[at[1]:at[2]], refs[at[2]:at[3]], refs[at[3]:at[4]],
             refs[at[4]:])

    args = [a if d == 1 else a.reshape(d, s // d, a.shape[1]) for a, d in row_ins]
    outs = _pcall(wrapped, name=name, grid=(n,), in_specs=in_specs, out_specs=out_specs, out_shape=out_shape,
                  scratch_shapes=list(scratch), compiler_params=_ARB)(*args, *res_ins)
    return [o.reshape(s, o.shape[-1]) if k < nro and row_outs[k][2] != 1 else o for k, o in enumerate(outs)]


def _stage_shape(name):
    return pltpu.VMEM((4, ROW_TILES.get(name, ROW_TILE), LANES), F32)


def _from_residue(ref, col, stage, slot):
    d, n = ref.shape[0], ref.shape[1]
    for r in range(d):
        stage.at[slot][pl.ds(r, n, stride=d), :] = ref[r, :, col:col + LANES].astype(F32)
    return stage[slot]


def _natural(ref, stage, slot0):
    if len(ref.shape) == 2:
        return ref[...]
    return jnp.concatenate([_from_residue(ref, c * LANES, stage, (slot0 + c) % 4)
                            for c in range(ref.shape[2] // LANES)], axis=1)


def _to_residue(val, ref, col, stage, slot):
    d, n = ref.shape[0], ref.shape[1]
    stage[slot] = val
    for r in range(d):
        ref[r, :, col:col + LANES] = stage.at[slot][pl.ds(r, n, stride=d), :].astype(ref.dtype)


def _fold8(t):
    r, w = t.shape
    return jnp.sum(t.reshape(r // 8, 8, w), axis=0)


def _sigmoid(t):
    return 0.5 * (jnp.tanh(0.5 * t) + 1.0)


def _head_lanes():
    return lax.broadcasted_iota(jnp.int32, (1, LANES), 1) < HEAD_DIM


def _head_mean(t, lo):
    s_lo = jnp.sum(jnp.where(lo, t, 0.0), axis=1, keepdims=True)
    s_hi = jnp.sum(jnp.where(lo, 0.0, t), axis=1, keepdims=True)
    return jnp.where(lo, s_lo, s_hi) * (1.0 / HEAD_DIM)


def _rms_mod(xv, g, sc, sh):
    rstd = lax.rsqrt(jnp.mean(xv * xv, axis=1, keepdims=True) + EPS)
    return (xv * rstd * g) * (1.0 + sc) + sh


def _rms_mod_bwd(xv, dh, g, sc):
    rstd = lax.rsqrt(jnp.mean(xv * xv, axis=1, keepdims=True) + EPS)
    xhat = xv * rstd
    dn = dh * (1.0 + sc)
    dxhat = dn * g
    dx = rstd * (dxhat - xhat * jnp.mean(dxhat * xhat, axis=1, keepdims=True))
    return dx, dh, dh * (xhat * g), dn * xhat


def _mix_weights(ls):
    m = jnp.maximum(jnp.maximum(ls[0], ls[1]), ls[2])
    es = [jnp.exp(t - m) for t in ls]
    den = es[0] + es[1] + es[2]
    return [e / den for e in es]


def _rope_tables():
    half = ROT_DIM // 2
    inv_freq = ROPE_THETA ** (-(jnp.arange(half, dtype=F32) * 2.0) / ROT_DIM)
    lane = np.arange(LANES) % HEAD_DIM
    ang = jnp.arange(SEQ).astype(F32)[:, None] * jnp.tile(inv_freq, LANES // half)[None, :]
    cos, sin = jnp.cos(ang), jnp.sin(ang)
    first, second = jnp.asarray(lane < half)[None, :], jnp.asarray((lane >= half) & (lane < ROT_DIM))[None, :]
    cos_t = jnp.where(first | second, cos, 1.0)
    return cos_t, jnp.where(second, sin, 0.0), jnp.where(first, -sin, 0.0)


_SECTIONS = ((0, WA, 0, False), (WA, 2 * WA, 1, False), (2 * WA, 3 * WA, -1, False),
             (3 * WA, 3 * WA + WB, 2, True), (3 * WA + WB, 3 * WA + 2 * WB, 3, True), (3 * WA + 2 * WB, W_QKV, -1, False))


def _pre_attn_fwd(x, cos_t, sa_t, sb_t, g1, sc1, sh1, w_qkv, w_gates, gains, *, name):
    half = ROT_DIM // 2
    dilated = [(g, dd) for g, (_, dd) in enumerate(DIL_CONFIGS) if dd > 1]

    def body(i, n, rin, res, rout, aout, scr):
        x_ref, cos_ref, sa_ref, sb_ref = rin
        g_ref, sc_ref, sh_ref, wq_ref, wg_ref, gains_ref = res
        h1_ref, qkvn_ref, pre_ref, gates_ref = rout[:4]
        group_ref = {g: rout[4 + k] for k, (g, _) in enumerate(dilated)}
        (stage,) = scr
        staged = 0
        hb = _rms_mod(x_ref[...], g_ref[...], sc_ref[...], sh_ref[...]).astype(BF16)
        h1_ref[...] = hb
        gates_ref[...] = _dot(hb, wg_ref[...]).astype(BF16)
        lo = _head_lanes()
        cosv, sav, sbv = cos_ref[...], sa_ref[...], sb_ref[...]
        pre_at = 0
        for si, (c0, c1, kind, rot) in enumerate(_SECTIONS):
            sec = _dot(hb, wq_ref[:, c0:c1])
            for ch in range((c1 - c0) // LANES):
                t = sec[:, ch * LANES:(ch + 1) * LANES]
                if kind >= 0:
                    pre_ref[:, pre_at:pre_at + LANES] = t.astype(BF16)
                    pre_at += LANES
                    t = t * lax.rsqrt(_head_mean(t * t, lo) + EPS) * gains_ref[kind:kind + 1, :]
                    if rot:
                        t = t * cosv + pltpu.roll(t, half, 1) * sav + pltpu.roll(t, LANES - half, 1) * sbv
                qkvn_ref[:, c0 + ch * LANES:c0 + (ch + 1) * LANES] = t.astype(BF16)
                group = ch * LANES // WB_OUT if si >= 3 else 0
                if group in group_ref:
                    col = (si - 3) * WB_OUT + ch * LANES % WB_OUT
                    _to_residue(t, group_ref[group], col, stage, staged % 4)
                    staged += 1

    return _row_call(body, name=name, row_ins=[x, cos_t, sa_t, sb_t], res_ins=[g1, sc1, sh1, w_qkv, w_gates, gains],
                     row_outs=[(D_MODEL, BF16), (W_QKV, BF16), (W_QK, BF16), (W_GATES, BF16)]
                     + [(3 * WB_OUT, BF16, dd) for _, dd in dilated], scratch=[_stage_shape(name)])


def _pre_attn_bwd(qk_pre, d_parts, dgates, x, dx1, cos_t, sa_t, sb_t, w_qkv, w_gates, gains, g1, sc1, *, name):
    half = ROT_DIM // 2
    nparts = len(d_parts)
    where = []
    residue = [isinstance(part, tuple) for part in d_parts]
    for pi, part in enumerate(d_parts):
        width = (part[0] if residue[pi] else part).shape[1]
        where += [(pi, cj) for cj in range(width // LANES)]
    assert len(where) == W_QKV // LANES

    def body(i, n, rin, res, rout, aout, scr):
        pre_ref, d_refs = rin[0], rin[1:1 + nparts]
        dgates_ref, x_ref, dx1_ref, cos_ref, sa_ref, sb_ref = rin[1 + nparts:]
        wq_ref, wg_ref, gains_ref, g_ref, sc_ref = res
        dqkv_ref, gx_ref = rout
        dgains_ref, sums_ref = aout
        accg, accs, stage = scr
        staged = 0

        @pl.when(i == 0)
        def _():
            accg[...] = jnp.zeros_like(accg)
            accs[...] = jnp.zeros_like(accs)

        lo = _head_lanes()
        cosv, sav, sbv = cos_ref[...], sa_ref[...], sb_ref[...]
        dh = _dot_nt(dgates_ref[...], wg_ref[...])
        pre_at = 0
        for c0, c1, kind, rot in _SECTIONS:
            for ch in range((c1 - c0) // LANES):
                pi, cj = where[c0 // LANES + ch]
                if residue[pi]:
                    dt = _from_residue(d_refs[pi], cj * LANES, stage, staged % 4)
                    staged += 1
                else:
                    dt = d_refs[pi][:, cj * LANES:(cj + 1) * LANES]
                if kind >= 0:
                    if rot:
                        dt = dt * cosv + pltpu.roll(dt * sav, LANES - half, 1) + pltpu.roll(dt * sbv, half, 1)
                    t = pre_ref[:, pre_at:pre_at + LANES].astype(F32)
                    pre_at += LANES
                    rstd = lax.rsqrt(_head_mean(t * t, lo) + EPS)
                    xhat = t * rstd
                    accg[kind] += _fold8(dt * xhat)
                    dxhat = dt * gains_ref[kind:kind + 1, :]
                    dt = rstd * (dxhat - xhat * _head_mean(dxhat * xhat, lo))
                dqkv_ref[:, c0 + ch * LANES:c0 + (ch + 1) * LANES] = dt.astype(BF16)
            dh = dh + _dot_nt(dqkv_ref[:, c0:c1], wq_ref[:, c0:c1])
        dx, t_sh, t_sc, t_g = _rms_mod_bwd(x_ref[...], dh, g_ref[...], sc_ref[...])
        gx_ref[...] = dx1_ref[...] + dx
        accs[0] += _fold8(t_sh)
        accs[1] += _fold8(t_sc)
        accs[2] += _fold8(t_g)

        @pl.when(i == n - 1)
        def _():
            t = jnp.sum(accg[...], axis=1)
            dgains_ref[...] = t + pltpu.roll(t, HEAD_DIM, 1)
            sums_ref[...] = jnp.sum(accs[...], axis=1)

    return _row_call(
        body, name=name, row_ins=[qk_pre, *d_parts, dgates, x, dx1, cos_t, sa_t, sb_t],
        res_ins=[w_qkv, w_gates, gains, g1, sc1], row_outs=[(W_QKV, BF16), (D_MODEL, F32)],
        acc_outs=[((4, LANES), F32), ((3, D_MODEL), F32)],
        scratch=[pltpu.VMEM((4, 8, LANES), F32), pltpu.VMEM((3, 8, D_MODEL), F32), _stage_shape(name)])


def _post_attn_fwd(o_a, o_g, l_g, gates, x, w_pa, w_pb, w_o, b_gate, gt1, g2, sc2, sh2, *, name):
    d = D_MODEL

    def body(i, n, rin, res, rout, aout, scr):
        oa_ref, o0, o1, o2, l0, l1, l2, gates_ref, x_ref = rin
        wpa_ref, wpb_ref, wo_ref, b_ref, gt_ref, g_ref, sc_ref, sh_ref = res
        ob_ref, merged_ref, mo_ref, x1_ref, h2_ref = rout
        (stage,) = scr
        ogs = [_natural(r, stage, 0) for r in (o0, o1, o2)]
        ws = _mix_weights([_natural(r, stage, 2) for r in (l0, l1, l2)])
        obb = (ws[0] * ogs[0] + ws[1] * ogs[1] + ws[2] * ogs[2]).astype(BF16)
        ob_ref[...] = obb
        pa = _dot(oa_ref[...].astype(BF16), wpa_ref[...])
        pb = _dot(obb, wpb_ref[...])
        ga = _sigmoid(gates_ref[:, :d].astype(F32) + b_ref[:, :d])
        gb = _sigmoid(gates_ref[:, d:].astype(F32) + b_ref[:, d:])
        merged = (ga * pa + gb * pb).astype(BF16)
        merged_ref[...] = merged
        mo = _dot(merged, wo_ref[...])
        mo_ref[...] = mo.astype(BF16)
        x1 = x_ref[...] + gt_ref[...] * mo
        x1_ref[...] = x1
        h2_ref[...] = _rms_mod(x1, g_ref[...], sc_ref[...], sh_ref[...]).astype(BF16)

    return _row_call(body, name=name, row_ins=[o_a, *o_g, *l_g, gates, x],
                     res_ins=[w_pa, w_pb, w_o, b_gate, gt1, g2, sc2, sh2],
                     row_outs=[(WB_OUT, BF16), (d, BF16), (d, BF16), (d, F32), (d, BF16)], scratch=[_stage_shape(name)])


def _ffn_fwd(h2, w_ffn_in, *, name):
    def body(i, n, rin, res, rout, aout, scr):
        (h_ref,), (w_ref,), (act_ref, ff_ref) = rin, res, rout
        hv = h_ref[...]
        for q in range(2):
            a = _dot(hv, w_ref[:, q * FF_CHIP:(q + 1) * FF_CHIP])
            up = _dot(hv, w_ref[:, D_FF + q * FF_CHIP:D_FF + (q + 1) * FF_CHIP])
            sl = slice(q * FF_CHIP, (q + 1) * FF_CHIP)
            act_ref[:, sl] = (a * _sigmoid(a) * up).astype(BF16)
            ff_ref[:, sl] = a.astype(BF16)
            ff_ref[:, D_FF + q * FF_CHIP:D_FF + (q + 1) * FF_CHIP] = up.astype(BF16)

    return _row_call(body, name=name, row_ins=[h2], res_ins=[w_ffn_in], row_outs=[(D_FF, BF16), (2 * D_FF, BF16)])


def _ffn_mid(act, ff, x1, tgt, w_ffn_out, gt2, *, name):
    d = D_MODEL

    def body(i, n, rin, res, rout, aout, scr):
        act_ref, ff_ref, x1_ref, tgt_ref = rin
        wo_ref, gt_ref = res
        dy_ref, dffo_ref, dff_ref = rout
        dgt_ref, loss_ref = aout
        (acc,) = scr

        @pl.when(i == 0)
        def _():
            acc[...] = jnp.zeros_like(acc)

        ffo = _dot(act_ref[...], wo_ref[...])
        gtv = gt_ref[...]
        e = x1_ref[...] + gtv * ffo - tgt_ref[...]
        dy = e * (1.0 / d)
        dy_ref[...] = dy
        dffo = (gtv * dy).astype(BF16)
        dffo_ref[...] = dffo
        acc[0] += _fold8(dy * ffo)
        acc[1] += _fold8(e * e)
        for q in range(2):
            sl = slice(q * FF_CHIP, (q + 1) * FF_CHIP)
            su = slice(D_FF + q * FF_CHIP, D_FF + (q + 1) * FF_CHIP)
            dact = _dot_nt(dffo, wo_ref[sl, :])
            a = ff_ref[:, sl].astype(F32)
            up = ff_ref[:, su].astype(F32)
            sg = _sigmoid(a)
            dff_ref[:, sl] = (dact * up * (sg * (1.0 + a * (1.0 - sg)))).astype(BF16)
            dff_ref[:, su] = (dact * (a * sg)).astype(BF16)

        @pl.when(i == n - 1)
        def _():
            dgt_ref[...] = jnp.sum(acc[0], axis=0, keepdims=True)
            tot = jnp.sum(jnp.sum(acc[1], axis=0, keepdims=True), axis=1, keepdims=True)
            loss_ref[...] = jnp.broadcast_to(tot * (0.5 / d), (1, LANES))

    return _row_call(body, name=name, row_ins=[act, ff, x1, tgt], res_ins=[w_ffn_out, gt2],
                     row_outs=[(d, F32), (d, BF16), (2 * D_FF, BF16)], acc_outs=[((1, d), F32), ((1, LANES), F32)],
                     scratch=[pltpu.VMEM((2, 8, d), F32)])


def _ffn_in_bwd(dff, x1, dy, mo, w_ffn_in, g2, sc2, gt1, *, name):
    d = D_MODEL

    def body(i, n, rin, res, rout, aout, scr):
        dff_ref, x1_ref, dy_ref, mo_ref = rin
        w_ref, g_ref, sc_ref, gt_ref = res
        dx1_ref, dmo_ref = rout
        (sums_ref,) = aout
        (acc,) = scr

        @pl.when(i == 0)
        def _():
            acc[...] = jnp.zeros_like(acc)

        dh = _dot_nt(dff_ref[...], w_ref[...])
        dx, t_sh, t_sc, t_g = _rms_mod_bwd(x1_ref[...], dh, g_ref[...], sc_ref[...])
        dx1 = dy_ref[...] + dx
        dx1_ref[...] = dx1
        dmo_ref[...] = (gt_ref[...] * dx1).astype(BF16)
        acc[0] += _fold8(t_sh)
        acc[1] += _fold8(t_sc)
        acc[2] += _fold8(t_g)
        acc[3] += _fold8(dx1 * mo_ref[...].astype(F32))

        @pl.when(i == n - 1)
        def _():
            sums_ref[...] = jnp.sum(acc[...], axis=1)

    return _row_call(body, name=name, row_ins=[dff, x1, dy, mo], res_ins=[w_ffn_in, g2, sc2, gt1],
                     row_outs=[(d, F32), (d, BF16)], acc_outs=[((4, d), F32)], scratch=[pltpu.VMEM((4, 8, d), F32)])


def _post_attn_bwd(dmo, gates, o_a, o_g, l_g, w_pa, w_pb, w_o, b_gate, *, name):
    d = D_MODEL

    def body(i, n, rin, res, rout, aout, scr):
        dmo_ref, gates_ref, oa_ref, o0, o1, o2, l0, l1, l2 = rin
        wpa_ref, wpb_ref, wo_ref, b_ref = res
        dpa_ref, dpb_ref, dgates_ref, doa_ref = rout[:4]
        do_refs, dl_refs = rout[4:7], rout[7:10]
        (dbg_ref,) = aout
        acc, stage = scr

        @pl.when(i == 0)
        def _():
            acc[...] = jnp.zeros_like(acc)

        ogs = [_natural(r, stage, 0) for r in (o0, o1, o2)]
        ws = _mix_weights([_natural(r, stage, 2) for r in (l0, l1, l2)])
        obb = (ws[0] * ogs[0] + ws[1] * ogs[1] + ws[2] * ogs[2]).astype(BF16)
        pa = _dot(oa_ref[...].astype(BF16), wpa_ref[...])
        pb = _dot(obb, wpb_ref[...])
        ga = _sigmoid(gates_ref[:, :d].astype(F32) + b_ref[:, :d])
        gb = _sigmoid(gates_ref[:, d:].astype(F32) + b_ref[:, d:])
        dm = _dot_nt(dmo_ref[...], wo_ref[...])
        dpa = (dm * ga).astype(BF16)
        dpb = (dm * gb).astype(BF16)
        dpa_ref[...] = dpa
        dpb_ref[...] = dpb
        dga = dm * pa * ga * (1.0 - ga)
        dgb = dm * pb * gb * (1.0 - gb)
        dgates_ref[:, :d] = dga.astype(BF16)
        dgates_ref[:, d:] = dgb.astype(BF16)
        acc[:, :d] += _fold8(dga)
        acc[:, d:] += _fold8(dgb)
        doa_ref[...] = _dot_nt(dpa, wpa_ref[...])
        dob = _dot_nt(dpb, wpb_ref[...])
        lo = _head_lanes()
        for ch in range(WB_OUT // LANES):
            sl = slice(ch * LANES, (ch + 1) * LANES)
            dv = dob[:, sl]
            wc = [w[:, sl] for w in ws]
            ts = [_head_mean(dv * og[:, sl], lo) * float(HEAD_DIM) for og in ogs]
            tbar = wc[0] * ts[0] + wc[1] * ts[1] + wc[2] * ts[2]
            for g in range(3):
                for k, (ref, val) in enumerate(((do_refs[g], wc[g] * dv), (dl_refs[g], wc[g] * (ts[g] - tbar)))):
                    if len(ref.shape) == 2:
                        ref[:, sl] = val
                    else:
                        _to_residue(val, ref, ch * LANES, stage, (2 * g + k) % 4)

        @pl.when(i == n - 1)
        def _():
            dbg_ref[...] = jnp.sum(acc[...], axis=0, keepdims=True)

    return _row_call(body, name=name, row_ins=[dmo, gates, o_a, *o_g, *l_g], res_ins=[w_pa, w_pb, w_o, b_gate],
                     row_outs=[(d, BF16), (d, BF16), (2 * d, BF16), (WA, F32)]
                     + 2 * [(WB_OUT, F32, dd) for _, dd in DIL_CONFIGS],
                     acc_outs=[((1, 2 * d), F32)], scratch=[pltpu.VMEM((8, 2 * d), F32), _stage_shape(name)])


def _na_class_tables():
    ro = np.full((3, NA_QROWS, 2 * NA_PAIRS), NA_RO_NONE, np.int64)
    slot = np.zeros((3, NA_QROWS, NA_PAIRS), np.int64)
    for t in range(3):
        for a in range(NA_QROWS):
            qr = _NA_CLASS_R0[t] + a
            rs = min(max(qr - NA_KH // 2, 0), ROWS - NA_KH)
            for b in range(NA_KROWS):
                kr = _NA_CLASS_K0[t] + b
                if rs <= kr < rs + NA_KH:
                    ro[t, a, b] = kr - qr + (NA_KH - 1)
            for j in range(NA_PAIRS):
                slot[t, a, j] = 2 * j - a + (_NA_CLASS_K0[t] - _NA_CLASS_R0[t] + NA_KH - 1) + (NA_QROWS - 1)
    assert slot.min() >= 0 and slot.max() < NA_SLOTS
    return ro, slot


def _na_build_bias(i, cls_ref, rp_ref, cm_ref, bias_scr):
    ro, _ = _na_class_tables()
    lo = _head_lanes()
    first = jnp.logical_or(i == 0, cls_ref[i] != cls_ref[jnp.maximum(i - 1, 0)])
    for t in range(3):
        @pl.when(jnp.logical_and(first, cls_ref[i] == t))
        def _():
            for hh in range(2):
                for a in range(NA_QROWS):
                    for j in range(NA_PAIRS):
                        r0, r1 = int(ro[t, a, 2 * j]), int(ro[t, a, 2 * j + 1])
                        x0 = jnp.broadcast_to(rp_ref[hh, r0:r0 + 1, :], (GRID_W, LANES))
                        x1 = jnp.broadcast_to(rp_ref[hh, r1:r1 + 1, :], (GRID_W, LANES))
                        blk = jnp.where(lo, pltpu.roll(x0, GRID_W + 1, 1, stride=1, stride_axis=0),
                                        pltpu.roll(x1, 1, 1, stride=1, stride_axis=0))
                        bias_scr[hh, a * GRID_W:(a + 1) * GRID_W, j * LANES:(j + 1) * LANES] = blk + cm_ref[...]
    return first


def _attn_fwd(qkv, qc0, kc0, vc0, npairs, table, kstart, cls, nk, *, name, na=None, qb=Q_BLOCK):
    s = qkv.shape[0]

    def body(ks_ref, cls_ref, q_ref, k_ref, v_ref, b_ref, *rest):
        if na:
            cm_ref, o_ref, lse_ref, bias_scr = rest
        else:
            o_ref, lse_ref = rest
        i = pl.program_id(1)
        if na:
            _na_build_bias(i, cls_ref, b_ref, cm_ref, bias_scr)
        ks = pl.multiple_of(ks_ref[i], 64)
        q2 = q_ref[...]
        k2 = k_ref[pl.ds(ks, nk), :]
        v2 = v_ref[pl.ds(ks, nk), :]
        lo = _head_lanes()
        outs, lses = [], []
        for h in range(2):
            qm = jnp.where(lo if h == 0 else jnp.logical_not(lo), q2, jnp.zeros_like(q2))
            sc = _dot_nt(qm, k2) * SCALE + (bias_scr[h, :, :nk] if na else b_ref[0, 0])
            m = jnp.max(sc, axis=1, keepdims=True)
            p = jnp.exp(sc - m)
            l = jnp.sum(p, axis=1, keepdims=True)
            pv = _dot(p.astype(BF16), v2)
            outs.append(pv / l)
            lses.append(m + jnp.log(l))
        o_ref[...] = jnp.where(lo, outs[0], outs[1])
        lse_ref[...] = jnp.where(lo, lses[0], lses[1])

    w = npairs * LANES
    in_specs = [
        pl.BlockSpec((qb, LANES), lambda p, i, ks, cl: (i, qc0 + p)),
        pl.BlockSpec((s, LANES), lambda p, i, ks, cl: (0, kc0 + p)),
        pl.BlockSpec((s, LANES), lambda p, i, ks, cl: (0, vc0 + p)),
    ]
    if na:
        in_specs += _na_bias_specs()
        args, scratch = (kstart, cls, qkv, qkv, qkv, *na), [pltpu.VMEM((2, Q_BLOCK, NA_W), F32)]
    else:
        in_specs.append(pl.BlockSpec((1, 1, qb, nk), lambda p, i, ks, cl: (cl[i], 0, 0, 0)))
        args, scratch = (kstart, cls, qkv, qkv, qkv, table), []
    grid_spec = pltpu.PrefetchScalarGridSpec(
        num_scalar_prefetch=2, grid=(npairs, s // qb), in_specs=in_specs,
        out_specs=[pl.BlockSpec((qb, LANES), lambda p, i, ks, cl: (i, p)),
                   pl.BlockSpec((qb, LANES), lambda p, i, ks, cl: (i, p))],
        scratch_shapes=scratch,
    )
    return _pcall(body, name=name, grid_spec=grid_spec,
                  out_shape=[jax.ShapeDtypeStruct((s, w), F32), jax.ShapeDtypeStruct((s, w), F32)],
                  compiler_params=pltpu.CompilerParams(dimension_semantics=("parallel", "arbitrary")),
                  )(*args)


def _na_bias_specs():
    return [pl.BlockSpec((2, 16, LANES), lambda p, i, ks, cl: (p, 0, 0)),
            pl.BlockSpec((GRID_W, LANES), lambda p, i, ks, cl: (0, 0))]


def _attn_bwd(qkv, qc0, kc0, vc0, npairs, table, kstart, cls, nk, do, o, lse, *, name, dlse=None, na=None,
              qb=Q_BLOCK, order=None):
    s = qkv.shape[0]
    has_dlse = dlse is not None
    _, slot = _na_class_tables()

    def body(ks_ref, cls_ref, q_ref, k_ref, v_ref, b_ref, *rest):
        if na:
            cm_ref, rest = rest[0], rest[1:]
        do_ref, o_ref, lse_ref, rest = rest[0], rest[1], rest[2], rest[3:]
        if has_dlse:
            dlse_ref, rest = rest[0], rest[1:]
        if order is not None:
            rest = rest[1:]
        dq_ref, dk_ref, dv_ref = rest[0], rest[1], rest[2]
        if na:
            bank_ref, bias_scr, dbias_scr, bank_scr = rest[3:]
        i = pl.program_id(1)
        if na:
            first = _na_build_bias(i, cls_ref, b_ref, cm_ref, bias_scr)

        @pl.when(i == 0)
        def _():
            dk_ref[...] = jnp.zeros_like(dk_ref)
            dv_ref[...] = jnp.zeros_like(dv_ref)
            if na:
                dbias_scr[...] = jnp.zeros_like(dbias_scr)
                bank_scr[...] = jnp.zeros_like(bank_scr)

        ks = pl.multiple_of(ks_ref[i], 64)
        q2 = q_ref[...]
        k2 = k_ref[pl.ds(ks, nk), :]
        v2 = v_ref[pl.ds(ks, nk), :]
        do2 = do_ref[...]
        lse2 = lse_ref[...]
        doo = do2 * o_ref[...]
        do2b = do2.astype(BF16)
        lo = _head_lanes()
        lane = lax.broadcasted_iota(jnp.int32, (1, LANES), 1)
        dqs, dks, dvs = [], [], []
        for h in range(2):
            mh = lo if h == 0 else jnp.logical_not(lo)
            qm = jnp.where(mh, q2, jnp.zeros_like(q2))
            sc = _dot_nt(qm, k2) * SCALE + (bias_scr[h, :, :nk] if na else b_ref[0, 0])
            lse_h = jnp.max(jnp.where(mh, lse2, NEG), axis=1, keepdims=True)
            p = jnp.exp(sc - lse_h)
            delta = jnp.sum(jnp.where(mh, doo, 0.0), axis=1, keepdims=True)
            dom = jnp.where(mh, do2b, jnp.zeros_like(do2b))
            dp = _dot_nt(dom, v2)
            t = dp - delta
            if has_dlse:
                t = t + jnp.sum(jnp.where(lane == h * HEAD_DIM, dlse_ref[...], 0.0), axis=1, keepdims=True)
            ds = p * t
            if na:
                @pl.when(first)
                def _():
                    dbias_scr[h, :, :nk] = ds

                @pl.when(jnp.logical_not(first))
                def _():
                    dbias_scr[h, :, :nk] += ds
            dsb = ds.astype(BF16)
            dqs.append(_dot(dsb, k2))
            dks.append(lax.dot_general(dsb, q2, _TN, preferred_element_type=F32))
            dvs.append(lax.dot_general(p.astype(BF16), do2b, _TN, preferred_element_type=F32))
        dq_ref[...] = jnp.where(lo, dqs[0], dqs[1]) * SCALE
        dk_ref[pl.ds(ks, nk), :] += jnp.where(lo, dks[0], dks[1]) * SCALE
        dv_ref[pl.ds(ks, nk), :] += jnp.where(lo, dvs[0], dvs[1])
        if na:
            last = jnp.logical_or(i == N_QBLK - 1, cls_ref[i] != cls_ref[jnp.minimum(i + 1, N_QBLK - 1)])
            for t in range(3):
                @pl.when(jnp.logical_and(last, cls_ref[i] == t))
                def _():
                    for hh in range(2):
                        for a in range(NA_QROWS):
                            for j in range(NA_PAIRS):
                                bank_scr[hh, int(slot[t, a, j])] += dbias_scr[
                                    hh, a * GRID_W:(a + 1) * GRID_W, j * LANES:(j + 1) * LANES]

            @pl.when(i == N_QBLK - 1)
            def _():
                bank_ref[...] = bank_scr[...]

    w = npairs * LANES
    blk = lambda: pl.BlockSpec((qb, LANES), lambda p, i, ks, cl: (i, p))
    full = lambda: pl.BlockSpec((s, LANES), lambda p, i, ks, cl: (0, p))
    in_specs = [
        pl.BlockSpec((qb, LANES), lambda p, i, ks, cl: (i, qc0 + p)),
        pl.BlockSpec((s, LANES), lambda p, i, ks, cl: (0, kc0 + p)),
        pl.BlockSpec((s, LANES), lambda p, i, ks, cl: (0, vc0 + p)),
    ]
    if na:
        in_specs += _na_bias_specs()
        args = [kstart, cls, qkv, qkv, qkv, *na]
    else:
        in_specs.append(pl.BlockSpec((1, 1, qb, nk), lambda p, i, ks, cl: (cl[i], 0, 0, 0)))
        args = [kstart, cls, qkv, qkv, qkv, table]
    in_specs += [blk(), blk(), blk()]
    args += [do, o, lse]
    if has_dlse:
        in_specs.append(blk())
        args.append(dlse)
    out_specs = [blk(), full(), full()]
    out_shape = [jax.ShapeDtypeStruct((s, w), F32)] * 3
    scratch = []
    if na:
        bank_shape = (2, NA_SLOTS, GRID_W, LANES)
        out_specs.append(pl.BlockSpec(bank_shape, lambda p, i, ks, cl: (p, 0, 0, 0)))
        out_shape.append(jax.ShapeDtypeStruct((2 * npairs,) + bank_shape[1:], F32))
        scratch = [pltpu.VMEM((2, Q_BLOCK, NA_W), F32), pltpu.VMEM((2, Q_BLOCK, NA_W), F32), pltpu.VMEM(bank_shape, F32)]
    if order is not None:
        in_specs.append(pl.BlockSpec(order.shape, lambda p, i, ks, cl: (0, 0)))
        args.append(order)
    grid_spec = pltpu.PrefetchScalarGridSpec(num_scalar_prefetch=2, grid=(npairs, s // qb), in_specs=in_specs,
                                             out_specs=out_specs, scratch_shapes=scratch)
    return _pcall(body, name=name, grid_spec=grid_spec, out_shape=out_shape,
                  compiler_params=pltpu.CompilerParams(dimension_semantics=("arbitrary", "arbitrary")))(*args)


_NA_CLASS_R0 = (0, NA_QROWS, ROWS - NA_QROWS)
_NA_CLASS_K0 = (0, 0, ROWS - NA_KROWS)
_RPB_RO = 2 * NA_KH - 1
_RPB_CO = 2 * NA_KW - 1
_BANK_ROWS = 48


def _na_constants():
    col = np.arange(GRID_W)
    cs = np.clip(col - NA_KW // 2, 0, GRID_W - NA_KW)
    vcol = (col[None, :] >= cs[:, None]) & (col[None, :] < cs[:, None] + NA_KW)
    colmask = np.where(np.concatenate([vcol, vcol], axis=1), 0.0, NEG).astype(np.float32)
    co = col[None, :] - col[:, None] + (NA_KW - 1)
    oh_col = np.zeros((GRID_W * GRID_W, LANES), np.float32)
    for qc in range(GRID_W):
        for kc in range(GRID_W):
            if vcol[qc, kc]:
                oh_col[qc * GRID_W + kc, co[qc, kc]] = 1.0
    ks = np.clip(np.arange(N_QBLK) * NA_QROWS - NA_KH // 2, 0, ROWS - NA_KROWS) * GRID_W
    cls = np.ones(N_QBLK, np.int32)
    cls[0], cls[-1] = 0, 2
    return colmask, oh_col, ks.astype(np.int32), cls


def _bank_reduce(bank, oh_col, *, name):
    def body(d_ref, ohc_ref, o_ref):
        o_ref[0] = jnp.dot(d_ref[0], ohc_ref[...], preferred_element_type=F32, precision=lax.Precision.HIGHEST)

    return _pcall(
        body, name=name, grid=(NA_HEADS,),
        in_specs=[pl.BlockSpec((1, _BANK_ROWS, GRID_W * GRID_W), lambda h: (h, 0, 0)),
                  pl.BlockSpec((GRID_W * GRID_W, LANES), lambda h: (0, 0))],
        out_specs=pl.BlockSpec((1, _BANK_ROWS, LANES), lambda h: (h, 0, 0)),
        out_shape=jax.ShapeDtypeStruct((NA_HEADS, _BANK_ROWS, LANES), F32), compiler_params=_PAR,
    )(bank, oh_col)


def _dil_constants(dilation):
    seg = SEQ // dilation
    nb = seg // DIL_QB
    nk = min(DIL_QB + 2 * DIL_HALF, seg)
    starts = [min(max(blk * DIL_QB - DIL_HALF, 0), seg - nk) for blk in range(nb)]
    shifts = sorted({w0 - blk * DIL_QB for blk, w0 in enumerate(starts)}, reverse=True)
    qi = np.arange(DIL_QB)[:, None]
    ki = np.arange(nk)[None, :]
    mask = np.stack([np.where(np.abs(ki + sh - qi) <= DIL_HALF, 0.0, NEG) for sh in shifts]).astype(np.float32)
    ks, cls = [], []
    for i in range(SEQ // DIL_QB):
        sub, blk = divmod(i, nb)
        cls.append(shifts.index(starts[blk] - blk * DIL_QB))
        ks.append(sub * seg + starts[blk])
    return mask.reshape(len(shifts), 1, DIL_QB, nk), np.asarray(ks, np.int32), np.asarray(cls, np.int32), nk


_VM = pl.BlockSpec(memory_space=pltpu.VMEM)


def _ada_fwd(c_all, w, b, *, name):
    def body(c_ref, w_ref, b_ref, o_ref):
        cv = c_ref[...]
        o_ref[...] = jnp.dot(cv * _sigmoid(cv), w_ref[...], preferred_element_type=F32,
                             precision=lax.Precision.HIGHEST) + b_ref[...]

    return _pcall(body, name=name, in_specs=[_VM, _VM, _VM], out_specs=_VM,
                  out_shape=jax.ShapeDtypeStruct((c_all.shape[0], w.shape[1]), F32))(c_all, w, b)


def _ada_bwd(c_all_t, dmod, *, name):
    def body(c_ref, d_ref, o_ref):
        cv = c_ref[...]
        o_ref[...] = jnp.dot(cv * _sigmoid(cv), d_ref[...], preferred_element_type=F32,
                             precision=lax.Precision.HIGHEST)

    return _pcall(body, name=name, in_specs=[_VM, _VM], out_specs=_VM,
                  out_shape=jax.ShapeDtypeStruct((c_all_t.shape[0], dmod.shape[1]), F32))(c_all_t, dmod)


def _row_sum(t, *, name):
    def body(t_ref, o_ref):
        o_ref[...] = jnp.sum(t_ref[...], axis=0, keepdims=True)

    return _pcall(body, name=name, in_specs=[_VM], out_specs=_VM,
                  out_shape=jax.ShapeDtypeStruct((1, t.shape[1]), F32))(t)


def _row_tile(rows):
    tr = rows
    for cand in range(8, 513, 8):
        if rows % cand == 0:
            tr = cand
    return tr


def _adamw_math(wv, gv, mv, vv):
    nm = ADAM_B1 * mv + (1.0 - ADAM_B1) * gv
    nv = ADAM_B2 * vv + (1.0 - ADAM_B2) * (gv * gv)
    m_hat = nm / (1.0 - ADAM_B1 ** ADAM_STEP)
    v_hat = nv / (1.0 - ADAM_B2 ** ADAM_STEP)
    return -ADAM_LR * (m_hat / (jnp.sqrt(v_hat) + ADAM_EPS) + ADAM_WD * wv), nm, nv


def _adamw(w, g, m, v, *, name):
    rows, cols = w.shape
    tr = _row_tile(rows)

    def body(w_ref, g_ref, m_ref, v_ref, d_ref, nm_ref, nv_ref):
        d_ref[...], nm_ref[...], nv_ref[...] = _adamw_math(w_ref[...], g_ref[...], m_ref[...], v_ref[...])

    spec = pl.BlockSpec((tr, cols), lambda i: (i, 0))
    return _pcall(body, name=name, grid=(rows // tr,), in_specs=[spec] * 4, out_specs=[spec] * 3,
                  out_shape=[jax.ShapeDtypeStruct((rows, cols), F32)] * 3, compiler_params=_PAR)(w, g, m, v)


def _adamw_halves(w, g_mine, g_other, m, v, c_arr, *, name):
    rows, cols = w.shape
    hr = rows // 2
    tr = _row_tile(hr)
    nt = hr // tr

    def body(c_ref, w_ref, t_ref, o_ref, m_ref, v_ref, g_ref, d_ref, nm_ref, nv_ref):
        gv = jnp.where(pl.program_id(0) == c_ref[0], t_ref[...], o_ref[...])
        g_ref[...] = gv
        d_ref[...], nm_ref[...], nv_ref[...] = _adamw_math(w_ref[...], gv, m_ref[...], v_ref[...])

    full = pl.BlockSpec((tr, cols), lambda h, i, c: (h * nt + i, 0))
    half = pl.BlockSpec((tr, cols), lambda h, i, c: (i, 0))
    grid_spec = pltpu.PrefetchScalarGridSpec(num_scalar_prefetch=1, grid=(2, nt),
                                             in_specs=[full, half, half, full, full], out_specs=[full] * 4)
    return _pcall(body, name=name, grid_spec=grid_spec, out_shape=[jax.ShapeDtypeStruct((rows, cols), F32)] * 4,
                  compiler_params=pltpu.CompilerParams(dimension_semantics=("parallel", "parallel")),
                  )(c_arr, w, g_mine, g_other, m, v)


_MESH = pl.DeviceIdType.MESH
_ANY = pl.BlockSpec(memory_space=pl.ANY)
_CHIP_FLIPS = ((1, 0), (0, 1), (1, 1))


def _pos():
    return lax.axis_index("x"), lax.axis_index("y"), lax.axis_index("c")


def _flip(v, f):
    return 1 - v if f else v


def _sem_pairs(n):
    return [pltpu.SemaphoreType.DMA((n,)), pltpu.SemaphoreType.DMA((n,))]


def _small_allgather(blk, *, name):
    m_per, n = blk.shape

    def body(x_ref, out_ref, send_sems, recv_sems, local_sem):
        x, y, c = _pos()
        me, sibling = (x, y, c), (x, y, 1 - c)
        chips = [(_flip(x, fx), _flip(y, fy)) for fx, fy in _CHIP_FLIPS]

        def rows(px, py, pc):
            return out_ref.at[pl.ds((4 * px + 2 * py + pc) * m_per, m_per), :]

        def copy(k, block, to, src=None):
            return pltpu.make_async_remote_copy(
                src_ref=rows(*block) if src is None else src, dst_ref=rows(*block),
                send_sem=send_sems.at[k], recv_sem=recv_sems.at[k], device_id=to, device_id_type=_MESH)

        mine = pltpu.make_async_copy(x_ref, rows(*me), local_sem)
        mine.start()
        first = [copy(0, me, sibling, src=x_ref)]
        first += [copy(1 + j, me, (*chip, c), src=x_ref) for j, chip in enumerate(chips)]
        for cp in first:
            cp.start()
        passed = [copy(4 + j, (*chip, c), sibling) for j, chip in enumerate(chips)]
        for j, chip in enumerate(chips):
            copy(1 + j, (*chip, c), me).wait_recv()
            passed[j].start()
        copy(0, sibling, me).wait_recv()
        for j, chip in enumerate(chips):
            copy(4 + j, (*chip, 1 - c), me).wait_recv()
        for cp in first + passed:
            cp.wait_send()
        mine.wait()

    return _pcall(
        body, name=name, out_shape=jax.ShapeDtypeStruct((N_DEV * m_per, n), blk.dtype),
        in_specs=[_VM], out_specs=_VM,
        scratch_shapes=_sem_pairs(7) + [pltpu.SemaphoreType.DMA],
    )(blk)


_HBM = pl.BlockSpec(memory_space=pltpu.HBM)
_SEM = pl.BlockSpec(memory_space=pltpu.SEMAPHORE)
_EFFECT = pltpu.SideEffectType.DATAFLOW_SIDE_EFFECTING


def _split_start(srcs, lands, plan, ncopies, after, *, name):
    ns, nl = len(srcs), len(lands)

    def body(*refs):
        src_refs, land_refs = refs[:ns], refs[ns:ns + nl]
        send_sems, recv_sems = refs[ns + nl + 1], refs[ns + nl + 2]
        token = refs[-1]
        x, y, c = _pos()
        for k, (src, dst, to, _) in enumerate(plan(x, y, c, src_refs, land_refs)):
            pltpu.make_async_remote_copy(src_ref=src, dst_ref=dst, send_sem=send_sems.at[k], recv_sem=recv_sems.at[k],
                                         device_id=to, device_id_type=_MESH).start()
        token[...] = jnp.zeros_like(token)

    hbm = lambda a: pltpu.HBM(a.shape, a.dtype)
    out = _pcall(
        body, name=name,
        out_shape=(pltpu.SemaphoreType.DMA((ncopies,)), pltpu.SemaphoreType.DMA((ncopies,)),
                   *[hbm(a) for a in srcs], *[hbm(a) for a in lands], jax.ShapeDtypeStruct((8, LANES), F32)),
        in_specs=[_HBM] * (ns + nl) + [_ANY], out_specs=(_SEM, _SEM, *[_HBM] * (ns + nl), _VM),
        input_output_aliases={i: 2 + i for i in range(ns + nl)},
        compiler_params=pltpu.CompilerParams(has_side_effects=_EFFECT),
    )(*[pltpu.with_memory_space_constraint(a, pltpu.HBM) for a in (*srcs, *lands)], after)
    return out[0], out[1], list(out[2:2 + ns]), list(out[2 + ns:2 + ns + nl]), out[-1]


def _split_wait(send_sems, recv_sems, srcs, lands, plan, after, *, name, with_sources=False):
    ns, nl = len(srcs), len(lands)

    def body(*refs):
        src_refs, land_refs = refs[:ns], refs[ns:ns + nl]
        send_sems, recv_sems = refs[ns + nl], refs[ns + nl + 1]
        x, y, c = _pos()
        for k, (src, _, _, mine) in enumerate(plan(x, y, c, src_refs, land_refs)):
            cp = pltpu.make_async_remote_copy(src_ref=src, dst_ref=mine, send_sem=send_sems.at[k],
                                              recv_sem=recv_sems.at[k], device_id=(x, y, c), device_id_type=_MESH)
            cp.wait_send()
            cp.wait_recv()

    hbm = lambda a: pltpu.HBM(a.shape, a.dtype)
    out = _pcall(
        body, name=name, out_shape=tuple(hbm(a) for a in (*srcs, *lands)),
        in_specs=[_HBM] * (ns + nl) + [_SEM, _SEM, _ANY], out_specs=tuple([_HBM] * (ns + nl)),
        input_output_aliases={i: i for i in range(ns + nl)},
        compiler_params=pltpu.CompilerParams(has_side_effects=_EFFECT),
    )(*srcs, *lands, send_sems, recv_sems, after)
    return (list(out[:ns]), list(out[ns:])) if with_sources else list(out[ns:])


def _ag_plan(nw):
    def plan(x, y, c, sh_refs, full_refs):
        j = 2 * x + y
        out = []
        for wi in range(nw):
            for fx, fy in _CHIP_FLIPS:
                px, py = _flip(x, fx), _flip(y, fy)
                out.append((sh_refs[wi].at[c], full_refs[wi].at[j, c], (px, py, c), full_refs[wi].at[2 * px + py, c]))
            out.append((sh_refs[wi], full_refs[wi].at[j], (x, y, 1 - c), full_refs[wi].at[j]))
        return out
    return plan


def _ag_pass(fulls, *, name):
    nw = len(fulls)

    def body(*refs):
        in_refs, out_refs = refs[:nw], refs[nw:2 * nw]
        send_sems, recv_sems = refs[2 * nw:]
        x, y, c = _pos()
        cps = []
        for wi in range(nw):
            for k, (fx, fy) in enumerate(_CHIP_FLIPS):
                jp = 2 * _flip(x, fx) + _flip(y, fy)
                sems = dict(send_sem=send_sems.at[3 * wi + k], recv_sem=recv_sems.at[3 * wi + k], device_id_type=_MESH)
                send = pltpu.make_async_remote_copy(src_ref=in_refs[wi].at[jp, c], dst_ref=out_refs[wi].at[jp, c],
                                                    device_id=(x, y, 1 - c), **sems)
                recv = pltpu.make_async_remote_copy(src_ref=in_refs[wi].at[jp, c], dst_ref=out_refs[wi].at[jp, 1 - c],
                                                    device_id=(x, y, c), **sems)
                cps.append((send, recv))
        for send, _ in cps:
            send.start()
        for send, recv in cps:
            send.wait_send()
            recv.wait_recv()

    return _pcall(body, name=name, out_shape=[jax.ShapeDtypeStruct(f.shape, f.dtype) for f in fulls],
                  in_specs=[_ANY] * nw, out_specs=[_ANY] * nw, input_output_aliases={i: i for i in range(nw)},
                  scratch_shapes=_sem_pairs(3 * nw))(*fulls)


def _sib_plan(nw):
    def plan(x, y, c, g_refs, ra_refs):
        return [(g_refs[wi].at[k, 1 - c], ra_refs[wi].at[k], (x, y, 1 - c), ra_refs[wi].at[k])
                for wi in range(nw) for k in range(N_CHIP)]
    return plan


def _rs_plan(nw):
    def plan(x, y, c, s_refs, rb_refs):
        out = []
        for wi in range(nw):
            for k, (fx, fy) in enumerate(_CHIP_FLIPS):
                px, py = _flip(x, fx), _flip(y, fy)
                out.append((s_refs[wi].at[2 * px + py], rb_refs[wi].at[k], (px, py, c), rb_refs[wi].at[k]))
        return out
    return plan


def _sibling_swap(ts, *, name):
    nw = len(ts)

    def body(*refs):
        t_refs, out_refs = refs[:nw], refs[nw:2 * nw]
        send_sems, recv_sems = refs[2 * nw:]
        x, y, c = _pos()
        cps = [pltpu.make_async_remote_copy(src_ref=t_refs[wi], dst_ref=out_refs[wi], send_sem=send_sems.at[wi],
                                            recv_sem=recv_sems.at[wi], device_id=(x, y, 1 - c), device_id_type=_MESH)
               for wi in range(nw)]
        for cp in cps:
            cp.start()
        for cp in cps:
            cp.wait()

    return _pcall(body, name=name, out_shape=[jax.ShapeDtypeStruct(t.shape, t.dtype) for t in ts],
                  in_specs=[_ANY] * nw, out_specs=[_ANY] * nw, scratch_shapes=_sem_pairs(nw))(*ts)


def _rs_add(g, ra, c_arr, *, name):
    n, _, r, w = g.shape

    def body(c_ref, g_ref, ra_ref, s_ref, sb_ref):
        t = g_ref[...] + ra_ref[...]
        s_ref[...] = t
        sb_ref[...] = t.astype(BF16)

    grid_spec = pltpu.PrefetchScalarGridSpec(
        num_scalar_prefetch=1, grid=(n,),
        in_specs=[pl.BlockSpec((None, None, r, w), lambda k, c: (k, c[0], 0, 0)),
                  pl.BlockSpec((None, r, w), lambda k, c: (k, 0, 0))],
        out_specs=[pl.BlockSpec((None, r, w), lambda k, c: (k, 0, 0))] * 2)
    return _pcall(body, name=name, grid_spec=grid_spec,
                  out_shape=[jax.ShapeDtypeStruct((n, r, w), F32), jax.ShapeDtypeStruct((n, r, w), BF16)],
                  compiler_params=_PAR)(c_arr, g, ra)


def _rs_final(s, rb, j_arr, *, name):
    _, r, w = s.shape

    def body(j_ref, s_ref, rb_ref, t_ref):
        t_ref[...] = ((s_ref[...] + rb_ref[0].astype(F32)) + rb_ref[1].astype(F32)) + rb_ref[2].astype(F32)

    grid_spec = pltpu.PrefetchScalarGridSpec(
        num_scalar_prefetch=1, grid=(1,),
        in_specs=[pl.BlockSpec((None, r, w), lambda i, j: (j[0], 0, 0)),
                  pl.BlockSpec((3, r, w), lambda i, j: (0, 0, 0))],
        out_specs=pl.BlockSpec((r, w), lambda i, j: (0, 0)))
    return _pcall(body, name=name, grid_spec=grid_spec, out_shape=jax.ShapeDtypeStruct((r, w), F32),
                  compiler_params=_ARB)(j_arr, s, rb)


def _tile2(g):
    return jnp.concatenate([g, g], axis=1)


_BIG = ("w_in", "w_ffn_in", "w_ffn_out", "w_o", "w_proj_a", "w_proj_b")
_BIG_SHARD = {"w_in": (1024, 1472), "w_ffn_in": (1024, 1408), "w_ffn_out": (704, 1024), "w_o": (256, 1024),
              "w_proj_a": (512, 256), "w_proj_b": (256, 256)}


def _device_step(x2, tgt, mod, first_weights, late_weights, early_grads, mid_grads, g_norm1, g_norm2, b_gate, g_qa, g_ka,
                 g_qb, g_kb, rpb):
    d = D_MODEL
    sh1, sc1, gt1, sh2, sc2, gt2 = [mod[:, k * d:(k + 1) * d] for k in range(6)]

    colmask, oh_col, na_ks, na_cls = _na_constants()
    rp = jnp.pad(rpb, ((0, 0), (0, 16 - _RPB_RO), (RP_LANE0, LANES - RP_LANE0 - _RPB_CO)), constant_values=NEG)
    na = (rp, jnp.asarray(colmask))
    na_ks, na_cls = jnp.asarray(na_ks), jnp.asarray(na_cls)
    dil = [_dil_constants(dd) for _, dd in DIL_CONFIGS]
    gains = jnp.concatenate([_tile2(g_qa), _tile2(g_ka), _tile2(g_qb), _tile2(g_kb)], axis=0)
    cos_t, sa_t, sb_t = _rope_tables()

    wts = first_weights(cos_t)
    h1, qkvn, qk_pre, gates, *qkv_dil = _pre_attn_fwd(x2, cos_t, sa_t, sb_t, g_norm1, sc1, sh1, wts["w_qkv"],
                                                      wts["w_gates"], gains, name="pre_attn_fwd")
    o_a, lse_a = _attn_fwd(qkvn, 0, 4, 8, 4, None, na_ks, na_cls, NA_NK, name="attn_a_fwd", na=na)
    arrs, o_g, l_g = [], [], []
    res = lambda t, dd: t if dd == 1 else (t, dd)
    for g, (_, dd) in enumerate(DIL_CONFIGS):
        tab_g, ks_g, cls_g, nk_g = jnp.asarray(dil[g][0]), jnp.asarray(dil[g][1]), jnp.asarray(dil[g][2]), dil[g][3]
        arr, cb = (qkvn, (12, 18, 24)) if dd == 1 else (qkv_dil.pop(0), (0, 2, 4))
        op, lp = _attn_fwd(arr, cb[0], cb[1], cb[2], 2, tab_g, ks_g, cls_g, nk_g, name=f"attn_d{g}_fwd", qb=DIL_QB)
        arrs.append((arr, cb, tab_g, ks_g, cls_g, nk_g))
        o_g.append(res(op, dd))
        l_g.append(res(lp, dd))
    wts = dict(wts, **late_weights(o_a))
    o_b, merged, mo, x1, h2 = _post_attn_fwd(o_a, o_g, l_g, gates, x2, wts["w_pa"], wts["w_pb"], wts["w_o"], b_gate,
                                             gt1, g_norm2, sc2, sh2, name="post_attn_fwd")
    act, ff = _ffn_fwd(h2, wts["w_ffn_in"], name="ffn_fwd")

    dy, dffo, dff, dgt2, loss_v = _ffn_mid(act, ff, x1, tgt, wts["w_ffn_out"], gt2, name="ffn_mid")
    grads = {}
    g_ffn_out = _wgrad(act, dffo, name="wg_ffn_out", tm=D_FF // 2, tn=d)
    grads["w_ffn_out"] = g_ffn_out.reshape(N_CHIP, D_FF // N_CHIP, d)
    grads["w_ffn_in"] = _wgrad(h2, dff, name="wg_ffn_in", tm=512, tn=2 * FF_CHIP, chips=N_CHIP)
    dx1, dmo, sums2 = _ffn_in_bwd(dff, x1, dy, mo, wts["w_ffn_in"], g_norm2, sc2, gt1, name="ffn_in_bwd")
    grads["w_o"] = _wgrad(merged, dmo, name="wg_o", tm=d, tn=d).reshape(N_CHIP, d // N_CHIP, d)
    pab = _post_attn_bwd(dmo, gates, o_a, o_g, l_g, wts["w_pa"], wts["w_pb"], wts["w_o"], b_gate, name="post_attn_bwd")
    dpa, dpb, dgates, do_a = pab[:4]
    do_g, dl_g, dbg = pab[4:7], pab[7:10], pab[10]
    g_pa = _wgrad(o_a, dpa, name="wg_pa", tm=WA, tn=d)
    g_pb = _wgrad(o_b, dpb, name="wg_pb", tm=WB_OUT, tn=d)
    grads["w_proj_a"] = g_pa.reshape(WA, N_CHIP, d // N_CHIP).transpose(1, 0, 2)
    grads["w_proj_b"] = g_pb.reshape(WB_OUT, N_CHIP, d // N_CHIP).transpose(1, 0, 2)
    order = early_grads(grads)
    dqs, dks, dvs = [], [], []
    for g, (_, dd) in enumerate(DIL_CONFIGS):
        arr, cb, tab_g, ks_g, cls_g, nk_g = arrs[g]
        plain = lambda t: t[0] if isinstance(t, tuple) else t
        dq, dk, dv = _attn_bwd(arr, cb[0], cb[1], cb[2], 2, tab_g, ks_g, cls_g, nk_g, do_g[g], plain(o_g[g]),
                               plain(l_g[g]), name=f"attn_d{g}_bwd", dlse=dl_g[g], qb=DIL_QB, order=order)
        dqs.append(res(dq, dd))
        dks.append(res(dk, dd))
        dvs.append(res(dv, dd))
    order = mid_grads(dv)
    dqa, dka, dva, bank = _attn_bwd(qkvn, 0, 4, 8, 4, None, na_ks, na_cls, NA_NK, do_a, o_a, lse_a,
                                    name="attn_a_bwd", na=na, order=order)
    dqkv, grad_x, dgains, sums1 = _pre_attn_bwd(qk_pre, [dqa, dka, dva] + dqs + dks + dvs, dgates, x2, dx1, cos_t, sa_t,
                                                sb_t, wts["w_qkv"], wts["w_gates"], gains, g_norm1, sc1,
                                                name="pre_attn_bwd")
    g_qkv = _wgrad(h1, dqkv, name="wg_qkv", tm=d, tn=W_QKV // 2)
    g_gates = _wgrad(h1, dgates, name="wg_gates", tm=d, tn=W_GATES)
    nc, cut = _BIG_SHARD["w_in"][1], 3 * _BIG_SHARD["w_in"][1] - W_QKV
    grads["w_in"] = jnp.stack([g_qkv[:, :nc], g_qkv[:, nc:2 * nc],
                               jnp.concatenate([g_qkv[:, 2 * nc:], g_gates[:, :cut]], axis=1), g_gates[:, cut:]])

    bank = bank.reshape(NA_HEADS, NA_SLOTS, GRID_W, 2, GRID_W).transpose(0, 1, 3, 2, 4)
    bank = jnp.pad(bank.reshape(NA_HEADS, 2 * NA_SLOTS, GRID_W * GRID_W), ((0, 0), (0, _BANK_ROWS - 2 * NA_SLOTS), (0, 0)))
    g2 = _bank_reduce(bank, jnp.asarray(oh_col), name="rpb_reduce")[:, :2 * NA_SLOTS].reshape(NA_HEADS, NA_SLOTS, 2, LANES)
    g_rpb = g2[:, 3:3 + _RPB_RO, 0, :_RPB_CO] + g2[:, 2:2 + _RPB_RO, 1, :_RPB_CO]

    dmod = jnp.concatenate([sums1[0:1], sums1[1:2], sums2[3:4], sums2[0:1], sums2[1:2], dgt2], axis=1)
    small = dict(g_norm1=sums1[2:3], g_norm2=sums2[2:3], b_gate=dbg, g_qa=dgains[0:1, :HEAD_DIM],
                 g_ka=dgains[1:2, :HEAD_DIM], g_qb=dgains[2:3, :HEAD_DIM], g_kb=dgains[3:4, :HEAD_DIM], rpb=g_rpb)
    return loss_v, grad_x, grads, dmod, small


_SMALL = ("b_ada", "g_norm1", "g_norm2", "b_gate", "g_qa", "g_ka", "g_qb", "g_kb", "rpb")
_SMALL_N = {"b_ada": 6 * D_MODEL, "g_norm1": D_MODEL, "g_norm2": D_MODEL, "b_gate": 2 * D_MODEL, "g_qa": HEAD_DIM,
            "g_ka": HEAD_DIM, "g_qb": HEAD_DIM, "g_kb": HEAD_DIM, "rpb": NA_HEADS * _RPB_RO * _RPB_CO}


def _pack_small(parts):
    flat = [parts[n].reshape(1, _SMALL_N[n]) for n in _SMALL]
    used = sum(_SMALL_N.values())
    return jnp.concatenate(flat + [jnp.zeros((1, STATS_W - used), F32)], axis=1)


def _unpack_small(v, shapes):
    out, at = {}, 0
    for n in _SMALL:
        out[n] = v[:, at:at + _SMALL_N[n]].reshape(shapes[n])
        at += _SMALL_N[n]
    return out


def _join_cols(t):
    _, r, c = t.shape
    return t.transpose(1, 0, 2).reshape(r, N_CHIP * c)


def kernel(x, c, w_ada, b_ada, g_norm1, g_norm2, w_in, b_gate, g_qa, g_ka, g_qb, g_kb, rpb, w_proj_a, w_proj_b, w_o, w_ffn_in, w_ffn_out, loss_target, m_w_ada, m_b_ada, m_g_norm1, m_g_norm2, m_w_in, m_b_gate, m_g_qa, m_g_ka, m_g_qb, m_g_kb, m_rpb, m_w_proj_a, m_w_proj_b, m_w_o, m_w_ffn_in, m_w_ffn_out, v_w_ada, v_b_ada, v_g_norm1, v_g_norm2, v_w_in, v_b_gate, v_g_qa, v_g_ka, v_g_qb, v_g_kb, v_rpb, v_w_proj_a, v_w_proj_b, v_w_o, v_w_ffn_in, v_w_ffn_out):
    names = ("w_ada", "b_ada", "g_norm1", "g_norm2", "w_in", "b_gate", "g_qa", "g_ka", "g_qb", "g_kb", "rpb",
             "w_proj_a", "w_proj_b", "w_o", "w_ffn_in", "w_ffn_out")
    w = dict(zip(names, (w_ada, b_ada, g_norm1, g_norm2, w_in, b_gate, g_qa, g_ka, g_qb, g_kb, rpb, w_proj_a, w_proj_b,
                         w_o, w_ffn_in, w_ffn_out)))
    m = dict(zip(names, (m_w_ada, m_b_ada, m_g_norm1, m_g_norm2, m_w_in, m_b_gate, m_g_qa, m_g_ka, m_g_qb, m_g_kb, m_rpb,
                         m_w_proj_a, m_w_proj_b, m_w_o, m_w_ffn_in, m_w_ffn_out)))
    v = dict(zip(names, (v_w_ada, v_b_ada, v_g_norm1, v_g_norm2, v_w_in, v_b_gate, v_g_qa, v_g_ka, v_g_qb, v_g_kb, v_rpb,
                         v_w_proj_a, v_w_proj_b, v_w_o, v_w_ffn_in, v_w_ffn_out)))
    d = D_MODEL
    xi, yi, ci = _pos()
    chip = 2 * xi + yi
    me = 2 * chip + ci
    ada_cols = 6 * d // N_CHIP

    c_arr, chip_arr = ci.reshape(1).astype(jnp.int32), chip.reshape(1).astype(jnp.int32)
    first, rest = _BIG[:1], _BIG[1:]

    c_all = _small_allgather(c.reshape(8, d // 8), name="ag_c").reshape(N_DEV, d)
    b_sh = lax.dynamic_slice(b_ada, (0, chip * ada_cols), (1, ada_cols))
    mod_part = _ada_fwd(c_all, w_ada[0], b_sh, name="ada_fwd")
    mod_all = _small_allgather(mod_part, name="ag_mod").reshape(N_CHIP, 2, 8, ada_cols)[:, 0]
    mod = lax.dynamic_index_in_dim(mod_all, me, axis=1, keepdims=False).reshape(1, 6 * d)

    halves = {n: (2, _BIG_SHARD[n][0] // 2, _BIG_SHARD[n][1]) for n in _BIG}
    shards = {n: w[n][0].astype(BF16).reshape(halves[n]) for n in _BIG}
    land = lambda n: lax.empty((N_CHIP,) + halves[n], BF16)
    ag1 = _split_start([shards[n] for n in first], [land(n) for n in first], _ag_plan(1), 4, mod, name="ag1_start")
    ag2 = _split_start([shards[n] for n in rest], [land(n) for n in rest], _ag_plan(len(rest)), 4 * len(rest),
                       ag1[4], name="ag2_start")
    rpb_after = rpb[0] + ag2[4][0, 0]

    def first_weights(after):
        full1 = _split_wait(ag1[0], ag1[1], ag1[2], ag1[3], _ag_plan(1), after, name="ag1_wait")
        p_in = _ag_pass(full1, name="ag1_pass")[0].reshape((N_CHIP,) + _BIG_SHARD["w_in"])
        cut = W_QKV - 2 * _BIG_SHARD["w_in"][1]
        return dict(w_qkv=jnp.concatenate([p_in[0], p_in[1], p_in[2][:, :cut]], axis=1),
                    w_gates=jnp.concatenate([p_in[2][:, cut:], p_in[3]], axis=1))

    def late_weights(after):
        full2 = _split_wait(ag2[0], ag2[1], ag2[2], ag2[3], _ag_plan(len(rest)), after, name="ag2_wait")
        full2 = _ag_pass(full2, name="ag2_pass")
        full = {n: fu.reshape((N_CHIP,) + _BIG_SHARD[n]) for n, fu in zip(rest, full2)}
        return dict(w_pa=_join_cols(full["w_proj_a"]), w_pb=_join_cols(full["w_proj_b"]), w_o=full["w_o"].reshape(d, d),
                    w_ffn_in=_join_cols(full["w_ffn_in"]), w_ffn_out=full["w_ffn_out"].reshape(D_FF, d))

    def sib_begin(group, grads, tag):
        gps = [grads[n].reshape((N_CHIP,) + halves[n]) for n in group]
        lands = [lax.empty((N_CHIP,) + halves[n][1:], F32) for n in group]
        return _split_start(gps, lands, _sib_plan(len(group)), N_CHIP * len(group), gps[0], name=f"rs_sib_{tag}_start")

    def rs_begin(group, sib, after, tag):
        gps, ras = _split_wait(sib[0], sib[1], sib[2], sib[3], _sib_plan(len(group)), after,
                               name=f"rs_sib_{tag}_wait", with_sources=True)
        sums = [_rs_add(gp, ra, c_arr, name=f"rs_add_{n}") for n, gp, ra in zip(group, gps, ras)]
        lands = [lax.empty((3,) + halves[n][1:], BF16) for n in group]
        st = _split_start([sb for _, sb in sums], lands, _rs_plan(len(group)), 3 * len(group), sums[0][0],
                          name=f"rs_{tag}_start")
        return sums, st

    def rs_end(group, begun, after, tag):
        sums, st = begun
        rbs = _split_wait(st[0], st[1], st[2], st[3], _rs_plan(len(group)), after, name=f"rs_{tag}_wait")
        return [_rs_final(sf, rb, chip_arr, name=f"rs_final_{n}") for n, (sf, _), rb in zip(group, sums, rbs)]

    begun = {}

    def early_grads(grads):
        begun["sib_rest"] = sib_begin(rest, grads, "rest")
        return begun["sib_rest"][4]

    def mid_grads(after):
        begun["rest"] = rs_begin(rest, begun["sib_rest"], after, "rest")
        return begun["rest"][1][4]

    loss_v, grad_x, grads, dmod, small = _device_step(
        x[0], loss_target[0], mod, first_weights, late_weights, early_grads, mid_grads, g_norm1, g_norm2, b_gate, g_qa,
        g_ka, g_qb, g_kb, rpb_after)
    sib_first = sib_begin(first, grads, "first")

    g, delta, new_m, new_v = {}, {}, {}, {}

    def finish(group, ts, tag):
        others = _sibling_swap(ts, name=f"rs_pair_{tag}")
        for n, t, o in zip(group, ts, others):
            gg, dl, nm, nv = _adamw_halves(w[n][0], t, o, m[n][0], v[n][0], c_arr, name=f"adamw_{n}")
            g[n], delta[n], new_m[n], new_v[n] = gg[None], dl[None], nm[None], nv[None]

    finish(rest, rs_end(rest, begun["rest"], sib_first[4], "rest"), "rest")
    done_rest = sum(new_v[n][0, :1, :1] for n in rest)
    begun["first"] = rs_begin(first, sib_first, done_rest, "first")

    stats = _pack_small(dict(b_ada=dmod, **small)) + begun["first"][1][4][0, 0]
    stats = stats.at[:, STATS_W - 1].set(loss_v[0, 0])
    rows = _small_allgather(stats.reshape(8, STATS_W // 8), name="ag_stats").reshape(N_DEV, STATS_W)
    dmod_sh = lax.dynamic_slice(rows, (0, chip * ada_cols), (8, ada_cols))
    g_ada = _ada_bwd(c_all.T, dmod_sh, name="ada_bwd")
    tot = _row_sum(rows, name="stats_sum")
    g_small = _unpack_small(tot, {n: w[n].shape for n in _SMALL})

    finish(first, rs_end(first, begun["first"], tot, "first"), "first")

    dl, nm, nv = _adamw(w_ada[0], g_ada, m_w_ada[0], v_w_ada[0], name="adamw_w_ada")
    g["w_ada"], delta["w_ada"], new_m["w_ada"], new_v["w_ada"] = g_ada[None], dl[None], nm[None], nv[None]
    shapes = {n: w[n].shape for n in _SMALL}
    dl, nm, nv = _adamw(_pack_small({n: w[n] for n in _SMALL}), tot, _pack_small({n: m[n] for n in _SMALL}),
                        _pack_small({n: v[n] for n in _SMALL}), name="adamw_small")
    delta.update(_unpack_small(dl, shapes))
    new_m.update(_unpack_small(nm, shapes))
    new_v.update(_unpack_small(nv, shapes))
    g.update(g_small)

    loss = tot[0, STATS_W - 1]
    return (loss, grad_x[None], *[g[n] for n in names], *[delta[n] for n in names], *[new_m[n] for n in names],
            *[new_v[n] for n in names])
```

```python
import numpy as np

import jax
import jax.numpy as jnp
from jax import lax
from jax.experimental import pallas as pl
from jax.experimental.pallas import tpu as pltpu

F32 = jnp.float32
BF16 = jnp.bfloat16

D_MODEL = 1024
SEQ = 8192
HEAD_DIM = 64
GRID_W = 64
ROWS = SEQ // GRID_W
NA_HEADS = 8
NA_KH = 8
NA_KW = 16
DIL_CONFIGS = ((128, 1), (512, 4), (2048, 16))
ROT_DIM = 16
ROPE_THETA = 500000.0
D_FF = 2816
EPS = 1e-6
NEG = -1e30
WA = 512
WB = 768
WB_OUT = 256
W_QKV = 3 * WA + 3 * WB
W_QK = 2 * WA + 2 * WB
W_GATES = 2 * D_MODEL
SCALE = HEAD_DIM ** -0.5

ADAM_LR = 0.001
ADAM_B1 = 0.9
ADAM_B2 = 0.999
ADAM_EPS = 1e-08
ADAM_WD = 0.01
ADAM_STEP = 10

LANES = 128
ROW_TILE = 256
ROW_TILES = {"ffn_fwd": 512, "post_attn_fwd": 512, "post_attn_bwd": 512,
             "ffn_in_bwd": 512}
Q_BLOCK = 256
NA_QROWS = Q_BLOCK // GRID_W
NA_KROWS = NA_QROWS + NA_KH - 1
NA_NK = NA_KROWS * GRID_W
NA_PAIRS = (NA_KROWS + 1) // 2
NA_W = NA_PAIRS * LANES
NA_RO_NONE = 15
NA_SLOTS = 21
RP_LANE0 = GRID_W - NA_KW
DIL_HALF = 64
DIL_QB = 512
N_QBLK = SEQ // Q_BLOCK

N_DEV = 8
N_CHIP = 4
FF_CHIP = 2 * D_FF // N_CHIP
STATS_W = 14336


def _pcall(body, *, name, **kw):
    return pl.pallas_call(body, name=name, **kw)


_NT = (((1,), (1,)), ((), ()))
_TN = (((0,), (0,)), ((), ()))
_ARB = pltpu.CompilerParams(dimension_semantics=("arbitrary",))
_PAR = pltpu.CompilerParams(dimension_semantics=("parallel",))


def _dot(a, b):
    return jnp.dot(a, b, preferred_element_type=F32)


def _dot_nt(a, b):
    return lax.dot_general(a, b, _NT, preferred_element_type=F32)


def _wgrad(a, b, *, name, tm, tn, tk=1024, chips=None):
    s, ma = a.shape
    nb = b.shape[1]
    nk = s // tk
    nc = nb // chips if chips else tn
    cpb = tn // nc

    def body(a_ref, b_ref, o_ref, acc):
        k = pl.program_id(2)
        r = lax.dot_general(a_ref[...].astype(BF16), b_ref[...].astype(BF16), _TN, preferred_element_type=F32)

        @pl.when(k == 0)
        def _():
            acc[...] = r

        @pl.when(k > 0)
        def _():
            acc[...] += r

        @pl.when(k == nk - 1)
        def _():
            if chips:
                for q in range(cpb):
                    o_ref[q] = acc[:, q * nc:(q + 1) * nc]
            else:
                o_ref[...] = acc[...]

    if chips:
        o_spec = pl.BlockSpec((cpb, tm, nc), lambda i, j, k: (j, i, 0))
        out_shape = jax.ShapeDtypeStruct((chips, ma, nc), F32)
    else:
        o_spec = pl.BlockSpec((tm, tn), lambda i, j, k: (i, j))
        out_shape = jax.ShapeDtypeStruct((ma, nb), F32)
    return _pcall(
        body, name=name, grid=(ma // tm, nb // tn, nk),
        in_specs=[pl.BlockSpec((tk, tm), lambda i, j, k: (k, i)), pl.BlockSpec((tk, tn), lambda i, j, k: (k, j))],
        out_specs=o_spec, out_shape=out_shape, scratch_shapes=[pltpu.VMEM((tm, tn), F32)],
        compiler_params=pltpu.CompilerParams(dimension_semantics=("parallel", "parallel", "arbitrary")),
    )(a, b)


def _row_call(body, *, name, row_ins, res_ins, row_outs, acc_outs=(), scratch=()):
    row_ins = [a if isinstance(a, tuple) else (a, 1) for a in row_ins]
    row_outs = [o if len(o) == 3 else (*o, 1) for o in row_outs]
    s = row_ins[0][0].shape[0]
    tile = ROW_TILES.get(name, ROW_TILE)
    n = s // tile
    nri, nre, nro, nao = len(row_ins), len(res_ins), len(row_outs), len(acc_outs)

    def whole(shape):
        nd = len(shape)
        return pl.BlockSpec(tuple(shape), lambda i: (0,) * nd, pipeline_mode=pl.Buffered(1))

    def whole_out(shape):
        nd = len(shape)
        return pl.BlockSpec(tuple(shape), lambda i: (0,) * nd)

    def rows(w, d):
        if d == 1:
            return pl.BlockSpec((tile, w), lambda i: (i, 0))
        return pl.BlockSpec((d, tile // d, w), lambda i: (0, i, 0))

    in_specs = [rows(a.shape[1], d) for a, d in row_ins]
    in_specs += [whole(a.shape) for a in res_ins]
    out_specs = [rows(w, d) for w, _, d in row_outs]
    out_specs += [whole_out(shp) for shp, _ in acc_outs]
    out_shape = [jax.ShapeDtypeStruct((s, w) if d == 1 else (d, s // d, w), dt) for w, dt, d in row_outs]
    out_shape += [jax.ShapeDtypeStruct(tuple(shp), dt) for shp, dt in acc_outs]

    def wrapped(*refs):
        at = [0, nri, nri + nre, nri + nre + nro, nri + nre + nro + nao]
        body(pl.program_id(0), n, refs[at[0]:at[1]], refs[at[1]:at[2]], refs[at[2]:at[3]], refs[at[3]:at[4]],
             refs[at[4]:])

    args = [a if d == 1 else a.reshape(d, s // d, a.shape[1]) for a, d in row_ins]
    outs = _pcall(wrapped, name=name, grid=(n,), in_specs=in_specs, out_specs=out_specs, out_shape=out_shape,
                  scratch_shapes=list(scratch), compiler_params=_ARB)(*args, *res_ins)
    return [o.reshape(s, o.shape[-1]) if k < nro and row_outs[k][2] != 1 else o for k, o in enumerate(outs)]


def _stage_shape(name):
    return pltpu.VMEM((4, ROW_TILES.get(name, ROW_TILE), LANES), F32)


def _from_residue(ref, col, stage, slot):
    d, n = ref.shape[0], ref.shape[1]
    for r in range(d):
        stage.at[slot][pl.ds(r, n, stride=d), :] = ref[r, :, col:col + LANES].astype(F32)
    return stage[slot]


def _natural(ref, stage, slot0):
    if len(ref.shape) == 2:
        return ref[...]
    return jnp.concatenate([_from_residue(ref, c * LANES, stage, (slot0 + c) % 4)
                            for c in range(ref.shape[2] // LANES)], axis=1)


def _to_residue(val, ref, col, stage, slot):
    d, n = ref.shape[0], ref.shape[1]
    stage[slot] = val
    for r in range(d):
        ref[r, :, col:col + LANES] = stage.at[slot][pl.ds(r, n, stride=d), :].astype(ref.dtype)


def _fold8(t):
    r, w = t.shape
    return jnp.sum(t.reshape(r // 8, 8, w), axis=0)


def _sigmoid(t):
    return 0.5 * (jnp.tanh(0.5 * t) + 1.0)


def _head_lanes():
    return lax.broadcasted_iota(jnp.int32, (1, LANES), 1) < HEAD_DIM


def _head_mean(t, lo):
    s_lo = jnp.sum(jnp.where(lo, t, 0.0), axis=1, keepdims=True)
    s_hi = jnp.sum(jnp.where(lo, 0.0, t), axis=1, keepdims=True)
    return jnp.where(lo, s_lo, s_hi) * (1.0 / HEAD_DIM)


def _rms_mod(xv, g, sc, sh):
    rstd = lax.rsqrt(jnp.mean(xv * xv, axis=1, keepdims=True) + EPS)
    return (xv * rstd * g) * (1.0 + sc) + sh


def _rms_mod_bwd(xv, dh, g, sc):
    rstd = lax.rsqrt(jnp.mean(xv * xv, axis=1, keepdims=True) + EPS)
    xhat = xv * rstd
    dn = dh * (1.0 + sc)
    dxhat = dn * g
    dx = rstd * (dxhat - xhat * jnp.mean(dxhat * xhat, axis=1, keepdims=True))
    return dx, dh, dh * (xhat * g), dn * xhat


def _mix_weights(ls):
    m = jnp.maximum(jnp.maximum(ls[0], ls[1]), ls[2])
    es = [jnp.exp(t - m) for t in ls]
    den = es[0] + es[1] + es[2]
    return [e / den for e in es]


def _rope_tables():
    half = ROT_DIM // 2
    inv_freq = ROPE_THETA ** (-(jnp.arange(half, dtype=F32) * 2.0) / ROT_DIM)
    lane = np.arange(LANES) % HEAD_DIM
    ang = jnp.arange(SEQ).astype(F32)[:, None] * jnp.tile(inv_freq, LANES // half)[None, :]
    cos, sin = jnp.cos(ang), jnp.sin(ang)
    first, second = jnp.asarray(lane < half)[None, :], jnp.asarray((lane >= half) & (lane < ROT_DIM))[None, :]
    cos_t = jnp.where(first | second, cos, 1.0)
    return cos_t, jnp.where(second, sin, 0.0), jnp.where(first, -sin, 0.0)


_SECTIONS = ((0, WA, 0, False), (WA, 2 * WA, 1, False), (2 * WA, 3 * WA, -1, False),
             (3 * WA, 3 * WA + WB, 2, True), (3 * WA + WB, 3 * WA + 2 * WB, 3, True), (3 * WA + 2 * WB, W_QKV, -1, False))


def _pre_attn_fwd(x, cos_t, sa_t, sb_t, g1, sc1, sh1, w_qkv, w_gates, gains, *, name):
    half = ROT_DIM // 2
    dilated = [(g, dd) for g, (_, dd) in enumerate(DIL_CONFIGS) if dd > 1]

    def body(i, n, rin, res, rout, aout, scr):
        x_ref, cos_ref, sa_ref, sb_ref = rin
        g_ref, sc_ref, sh_ref, wq_ref, wg_ref, gains_ref = res
        h1_ref, qkvn_ref, pre_ref, gates_ref = rout[:4]
        group_ref = {g: rout[4 + k] for k, (g, _) in enumerate(dilated)}
        (stage,) = scr
        staged = 0
        hb = _rms_mod(x_ref[...], g_ref[...], sc_ref[...], sh_ref[...]).astype(BF16)
        h1_ref[...] = hb
        gates_ref[...] = _dot(hb, wg_ref[...]).astype(BF16)
        lo = _head_lanes()
        cosv, sav, sbv = cos_ref[...], sa_ref[...], sb_ref[...]
        pre_at = 0
        for si, (c0, c1, kind, rot) in enumerate(_SECTIONS):
            sec = _dot(hb, wq_ref[:, c0:c1])
            for ch in range((c1 - c0) // LANES):
                t = sec[:, ch * LANES:(ch + 1) * LANES]
                if kind >= 0:
                    pre_ref[:, pre_at:pre_at + LANES] = t.astype(BF16)
                    pre_at += LANES
                    t = t * lax.rsqrt(_head_mean(t * t, lo) + EPS) * gains_ref[kind:kind + 1, :]
                    if rot:
                        t = t * cosv + pltpu.roll(t, half, 1) * sav + pltpu.roll(t, LANES - half, 1) * sbv
                qkvn_ref[:, c0 + ch * LANES:c0 + (ch + 1) * LANES] = t.astype(BF16)
                group = ch * LANES // WB_OUT if si >= 3 else 0
                if group in group_ref:
                    col = (si - 3) * WB_OUT + ch * LANES % WB_OUT
                    _to_residue(t, group_ref[group], col, stage, staged % 4)
                    staged += 1

    return _row_call(body, name=name, row_ins=[x, cos_t, sa_t, sb_t], res_ins=[g1, sc1, sh1, w_qkv, w_gates, gains],
                     row_outs=[(D_MODEL, BF16), (W_QKV, BF16), (W_QK, BF16), (W_GATES, BF16)]
                     + [(3 * WB_OUT, BF16, dd) for _, dd in dilated], scratch=[_stage_shape(name)])


def _pre_attn_bwd(qk_pre, d_parts, dgates, x, dx1, cos_t, sa_t, sb_t, w_qkv, w_gates, gains, g1, sc1, *, name):
    half = ROT_DIM // 2
    nparts = len(d_parts)
    where = []
    residue = [isinstance(part, tuple) for part in d_parts]
    for pi, part in enumerate(d_parts):
        width = (part[0] if residue[pi] else part).shape[1]
        where += [(pi, cj) for cj in range(width // LANES)]
    assert len(where) == W_QKV // LANES

    def body(i, n, rin, res, rout, aout, scr):
        pre_ref, d_refs = rin[0], rin[1:1 + nparts]
        dgates_ref, x_ref, dx1_ref, cos_ref, sa_ref, sb_ref = rin[1 + nparts:]
        wq_ref, wg_ref, gains_ref, g_ref, sc_ref = res
        dqkv_ref, gx_ref = rout
        dgains_ref, sums_ref = aout
        accg, accs, stage = scr
        staged = 0

        @pl.when(i == 0)
        def _():
            accg[...] = jnp.zeros_like(accg)
            accs[...] = jnp.zeros_like(accs)

        lo = _head_lanes()
        cosv, sav, sbv = cos_ref[...], sa_ref[...], sb_ref[...]
        dh = _dot_nt(dgates_ref[...], wg_ref[...])
        pre_at = 0
        for c0, c1, kind, rot in _SECTIONS:
            for ch in range((c1 - c0) // LANES):
                pi, cj = where[c0 // LANES + ch]
                if residue[pi]:
                    dt = _from_residue(d_refs[pi], cj * LANES, stage, staged % 4)
                    staged += 1
                else:
                    dt = d_refs[pi][:, cj * LANES:(cj + 1) * LANES]
                if kind >= 0:
                    if rot:
                        dt = dt * cosv + pltpu.roll(dt * sav, LANES - half, 1) + pltpu.roll(dt * sbv, half, 1)
                    t = pre_ref[:, pre_at:pre_at + LANES].astype(F32)
                    pre_at += LANES
                    rstd = lax.rsqrt(_head_mean(t * t, lo) + EPS)
                    xhat = t * rstd
                    accg[kind] += _fold8(dt * xhat)
                    dxhat = dt * gains_ref[kind:kind + 1, :]
                    dt = rstd * (dxhat - xhat * _head_mean(dxhat * xhat, lo))
                dqkv_ref[:, c0 + ch * LANES:c0 + (ch + 1) * LANES] = dt.astype(BF16)
            dh = dh + _dot_nt(dqkv_ref[:, c0:c1], wq_ref[:, c0:c1])
        dx, t_sh, t_sc, t_g = _rms_mod_bwd(x_ref[...], dh, g_ref[...], sc_ref[...])
        gx_ref[...] = dx1_ref[...] + dx
        accs[0] += _fold8(t_sh)
        accs[1] += _fold8(t_sc)
        accs[2] += _fold8(t_g)

        @pl.when(i == n - 1)
        def _():
            t = jnp.sum(accg[...], axis=1)
            dgains_ref[...] = t + pltpu.roll(t, HEAD_DIM, 1)
            sums_ref[...] = jnp.sum(accs[...], axis=1)

    return _row_call(
        body, name=name, row_ins=[qk_pre, *d_parts, dgates, x, dx1, cos_t, sa_t, sb_t],
        res_ins=[w_qkv, w_gates, gains, g1, sc1], row_outs=[(W_QKV, BF16), (D_MODEL, F32)],
        acc_outs=[((4, LANES), F32), ((3, D_MODEL), F32)],
        scratch=[pltpu.VMEM((4, 8, LANES), F32), pltpu.VMEM((3, 8, D_MODEL), F32), _stage_shape(name)])


def _post_attn_fwd(o_a, o_g, l_g, gates, x, w_pa, w_pb, w_o, b_gate, gt1, g2, sc2, sh2, *, name):
    d = D_MODEL

    def body(i, n, rin, res, rout, aout, scr):
        oa_ref, o0, o1, o2, l0, l1, l2, gates_ref, x_ref = rin
        wpa_ref, wpb_ref, wo_ref, b_ref, gt_ref, g_ref, sc_ref, sh_ref = res
        ob_ref, merged_ref, mo_ref, x1_ref, h2_ref = rout
        (stage,) = scr
        ogs = [_natural(r, stage, 0) for r in (o0, o1, o2)]
        ws = _mix_weights([_natural(r, stage, 2) for r in (l0, l1, l2)])
        obb = (ws[0] * ogs[0] + ws[1] * ogs[1] + ws[2] * ogs[2]).astype(BF16)
        ob_ref[...] = obb
        pa = _dot(oa_ref[...].astype(BF16), wpa_ref[...])
        pb = _dot(obb, wpb_ref[...])
        ga = _sigmoid(gates_ref[:, :d].astype(F32) + b_ref[:, :d])
        gb = _sigmoid(gates_ref[:, d:].astype(F32) + b_ref[:, d:])
        merged = (ga * pa + gb * pb).astype(BF16)
        merged_ref[...] = merged
        mo = _dot(merged, wo_ref[...])
        mo_ref[...] = mo.astype(BF16)
        x1 = x_ref[...] + gt_ref[...] * mo
        x1_ref[...] = x1
        h2_ref[...] = _rms_mod(x1, g_ref[...], sc_ref[...], sh_ref[...]).astype(BF16)

    return _row_call(body, name=name, row_ins=[o_a, *o_g, *l_g, gates, x],
                     res_ins=[w_pa, w_pb, w_o, b_gate, gt1, g2, sc2, sh2],
                     row_outs=[(WB_OUT, BF16), (d, BF16), (d, BF16), (d, F32), (d, BF16)], scratch=[_stage_shape(name)])


def _ffn_fwd(h2, w_ffn_in, *, name):
    def body(i, n, rin, res, rout, aout, scr):
        (h_ref,), (w_ref,), (act_ref, ff_ref) = rin, res, rout
        hv = h_ref[...]
        for q in range(2):
            a = _dot(hv, w_ref[:, q * FF_CHIP:(q + 1) * FF_CHIP])
            up = _dot(hv, w_ref[:, D_FF + q * FF_CHIP:D_FF + (q + 1) * FF_CHIP])
            sl = slice(q * FF_CHIP, (q + 1) * FF_CHIP)
            act_ref[:, sl] = (a * _sigmoid(a) * up).astype(BF16)
            ff_ref[:, sl] = a.astype(BF16)
            ff_ref[:, D_FF + q * FF_CHIP:D_FF + (q + 1) * FF_CHIP] = up.astype(BF16)

    return _row_call(body, name=name, row_ins=[h2], res_ins=[w_ffn_in], row_outs=[(D_FF, BF16), (2 * D_FF, BF16)])


def _ffn_mid(act, ff, x1, tgt, w_ffn_out, gt2, *, name):
    d = D_MODEL

    def body(i, n, rin, res, rout, aout, scr):
        act_ref, ff_ref, x1_ref, tgt_ref = rin
        wo_ref, gt_ref = res
        dy_ref, dffo_ref, dff_ref = rout
        dgt_ref, loss_ref = aout
        (acc,) = scr

        @pl.when(i == 0)
        def _():
            acc[...] = jnp.zeros_like(acc)

        ffo = _dot(act_ref[...], wo_ref[...])
        gtv = gt_ref[...]
        e = x1_ref[...] + gtv * ffo - tgt_ref[...]
        dy = e * (1.0 / d)
        dy_ref[...] = dy
        dffo = (gtv * dy).astype(BF16)
        dffo_ref[...] = dffo
        acc[0] += _fold8(dy * ffo)
        acc[1] += _fold8(e * e)
        for q in range(2):
            sl = slice(q * FF_CHIP, (q + 1) * FF_CHIP)
            su = slice(D_FF + q * FF_CHIP, D_FF + (q + 1) * FF_CHIP)
            dact = _dot_nt(dffo, wo_ref[sl, :])
            a = ff_ref[:, sl].astype(F32)
            up = ff_ref[:, su].astype(F32)
            sg = _sigmoid(a)
            dff_ref[:, sl] = (dact * up * (sg * (1.0 + a * (1.0 - sg)))).astype(BF16)
            dff_ref[:, su] = (dact * (a * sg)).astype(BF16)

        @pl.when(i == n - 1)
        def _():
            dgt_ref[...] = jnp.sum(acc[0], axis=0, keepdims=True)
            tot = jnp.sum(jnp.sum(acc[1], axis=0, keepdims=True), axis=1, keepdims=True)
            loss_ref[...] = jnp.broadcast_to(tot * (0.5 / d), (1, LANES))

    return _row_call(body, name=name, row_ins=[act, ff, x1, tgt], res_ins=[w_ffn_out, gt2],
                     row_outs=[(d, F32), (d, BF16), (2 * D_FF, BF16)], acc_outs=[((1, d), F32), ((1, LANES), F32)],
                     scratch=[pltpu.VMEM((2, 8, d), F32)])


def _ffn_in_bwd(dff, x1, dy, mo, w_ffn_in, g2, sc2, gt1, *, name):
    d = D_MODEL

    def body(i, n, rin, res, rout, aout, scr):
        dff_ref, x1_ref, dy_ref, mo_ref = rin
        w_ref, g_ref, sc_ref, gt_ref = res
        dx1_ref, dmo_ref = rout
        (sums_ref,) = aout
        (acc,) = scr

        @pl.when(i == 0)
        def _():
            acc[...] = jnp.zeros_like(acc)

        dh = _dot_nt(dff_ref[...], w_ref[...])
        dx, t_sh, t_sc, t_g = _rms_mod_bwd(x1_ref[...], dh, g_ref[...], sc_ref[...])
        dx1 = dy_ref[...] + dx
        dx1_ref[...] = dx1
        dmo_ref[...] = (gt_ref[...] * dx1).astype(BF16)
        acc[0] += _fold8(t_sh)
        acc[1] += _fold8(t_sc)
        acc[2] += _fold8(t_g)
        acc[3] += _fold8(dx1 * mo_ref[...].astype(F32))

        @pl.when(i == n - 1)
        def _():
            sums_ref[...] = jnp.sum(acc[...], axis=1)

    return _row_call(body, name=name, row_ins=[dff, x1, dy, mo], res_ins=[w_ffn_in, g2, sc2, gt1],
                     row_outs=[(d, F32), (d, BF16)], acc_outs=[((4, d), F32)], scratch=[pltpu.VMEM((4, 8, d), F32)])


def _post_attn_bwd(dmo, gates, o_a, o_g, l_g, w_pa, w_pb, w_o, b_gate, *, name):
    d = D_MODEL

    def body(i, n, rin, res, rout, aout, scr):
        dmo_ref, gates_ref, oa_ref, o0, o1, o2, l0, l1, l2 = rin
        wpa_ref, wpb_ref, wo_ref, b_ref = res
        dpa_ref, dpb_ref, dgates_ref, doa_ref = rout[:4]
        do_refs, dl_refs = rout[4:7], rout[7:10]
        (dbg_ref,) = aout
        acc, stage = scr

        @pl.when(i == 0)
        def _():
            acc[...] = jnp.zeros_like(acc)

        ogs = [_natural(r, stage, 0) for r in (o0, o1, o2)]
        ws = _mix_weights([_natural(r, stage, 2) for r in (l0, l1, l2)])
        obb = (ws[0] * ogs[0] + ws[1] * ogs[1] + ws[2] * ogs[2]).astype(BF16)
        pa = _dot(oa_ref[...].astype(BF16), wpa_ref[...])
        pb = _dot(obb, wpb_ref[...])
        ga = _sigmoid(gates_ref[:, :d].astype(F32) + b_ref[:, :d])
        gb = _sigmoid(gates_ref[:, d:].astype(F32) + b_ref[:, d:])
        dm = _dot_nt(dmo_ref[...], wo_ref[...])
        dpa = (dm * ga).astype(BF16)
        dpb = (dm * gb).astype(BF16)
        dpa_ref[...] = dpa
        dpb_ref[...] = dpb
        dga = dm * pa * ga * (1.0 - ga)
        dgb = dm * pb * gb * (1.0 - gb)
        dgates_ref[:, :d] = dga.astype(BF16)
        dgates_ref[:, d:] = dgb.astype(BF16)
        acc[:, :d] += _fold8(dga)
        acc[:, d:] += _fold8(dgb)
        doa_ref[...] = _dot_nt(dpa, wpa_ref[...])
        dob = _dot_nt(dpb, wpb_ref[...])
        lo = _head_lanes()
        for ch in range(WB_OUT // LANES):
            sl = slice(ch * LANES, (ch + 1) * LANES)
            dv = dob[:, sl]
            wc = [w[:, sl] for w in ws]
            ts = [_head_mean(dv * og[:, sl], lo) * float(HEAD_DIM) for og in ogs]
            tbar = wc[0] * ts[0] + wc[1] * ts[1] + wc[2] * ts[2]
            for g in range(3):
                for k, (ref, val) in enumerate(((do_refs[g], wc[g] * dv), (dl_refs[g], wc[g] * (ts[g] - tbar)))):
                    if len(ref.shape) == 2:
                        ref[:, sl] = val
                    else:
                        _to_residue(val, ref, ch * LANES, stage, (2 * g + k) % 4)

        @pl.when(i == n - 1)
        def _():
            dbg_ref[...] = jnp.sum(acc[...], axis=0, keepdims=True)

    return _row_call(body, name=name, row_ins=[dmo, gates, o_a, *o_g, *l_g], res_ins=[w_pa, w_pb, w_o, b_gate],
                     row_outs=[(d, BF16), (d, BF16), (2 * d, BF16), (WA, F32)]
                     + 2 * [(WB_OUT, F32, dd) for _, dd in DIL_CONFIGS],
                     acc_outs=[((1, 2 * d), F32)], scratch=[pltpu.VMEM((8, 2 * d), F32), _stage_shape(name)])


def _na_class_tables():
    ro = np.full((3, NA_QROWS, 2 * NA_PAIRS), NA_RO_NONE, np.int64)
    slot = np.zeros((3, NA_QROWS, NA_PAIRS), np.int64)
    for t in range(3):
        for a in range(NA_QROWS):
            qr = _NA_CLASS_R0[t] + a
            rs = min(max(qr - NA_KH // 2, 0), ROWS - NA_KH)
            for b in range(NA_KROWS):
                kr = _NA_CLASS_K0[t] + b
                if rs <= kr < rs + NA_KH:
                    ro[t, a, b] = kr - qr + (NA_KH - 1)
            for j in range(NA_PAIRS):
                slot[t, a, j] = 2 * j - a + (_NA_CLASS_K0[t] - _NA_CLASS_R0[t] + NA_KH - 1) + (NA_QROWS - 1)
    assert slot.min() >= 0 and slot.max() < NA_SLOTS
    return ro, slot


def _na_build_bias(i, cls_ref, rp_ref, cm_ref, bias_scr):
    ro, _ = _na_class_tables()
    lo = _head_lanes()
    first = jnp.logical_or(i == 0, cls_ref[i] != cls_ref[jnp.maximum(i - 1, 0)])
    for t in range(3):
        @pl.when(jnp.logical_and(first, cls_ref[i] == t))
        def _():
            for hh in range(2):
                for a in range(NA_QROWS):
                    for j in range(NA_PAIRS):
                        r0, r1 = int(ro[t, a, 2 * j]), int(ro[t, a, 2 * j + 1])
                        x0 = jnp.broadcast_to(rp_ref[hh, r0:r0 + 1, :], (GRID_W, LANES))
                        x1 = jnp.broadcast_to(rp_ref[hh, r1:r1 + 1, :], (GRID_W, LANES))
                        blk = jnp.where(lo, pltpu.roll(x0, GRID_W + 1, 1, stride=1, stride_axis=0),
                                        pltpu.roll(x1, 1, 1, stride=1, stride_axis=0))
                        bias_scr[hh, a * GRID_W:(a + 1) * GRID_W, j * LANES:(j + 1) * LANES] = blk + cm_ref[...]
    return first


def _attn_fwd(qkv, qc0, kc0, vc0, npairs, table, kstart, cls, nk, *, name, na=None, qb=Q_BLOCK):
    s = qkv.shape[0]

    def body(ks_ref, cls_ref, q_ref, k_ref, v_ref, b_ref, *rest):
        if na:
            cm_ref, o_ref, lse_ref, bias_scr = rest
        else:
            o_ref, lse_ref = rest
        i = pl.program_id(1)
        if na:
            _na_build_bias(i, cls_ref, b_ref, cm_ref, bias_scr)
        ks = pl.multiple_of(ks_ref[i], 64)
        q2 = q_ref[...]
        k2 = k_ref[pl.ds(ks, nk), :]
        v2 = v_ref[pl.ds(ks, nk), :]
        lo = _head_lanes()
        outs, lses = [], []
        for h in range(2):
            qm = jnp.where(lo if h == 0 else jnp.logical_not(lo), q2, jnp.zeros_like(q2))
            sc = _dot_nt(qm, k2) * SCALE + (bias_scr[h, :, :nk] if na else b_ref[0, 0])
            m = jnp.max(sc, axis=1, keepdims=True)
            p = jnp.exp(sc - m)
            l = jnp.sum(p, axis=1, keepdims=True)
            pv = _dot(p.astype(BF16), v2)
            outs.append(pv / l)
            lses.append(m + jnp.log(l))
        o_ref[...] = jnp.where(lo, outs[0], outs[1])
        lse_ref[...] = jnp.where(lo, lses[0], lses[1])

    w = npairs * LANES
    in_specs = [
        pl.BlockSpec((qb, LANES), lambda p, i, ks, cl: (i, qc0 + p)),
        pl.BlockSpec((s, LANES), lambda p, i, ks, cl: (0, kc0 + p)),
        pl.BlockSpec((s, LANES), lambda p, i, ks, cl: (0, vc0 + p)),
    ]
    if na:
        in_specs += _na_bias_specs()
        args, scratch = (kstart, cls, qkv, qkv, qkv, *na), [pltpu.VMEM((2, Q_BLOCK, NA_W), F32)]
    else:
        in_specs.append(pl.BlockSpec((1, 1, qb, nk), lambda p, i, ks, cl: (cl[i], 0, 0, 0)))
        args, scratch = (kstart, cls, qkv, qkv, qkv, table), []
    grid_spec = pltpu.PrefetchScalarGridSpec(
        num_scalar_prefetch=2, grid=(npairs, s // qb), in_specs=in_specs,
        out_specs=[pl.BlockSpec((qb, LANES), lambda p, i, ks, cl: (i, p)),
                   pl.BlockSpec((qb, LANES), lambda p, i, ks, cl: (i, p))],
        scratch_shapes=scratch,
    )
    return _pcall(body, name=name, grid_spec=grid_spec,
                  out_shape=[jax.ShapeDtypeStruct((s, w), F32), jax.ShapeDtypeStruct((s, w), F32)],
                  compiler_params=pltpu.CompilerParams(dimension_semantics=("parallel", "arbitrary")),
                  )(*args)


def _na_bias_specs():
    return [pl.BlockSpec((2, 16, LANES), lambda p, i, ks, cl: (p, 0, 0)),
            pl.BlockSpec((GRID_W, LANES), lambda p, i, ks, cl: (0, 0))]


def _attn_bwd(qkv, qc0, kc0, vc0, npairs, table, kstart, cls, nk, do, o, lse, *, name, dlse=None, na=None,
              qb=Q_BLOCK, order=None):
    s = qkv.shape[0]
    has_dlse = dlse is not None
    _, slot = _na_class_tables()

    def body(ks_ref, cls_ref, q_ref, k_ref, v_ref, b_ref, *rest):
        if na:
            cm_ref, rest = rest[0], rest[1:]
        do_ref, o_ref, lse_ref, rest = rest[0], rest[1], rest[2], rest[3:]
        if has_dlse:
            dlse_ref, rest = rest[0], rest[1:]
        if order is not None:
            rest = rest[1:]
        dq_ref, dk_ref, dv_ref = rest[0], rest[1], rest[2]
        if na:
            bank_ref, bias_scr, dbias_scr, bank_scr = rest[3:]
        i = pl.program_id(1)
        if na:
            first = _na_build_bias(i, cls_ref, b_ref, cm_ref, bias_scr)

        @pl.when(i == 0)
        def _():
            dk_ref[...] = jnp.zeros_like(dk_ref)
            dv_ref[...] = jnp.zeros_like(dv_ref)
            if na:
                dbias_scr[...] = jnp.zeros_like(dbias_scr)
                bank_scr[...] = jnp.zeros_like(bank_scr)

        ks = pl.multiple_of(ks_ref[i], 64)
        q2 = q_ref[...]
        k2 = k_ref[pl.ds(ks, nk), :]
        v2 = v_ref[pl.ds(ks, nk), :]
        do2 = do_ref[...]
        lse2 = lse_ref[...]
        doo = do2 * o_ref[...]
        do2b = do2.astype(BF16)
        lo = _head_lanes()
        lane = lax.broadcasted_iota(jnp.int32, (1, LANES), 1)
        dqs, dks, dvs = [], [], []
        for h in range(2):
            mh = lo if h == 0 else jnp.logical_not(lo)
            qm = jnp.where(mh, q2, jnp.zeros_like(q2))
            sc = _dot_nt(qm, k2) * SCALE + (bias_scr[h, :, :nk] if na else b_ref[0, 0])
            lse_h = jnp.max(jnp.where(mh, lse2, NEG), axis=1, keepdims=True)
            p = jnp.exp(sc - lse_h)
            delta = jnp.sum(jnp.where(mh, doo, 0.0), axis=1, keepdims=True)
            dom = jnp.where(mh, do2b, jnp.zeros_like(do2b))
            dp = _dot_nt(dom, v2)
            t = dp - delta
            if has_dlse:
                t = t + jnp.sum(jnp.where(lane == h * HEAD_DIM, dlse_ref[...], 0.0), axis=1, keepdims=True)
            ds = p * t
            if na:
                @pl.when(first)
                def _():
                    dbias_scr[h, :, :nk] = ds

                @pl.when(jnp.logical_not(first))
                def _():
                    dbias_scr[h, :, :nk] += ds
            dsb = ds.astype(BF16)
            dqs.append(_dot(dsb, k2))
            dks.append(lax.dot_general(dsb, q2, _TN, preferred_element_type=F32))
            dvs.append(lax.dot_general(p.astype(BF16), do2b, _TN, preferred_element_type=F32))
        dq_ref[...] = jnp.where(lo, dqs[0], dqs[1]) * SCALE
        dk_ref[pl.ds(ks, nk), :] += jnp.where(lo, dks[0], dks[1]) * SCALE
        dv_ref[pl.ds(ks, nk), :] += jnp.where(lo, dvs[0], dvs[1])
        if na:
            last = jnp.logical_or(i == N_QBLK - 1, cls_ref[i] != cls_ref[jnp.minimum(i + 1, N_QBLK - 1)])
            for t in range(3):
                @pl.when(jnp.logical_and(last, cls_ref[i] == t))
                def _():
                    for hh in range(2):
                        for a in range(NA_QROWS):
                            for j in range(NA_PAIRS):
                                bank_scr[hh, int(slot[t, a, j])] += dbias_scr[
                                    hh, a * GRID_W:(a + 1) * GRID_W, j * LANES:(j + 1) * LANES]

            @pl.when(i == N_QBLK - 1)
            def _():
                bank_ref[...] = bank_scr[...]

    w = npairs * LANES
    blk = lambda: pl.BlockSpec((qb, LANES), lambda p, i, ks, cl: (i, p))
    full = lambda: pl.BlockSpec((s, LANES), lambda p, i, ks, cl: (0, p))
    in_specs = [
        pl.BlockSpec((qb, LANES), lambda p, i, ks, cl: (i, qc0 + p)),
        pl.BlockSpec((s, LANES), lambda p, i, ks, cl: (0, kc0 + p)),
        pl.BlockSpec((s, LANES), lambda p, i, ks, cl: (0, vc0 + p)),
    ]
    if na:
        in_specs += _na_bias_specs()
        args = [kstart, cls, qkv, qkv, qkv, *na]
    else:
        in_specs.append(pl.BlockSpec((1, 1, qb, nk), lambda p, i, ks, cl: (cl[i], 0, 0, 0)))
        args = [kstart, cls, qkv, qkv, qkv, table]
    in_specs += [blk(), blk(), blk()]
    args += [do, o, lse]
    if has_dlse:
        in_specs.append(blk())
        args.append(dlse)
    out_specs = [blk(), full(), full()]
    out_shape = [jax.ShapeDtypeStruct((s, w), F32)] * 3
    scratch = []
    if na:
        bank_shape = (2, NA_SLOTS, GRID_W, LANES)
        out_specs.append(pl.BlockSpec(bank_shape, lambda p, i, ks, cl: (p, 0, 0, 0)))
        out_shape.append(jax.ShapeDtypeStruct((2 * npairs,) + bank_shape[1:], F32))
        scratch = [pltpu.VMEM((2, Q_BLOCK, NA_W), F32), pltpu.VMEM((2, Q_BLOCK, NA_W), F32), pltpu.VMEM(bank_shape, F32)]
    if order is not None:
        in_specs.append(pl.BlockSpec(order.shape, lambda p, i, ks, cl: (0, 0)))
        args.append(order)
    grid_spec = pltpu.PrefetchScalarGridSpec(num_scalar_prefetch=2, grid=(npairs, s // qb), in_specs=in_specs,
                                             out_specs=out_specs, scratch_shapes=scratch)
    return _pcall(body, name=name, grid_spec=grid_spec, out_shape=out_shape,
                  compiler_params=pltpu.CompilerParams(dimension_semantics=("arbitrary", "arbitrary")))(*args)


_NA_CLASS_R0 = (0, NA_QROWS, ROWS - NA_QROWS)
_NA_CLASS_K0 = (0, 0, ROWS - NA_KROWS)
_RPB_RO = 2 * NA_KH - 1
_RPB_CO = 2 * NA_KW - 1
_BANK_ROWS = 48


def _na_constants():
    col = np.arange(GRID_W)
    cs = np.clip(col - NA_KW // 2, 0, GRID_W - NA_KW)
    vcol = (col[None, :] >= cs[:, None]) & (col[None, :] < cs[:, None] + NA_KW)
    colmask = np.where(np.concatenate([vcol, vcol], axis=1), 0.0, NEG).astype(np.float32)
    co = col[None, :] - col[:, None] + (NA_KW - 1)
    oh_col = np.zeros((GRID_W * GRID_W, LANES), np.float32)
    for qc in range(GRID_W):
        for kc in range(GRID_W):
            if vcol[qc, kc]:
                oh_col[qc * GRID_W + kc, co[qc, kc]] = 1.0
    ks = np.clip(np.arange(N_QBLK) * NA_QROWS - NA_KH // 2, 0, ROWS - NA_KROWS) * GRID_W
    cls = np.ones(N_QBLK, np.int32)
    cls[0], cls[-1] = 0, 2
    return colmask, oh_col, ks.astype(np.int32), cls


def _bank_reduce(bank, oh_col, *, name):
    def body(d_ref, ohc_ref, o_ref):
        o_ref[0] = jnp.dot(d_ref[0], ohc_ref[...], preferred_element_type=F32, precision=lax.Precision.HIGHEST)

    return _pcall(
        body, name=name, grid=(NA_HEADS,),
        in_specs=[pl.BlockSpec((1, _BANK_ROWS, GRID_W * GRID_W), lambda h: (h, 0, 0)),
                  pl.BlockSpec((GRID_W * GRID_W, LANES), lambda h: (0, 0))],
        out_specs=pl.BlockSpec((1, _BANK_ROWS, LANES), lambda h: (h, 0, 0)),
        out_shape=jax.ShapeDtypeStruct((NA_HEADS, _BANK_ROWS, LANES), F32), compiler_params=_PAR,
    )(bank, oh_col)


def _dil_constants(dilation):
    seg = SEQ // dilation
    nb = seg // DIL_QB
    nk = min(DIL_QB + 2 * DIL_HALF, seg)
    starts = [min(max(blk * DIL_QB - DIL_HALF, 0), seg - nk) for blk in range(nb)]
    shifts = sorted({w0 - blk * DIL_QB for blk, w0 in enumerate(starts)}, reverse=True)
    qi = np.arange(DIL_QB)[:, None]
    ki = np.arange(nk)[None, :]
    mask = np.stack([np.where(np.abs(ki + sh - qi) <= DIL_HALF, 0.0, NEG) for sh in shifts]).astype(np.float32)
    ks, cls = [], []
    for i in range(SEQ // DIL_QB):
        sub, blk = divmod(i, nb)
        cls.append(shifts.index(starts[blk] - blk * DIL_QB))
        ks.append(sub * seg + starts[blk])
    return mask.reshape(len(shifts), 1, DIL_QB, nk), np.asarray(ks, np.int32), np.asarray(cls, np.int32), nk


_VM = pl.BlockSpec(memory_space=pltpu.VMEM)


def _ada_fwd(c_all, w, b, *, name):
    def body(c_ref, w_ref, b_ref, o_ref):
        cv = c_ref[...]
        o_ref[...] = jnp.dot(cv * _sigmoid(cv), w_ref[...], preferred_element_type=F32,
                             precision=lax.Precision.HIGHEST) + b_ref[...]

    return _pcall(body, name=name, in_specs=[_VM, _VM, _VM], out_specs=_VM,
                  out_shape=jax.ShapeDtypeStruct((c_all.shape[0], w.shape[1]), F32))(c_all, w, b)


def _ada_bwd(c_all_t, dmod, *, name):
    def body(c_ref, d_ref, o_ref):
        cv = c_ref[...]
        o_ref[...] = jnp.dot(cv * _sigmoid(cv), d_ref[...], preferred_element_type=F32,
                             precision=lax.Precision.HIGHEST)

    return _pcall(body, name=name, in_specs=[_VM, _VM], out_specs=_VM,
                  out_shape=jax.ShapeDtypeStruct((c_all_t.shape[0], dmod.shape[1]), F32))(c_all_t, dmod)


def _row_sum(t, *, name):
    def body(t_ref, o_ref):
        o_ref[...] = jnp.sum(t_ref[...], axis=0, keepdims=True)

    return _pcall(body, name=name, in_specs=[_VM], out_specs=_VM,
                  out_shape=jax.ShapeDtypeStruct((1, t.shape[1]), F32))(t)


def _row_tile(rows):
    tr = rows
    for cand in range(8, 513, 8):
        if rows % cand == 0:
            tr = cand
    return tr


def _adamw_math(wv, gv, mv, vv):
    nm = ADAM_B1 * mv + (1.0 - ADAM_B1) * gv
    nv = ADAM_B2 * vv + (1.0 - ADAM_B2) * (gv * gv)
    m_hat = nm / (1.0 - ADAM_B1 ** ADAM_STEP)
    v_hat = nv / (1.0 - ADAM_B2 ** ADAM_STEP)
    return -ADAM_LR * (m_hat / (jnp.sqrt(v_hat) + ADAM_EPS) + ADAM_WD * wv), nm, nv


def _adamw(w, g, m, v, *, name):
    rows, cols = w.shape
    tr = _row_tile(rows)

    def body(w_ref, g_ref, m_ref, v_ref, d_ref, nm_ref, nv_ref):
        d_ref[...], nm_ref[...], nv_ref[...] = _adamw_math(w_ref[...], g_ref[...], m_ref[...], v_ref[...])

    spec = pl.BlockSpec((tr, cols), lambda i: (i, 0))
    return _pcall(body, name=name, grid=(rows // tr,), in_specs=[spec] * 4, out_specs=[spec] * 3,
                  out_shape=[jax.ShapeDtypeStruct((rows, cols), F32)] * 3, compiler_params=_PAR)(w, g, m, v)


def _adamw_halves(w, g_mine, g_other, m, v, c_arr, *, name):
    rows, cols = w.shape
    hr = rows // 2
    tr = _row_tile(hr)
    nt = hr // tr

    def body(c_ref, w_ref, t_ref, o_ref, m_ref, v_ref, g_ref, d_ref, nm_ref, nv_ref):
        gv = jnp.where(pl.program_id(0) == c_ref[0], t_ref[...], o_ref[...])
        g_ref[...] = gv
        d_ref[...], nm_ref[...], nv_ref[...] = _adamw_math(w_ref[...], gv, m_ref[...], v_ref[...])

    full = pl.BlockSpec((tr, cols), lambda h, i, c: (h * nt + i, 0))
    half = pl.BlockSpec((tr, cols), lambda h, i, c: (i, 0))
    grid_spec = pltpu.PrefetchScalarGridSpec(num_scalar_prefetch=1, grid=(2, nt),
                                             in_specs=[full, half, half, full, full], out_specs=[full] * 4)
    return _pcall(body, name=name, grid_spec=grid_spec, out_shape=[jax.ShapeDtypeStruct((rows, cols), F32)] * 4,
                  compiler_params=pltpu.CompilerParams(dimension_semantics=("parallel", "parallel")),
                  )(c_arr, w, g_mine, g_other, m, v)


_MESH = pl.DeviceIdType.MESH
_ANY = pl.BlockSpec(memory_space=pl.ANY)
_CHIP_FLIPS = ((1, 0), (0, 1), (1, 1))


def _pos():
    return lax.axis_index("x"), lax.axis_index("y"), lax.axis_index("c")


def _flip(v, f):
    return 1 - v if f else v


def _sem_pairs(n):
    return [pltpu.SemaphoreType.DMA((n,)), pltpu.SemaphoreType.DMA((n,))]


def _small_allgather(blk, *, name):
    m_per, n = blk.shape

    def body(x_ref, out_ref, send_sems, recv_sems, local_sem):
        x, y, c = _pos()
        me, sibling = (x, y, c), (x, y, 1 - c)
        chips = [(_flip(x, fx), _flip(y, fy)) for fx, fy in _CHIP_FLIPS]

        def rows(px, py, pc):
            return out_ref.at[pl.ds((4 * px + 2 * py + pc) * m_per, m_per), :]

        def copy(k, block, to, src=None):
            return pltpu.make_async_remote_copy(
                src_ref=rows(*block) if src is None else src, dst_ref=rows(*block),
                send_sem=send_sems.at[k], recv_sem=recv_sems.at[k], device_id=to, device_id_type=_MESH)

        mine = pltpu.make_async_copy(x_ref, rows(*me), local_sem)
        mine.start()
        first = [copy(0, me, sibling, src=x_ref)]
        first += [copy(1 + j, me, (*chip, c), src=x_ref) for j, chip in enumerate(chips)]
        for cp in first:
            cp.start()
        passed = [copy(4 + j, (*chip, c), sibling) for j, chip in enumerate(chips)]
        for j, chip in enumerate(chips):
            copy(1 + j, (*chip, c), me).wait_recv()
            passed[j].start()
        copy(0, sibling, me).wait_recv()
        for j, chip in enumerate(chips):
            copy(4 + j, (*chip, 1 - c), me).wait_recv()
        for cp in first + passed:
            cp.wait_send()
        mine.wait()

    return _pcall(
        body, name=name, out_shape=jax.ShapeDtypeStruct((N_DEV * m_per, n), blk.dtype),
        in_specs=[_VM], out_specs=_VM,
        scratch_shapes=_sem_pairs(7) + [pltpu.SemaphoreType.DMA],
    )(blk)


_HBM = pl.BlockSpec(memory_space=pltpu.HBM)
_SEM = pl.BlockSpec(memory_space=pltpu.SEMAPHORE)
_EFFECT = pltpu.SideEffectType.DATAFLOW_SIDE_EFFECTING


def _split_start(srcs, lands, plan, ncopies, after, *, name):
    ns, nl = len(srcs), len(lands)

    def body(*refs):
        src_refs, land_refs = refs[:ns], refs[ns:ns + nl]
        send_sems, recv_sems = refs[ns + nl + 1], refs[ns + nl + 2]
        token = refs[-1]
        x, y, c = _pos()
        for k, (src, dst, to, _) in enumerate(plan(x, y, c, src_refs, land_refs)):
            pltpu.make_async_remote_copy(src_ref=src, dst_ref=dst, send_sem=send_sems.at[k], recv_sem=recv_sems.at[k],
                                         device_id=to, device_id_type=_MESH).start()
        token[...] = jnp.zeros_like(token)

    hbm = lambda a: pltpu.HBM(a.shape, a.dtype)
    out = _pcall(
        body, name=name,
        out_shape=(pltpu.SemaphoreType.DMA((ncopies,)), pltpu.SemaphoreType.DMA((ncopies,)),
                   *[hbm(a) for a in srcs], *[hbm(a) for a in lands], jax.ShapeDtypeStruct((8, LANES), F32)),
        in_specs=[_HBM] * (ns + nl) + [_ANY], out_specs=(_SEM, _SEM, *[_HBM] * (ns + nl), _VM),
        input_output_aliases={i: 2 + i for i in range(ns + nl)},
        compiler_params=pltpu.CompilerParams(has_side_effects=_EFFECT),
    )(*[pltpu.with_memory_space_constraint(a, pltpu.HBM) for a in (*srcs, *lands)], after)
    return out[0], out[1], list(out[2:2 + ns]), list(out[2 + ns:2 + ns + nl]), out[-1]


def _split_wait(send_sems, recv_sems, srcs, lands, plan, after, *, name, with_sources=False):
    ns, nl = len(srcs), len(lands)

    def body(*refs):
        src_refs, land_refs = refs[:ns], refs[ns:ns + nl]
        send_sems, recv_sems = refs[ns + nl], refs[ns + nl + 1]
        x, y, c = _pos()
        for k, (src, _, _, mine) in enumerate(plan(x, y, c, src_refs, land_refs)):
            cp = pltpu.make_async_remote_copy(src_ref=src, dst_ref=mine, send_sem=send_sems.at[k],
                                              recv_sem=recv_sems.at[k], device_id=(x, y, c), device_id_type=_MESH)
            cp.wait_send()
            cp.wait_recv()

    hbm = lambda a: pltpu.HBM(a.shape, a.dtype)
    out = _pcall(
        body, name=name, out_shape=tuple(hbm(a) for a in (*srcs, *lands)),
        in_specs=[_HBM] * (ns + nl) + [_SEM, _SEM, _ANY], out_specs=tuple([_HBM] * (ns + nl)),
        input_output_aliases={i: i for i in range(ns + nl)},
        compiler_params=pltpu.CompilerParams(has_side_effects=_EFFECT),
    )(*srcs, *lands, send_sems, recv_sems, after)
    return (list(out[:ns]), list(out[ns:])) if with_sources else list(out[ns:])


def _ag_plan(nw):
    def plan(x, y, c, sh_refs, full_refs):
        j = 2 * x + y
        out = []
        for wi in range(nw):
            for fx, fy in _CHIP_FLIPS:
                px, py = _flip(x, fx), _flip(y, fy)
                out.append((sh_refs[wi].at[c], full_refs[wi].at[j, c], (px, py, c), full_refs[wi].at[2 * px + py, c]))
            out.append((sh_refs[wi], full_refs[wi].at[j], (x, y, 1 - c), full_refs[wi].at[j]))
        return out
    return plan


def _ag_pass(fulls, *, name):
    nw = len(fulls)

    def body(*refs):
        in_refs, out_refs = refs[:nw], refs[nw:2 * nw]
        send_sems, recv_sems = refs[2 * nw:]
        x, y, c = _pos()
        cps = []
        for wi in range(nw):
            for k, (fx, fy) in enumerate(_CHIP_FLIPS):
                jp = 2 * _flip(x, fx) + _flip(y, fy)
                sems = dict(send_sem=send_sems.at[3 * wi + k], recv_sem=recv_sems.at[3 * wi + k], device_id_type=_MESH)
                send = pltpu.make_async_remote_copy(src_ref=in_refs[wi].at[jp, c], dst_ref=out_refs[wi].at[jp, c],
                                                    device_id=(x, y, 1 - c), **sems)
                recv = pltpu.make_async_remote_copy(src_ref=in_refs[wi].at[jp, c], dst_ref=out_refs[wi].at[jp, 1 - c],
                                                    device_id=(x, y, c), **sems)
                cps.append((send, recv))
        for send, _ in cps:
            send.start()
        for send, recv in cps:
            send.wait_send()
            recv.wait_recv()

    return _pcall(body, name=name, out_shape=[jax.ShapeDtypeStruct(f.shape, f.dtype) for f in fulls],
                  in_specs=[_ANY] * nw, out_specs=[_ANY] * nw, input_output_aliases={i: i for i in range(nw)},
                  scratch_shapes=_sem_pairs(3 * nw))(*fulls)


def _sib_plan(nw):
    def plan(x, y, c, g_refs, ra_refs):
        return [(g_refs[wi].at[k, 1 - c], ra_refs[wi].at[k], (x, y, 1 - c), ra_refs[wi].at[k])
                for wi in range(nw) for k in range(N_CHIP)]
    return plan


def _rs_plan(nw):
    def plan(x, y, c, s_refs, rb_refs):
        out = []
        for wi in range(nw):
            for k, (fx, fy) in enumerate(_CHIP_FLIPS):
                px, py = _flip(x, fx), _flip(y, fy)
                out.append((s_refs[wi].at[2 * px + py], rb_refs[wi].at[k], (px, py, c), rb_refs[wi].at[k]))
        return out
    return plan


def _sibling_swap(ts, *, name):
    nw = len(ts)

    def body(*refs):
        t_refs, out_refs = refs[:nw], refs[nw:2 * nw]
        send_sems, recv_sems = refs[2 * nw:]
        x, y, c = _pos()
        cps = [pltpu.make_async_remote_copy(src_ref=t_refs[wi], dst_ref=out_refs[wi], send_sem=send_sems.at[wi],
                                            recv_sem=recv_sems.at[wi], device_id=(x, y, 1 - c), device_id_type=_MESH)
               for wi in range(nw)]
        for cp in cps:
            cp.start()
        for cp in cps:
            cp.wait()

    return _pcall(body, name=name, out_shape=[jax.ShapeDtypeStruct(t.shape, t.dtype) for t in ts],
                  in_specs=[_ANY] * nw, out_specs=[_ANY] * nw, scratch_shapes=_sem_pairs(nw))(*ts)


def _rs_add(g, ra, c_arr, *, name):
    n, _, r, w = g.shape

    def body(c_ref, g_ref, ra_ref, s_ref, sb_ref):
        t = g_ref[...] + ra_ref[...]
        s_ref[...] = t
        sb_ref[...] = t.astype(BF16)

    grid_spec = pltpu.PrefetchScalarGridSpec(
        num_scalar_prefetch=1, grid=(n,),
        in_specs=[pl.BlockSpec((None, None, r, w), lambda k, c: (k, c[0], 0, 0)),
                  pl.BlockSpec((None, r, w), lambda k, c: (k, 0, 0))],
        out_specs=[pl.BlockSpec((None, r, w), lambda k, c: (k, 0, 0))] * 2)
    return _pcall(body, name=name, grid_spec=grid_spec,
                  out_shape=[jax.ShapeDtypeStruct((n, r, w), F32), jax.ShapeDtypeStruct((n, r, w), BF16)],
                  compiler_params=_PAR)(c_arr, g, ra)


def _rs_final(s, rb, j_arr, *, name):
    _, r, w = s.shape

    def body(j_ref, s_ref, rb_ref, t_ref):
        t_ref[...] = ((s_ref[...] + rb_ref[0].astype(F32)) + rb_ref[1].astype(F32)) + rb_ref[2].astype(F32)

    grid_spec = pltpu.PrefetchScalarGridSpec(
        num_scalar_prefetch=1, grid=(1,),
        in_specs=[pl.BlockSpec((None, r, w), lambda i, j: (j[0], 0, 0)),
                  pl.BlockSpec((3, r, w), lambda i, j: (0, 0, 0))],
        out_specs=pl.BlockSpec((r, w), lambda i, j: (0, 0)))
    return _pcall(body, name=name, grid_spec=grid_spec, out_shape=jax.ShapeDtypeStruct((r, w), F32),
                  compiler_params=_ARB)(j_arr, s, rb)


def _tile2(g):
    return jnp.concatenate([g, g], axis=1)


_BIG = ("w_in", "w_ffn_in", "w_ffn_out", "w_o", "w_proj_a", "w_proj_b")
_BIG_SHARD = {"w_in": (1024, 1472), "w_ffn_in": (1024, 1408), "w_ffn_out": (704, 1024), "w_o": (256, 1024),
              "w_proj_a": (512, 256), "w_proj_b": (256, 256)}


def _device_step(x2, tgt, mod, first_weights, late_weights, early_grads, mid_grads, g_norm1, g_norm2, b_gate, g_qa, g_ka,
                 g_qb, g_kb, rpb):
    d = D_MODEL
    sh1, sc1, gt1, sh2, sc2, gt2 = [mod[:, k * d:(k + 1) * d] for k in range(6)]

    colmask, oh_col, na_ks, na_cls = _na_constants()
    rp = jnp.pad(rpb, ((0, 0), (0, 16 - _RPB_RO), (RP_LANE0, LANES - RP_LANE0 - _RPB_CO)), constant_values=NEG)
    na = (rp, jnp.asarray(colmask))
    na_ks, na_cls = jnp.asarray(na_ks), jnp.asarray(na_cls)
    dil = [_dil_constants(dd) for _, dd in DIL_CONFIGS]
    gains = jnp.concatenate([_tile2(g_qa), _tile2(g_ka), _tile2(g_qb), _tile2(g_kb)], axis=0)
    cos_t, sa_t, sb_t = _rope_tables()

    wts = first_weights(cos_t)
    h1, qkvn, qk_pre, gates, *qkv_dil = _pre_attn_fwd(x2, cos_t, sa_t, sb_t, g_norm1, sc1, sh1, wts["w_qkv"],
                                                      wts["w_gates"], gains, name="pre_attn_fwd")
    o_a, lse_a = _attn_fwd(qkvn, 0, 4, 8, 4, None, na_ks, na_cls, NA_NK, name="attn_a_fwd", na=na)
    arrs, o_g, l_g = [], [], []
    res = lambda t, dd: t if dd == 1 else (t, dd)
    for g, (_, dd) in enumerate(DIL_CONFIGS):
        tab_g, ks_g, cls_g, nk_g = jnp.asarray(dil[g][0]), jnp.asarray(dil[g][1]), jnp.asarray(dil[g][2]), dil[g][3]
        arr, cb = (qkvn, (12, 18, 24)) if dd == 1 else (qkv_dil.pop(0), (0, 2, 4))
        op, lp = _attn_fwd(arr, cb[0], cb[1], cb[2], 2, tab_g, ks_g, cls_g, nk_g, name=f"attn_d{g}_fwd", qb=DIL_QB)
        arrs.append((arr, cb, tab_g, ks_g, cls_g, nk_g))
        o_g.append(res(op, dd))
        l_g.append(res(lp, dd))
    wts = dict(wts, **late_weights(o_a))
    o_b, merged, mo, x1, h2 = _post_attn_fwd(o_a, o_g, l_g, gates, x2, wts["w_pa"], wts["w_pb"], wts["w_o"], b_gate,
                                             gt1, g_norm2, sc2, sh2, name="post_attn_fwd")
    act, ff = _ffn_fwd(h2, wts["w_ffn_in"], name="ffn_fwd")

    dy, dffo, dff, dgt2, loss_v = _ffn_mid(act, ff, x1, tgt, wts["w_ffn_out"], gt2, name="ffn_mid")
    grads = {}
    g_ffn_out = _wgrad(act, dffo, name="wg_ffn_out", tm=D_FF // 2, tn=d, tk=2048)
    grads["w_ffn_out"] = g_ffn_out.reshape(N_CHIP, D_FF // N_CHIP, d)
    grads["w_ffn_in"] = _wgrad(h2, dff, name="wg_ffn_in", tm=512, tn=2 * FF_CHIP, chips=N_CHIP)
    dx1, dmo, sums2 = _ffn_in_bwd(dff, x1, dy, mo, wts["w_ffn_in"], g_norm2, sc2, gt1, name="ffn_in_bwd")
    grads["w_o"] = _wgrad(merged, dmo, name="wg_o", tm=d, tn=d, tk=2048).reshape(N_CHIP, d // N_CHIP, d)
    pab = _post_attn_bwd(dmo, gates, o_a, o_g, l_g, wts["w_pa"], wts["w_pb"], wts["w_o"], b_gate, name="post_attn_bwd")
    dpa, dpb, dgates, do_a = pab[:4]
    do_g, dl_g, dbg = pab[4:7], pab[7:10], pab[10]
    g_pa = _wgrad(o_a, dpa, name="wg_pa", tm=WA, tn=d)
    g_pb = _wgrad(o_b, dpb, name="wg_pb", tm=WB_OUT, tn=d)
    grads["w_proj_a"] = g_pa.reshape(WA, N_CHIP, d // N_CHIP).transpose(1, 0, 2)
    grads["w_proj_b"] = g_pb.reshape(WB_OUT, N_CHIP, d // N_CHIP).transpose(1, 0, 2)
    order = early_grads(grads)
    dqs, dks, dvs = [], [], []
    for g, (_, dd) in enumerate(DIL_CONFIGS):
        arr, cb, tab_g, ks_g, cls_g, nk_g = arrs[g]
        plain = lambda t: t[0] if isinstance(t, tuple) else t
        dq, dk, dv = _attn_bwd(arr, cb[0], cb[1], cb[2], 2, tab_g, ks_g, cls_g, nk_g, do_g[g], plain(o_g[g]),
                               plain(l_g[g]), name=f"attn_d{g}_bwd", dlse=dl_g[g], qb=DIL_QB, order=order)
        dqs.append(res(dq, dd))
        dks.append(res(dk, dd))
        dvs.append(res(dv, dd))
    order = mid_grads(dv)
    dqa, dka, dva, bank = _attn_bwd(qkvn, 0, 4, 8, 4, None, na_ks, na_cls, NA_NK, do_a, o_a, lse_a,
                                    name="attn_a_bwd", na=na, order=order)
    dqkv, grad_x, dgains, sums1 = _pre_attn_bwd(qk_pre, [dqa, dka, dva] + dqs + dks + dvs, dgates, x2, dx1, cos_t, sa_t,
                                                sb_t, wts["w_qkv"], wts["w_gates"], gains, g_norm1, sc1,
                                                name="pre_attn_bwd")
    g_qkv = _wgrad(h1, dqkv, name="wg_qkv", tm=d, tn=W_QKV // 2)
    g_gates = _wgrad(h1, dgates, name="wg_gates", tm=d, tn=W_GATES)
    nc, cut = _BIG_SHARD["w_in"][1], 3 * _BIG_SHARD["w_in"][1] - W_QKV
    grads["w_in"] = jnp.stack([g_qkv[:, :nc], g_qkv[:, nc:2 * nc],
                               jnp.concatenate([g_qkv[:, 2 * nc:], g_gates[:, :cut]], axis=1), g_gates[:, cut:]])

    bank = bank.reshape(NA_HEADS, NA_SLOTS, GRID_W, 2, GRID_W).transpose(0, 1, 3, 2, 4)
    bank = jnp.pad(bank.reshape(NA_HEADS, 2 * NA_SLOTS, GRID_W * GRID_W), ((0, 0), (0, _BANK_ROWS - 2 * NA_SLOTS), (0, 0)))
    g2 = _bank_reduce(bank, jnp.asarray(oh_col), name="rpb_reduce")[:, :2 * NA_SLOTS].reshape(NA_HEADS, NA_SLOTS, 2, LANES)
    g_rpb = g2[:, 3:3 + _RPB_RO, 0, :_RPB_CO] + g2[:, 2:2 + _RPB_RO, 1, :_RPB_CO]

    dmod = jnp.concatenate([sums1[0:1], sums1[1:2], sums2[3:4], sums2[0:1], sums2[1:2], dgt2], axis=1)
    small = dict(g_norm1=sums1[2:3], g_norm2=sums2[2:3], b_gate=dbg, g_qa=dgains[0:1, :HEAD_DIM],
                 g_ka=dgains[1:2, :HEAD_DIM], g_qb=dgains[2:3, :HEAD_DIM], g_kb=dgains[3:4, :HEAD_DIM], rpb=g_rpb)
    return loss_v, grad_x, grads, dmod, small


_SMALL = ("b_ada", "g_norm1", "g_norm2", "b_gate", "g_qa", "g_ka", "g_qb", "g_kb", "rpb")
_SMALL_N = {"b_ada": 6 * D_MODEL, "g_norm1": D_MODEL, "g_norm2": D_MODEL, "b_gate": 2 * D_MODEL, "g_qa": HEAD_DIM,
            "g_ka": HEAD_DIM, "g_qb": HEAD_DIM, "g_kb": HEAD_DIM, "rpb": NA_HEADS * _RPB_RO * _RPB_CO}


def _pack_small(parts):
    flat = [parts[n].reshape(1, _SMALL_N[n]) for n in _SMALL]
    used = sum(_SMALL_N.values())
    return jnp.concatenate(flat + [jnp.zeros((1, STATS_W - used), F32)], axis=1)


def _unpack_small(v, shapes):
    out, at = {}, 0
    for n in _SMALL:
        out[n] = v[:, at:at + _SMALL_N[n]].reshape(shapes[n])
        at += _SMALL_N[n]
    return out


def _join_cols(t):
    _, r, c = t.shape
    return t.transpose(1, 0, 2).reshape(r, N_CHIP * c)


def kernel(x, c, w_ada, b_ada, g_norm1, g_norm2, w_in, b_gate, g_qa, g_ka, g_qb, g_kb, rpb, w_proj_a, w_proj_b, w_o, w_ffn_in, w_ffn_out, loss_target, m_w_ada, m_b_ada, m_g_norm1, m_g_norm2, m_w_in, m_b_gate, m_g_qa, m_g_ka, m_g_qb, m_g_kb, m_rpb, m_w_proj_a, m_w_proj_b, m_w_o, m_w_ffn_in, m_w_ffn_out, v_w_ada, v_b_ada, v_g_norm1, v_g_norm2, v_w_in, v_b_gate, v_g_qa, v_g_ka, v_g_qb, v_g_kb, v_rpb, v_w_proj_a, v_w_proj_b, v_w_o, v_w_ffn_in, v_w_ffn_out):
    names = ("w_ada", "b_ada", "g_norm1", "g_norm2", "w_in", "b_gate", "g_qa", "g_ka", "g_qb", "g_kb", "rpb",
             "w_proj_a", "w_proj_b", "w_o", "w_ffn_in", "w_ffn_out")
    w = dict(zip(names, (w_ada, b_ada, g_norm1, g_norm2, w_in, b_gate, g_qa, g_ka, g_qb, g_kb, rpb, w_proj_a, w_proj_b,
                         w_o, w_ffn_in, w_ffn_out)))
    m = dict(zip(names, (m_w_ada, m_b_ada, m_g_norm1, m_g_norm2, m_w_in, m_b_gate, m_g_qa, m_g_ka, m_g_qb, m_g_kb, m_rpb,
                         m_w_proj_a, m_w_proj_b, m_w_o, m_w_ffn_in, m_w_ffn_out)))
    v = dict(zip(names, (v_w_ada, v_b_ada, v_g_norm1, v_g_norm2, v_w_in, v_b_gate, v_g_qa, v_g_ka, v_g_qb, v_g_kb, v_rpb,
                         v_w_proj_a, v_w_proj_b, v_w_o, v_w_ffn_in, v_w_ffn_out)))
    d = D_MODEL
    xi, yi, ci = _pos()
    chip = 2 * xi + yi
    me = 2 * chip + ci
    ada_cols = 6 * d // N_CHIP

    c_arr, chip_arr = ci.reshape(1).astype(jnp.int32), chip.reshape(1).astype(jnp.int32)
    first, rest = _BIG[:1], _BIG[1:]

    c_all = _small_allgather(c.reshape(8, d // 8), name="ag_c").reshape(N_DEV, d)
    b_sh = lax.dynamic_slice(b_ada, (0, chip * ada_cols), (1, ada_cols))
    mod_part = _ada_fwd(c_all, w_ada[0], b_sh, name="ada_fwd")
    mod_all = _small_allgather(mod_part, name="ag_mod").reshape(N_CHIP, 2, 8, ada_cols)[:, 0]
    mod = lax.dynamic_index_in_dim(mod_all, me, axis=1, keepdims=False).reshape(1, 6 * d)

    halves = {n: (2, _BIG_SHARD[n][0] // 2, _BIG_SHARD[n][1]) for n in _BIG}
    shards = {n: w[n][0].astype(BF16).reshape(halves[n]) for n in _BIG}
    land = lambda n: lax.empty((N_CHIP,) + halves[n], BF16)
    ag1 = _split_start([shards[n] for n in first], [land(n) for n in first], _ag_plan(1), 4, mod, name="ag1_start")
    ag2 = _split_start([shards[n] for n in rest], [land(n) for n in rest], _ag_plan(len(rest)), 4 * len(rest),
                       ag1[4], name="ag2_start")
    rpb_after = rpb[0] + ag2[4][0, 0]

    def first_weights(after):
        after = after[:1, :1] + ag2[4][:1, :1]
        full1 = _split_wait(ag1[0], ag1[1], ag1[2], ag1[3], _ag_plan(1), after, name="ag1_wait")
        p_in = _ag_pass(full1, name="ag1_pass")[0].reshape((N_CHIP,) + _BIG_SHARD["w_in"])
        cut = W_QKV - 2 * _BIG_SHARD["w_in"][1]
        return dict(w_qkv=jnp.concatenate([p_in[0], p_in[1], p_in[2][:, :cut]], axis=1),
                    w_gates=jnp.concatenate([p_in[2][:, cut:], p_in[3]], axis=1))

    def late_weights(after):
        full2 = _split_wait(ag2[0], ag2[1], ag2[2], ag2[3], _ag_plan(len(rest)), after, name="ag2_wait")
        full2 = _ag_pass(full2, name="ag2_pass")
        full = {n: fu.reshape((N_CHIP,) + _BIG_SHARD[n]) for n, fu in zip(rest, full2)}
        return dict(w_pa=_join_cols(full["w_proj_a"]), w_pb=_join_cols(full["w_proj_b"]), w_o=full["w_o"].reshape(d, d),
                    w_ffn_in=_join_cols(full["w_ffn_in"]), w_ffn_out=full["w_ffn_out"].reshape(D_FF, d))

    def sib_begin(group, grads, tag):
        gps = [grads[n].reshape((N_CHIP,) + halves[n]) for n in group]
        lands = [lax.empty((N_CHIP,) + halves[n][1:], F32) for n in group]
        return _split_start(gps, lands, _sib_plan(len(group)), N_CHIP * len(group), gps[0], name=f"rs_sib_{tag}_start")

    def rs_begin(group, sib, after, tag):
        gps, ras = _split_wait(sib[0], sib[1], sib[2], sib[3], _sib_plan(len(group)), after,
                               name=f"rs_sib_{tag}_wait", with_sources=True)
        sums = [_rs_add(gp, ra, c_arr, name=f"rs_add_{n}") for n, gp, ra in zip(group, gps, ras)]
        lands = [lax.empty((3,) + halves[n][1:], BF16) for n in group]
        st = _split_start([sb for _, sb in sums], lands, _rs_plan(len(group)), 3 * len(group), sums[0][0],
                          name=f"rs_{tag}_start")
        return sums, st

    def rs_end(group, begun, after, tag):
        sums, st = begun
        rbs = _split_wait(st[0], st[1], st[2], st[3], _rs_plan(len(group)), after, name=f"rs_{tag}_wait")
        return [_rs_final(sf, rb, chip_arr, name=f"rs_final_{n}") for n, (sf, _), rb in zip(group, sums, rbs)]

    begun = {}

    def early_grads(grads):
        begun["sib_rest"] = sib_begin(rest, grads, "rest")
        return begun["sib_rest"][4]

    def mid_grads(after):
        begun["rest"] = rs_begin(rest, begun["sib_rest"], after, "rest")
        return begun["rest"][1][4]

    loss_v, grad_x, grads, dmod, small = _device_step(
        x[0], loss_target[0], mod, first_weights, late_weights, early_grads, mid_grads, g_norm1, g_norm2, b_gate, g_qa,
        g_ka, g_qb, g_kb, rpb_after)
    sib_first = sib_begin(first, grads, "first")

    g, delta, new_m, new_v = {}, {}, {}, {}

    def finish(group, ts, tag):
        others = _sibling_swap(ts, name=f"rs_pair_{tag}")
        for n, t, o in zip(group, ts, others):
            gg, dl, nm, nv = _adamw_halves(w[n][0], t, o, m[n][0], v[n][0], c_arr, name=f"adamw_{n}")
            g[n], delta[n], new_m[n], new_v[n] = gg[None], dl[None], nm[None], nv[None]

    finish(rest, rs_end(rest, begun["rest"], sib_first[4], "rest"), "rest")
    done_rest = sum(new_v[n][0, :1, :1] for n in rest)
    begun["first"] = rs_begin(first, sib_first, done_rest, "first")

    stats = _pack_small(dict(b_ada=dmod, **small)) + begun["first"][1][4][0, 0]
    stats = stats.at[:, STATS_W - 1].set(loss_v[0, 0])
    rows = _small_allgather(stats.reshape(8, STATS_W // 8), name="ag_stats").reshape(N_DEV, STATS_W)
    dmod_sh = lax.dynamic_slice(rows, (0, chip * ada_cols), (8, ada_cols))
    g_ada = _ada_bwd(c_all.T, dmod_sh, name="ada_bwd")
    tot = _row_sum(rows, name="stats_sum")
    g_small = _unpack_small(tot, {n: w[n].shape for n in _SMALL})

    finish(first, rs_end(first, begun["first"], tot, "first"), "first")

    dl, nm, nv = _adamw(w_ada[0], g_ada, m_w_ada[0], v_w_ada[0], name="adamw_w_ada")
    g["w_ada"], delta["w_ada"], new_m["w_ada"], new_v["w_ada"] = g_ada[None], dl[None], nm[None], nv[None]
    shapes = {n: w[n].shape for n in _SMALL}
    dl, nm, nv = _adamw(_pack_small({n: w[n] for n in _SMALL}), tot, _pack_small({n: m[n] for n in _SMALL}),
                        _pack_small({n: v[n] for n in _SMALL}), name="adamw_small")
    delta.update(_unpack_small(dl, shapes))
    new_m.update(_unpack_small(nm, shapes))
    new_v.update(_unpack_small(nv, shapes))
    g.update(g_small)

    loss = tot[0, STATS_W - 1]
    return (loss, grad_x[None], *[g[n] for n in names], *[delta[n] for n in names], *[new_m[n] for n in names],
            *[new_v[n] for n in names])
```

```python
import numpy as np

import jax
import jax.numpy as jnp
from jax import lax
from jax.experimental import pallas as pl
from jax.experimental.pallas import tpu as pltpu

F32 = jnp.float32
BF16 = jnp.bfloat16

D_MODEL = 1024
SEQ = 8192
HEAD_DIM = 64
GRID_W = 64
ROWS = SEQ // GRID_W
NA_HEADS = 8
NA_KH = 8
NA_KW = 16
DIL_CONFIGS = ((128, 1), (512, 4), (2048, 16))
ROT_DIM = 16
ROPE_THETA = 500000.0
D_FF = 2816
EPS = 1e-6
NEG = -1e30
WA = 512
WB = 768
WB_OUT = 256
W_QKV = 3 * WA + 3 * WB
W_QK = 2 * WA + 2 * WB
W_GATES = 2 * D_MODEL
SCALE = HEAD_DIM ** -0.5

ADAM_LR = 0.001
ADAM_B1 = 0.9
ADAM_B2 = 0.999
ADAM_EPS = 1e-08
ADAM_WD = 0.01
ADAM_STEP = 10

LANES = 128
ROW_TILE = 256
ROW_TILES = {"ffn_fwd": 512, "post_attn_fwd": 512, "post_attn_bwd": 512,
             "ffn_in_bwd": 512}
Q_BLOCK = 256
NA_QROWS = Q_BLOCK // GRID_W
NA_KROWS = NA_QROWS + NA_KH - 1
NA_NK = NA_KROWS * GRID_W
NA_PAIRS = (NA_KROWS + 1) // 2
NA_W = NA_PAIRS * LANES
NA_RO_NONE = 15
NA_SLOTS = 21
RP_LANE0 = GRID_W - NA_KW
DIL_HALF = 64
DIL_QB = 512
N_QBLK = SEQ // Q_BLOCK

N_DEV = 8
N_CHIP = 4
FF_CHIP = 2 * D_FF // N_CHIP
STATS_W = 14336


def _pcall(body, *, name, **kw):
    return pl.pallas_call(body, name=name, **kw)


_NT = (((1,), (1,)), ((), ()))
_TN = (((0,), (0,)), ((), ()))
_ARB = pltpu.CompilerParams(dimension_semantics=("arbitrary",))
_PAR = pltpu.CompilerParams(dimension_semantics=("parallel",))


def _dot(a, b):
    return jnp.dot(a, b, preferred_element_type=F32)


def _dot_nt(a, b):
    return lax.dot_general(a, b, _NT, preferred_element_type=F32)


def _wgrad(a, b, *, name, tm, tn, tk=1024, chips=None):
    s, ma = a.shape
    nb = b.shape[1]
    nk = s // tk
    nc = nb // chips if chips else tn
    cpb = tn // nc

    def body(a_ref, b_ref, o_ref, acc):
        k = pl.program_id(2)
        r = lax.dot_general(a_ref[...].astype(BF16), b_ref[...].astype(BF16), _TN, preferred_element_type=F32)

        @pl.when(k == 0)
        def _():
            acc[...] = r

        @pl.when(k > 0)
        def _():
            acc[...] += r

        @pl.when(k == nk - 1)
        def _():
            if chips:
                for q in range(cpb):
                    o_ref[q] = acc[:, q * nc:(q + 1) * nc]
            else:
                o_ref[...] = acc[...]

    if chips:
        o_spec = pl.BlockSpec((cpb, tm, nc), lambda i, j, k: (j, i, 0))
        out_shape = jax.ShapeDtypeStruct((chips, ma, nc), F32)
    else:
        o_spec = pl.BlockSpec((tm, tn), lambda i, j, k: (i, j))
        out_shape = jax.ShapeDtypeStruct((ma, nb), F32)
    return _pcall(
        body, name=name, grid=(ma // tm, nb // tn, nk),
        in_specs=[pl.BlockSpec((tk, tm), lambda i, j, k: (k, i)), pl.BlockSpec((tk, tn), lambda i, j, k: (k, j))],
        out_specs=o_spec, out_shape=out_shape, scratch_shapes=[pltpu.VMEM((tm, tn), F32)],
        compiler_params=pltpu.CompilerParams(dimension_semantics=("parallel", "parallel", "arbitrary")),
    )(a, b)


def _row_call(body, *, name, row_ins, res_ins, row_outs, acc_outs=(), scratch=()):
    row_ins = [a if isinstance(a, tuple) else (a, 1) for a in row_ins]
    row_outs = [o if len(o) == 3 else (*o, 1) for o in row_outs]
    s = row_ins[0][0].shape[0]
    tile = ROW_TILES.get(name, ROW_TILE)
    n = s // tile
    nri, nre, nro, nao = len(row_ins), len(res_ins), len(row_outs), len(acc_outs)

    def whole(shape):
        nd = len(shape)
        return pl.BlockSpec(tuple(shape), lambda i: (0,) * nd, pipeline_mode=pl.Buffered(1))

    def whole_out(shape):
        nd = len(shape)
        return pl.BlockSpec(tuple(shape), lambda i: (0,) * nd)

    def rows(w, d):
        if d == 1:
            return pl.BlockSpec((tile, w), lambda i: (i, 0))
        return pl.BlockSpec((d, tile // d, w), lambda i: (0, i, 0))

    in_specs = [rows(a.shape[1], d) for a, d in row_ins]
    in_specs += [whole(a.shape) for a in res_ins]
    out_specs = [rows(w, d) for w, _, d in row_outs]
    out_specs += [whole_out(shp) for shp, _ in acc_outs]
    out_shape = [jax.ShapeDtypeStruct((s, w) if d == 1 else (d, s // d, w), dt) for w, dt, d in row_outs]
    out_shape += [jax.ShapeDtypeStruct(tuple(shp), dt) for shp, dt in acc_outs]

    def wrapped(*refs):
        at = [0, nri, nri + nre, nri + nre + nro, nri + nre + nro + nao]
        body(pl.program_id(0), n, refs[at[0]:at[1]], refs[at[1]:at[2]], refs[at[2]:at[3]], refs[at[3]:at[4]],
             refs[at[4]:])

    args = [a if d == 1 else a.reshape(d, s // d, a.shape[1]) for a, d in row_ins]
    outs = _pcall(wrapped, name=name, grid=(n,), in_specs=in_specs, out_specs=out_specs, out_shape=out_shape,
                  scratch_shapes=list(scratch), compiler_params=_ARB)(*args, *res_ins)
    return [o.reshape(s, o.shape[-1]) if k < nro and row_outs[k][2] != 1 else o for k, o in enumerate(outs)]


def _stage_shape(name):
    return pltpu.VMEM((4, ROW_TILES.get(name, ROW_TILE), LANES), F32)


def _from_residue(ref, col, stage, slot):
    d, n = ref.shape[0], ref.shape[1]
    for r in range(d):
        stage.at[slot][pl.ds(r, n, stride=d), :] = ref[r, :, col:col + LANES].astype(F32)
    return stage[slot]


def _natural(ref, stage, slot0):
    if len(ref.shape) == 2:
        return ref[...]
    return jnp.concatenate([_from_residue(ref, c * LANES, stage, (slot0 + c) % 4)
                            for c in range(ref.shape[2] // LANES)], axis=1)


def _to_residue(val, ref, col, stage, slot):
    d, n = ref.shape[0], ref.shape[1]
    stage[slot] = val
    for r in range(d):
        ref[r, :, col:col + LANES] = stage.at[slot][pl.ds(r, n, stride=d), :].astype(ref.dtype)


def _fold8(t):
    r, w = t.shape
    return jnp.sum(t.reshape(r // 8, 8, w), axis=0)


def _sigmoid(t):
    return 0.5 * (jnp.tanh(0.5 * t) + 1.0)


def _head_lanes():
    return lax.broadcasted_iota(jnp.int32, (1, LANES), 1) < HEAD_DIM


def _head_mean(t, lo):
    s_lo = jnp.sum(jnp.where(lo, t, 0.0), axis=1, keepdims=True)
    s_hi = jnp.sum(jnp.where(lo, 0.0, t), axis=1, keepdims=True)
    return jnp.where(lo, s_lo, s_hi) * (1.0 / HEAD_DIM)


def _rms_mod(xv, g, sc, sh):
    rstd = lax.rsqrt(jnp.mean(xv * xv, axis=1, keepdims=True) + EPS)
    return (xv * rstd * g) * (1.0 + sc) + sh


def _rms_mod_bwd(xv, dh, g, sc):
    rstd = lax.rsqrt(jnp.mean(xv * xv, axis=1, keepdims=True) + EPS)
    xhat = xv * rstd
    dn = dh * (1.0 + sc)
    dxhat = dn * g
    dx = rstd * (dxhat - xhat * jnp.mean(dxhat * xhat, axis=1, keepdims=True))
    return dx, dh, dh * (xhat * g), dn * xhat


def _mix_weights(ls):
    m = jnp.maximum(jnp.maximum(ls[0], ls[1]), ls[2])
    es = [jnp.exp(t - m) for t in ls]
    den = es[0] + es[1] + es[2]
    return [e / den for e in es]


def _rope_tables():
    half = ROT_DIM // 2
    inv_freq = ROPE_THETA ** (-(jnp.arange(half, dtype=F32) * 2.0) / ROT_DIM)
    lane = np.arange(LANES) % HEAD_DIM
    ang = jnp.arange(SEQ).astype(F32)[:, None] * jnp.tile(inv_freq, LANES // half)[None, :]
    cos, sin = jnp.cos(ang), jnp.sin(ang)
    first, second = jnp.asarray(lane < half)[None, :], jnp.asarray((lane >= half) & (lane < ROT_DIM))[None, :]
    cos_t = jnp.where(first | second, cos, 1.0)
    return cos_t, jnp.where(second, sin, 0.0), jnp.where(first, -sin, 0.0)


_SECTIONS = ((0, WA, 0, False), (WA, 2 * WA, 1, False), (2 * WA, 3 * WA, -1, False),
             (3 * WA, 3 * WA + WB, 2, True), (3 * WA + WB, 3 * WA + 2 * WB, 3, True), (3 * WA + 2 * WB, W_QKV, -1, False))


def _pre_attn_fwd(x, cos_t, sa_t, sb_t, g1, sc1, sh1, w_qkv, w_gates, gains, *, name):
    half = ROT_DIM // 2
    dilated = [(g, dd) for g, (_, dd) in enumerate(DIL_CONFIGS) if dd > 1]

    def body(i, n, rin, res, rout, aout, scr):
        x_ref, cos_ref, sa_ref, sb_ref = rin
        g_ref, sc_ref, sh_ref, wq_ref, wg_ref, gains_ref = res
        h1_ref, qkvn_ref, pre_ref, gates_ref = rout[:4]
        group_ref = {g: rout[4 + k] for k, (g, _) in enumerate(dilated)}
        (stage,) = scr
        staged = 0
        hb = _rms_mod(x_ref[...], g_ref[...], sc_ref[...], sh_ref[...]).astype(BF16)
        h1_ref[...] = hb
        gates_ref[...] = _dot(hb, wg_ref[...]).astype(BF16)
        lo = _head_lanes()
        cosv, sav, sbv = cos_ref[...], sa_ref[...], sb_ref[...]
        pre_at = 0
        for si, (c0, c1, kind, rot) in enumerate(_SECTIONS):
            sec = _dot(hb, wq_ref[:, c0:c1])
            for ch in range((c1 - c0) // LANES):
                t = sec[:, ch * LANES:(ch + 1) * LANES]
                if kind >= 0:
                    pre_ref[:, pre_at:pre_at + LANES] = t.astype(BF16)
                    pre_at += LANES
                    t = t * lax.rsqrt(_head_mean(t * t, lo) + EPS) * gains_ref[kind:kind + 1, :]
                    if rot:
                        t = t * cosv + pltpu.roll(t, half, 1) * sav + pltpu.roll(t, LANES - half, 1) * sbv
                qkvn_ref[:, c0 + ch * LANES:c0 + (ch + 1) * LANES] = t.astype(BF16)
                group = ch * LANES // WB_OUT if si >= 3 else 0
                if group in group_ref:
                    col = (si - 3) * WB_OUT + ch * LANES % WB_OUT
                    _to_residue(t, group_ref[group], col, stage, staged % 4)
                    staged += 1

    return _row_call(body, name=name, row_ins=[x, cos_t, sa_t, sb_t], res_ins=[g1, sc1, sh1, w_qkv, w_gates, gains],
                     row_outs=[(D_MODEL, BF16), (W_QKV, BF16), (W_QK, BF16), (W_GATES, BF16)]
                     + [(3 * WB_OUT, BF16, dd) for _, dd in dilated], scratch=[_stage_shape(name)])


def _pre_attn_bwd(qk_pre, d_parts, dgates, x, dx1, cos_t, sa_t, sb_t, w_qkv, w_gates, gains, g1, sc1, *, name):
    half = ROT_DIM // 2
    nparts = len(d_parts)
    where = []
    residue = [isinstance(part, tuple) for part in d_parts]
    for pi, part in enumerate(d_parts):
        width = (part[0] if residue[pi] else part).shape[1]
        where += [(pi, cj) for cj in range(width // LANES)]
    assert len(where) == W_QKV // LANES

    def body(i, n, rin, res, rout, aout, scr):
        pre_ref, d_refs = rin[0], rin[1:1 + nparts]
        dgates_ref, x_ref, dx1_ref, cos_ref, sa_ref, sb_ref = rin[1 + nparts:]
        wq_ref, wg_ref, gains_ref, g_ref, sc_ref = res
        dqkv_ref, gx_ref = rout
        dgains_ref, sums_ref = aout
        accg, accs, stage = scr
        staged = 0

        @pl.when(i == 0)
        def _():
            accg[...] = jnp.zeros_like(accg)
            accs[...] = jnp.zeros_like(accs)

        lo = _head_lanes()
        cosv, sav, sbv = cos_ref[...], sa_ref[...], sb_ref[...]
        dh = _dot_nt(dgates_ref[...], wg_ref[...])
        pre_at = 0
        for c0, c1, kind, rot in _SECTIONS:
            for ch in range((c1 - c0) // LANES):
                pi, cj = where[c0 // LANES + ch]
                if residue[pi]:
                    dt = _from_residue(d_refs[pi], cj * LANES, stage, staged % 4)
                    staged += 1
                else:
                    dt = d_refs[pi][:, cj * LANES:(cj + 1) * LANES]
                if kind >= 0:
                    if rot:
                        dt = dt * cosv + pltpu.roll(dt * sav, LANES - half, 1) + pltpu.roll(dt * sbv, half, 1)
                    t = pre_ref[:, pre_at:pre_at + LANES].astype(F32)
                    pre_at += LANES
                    rstd = lax.rsqrt(_head_mean(t * t, lo) + EPS)
                    xhat = t * rstd
                    accg[kind] += _fold8(dt * xhat)
                    dxhat = dt * gains_ref[kind:kind + 1, :]
                    dt = rstd * (dxhat - xhat * _head_mean(dxhat * xhat, lo))
                dqkv_ref[:, c0 + ch * LANES:c0 + (ch + 1) * LANES] = dt.astype(BF16)
            dh = dh + _dot_nt(dqkv_ref[:, c0:c1], wq_ref[:, c0:c1])
        dx, t_sh, t_sc, t_g = _rms_mod_bwd(x_ref[...], dh, g_ref[...], sc_ref[...])
        gx_ref[...] = dx1_ref[...] + dx
        accs[0] += _fold8(t_sh)
        accs[1] += _fold8(t_sc)
        accs[2] += _fold8(t_g)

        @pl.when(i == n - 1)
        def _():
            t = jnp.sum(accg[...], axis=1)
            dgains_ref[...] = t + pltpu.roll(t, HEAD_DIM, 1)
            sums_ref[...] = jnp.sum(accs[...], axis=1)

    return _row_call(
        body, name=name, row_ins=[qk_pre, *d_parts, dgates, x, dx1, cos_t, sa_t, sb_t],
        res_ins=[w_qkv, w_gates, gains, g1, sc1], row_outs=[(W_QKV, BF16), (D_MODEL, F32)],
        acc_outs=[((4, LANES), F32), ((3, D_MODEL), F32)],
        scratch=[pltpu.VMEM((4, 8, LANES), F32), pltpu.VMEM((3, 8, D_MODEL), F32), _stage_shape(name)])


def _post_attn_fwd(o_a, o_g, l_g, gates, x, w_pa, w_pb, w_o, b_gate, gt1, g2, sc2, sh2, *, name):
    d = D_MODEL

    def body(i, n, rin, res, rout, aout, scr):
        oa_ref, o0, o1, o2, l0, l1, l2, gates_ref, x_ref = rin
        wpa_ref, wpb_ref, wo_ref, b_ref, gt_ref, g_ref, sc_ref, sh_ref = res
        ob_ref, merged_ref, mo_ref, x1_ref, h2_ref = rout
        (stage,) = scr
        ogs = [_natural(r, stage, 0) for r in (o0, o1, o2)]
        ws = _mix_weights([_natural(r, stage, 2) for r in (l0, l1, l2)])
        obb = (ws[0] * ogs[0] + ws[1] * ogs[1] + ws[2] * ogs[2]).astype(BF16)
        ob_ref[...] = obb
        pa = _dot(oa_ref[...].astype(BF16), wpa_ref[...])
        pb = _dot(obb, wpb_ref[...])
        ga = _sigmoid(gates_ref[:, :d].astype(F32) + b_ref[:, :d])
        gb = _sigmoid(gates_ref[:, d:].astype(F32) + b_ref[:, d:])
        merged = (ga * pa + gb * pb).astype(BF16)
        merged_ref[...] = merged
        mo = _dot(merged, wo_ref[...])
        mo_ref[...] = mo.astype(BF16)
        x1 = x_ref[...] + gt_ref[...] * mo
        x1_ref[...] = x1
        h2_ref[...] = _rms_mod(x1, g_ref[...], sc_ref[...], sh_ref[...]).astype(BF16)

    return _row_call(body, name=name, row_ins=[o_a, *o_g, *l_g, gates, x],
                     res_ins=[w_pa, w_pb, w_o, b_gate, gt1, g2, sc2, sh2],
                     row_outs=[(WB_OUT, BF16), (d, BF16), (d, BF16), (d, F32), (d, BF16)], scratch=[_stage_shape(name)])


def _ffn_fwd(h2, w_ffn_in, *, name):
    def body(i, n, rin, res, rout, aout, scr):
        (h_ref,), (w_ref,), (act_ref, ff_ref) = rin, res, rout
        hv = h_ref[...]
        for q in range(2):
            a = _dot(hv, w_ref[:, q * FF_CHIP:(q + 1) * FF_CHIP])
            up = _dot(hv, w_ref[:, D_FF + q * FF_CHIP:D_FF + (q + 1) * FF_CHIP])
            sl = slice(q * FF_CHIP, (q + 1) * FF_CHIP)
            act_ref[:, sl] = (a * _sigmoid(a) * up).astype(BF16)
            ff_ref[:, sl] = a.astype(BF16)
            ff_ref[:, D_FF + q * FF_CHIP:D_FF + (q + 1) * FF_CHIP] = up.astype(BF16)

    return _row_call(body, name=name, row_ins=[h2], res_ins=[w_ffn_in], row_outs=[(D_FF, BF16), (2 * D_FF, BF16)])


def _ffn_mid(act, ff, x1, tgt, w_ffn_out, gt2, *, name):
    d = D_MODEL

    def body(i, n, rin, res, rout, aout, scr):
        act_ref, ff_ref, x1_ref, tgt_ref = rin
        wo_ref, gt_ref = res
        dy_ref, dffo_ref, dff_ref = rout
        dgt_ref, loss_ref = aout
        (acc,) = scr

        @pl.when(i == 0)
        def _():
            acc[...] = jnp.zeros_like(acc)

        ffo = _dot(act_ref[...], wo_ref[...])
        gtv = gt_ref[...]
        e = x1_ref[...] + gtv * ffo - tgt_ref[...]
        dy = e * (1.0 / d)
        dy_ref[...] = dy
        dffo = (gtv * dy).astype(BF16)
        dffo_ref[...] = dffo
        acc[0] += _fold8(dy * ffo)
        acc[1] += _fold8(e * e)
        for q in range(2):
            sl = slice(q * FF_CHIP, (q + 1) * FF_CHIP)
            su = slice(D_FF + q * FF_CHIP, D_FF + (q + 1) * FF_CHIP)
            dact = _dot_nt(dffo, wo_ref[sl, :])
            a = ff_ref[:, sl].astype(F32)
            up = ff_ref[:, su].astype(F32)
            sg = _sigmoid(a)
            dff_ref[:, sl] = (dact * up * (sg * (1.0 + a * (1.0 - sg)))).astype(BF16)
            dff_ref[:, su] = (dact * (a * sg)).astype(BF16)

        @pl.when(i == n - 1)
        def _():
            dgt_ref[...] = jnp.sum(acc[0], axis=0, keepdims=True)
            tot = jnp.sum(jnp.sum(acc[1], axis=0, keepdims=True), axis=1, keepdims=True)
            loss_ref[...] = jnp.broadcast_to(tot * (0.5 / d), (1, LANES))

    return _row_call(body, name=name, row_ins=[act, ff, x1, tgt], res_ins=[w_ffn_out, gt2],
                     row_outs=[(d, F32), (d, BF16), (2 * D_FF, BF16)], acc_outs=[((1, d), F32), ((1, LANES), F32)],
                     scratch=[pltpu.VMEM((2, 8, d), F32)])


def _ffn_in_bwd(dff, x1, dy, mo, w_ffn_in, g2, sc2, gt1, *, name):
    d = D_MODEL

    def body(i, n, rin, res, rout, aout, scr):
        dff_ref, x1_ref, dy_ref, mo_ref = rin
        w_ref, g_ref, sc_ref, gt_ref = res
        dx1_ref, dmo_ref = rout
        (sums_ref,) = aout
        (acc,) = scr

        @pl.when(i == 0)
        def _():
            acc[...] = jnp.zeros_like(acc)

        dh = _dot_nt(dff_ref[...], w_ref[...])
        dx, t_sh, t_sc, t_g = _rms_mod_bwd(x1_ref[...], dh, g_ref[...], sc_ref[...])
        dx1 = dy_ref[...] + dx
        dx1_ref[...] = dx1
        dmo_ref[...] = (gt_ref[...] * dx1).astype(BF16)
        acc[0] += _fold8(t_sh)
        acc[1] += _fold8(t_sc)
        acc[2] += _fold8(t_g)
        acc[3] += _fold8(dx1 * mo_ref[...].astype(F32))

        @pl.when(i == n - 1)
        def _():
            sums_ref[...] = jnp.sum(acc[...], axis=1)

    return _row_call(body, name=name, row_ins=[dff, x1, dy, mo], res_ins=[w_ffn_in, g2, sc2, gt1],
                     row_outs=[(d, F32), (d, BF16)], acc_outs=[((4, d), F32)], scratch=[pltpu.VMEM((4, 8, d), F32)])


def _post_attn_bwd(dmo, gates, o_a, o_g, l_g, w_pa, w_pb, w_o, b_gate, *, name):
    d = D_MODEL

    def body(i, n, rin, res, rout, aout, scr):
        dmo_ref, gates_ref, oa_ref, o0, o1, o2, l0, l1, l2 = rin
        wpa_ref, wpb_ref, wo_ref, b_ref = res
        dpa_ref, dpb_ref, dgates_ref, doa_ref = rout[:4]
        do_refs, dl_refs = rout[4:7], rout[7:10]
        (dbg_ref,) = aout
        acc, stage = scr

        @pl.when(i == 0)
        def _():
            acc[...] = jnp.zeros_like(acc)

        ogs = [_natural(r, stage, 0) for r in (o0, o1, o2)]
        ws = _mix_weights([_natural(r, stage, 2) for r in (l0, l1, l2)])
        obb = (ws[0] * ogs[0] + ws[1] * ogs[1] + ws[2] * ogs[2]).astype(BF16)
        pa = _dot(oa_ref[...].astype(BF16), wpa_ref[...])
        pb = _dot(obb, wpb_ref[...])
        ga = _sigmoid(gates_ref[:, :d].astype(F32) + b_ref[:, :d])
        gb = _sigmoid(gates_ref[:, d:].astype(F32) + b_ref[:, d:])
        dm = _dot_nt(dmo_ref[...], wo_ref[...])
        dpa = (dm * ga).astype(BF16)
        dpb = (dm * gb).astype(BF16)
        dpa_ref[...] = dpa
        dpb_ref[...] = dpb
        dga = dm * pa * ga * (1.0 - ga)
        dgb = dm * pb * gb * (1.0 - gb)
        dgates_ref[:, :d] = dga.astype(BF16)
        dgates_ref[:, d:] = dgb.astype(BF16)
        acc[:, :d] += _fold8(dga)
        acc[:, d:] += _fold8(dgb)
        doa_ref[...] = _dot_nt(dpa, wpa_ref[...])
        dob = _dot_nt(dpb, wpb_ref[...])
        lo = _head_lanes()
        for ch in range(WB_OUT // LANES):
            sl = slice(ch * LANES, (ch + 1) * LANES)
            dv = dob[:, sl]
            wc = [w[:, sl] for w in ws]
            ts = [_head_mean(dv * og[:, sl], lo) * float(HEAD_DIM) for og in ogs]
            tbar = wc[0] * ts[0] + wc[1] * ts[1] + wc[2] * ts[2]
            for g in range(3):
                for k, (ref, val) in enumerate(((do_refs[g], wc[g] * dv), (dl_refs[g], wc[g] * (ts[g] - tbar)))):
                    if len(ref.shape) == 2:
                        ref[:, sl] = val
                    else:
                        _to_residue(val, ref, ch * LANES, stage, (2 * g + k) % 4)

        @pl.when(i == n - 1)
        def _():
            dbg_ref[...] = jnp.sum(acc[...], axis=0, keepdims=True)

    return _row_call(body, name=name, row_ins=[dmo, gates, o_a, *o_g, *l_g], res_ins=[w_pa, w_pb, w_o, b_gate],
                     row_outs=[(d, BF16), (d, BF16), (2 * d, BF16), (WA, F32)]
                     + 2 * [(WB_OUT, F32, dd) for _, dd in DIL_CONFIGS],
                     acc_outs=[((1, 2 * d), F32)], scratch=[pltpu.VMEM((8, 2 * d), F32), _stage_shape(name)])


def _na_class_tables():
    ro = np.full((3, NA_QROWS, 2 * NA_PAIRS), NA_RO_NONE, np.int64)
    slot = np.zeros((3, NA_QROWS, NA_PAIRS), np.int64)
    for t in range(3):
        for a in range(NA_QROWS):
            qr = _NA_CLASS_R0[t] + a
            rs = min(max(qr - NA_KH // 2, 0), ROWS - NA_KH)
            for b in range(NA_KROWS):
                kr = _NA_CLASS_K0[t] + b
                if rs <= kr < rs + NA_KH:
                    ro[t, a, b] = kr - qr + (NA_KH - 1)
            for j in range(NA_PAIRS):
                slot[t, a, j] = 2 * j - a + (_NA_CLASS_K0[t] - _NA_CLASS_R0[t] + NA_KH - 1) + (NA_QROWS - 1)
    assert slot.min() >= 0 and slot.max() < NA_SLOTS
    return ro, slot


def _na_build_bias(i, cls_ref, rp_ref, cm_ref, bias_scr):
    ro, _ = _na_class_tables()
    lo = _head_lanes()
    first = jnp.logical_or(i == 0, cls_ref[i] != cls_ref[jnp.maximum(i - 1, 0)])
    for t in range(3):
        @pl.when(jnp.logical_and(first, cls_ref[i] == t))
        def _():
            for hh in range(2):
                for a in range(NA_QROWS):
                    for j in range(NA_PAIRS):
                        r0, r1 = int(ro[t, a, 2 * j]), int(ro[t, a, 2 * j + 1])
                        x0 = jnp.broadcast_to(rp_ref[hh, r0:r0 + 1, :], (GRID_W, LANES))
                        x1 = jnp.broadcast_to(rp_ref[hh, r1:r1 + 1, :], (GRID_W, LANES))
                        blk = jnp.where(lo, pltpu.roll(x0, GRID_W + 1, 1, stride=1, stride_axis=0),
                                        pltpu.roll(x1, 1, 1, stride=1, stride_axis=0))
                        bias_scr[hh, a * GRID_W:(a + 1) * GRID_W, j * LANES:(j + 1) * LANES] = blk + cm_ref[...]
    return first


def _attn_fwd(qkv, qc0, kc0, vc0, npairs, table, kstart, cls, nk, *, name, na=None, qb=Q_BLOCK):
    s = qkv.shape[0]

    def body(ks_ref, cls_ref, q_ref, k_ref, v_ref, b_ref, *rest):
        if na:
            cm_ref, o_ref, lse_ref, bias_scr = rest
        else:
            o_ref, lse_ref = rest
        i = pl.program_id(1)
        if na:
            _na_build_bias(i, cls_ref, b_ref, cm_ref, bias_scr)
        ks = pl.multiple_of(ks_ref[i], 64)
        q2 = q_ref[...]
        k2 = k_ref[pl.ds(ks, nk), :]
        v2 = v_ref[pl.ds(ks, nk), :]
        lo = _head_lanes()
        outs, lses = [], []
        for h in range(2):
            qm = jnp.where(lo if h == 0 else jnp.logical_not(lo), q2, jnp.zeros_like(q2))
            sc = _dot_nt(qm, k2) * SCALE + (bias_scr[h, :, :nk] if na else b_ref[0, 0])
            m = jnp.max(sc, axis=1, keepdims=True)
            p = jnp.exp(sc - m)
            l = jnp.sum(p, axis=1, keepdims=True)
            pv = _dot(p.astype(BF16), v2)
            outs.append(pv / l)
            lses.append(m + jnp.log(l))
        o_ref[...] = jnp.where(lo, outs[0], outs[1])
        lse_ref[...] = jnp.where(lo, lses[0], lses[1])

    w = npairs * LANES
    in_specs = [
        pl.BlockSpec((qb, LANES), lambda p, i, ks, cl: (i, qc0 + p)),
        pl.BlockSpec((s, LANES), lambda p, i, ks, cl: (0, kc0 + p)),
        pl.BlockSpec((s, LANES), lambda p, i, ks, cl: (0, vc0 + p)),
    ]
    if na:
        in_specs += _na_bias_specs()
        args, scratch = (kstart, cls, qkv, qkv, qkv, *na), [pltpu.VMEM((2, Q_BLOCK, NA_W), F32)]
    else:
        in_specs.append(pl.BlockSpec((1, 1, qb, nk), lambda p, i, ks, cl: (cl[i], 0, 0, 0)))
        args, scratch = (kstart, cls, qkv, qkv, qkv, table), []
    grid_spec = pltpu.PrefetchScalarGridSpec(
        num_scalar_prefetch=2, grid=(npairs, s // qb), in_specs=in_specs,
        out_specs=[pl.BlockSpec((qb, LANES), lambda p, i, ks, cl: (i, p)),
                   pl.BlockSpec((qb, LANES), lambda p, i, ks, cl: (i, p))],
        scratch_shapes=scratch,
    )
    return _pcall(body, name=name, grid_spec=grid_spec,
                  out_shape=[jax.ShapeDtypeStruct((s, w), F32), jax.ShapeDtypeStruct((s, w), F32)],
                  compiler_params=pltpu.CompilerParams(dimension_semantics=("parallel", "arbitrary")),
                  )(*args)


def _na_bias_specs():
    return [pl.BlockSpec((2, 16, LANES), lambda p, i, ks, cl: (p, 0, 0)),
            pl.BlockSpec((GRID_W, LANES), lambda p, i, ks, cl: (0, 0))]


def _attn_bwd(qkv, qc0, kc0, vc0, npairs, table, kstart, cls, nk, do, o, lse, *, name, dlse=None, na=None,
              qb=Q_BLOCK, order=None):
    s = qkv.shape[0]
    has_dlse = dlse is not None
    _, slot = _na_class_tables()

    def body(ks_ref, cls_ref, q_ref, k_ref, v_ref, b_ref, *rest):
        if na:
            cm_ref, rest = rest[0], rest[1:]
        do_ref, o_ref, lse_ref, rest = rest[0], rest[1], rest[2], rest[3:]
        if has_dlse:
            dlse_ref, rest = rest[0], rest[1:]
        if order is not None:
            rest = rest[1:]
        dq_ref, dk_ref, dv_ref = rest[0], rest[1], rest[2]
        if na:
            bank_ref, bias_scr, dbias_scr, bank_scr = rest[3:]
        i = pl.program_id(1)
        if na:
            first = _na_build_bias(i, cls_ref, b_ref, cm_ref, bias_scr)

        @pl.when(i == 0)
        def _():
            dk_ref[...] = jnp.zeros_like(dk_ref)
            dv_ref[...] = jnp.zeros_like(dv_ref)
            if na:
                dbias_scr[...] = jnp.zeros_like(dbias_scr)
                bank_scr[...] = jnp.zeros_like(bank_scr)

        ks = pl.multiple_of(ks_ref[i], 64)
        q2 = q_ref[...]
        k2 = k_ref[pl.ds(ks, nk), :]
        v2 = v_ref[pl.ds(ks, nk), :]
        do2 = do_ref[...]
        lse2 = lse_ref[...]
        doo = do2 * o_ref[...]
        do2b = do2.astype(BF16)
        lo = _head_lanes()
        lane = lax.broadcasted_iota(jnp.int32, (1, LANES), 1)
        dqs, dks, dvs = [], [], []
        for h in range(2):
            mh = lo if h == 0 else jnp.logical_not(lo)
            qm = jnp.where(mh, q2, jnp.zeros_like(q2))
            sc = _dot_nt(qm, k2) * SCALE + (bias_scr[h, :, :nk] if na else b_ref[0, 0])
            lse_h = jnp.max(jnp.where(mh, lse2, NEG), axis=1, keepdims=True)
            p = jnp.exp(sc - lse_h)
            delta = jnp.sum(jnp.where(mh, doo, 0.0), axis=1, keepdims=True)
            dom = jnp.where(mh, do2b, jnp.zeros_like(do2b))
            dp = _dot_nt(dom, v2)
            t = dp - delta
            if has_dlse:
                t = t + jnp.sum(jnp.where(lane == h * HEAD_DIM, dlse_ref[...], 0.0), axis=1, keepdims=True)
            ds = p * t
            if na:
                @pl.when(first)
                def _():
                    dbias_scr[h, :, :nk] = ds

                @pl.when(jnp.logical_not(first))
                def _():
                    dbias_scr[h, :, :nk] += ds
            dsb = ds.astype(BF16)
            dqs.append(_dot(dsb, k2))
            dks.append(lax.dot_general(dsb, q2, _TN, preferred_element_type=F32))
            dvs.append(lax.dot_general(p.astype(BF16), do2b, _TN, preferred_element_type=F32))
        dq_ref[...] = jnp.where(lo, dqs[0], dqs[1]) * SCALE
        dk_ref[pl.ds(ks, nk), :] += jnp.where(lo, dks[0], dks[1]) * SCALE
        dv_ref[pl.ds(ks, nk), :] += jnp.where(lo, dvs[0], dvs[1])
        if na:
            last = jnp.logical_or(i == N_QBLK - 1, cls_ref[i] != cls_ref[jnp.minimum(i + 1, N_QBLK - 1)])
            for t in range(3):
                @pl.when(jnp.logical_and(last, cls_ref[i] == t))
                def _():
                    for hh in range(2):
                        for a in range(NA_QROWS):
                            for j in range(NA_PAIRS):
                                bank_scr[hh, int(slot[t, a, j])] += dbias_scr[
                                    hh, a * GRID_W:(a + 1) * GRID_W, j * LANES:(j + 1) * LANES]

            @pl.when(i == N_QBLK - 1)
            def _():
                bank_ref[...] = bank_scr[...]

    w = npairs * LANES
    blk = lambda: pl.BlockSpec((qb, LANES), lambda p, i, ks, cl: (i, p))
    full = lambda: pl.BlockSpec((s, LANES), lambda p, i, ks, cl: (0, p))
    in_specs = [
        pl.BlockSpec((qb, LANES), lambda p, i, ks, cl: (i, qc0 + p)),
        pl.BlockSpec((s, LANES), lambda p, i, ks, cl: (0, kc0 + p)),
        pl.BlockSpec((s, LANES), lambda p, i, ks, cl: (0, vc0 + p)),
    ]
    if na:
        in_specs += _na_bias_specs()
        args = [kstart, cls, qkv, qkv, qkv, *na]
    else:
        in_specs.append(pl.BlockSpec((1, 1, qb, nk), lambda p, i, ks, cl: (cl[i], 0, 0, 0)))
        args = [kstart, cls, qkv, qkv, qkv, table]
    in_specs += [blk(), blk(), blk()]
    args += [do, o, lse]
    if has_dlse:
        in_specs.append(blk())
        args.append(dlse)
    out_specs = [blk(), full(), full()]
    out_shape = [jax.ShapeDtypeStruct((s, w), F32)] * 3
    scratch = []
    if na:
        bank_shape = (2, NA_SLOTS, GRID_W, LANES)
        out_specs.append(pl.BlockSpec(bank_shape, lambda p, i, ks, cl: (p, 0, 0, 0)))
        out_shape.append(jax.ShapeDtypeStruct((2 * npairs,) + bank_shape[1:], F32))
        scratch = [pltpu.VMEM((2, Q_BLOCK, NA_W), F32), pltpu.VMEM((2, Q_BLOCK, NA_W), F32), pltpu.VMEM(bank_shape, F32)]
    if order is not None:
        in_specs.append(pl.BlockSpec(order.shape, lambda p, i, ks, cl: (0, 0)))
        args.append(order)
    grid_spec = pltpu.PrefetchScalarGridSpec(num_scalar_prefetch=2, grid=(npairs, s // qb), in_specs=in_specs,
                                             out_specs=out_specs, scratch_shapes=scratch)
    return _pcall(body, name=name, grid_spec=grid_spec, out_shape=out_shape,
                  compiler_params=pltpu.CompilerParams(dimension_semantics=("arbitrary", "arbitrary")))(*args)


_NA_CLASS_R0 = (0, NA_QROWS, ROWS - NA_QROWS)
_NA_CLASS_K0 = (0, 0, ROWS - NA_KROWS)
_RPB_RO = 2 * NA_KH - 1
_RPB_CO = 2 * NA_KW - 1
_BANK_ROWS = 48


def _na_constants():
    col = np.arange(GRID_W)
    cs = np.clip(col - NA_KW // 2, 0, GRID_W - NA_KW)
    vcol = (col[None, :] >= cs[:, None]) & (col[None, :] < cs[:, None] + NA_KW)
    colmask = np.where(np.concatenate([vcol, vcol], axis=1), 0.0, NEG).astype(np.float32)
    co = col[None, :] - col[:, None] + (NA_KW - 1)
    oh_col = np.zeros((GRID_W * GRID_W, LANES), np.float32)
    for qc in range(GRID_W):
        for kc in range(GRID_W):
            if vcol[qc, kc]:
                oh_col[qc * GRID_W + kc, co[qc, kc]] = 1.0
    ks = np.clip(np.arange(N_QBLK) * NA_QROWS - NA_KH // 2, 0, ROWS - NA_KROWS) * GRID_W
    cls = np.ones(N_QBLK, np.int32)
    cls[0], cls[-1] = 0, 2
    return colmask, oh_col, ks.astype(np.int32), cls


def _bank_reduce(bank, oh_col, *, name):
    def body(d_ref, ohc_ref, o_ref):
        o_ref[0] = jnp.dot(d_ref[0], ohc_ref[...], preferred_element_type=F32, precision=lax.Precision.HIGHEST)

    return _pcall(
        body, name=name, grid=(NA_HEADS,),
        in_specs=[pl.BlockSpec((1, _BANK_ROWS, GRID_W * GRID_W), lambda h: (h, 0, 0)),
                  pl.BlockSpec((GRID_W * GRID_W, LANES), lambda h: (0, 0))],
        out_specs=pl.BlockSpec((1, _BANK_ROWS, LANES), lambda h: (h, 0, 0)),
        out_shape=jax.ShapeDtypeStruct((NA_HEADS, _BANK_ROWS, LANES), F32), compiler_params=_PAR,
    )(bank, oh_col)


def _dil_constants(dilation):
    seg = SEQ // dilation
    nb = seg // DIL_QB
    nk = min(DIL_QB + 2 * DIL_HALF, seg)
    starts = [min(max(blk * DIL_QB - DIL_HALF, 0), seg - nk) for blk in range(nb)]
    shifts = sorted({w0 - blk * DIL_QB for blk, w0 in enumerate(starts)}, reverse=True)
    qi = np.arange(DIL_QB)[:, None]
    ki = np.arange(nk)[None, :]
    mask = np.stack([np.where(np.abs(ki + sh - qi) <= DIL_HALF, 0.0, NEG) for sh in shifts]).astype(np.float32)
    ks, cls = [], []
    for i in range(SEQ // DIL_QB):
        sub, blk = divmod(i, nb)
        cls.append(shifts.index(starts[blk] - blk * DIL_QB))
        ks.append(sub * seg + starts[blk])
    return mask.reshape(len(shifts), 1, DIL_QB, nk), np.asarray(ks, np.int32), np.asarray(cls, np.int32), nk


_VM = pl.BlockSpec(memory_space=pltpu.VMEM)


def _ada_fwd(c_all, w, b, *, name):
    def body(c_ref, w_ref, b_ref, o_ref):
        cv = c_ref[...]
        o_ref[...] = jnp.dot(cv * _sigmoid(cv), w_ref[...], preferred_element_type=F32,
                             precision=lax.Precision.HIGHEST) + b_ref[...]

    return _pcall(body, name=name, in_specs=[_VM, _VM, _VM], out_specs=_VM,
                  out_shape=jax.ShapeDtypeStruct((c_all.shape[0], w.shape[1]), F32))(c_all, w, b)


def _ada_bwd(c_all_t, dmod, *, name):
    def body(c_ref, d_ref, o_ref):
        cv = c_ref[...]
        o_ref[...] = jnp.dot(cv * _sigmoid(cv), d_ref[...], preferred_element_type=F32,
                             precision=lax.Precision.HIGHEST)

    return _pcall(body, name=name, in_specs=[_VM, _VM], out_specs=_VM,
                  out_shape=jax.ShapeDtypeStruct((c_all_t.shape[0], dmod.shape[1]), F32))(c_all_t, dmod)


def _row_sum(t, *, name):
    def body(t_ref, o_ref):
        o_ref[...] = jnp.sum(t_ref[...], axis=0, keepdims=True)

    return _pcall(body, name=name, in_specs=[_VM], out_specs=_VM,
                  out_shape=jax.ShapeDtypeStruct((1, t.shape[1]), F32))(t)


def _row_tile(rows):
    tr = rows
    for cand in range(8, 513, 8):
        if rows % cand == 0:
            tr = cand
    return tr


def _adamw_math(wv, gv, mv, vv):
    nm = ADAM_B1 * mv + (1.0 - ADAM_B1) * gv
    nv = ADAM_B2 * vv + (1.0 - ADAM_B2) * (gv * gv)
    m_hat = nm / (1.0 - ADAM_B1 ** ADAM_STEP)
    v_hat = nv / (1.0 - ADAM_B2 ** ADAM_STEP)
    return -ADAM_LR * (m_hat / (jnp.sqrt(v_hat) + ADAM_EPS) + ADAM_WD * wv), nm, nv


def _adamw(w, g, m, v, *, name):
    rows, cols = w.shape
    tr = _row_tile(rows)

    def body(w_ref, g_ref, m_ref, v_ref, d_ref, nm_ref, nv_ref):
        d_ref[...], nm_ref[...], nv_ref[...] = _adamw_math(w_ref[...], g_ref[...], m_ref[...], v_ref[...])

    spec = pl.BlockSpec((tr, cols), lambda i: (i, 0))
    return _pcall(body, name=name, grid=(rows // tr,), in_specs=[spec] * 4, out_specs=[spec] * 3,
                  out_shape=[jax.ShapeDtypeStruct((rows, cols), F32)] * 3, compiler_params=_PAR)(w, g, m, v)


def _adamw_halves(w, g_mine, g_other, m, v, c_arr, *, name):
    rows, cols = w.shape
    hr = rows // 2
    tr = _row_tile(hr)
    nt = hr // tr

    def body(c_ref, w_ref, t_ref, o_ref, m_ref, v_ref, g_ref, d_ref, nm_ref, nv_ref):
        gv = jnp.where(pl.program_id(0) == c_ref[0], t_ref[...], o_ref[...])
        g_ref[...] = gv
        d_ref[...], nm_ref[...], nv_ref[...] = _adamw_math(w_ref[...], gv, m_ref[...], v_ref[...])

    full = pl.BlockSpec((tr, cols), lambda h, i, c: (h * nt + i, 0))
    half = pl.BlockSpec((tr, cols), lambda h, i, c: (i, 0))
    grid_spec = pltpu.PrefetchScalarGridSpec(num_scalar_prefetch=1, grid=(2, nt),
                                             in_specs=[full, half, half, full, full], out_specs=[full] * 4)
    return _pcall(body, name=name, grid_spec=grid_spec, out_shape=[jax.ShapeDtypeStruct((rows, cols), F32)] * 4,
                  compiler_params=pltpu.CompilerParams(dimension_semantics=("parallel", "parallel")),
                  )(c_arr, w, g_mine, g_other, m, v)


_MESH = pl.DeviceIdType.MESH
_ANY = pl.BlockSpec(memory_space=pl.ANY)
_CHIP_FLIPS = ((1, 0), (0, 1), (1, 1))


def _pos():
    return lax.axis_index("x"), lax.axis_index("y"), lax.axis_index("c")


def _flip(v, f):
    return 1 - v if f else v


def _sem_pairs(n):
    return [pltpu.SemaphoreType.DMA((n,)), pltpu.SemaphoreType.DMA((n,))]


def _small_allgather(blk, *, name):
    m_per, n = blk.shape

    def body(x_ref, out_ref, send_sems, recv_sems, local_sem):
        x, y, c = _pos()
        me, sibling = (x, y, c), (x, y, 1 - c)
        chips = [(_flip(x, fx), _flip(y, fy)) for fx, fy in _CHIP_FLIPS]

        def rows(px, py, pc):
            return out_ref.at[pl.ds((4 * px + 2 * py + pc) * m_per, m_per), :]

        def copy(k, block, to, src=None):
            return pltpu.make_async_remote_copy(
                src_ref=rows(*block) if src is None else src, dst_ref=rows(*block),
                send_sem=send_sems.at[k], recv_sem=recv_sems.at[k], device_id=to, device_id_type=_MESH)

        mine = pltpu.make_async_copy(x_ref, rows(*me), local_sem)
        mine.start()
        first = [copy(0, me, sibling, src=x_ref)]
        first += [copy(1 + j, me, (*chip, c), src=x_ref) for j, chip in enumerate(chips)]
        for cp in first:
            cp.start()
        passed = [copy(4 + j, (*chip, c), sibling) for j, chip in enumerate(chips)]
        for j, chip in enumerate(chips):
            copy(1 + j, (*chip, c), me).wait_recv()
            passed[j].start()
        copy(0, sibling, me).wait_recv()
        for j, chip in enumerate(chips):
            copy(4 + j, (*chip, 1 - c), me).wait_recv()
        for cp in first + passed:
            cp.wait_send()
        mine.wait()

    return _pcall(
        body, name=name, out_shape=jax.ShapeDtypeStruct((N_DEV * m_per, n), blk.dtype),
        in_specs=[_VM], out_specs=_VM,
        scratch_shapes=_sem_pairs(7) + [pltpu.SemaphoreType.DMA],
    )(blk)


_HBM = pl.BlockSpec(memory_space=pltpu.HBM)
_SEM = pl.BlockSpec(memory_space=pltpu.SEMAPHORE)
_EFFECT = pltpu.SideEffectType.DATAFLOW_SIDE_EFFECTING


def _split_start(srcs, lands, plan, ncopies, after, *, name):
    ns, nl = len(srcs), len(lands)

    def body(*refs):
        src_refs, land_refs = refs[:ns], refs[ns:ns + nl]
        send_sems, recv_sems = refs[ns + nl + 1], refs[ns + nl + 2]
        token = refs[-1]
        x, y, c = _pos()
        for k, (src, dst, to, _) in enumerate(plan(x, y, c, src_refs, land_refs)):
            pltpu.make_async_remote_copy(src_ref=src, dst_ref=dst, send_sem=send_sems.at[k], recv_sem=recv_sems.at[k],
                                         device_id=to, device_id_type=_MESH).start()
        token[...] = jnp.zeros_like(token)

    hbm = lambda a: pltpu.HBM(a.shape, a.dtype)
    out = _pcall(
        body, name=name,
        out_shape=(pltpu.SemaphoreType.DMA((ncopies,)), pltpu.SemaphoreType.DMA((ncopies,)),
                   *[hbm(a) for a in srcs], *[hbm(a) for a in lands], jax.ShapeDtypeStruct((8, LANES), F32)),
        in_specs=[_HBM] * (ns + nl) + [_ANY], out_specs=(_SEM, _SEM, *[_HBM] * (ns + nl), _VM),
        input_output_aliases={i: 2 + i for i in range(ns + nl)},
        compiler_params=pltpu.CompilerParams(has_side_effects=_EFFECT),
    )(*[pltpu.with_memory_space_constraint(a, pltpu.HBM) for a in (*srcs, *lands)], after)
    return out[0], out[1], list(out[2:2 + ns]), list(out[2 + ns:2 + ns + nl]), out[-1]


def _split_wait(send_sems, recv_sems, srcs, lands, plan, after, *, name, with_sources=False):
    ns, nl = len(srcs), len(lands)

    def body(*refs):
        src_refs, land_refs = refs[:ns], refs[ns:ns + nl]
        send_sems, recv_sems = refs[ns + nl], refs[ns + nl + 1]
        x, y, c = _pos()
        for k, (src, _, _, mine) in enumerate(plan(x, y, c, src_refs, land_refs)):
            cp = pltpu.make_async_remote_copy(src_ref=src, dst_ref=mine, send_sem=send_sems.at[k],
                                              recv_sem=recv_sems.at[k], device_id=(x, y, c), device_id_type=_MESH)
            cp.wait_send()
            cp.wait_recv()

    hbm = lambda a: pltpu.HBM(a.shape, a.dtype)
    out = _pcall(
        body, name=name, out_shape=tuple(hbm(a) for a in (*srcs, *lands)),
        in_specs=[_HBM] * (ns + nl) + [_SEM, _SEM, _ANY], out_specs=tuple([_HBM] * (ns + nl)),
        input_output_aliases={i: i for i in range(ns + nl)},
        compiler_params=pltpu.CompilerParams(has_side_effects=_EFFECT),
    )(*srcs, *lands, send_sems, recv_sems, after)
    return (list(out[:ns]), list(out[ns:])) if with_sources else list(out[ns:])


def _ag_plan(nw):
    def plan(x, y, c, sh_refs, full_refs):
        j = 2 * x + y
        out = []
        for wi in range(nw):
            for fx, fy in _CHIP_FLIPS:
                px, py = _flip(x, fx), _flip(y, fy)
                out.append((sh_refs[wi].at[c], full_refs[wi].at[j, c], (px, py, c), full_refs[wi].at[2 * px + py, c]))
            out.append((sh_refs[wi], full_refs[wi].at[j], (x, y, 1 - c), full_refs[wi].at[j]))
        return out
    return plan


def _ag_plan_direct(nw):
    def plan(x, y, c, sh_refs, full_refs):
        j = 2 * x + y
        out = []
        for wi in range(nw):
            for fx, fy in _CHIP_FLIPS:
                px, py = _flip(x, fx), _flip(y, fy)
                for rel in (0, 1):
                    t = _flip(c, rel)
                    out.append((sh_refs[wi].at[c], full_refs[wi].at[j, c], (px, py, t), full_refs[wi].at[2 * px + py, t]))
            out.append((sh_refs[wi], full_refs[wi].at[j], (x, y, 1 - c), full_refs[wi].at[j]))
        return out
    return plan


def _ag_pass(fulls, *, name):
    nw = len(fulls)

    def body(*refs):
        in_refs, out_refs = refs[:nw], refs[nw:2 * nw]
        send_sems, recv_sems = refs[2 * nw:]
        x, y, c = _pos()
        cps = []
        for wi in range(nw):
            for k, (fx, fy) in enumerate(_CHIP_FLIPS):
                jp = 2 * _flip(x, fx) + _flip(y, fy)
                sems = dict(send_sem=send_sems.at[3 * wi + k], recv_sem=recv_sems.at[3 * wi + k], device_id_type=_MESH)
                send = pltpu.make_async_remote_copy(src_ref=in_refs[wi].at[jp, c], dst_ref=out_refs[wi].at[jp, c],
                                                    device_id=(x, y, 1 - c), **sems)
                recv = pltpu.make_async_remote_copy(src_ref=in_refs[wi].at[jp, c], dst_ref=out_refs[wi].at[jp, 1 - c],
                                                    device_id=(x, y, c), **sems)
                cps.append((send, recv))
        for send, _ in cps:
            send.start()
        for send, recv in cps:
            send.wait_send()
            recv.wait_recv()

    return _pcall(body, name=name, out_shape=[jax.ShapeDtypeStruct(f.shape, f.dtype) for f in fulls],
                  in_specs=[_ANY] * nw, out_specs=[_ANY] * nw, input_output_aliases={i: i for i in range(nw)},
                  scratch_shapes=_sem_pairs(3 * nw))(*fulls)


def _sib_plan(nw):
    def plan(x, y, c, g_refs, ra_refs):
        return [(g_refs[wi].at[k, 1 - c], ra_refs[wi].at[k], (x, y, 1 - c), ra_refs[wi].at[k])
                for wi in range(nw) for k in range(N_CHIP)]
    return plan


def _rs_plan(nw):
    def plan(x, y, c, s_refs, rb_refs):
        out = []
        for wi in range(nw):
            for k, (fx, fy) in enumerate(_CHIP_FLIPS):
                px, py = _flip(x, fx), _flip(y, fy)
                out.append((s_refs[wi].at[2 * px + py], rb_refs[wi].at[k], (px, py, c), rb_refs[wi].at[k]))
        return out
    return plan


def _sibling_swap(ts, *, name):
    nw = len(ts)

    def body(*refs):
        t_refs, out_refs = refs[:nw], refs[nw:2 * nw]
        send_sems, recv_sems = refs[2 * nw:]
        x, y, c = _pos()
        cps = [pltpu.make_async_remote_copy(src_ref=t_refs[wi], dst_ref=out_refs[wi], send_sem=send_sems.at[wi],
                                            recv_sem=recv_sems.at[wi], device_id=(x, y, 1 - c), device_id_type=_MESH)
               for wi in range(nw)]
        for cp in cps:
            cp.start()
        for cp in cps:
            cp.wait()

    return _pcall(body, name=name, out_shape=[jax.ShapeDtypeStruct(t.shape, t.dtype) for t in ts],
                  in_specs=[_ANY] * nw, out_specs=[_ANY] * nw, scratch_shapes=_sem_pairs(nw))(*ts)


def _rs_add(g, ra, c_arr, *, name):
    n, _, r, w = g.shape

    def body(c_ref, g_ref, ra_ref, s_ref, sb_ref):
        t = g_ref[...] + ra_ref[...]
        s_ref[...] = t
        sb_ref[...] = t.astype(BF16)

    grid_spec = pltpu.PrefetchScalarGridSpec(
        num_scalar_prefetch=1, grid=(n,),
        in_specs=[pl.BlockSpec((None, None, r, w), lambda k, c: (k, c[0], 0, 0)),
                  pl.BlockSpec((None, r, w), lambda k, c: (k, 0, 0))],
        out_specs=[pl.BlockSpec((None, r, w), lambda k, c: (k, 0, 0))] * 2)
    return _pcall(body, name=name, grid_spec=grid_spec,
                  out_shape=[jax.ShapeDtypeStruct((n, r, w), F32), jax.ShapeDtypeStruct((n, r, w), BF16)],
                  compiler_params=_PAR)(c_arr, g, ra)


def _rs_final(s, rb, j_arr, *, name):
    _, r, w = s.shape

    def body(j_ref, s_ref, rb_ref, t_ref):
        t_ref[...] = ((s_ref[...] + rb_ref[0].astype(F32)) + rb_ref[1].astype(F32)) + rb_ref[2].astype(F32)

    grid_spec = pltpu.PrefetchScalarGridSpec(
        num_scalar_prefetch=1, grid=(1,),
        in_specs=[pl.BlockSpec((None, r, w), lambda i, j: (j[0], 0, 0)),
                  pl.BlockSpec((3, r, w), lambda i, j: (0, 0, 0))],
        out_specs=pl.BlockSpec((r, w), lambda i, j: (0, 0)))
    return _pcall(body, name=name, grid_spec=grid_spec, out_shape=jax.ShapeDtypeStruct((r, w), F32),
                  compiler_params=_ARB)(j_arr, s, rb)


def _tile2(g):
    return jnp.concatenate([g, g], axis=1)


_BIG = ("w_in", "w_ffn_in", "w_ffn_out", "w_o", "w_proj_a", "w_proj_b")
_BIG_SHARD = {"w_in": (1024, 1472), "w_ffn_in": (1024, 1408), "w_ffn_out": (704, 1024), "w_o": (256, 1024),
              "w_proj_a": (512, 256), "w_proj_b": (256, 256)}


def _device_step(x2, tgt, mod, first_weights, late_weights, early_grads, mid_grads, g_norm1, g_norm2, b_gate, g_qa, g_ka,
                 g_qb, g_kb, rpb):
    d = D_MODEL
    sh1, sc1, gt1, sh2, sc2, gt2 = [mod[:, k * d:(k + 1) * d] for k in range(6)]

    colmask, oh_col, na_ks, na_cls = _na_constants()
    rp = jnp.pad(rpb, ((0, 0), (0, 16 - _RPB_RO), (RP_LANE0, LANES - RP_LANE0 - _RPB_CO)), constant_values=NEG)
    na = (rp, jnp.asarray(colmask))
    na_ks, na_cls = jnp.asarray(na_ks), jnp.asarray(na_cls)
    dil = [_dil_constants(dd) for _, dd in DIL_CONFIGS]
    gains = jnp.concatenate([_tile2(g_qa), _tile2(g_ka), _tile2(g_qb), _tile2(g_kb)], axis=0)
    cos_t, sa_t, sb_t = _rope_tables()

    wts = first_weights(cos_t)
    h1, qkvn, qk_pre, gates, *qkv_dil = _pre_attn_fwd(x2, cos_t, sa_t, sb_t, g_norm1, sc1, sh1, wts["w_qkv"],
                                                      wts["w_gates"], gains, name="pre_attn_fwd")
    o_a, lse_a = _attn_fwd(qkvn, 0, 4, 8, 4, None, na_ks, na_cls, NA_NK, name="attn_a_fwd", na=na)
    arrs, o_g, l_g = [], [], []
    res = lambda t, dd: t if dd == 1 else (t, dd)
    for g, (_, dd) in enumerate(DIL_CONFIGS):
        tab_g, ks_g, cls_g, nk_g = jnp.asarray(dil[g][0]), jnp.asarray(dil[g][1]), jnp.asarray(dil[g][2]), dil[g][3]
        arr, cb = (qkvn, (12, 18, 24)) if dd == 1 else (qkv_dil.pop(0), (0, 2, 4))
        op, lp = _attn_fwd(arr, cb[0], cb[1], cb[2], 2, tab_g, ks_g, cls_g, nk_g, name=f"attn_d{g}_fwd", qb=DIL_QB)
        arrs.append((arr, cb, tab_g, ks_g, cls_g, nk_g))
        o_g.append(res(op, dd))
        l_g.append(res(lp, dd))
    wts = dict(wts, **late_weights(o_a))
    o_b, merged, mo, x1, h2 = _post_attn_fwd(o_a, o_g, l_g, gates, x2, wts["w_pa"], wts["w_pb"], wts["w_o"], b_gate,
                                             gt1, g_norm2, sc2, sh2, name="post_attn_fwd")
    act, ff = _ffn_fwd(h2, wts["w_ffn_in"], name="ffn_fwd")

    dy, dffo, dff, dgt2, loss_v = _ffn_mid(act, ff, x1, tgt, wts["w_ffn_out"], gt2, name="ffn_mid")
    grads = {}
    g_ffn_out = _wgrad(act, dffo, name="wg_ffn_out", tm=D_FF // 2, tn=d, tk=2048)
    grads["w_ffn_out"] = g_ffn_out.reshape(N_CHIP, D_FF // N_CHIP, d)
    grads["w_ffn_in"] = _wgrad(h2, dff, name="wg_ffn_in", tm=512, tn=2 * FF_CHIP, tk=2048, chips=N_CHIP)
    dx1, dmo, sums2 = _ffn_in_bwd(dff, x1, dy, mo, wts["w_ffn_in"], g_norm2, sc2, gt1, name="ffn_in_bwd")
    grads["w_o"] = _wgrad(merged, dmo, name="wg_o", tm=d, tn=d, tk=2048).reshape(N_CHIP, d // N_CHIP, d)
    pab = _post_attn_bwd(dmo, gates, o_a, o_g, l_g, wts["w_pa"], wts["w_pb"], wts["w_o"], b_gate, name="post_attn_bwd")
    dpa, dpb, dgates, do_a = pab[:4]
    do_g, dl_g, dbg = pab[4:7], pab[7:10], pab[10]
    g_pa = _wgrad(o_a, dpa, name="wg_pa", tm=WA, tn=d)
    g_pb = _wgrad(o_b, dpb, name="wg_pb", tm=WB_OUT, tn=d)
    grads["w_proj_a"] = g_pa.reshape(WA, N_CHIP, d // N_CHIP).transpose(1, 0, 2)
    grads["w_proj_b"] = g_pb.reshape(WB_OUT, N_CHIP, d // N_CHIP).transpose(1, 0, 2)
    order = early_grads(grads)
    dqs, dks, dvs = [], [], []
    for g, (_, dd) in enumerate(DIL_CONFIGS):
        arr, cb, tab_g, ks_g, cls_g, nk_g = arrs[g]
        plain = lambda t: t[0] if isinstance(t, tuple) else t
        dq, dk, dv = _attn_bwd(arr, cb[0], cb[1], cb[2], 2, tab_g, ks_g, cls_g, nk_g, do_g[g], plain(o_g[g]),
                               plain(l_g[g]), name=f"attn_d{g}_bwd", dlse=dl_g[g], qb=DIL_QB, order=order)
        dqs.append(res(dq, dd))
        dks.append(res(dk, dd))
        dvs.append(res(dv, dd))
    order = mid_grads(dv)
    dqa, dka, dva, bank = _attn_bwd(qkvn, 0, 4, 8, 4, None, na_ks, na_cls, NA_NK, do_a, o_a, lse_a,
                                    name="attn_a_bwd", na=na, order=order)
    dqkv, grad_x, dgains, sums1 = _pre_attn_bwd(qk_pre, [dqa, dka, dva] + dqs + dks + dvs, dgates, x2, dx1, cos_t, sa_t,
                                                sb_t, wts["w_qkv"], wts["w_gates"], gains, g_norm1, sc1,
                                                name="pre_attn_bwd")
    g_qkv = _wgrad(h1, dqkv, name="wg_qkv", tm=d, tn=W_QKV // 2)
    g_gates = _wgrad(h1, dgates, name="wg_gates", tm=d, tn=W_GATES)
    nc, cut = _BIG_SHARD["w_in"][1], 3 * _BIG_SHARD["w_in"][1] - W_QKV
    grads["w_in"] = jnp.stack([g_qkv[:, :nc], g_qkv[:, nc:2 * nc],
                               jnp.concatenate([g_qkv[:, 2 * nc:], g_gates[:, :cut]], axis=1), g_gates[:, cut:]])

    bank = bank.reshape(NA_HEADS, NA_SLOTS, GRID_W, 2, GRID_W).transpose(0, 1, 3, 2, 4)
    bank = jnp.pad(bank.reshape(NA_HEADS, 2 * NA_SLOTS, GRID_W * GRID_W), ((0, 0), (0, _BANK_ROWS - 2 * NA_SLOTS), (0, 0)))
    g2 = _bank_reduce(bank, jnp.asarray(oh_col), name="rpb_reduce")[:, :2 * NA_SLOTS].reshape(NA_HEADS, NA_SLOTS, 2, LANES)
    g_rpb = g2[:, 3:3 + _RPB_RO, 0, :_RPB_CO] + g2[:, 2:2 + _RPB_RO, 1, :_RPB_CO]

    dmod = jnp.concatenate([sums1[0:1], sums1[1:2], sums2[3:4], sums2[0:1], sums2[1:2], dgt2], axis=1)
    small = dict(g_norm1=sums1[2:3], g_norm2=sums2[2:3], b_gate=dbg, g_qa=dgains[0:1, :HEAD_DIM],
                 g_ka=dgains[1:2, :HEAD_DIM], g_qb=dgains[2:3, :HEAD_DIM], g_kb=dgains[3:4, :HEAD_DIM], rpb=g_rpb)
    return loss_v, grad_x, grads, dmod, small


_SMALL = ("b_ada", "g_norm1", "g_norm2", "b_gate", "g_qa", "g_ka", "g_qb", "g_kb", "rpb")
_SMALL_N = {"b_ada": 6 * D_MODEL, "g_norm1": D_MODEL, "g_norm2": D_MODEL, "b_gate": 2 * D_MODEL, "g_qa": HEAD_DIM,
            "g_ka": HEAD_DIM, "g_qb": HEAD_DIM, "g_kb": HEAD_DIM, "rpb": NA_HEADS * _RPB_RO * _RPB_CO}


def _pack_small(parts):
    flat = [parts[n].reshape(1, _SMALL_N[n]) for n in _SMALL]
    used = sum(_SMALL_N.values())
    return jnp.concatenate(flat + [jnp.zeros((1, STATS_W - used), F32)], axis=1)


def _unpack_small(v, shapes):
    out, at = {}, 0
    for n in _SMALL:
        out[n] = v[:, at:at + _SMALL_N[n]].reshape(shapes[n])
        at += _SMALL_N[n]
    return out


def _join_cols(t):
    _, r, c = t.shape
    return t.transpose(1, 0, 2).reshape(r, N_CHIP * c)


def kernel(x, c, w_ada, b_ada, g_norm1, g_norm2, w_in, b_gate, g_qa, g_ka, g_qb, g_kb, rpb, w_proj_a, w_proj_b, w_o, w_ffn_in, w_ffn_out, loss_target, m_w_ada, m_b_ada, m_g_norm1, m_g_norm2, m_w_in, m_b_gate, m_g_qa, m_g_ka, m_g_qb, m_g_kb, m_rpb, m_w_proj_a, m_w_proj_b, m_w_o, m_w_ffn_in, m_w_ffn_out, v_w_ada, v_b_ada, v_g_norm1, v_g_norm2, v_w_in, v_b_gate, v_g_qa, v_g_ka, v_g_qb, v_g_kb, v_rpb, v_w_proj_a, v_w_proj_b, v_w_o, v_w_ffn_in, v_w_ffn_out):
    names = ("w_ada", "b_ada", "g_norm1", "g_norm2", "w_in", "b_gate", "g_qa", "g_ka", "g_qb", "g_kb", "rpb",
             "w_proj_a", "w_proj_b", "w_o", "w_ffn_in", "w_ffn_out")
    w = dict(zip(names, (w_ada, b_ada, g_norm1, g_norm2, w_in, b_gate, g_qa, g_ka, g_qb, g_kb, rpb, w_proj_a, w_proj_b,
                         w_o, w_ffn_in, w_ffn_out)))
    m = dict(zip(names, (m_w_ada, m_b_ada, m_g_norm1, m_g_norm2, m_w_in, m_b_gate, m_g_qa, m_g_ka, m_g_qb, m_g_kb, m_rpb,
                         m_w_proj_a, m_w_proj_b, m_w_o, m_w_ffn_in, m_w_ffn_out)))
    v = dict(zip(names, (v_w_ada, v_b_ada, v_g_norm1, v_g_norm2, v_w_in, v_b_gate, v_g_qa, v_g_ka, v_g_qb, v_g_kb, v_rpb,
                         v_w_proj_a, v_w_proj_b, v_w_o, v_w_ffn_in, v_w_ffn_out)))
    d = D_MODEL
    xi, yi, ci = _pos()
    chip = 2 * xi + yi
    me = 2 * chip + ci
    ada_cols = 6 * d // N_CHIP

    c_arr, chip_arr = ci.reshape(1).astype(jnp.int32), chip.reshape(1).astype(jnp.int32)
    first, rest = _BIG[:1], _BIG[1:]

    c_all = _small_allgather(c.reshape(8, d // 8), name="ag_c").reshape(N_DEV, d)
    b_sh = lax.dynamic_slice(b_ada, (0, chip * ada_cols), (1, ada_cols))
    mod_part = _ada_fwd(c_all, w_ada[0], b_sh, name="ada_fwd")
    mod_all = _small_allgather(mod_part, name="ag_mod").reshape(N_CHIP, 2, 8, ada_cols)[:, 0]
    mod = lax.dynamic_index_in_dim(mod_all, me, axis=1, keepdims=False).reshape(1, 6 * d)

    halves = {n: (2, _BIG_SHARD[n][0] // 2, _BIG_SHARD[n][1]) for n in _BIG}
    shards = {n: w[n][0].astype(BF16).reshape(halves[n]) for n in _BIG}
    land = lambda n: lax.empty((N_CHIP,) + halves[n], BF16)
    ag1 = _split_start([shards[n] for n in first], [land(n) for n in first], _ag_plan(1), 4, mod, name="ag1_start")
    ag2 = _split_start([shards[n] for n in rest], [land(n) for n in rest], _ag_plan_direct(len(rest)), 7 * len(rest),
                       ag1[4], name="ag2_start")
    rpb_after = rpb[0] + ag2[4][0, 0]

    def first_weights(after):
        after = after[:1, :1] + ag2[4][:1, :1]
        full1 = _split_wait(ag1[0], ag1[1], ag1[2], ag1[3], _ag_plan(1), after, name="ag1_wait")
        p_in = _ag_pass(full1, name="ag1_pass")[0].reshape((N_CHIP,) + _BIG_SHARD["w_in"])
        cut = W_QKV - 2 * _BIG_SHARD["w_in"][1]
        return dict(w_qkv=jnp.concatenate([p_in[0], p_in[1], p_in[2][:, :cut]], axis=1),
                    w_gates=jnp.concatenate([p_in[2][:, cut:], p_in[3]], axis=1))

    def late_weights(after):
        full2 = _split_wait(ag2[0], ag2[1], ag2[2], ag2[3], _ag_plan_direct(len(rest)), after, name="ag2_wait")
        full ={n: fu.reshape((N_CHIP,) + _BIG_SHARD[n]) for n, fu in zip(rest, full2)}
        return dict(w_pa=_join_cols(full["w_proj_a"]), w_pb=_join_cols(full["w_proj_b"]), w_o=full["w_o"].reshape(d, d),
                    w_ffn_in=_join_cols(full["w_ffn_in"]), w_ffn_out=full["w_ffn_out"].reshape(D_FF, d))

    def sib_begin(group, grads, tag):
        gps = [grads[n].reshape((N_CHIP,) + halves[n]) for n in group]
        lands = [lax.empty((N_CHIP,) + halves[n][1:], F32) for n in group]
        return _split_start(gps, lands, _sib_plan(len(group)), N_CHIP * len(group), gps[0], name=f"rs_sib_{tag}_start")

    def rs_begin(group, sib, after, tag):
        gps, ras = _split_wait(sib[0], sib[1], sib[2], sib[3], _sib_plan(len(group)), after,
                               name=f"rs_sib_{tag}_wait", with_sources=True)
        sums = [_rs_add(gp, ra, c_arr, name=f"rs_add_{n}") for n, gp, ra in zip(group, gps, ras)]
        lands = [lax.empty((3,) + halves[n][1:], BF16) for n in group]
        st = _split_start([sb for _, sb in sums], lands, _rs_plan(len(group)), 3 * len(group), sums[0][0],
                          name=f"rs_{tag}_start")
        return sums, st

    def rs_end(group, begun, after, tag):
        sums, st = begun
        rbs = _split_wait(st[0], st[1], st[2], st[3], _rs_plan(len(group)), after, name=f"rs_{tag}_wait")
        return [_rs_final(sf, rb, chip_arr, name=f"rs_final_{n}") for n, (sf, _), rb in zip(group, sums, rbs)]

    begun = {}

    def early_grads(grads):
        begun["sib_rest"] = sib_begin(rest, grads, "rest")
        return begun["sib_rest"][4]

    def mid_grads(after):
        begun["rest"] = rs_begin(rest, begun["sib_rest"], after, "rest")
        return begun["rest"][1][4]

    loss_v, grad_x, grads, dmod, small = _device_step(
        x[0], loss_target[0], mod, first_weights, late_weights, early_grads, mid_grads, g_norm1, g_norm2, b_gate, g_qa,
        g_ka, g_qb, g_kb, rpb_after)
    sib_first = sib_begin(first, grads, "first")

    g, delta, new_m, new_v = {}, {}, {}, {}

    def finish(group, ts, tag):
        others = _sibling_swap(ts, name=f"rs_pair_{tag}")
        for n, t, o in zip(group, ts, others):
            gg, dl, nm, nv = _adamw_halves(w[n][0], t, o, m[n][0], v[n][0], c_arr, name=f"adamw_{n}")
            g[n], delta[n], new_m[n], new_v[n] = gg[None], dl[None], nm[None], nv[None]

    finish(rest, rs_end(rest, begun["rest"], sib_first[4], "rest"), "rest")
    done_rest = sum(new_v[n][0, :1, :1] for n in rest)
    begun["first"] = rs_begin(first, sib_first, done_rest, "first")

    stats = _pack_small(dict(b_ada=dmod, **small)) + begun["first"][1][4][0, 0]
    stats = stats.at[:, STATS_W - 1].set(loss_v[0, 0])
    rows = _small_allgather(stats.reshape(8, STATS_W // 8), name="ag_stats").reshape(N_DEV, STATS_W)
    dmod_sh = lax.dynamic_slice(rows, (0, chip * ada_cols), (8, ada_cols))
    g_ada = _ada_bwd(c_all.T, dmod_sh, name="ada_bwd")
    tot = _row_sum(rows, name="stats_sum")
    g_small = _unpack_small(tot, {n: w[n].shape for n in _SMALL})

    finish(first, rs_end(first, begun["first"], tot, "first"), "first")

    dl, nm, nv = _adamw(w_ada[0], g_ada, m_w_ada[0], v_w_ada[0], name="adamw_w_ada")
    g["w_ada"], delta["w_ada"], new_m["w_ada"], new_v["w_ada"] = g_ada[None], dl[None], nm[None], nv[None]
    shapes = {n: w[n].shape for n in _SMALL}
    dl, nm, nv = _adamw(_pack_small({n: w[n] for n in _SMALL}), tot, _pack_small({n: m[n] for n in _SMALL}),
                        _pack_small({n: v[n] for n in _SMALL}), name="adamw_small")
    delta.update(_unpack_small(dl, shapes))
    new_m.update(_unpack_small(nm, shapes))
    new_v.update(_unpack_small(nv, shapes))
    g.update(g_small)

    loss = tot[0, STATS_W - 1]
    return (loss, grad_x[None], *[g[n] for n in names], *[delta[n] for n in names], *[new_m[n] for n in names],
            *[new_v[n] for n in names])
```

```python
import numpy as np

import jax
import jax.numpy as jnp
from jax import lax
from jax.experimental import pallas as pl
from jax.experimental.pallas import tpu as pltpu

F32 = jnp.float32
BF16 = jnp.bfloat16

D_MODEL = 1024
SEQ = 8192
HEAD_DIM = 64
GRID_W = 64
ROWS = SEQ // GRID_W
NA_HEADS = 8
NA_KH = 8
NA_KW = 16
DIL_CONFIGS = ((128, 1), (512, 4), (2048, 16))
ROT_DIM = 16
ROPE_THETA = 500000.0
D_FF = 2816
EPS = 1e-6
NEG = -1e30
WA = 512
WB = 768
WB_OUT = 256
W_QKV = 3 * WA + 3 * WB
W_QK = 2 * WA + 2 * WB
W_GATES = 2 * D_MODEL
SCALE = HEAD_DIM ** -0.5

ADAM_LR = 0.001
ADAM_B1 = 0.9
ADAM_B2 = 0.999
ADAM_EPS = 1e-08
ADAM_WD = 0.01
ADAM_STEP = 10

LANES = 128
ROW_TILE = 256
ROW_TILES = {"ffn_fwd": 512, "post_attn_fwd": 512, "post_attn_bwd": 512,
             "ffn_in_bwd": 512}
Q_BLOCK = 256
NA_QROWS = Q_BLOCK // GRID_W
NA_KROWS = NA_QROWS + NA_KH - 1
NA_NK = NA_KROWS * GRID_W
NA_PAIRS = (NA_KROWS + 1) // 2
NA_W = NA_PAIRS * LANES
NA_RO_NONE = 15
NA_SLOTS = 21
RP_LANE0 = GRID_W - NA_KW
DIL_HALF = 64
DIL_QB = 512
N_QBLK = SEQ // Q_BLOCK

N_DEV = 8
N_CHIP = 4
FF_CHIP = 2 * D_FF // N_CHIP
STATS_W = 14336


def _pcall(body, *, name, **kw):
    return pl.pallas_call(body, name=name, **kw)


_NT = (((1,), (1,)), ((), ()))
_TN = (((0,), (0,)), ((), ()))
_ARB = pltpu.CompilerParams(dimension_semantics=("arbitrary",))
_PAR = pltpu.CompilerParams(dimension_semantics=("parallel",))


def _dot(a, b):
    return jnp.dot(a, b, preferred_element_type=F32)


def _dot_nt(a, b):
    return lax.dot_general(a, b, _NT, preferred_element_type=F32)


def _wgrad(a, b, *, name, tm, tn, tk=1024, chips=None):
    s, ma = a.shape
    nb = b.shape[1]
    nk = s // tk
    nc = nb // chips if chips else tn
    cpb = tn // nc

    def body(a_ref, b_ref, o_ref, acc):
        k = pl.program_id(2)
        r = lax.dot_general(a_ref[...].astype(BF16), b_ref[...].astype(BF16), _TN, preferred_element_type=F32)

        @pl.when(k == 0)
        def _():
            acc[...] = r

        @pl.when(k > 0)
        def _():
            acc[...] += r

        @pl.when(k == nk - 1)
        def _():
            if chips:
                for q in range(cpb):
                    o_ref[q] = acc[:, q * nc:(q + 1) * nc]
            else:
                o_ref[...] = acc[...]

    if chips:
        o_spec = pl.BlockSpec((cpb, tm, nc), lambda i, j, k: (j, i, 0))
        out_shape = jax.ShapeDtypeStruct((chips, ma, nc), F32)
    else:
        o_spec = pl.BlockSpec((tm, tn), lambda i, j, k: (i, j))
        out_shape = jax.ShapeDtypeStruct((ma, nb), F32)
    return _pcall(
        body, name=name, grid=(ma // tm, nb // tn, nk),
        in_specs=[pl.BlockSpec((tk, tm), lambda i, j, k: (k, i)), pl.BlockSpec((tk, tn), lambda i, j, k: (k, j))],
        out_specs=o_spec, out_shape=out_shape, scratch_shapes=[pltpu.VMEM((tm, tn), F32)],
        compiler_params=pltpu.CompilerParams(dimension_semantics=("parallel", "parallel", "arbitrary")),
    )(a, b)


def _row_call(body, *, name, row_ins, res_ins, row_outs, acc_outs=(), scratch=()):
    row_ins = [a if isinstance(a, tuple) else (a, 1) for a in row_ins]
    row_outs = [o if len(o) == 3 else (*o, 1) for o in row_outs]
    s = row_ins[0][0].shape[0]
    tile = ROW_TILES.get(name, ROW_TILE)
    n = s // tile
    nri, nre, nro, nao = len(row_ins), len(res_ins), len(row_outs), len(acc_outs)

    def whole(shape):
        nd = len(shape)
        return pl.BlockSpec(tuple(shape), lambda i: (0,) * nd, pipeline_mode=pl.Buffered(1))

    def whole_out(shape):
        nd = len(shape)
        return pl.BlockSpec(tuple(shape), lambda i: (0,) * nd)

    def rows(w, d):
        if d == 1:
            return pl.BlockSpec((tile, w), lambda i: (i, 0))
        return pl.BlockSpec((d, tile // d, w), lambda i: (0, i, 0))

    in_specs = [rows(a.shape[1], d) for a, d in row_ins]
    in_specs += [whole(a.shape) for a in res_ins]
    out_specs = [rows(w, d) for w, _, d in row_outs]
    out_specs += [whole_out(shp) for shp, _ in acc_outs]
    out_shape = [jax.ShapeDtypeStruct((s, w) if d == 1 else (d, s // d, w), dt) for w, dt, d in row_outs]
    out_shape += [jax.ShapeDtypeStruct(tuple(shp), dt) for shp, dt in acc_outs]

    def wrapped(*refs):
        at = [0, nri, nri + nre, nri + nre + nro, nri + nre + nro + nao]
        body(pl.program_id(0), n, refs[at[0]:at[1]], refs[at[1]:at[2]], refs[at[2]:at[3]], refs[at[3]:at[4]],
             refs[at[4]:])

    args = [a if d == 1 else a.reshape(d, s // d, a.shape[1]) for a, d in row_ins]
    outs = _pcall(wrapped, name=name, grid=(n,), in_specs=in_specs, out_specs=out_specs, out_shape=out_shape,
                  scratch_shapes=list(scratch), compiler_params=_ARB)(*args, *res_ins)
    return [o.reshape(s, o.shape[-1]) if k < nro and row_outs[k][2] != 1 else o for k, o in enumerate(outs)]


def _stage_shape(name):
    return pltpu.VMEM((4, ROW_TILES.get(name, ROW_TILE), LANES), F32)


def _from_residue(ref, col, stage, slot):
    d, n = ref.shape[0], ref.shape[1]
    for r in range(d):
        stage.at[slot][pl.ds(r, n, stride=d), :] = ref[r, :, col:col + LANES].astype(F32)
    return stage[slot]


def _natural(ref, stage, slot0):
    if len(ref.shape) == 2:
        return ref[...]
    return jnp.concatenate([_from_residue(ref, c * LANES, stage, (slot0 + c) % 4)
                            for c in range(ref.shape[2] // LANES)], axis=1)


def _to_residue(val, ref, col, stage, slot):
    d, n = ref.shape[0], ref.shape[1]
    stage[slot] = val
    for r in range(d):
        ref[r, :, col:col + LANES] = stage.at[slot][pl.ds(r, n, stride=d), :].astype(ref.dtype)


def _fold8(t):
    r, w = t.shape
    return jnp.sum(t.reshape(r // 8, 8, w), axis=0)


def _sigmoid(t):
    return 0.5 * (jnp.tanh(0.5 * t) + 1.0)


def _head_lanes():
    return lax.broadcasted_iota(jnp.int32, (1, LANES), 1) < HEAD_DIM


def _head_mean(t, lo):
    s_lo = jnp.sum(jnp.where(lo, t, 0.0), axis=1, keepdims=True)
    s_hi = jnp.sum(jnp.where(lo, 0.0, t), axis=1, keepdims=True)
    return jnp.where(lo, s_lo, s_hi) * (1.0 / HEAD_DIM)


def _rms_mod(xv, g, sc, sh):
    rstd = lax.rsqrt(jnp.mean(xv * xv, axis=1, keepdims=True) + EPS)
    return (xv * rstd * g) * (1.0 + sc) + sh


def _rms_mod_bwd(xv, dh, g, sc):
    rstd = lax.rsqrt(jnp.mean(xv * xv, axis=1, keepdims=True) + EPS)
    xhat = xv * rstd
    dn = dh * (1.0 + sc)
    dxhat = dn * g
    dx = rstd * (dxhat - xhat * jnp.mean(dxhat * xhat, axis=1, keepdims=True))
    return dx, dh, dh * (xhat * g), dn * xhat


def _mix_weights(ls):
    m = jnp.maximum(jnp.maximum(ls[0], ls[1]), ls[2])
    es = [jnp.exp(t - m) for t in ls]
    den = es[0] + es[1] + es[2]
    return [e / den for e in es]


def _rope_tables():
    half = ROT_DIM // 2
    inv_freq = ROPE_THETA ** (-(jnp.arange(half, dtype=F32) * 2.0) / ROT_DIM)
    lane = np.arange(LANES) % HEAD_DIM
    ang = jnp.arange(SEQ).astype(F32)[:, None] * jnp.tile(inv_freq, LANES // half)[None, :]
    cos, sin = jnp.cos(ang), jnp.sin(ang)
    first, second = jnp.asarray(lane < half)[None, :], jnp.asarray((lane >= half) & (lane < ROT_DIM))[None, :]
    cos_t = jnp.where(first | second, cos, 1.0)
    return cos_t, jnp.where(second, sin, 0.0), jnp.where(first, -sin, 0.0)


_SECTIONS = ((0, WA, 0, False), (WA, 2 * WA, 1, False), (2 * WA, 3 * WA, -1, False),
             (3 * WA, 3 * WA + WB, 2, True), (3 * WA + WB, 3 * WA + 2 * WB, 3, True), (3 * WA + 2 * WB, W_QKV, -1, False))


def _pre_attn_fwd(x, cos_t, sa_t, sb_t, g1, sc1, sh1, w_qkv, w_gates, gains, *, name):
    half = ROT_DIM // 2
    dilated = [(g, dd) for g, (_, dd) in enumerate(DIL_CONFIGS) if dd > 1]

    def body(i, n, rin, res, rout, aout, scr):
        x_ref, cos_ref, sa_ref, sb_ref = rin
        g_ref, sc_ref, sh_ref, wq_ref, wg_ref, gains_ref = res
        h1_ref, qkvn_ref, pre_ref, gates_ref = rout[:4]
        group_ref = {g: rout[4 + k] for k, (g, _) in enumerate(dilated)}
        (stage,) = scr
        staged = 0
        hb = _rms_mod(x_ref[...], g_ref[...], sc_ref[...], sh_ref[...]).astype(BF16)
        h1_ref[...] = hb
        gates_ref[...] = _dot(hb, wg_ref[...]).astype(BF16)
        lo = _head_lanes()
        cosv, sav, sbv = cos_ref[...], sa_ref[...], sb_ref[...]
        pre_at = 0
        for si, (c0, c1, kind, rot) in enumerate(_SECTIONS):
            sec = _dot(hb, wq_ref[:, c0:c1])
            for ch in range((c1 - c0) // LANES):
                t = sec[:, ch * LANES:(ch + 1) * LANES]
                if kind >= 0:
                    pre_ref[:, pre_at:pre_at + LANES] = t.astype(BF16)
                    pre_at += LANES
                    t = t * lax.rsqrt(_head_mean(t * t, lo) + EPS) * gains_ref[kind:kind + 1, :]
                    if rot:
                        t = t * cosv + pltpu.roll(t, half, 1) * sav + pltpu.roll(t, LANES - half, 1) * sbv
                qkvn_ref[:, c0 + ch * LANES:c0 + (ch + 1) * LANES] = t.astype(BF16)
                group = ch * LANES // WB_OUT if si >= 3 else 0
                if group in group_ref:
                    col = (si - 3) * WB_OUT + ch * LANES % WB_OUT
                    _to_residue(t, group_ref[group], col, stage, staged % 4)
                    staged += 1

    return _row_call(body, name=name, row_ins=[x, cos_t, sa_t, sb_t], res_ins=[g1, sc1, sh1, w_qkv, w_gates, gains],
                     row_outs=[(D_MODEL, BF16), (W_QKV, BF16), (W_QK, BF16), (W_GATES, BF16)]
                     + [(3 * WB_OUT, BF16, dd) for _, dd in dilated], scratch=[_stage_shape(name)])


def _pre_attn_bwd(qk_pre, d_parts, dgates, x, dx1, cos_t, sa_t, sb_t, w_qkv, w_gates, gains, g1, sc1, *, name):
    half = ROT_DIM // 2
    nparts = len(d_parts)
    where = []
    residue = [isinstance(part, tuple) for part in d_parts]
    for pi, part in enumerate(d_parts):
        width = (part[0] if residue[pi] else part).shape[1]
        where += [(pi, cj) for cj in range(width // LANES)]
    assert len(where) == W_QKV // LANES

    def body(i, n, rin, res, rout, aout, scr):
        pre_ref, d_refs = rin[0], rin[1:1 + nparts]
        dgates_ref, x_ref, dx1_ref, cos_ref, sa_ref, sb_ref = rin[1 + nparts:]
        wq_ref, wg_ref, gains_ref, g_ref, sc_ref = res
        dqkv_ref, gx_ref = rout
        dgains_ref, sums_ref = aout
        accg, accs, stage = scr
        staged = 0

        @pl.when(i == 0)
        def _():
            accg[...] = jnp.zeros_like(accg)
            accs[...] = jnp.zeros_like(accs)

        lo = _head_lanes()
        cosv, sav, sbv = cos_ref[...], sa_ref[...], sb_ref[...]
        dh = _dot_nt(dgates_ref[...], wg_ref[...])
        pre_at = 0
        for c0, c1, kind, rot in _SECTIONS:
            for ch in range((c1 - c0) // LANES):
                pi, cj = where[c0 // LANES + ch]
                if residue[pi]:
                    dt = _from_residue(d_refs[pi], cj * LANES, stage, staged % 4)
                    staged += 1
                else:
                    dt = d_refs[pi][:, cj * LANES:(cj + 1) * LANES]
                if kind >= 0:
                    if rot:
                        dt = dt * cosv + pltpu.roll(dt * sav, LANES - half, 1) + pltpu.roll(dt * sbv, half, 1)
                    t = pre_ref[:, pre_at:pre_at + LANES].astype(F32)
                    pre_at += LANES
                    rstd = lax.rsqrt(_head_mean(t * t, lo) + EPS)
                    xhat = t * rstd
                    accg[kind] += _fold8(dt * xhat)
                    dxhat = dt * gains_ref[kind:kind + 1, :]
                    dt = rstd * (dxhat - xhat * _head_mean(dxhat * xhat, lo))
                dqkv_ref[:, c0 + ch * LANES:c0 + (ch + 1) * LANES] = dt.astype(BF16)
            dh = dh + _dot_nt(dqkv_ref[:, c0:c1], wq_ref[:, c0:c1])
        dx, t_sh, t_sc, t_g = _rms_mod_bwd(x_ref[...], dh, g_ref[...], sc_ref[...])
        gx_ref[...] = dx1_ref[...] + dx
        accs[0] += _fold8(t_sh)
        accs[1] += _fold8(t_sc)
        accs[2] += _fold8(t_g)

        @pl.when(i == n - 1)
        def _():
            t = jnp.sum(accg[...], axis=1)
            dgains_ref[...] = t + pltpu.roll(t, HEAD_DIM, 1)
            sums_ref[...] = jnp.sum(accs[...], axis=1)

    return _row_call(
        body, name=name, row_ins=[qk_pre, *d_parts, dgates, x, dx1, cos_t, sa_t, sb_t],
        res_ins=[w_qkv, w_gates, gains, g1, sc1], row_outs=[(W_QKV, BF16), (D_MODEL, F32)],
        acc_outs=[((4, LANES), F32), ((3, D_MODEL), F32)],
        scratch=[pltpu.VMEM((4, 8, LANES), F32), pltpu.VMEM((3, 8, D_MODEL), F32), _stage_shape(name)])


def _post_attn_fwd(o_a, o_g, l_g, gates, x, w_pa, w_pb, w_o, b_gate, gt1, g2, sc2, sh2, *, name):
    d = D_MODEL

    def body(i, n, rin, res, rout, aout, scr):
        oa_ref, o0, o1, o2, l0, l1, l2, gates_ref, x_ref = rin
        wpa_ref, wpb_ref, wo_ref, b_ref, gt_ref, g_ref, sc_ref, sh_ref = res
        ob_ref, merged_ref, mo_ref, x1_ref, h2_ref = rout
        (stage,) = scr
        ogs = [_natural(r, stage, 0) for r in (o0, o1, o2)]
        ws = _mix_weights([_natural(r, stage, 2) for r in (l0, l1, l2)])
        obb = (ws[0] * ogs[0] + ws[1] * ogs[1] + ws[2] * ogs[2]).astype(BF16)
        ob_ref[...] = obb
        pa = _dot(oa_ref[...].astype(BF16), wpa_ref[...])
        pb = _dot(obb, wpb_ref[...])
        ga = _sigmoid(gates_ref[:, :d].astype(F32) + b_ref[:, :d])
        gb = _sigmoid(gates_ref[:, d:].astype(F32) + b_ref[:, d:])
        merged = (ga * pa + gb * pb).astype(BF16)
        merged_ref[...] = merged
        mo = _dot(merged, wo_ref[...])
        mo_ref[...] = mo.astype(BF16)
        x1 = x_ref[...] + gt_ref[...] * mo
        x1_ref[...] = x1
        h2_ref[...] = _rms_mod(x1, g_ref[...], sc_ref[...], sh_ref[...]).astype(BF16)

    return _row_call(body, name=name, row_ins=[o_a, *o_g, *l_g, gates, x],
                     res_ins=[w_pa, w_pb, w_o, b_gate, gt1, g2, sc2, sh2],
                     row_outs=[(WB_OUT, BF16), (d, BF16), (d, BF16), (d, F32), (d, BF16)], scratch=[_stage_shape(name)])


def _ffn_fwd(h2, w_ffn_in, *, name):
    def body(i, n, rin, res, rout, aout, scr):
        (h_ref,), (w_ref,), (act_ref, ff_ref) = rin, res, rout
        hv = h_ref[...]
        for q in range(2):
            a = _dot(hv, w_ref[:, q * FF_CHIP:(q + 1) * FF_CHIP])
            up = _dot(hv, w_ref[:, D_FF + q * FF_CHIP:D_FF + (q + 1) * FF_CHIP])
            sl = slice(q * FF_CHIP, (q + 1) * FF_CHIP)
            act_ref[:, sl] = (a * _sigmoid(a) * up).astype(BF16)
            ff_ref[:, sl] = a.astype(BF16)
            ff_ref[:, D_FF + q * FF_CHIP:D_FF + (q + 1) * FF_CHIP] = up.astype(BF16)

    return _row_call(body, name=name, row_ins=[h2], res_ins=[w_ffn_in], row_outs=[(D_FF, BF16), (2 * D_FF, BF16)])


def _ffn_mid(act, ff, x1, tgt, w_ffn_out, gt2, *, name):
    d = D_MODEL

    def body(i, n, rin, res, rout, aout, scr):
        act_ref, ff_ref, x1_ref, tgt_ref = rin
        wo_ref, gt_ref = res
        dy_ref, dffo_ref, dff_ref = rout
        dgt_ref, loss_ref = aout
        (acc,) = scr

        @pl.when(i == 0)
        def _():
            acc[...] = jnp.zeros_like(acc)

        ffo = _dot(act_ref[...], wo_ref[...])
        gtv = gt_ref[...]
        e = x1_ref[...] + gtv * ffo - tgt_ref[...]
        dy = e * (1.0 / d)
        dy_ref[...] = dy
        dffo = (gtv * dy).astype(BF16)
        dffo_ref[...] = dffo
        acc[0] += _fold8(dy * ffo)
        acc[1] += _fold8(e * e)
        for q in range(2):
            sl = slice(q * FF_CHIP, (q + 1) * FF_CHIP)
            su = slice(D_FF + q * FF_CHIP, D_FF + (q + 1) * FF_CHIP)
            dact = _dot_nt(dffo, wo_ref[sl, :])
            a = ff_ref[:, sl].astype(F32)
            up = ff_ref[:, su].astype(F32)
            sg = _sigmoid(a)
            dff_ref[:, sl] = (dact * up * (sg * (1.0 + a * (1.0 - sg)))).astype(BF16)
            dff_ref[:, su] = (dact * (a * sg)).astype(BF16)

        @pl.when(i == n - 1)
        def _():
            dgt_ref[...] = jnp.sum(acc[0], axis=0, keepdims=True)
            tot = jnp.sum(jnp.sum(acc[1], axis=0, keepdims=True), axis=1, keepdims=True)
            loss_ref[...] = jnp.broadcast_to(tot * (0.5 / d), (1, LANES))

    return _row_call(body, name=name, row_ins=[act, ff, x1, tgt], res_ins=[w_ffn_out, gt2],
                     row_outs=[(d, F32), (d, BF16), (2 * D_FF, BF16)], acc_outs=[((1, d), F32), ((1, LANES), F32)],
                     scratch=[pltpu.VMEM((2, 8, d), F32)])


def _ffn_in_bwd(dff, x1, dy, mo, w_ffn_in, g2, sc2, gt1, *, name):
    d = D_MODEL

    def body(i, n, rin, res, rout, aout, scr):
        dff_ref, x1_ref, dy_ref, mo_ref = rin
        w_ref, g_ref, sc_ref, gt_ref = res
        dx1_ref, dmo_ref = rout
        (sums_ref,) = aout
        (acc,) = scr

        @pl.when(i == 0)
        def _():
            acc[...] = jnp.zeros_like(acc)

        dh = _dot_nt(dff_ref[...], w_ref[...])
        dx, t_sh, t_sc, t_g = _rms_mod_bwd(x1_ref[...], dh, g_ref[...], sc_ref[...])
        dx1 = dy_ref[...] + dx
        dx1_ref[...] = dx1
        dmo_ref[...] = (gt_ref[...] * dx1).astype(BF16)
        acc[0] += _fold8(t_sh)
        acc[1] += _fold8(t_sc)
        acc[2] += _fold8(t_g)
        acc[3] += _fold8(dx1 * mo_ref[...].astype(F32))

        @pl.when(i == n - 1)
        def _():
            sums_ref[...] = jnp.sum(acc[...], axis=1)

    return _row_call(body, name=name, row_ins=[dff, x1, dy, mo], res_ins=[w_ffn_in, g2, sc2, gt1],
                     row_outs=[(d, F32), (d, BF16)], acc_outs=[((4, d), F32)], scratch=[pltpu.VMEM((4, 8, d), F32)])


def _post_attn_bwd(dmo, gates, o_a, o_g, l_g, w_pa, w_pb, w_o, b_gate, *, name):
    d = D_MODEL

    def body(i, n, rin, res, rout, aout, scr):
        dmo_ref, gates_ref, oa_ref, o0, o1, o2, l0, l1, l2 = rin
        wpa_ref, wpb_ref, wo_ref, b_ref = res
        dpa_ref, dpb_ref, dgates_ref, doa_ref = rout[:4]
        do_refs, dl_refs = rout[4:7], rout[7:10]
        (dbg_ref,) = aout
        acc, stage = scr

        @pl.when(i == 0)
        def _():
            acc[...] = jnp.zeros_like(acc)

        ogs = [_natural(r, stage, 0) for r in (o0, o1, o2)]
        ws = _mix_weights([_natural(r, stage, 2) for r in (l0, l1, l2)])
        obb = (ws[0] * ogs[0] + ws[1] * ogs[1] + ws[2] * ogs[2]).astype(BF16)
        pa = _dot(oa_ref[...].astype(BF16), wpa_ref[...])
        pb = _dot(obb, wpb_ref[...])
        ga = _sigmoid(gates_ref[:, :d].astype(F32) + b_ref[:, :d])
        gb = _sigmoid(gates_ref[:, d:].astype(F32) + b_ref[:, d:])
        dm = _dot_nt(dmo_ref[...], wo_ref[...])
        dpa = (dm * ga).astype(BF16)
        dpb = (dm * gb).astype(BF16)
        dpa_ref[...] = dpa
        dpb_ref[...] = dpb
        dga = dm * pa * ga * (1.0 - ga)
        dgb = dm * pb * gb * (1.0 - gb)
        dgates_ref[:, :d] = dga.astype(BF16)
        dgates_ref[:, d:] = dgb.astype(BF16)
        acc[:, :d] += _fold8(dga)
        acc[:, d:] += _fold8(dgb)
        doa_ref[...] = _dot_nt(dpa, wpa_ref[...])
        dob = _dot_nt(dpb, wpb_ref[...])
        lo = _head_lanes()
        for ch in range(WB_OUT // LANES):
            sl = slice(ch * LANES, (ch + 1) * LANES)
            dv = dob[:, sl]
            wc = [w[:, sl] for w in ws]
            ts = [_head_mean(dv * og[:, sl], lo) * float(HEAD_DIM) for og in ogs]
            tbar = wc[0] * ts[0] + wc[1] * ts[1] + wc[2] * ts[2]
            for g in range(3):
                for k, (ref, val) in enumerate(((do_refs[g], wc[g] * dv), (dl_refs[g], wc[g] * (ts[g] - tbar)))):
                    if len(ref.shape) == 2:
                        ref[:, sl] = val
                    else:
                        _to_residue(val, ref, ch * LANES, stage, (2 * g + k) % 4)

        @pl.when(i == n - 1)
        def _():
            dbg_ref[...] = jnp.sum(acc[...], axis=0, keepdims=True)

    return _row_call(body, name=name, row_ins=[dmo, gates, o_a, *o_g, *l_g], res_ins=[w_pa, w_pb, w_o, b_gate],
                     row_outs=[(d, BF16), (d, BF16), (2 * d, BF16), (WA, F32)]
                     + 2 * [(WB_OUT, F32, dd) for _, dd in DIL_CONFIGS],
                     acc_outs=[((1, 2 * d), F32)], scratch=[pltpu.VMEM((8, 2 * d), F32), _stage_shape(name)])


def _na_class_tables():
    ro = np.full((3, NA_QROWS, 2 * NA_PAIRS), NA_RO_NONE, np.int64)
    slot = np.zeros((3, NA_QROWS, NA_PAIRS), np.int64)
    for t in range(3):
        for a in range(NA_QROWS):
            qr = _NA_CLASS_R0[t] + a
            rs = min(max(qr - NA_KH // 2, 0), ROWS - NA_KH)
            for b in range(NA_KROWS):
                kr = _NA_CLASS_K0[t] + b
                if rs <= kr < rs + NA_KH:
                    ro[t, a, b] = kr - qr + (NA_KH - 1)
            for j in range(NA_PAIRS):
                slot[t, a, j] = 2 * j - a + (_NA_CLASS_K0[t] - _NA_CLASS_R0[t] + NA_KH - 1) + (NA_QROWS - 1)
    assert slot.min() >= 0 and slot.max() < NA_SLOTS
    return ro, slot


def _na_build_bias(i, cls_ref, rp_ref, cm_ref, bias_scr):
    ro, _ = _na_class_tables()
    lo = _head_lanes()
    first = jnp.logical_or(i == 0, cls_ref[i] != cls_ref[jnp.maximum(i - 1, 0)])
    for t in range(3):
        @pl.when(jnp.logical_and(first, cls_ref[i] == t))
        def _():
            for hh in range(2):
                for a in range(NA_QROWS):
                    for j in range(NA_PAIRS):
                        r0, r1 = int(ro[t, a, 2 * j]), int(ro[t, a, 2 * j + 1])
                        x0 = jnp.broadcast_to(rp_ref[hh, r0:r0 + 1, :], (GRID_W, LANES))
                        x1 = jnp.broadcast_to(rp_ref[hh, r1:r1 + 1, :], (GRID_W, LANES))
                        blk = jnp.where(lo, pltpu.roll(x0, GRID_W + 1, 1, stride=1, stride_axis=0),
                                        pltpu.roll(x1, 1, 1, stride=1, stride_axis=0))
                        bias_scr[hh, a * GRID_W:(a + 1) * GRID_W, j * LANES:(j + 1) * LANES] = blk + cm_ref[...]
    return first


def _attn_fwd(qkv, qc0, kc0, vc0, npairs, table, kstart, cls, nk, *, name, na=None, qb=Q_BLOCK):
    s = qkv.shape[0]

    def body(ks_ref, cls_ref, q_ref, k_ref, v_ref, b_ref, *rest):
        if na:
            cm_ref, o_ref, lse_ref, bias_scr = rest
        else:
            o_ref, lse_ref = rest
        i = pl.program_id(1)
        if na:
            _na_build_bias(i, cls_ref, b_ref, cm_ref, bias_scr)
        ks = pl.multiple_of(ks_ref[i], 64)
        q2 = q_ref[...]
        k2 = k_ref[pl.ds(ks, nk), :]
        v2 = v_ref[pl.ds(ks, nk), :]
        lo = _head_lanes()
        outs, lses = [], []
        for h in range(2):
            qm = jnp.where(lo if h == 0 else jnp.logical_not(lo), q2, jnp.zeros_like(q2))
            sc = _dot_nt(qm, k2) * SCALE + (bias_scr[h, :, :nk] if na else b_ref[0, 0])
            m = jnp.max(sc, axis=1, keepdims=True)
            p = jnp.exp(sc - m)
            l = jnp.sum(p, axis=1, keepdims=True)
            pv = _dot(p.astype(BF16), v2)
            outs.append(pv / l)
            lses.append(m + jnp.log(l))
        o_ref[...] = jnp.where(lo, outs[0], outs[1])
        lse_ref[...] = jnp.where(lo, lses[0], lses[1])

    w = npairs * LANES
    in_specs = [
        pl.BlockSpec((qb, LANES), lambda p, i, ks, cl: (i, qc0 + p)),
        pl.BlockSpec((s, LANES), lambda p, i, ks, cl: (0, kc0 + p)),
        pl.BlockSpec((s, LANES), lambda p, i, ks, cl: (0, vc0 + p)),
    ]
    if na:
        in_specs += _na_bias_specs()
        args, scratch = (kstart, cls, qkv, qkv, qkv, *na), [pltpu.VMEM((2, Q_BLOCK, NA_W), F32)]
    else:
        in_specs.append(pl.BlockSpec((1, 1, qb, nk), lambda p, i, ks, cl: (cl[i], 0, 0, 0)))
        args, scratch = (kstart, cls, qkv, qkv, qkv, table), []
    grid_spec = pltpu.PrefetchScalarGridSpec(
        num_scalar_prefetch=2, grid=(npairs, s // qb), in_specs=in_specs,
        out_specs=[pl.BlockSpec((qb, LANES), lambda p, i, ks, cl: (i, p)),
                   pl.BlockSpec((qb, LANES), lambda p, i, ks, cl: (i, p))],
        scratch_shapes=scratch,
    )
    return _pcall(body, name=name, grid_spec=grid_spec,
                  out_shape=[jax.ShapeDtypeStruct((s, w), F32), jax.ShapeDtypeStruct((s, w), F32)],
                  compiler_params=pltpu.CompilerParams(dimension_semantics=("parallel", "arbitrary")),
                  )(*args)


def _na_bias_specs():
    return [pl.BlockSpec((2, 16, LANES), lambda p, i, ks, cl: (p, 0, 0)),
            pl.BlockSpec((GRID_W, LANES), lambda p, i, ks, cl: (0, 0))]


def _attn_bwd(qkv, qc0, kc0, vc0, npairs, table, kstart, cls, nk, do, o, lse, *, name, dlse=None, na=None,
              qb=Q_BLOCK, order=None):
    s = qkv.shape[0]
    has_dlse = dlse is not None
    _, slot = _na_class_tables()

    def body(ks_ref, cls_ref, q_ref, k_ref, v_ref, b_ref, *rest):
        if na:
            cm_ref, rest = rest[0], rest[1:]
        do_ref, o_ref, lse_ref, rest = rest[0], rest[1], rest[2], rest[3:]
        if has_dlse:
            dlse_ref, rest = rest[0], rest[1:]
        if order is not None:
            rest = rest[1:]
        dq_ref, dk_ref, dv_ref = rest[0], rest[1], rest[2]
        if na:
            bank_ref, bias_scr, dbias_scr, bank_scr = rest[3:]
        i = pl.program_id(1)
        if na:
            first = _na_build_bias(i, cls_ref, b_ref, cm_ref, bias_scr)

        @pl.when(i == 0)
        def _():
            dk_ref[...] = jnp.zeros_like(dk_ref)
            dv_ref[...] = jnp.zeros_like(dv_ref)
            if na:
                dbias_scr[...] = jnp.zeros_like(dbias_scr)
                bank_scr[...] = jnp.zeros_like(bank_scr)

        ks = pl.multiple_of(ks_ref[i], 64)
        q2 = q_ref[...]
        k2 = k_ref[pl.ds(ks, nk), :]
        v2 = v_ref[pl.ds(ks, nk), :]
        do2 = do_ref[...]
        lse2 = lse_ref[...]
        doo = do2 * o_ref[...]
        do2b = do2.astype(BF16)
        lo = _head_lanes()
        lane = lax.broadcasted_iota(jnp.int32, (1, LANES), 1)
        dqs, dks, dvs = [], [], []
        for h in range(2):
            mh = lo if h == 0 else jnp.logical_not(lo)
            qm = jnp.where(mh, q2, jnp.zeros_like(q2))
            sc = _dot_nt(qm, k2) * SCALE + (bias_scr[h, :, :nk] if na else b_ref[0, 0])
            lse_h = jnp.max(jnp.where(mh, lse2, NEG), axis=1, keepdims=True)
            p = jnp.exp(sc - lse_h)
            delta = jnp.sum(jnp.where(mh, doo, 0.0), axis=1, keepdims=True)
            dom = jnp.where(mh, do2b, jnp.zeros_like(do2b))
            dp = _dot_nt(dom, v2)
            t = dp - delta
            if has_dlse:
                t = t + jnp.sum(jnp.where(lane == h * HEAD_DIM, dlse_ref[...], 0.0), axis=1, keepdims=True)
            ds = p * t
            if na:
                @pl.when(first)
                def _():
                    dbias_scr[h, :, :nk] = ds

                @pl.when(jnp.logical_not(first))
                def _():
                    dbias_scr[h, :, :nk] += ds
            dsb = ds.astype(BF16)
            dqs.append(_dot(dsb, k2))
            dks.append(lax.dot_general(dsb, q2, _TN, preferred_element_type=F32))
            dvs.append(lax.dot_general(p.astype(BF16), do2b, _TN, preferred_element_type=F32))
        dq_ref[...] = jnp.where(lo, dqs[0], dqs[1]) * SCALE
        dk_ref[pl.ds(ks, nk), :] += jnp.where(lo, dks[0], dks[1]) * SCALE
        dv_ref[pl.ds(ks, nk), :] += jnp.where(lo, dvs[0], dvs[1])
        if na:
            last = jnp.logical_or(i == N_QBLK - 1, cls_ref[i] != cls_ref[jnp.minimum(i + 1, N_QBLK - 1)])
            for t in range(3):
                @pl.when(jnp.logical_and(last, cls_ref[i] == t))
                def _():
                    for hh in range(2):
                        for a in range(NA_QROWS):
                            for j in range(NA_PAIRS):
                                bank_scr[hh, int(slot[t, a, j])] += dbias_scr[
                                    hh, a * GRID_W:(a + 1) * GRID_W, j * LANES:(j + 1) * LANES]

            @pl.when(i == N_QBLK - 1)
            def _():
                bank_ref[...] = bank_scr[...]

    w = npairs * LANES
    blk = lambda: pl.BlockSpec((qb, LANES), lambda p, i, ks, cl: (i, p))
    full = lambda: pl.BlockSpec((s, LANES), lambda p, i, ks, cl: (0, p))
    in_specs = [
        pl.BlockSpec((qb, LANES), lambda p, i, ks, cl: (i, qc0 + p)),
        pl.BlockSpec((s, LANES), lambda p, i, ks, cl: (0, kc0 + p)),
        pl.BlockSpec((s, LANES), lambda p, i, ks, cl: (0, vc0 + p)),
    ]
    if na:
        in_specs += _na_bias_specs()
        args = [kstart, cls, qkv, qkv, qkv, *na]
    else:
        in_specs.append(pl.BlockSpec((1, 1, qb, nk), lambda p, i, ks, cl: (cl[i], 0, 0, 0)))
        args = [kstart, cls, qkv, qkv, qkv, table]
    in_specs += [blk(), blk(), blk()]
    args += [do, o, lse]
    if has_dlse:
        in_specs.append(blk())
        args.append(dlse)
    out_specs = [blk(), full(), full()]
    out_shape = [jax.ShapeDtypeStruct((s, w), F32)] * 3
    scratch = []
    if na:
        bank_shape = (2, NA_SLOTS, GRID_W, LANES)
        out_specs.append(pl.BlockSpec(bank_shape, lambda p, i, ks, cl: (p, 0, 0, 0)))
        out_shape.append(jax.ShapeDtypeStruct((2 * npairs,) + bank_shape[1:], F32))
        scratch = [pltpu.VMEM((2, Q_BLOCK, NA_W), F32), pltpu.VMEM((2, Q_BLOCK, NA_W), F32), pltpu.VMEM(bank_shape, F32)]
    if order is not None:
        in_specs.append(pl.BlockSpec(order.shape, lambda p, i, ks, cl: (0, 0)))
        args.append(order)
    grid_spec = pltpu.PrefetchScalarGridSpec(num_scalar_prefetch=2, grid=(npairs, s // qb), in_specs=in_specs,
                                             out_specs=out_specs, scratch_shapes=scratch)
    return _pcall(body, name=name, grid_spec=grid_spec, out_shape=out_shape,
                  compiler_params=pltpu.CompilerParams(dimension_semantics=("arbitrary", "arbitrary")))(*args)


_NA_CLASS_R0 = (0, NA_QROWS, ROWS - NA_QROWS)
_NA_CLASS_K0 = (0, 0, ROWS - NA_KROWS)
_RPB_RO = 2 * NA_KH - 1
_RPB_CO = 2 * NA_KW - 1
_BANK_ROWS = 48


def _na_constants():
    col = np.arange(GRID_W)
    cs = np.clip(col - NA_KW // 2, 0, GRID_W - NA_KW)
    vcol = (col[None, :] >= cs[:, None]) & (col[None, :] < cs[:, None] + NA_KW)
    colmask = np.where(np.concatenate([vcol, vcol], axis=1), 0.0, NEG).astype(np.float32)
    co = col[None, :] - col[:, None] + (NA_KW - 1)
    oh_col = np.zeros((GRID_W * GRID_W, LANES), np.float32)
    for qc in range(GRID_W):
        for kc in range(GRID_W):
            if vcol[qc, kc]:
                oh_col[qc * GRID_W + kc, co[qc, kc]] = 1.0
    ks = np.clip(np.arange(N_QBLK) * NA_QROWS - NA_KH // 2, 0, ROWS - NA_KROWS) * GRID_W
    cls = np.ones(N_QBLK, np.int32)
    cls[0], cls[-1] = 0, 2
    return colmask, oh_col, ks.astype(np.int32), cls


def _bank_reduce(bank, oh_col, *, name):
    def body(d_ref, ohc_ref, o_ref):
        o_ref[0] = jnp.dot(d_ref[0], ohc_ref[...], preferred_element_type=F32, precision=lax.Precision.HIGHEST)

    return _pcall(
        body, name=name, grid=(NA_HEADS,),
        in_specs=[pl.BlockSpec((1, _BANK_ROWS, GRID_W * GRID_W), lambda h: (h, 0, 0)),
                  pl.BlockSpec((GRID_W * GRID_W, LANES), lambda h: (0, 0))],
        out_specs=pl.BlockSpec((1, _BANK_ROWS, LANES), lambda h: (h, 0, 0)),
        out_shape=jax.ShapeDtypeStruct((NA_HEADS, _BANK_ROWS, LANES), F32), compiler_params=_PAR,
    )(bank, oh_col)


def _dil_constants(dilation):
    seg = SEQ // dilation
    nb = seg // DIL_QB
    nk = min(DIL_QB + 2 * DIL_HALF, seg)
    starts = [min(max(blk * DIL_QB - DIL_HALF, 0), seg - nk) for blk in range(nb)]
    shifts = sorted({w0 - blk * DIL_QB for blk, w0 in enumerate(starts)}, reverse=True)
    qi = np.arange(DIL_QB)[:, None]
    ki = np.arange(nk)[None, :]
    mask = np.stack([np.where(np.abs(ki + sh - qi) <= DIL_HALF, 0.0, NEG) for sh in shifts]).astype(np.float32)
    ks, cls = [], []
    for i in range(SEQ // DIL_QB):
        sub, blk = divmod(i, nb)
        cls.append(shifts.index(starts[blk] - blk * DIL_QB))
        ks.append(sub * seg + starts[blk])
    return mask.reshape(len(shifts), 1, DIL_QB, nk), np.asarray(ks, np.int32), np.asarray(cls, np.int32), nk


_VM = pl.BlockSpec(memory_space=pltpu.VMEM)


def _ada_fwd(c_all, w, b, *, name):
    def body(c_ref, w_ref, b_ref, o_ref):
        cv = c_ref[...]
        o_ref[...] = jnp.dot(cv * _sigmoid(cv), w_ref[...], preferred_element_type=F32,
                             precision=lax.Precision.HIGHEST) + b_ref[...]

    return _pcall(body, name=name, in_specs=[_VM, _VM, _VM], out_specs=_VM,
                  out_shape=jax.ShapeDtypeStruct((c_all.shape[0], w.shape[1]), F32))(c_all, w, b)


def _ada_bwd(c_all_t, dmod, *, name):
    def body(c_ref, d_ref, o_ref):
        cv = c_ref[...]
        o_ref[...] = jnp.dot(cv * _sigmoid(cv), d_ref[...], preferred_element_type=F32,
                             precision=lax.Precision.HIGHEST)

    return _pcall(body, name=name, in_specs=[_VM, _VM], out_specs=_VM,
                  out_shape=jax.ShapeDtypeStruct((c_all_t.shape[0], dmod.shape[1]), F32))(c_all_t, dmod)


def _row_sum(t, *, name):
    def body(t_ref, o_ref):
        o_ref[...] = jnp.sum(t_ref[...], axis=0, keepdims=True)

    return _pcall(body, name=name, in_specs=[_VM], out_specs=_VM,
                  out_shape=jax.ShapeDtypeStruct((1, t.shape[1]), F32))(t)


def _row_tile(rows):
    tr = rows
    for cand in range(8, 513, 8):
        if rows % cand == 0:
            tr = cand
    return tr


def _adamw_math(wv, gv, mv, vv):
    nm = ADAM_B1 * mv + (1.0 - ADAM_B1) * gv
    nv = ADAM_B2 * vv + (1.0 - ADAM_B2) * (gv * gv)
    m_hat = nm / (1.0 - ADAM_B1 ** ADAM_STEP)
    v_hat = nv / (1.0 - ADAM_B2 ** ADAM_STEP)
    return -ADAM_LR * (m_hat / (jnp.sqrt(v_hat) + ADAM_EPS) + ADAM_WD * wv), nm, nv


def _adamw(w, g, m, v, *, name):
    rows, cols = w.shape
    tr = _row_tile(rows)

    def body(w_ref, g_ref, m_ref, v_ref, d_ref, nm_ref, nv_ref):
        d_ref[...], nm_ref[...], nv_ref[...] = _adamw_math(w_ref[...], g_ref[...], m_ref[...], v_ref[...])

    spec = pl.BlockSpec((tr, cols), lambda i: (i, 0))
    return _pcall(body, name=name, grid=(rows // tr,), in_specs=[spec] * 4, out_specs=[spec] * 3,
                  out_shape=[jax.ShapeDtypeStruct((rows, cols), F32)] * 3, compiler_params=_PAR)(w, g, m, v)


def _adamw_halves(w, g_mine, g_other, m, v, c_arr, *, name):
    rows, cols = w.shape
    hr = rows // 2
    tr = _row_tile(hr)
    nt = hr // tr

    def body(c_ref, w_ref, t_ref, o_ref, m_ref, v_ref, g_ref, d_ref, nm_ref, nv_ref):
        gv = jnp.where(pl.program_id(0) == c_ref[0], t_ref[...], o_ref[...])
        g_ref[...] = gv
        d_ref[...], nm_ref[...], nv_ref[...] = _adamw_math(w_ref[...], gv, m_ref[...], v_ref[...])

    full = pl.BlockSpec((tr, cols), lambda h, i, c: (h * nt + i, 0))
    half = pl.BlockSpec((tr, cols), lambda h, i, c: (i, 0))
    grid_spec = pltpu.PrefetchScalarGridSpec(num_scalar_prefetch=1, grid=(2, nt),
                                             in_specs=[full, half, half, full, full], out_specs=[full] * 4)
    return _pcall(body, name=name, grid_spec=grid_spec, out_shape=[jax.ShapeDtypeStruct((rows, cols), F32)] * 4,
                  compiler_params=pltpu.CompilerParams(dimension_semantics=("parallel", "parallel")),
                  )(c_arr, w, g_mine, g_other, m, v)


_MESH = pl.DeviceIdType.MESH
_ANY = pl.BlockSpec(memory_space=pl.ANY)
_CHIP_FLIPS = ((1, 0), (0, 1), (1, 1))


def _pos():
    return lax.axis_index("x"), lax.axis_index("y"), lax.axis_index("c")


def _flip(v, f):
    return 1 - v if f else v


def _sem_pairs(n):
    return [pltpu.SemaphoreType.DMA((n,)), pltpu.SemaphoreType.DMA((n,))]


def _small_allgather(blk, *, name):
    m_per, n = blk.shape

    def body(x_ref, out_ref, send_sems, recv_sems, local_sem):
        x, y, c = _pos()
        me, sibling = (x, y, c), (x, y, 1 - c)
        chips = [(_flip(x, fx), _flip(y, fy)) for fx, fy in _CHIP_FLIPS]

        def rows(px, py, pc):
            return out_ref.at[pl.ds((4 * px + 2 * py + pc) * m_per, m_per), :]

        def copy(k, block, to, src=None):
            return pltpu.make_async_remote_copy(
                src_ref=rows(*block) if src is None else src, dst_ref=rows(*block),
                send_sem=send_sems.at[k], recv_sem=recv_sems.at[k], device_id=to, device_id_type=_MESH)

        mine = pltpu.make_async_copy(x_ref, rows(*me), local_sem)
        mine.start()
        first = [copy(0, me, sibling, src=x_ref)]
        first += [copy(1 + j, me, (*chip, c), src=x_ref) for j, chip in enumerate(chips)]
        for cp in first:
            cp.start()
        passed = [copy(4 + j, (*chip, c), sibling) for j, chip in enumerate(chips)]
        for j, chip in enumerate(chips):
            copy(1 + j, (*chip, c), me).wait_recv()
            passed[j].start()
        copy(0, sibling, me).wait_recv()
        for j, chip in enumerate(chips):
            copy(4 + j, (*chip, 1 - c), me).wait_recv()
        for cp in first + passed:
            cp.wait_send()
        mine.wait()

    return _pcall(
        body, name=name, out_shape=jax.ShapeDtypeStruct((N_DEV * m_per, n), blk.dtype),
        in_specs=[_VM], out_specs=_VM,
        scratch_shapes=_sem_pairs(7) + [pltpu.SemaphoreType.DMA],
    )(blk)


_HBM = pl.BlockSpec(memory_space=pltpu.HBM)
_SEM = pl.BlockSpec(memory_space=pltpu.SEMAPHORE)
_EFFECT = pltpu.SideEffectType.DATAFLOW_SIDE_EFFECTING


def _split_start(srcs, lands, plan, ncopies, after, *, name):
    ns, nl = len(srcs), len(lands)

    def body(*refs):
        src_refs, land_refs = refs[:ns], refs[ns:ns + nl]
        send_sems, recv_sems = refs[ns + nl + 1], refs[ns + nl + 2]
        token = refs[-1]
        x, y, c = _pos()
        for k, (src, dst, to, _) in enumerate(plan(x, y, c, src_refs, land_refs)):
            pltpu.make_async_remote_copy(src_ref=src, dst_ref=dst, send_sem=send_sems.at[k], recv_sem=recv_sems.at[k],
                                         device_id=to, device_id_type=_MESH).start()
        token[...] = jnp.zeros_like(token)

    hbm = lambda a: pltpu.HBM(a.shape, a.dtype)
    out = _pcall(
        body, name=name,
        out_shape=(pltpu.SemaphoreType.DMA((ncopies,)), pltpu.SemaphoreType.DMA((ncopies,)),
                   *[hbm(a) for a in srcs], *[hbm(a) for a in lands], jax.ShapeDtypeStruct((8, LANES), F32)),
        in_specs=[_HBM] * (ns + nl) + [_ANY], out_specs=(_SEM, _SEM, *[_HBM] * (ns + nl), _VM),
        input_output_aliases={i: 2 + i for i in range(ns + nl)},
        compiler_params=pltpu.CompilerParams(has_side_effects=_EFFECT),
    )(*[pltpu.with_memory_space_constraint(a, pltpu.HBM) for a in (*srcs, *lands)], after)
    return out[0], out[1], list(out[2:2 + ns]), list(out[2 + ns:2 + ns + nl]), out[-1]


def _split_wait(send_sems, recv_sems, srcs, lands, plan, after, *, name, with_sources=False):
    ns, nl = len(srcs), len(lands)

    def body(*refs):
        src_refs, land_refs = refs[:ns], refs[ns:ns + nl]
        send_sems, recv_sems = refs[ns + nl], refs[ns + nl + 1]
        x, y, c = _pos()
        for k, (src, _, _, mine) in enumerate(plan(x, y, c, src_refs, land_refs)):
            cp = pltpu.make_async_remote_copy(src_ref=src, dst_ref=mine, send_sem=send_sems.at[k],
                                              recv_sem=recv_sems.at[k], device_id=(x, y, c), device_id_type=_MESH)
            cp.wait_send()
            cp.wait_recv()

    hbm = lambda a: pltpu.HBM(a.shape, a.dtype)
    out = _pcall(
        body, name=name, out_shape=tuple(hbm(a) for a in (*srcs, *lands)),
        in_specs=[_HBM] * (ns + nl) + [_SEM, _SEM, _ANY], out_specs=tuple([_HBM] * (ns + nl)),
        input_output_aliases={i: i for i in range(ns + nl)},
        compiler_params=pltpu.CompilerParams(has_side_effects=_EFFECT),
    )(*srcs, *lands, send_sems, recv_sems, after)
    return (list(out[:ns]), list(out[ns:])) if with_sources else list(out[ns:])


def _ag_plan(nw):
    def plan(x, y, c, sh_refs, full_refs):
        j = 2 * x + y
        out = []
        for wi in range(nw):
            for fx, fy in _CHIP_FLIPS:
                px, py = _flip(x, fx), _flip(y, fy)
                out.append((sh_refs[wi].at[c], full_refs[wi].at[j, c], (px, py, c), full_refs[wi].at[2 * px + py, c]))
            out.append((sh_refs[wi], full_refs[wi].at[j], (x, y, 1 - c), full_refs[wi].at[j]))
        return out
    return plan


def _ag_plan_direct(nw):
    def plan(x, y, c, sh_refs, full_refs):
        j = 2 * x + y
        out = []
        for wi in range(nw):
            for fx, fy in _CHIP_FLIPS:
                px, py = _flip(x, fx), _flip(y, fy)
                for rel in (0, 1):
                    t = _flip(c, rel)
                    out.append((sh_refs[wi].at[c], full_refs[wi].at[j, c], (px, py, t), full_refs[wi].at[2 * px + py, t]))
            out.append((sh_refs[wi], full_refs[wi].at[j], (x, y, 1 - c), full_refs[wi].at[j]))
        return out
    return plan


def _ag_pass(fulls, *, name):
    nw = len(fulls)

    def body(*refs):
        in_refs, out_refs = refs[:nw], refs[nw:2 * nw]
        send_sems, recv_sems = refs[2 * nw:]
        x, y, c = _pos()
        cps = []
        for wi in range(nw):
            for k, (fx, fy) in enumerate(_CHIP_FLIPS):
                jp = 2 * _flip(x, fx) + _flip(y, fy)
                sems = dict(send_sem=send_sems.at[3 * wi + k], recv_sem=recv_sems.at[3 * wi + k], device_id_type=_MESH)
                send = pltpu.make_async_remote_copy(src_ref=in_refs[wi].at[jp, c], dst_ref=out_refs[wi].at[jp, c],
                                                    device_id=(x, y, 1 - c), **sems)
                recv = pltpu.make_async_remote_copy(src_ref=in_refs[wi].at[jp, c], dst_ref=out_refs[wi].at[jp, 1 - c],
                                                    device_id=(x, y, c), **sems)
                cps.append((send, recv))
        for send, _ in cps:
            send.start()
        for send, recv in cps:
            send.wait_send()
            recv.wait_recv()

    return _pcall(body, name=name, out_shape=[jax.ShapeDtypeStruct(f.shape, f.dtype) for f in fulls],
                  in_specs=[_ANY] * nw, out_specs=[_ANY] * nw, input_output_aliases={i: i for i in range(nw)},
                  scratch_shapes=_sem_pairs(3 * nw))(*fulls)


def _sib_plan(nw):
    def plan(x, y, c, g_refs, ra_refs):
        return [(g_refs[wi].at[k, 1 - c], ra_refs[wi].at[k], (x, y, 1 - c), ra_refs[wi].at[k])
                for wi in range(nw) for k in range(N_CHIP)]
    return plan


def _rs_plan(nw):
    def plan(x, y, c, s_refs, rb_refs):
        out = []
        for wi in range(nw):
            for k, (fx, fy) in enumerate(_CHIP_FLIPS):
                px, py = _flip(x, fx), _flip(y, fy)
                out.append((s_refs[wi].at[2 * px + py], rb_refs[wi].at[k], (px, py, c), rb_refs[wi].at[k]))
        return out
    return plan


def _sibling_swap(ts, *, name):
    nw = len(ts)

    def body(*refs):
        t_refs, out_refs = refs[:nw], refs[nw:2 * nw]
        send_sems, recv_sems = refs[2 * nw:]
        x, y, c = _pos()
        cps = [pltpu.make_async_remote_copy(src_ref=t_refs[wi], dst_ref=out_refs[wi], send_sem=send_sems.at[wi],
                                            recv_sem=recv_sems.at[wi], device_id=(x, y, 1 - c), device_id_type=_MESH)
               for wi in range(nw)]
        for cp in cps:
            cp.start()
        for cp in cps:
            cp.wait()

    return _pcall(body, name=name, out_shape=[jax.ShapeDtypeStruct(t.shape, t.dtype) for t in ts],
                  in_specs=[_ANY] * nw, out_specs=[_ANY] * nw, scratch_shapes=_sem_pairs(nw))(*ts)


def _rs_add(g, ra, c_arr, *, name):
    n, _, r, w = g.shape

    def body(c_ref, g_ref, ra_ref, s_ref, sb_ref):
        t = g_ref[...] + ra_ref[...]
        s_ref[...] = t
        sb_ref[...] = t.astype(BF16)

    grid_spec = pltpu.PrefetchScalarGridSpec(
        num_scalar_prefetch=1, grid=(n,),
        in_specs=[pl.BlockSpec((None, None, r, w), lambda k, c: (k, c[0], 0, 0)),
                  pl.BlockSpec((None, r, w), lambda k, c: (k, 0, 0))],
        out_specs=[pl.BlockSpec((None, r, w), lambda k, c: (k, 0, 0))] * 2)
    return _pcall(body, name=name, grid_spec=grid_spec,
                  out_shape=[jax.ShapeDtypeStruct((n, r, w), F32), jax.ShapeDtypeStruct((n, r, w), BF16)],
                  compiler_params=_PAR)(c_arr, g, ra)


def _rs_final(s, rb, j_arr, *, name):
    _, r, w = s.shape

    def body(j_ref, s_ref, rb_ref, t_ref):
        t_ref[...] = ((s_ref[...] + rb_ref[0].astype(F32)) + rb_ref[1].astype(F32)) + rb_ref[2].astype(F32)

    grid_spec = pltpu.PrefetchScalarGridSpec(
        num_scalar_prefetch=1, grid=(1,),
        in_specs=[pl.BlockSpec((None, r, w), lambda i, j: (j[0], 0, 0)),
                  pl.BlockSpec((3, r, w), lambda i, j: (0, 0, 0))],
        out_specs=pl.BlockSpec((r, w), lambda i, j: (0, 0)))
    return _pcall(body, name=name, grid_spec=grid_spec, out_shape=jax.ShapeDtypeStruct((r, w), F32),
                  compiler_params=_ARB)(j_arr, s, rb)


def _tile2(g):
    return jnp.concatenate([g, g], axis=1)


_BIG = ("w_in", "w_ffn_in", "w_ffn_out", "w_o", "w_proj_a", "w_proj_b")
_BIG_SHARD = {"w_in": (1024, 1472), "w_ffn_in": (1024, 1408), "w_ffn_out": (704, 1024), "w_o": (256, 1024),
              "w_proj_a": (512, 256), "w_proj_b": (256, 256)}


def _device_step(x2, tgt, mod, first_weights, late_weights, early_grads, mid_grads, g_norm1, g_norm2, b_gate, g_qa, g_ka,
                 g_qb, g_kb, rpb):
    d = D_MODEL
    sh1, sc1, gt1, sh2, sc2, gt2 = [mod[:, k * d:(k + 1) * d] for k in range(6)]

    colmask, oh_col, na_ks, na_cls = _na_constants()
    rp = jnp.pad(rpb, ((0, 0), (0, 16 - _RPB_RO), (RP_LANE0, LANES - RP_LANE0 - _RPB_CO)), constant_values=NEG)
    na = (rp, jnp.asarray(colmask))
    na_ks, na_cls = jnp.asarray(na_ks), jnp.asarray(na_cls)
    dil = [_dil_constants(dd) for _, dd in DIL_CONFIGS]
    gains = jnp.concatenate([_tile2(g_qa), _tile2(g_ka), _tile2(g_qb), _tile2(g_kb)], axis=0)
    cos_t, sa_t, sb_t = _rope_tables()

    wts = first_weights(cos_t)
    h1, qkvn, qk_pre, gates, *qkv_dil = _pre_attn_fwd(x2, cos_t, sa_t, sb_t, g_norm1, sc1, sh1, wts["w_qkv"],
                                                      wts["w_gates"], gains, name="pre_attn_fwd")
    o_a, lse_a = _attn_fwd(qkvn, 0, 4, 8, 4, None, na_ks, na_cls, NA_NK, name="attn_a_fwd", na=na)
    arrs, o_g, l_g = [], [], []
    res = lambda t, dd: t if dd == 1 else (t, dd)
    for g, (_, dd) in enumerate(DIL_CONFIGS):
        tab_g, ks_g, cls_g, nk_g = jnp.asarray(dil[g][0]), jnp.asarray(dil[g][1]), jnp.asarray(dil[g][2]), dil[g][3]
        arr, cb = (qkvn, (12, 18, 24)) if dd == 1 else (qkv_dil.pop(0), (0, 2, 4))
        op, lp = _attn_fwd(arr, cb[0], cb[1], cb[2], 2, tab_g, ks_g, cls_g, nk_g, name=f"attn_d{g}_fwd", qb=DIL_QB)
        arrs.append((arr, cb, tab_g, ks_g, cls_g, nk_g))
        o_g.append(res(op, dd))
        l_g.append(res(lp, dd))
    wts = dict(wts, **late_weights(o_a))
    o_b, merged, mo, x1, h2 = _post_attn_fwd(o_a, o_g, l_g, gates, x2, wts["w_pa"], wts["w_pb"], wts["w_o"], b_gate,
                                             gt1, g_norm2, sc2, sh2, name="post_attn_fwd")
    act, ff = _ffn_fwd(h2, wts["w_ffn_in"], name="ffn_fwd")

    dy, dffo, dff, dgt2, loss_v = _ffn_mid(act, ff, x1, tgt, wts["w_ffn_out"], gt2, name="ffn_mid")
    grads = {}
    g_ffn_out = _wgrad(act, dffo, name="wg_ffn_out", tm=D_FF // 2, tn=d, tk=2048)
    grads["w_ffn_out"] = g_ffn_out.reshape(N_CHIP, D_FF // N_CHIP, d)
    grads["w_ffn_in"] = _wgrad(h2, dff, name="wg_ffn_in", tm=512, tn=2 * FF_CHIP, tk=2048, chips=N_CHIP)
    dx1, dmo, sums2 = _ffn_in_bwd(dff, x1, dy, mo, wts["w_ffn_in"], g_norm2, sc2, gt1, name="ffn_in_bwd")
    grads["w_o"] = _wgrad(merged, dmo, name="wg_o", tm=d, tn=d, tk=2048).reshape(N_CHIP, d // N_CHIP, d)
    pab = _post_attn_bwd(dmo, gates, o_a, o_g, l_g, wts["w_pa"], wts["w_pb"], wts["w_o"], b_gate, name="post_attn_bwd")
    dpa, dpb, dgates, do_a = pab[:4]
    do_g, dl_g, dbg = pab[4:7], pab[7:10], pab[10]
    g_pa = _wgrad(o_a, dpa, name="wg_pa", tm=WA, tn=d, tk=2048)
    g_pb = _wgrad(o_b, dpb, name="wg_pb", tm=WB_OUT, tn=d, tk=2048)
    grads["w_proj_a"] = g_pa.reshape(WA, N_CHIP, d // N_CHIP).transpose(1, 0, 2)
    grads["w_proj_b"] = g_pb.reshape(WB_OUT, N_CHIP, d // N_CHIP).transpose(1, 0, 2)
    order = early_grads(grads)
    dqs, dks, dvs = [], [], []
    for g, (_, dd) in enumerate(DIL_CONFIGS):
        arr, cb, tab_g, ks_g, cls_g, nk_g = arrs[g]
        plain = lambda t: t[0] if isinstance(t, tuple) else t
        dq, dk, dv = _attn_bwd(arr, cb[0], cb[1], cb[2], 2, tab_g, ks_g, cls_g, nk_g, do_g[g], plain(o_g[g]),
                               plain(l_g[g]), name=f"attn_d{g}_bwd", dlse=dl_g[g], qb=DIL_QB, order=order)
        dqs.append(res(dq, dd))
        dks.append(res(dk, dd))
        dvs.append(res(dv, dd))
    order = mid_grads(dv)
    dqa, dka, dva, bank = _attn_bwd(qkvn, 0, 4, 8, 4, None, na_ks, na_cls, NA_NK, do_a, o_a, lse_a,
                                    name="attn_a_bwd", na=na, order=order)
    dqkv, grad_x, dgains, sums1 = _pre_attn_bwd(qk_pre, [dqa, dka, dva] + dqs + dks + dvs, dgates, x2, dx1, cos_t, sa_t,
                                                sb_t, wts["w_qkv"], wts["w_gates"], gains, g_norm1, sc1,
                                                name="pre_attn_bwd")
    g_qkv = _wgrad(h1, dqkv, name="wg_qkv", tm=d, tn=W_QKV // 2, tk=2048)
    g_gates = _wgrad(h1, dgates, name="wg_gates", tm=d, tn=W_GATES // 2, tk=2048)
    nc, cut = _BIG_SHARD["w_in"][1], 3 * _BIG_SHARD["w_in"][1] - W_QKV
    grads["w_in"] = jnp.stack([g_qkv[:, :nc], g_qkv[:, nc:2 * nc],
                               jnp.concatenate([g_qkv[:, 2 * nc:], g_gates[:, :cut]], axis=1), g_gates[:, cut:]])

    bank = bank.reshape(NA_HEADS, NA_SLOTS, GRID_W, 2, GRID_W).transpose(0, 1, 3, 2, 4)
    bank = jnp.pad(bank.reshape(NA_HEADS, 2 * NA_SLOTS, GRID_W * GRID_W), ((0, 0), (0, _BANK_ROWS - 2 * NA_SLOTS), (0, 0)))
    g2 = _bank_reduce(bank, jnp.asarray(oh_col), name="rpb_reduce")[:, :2 * NA_SLOTS].reshape(NA_HEADS, NA_SLOTS, 2, LANES)
    g_rpb = g2[:, 3:3 + _RPB_RO, 0, :_RPB_CO] + g2[:, 2:2 + _RPB_RO, 1, :_RPB_CO]

    dmod = jnp.concatenate([sums1[0:1], sums1[1:2], sums2[3:4], sums2[0:1], sums2[1:2], dgt2], axis=1)
    small = dict(g_norm1=sums1[2:3], g_norm2=sums2[2:3], b_gate=dbg, g_qa=dgains[0:1, :HEAD_DIM],
                 g_ka=dgains[1:2, :HEAD_DIM], g_qb=dgains[2:3, :HEAD_DIM], g_kb=dgains[3:4, :HEAD_DIM], rpb=g_rpb)
    return loss_v, grad_x, grads, dmod, small


_SMALL = ("b_ada", "g_norm1", "g_norm2", "b_gate", "g_qa", "g_ka", "g_qb", "g_kb", "rpb")
_SMALL_N = {"b_ada": 6 * D_MODEL, "g_norm1": D_MODEL, "g_norm2": D_MODEL, "b_gate": 2 * D_MODEL, "g_qa": HEAD_DIM,
            "g_ka": HEAD_DIM, "g_qb": HEAD_DIM, "g_kb": HEAD_DIM, "rpb": NA_HEADS * _RPB_RO * _RPB_CO}


def _pack_small(parts):
    flat = [parts[n].reshape(1, _SMALL_N[n]) for n in _SMALL]
    used = sum(_SMALL_N.values())
    return jnp.concatenate(flat + [jnp.zeros((1, STATS_W - used), F32)], axis=1)


def _unpack_small(v, shapes):
    out, at = {}, 0
    for n in _SMALL:
        out[n] = v[:, at:at + _SMALL_N[n]].reshape(shapes[n])
        at += _SMALL_N[n]
    return out


def _join_cols(t):
    _, r, c = t.shape
    return t.transpose(1, 0, 2).reshape(r, N_CHIP * c)


def kernel(x, c, w_ada, b_ada, g_norm1, g_norm2, w_in, b_gate, g_qa, g_ka, g_qb, g_kb, rpb, w_proj_a, w_proj_b, w_o, w_ffn_in, w_ffn_out, loss_target, m_w_ada, m_b_ada, m_g_norm1, m_g_norm2, m_w_in, m_b_gate, m_g_qa, m_g_ka, m_g_qb, m_g_kb, m_rpb, m_w_proj_a, m_w_proj_b, m_w_o, m_w_ffn_in, m_w_ffn_out, v_w_ada, v_b_ada, v_g_norm1, v_g_norm2, v_w_in, v_b_gate, v_g_qa, v_g_ka, v_g_qb, v_g_kb, v_rpb, v_w_proj_a, v_w_proj_b, v_w_o, v_w_ffn_in, v_w_ffn_out):
    names = ("w_ada", "b_ada", "g_norm1", "g_norm2", "w_in", "b_gate", "g_qa", "g_ka", "g_qb", "g_kb", "rpb",
             "w_proj_a", "w_proj_b", "w_o", "w_ffn_in", "w_ffn_out")
    w = dict(zip(names, (w_ada, b_ada, g_norm1, g_norm2, w_in, b_gate, g_qa, g_ka, g_qb, g_kb, rpb, w_proj_a, w_proj_b,
                         w_o, w_ffn_in, w_ffn_out)))
    m = dict(zip(names, (m_w_ada, m_b_ada, m_g_norm1, m_g_norm2, m_w_in, m_b_gate, m_g_qa, m_g_ka, m_g_qb, m_g_kb, m_rpb,
                         m_w_proj_a, m_w_proj_b, m_w_o, m_w_ffn_in, m_w_ffn_out)))
    v = dict(zip(names, (v_w_ada, v_b_ada, v_g_norm1, v_g_norm2, v_w_in, v_b_gate, v_g_qa, v_g_ka, v_g_qb, v_g_kb, v_rpb,
                         v_w_proj_a, v_w_proj_b, v_w_o, v_w_ffn_in, v_w_ffn_out)))
    d = D_MODEL
    xi, yi, ci = _pos()
    chip = 2 * xi + yi
    me = 2 * chip + ci
    ada_cols = 6 * d // N_CHIP

    c_arr, chip_arr = ci.reshape(1).astype(jnp.int32), chip.reshape(1).astype(jnp.int32)
    first, rest = _BIG[:1], _BIG[1:]

    c_all = _small_allgather(c.reshape(8, d // 8), name="ag_c").reshape(N_DEV, d)
    b_sh = lax.dynamic_slice(b_ada, (0, chip * ada_cols), (1, ada_cols))
    mod_part = _ada_fwd(c_all, w_ada[0], b_sh, name="ada_fwd")
    mod_all = _small_allgather(mod_part, name="ag_mod").reshape(N_CHIP, 2, 8, ada_cols)[:, 0]
    mod = lax.dynamic_index_in_dim(mod_all, me, axis=1, keepdims=False).reshape(1, 6 * d)

    halves = {n: (2, _BIG_SHARD[n][0] // 2, _BIG_SHARD[n][1]) for n in _BIG}
    shards = {n: w[n][0].astype(BF16).reshape(halves[n]) for n in _BIG}
    land = lambda n: lax.empty((N_CHIP,) + halves[n], BF16)
    ag1 = _split_start([shards[n] for n in first], [land(n) for n in first], _ag_plan(1), 4, mod, name="ag1_start")
    ag2 = _split_start([shards[n] for n in rest], [land(n) for n in rest], _ag_plan_direct(len(rest)), 7 * len(rest),
                       ag1[4], name="ag2_start")
    rpb_after = rpb[0] + ag2[4][0, 0]

    def first_weights(after):
        after = after[:1, :1] + ag2[4][:1, :1]
        full1 = _split_wait(ag1[0], ag1[1], ag1[2], ag1[3], _ag_plan(1), after, name="ag1_wait")
        p_in = _ag_pass(full1, name="ag1_pass")[0].reshape((N_CHIP,) + _BIG_SHARD["w_in"])
        cut = W_QKV - 2 * _BIG_SHARD["w_in"][1]
        return dict(w_qkv=jnp.concatenate([p_in[0], p_in[1], p_in[2][:, :cut]], axis=1),
                    w_gates=jnp.concatenate([p_in[2][:, cut:], p_in[3]], axis=1))

    def late_weights(after):
        full2 = _split_wait(ag2[0], ag2[1], ag2[2], ag2[3], _ag_plan_direct(len(rest)), after, name="ag2_wait")
        full ={n: fu.reshape((N_CHIP,) + _BIG_SHARD[n]) for n, fu in zip(rest, full2)}
        return dict(w_pa=_join_cols(full["w_proj_a"]), w_pb=_join_cols(full["w_proj_b"]), w_o=full["w_o"].reshape(d, d),
                    w_ffn_in=_join_cols(full["w_ffn_in"]), w_ffn_out=full["w_ffn_out"].reshape(D_FF, d))

    def sib_begin(group, grads, tag):
        gps = [grads[n].reshape((N_CHIP,) + halves[n]) for n in group]
        lands = [lax.empty((N_CHIP,) + halves[n][1:], F32) for n in group]
        return _split_start(gps, lands, _sib_plan(len(group)), N_CHIP * len(group), gps[0], name=f"rs_sib_{tag}_start")

    def rs_begin(group, sib, after, tag):
        gps, ras = _split_wait(sib[0], sib[1], sib[2], sib[3], _sib_plan(len(group)), after,
                               name=f"rs_sib_{tag}_wait", with_sources=True)
        sums = [_rs_add(gp, ra, c_arr, name=f"rs_add_{n}") for n, gp, ra in zip(group, gps, ras)]
        lands = [lax.empty((3,) + halves[n][1:], BF16) for n in group]
        st = _split_start([sb for _, sb in sums], lands, _rs_plan(len(group)), 3 * len(group), sums[0][0],
                          name=f"rs_{tag}_start")
        return sums, st

    def rs_end(group, begun, after, tag):
        sums, st = begun
        rbs = _split_wait(st[0], st[1], st[2], st[3], _rs_plan(len(group)), after, name=f"rs_{tag}_wait")
        return [_rs_final(sf, rb, chip_arr, name=f"rs_final_{n}") for n, (sf, _), rb in zip(group, sums, rbs)]

    begun = {}

    def early_grads(grads):
        begun["sib_rest"] = sib_begin(rest, grads, "rest")
        return begun["sib_rest"][4]

    def mid_grads(after):
        begun["rest"] = rs_begin(rest, begun["sib_rest"], after, "rest")
        return begun["rest"][1][4]

    loss_v, grad_x, grads, dmod, small = _device_step(
        x[0], loss_target[0], mod, first_weights, late_weights, early_grads, mid_grads, g_norm1, g_norm2, b_gate, g_qa,
        g_ka, g_qb, g_kb, rpb_after)
    sib_first = sib_begin(first, grads, "first")

    g, delta, new_m, new_v = {}, {}, {}, {}

    def finish(group, ts, tag):
        others = _sibling_swap(ts, name=f"rs_pair_{tag}")
        for n, t, o in zip(group, ts, others):
            gg, dl, nm, nv = _adamw_halves(w[n][0], t, o, m[n][0], v[n][0], c_arr, name=f"adamw_{n}")
            g[n], delta[n], new_m[n], new_v[n] = gg[None], dl[None], nm[None], nv[None]

    finish(rest, rs_end(rest, begun["rest"], sib_first[4], "rest"), "rest")
    done_rest = sum(new_v[n][0, :1, :1] for n in rest)
    begun["first"] = rs_begin(first, sib_first, done_rest, "first")

    stats = _pack_small(dict(b_ada=dmod, **small)) + begun["first"][1][4][0, 0]
    stats = stats.at[:, STATS_W - 1].set(loss_v[0, 0])
    rows = _small_allgather(stats.reshape(8, STATS_W // 8), name="ag_stats").reshape(N_DEV, STATS_W)
    dmod_sh = lax.dynamic_slice(rows, (0, chip * ada_cols), (8, ada_cols))
    g_ada = _ada_bwd(c_all.T, dmod_sh, name="ada_bwd")
    tot = _row_sum(rows, name="stats_sum")
    g_small = _unpack_small(tot, {n: w[n].shape for n in _SMALL})

    finish(first, rs_end(first, begun["first"], tot, "first"), "first")

    dl, nm, nv = _adamw(w_ada[0], g_ada, m_w_ada[0], v_w_ada[0], name="adamw_w_ada")
    g["w_ada"], delta["w_ada"], new_m["w_ada"], new_v["w_ada"] = g_ada[None], dl[None], nm[None], nv[None]
    shapes = {n: w[n].shape for n in _SMALL}
    dl, nm, nv = _adamw(_pack_small({n: w[n] for n in _SMALL}), tot, _pack_small({n: m[n] for n in _SMALL}),
                        _pack_small({n: v[n] for n in _SMALL}), name="adamw_small")
    delta.update(_unpack_small(dl, shapes))
    new_m.update(_unpack_small(nm, shapes))
    new_v.update(_unpack_small(nv, shapes))
    g.update(g_small)

    loss = tot[0, STATS_W - 1]
    return (loss, grad_x[None], *[g[n] for n in names], *[delta[n] for n in names], *[new_m[n] for n in names],
            *[new_v[n] for n in names])
```

```python
import numpy as np

import jax
import jax.numpy as jnp
from jax import lax
from jax.experimental import pallas as pl
from jax.experimental.pallas import tpu as pltpu

F32 = jnp.float32
BF16 = jnp.bfloat16

D_MODEL = 1024
SEQ = 8192
HEAD_DIM = 64
GRID_W = 64
ROWS = SEQ // GRID_W
NA_HEADS = 8
NA_KH = 8
NA_KW = 16
DIL_CONFIGS = ((128, 1), (512, 4), (2048, 16))
ROT_DIM = 16
ROPE_THETA = 500000.0
D_FF = 2816
EPS = 1e-6
NEG = -1e30
WA = 512
WB = 768
WB_OUT = 256
W_QKV = 3 * WA + 3 * WB
W_QK = 2 * WA + 2 * WB
W_GATES = 2 * D_MODEL
SCALE = HEAD_DIM ** -0.5

ADAM_LR = 0.001
ADAM_B1 = 0.9
ADAM_B2 = 0.999
ADAM_EPS = 1e-08
ADAM_WD = 0.01
ADAM_STEP = 10

LANES = 128
ROW_TILE = 256
ROW_TILES = {"ffn_fwd": 512, "post_attn_fwd": 512, "post_attn_bwd": 512,
             "ffn_in_bwd": 512}
Q_BLOCK = 256
NA_QROWS = Q_BLOCK // GRID_W
NA_KROWS = NA_QROWS + NA_KH - 1
NA_NK = NA_KROWS * GRID_W
NA_PAIRS = (NA_KROWS + 1) // 2
NA_W = NA_PAIRS * LANES
NA_RO_NONE = 15
NA_SLOTS = 21
RP_LANE0 = GRID_W - NA_KW
DIL_HALF = 64
DIL_QB = 512
N_QBLK = SEQ // Q_BLOCK

N_DEV = 8
N_CHIP = 4
FF_CHIP = 2 * D_FF // N_CHIP
STATS_W = 14336


def _pcall(body, *, name, **kw):
    return pl.pallas_call(body, name=name, **kw)


_NT = (((1,), (1,)), ((), ()))
_TN = (((0,), (0,)), ((), ()))
_ARB = pltpu.CompilerParams(dimension_semantics=("arbitrary",))
_PAR = pltpu.CompilerParams(dimension_semantics=("parallel",))


def _dot(a, b):
    return jnp.dot(a, b, preferred_element_type=F32)


def _dot_nt(a, b):
    return lax.dot_general(a, b, _NT, preferred_element_type=F32)


def _wgrad(a, b, *, name, tm, tn, tk=1024, chips=None):
    s, ma = a.shape
    nb = b.shape[1]
    nk = s // tk
    nc = nb // chips if chips else tn
    cpb = tn // nc

    def body(a_ref, b_ref, o_ref, acc):
        k = pl.program_id(2)
        r = lax.dot_general(a_ref[...].astype(BF16), b_ref[...].astype(BF16), _TN, preferred_element_type=F32)

        @pl.when(k == 0)
        def _():
            acc[...] = r

        @pl.when(k > 0)
        def _():
            acc[...] += r

        @pl.when(k == nk - 1)
        def _():
            if chips:
                for q in range(cpb):
                    o_ref[q] = acc[:, q * nc:(q + 1) * nc]
            else:
                o_ref[...] = acc[...]

    if chips:
        o_spec = pl.BlockSpec((cpb, tm, nc), lambda i, j, k: (j, i, 0))
        out_shape = jax.ShapeDtypeStruct((chips, ma, nc), F32)
    else:
        o_spec = pl.BlockSpec((tm, tn), lambda i, j, k: (i, j))
        out_shape = jax.ShapeDtypeStruct((ma, nb), F32)
    return _pcall(
        body, name=name, grid=(ma // tm, nb // tn, nk),
        in_specs=[pl.BlockSpec((tk, tm), lambda i, j, k: (k, i)), pl.BlockSpec((tk, tn), lambda i, j, k: (k, j))],
        out_specs=o_spec, out_shape=out_shape, scratch_shapes=[pltpu.VMEM((tm, tn), F32)],
        compiler_params=pltpu.CompilerParams(dimension_semantics=("parallel", "parallel", "arbitrary")),
    )(a, b)


def _row_call(body, *, name, row_ins, res_ins, row_outs, acc_outs=(), scratch=()):
    row_ins = [a if isinstance(a, tuple) else (a, 1) for a in row_ins]
    row_outs = [o if len(o) == 3 else (*o, 1) for o in row_outs]
    s = row_ins[0][0].shape[0]
    tile = ROW_TILES.get(name, ROW_TILE)
    n = s // tile
    nri, nre, nro, nao = len(row_ins), len(res_ins), len(row_outs), len(acc_outs)

    def whole(shape):
        nd = len(shape)
        return pl.BlockSpec(tuple(shape), lambda i: (0,) * nd, pipeline_mode=pl.Buffered(1))

    def whole_out(shape):
        nd = len(shape)
        return pl.BlockSpec(tuple(shape), lambda i: (0,) * nd)

    def rows(w, d):
        if d == 1:
            return pl.BlockSpec((tile, w), lambda i: (i, 0))
        return pl.BlockSpec((d, tile // d, w), lambda i: (0, i, 0))

    in_specs = [rows(a.shape[1], d) for a, d in row_ins]
    in_specs += [whole(a.shape) for a in res_ins]
    out_specs = [rows(w, d) for w, _, d in row_outs]
    out_specs += [whole_out(shp) for shp, _ in acc_outs]
    out_shape = [jax.ShapeDtypeStruct((s, w) if d == 1 else (d, s // d, w), dt) for w, dt, d in row_outs]
    out_shape += [jax.ShapeDtypeStruct(tuple(shp), dt) for shp, dt in acc_outs]

    def wrapped(*refs):
        at = [0, nri, nri + nre, nri + nre + nro, nri + nre + nro + nao]
        body(pl.program_id(0), n, refs[at[0]:at[1]], refs[at[1]:at[2]], refs[at[2]:at[3]], refs[at[3]:at[4]],
             refs[at[4]:])

    args = [a if d == 1 else a.reshape(d, s // d, a.shape[1]) for a, d in row_ins]
    outs = _pcall(wrapped, name=name, grid=(n,), in_specs=in_specs, out_specs=out_specs, out_shape=out_shape,
                  scratch_shapes=list(scratch), compiler_params=_ARB)(*args, *res_ins)
    return [o.reshape(s, o.shape[-1]) if k < nro and row_outs[k][2] != 1 else o for k, o in enumerate(outs)]


def _stage_shape(name):
    return pltpu.VMEM((4, ROW_TILES.get(name, ROW_TILE), LANES), F32)


def _from_residue(ref, col, stage, slot):
    d, n = ref.shape[0], ref.shape[1]
    for r in range(d):
        stage.at[slot][pl.ds(r, n, stride=d), :] = ref[r, :, col:col + LANES].astype(F32)
    return stage[slot]


def _natural(ref, stage, slot0):
    if len(ref.shape) == 2:
        return ref[...]
    return jnp.concatenate([_from_residue(ref, c * LANES, stage, (slot0 + c) % 4)
                            for c in range(ref.shape[2] // LANES)], axis=1)


def _to_residue(val, ref, col, stage, slot):
    d, n = ref.shape[0], ref.shape[1]
    stage[slot] = val
    for r in range(d):
        ref[r, :, col:col + LANES] = stage.at[slot][pl.ds(r, n, stride=d), :].astype(ref.dtype)


def _fold8(t):
    r, w = t.shape
    return jnp.sum(t.reshape(r // 8, 8, w), axis=0)


def _sigmoid(t):
    return 0.5 * (jnp.tanh(0.5 * t) + 1.0)


def _head_lanes():
    return lax.broadcasted_iota(jnp.int32, (1, LANES), 1) < HEAD_DIM


def _head_mean(t, lo):
    s_lo = jnp.sum(jnp.where(lo, t, 0.0), axis=1, keepdims=True)
    s_hi = jnp.sum(jnp.where(lo, 0.0, t), axis=1, keepdims=True)
    return jnp.where(lo, s_lo, s_hi) * (1.0 / HEAD_DIM)


def _rms_mod(xv, g, sc, sh):
    rstd = lax.rsqrt(jnp.mean(xv * xv, axis=1, keepdims=True) + EPS)
    return (xv * rstd * g) * (1.0 + sc) + sh


def _rms_mod_bwd(xv, dh, g, sc):
    rstd = lax.rsqrt(jnp.mean(xv * xv, axis=1, keepdims=True) + EPS)
    xhat = xv * rstd
    dn = dh * (1.0 + sc)
    dxhat = dn * g
    dx = rstd * (dxhat - xhat * jnp.mean(dxhat * xhat, axis=1, keepdims=True))
    return dx, dh, dh * (xhat * g), dn * xhat


def _mix_weights(ls):
    m = jnp.maximum(jnp.maximum(ls[0], ls[1]), ls[2])
    es = [jnp.exp(t - m) for t in ls]
    den = es[0] + es[1] + es[2]
    return [e / den for e in es]


def _rope_tables():
    half = ROT_DIM // 2
    inv_freq = ROPE_THETA ** (-(jnp.arange(half, dtype=F32) * 2.0) / ROT_DIM)
    lane = np.arange(LANES) % HEAD_DIM
    ang = jnp.arange(SEQ).astype(F32)[:, None] * jnp.tile(inv_freq, LANES // half)[None, :]
    cos, sin = jnp.cos(ang), jnp.sin(ang)
    first, second = jnp.asarray(lane < half)[None, :], jnp.asarray((lane >= half) & (lane < ROT_DIM))[None, :]
    cos_t = jnp.where(first | second, cos, 1.0)
    return cos_t, jnp.where(second, sin, 0.0), jnp.where(first, -sin, 0.0)


_SECTIONS = ((0, WA, 0, False), (WA, 2 * WA, 1, False), (2 * WA, 3 * WA, -1, False),
             (3 * WA, 3 * WA + WB, 2, True), (3 * WA + WB, 3 * WA + 2 * WB, 3, True), (3 * WA + 2 * WB, W_QKV, -1, False))


def _pre_attn_fwd(x, cos_t, sa_t, sb_t, g1, sc1, sh1, w_qkv, w_gates, gains, *, name):
    half = ROT_DIM // 2
    dilated = [(g, dd) for g, (_, dd) in enumerate(DIL_CONFIGS) if dd > 1]

    def body(i, n, rin, res, rout, aout, scr):
        x_ref, cos_ref, sa_ref, sb_ref = rin
        g_ref, sc_ref, sh_ref, wq_ref, wg_ref, gains_ref = res
        h1_ref, qkvn_ref, pre_ref, gates_ref = rout[:4]
        group_ref = {g: rout[4 + k] for k, (g, _) in enumerate(dilated)}
        (stage,) = scr
        staged = 0
        hb = _rms_mod(x_ref[...], g_ref[...], sc_ref[...], sh_ref[...]).astype(BF16)
        h1_ref[...] = hb
        gates_ref[...] = _dot(hb, wg_ref[...]).astype(BF16)
        lo = _head_lanes()
        cosv, sav, sbv = cos_ref[...], sa_ref[...], sb_ref[...]
        pre_at = 0
        for si, (c0, c1, kind, rot) in enumerate(_SECTIONS):
            sec = _dot(hb, wq_ref[:, c0:c1])
            for ch in range((c1 - c0) // LANES):
                t = sec[:, ch * LANES:(ch + 1) * LANES]
                if kind >= 0:
                    pre_ref[:, pre_at:pre_at + LANES] = t.astype(BF16)
                    pre_at += LANES
                    t = t * lax.rsqrt(_head_mean(t * t, lo) + EPS) * gains_ref[kind:kind + 1, :]
                    if rot:
                        t = t * cosv + pltpu.roll(t, half, 1) * sav + pltpu.roll(t, LANES - half, 1) * sbv
                qkvn_ref[:, c0 + ch * LANES:c0 + (ch + 1) * LANES] = t.astype(BF16)
                group = ch * LANES // WB_OUT if si >= 3 else 0
                if group in group_ref:
                    col = (si - 3) * WB_OUT + ch * LANES % WB_OUT
                    _to_residue(t, group_ref[group], col, stage, staged % 4)
                    staged += 1

    return _row_call(body, name=name, row_ins=[x, cos_t, sa_t, sb_t], res_ins=[g1, sc1, sh1, w_qkv, w_gates, gains],
                     row_outs=[(D_MODEL, BF16), (W_QKV, BF16), (W_QK, BF16), (W_GATES, BF16)]
                     + [(3 * WB_OUT, BF16, dd) for _, dd in dilated], scratch=[_stage_shape(name)])


def _pre_attn_bwd(qk_pre, d_parts, dgates, x, dx1, cos_t, sa_t, sb_t, w_qkv, w_gates, gains, g1, sc1, *, name):
    half = ROT_DIM // 2
    nparts = len(d_parts)
    where = []
    residue = [isinstance(part, tuple) for part in d_parts]
    for pi, part in enumerate(d_parts):
        width = (part[0] if residue[pi] else part).shape[1]
        where += [(pi, cj) for cj in range(width // LANES)]
    assert len(where) == W_QKV // LANES

    def body(i, n, rin, res, rout, aout, scr):
        pre_ref, d_refs = rin[0], rin[1:1 + nparts]
        dgates_ref, x_ref, dx1_ref, cos_ref, sa_ref, sb_ref = rin[1 + nparts:]
        wq_ref, wg_ref, gains_ref, g_ref, sc_ref = res
        dqkv_ref, gx_ref = rout
        dgains_ref, sums_ref = aout
        accg, accs, stage = scr
        staged = 0

        @pl.when(i == 0)
        def _():
            accg[...] = jnp.zeros_like(accg)
            accs[...] = jnp.zeros_like(accs)

        lo = _head_lanes()
        cosv, sav, sbv = cos_ref[...], sa_ref[...], sb_ref[...]
        dh = _dot_nt(dgates_ref[...], wg_ref[...])
        pre_at = 0
        for c0, c1, kind, rot in _SECTIONS:
            for ch in range((c1 - c0) // LANES):
                pi, cj = where[c0 // LANES + ch]
                if residue[pi]:
                    dt = _from_residue(d_refs[pi], cj * LANES, stage, staged % 4)
                    staged += 1
                else:
                    dt = d_refs[pi][:, cj * LANES:(cj + 1) * LANES]
                if kind >= 0:
                    if rot:
                        dt = dt * cosv + pltpu.roll(dt * sav, LANES - half, 1) + pltpu.roll(dt * sbv, half, 1)
                    t = pre_ref[:, pre_at:pre_at + LANES].astype(F32)
                    pre_at += LANES
                    rstd = lax.rsqrt(_head_mean(t * t, lo) + EPS)
                    xhat = t * rstd
                    accg[kind] += _fold8(dt * xhat)
                    dxhat = dt * gains_ref[kind:kind + 1, :]
                    dt = rstd * (dxhat - xhat * _head_mean(dxhat * xhat, lo))
                dqkv_ref[:, c0 + ch * LANES:c0 + (ch + 1) * LANES] = dt.astype(BF16)
            dh = dh + _dot_nt(dqkv_ref[:, c0:c1], wq_ref[:, c0:c1])
        dx, t_sh, t_sc, t_g = _rms_mod_bwd(x_ref[...], dh, g_ref[...], sc_ref[...])
        gx_ref[...] = dx1_ref[...] + dx
        accs[0] += _fold8(t_sh)
        accs[1] += _fold8(t_sc)
        accs[2] += _fold8(t_g)

        @pl.when(i == n - 1)
        def _():
            t = jnp.sum(accg[...], axis=1)
            dgains_ref[...] = t + pltpu.roll(t, HEAD_DIM, 1)
            sums_ref[...] = jnp.sum(accs[...], axis=1)

    return _row_call(
        body, name=name, row_ins=[qk_pre, *d_parts, dgates, x, dx1, cos_t, sa_t, sb_t],
        res_ins=[w_qkv, w_gates, gains, g1, sc1], row_outs=[(W_QKV, BF16), (D_MODEL, F32)],
        acc_outs=[((4, LANES), F32), ((3, D_MODEL), F32)],
        scratch=[pltpu.VMEM((4, 8, LANES), F32), pltpu.VMEM((3, 8, D_MODEL), F32), _stage_shape(name)])


def _post_attn_fwd(o_a, o_g, l_g, gates, x, w_pa, w_pb, w_o, b_gate, gt1, g2, sc2, sh2, *, name):
    d = D_MODEL

    def body(i, n, rin, res, rout, aout, scr):
        oa_ref, o0, o1, o2, l0, l1, l2, gates_ref, x_ref = rin
        wpa_ref, wpb_ref, wo_ref, b_ref, gt_ref, g_ref, sc_ref, sh_ref = res
        ob_ref, merged_ref, mo_ref, x1_ref, h2_ref = rout
        (stage,) = scr
        ogs = [_natural(r, stage, 0) for r in (o0, o1, o2)]
        ws = _mix_weights([_natural(r, stage, 2) for r in (l0, l1, l2)])
        obb = (ws[0] * ogs[0] + ws[1] * ogs[1] + ws[2] * ogs[2]).astype(BF16)
        ob_ref[...] = obb
        pa = _dot(oa_ref[...].astype(BF16), wpa_ref[...])
        pb = _dot(obb, wpb_ref[...])
        ga = _sigmoid(gates_ref[:, :d].astype(F32) + b_ref[:, :d])
        gb = _sigmoid(gates_ref[:, d:].astype(F32) + b_ref[:, d:])
        merged = (ga * pa + gb * pb).astype(BF16)
        merged_ref[...] = merged
        mo = _dot(merged, wo_ref[...])
        mo_ref[...] = mo.astype(BF16)
        x1 = x_ref[...] + gt_ref[...] * mo
        x1_ref[...] = x1
        h2_ref[...] = _rms_mod(x1, g_ref[...], sc_ref[...], sh_ref[...]).astype(BF16)

    return _row_call(body, name=name, row_ins=[o_a, *o_g, *l_g, gates, x],
                     res_ins=[w_pa, w_pb, w_o, b_gate, gt1, g2, sc2, sh2],
                     row_outs=[(WB_OUT, BF16), (d, BF16), (d, BF16), (d, F32), (d, BF16)], scratch=[_stage_shape(name)])


def _ffn_fwd(h2, w_ffn_in, *, name):
    def body(i, n, rin, res, rout, aout, scr):
        (h_ref,), (w_ref,), (act_ref, ff_ref) = rin, res, rout
        hv = h_ref[...]
        for q in range(2):
            a = _dot(hv, w_ref[:, q * FF_CHIP:(q + 1) * FF_CHIP])
            up = _dot(hv, w_ref[:, D_FF + q * FF_CHIP:D_FF + (q + 1) * FF_CHIP])
            sl = slice(q * FF_CHIP, (q + 1) * FF_CHIP)
            act_ref[:, sl] = (a * _sigmoid(a) * up).astype(BF16)
            ff_ref[:, sl] = a.astype(BF16)
            ff_ref[:, D_FF + q * FF_CHIP:D_FF + (q + 1) * FF_CHIP] = up.astype(BF16)

    return _row_call(body, name=name, row_ins=[h2], res_ins=[w_ffn_in], row_outs=[(D_FF, BF16), (2 * D_FF, BF16)])


def _ffn_mid(act, ff, x1, tgt, w_ffn_out, gt2, *, name):
    d = D_MODEL

    def body(i, n, rin, res, rout, aout, scr):
        act_ref, ff_ref, x1_ref, tgt_ref = rin
        wo_ref, gt_ref = res
        dy_ref, dffo_ref, dff_ref = rout
        dgt_ref, loss_ref = aout
        (acc,) = scr

        @pl.when(i == 0)
        def _():
            acc[...] = jnp.zeros_like(acc)

        ffo = _dot(act_ref[...], wo_ref[...])
        gtv = gt_ref[...]
        e = x1_ref[...] + gtv * ffo - tgt_ref[...]
        dy = e * (1.0 / d)
        dy_ref[...] = dy
        dffo = (gtv * dy).astype(BF16)
        dffo_ref[...] = dffo
        acc[0] += _fold8(dy * ffo)
        acc[1] += _fold8(e * e)
        for q in range(2):
            sl = slice(q * FF_CHIP, (q + 1) * FF_CHIP)
            su = slice(D_FF + q * FF_CHIP, D_FF + (q + 1) * FF_CHIP)
            dact = _dot_nt(dffo, wo_ref[sl, :])
            a = ff_ref[:, sl].astype(F32)
            up = ff_ref[:, su].astype(F32)
            sg = _sigmoid(a)
            dff_ref[:, sl] = (dact * up * (sg * (1.0 + a * (1.0 - sg)))).astype(BF16)
            dff_ref[:, su] = (dact * (a * sg)).astype(BF16)

        @pl.when(i == n - 1)
        def _():
            dgt_ref[...] = jnp.sum(acc[0], axis=0, keepdims=True)
            tot = jnp.sum(jnp.sum(acc[1], axis=0, keepdims=True), axis=1, keepdims=True)
            loss_ref[...] = jnp.broadcast_to(tot * (0.5 / d), (1, LANES))

    return _row_call(body, name=name, row_ins=[act, ff, x1, tgt], res_ins=[w_ffn_out, gt2],
                     row_outs=[(d, F32), (d, BF16), (2 * D_FF, BF16)], acc_outs=[((1, d), F32), ((1, LANES), F32)],
                     scratch=[pltpu.VMEM((2, 8, d), F32)])


def _ffn_in_bwd(dff, x1, dy, mo, w_ffn_in, g2, sc2, gt1, *, name):
    d = D_MODEL

    def body(i, n, rin, res, rout, aout, scr):
        dff_ref, x1_ref, dy_ref, mo_ref = rin
        w_ref, g_ref, sc_ref, gt_ref = res
        dx1_ref, dmo_ref = rout
        (sums_ref,) = aout
        (acc,) = scr

        @pl.when(i == 0)
        def _():
            acc[...] = jnp.zeros_like(acc)

        dh = _dot_nt(dff_ref[...], w_ref[...])
        dx, t_sh, t_sc, t_g = _rms_mod_bwd(x1_ref[...], dh, g_ref[...], sc_ref[...])
        dx1 = dy_ref[...] + dx
        dx1_ref[...] = dx1
        dmo_ref[...] = (gt_ref[...] * dx1).astype(BF16)
        acc[0] += _fold8(t_sh)
        acc[1] += _fold8(t_sc)
        acc[2] += _fold8(t_g)
        acc[3] += _fold8(dx1 * mo_ref[...].astype(F32))

        @pl.when(i == n - 1)
        def _():
            sums_ref[...] = jnp.sum(acc[...], axis=1)

    return _row_call(body, name=name, row_ins=[dff, x1, dy, mo], res_ins=[w_ffn_in, g2, sc2, gt1],
                     row_outs=[(d, F32), (d, BF16)], acc_outs=[((4, d), F32)], scratch=[pltpu.VMEM((4, 8, d), F32)])


def _post_attn_bwd(dmo, gates, o_a, o_g, l_g, w_pa, w_pb, w_o, b_gate, *, name):
    d = D_MODEL

    def body(i, n, rin, res, rout, aout, scr):
        dmo_ref, gates_ref, oa_ref, o0, o1, o2, l0, l1, l2 = rin
        wpa_ref, wpb_ref, wo_ref, b_ref = res
        dpa_ref, dpb_ref, dgates_ref, doa_ref = rout[:4]
        do_refs, dl_refs = rout[4:7], rout[7:10]
        (dbg_ref,) = aout
        acc, stage = scr

        @pl.when(i == 0)
        def _():
            acc[...] = jnp.zeros_like(acc)

        ogs = [_natural(r, stage, 0) for r in (o0, o1, o2)]
        ws = _mix_weights([_natural(r, stage, 2) for r in (l0, l1, l2)])
        obb = (ws[0] * ogs[0] + ws[1] * ogs[1] + ws[2] * ogs[2]).astype(BF16)
        pa = _dot(oa_ref[...].astype(BF16), wpa_ref[...])
        pb = _dot(obb, wpb_ref[...])
        ga = _sigmoid(gates_ref[:, :d].astype(F32) + b_ref[:, :d])
        gb = _sigmoid(gates_ref[:, d:].astype(F32) + b_ref[:, d:])
        dm = _dot_nt(dmo_ref[...], wo_ref[...])
        dpa = (dm * ga).astype(BF16)
        dpb = (dm * gb).astype(BF16)
        dpa_ref[...] = dpa
        dpb_ref[...] = dpb
        dga = dm * pa * ga * (1.0 - ga)
        dgb = dm * pb * gb * (1.0 - gb)
        dgates_ref[:, :d] = dga.astype(BF16)
        dgates_ref[:, d:] = dgb.astype(BF16)
        acc[:, :d] += _fold8(dga)
        acc[:, d:] += _fold8(dgb)
        doa_ref[...] = _dot_nt(dpa, wpa_ref[...])
        dob = _dot_nt(dpb, wpb_ref[...])
        lo = _head_lanes()
        for ch in range(WB_OUT // LANES):
            sl = slice(ch * LANES, (ch + 1) * LANES)
            dv = dob[:, sl]
            wc = [w[:, sl] for w in ws]
            ts = [_head_mean(dv * og[:, sl], lo) * float(HEAD_DIM) for og in ogs]
            tbar = wc[0] * ts[0] + wc[1] * ts[1] + wc[2] * ts[2]
            for g in range(3):
                for k, (ref, val) in enumerate(((do_refs[g], wc[g] * dv), (dl_refs[g], wc[g] * (ts[g] - tbar)))):
                    if len(ref.shape) == 2:
                        ref[:, sl] = val
                    else:
                        _to_residue(val, ref, ch * LANES, stage, (2 * g + k) % 4)

        @pl.when(i == n - 1)
        def _():
            dbg_ref[...] = jnp.sum(acc[...], axis=0, keepdims=True)

    return _row_call(body, name=name, row_ins=[dmo, gates, o_a, *o_g, *l_g], res_ins=[w_pa, w_pb, w_o, b_gate],
                     row_outs=[(d, BF16), (d, BF16), (2 * d, BF16), (WA, F32)]
                     + 2 * [(WB_OUT, F32, dd) for _, dd in DIL_CONFIGS],
                     acc_outs=[((1, 2 * d), F32)], scratch=[pltpu.VMEM((8, 2 * d), F32), _stage_shape(name)])


def _na_class_tables():
    ro = np.full((3, NA_QROWS, 2 * NA_PAIRS), NA_RO_NONE, np.int64)
    slot = np.zeros((3, NA_QROWS, NA_PAIRS), np.int64)
    for t in range(3):
        for a in range(NA_QROWS):
            qr = _NA_CLASS_R0[t] + a
            rs = min(max(qr - NA_KH // 2, 0), ROWS - NA_KH)
            for b in range(NA_KROWS):
                kr = _NA_CLASS_K0[t] + b
                if rs <= kr < rs + NA_KH:
                    ro[t, a, b] = kr - qr + (NA_KH - 1)
            for j in range(NA_PAIRS):
                slot[t, a, j] = 2 * j - a + (_NA_CLASS_K0[t] - _NA_CLASS_R0[t] + NA_KH - 1) + (NA_QROWS - 1)
    assert slot.min() >= 0 and slot.max() < NA_SLOTS
    return ro, slot


def _na_build_bias(i, cls_ref, rp_ref, cm_ref, bias_scr):
    ro, _ = _na_class_tables()
    lo = _head_lanes()
    first = jnp.logical_or(i == 0, cls_ref[i] != cls_ref[jnp.maximum(i - 1, 0)])
    for t in range(3):
        @pl.when(jnp.logical_and(first, cls_ref[i] == t))
        def _():
            for hh in range(2):
                for a in range(NA_QROWS):
                    for j in range(NA_PAIRS):
                        r0, r1 = int(ro[t, a, 2 * j]), int(ro[t, a, 2 * j + 1])
                        x0 = jnp.broadcast_to(rp_ref[hh, r0:r0 + 1, :], (GRID_W, LANES))
                        x1 = jnp.broadcast_to(rp_ref[hh, r1:r1 + 1, :], (GRID_W, LANES))
                        blk = jnp.where(lo, pltpu.roll(x0, GRID_W + 1, 1, stride=1, stride_axis=0),
                                        pltpu.roll(x1, 1, 1, stride=1, stride_axis=0))
                        bias_scr[hh, a * GRID_W:(a + 1) * GRID_W, j * LANES:(j + 1) * LANES] = blk + cm_ref[...]
    return first


def _attn_fwd(qkv, qc0, kc0, vc0, npairs, table, kstart, cls, nk, *, name, na=None, qb=Q_BLOCK):
    s = qkv.shape[0]

    def body(ks_ref, cls_ref, q_ref, k_ref, v_ref, b_ref, *rest):
        if na:
            cm_ref, o_ref, lse_ref, bias_scr = rest
        else:
            o_ref, lse_ref = rest
        i = pl.program_id(1)
        if na:
            _na_build_bias(i, cls_ref, b_ref, cm_ref, bias_scr)
        ks = pl.multiple_of(ks_ref[i], 64)
        q2 = q_ref[...]
        k2 = k_ref[pl.ds(ks, nk), :]
        v2 = v_ref[pl.ds(ks, nk), :]
        lo = _head_lanes()
        outs, lses = [], []
        for h in range(2):
            qm = jnp.where(lo if h == 0 else jnp.logical_not(lo), q2, jnp.zeros_like(q2))
            sc = _dot_nt(qm, k2) * SCALE + (bias_scr[h, :, :nk] if na else b_ref[0, 0])
            m = jnp.max(sc, axis=1, keepdims=True)
            p = jnp.exp(sc - m)
            l = jnp.sum(p, axis=1, keepdims=True)
            pv = _dot(p.astype(BF16), v2)
            outs.append(pv / l)
            lses.append(m + jnp.log(l))
        o_ref[...] = jnp.where(lo, outs[0], outs[1])
        lse_ref[...] = jnp.where(lo, lses[0], lses[1])

    w = npairs * LANES
    in_specs = [
        pl.BlockSpec((qb, LANES), lambda p, i, ks, cl: (i, qc0 + p)),
        pl.BlockSpec((s, LANES), lambda p, i, ks, cl: (0, kc0 + p)),
        pl.BlockSpec((s, LANES), lambda p, i, ks, cl: (0, vc0 + p)),
    ]
    if na:
        in_specs += _na_bias_specs()
        args, scratch = (kstart, cls, qkv, qkv, qkv, *na), [pltpu.VMEM((2, Q_BLOCK, NA_W), F32)]
    else:
        in_specs.append(pl.BlockSpec((1, 1, qb, nk), lambda p, i, ks, cl: (cl[i], 0, 0, 0)))
        args, scratch = (kstart, cls, qkv, qkv, qkv, table), []
    grid_spec = pltpu.PrefetchScalarGridSpec(
        num_scalar_prefetch=2, grid=(npairs, s // qb), in_specs=in_specs,
        out_specs=[pl.BlockSpec((qb, LANES), lambda p, i, ks, cl: (i, p)),
                   pl.BlockSpec((qb, LANES), lambda p, i, ks, cl: (i, p))],
        scratch_shapes=scratch,
    )
    return _pcall(body, name=name, grid_spec=grid_spec,
                  out_shape=[jax.ShapeDtypeStruct((s, w), F32), jax.ShapeDtypeStruct((s, w), F32)],
                  compiler_params=pltpu.CompilerParams(dimension_semantics=("parallel", "arbitrary")),
                  )(*args)


def _na_bias_specs():
    return [pl.BlockSpec((2, 16, LANES), lambda p, i, ks, cl: (p, 0, 0)),
            pl.BlockSpec((GRID_W, LANES), lambda p, i, ks, cl: (0, 0))]


def _attn_bwd(qkv, qc0, kc0, vc0, npairs, table, kstart, cls, nk, do, o, lse, *, name, dlse=None, na=None,
              qb=Q_BLOCK, order=None):
    s = qkv.shape[0]
    has_dlse = dlse is not None
    _, slot = _na_class_tables()

    def body(ks_ref, cls_ref, q_ref, k_ref, v_ref, b_ref, *rest):
        if na:
            cm_ref, rest = rest[0], rest[1:]
        do_ref, o_ref, lse_ref, rest = rest[0], rest[1], rest[2], rest[3:]
        if has_dlse:
            dlse_ref, rest = rest[0], rest[1:]
        if order is not None:
            rest = rest[1:]
        dq_ref, dk_ref, dv_ref = rest[0], rest[1], rest[2]
        if na:
            bank_ref, bias_scr, dbias_scr, bank_scr = rest[3:]
        i = pl.program_id(1)
        if na:
            first = _na_build_bias(i, cls_ref, b_ref, cm_ref, bias_scr)

        @pl.when(i == 0)
        def _():
            dk_ref[...] = jnp.zeros_like(dk_ref)
            dv_ref[...] = jnp.zeros_like(dv_ref)
            if na:
                dbias_scr[...] = jnp.zeros_like(dbias_scr)
                bank_scr[...] = jnp.zeros_like(bank_scr)

        ks = pl.multiple_of(ks_ref[i], 64)
        q2 = q_ref[...]
        k2 = k_ref[pl.ds(ks, nk), :]
        v2 = v_ref[pl.ds(ks, nk), :]
        do2 = do_ref[...]
        lse2 = lse_ref[...]
        doo = do2 * o_ref[...]
        do2b = do2.astype(BF16)
        lo = _head_lanes()
        lane = lax.broadcasted_iota(jnp.int32, (1, LANES), 1)
        dqs, dks, dvs = [], [], []
        for h in range(2):
            mh = lo if h == 0 else jnp.logical_not(lo)
            qm = jnp.where(mh, q2, jnp.zeros_like(q2))
            sc = _dot_nt(qm, k2) * SCALE + (bias_scr[h, :, :nk] if na else b_ref[0, 0])
            lse_h = jnp.max(jnp.where(mh, lse2, NEG), axis=1, keepdims=True)
            p = jnp.exp(sc - lse_h)
            delta = jnp.sum(jnp.where(mh, doo, 0.0), axis=1, keepdims=True)
            dom = jnp.where(mh, do2b, jnp.zeros_like(do2b))
            dp = _dot_nt(dom, v2)
            t = dp - delta
            if has_dlse:
                t = t + jnp.sum(jnp.where(lane == h * HEAD_DIM, dlse_ref[...], 0.0), axis=1, keepdims=True)
            ds = p * t
            if na:
                @pl.when(first)
                def _():
                    dbias_scr[h, :, :nk] = ds

                @pl.when(jnp.logical_not(first))
                def _():
                    dbias_scr[h, :, :nk] += ds
            dsb = ds.astype(BF16)
            dqs.append(_dot(dsb, k2))
            dks.append(lax.dot_general(dsb, q2, _TN, preferred_element_type=F32))
            dvs.append(lax.dot_general(p.astype(BF16), do2b, _TN, preferred_element_type=F32))
        dq_ref[...] = jnp.where(lo, dqs[0], dqs[1]) * SCALE
        dk_ref[pl.ds(ks, nk), :] += jnp.where(lo, dks[0], dks[1]) * SCALE
        dv_ref[pl.ds(ks, nk), :] += jnp.where(lo, dvs[0], dvs[1])
        if na:
            last = jnp.logical_or(i == N_QBLK - 1, cls_ref[i] != cls_ref[jnp.minimum(i + 1, N_QBLK - 1)])
            for t in range(3):
                @pl.when(jnp.logical_and(last, cls_ref[i] == t))
                def _():
                    for hh in range(2):
                        for a in range(NA_QROWS):
                            for j in range(NA_PAIRS):
                                bank_scr[hh, int(slot[t, a, j])] += dbias_scr[
                                    hh, a * GRID_W:(a + 1) * GRID_W, j * LANES:(j + 1) * LANES]

            @pl.when(i == N_QBLK - 1)
            def _():
                bank_ref[...] = bank_scr[...]

    w = npairs * LANES
    blk = lambda: pl.BlockSpec((qb, LANES), lambda p, i, ks, cl: (i, p))
    full = lambda: pl.BlockSpec((s, LANES), lambda p, i, ks, cl: (0, p))
    in_specs = [
        pl.BlockSpec((qb, LANES), lambda p, i, ks, cl: (i, qc0 + p)),
        pl.BlockSpec((s, LANES), lambda p, i, ks, cl: (0, kc0 + p)),
        pl.BlockSpec((s, LANES), lambda p, i, ks, cl: (0, vc0 + p)),
    ]
    if na:
        in_specs += _na_bias_specs()
        args = [kstart, cls, qkv, qkv, qkv, *na]
    else:
        in_specs.append(pl.BlockSpec((1, 1, qb, nk), lambda p, i, ks, cl: (cl[i], 0, 0, 0)))
        args = [kstart, cls, qkv, qkv, qkv, table]
    in_specs += [blk(), blk(), blk()]
    args += [do, o, lse]
    if has_dlse:
        in_specs.append(blk())
        args.append(dlse)
    out_specs = [blk(), full(), full()]
    out_shape = [jax.ShapeDtypeStruct((s, w), F32)] * 3
    scratch = []
    if na:
        bank_shape = (2, NA_SLOTS, GRID_W, LANES)
        out_specs.append(pl.BlockSpec(bank_shape, lambda p, i, ks, cl: (p, 0, 0, 0)))
        out_shape.append(jax.ShapeDtypeStruct((2 * npairs,) + bank_shape[1:], F32))
        scratch = [pltpu.VMEM((2, Q_BLOCK, NA_W), F32), pltpu.VMEM((2, Q_BLOCK, NA_W), F32), pltpu.VMEM(bank_shape, F32)]
    if order is not None:
        in_specs.append(pl.BlockSpec(order.shape, lambda p, i, ks, cl: (0, 0)))
        args.append(order)
    grid_spec = pltpu.PrefetchScalarGridSpec(num_scalar_prefetch=2, grid=(npairs, s // qb), in_specs=in_specs,
                                             out_specs=out_specs, scratch_shapes=scratch)
    return _pcall(body, name=name, grid_spec=grid_spec, out_shape=out_shape,
                  compiler_params=pltpu.CompilerParams(dimension_semantics=("arbitrary", "arbitrary")))(*args)


_NA_CLASS_R0 = (0, NA_QROWS, ROWS - NA_QROWS)
_NA_CLASS_K0 = (0, 0, ROWS - NA_KROWS)
_RPB_RO = 2 * NA_KH - 1
_RPB_CO = 2 * NA_KW - 1
_BANK_ROWS = 48


def _na_constants():
    col = np.arange(GRID_W)
    cs = np.clip(col - NA_KW // 2, 0, GRID_W - NA_KW)
    vcol = (col[None, :] >= cs[:, None]) & (col[None, :] < cs[:, None] + NA_KW)
    colmask = np.where(np.concatenate([vcol, vcol], axis=1), 0.0, NEG).astype(np.float32)
    co = col[None, :] - col[:, None] + (NA_KW - 1)
    oh_col = np.zeros((GRID_W * GRID_W, LANES), np.float32)
    for qc in range(GRID_W):
        for kc in range(GRID_W):
            if vcol[qc, kc]:
                oh_col[qc * GRID_W + kc, co[qc, kc]] = 1.0
    ks = np.clip(np.arange(N_QBLK) * NA_QROWS - NA_KH // 2, 0, ROWS - NA_KROWS) * GRID_W
    cls = np.ones(N_QBLK, np.int32)
    cls[0], cls[-1] = 0, 2
    return colmask, oh_col, ks.astype(np.int32), cls


def _bank_reduce(bank, oh_col, *, name):
    def body(d_ref, ohc_ref, o_ref):
        o_ref[0] = jnp.dot(d_ref[0], ohc_ref[...], preferred_element_type=F32, precision=lax.Precision.HIGHEST)

    return _pcall(
        body, name=name, grid=(NA_HEADS,),
        in_specs=[pl.BlockSpec((1, _BANK_ROWS, GRID_W * GRID_W), lambda h: (h, 0, 0)),
                  pl.BlockSpec((GRID_W * GRID_W, LANES), lambda h: (0, 0))],
        out_specs=pl.BlockSpec((1, _BANK_ROWS, LANES), lambda h: (h, 0, 0)),
        out_shape=jax.ShapeDtypeStruct((NA_HEADS, _BANK_ROWS, LANES), F32), compiler_params=_PAR,
    )(bank, oh_col)


def _dil_constants(dilation):
    seg = SEQ // dilation
    nb = seg // DIL_QB
    nk = min(DIL_QB + 2 * DIL_HALF, seg)
    starts = [min(max(blk * DIL_QB - DIL_HALF, 0), seg - nk) for blk in range(nb)]
    shifts = sorted({w0 - blk * DIL_QB for blk, w0 in enumerate(starts)}, reverse=True)
    qi = np.arange(DIL_QB)[:, None]
    ki = np.arange(nk)[None, :]
    mask = np.stack([np.where(np.abs(ki + sh - qi) <= DIL_HALF, 0.0, NEG) for sh in shifts]).astype(np.float32)
    ks, cls = [], []
    for i in range(SEQ // DIL_QB):
        sub, blk = divmod(i, nb)
        cls.append(shifts.index(starts[blk] - blk * DIL_QB))
        ks.append(sub * seg + starts[blk])
    return mask.reshape(len(shifts), 1, DIL_QB, nk), np.asarray(ks, np.int32), np.asarray(cls, np.int32), nk


_VM = pl.BlockSpec(memory_space=pltpu.VMEM)


def _ada_fwd(c_all, w, b, *, name):
    def body(c_ref, w_ref, b_ref, o_ref):
        cv = c_ref[...]
        o_ref[...] = jnp.dot(cv * _sigmoid(cv), w_ref[...], preferred_element_type=F32,
                             precision=lax.Precision.HIGHEST) + b_ref[...]

    return _pcall(body, name=name, in_specs=[_VM, _VM, _VM], out_specs=_VM,
                  out_shape=jax.ShapeDtypeStruct((c_all.shape[0], w.shape[1]), F32))(c_all, w, b)


def _ada_bwd(c_all_t, dmod, *, name):
    def body(c_ref, d_ref, o_ref):
        cv = c_ref[...]
        o_ref[...] = jnp.dot(cv * _sigmoid(cv), d_ref[...], preferred_element_type=F32,
                             precision=lax.Precision.HIGHEST)

    return _pcall(body, name=name, in_specs=[_VM, _VM], out_specs=_VM,
                  out_shape=jax.ShapeDtypeStruct((c_all_t.shape[0], dmod.shape[1]), F32))(c_all_t, dmod)


def _row_sum(t, *, name):
    def body(t_ref, o_ref):
        o_ref[...] = jnp.sum(t_ref[...], axis=0, keepdims=True)

    return _pcall(body, name=name, in_specs=[_VM], out_specs=_VM,
                  out_shape=jax.ShapeDtypeStruct((1, t.shape[1]), F32))(t)


def _row_tile(rows):
    tr = rows
    for cand in range(8, 513, 8):
        if rows % cand == 0:
            tr = cand
    return tr


def _adamw_math(wv, gv, mv, vv):
    nm = ADAM_B1 * mv + (1.0 - ADAM_B1) * gv
    nv = ADAM_B2 * vv + (1.0 - ADAM_B2) * (gv * gv)
    m_hat = nm / (1.0 - ADAM_B1 ** ADAM_STEP)
    v_hat = nv / (1.0 - ADAM_B2 ** ADAM_STEP)
    return -ADAM_LR * (m_hat / (jnp.sqrt(v_hat) + ADAM_EPS) + ADAM_WD * wv), nm, nv


def _adamw(w, g, m, v, *, name):
    rows, cols = w.shape
    tr = _row_tile(rows)

    def body(w_ref, g_ref, m_ref, v_ref, d_ref, nm_ref, nv_ref):
        d_ref[...], nm_ref[...], nv_ref[...] = _adamw_math(w_ref[...], g_ref[...], m_ref[...], v_ref[...])

    spec = pl.BlockSpec((tr, cols), lambda i: (i, 0))
    return _pcall(body, name=name, grid=(rows // tr,), in_specs=[spec] * 4, out_specs=[spec] * 3,
                  out_shape=[jax.ShapeDtypeStruct((rows, cols), F32)] * 3, compiler_params=_PAR)(w, g, m, v)


def _adamw_halves(w, g_mine, g_other, m, v, c_arr, *, name):
    rows, cols = w.shape
    hr = rows // 2
    tr = _row_tile(hr)
    nt = hr // tr

    def body(c_ref, w_ref, t_ref, o_ref, m_ref, v_ref, g_ref, d_ref, nm_ref, nv_ref):
        gv = jnp.where(pl.program_id(0) == c_ref[0], t_ref[...], o_ref[...])
        g_ref[...] = gv
        d_ref[...], nm_ref[...], nv_ref[...] = _adamw_math(w_ref[...], gv, m_ref[...], v_ref[...])

    full = pl.BlockSpec((tr, cols), lambda h, i, c: (h * nt + i, 0))
    half = pl.BlockSpec((tr, cols), lambda h, i, c: (i, 0))
    grid_spec = pltpu.PrefetchScalarGridSpec(num_scalar_prefetch=1, grid=(2, nt),
                                             in_specs=[full, half, half, full, full], out_specs=[full] * 4)
    return _pcall(body, name=name, grid_spec=grid_spec, out_shape=[jax.ShapeDtypeStruct((rows, cols), F32)] * 4,
                  compiler_params=pltpu.CompilerParams(dimension_semantics=("parallel", "parallel")),
                  )(c_arr, w, g_mine, g_other, m, v)


_MESH = pl.DeviceIdType.MESH
_ANY = pl.BlockSpec(memory_space=pl.ANY)
_CHIP_FLIPS = ((1, 0), (0, 1), (1, 1))


def _pos():
    return lax.axis_index("x"), lax.axis_index("y"), lax.axis_index("c")


def _flip(v, f):
    return 1 - v if f else v


def _sem_pairs(n):
    return [pltpu.SemaphoreType.DMA((n,)), pltpu.SemaphoreType.DMA((n,))]


def _small_allgather(blk, *, name):
    m_per, n = blk.shape

    def body(x_ref, out_ref, send_sems, recv_sems, local_sem):
        x, y, c = _pos()
        me, sibling = (x, y, c), (x, y, 1 - c)
        chips = [(_flip(x, fx), _flip(y, fy)) for fx, fy in _CHIP_FLIPS]

        def rows(px, py, pc):
            return out_ref.at[pl.ds((4 * px + 2 * py + pc) * m_per, m_per), :]

        def copy(k, block, to, src=None):
            return pltpu.make_async_remote_copy(
                src_ref=rows(*block) if src is None else src, dst_ref=rows(*block),
                send_sem=send_sems.at[k], recv_sem=recv_sems.at[k], device_id=to, device_id_type=_MESH)

        mine = pltpu.make_async_copy(x_ref, rows(*me), local_sem)
        mine.start()
        first = [copy(0, me, sibling, src=x_ref)]
        first += [copy(1 + j, me, (*chip, c), src=x_ref) for j, chip in enumerate(chips)]
        for cp in first:
            cp.start()
        passed = [copy(4 + j, (*chip, c), sibling) for j, chip in enumerate(chips)]
        for j, chip in enumerate(chips):
            copy(1 + j, (*chip, c), me).wait_recv()
            passed[j].start()
        copy(0, sibling, me).wait_recv()
        for j, chip in enumerate(chips):
            copy(4 + j, (*chip, 1 - c), me).wait_recv()
        for cp in first + passed:
            cp.wait_send()
        mine.wait()

    return _pcall(
        body, name=name, out_shape=jax.ShapeDtypeStruct((N_DEV * m_per, n), blk.dtype),
        in_specs=[_VM], out_specs=_VM,
        scratch_shapes=_sem_pairs(7) + [pltpu.SemaphoreType.DMA],
    )(blk)


_HBM = pl.BlockSpec(memory_space=pltpu.HBM)
_SEM = pl.BlockSpec(memory_space=pltpu.SEMAPHORE)
_EFFECT = pltpu.SideEffectType.DATAFLOW_SIDE_EFFECTING


def _split_start(srcs, lands, plan, ncopies, after, *, name, alias_sources=True):
    ns, nl = len(srcs), len(lands)
    thru = (*srcs, *lands) if alias_sources else tuple(lands)

    def body(*refs):
        src_refs, land_refs = refs[:ns], refs[ns:ns + nl]
        send_sems, recv_sems = refs[ns + nl + 1], refs[ns + nl + 2]
        token = refs[-1]
        x, y, c = _pos()
        for k, (src, dst, to, _) in enumerate(plan(x, y, c, src_refs, land_refs)):
            pltpu.make_async_remote_copy(src_ref=src, dst_ref=dst, send_sem=send_sems.at[k], recv_sem=recv_sems.at[k],
                                         device_id=to, device_id_type=_MESH).start()
        token[...] = jnp.zeros_like(token)

    hbm = lambda a: pltpu.HBM(a.shape, a.dtype)
    out = _pcall(
        body, name=name,
        out_shape=(pltpu.SemaphoreType.DMA((ncopies,)), pltpu.SemaphoreType.DMA((ncopies,)),
                   *[hbm(a) for a in thru], jax.ShapeDtypeStruct((8, LANES), F32)),
        in_specs=[_HBM] * (ns + nl) + [_ANY], out_specs=(_SEM, _SEM, *[_HBM] * len(thru), _VM),
        input_output_aliases={ns + nl - len(thru) + i: 2 + i for i in range(len(thru))},
        compiler_params=pltpu.CompilerParams(has_side_effects=_EFFECT),
    )(*[pltpu.with_memory_space_constraint(a, pltpu.HBM) for a in (*srcs, *lands)], after)
    if alias_sources:
        return out[0], out[1], list(out[2:2 + ns]), list(out[2 + ns:2 + ns + nl]), out[-1]
    return out[0], out[1], list(srcs), list(out[2:2 + nl]), out[-1]


def _split_wait(send_sems, recv_sems, srcs, lands, plan, after, *, name, with_sources=False, alias_sources=True):
    ns, nl = len(srcs), len(lands)
    thru = (*srcs, *lands) if alias_sources else tuple(lands)

    def body(*refs):
        src_refs, land_refs = refs[:ns], refs[ns:ns + nl]
        send_sems, recv_sems = refs[ns + nl], refs[ns + nl + 1]
        x, y, c = _pos()
        for k, (src, _, _, mine) in enumerate(plan(x, y, c, src_refs, land_refs)):
            cp = pltpu.make_async_remote_copy(src_ref=src, dst_ref=mine, send_sem=send_sems.at[k],
                                              recv_sem=recv_sems.at[k], device_id=(x, y, c), device_id_type=_MESH)
            cp.wait_send()
            cp.wait_recv()

    hbm = lambda a: pltpu.HBM(a.shape, a.dtype)
    out = _pcall(
        body, name=name, out_shape=tuple(hbm(a) for a in thru),
        in_specs=[_HBM] * (ns + nl) + [_SEM, _SEM, _ANY], out_specs=tuple([_HBM] * len(thru)),
        input_output_aliases={ns + nl - len(thru) + i: i for i in range(len(thru))},
        compiler_params=pltpu.CompilerParams(has_side_effects=_EFFECT),
    )(*srcs, *lands, send_sems, recv_sems, after)
    got_srcs, got_lands = (list(out[:ns]), list(out[ns:])) if alias_sources else (list(srcs), list(out))
    return (got_srcs, got_lands) if with_sources else got_lands


def _ag_plan(nw):
    def plan(x, y, c, sh_refs, full_refs):
        j = 2 * x + y
        out = []
        for wi in range(nw):
            for fx, fy in _CHIP_FLIPS:
                px, py = _flip(x, fx), _flip(y, fy)
                out.append((sh_refs[wi].at[c], full_refs[wi].at[j, c], (px, py, c), full_refs[wi].at[2 * px + py, c]))
            out.append((sh_refs[wi], full_refs[wi].at[j], (x, y, 1 - c), full_refs[wi].at[j]))
        return out
    return plan


def _ag_plan_direct(nw):
    def plan(x, y, c, sh_refs, full_refs):
        j = 2 * x + y
        out = []
        for wi in range(nw):
            for fx, fy in _CHIP_FLIPS:
                px, py = _flip(x, fx), _flip(y, fy)
                for rel in (0, 1):
                    t = _flip(c, rel)
                    out.append((sh_refs[wi].at[c], full_refs[wi].at[j, c], (px, py, t), full_refs[wi].at[2 * px + py, t]))
            out.append((sh_refs[wi], full_refs[wi].at[j], (x, y, 1 - c), full_refs[wi].at[j]))
        return out
    return plan


def _ag_pass(fulls, *, name):
    nw = len(fulls)

    def body(*refs):
        in_refs, out_refs = refs[:nw], refs[nw:2 * nw]
        send_sems, recv_sems = refs[2 * nw:]
        x, y, c = _pos()
        cps = []
        for wi in range(nw):
            for k, (fx, fy) in enumerate(_CHIP_FLIPS):
                jp = 2 * _flip(x, fx) + _flip(y, fy)
                sems = dict(send_sem=send_sems.at[3 * wi + k], recv_sem=recv_sems.at[3 * wi + k], device_id_type=_MESH)
                send = pltpu.make_async_remote_copy(src_ref=in_refs[wi].at[jp, c], dst_ref=out_refs[wi].at[jp, c],
                                                    device_id=(x, y, 1 - c), **sems)
                recv = pltpu.make_async_remote_copy(src_ref=in_refs[wi].at[jp, c], dst_ref=out_refs[wi].at[jp, 1 - c],
                                                    device_id=(x, y, c), **sems)
                cps.append((send, recv))
        for send, _ in cps:
            send.start()
        for send, recv in cps:
            send.wait_send()
            recv.wait_recv()

    return _pcall(body, name=name, out_shape=[jax.ShapeDtypeStruct(f.shape, f.dtype) for f in fulls],
                  in_specs=[_ANY] * nw, out_specs=[_ANY] * nw, input_output_aliases={i: i for i in range(nw)},
                  scratch_shapes=_sem_pairs(3 * nw))(*fulls)


def _sib_plan(nw):
    def plan(x, y, c, g_refs, ra_refs):
        return [(g_refs[wi].at[k, 1 - c], ra_refs[wi].at[k], (x, y, 1 - c), ra_refs[wi].at[k])
                for wi in range(nw) for k in range(N_CHIP)]
    return plan


def _rs_plan(nw):
    def plan(x, y, c, s_refs, rb_refs):
        out = []
        for wi in range(nw):
            for k, (fx, fy) in enumerate(_CHIP_FLIPS):
                px, py = _flip(x, fx), _flip(y, fy)
                out.append((s_refs[wi].at[2 * px + py], rb_refs[wi].at[k], (px, py, c), rb_refs[wi].at[k]))
        return out
    return plan


def _sibling_swap(ts, *, name):
    nw = len(ts)

    def body(*refs):
        t_refs, out_refs = refs[:nw], refs[nw:2 * nw]
        send_sems, recv_sems = refs[2 * nw:]
        x, y, c = _pos()
        cps = [pltpu.make_async_remote_copy(src_ref=t_refs[wi], dst_ref=out_refs[wi], send_sem=send_sems.at[wi],
                                            recv_sem=recv_sems.at[wi], device_id=(x, y, 1 - c), device_id_type=_MESH)
               for wi in range(nw)]
        for cp in cps:
            cp.start()
        for cp in cps:
            cp.wait()

    return _pcall(body, name=name, out_shape=[jax.ShapeDtypeStruct(t.shape, t.dtype) for t in ts],
                  in_specs=[_ANY] * nw, out_specs=[_ANY] * nw, scratch_shapes=_sem_pairs(nw))(*ts)


def _rs_add(g, ra, c_arr, *, name):
    n, _, r, w = g.shape

    def body(c_ref, g_ref, ra_ref, s_ref, sb_ref):
        t = g_ref[...] + ra_ref[...]
        s_ref[...] = t
        sb_ref[...] = t.astype(BF16)

    grid_spec = pltpu.PrefetchScalarGridSpec(
        num_scalar_prefetch=1, grid=(n,),
        in_specs=[pl.BlockSpec((None, None, r, w), lambda k, c: (k, c[0], 0, 0)),
                  pl.BlockSpec((None, r, w), lambda k, c: (k, 0, 0))],
        out_specs=[pl.BlockSpec((None, r, w), lambda k, c: (k, 0, 0))] * 2)
    return _pcall(body, name=name, grid_spec=grid_spec,
                  out_shape=[jax.ShapeDtypeStruct((n, r, w), F32), jax.ShapeDtypeStruct((n, r, w), BF16)],
                  compiler_params=_PAR)(c_arr, g, ra)


def _rs_final(s, rb, j_arr, *, name):
    _, r, w = s.shape

    def body(j_ref, s_ref, rb_ref, t_ref):
        t_ref[...] = ((s_ref[...] + rb_ref[0].astype(F32)) + rb_ref[1].astype(F32)) + rb_ref[2].astype(F32)

    grid_spec = pltpu.PrefetchScalarGridSpec(
        num_scalar_prefetch=1, grid=(1,),
        in_specs=[pl.BlockSpec((None, r, w), lambda i, j: (j[0], 0, 0)),
                  pl.BlockSpec((3, r, w), lambda i, j: (0, 0, 0))],
        out_specs=pl.BlockSpec((r, w), lambda i, j: (0, 0)))
    return _pcall(body, name=name, grid_spec=grid_spec, out_shape=jax.ShapeDtypeStruct((r, w), F32),
                  compiler_params=_ARB)(j_arr, s, rb)


def _tile2(g):
    return jnp.concatenate([g, g], axis=1)


_BIG = ("w_in", "w_ffn_in", "w_ffn_out", "w_o", "w_proj_a", "w_proj_b")
_BIG_SHARD = {"w_in": (1024, 1472), "w_ffn_in": (1024, 1408), "w_ffn_out": (704, 1024), "w_o": (256, 1024),
              "w_proj_a": (512, 256), "w_proj_b": (256, 256)}


def _device_step(x2, tgt, mod, first_weights, late_weights, early_grads, mid_grads, g_norm1, g_norm2, b_gate, g_qa, g_ka,
                 g_qb, g_kb, rpb):
    d = D_MODEL
    sh1, sc1, gt1, sh2, sc2, gt2 = [mod[:, k * d:(k + 1) * d] for k in range(6)]

    colmask, oh_col, na_ks, na_cls = _na_constants()
    rp = jnp.pad(rpb, ((0, 0), (0, 16 - _RPB_RO), (RP_LANE0, LANES - RP_LANE0 - _RPB_CO)), constant_values=NEG)
    na = (rp, jnp.asarray(colmask))
    na_ks, na_cls = jnp.asarray(na_ks), jnp.asarray(na_cls)
    dil = [_dil_constants(dd) for _, dd in DIL_CONFIGS]
    gains = jnp.concatenate([_tile2(g_qa), _tile2(g_ka), _tile2(g_qb), _tile2(g_kb)], axis=0)
    cos_t, sa_t, sb_t = _rope_tables()

    wts = first_weights(cos_t)
    h1, qkvn, qk_pre, gates, *qkv_dil = _pre_attn_fwd(x2, cos_t, sa_t, sb_t, g_norm1, sc1, sh1, wts["w_qkv"],
                                                      wts["w_gates"], gains, name="pre_attn_fwd")
    o_a, lse_a = _attn_fwd(qkvn, 0, 4, 8, 4, None, na_ks, na_cls, NA_NK, name="attn_a_fwd", na=na)
    arrs, o_g, l_g = [], [], []
    res = lambda t, dd: t if dd == 1 else (t, dd)
    for g, (_, dd) in enumerate(DIL_CONFIGS):
        tab_g, ks_g, cls_g, nk_g = jnp.asarray(dil[g][0]), jnp.asarray(dil[g][1]), jnp.asarray(dil[g][2]), dil[g][3]
        arr, cb = (qkvn, (12, 18, 24)) if dd == 1 else (qkv_dil.pop(0), (0, 2, 4))
        op, lp = _attn_fwd(arr, cb[0], cb[1], cb[2], 2, tab_g, ks_g, cls_g, nk_g, name=f"attn_d{g}_fwd", qb=DIL_QB)
        arrs.append((arr, cb, tab_g, ks_g, cls_g, nk_g))
        o_g.append(res(op, dd))
        l_g.append(res(lp, dd))
    wts = dict(wts, **late_weights(o_a))
    o_b, merged, mo, x1, h2 = _post_attn_fwd(o_a, o_g, l_g, gates, x2, wts["w_pa"], wts["w_pb"], wts["w_o"], b_gate,
                                             gt1, g_norm2, sc2, sh2, name="post_attn_fwd")
    act, ff = _ffn_fwd(h2, wts["w_ffn_in"], name="ffn_fwd")

    dy, dffo, dff, dgt2, loss_v = _ffn_mid(act, ff, x1, tgt, wts["w_ffn_out"], gt2, name="ffn_mid")
    grads = {}
    g_ffn_out = _wgrad(act, dffo, name="wg_ffn_out", tm=D_FF // 2, tn=d, tk=2048)
    grads["w_ffn_out"] = g_ffn_out.reshape(N_CHIP, D_FF // N_CHIP, d)
    grads["w_ffn_in"] = _wgrad(h2, dff, name="wg_ffn_in", tm=512, tn=2 * FF_CHIP, tk=2048, chips=N_CHIP)
    dx1, dmo, sums2 = _ffn_in_bwd(dff, x1, dy, mo, wts["w_ffn_in"], g_norm2, sc2, gt1, name="ffn_in_bwd")
    grads["w_o"] = _wgrad(merged, dmo, name="wg_o", tm=d, tn=d, tk=2048).reshape(N_CHIP, d // N_CHIP, d)
    pab = _post_attn_bwd(dmo, gates, o_a, o_g, l_g, wts["w_pa"], wts["w_pb"], wts["w_o"], b_gate, name="post_attn_bwd")
    dpa, dpb, dgates, do_a = pab[:4]
    do_g, dl_g, dbg = pab[4:7], pab[7:10], pab[10]
    g_pa = _wgrad(o_a, dpa, name="wg_pa", tm=WA, tn=d, tk=2048)
    g_pb = _wgrad(o_b, dpb, name="wg_pb", tm=WB_OUT, tn=d, tk=2048)
    grads["w_proj_a"] = g_pa.reshape(WA, N_CHIP, d // N_CHIP).transpose(1, 0, 2)
    grads["w_proj_b"] = g_pb.reshape(WB_OUT, N_CHIP, d // N_CHIP).transpose(1, 0, 2)
    order = early_grads(grads)
    dqs, dks, dvs = [], [], []
    for g, (_, dd) in enumerate(DIL_CONFIGS):
        arr, cb, tab_g, ks_g, cls_g, nk_g = arrs[g]
        plain = lambda t: t[0] if isinstance(t, tuple) else t
        dq, dk, dv = _attn_bwd(arr, cb[0], cb[1], cb[2], 2, tab_g, ks_g, cls_g, nk_g, do_g[g], plain(o_g[g]),
                               plain(l_g[g]), name=f"attn_d{g}_bwd", dlse=dl_g[g], qb=DIL_QB, order=order)
        dqs.append(res(dq, dd))
        dks.append(res(dk, dd))
        dvs.append(res(dv, dd))
    order = mid_grads(dv)
    dqa, dka, dva, bank = _attn_bwd(qkvn, 0, 4, 8, 4, None, na_ks, na_cls, NA_NK, do_a, o_a, lse_a,
                                    name="attn_a_bwd", na=na, order=order)
    dqkv, grad_x, dgains, sums1 = _pre_attn_bwd(qk_pre, [dqa, dka, dva] + dqs + dks + dvs, dgates, x2, dx1, cos_t, sa_t,
                                                sb_t, wts["w_qkv"], wts["w_gates"], gains, g_norm1, sc1,
                                                name="pre_attn_bwd")
    g_qkv = _wgrad(h1, dqkv, name="wg_qkv", tm=d, tn=W_QKV // 2, tk=2048)
    g_gates = _wgrad(h1, dgates, name="wg_gates", tm=d, tn=W_GATES // 2, tk=2048)
    nc, cut = _BIG_SHARD["w_in"][1], 3 * _BIG_SHARD["w_in"][1] - W_QKV
    grads["w_in"] = jnp.stack([g_qkv[:, :nc], g_qkv[:, nc:2 * nc],
                               jnp.concatenate([g_qkv[:, 2 * nc:], g_gates[:, :cut]], axis=1), g_gates[:, cut:]])

    bank = bank.reshape(NA_HEADS, NA_SLOTS, GRID_W, 2, GRID_W).transpose(0, 1, 3, 2, 4)
    bank = jnp.pad(bank.reshape(NA_HEADS, 2 * NA_SLOTS, GRID_W * GRID_W), ((0, 0), (0, _BANK_ROWS - 2 * NA_SLOTS), (0, 0)))
    g2 = _bank_reduce(bank, jnp.asarray(oh_col), name="rpb_reduce")[:, :2 * NA_SLOTS].reshape(NA_HEADS, NA_SLOTS, 2, LANES)
    g_rpb = g2[:, 3:3 + _RPB_RO, 0, :_RPB_CO] + g2[:, 2:2 + _RPB_RO, 1, :_RPB_CO]

    dmod = jnp.concatenate([sums1[0:1], sums1[1:2], sums2[3:4], sums2[0:1], sums2[1:2], dgt2], axis=1)
    small = dict(g_norm1=sums1[2:3], g_norm2=sums2[2:3], b_gate=dbg, g_qa=dgains[0:1, :HEAD_DIM],
                 g_ka=dgains[1:2, :HEAD_DIM], g_qb=dgains[2:3, :HEAD_DIM], g_kb=dgains[3:4, :HEAD_DIM], rpb=g_rpb)
    return loss_v, grad_x, grads, dmod, small


_SMALL = ("b_ada", "g_norm1", "g_norm2", "b_gate", "g_qa", "g_ka", "g_qb", "g_kb", "rpb")
_SMALL_N = {"b_ada": 6 * D_MODEL, "g_norm1": D_MODEL, "g_norm2": D_MODEL, "b_gate": 2 * D_MODEL, "g_qa": HEAD_DIM,
            "g_ka": HEAD_DIM, "g_qb": HEAD_DIM, "g_kb": HEAD_DIM, "rpb": NA_HEADS * _RPB_RO * _RPB_CO}


def _pack_small(parts):
    flat = [parts[n].reshape(1, _SMALL_N[n]) for n in _SMALL]
    used = sum(_SMALL_N.values())
    return jnp.concatenate(flat + [jnp.zeros((1, STATS_W - used), F32)], axis=1)


def _unpack_small(v, shapes):
    out, at = {}, 0
    for n in _SMALL:
        out[n] = v[:, at:at + _SMALL_N[n]].reshape(shapes[n])
        at += _SMALL_N[n]
    return out


def _join_cols(t):
    _, r, c = t.shape
    return t.transpose(1, 0, 2).reshape(r, N_CHIP * c)


def kernel(x, c, w_ada, b_ada, g_norm1, g_norm2, w_in, b_gate, g_qa, g_ka, g_qb, g_kb, rpb, w_proj_a, w_proj_b, w_o, w_ffn_in, w_ffn_out, loss_target, m_w_ada, m_b_ada, m_g_norm1, m_g_norm2, m_w_in, m_b_gate, m_g_qa, m_g_ka, m_g_qb, m_g_kb, m_rpb, m_w_proj_a, m_w_proj_b, m_w_o, m_w_ffn_in, m_w_ffn_out, v_w_ada, v_b_ada, v_g_norm1, v_g_norm2, v_w_in, v_b_gate, v_g_qa, v_g_ka, v_g_qb, v_g_kb, v_rpb, v_w_proj_a, v_w_proj_b, v_w_o, v_w_ffn_in, v_w_ffn_out):
    names = ("w_ada", "b_ada", "g_norm1", "g_norm2", "w_in", "b_gate", "g_qa", "g_ka", "g_qb", "g_kb", "rpb",
             "w_proj_a", "w_proj_b", "w_o", "w_ffn_in", "w_ffn_out")
    w = dict(zip(names, (w_ada, b_ada, g_norm1, g_norm2, w_in, b_gate, g_qa, g_ka, g_qb, g_kb, rpb, w_proj_a, w_proj_b,
                         w_o, w_ffn_in, w_ffn_out)))
    m = dict(zip(names, (m_w_ada, m_b_ada, m_g_norm1, m_g_norm2, m_w_in, m_b_gate, m_g_qa, m_g_ka, m_g_qb, m_g_kb, m_rpb,
                         m_w_proj_a, m_w_proj_b, m_w_o, m_w_ffn_in, m_w_ffn_out)))
    v = dict(zip(names, (v_w_ada, v_b_ada, v_g_norm1, v_g_norm2, v_w_in, v_b_gate, v_g_qa, v_g_ka, v_g_qb, v_g_kb, v_rpb,
                         v_w_proj_a, v_w_proj_b, v_w_o, v_w_ffn_in, v_w_ffn_out)))
    d = D_MODEL
    xi, yi, ci = _pos()
    chip = 2 * xi + yi
    me = 2 * chip + ci
    ada_cols = 6 * d // N_CHIP

    c_arr, chip_arr = ci.reshape(1).astype(jnp.int32), chip.reshape(1).astype(jnp.int32)
    first, rest = _BIG[:1], _BIG[1:]

    c_all = _small_allgather(c.reshape(8, d // 8), name="ag_c").reshape(N_DEV, d)
    b_sh = lax.dynamic_slice(b_ada, (0, chip * ada_cols), (1, ada_cols))
    mod_part = _ada_fwd(c_all, w_ada[0], b_sh, name="ada_fwd")
    mod_all = _small_allgather(mod_part, name="ag_mod").reshape(N_CHIP, 2, 8, ada_cols)[:, 0]
    mod = lax.dynamic_index_in_dim(mod_all, me, axis=1, keepdims=False).reshape(1, 6 * d)

    halves = {n: (2, _BIG_SHARD[n][0] // 2, _BIG_SHARD[n][1]) for n in _BIG}
    shards = {n: w[n][0].astype(BF16).reshape(halves[n]) for n in _BIG}
    land = lambda n: lax.empty((N_CHIP,) + halves[n], BF16)
    ag1 = _split_start([shards[n] for n in first], [land(n) for n in first], _ag_plan(1), 4, mod, name="ag1_start")
    ag2 = _split_start([shards[n] for n in rest], [land(n) for n in rest], _ag_plan_direct(len(rest)), 7 * len(rest),
                       ag1[4], name="ag2_start")
    rpb_after = rpb[0] + ag2[4][0, 0]

    def first_weights(after):
        after = after[:1, :1] + ag2[4][:1, :1]
        full1 = _split_wait(ag1[0], ag1[1], ag1[2], ag1[3], _ag_plan(1), after, name="ag1_wait")
        p_in = _ag_pass(full1, name="ag1_pass")[0].reshape((N_CHIP,) + _BIG_SHARD["w_in"])
        cut = W_QKV - 2 * _BIG_SHARD["w_in"][1]
        return dict(w_qkv=jnp.concatenate([p_in[0], p_in[1], p_in[2][:, :cut]], axis=1),
                    w_gates=jnp.concatenate([p_in[2][:, cut:], p_in[3]], axis=1))

    def late_weights(after):
        full2 = _split_wait(ag2[0], ag2[1], ag2[2], ag2[3], _ag_plan_direct(len(rest)), after, name="ag2_wait")
        full ={n: fu.reshape((N_CHIP,) + _BIG_SHARD[n]) for n, fu in zip(rest, full2)}
        return dict(w_pa=_join_cols(full["w_proj_a"]), w_pb=_join_cols(full["w_proj_b"]), w_o=full["w_o"].reshape(d, d),
                    w_ffn_in=_join_cols(full["w_ffn_in"]), w_ffn_out=full["w_ffn_out"].reshape(D_FF, d))

    def sib_begin(group, grads, tag):
        gps = [grads[n].reshape((N_CHIP,) + halves[n]) for n in group]
        lands = [lax.empty((N_CHIP,) + halves[n][1:], F32) for n in group]
        return _split_start(gps, lands, _sib_plan(len(group)), N_CHIP * len(group), gps[0], name=f"rs_sib_{tag}_start",
                            alias_sources=False)

    def rs_begin(group, sib, after, tag):
        gps, ras = _split_wait(sib[0], sib[1], sib[2], sib[3], _sib_plan(len(group)), after,
                               name=f"rs_sib_{tag}_wait", with_sources=True, alias_sources=False)
        sums = [_rs_add(gp, ra, c_arr, name=f"rs_add_{n}") for n, gp, ra in zip(group, gps, ras)]
        lands = [lax.empty((3,) + halves[n][1:], BF16) for n in group]
        st = _split_start([sb for _, sb in sums], lands, _rs_plan(len(group)), 3 * len(group), sums[0][0],
                          name=f"rs_{tag}_start")
        return sums, st

    def rs_end(group, begun, after, tag):
        sums, st = begun
        rbs = _split_wait(st[0], st[1], st[2], st[3], _rs_plan(len(group)), after, name=f"rs_{tag}_wait")
        return [_rs_final(sf, rb, chip_arr, name=f"rs_final_{n}") for n, (sf, _), rb in zip(group, sums, rbs)]

    begun = {}

    def early_grads(grads):
        begun["sib_rest"] = sib_begin(rest, grads, "rest")
        return begun["sib_rest"][4]

    def mid_grads(after):
        begun["rest"] = rs_begin(rest, begun["sib_rest"], after, "rest")
        return begun["rest"][1][4]

    loss_v, grad_x, grads, dmod, small = _device_step(
        x[0], loss_target[0], mod, first_weights, late_weights, early_grads, mid_grads, g_norm1, g_norm2, b_gate, g_qa,
        g_ka, g_qb, g_kb, rpb_after)
    sib_first = sib_begin(first, grads, "first")

    g, delta, new_m, new_v = {}, {}, {}, {}

    def finish(group, ts, tag):
        others = _sibling_swap(ts, name=f"rs_pair_{tag}")
        for n, t, o in zip(group, ts, others):
            gg, dl, nm, nv = _adamw_halves(w[n][0], t, o, m[n][0], v[n][0], c_arr, name=f"adamw_{n}")
            g[n], delta[n], new_m[n], new_v[n] = gg[None], dl[None], nm[None], nv[None]

    finish(rest, rs_end(rest, begun["rest"], sib_first[4], "rest"), "rest")
    done_rest = sum(new_v[n][0, :1, :1] for n in rest)
    begun["first"] = rs_begin(first, sib_first, done_rest, "first")

    stats = _pack_small(dict(b_ada=dmod, **small)) + begun["first"][1][4][0, 0]
    stats = stats.at[:, STATS_W - 1].set(loss_v[0, 0])
    rows = _small_allgather(stats.reshape(8, STATS_W // 8), name="ag_stats").reshape(N_DEV, STATS_W)
    dmod_sh = lax.dynamic_slice(rows, (0, chip * ada_cols), (8, ada_cols))
    g_ada = _ada_bwd(c_all.T, dmod_sh, name="ada_bwd")
    tot = _row_sum(rows, name="stats_sum")
    g_small = _unpack_small(tot, {n: w[n].shape for n in _SMALL})

    finish(first, rs_end(first, begun["first"], tot, "first"), "first")

    dl, nm, nv = _adamw(w_ada[0], g_ada, m_w_ada[0], v_w_ada[0], name="adamw_w_ada")
    g["w_ada"], delta["w_ada"], new_m["w_ada"], new_v["w_ada"] = g_ada[None], dl[None], nm[None], nv[None]
    shapes = {n: w[n].shape for n in _SMALL}
    dl, nm, nv = _adamw(_pack_small({n: w[n] for n in _SMALL}), tot, _pack_small({n: m[n] for n in _SMALL}),
                        _pack_small({n: v[n] for n in _SMALL}), name="adamw_small")
    delta.update(_unpack_small(dl, shapes))
    new_m.update(_unpack_small(nm, shapes))
    new_v.update(_unpack_small(nv, shapes))
    g.update(g_small)

    loss = tot[0, STATS_W - 1]
    return (loss, grad_x[None], *[g[n] for n in names], *[delta[n] for n in names], *[new_m[n] for n in names],
            *[new_v[n] for n in names])
```

```python
import numpy as np

import jax
import jax.numpy as jnp
from jax import lax
from jax.experimental import pallas as pl
from jax.experimental.pallas import tpu as pltpu

F32 = jnp.float32
BF16 = jnp.bfloat16

D_MODEL = 1024
SEQ = 8192
HEAD_DIM = 64
GRID_W = 64
ROWS = SEQ // GRID_W
NA_HEADS = 8
NA_KH = 8
NA_KW = 16
DIL_CONFIGS = ((128, 1), (512, 4), (2048, 16))
ROT_DIM = 16
ROPE_THETA = 500000.0
D_FF = 2816
EPS = 1e-6
NEG = -1e30
WA = 512
WB = 768
WB_OUT = 256
W_QKV = 3 * WA + 3 * WB
W_QK = 2 * WA + 2 * WB
W_GATES = 2 * D_MODEL
SCALE = HEAD_DIM ** -0.5

ADAM_LR = 0.001
ADAM_B1 = 0.9
ADAM_B2 = 0.999
ADAM_EPS = 1e-08
ADAM_WD = 0.01
ADAM_STEP = 10

LANES = 128
ROW_TILE = 256
ROW_TILES = {"ffn_fwd": 512, "post_attn_fwd": 512, "post_attn_bwd": 512,
             "ffn_in_bwd": 512}
WGRAD_TILES = {"wg_ffn_out": (1408, 1024, 2048), "wg_ffn_in": (512, 2816, 2048), "wg_o": (1024, 1024, 2048),
               "wg_pa": (512, 1024, 2048), "wg_pb": (256, 1024, 2048), "wg_qkv": (1024, 1920, 2048),
               "wg_gates": (1024, 1024, 2048)}
Q_BLOCK = 256
NA_QROWS = Q_BLOCK // GRID_W
NA_KROWS = NA_QROWS + NA_KH - 1
NA_NK = NA_KROWS * GRID_W
NA_PAIRS = (NA_KROWS + 1) // 2
NA_W = NA_PAIRS * LANES
NA_RO_NONE = 15
NA_SLOTS = 21
RP_LANE0 = GRID_W - NA_KW
DIL_HALF = 64
DIL_QB = 512
N_QBLK = SEQ // Q_BLOCK

N_DEV = 8
N_CHIP = 4
FF_CHIP = 2 * D_FF // N_CHIP
STATS_W = 14336


def _pcall(body, *, name, **kw):
    return pl.pallas_call(body, name=name, **kw)


_NT = (((1,), (1,)), ((), ()))
_TN = (((0,), (0,)), ((), ()))
_ARB = pltpu.CompilerParams(dimension_semantics=("arbitrary",))
_PAR = pltpu.CompilerParams(dimension_semantics=("parallel",))


def _dot(a, b):
    return jnp.dot(a, b, preferred_element_type=F32)


def _dot_nt(a, b):
    return lax.dot_general(a, b, _NT, preferred_element_type=F32)


def _wgrad(a, b, *, name, chips=None):
    s, ma = a.shape
    nb = b.shape[1]
    tm, tn, tk = WGRAD_TILES[name]
    nk = s // tk
    nc = nb // chips if chips else tn
    cpb = tn // nc

    def body(a_ref, b_ref, o_ref, acc):
        k = pl.program_id(2)
        r = lax.dot_general(a_ref[...].astype(BF16), b_ref[...].astype(BF16), _TN, preferred_element_type=F32)

        @pl.when(k == 0)
        def _():
            acc[...] = r

        @pl.when(k > 0)
        def _():
            acc[...] += r

        @pl.when(k == nk - 1)
        def _():
            if chips:
                for q in range(cpb):
                    o_ref[q] = acc[:, q * nc:(q + 1) * nc]
            else:
                o_ref[...] = acc[...]

    if chips:
        o_spec = pl.BlockSpec((cpb, tm, nc), lambda i, j, k: (j, i, 0))
        out_shape = jax.ShapeDtypeStruct((chips, ma, nc), F32)
    else:
        o_spec = pl.BlockSpec((tm, tn), lambda i, j, k: (i, j))
        out_shape = jax.ShapeDtypeStruct((ma, nb), F32)
    return _pcall(
        body, name=name, grid=(ma // tm, nb // tn, nk),
        in_specs=[pl.BlockSpec((tk, tm), lambda i, j, k: (k, i)), pl.BlockSpec((tk, tn), lambda i, j, k: (k, j))],
        out_specs=o_spec, out_shape=out_shape, scratch_shapes=[pltpu.VMEM((tm, tn), F32)],
        compiler_params=pltpu.CompilerParams(dimension_semantics=("parallel", "parallel", "arbitrary")),
    )(a, b)


def _row_call(body, *, name, row_ins, res_ins, row_outs, acc_outs=(), scratch=()):
    row_ins = [a if isinstance(a, tuple) else (a, 1) for a in row_ins]
    row_outs = [o if len(o) == 3 else (*o, 1) for o in row_outs]
    s = row_ins[0][0].shape[0]
    tile = ROW_TILES.get(name, ROW_TILE)
    n = s // tile
    nri, nre, nro, nao = len(row_ins), len(res_ins), len(row_outs), len(acc_outs)

    def whole(shape):
        nd = len(shape)
        return pl.BlockSpec(tuple(shape), lambda i: (0,) * nd, pipeline_mode=pl.Buffered(1))

    def whole_out(shape):
        nd = len(shape)
        return pl.BlockSpec(tuple(shape), lambda i: (0,) * nd)

    def rows(w, d):
        if d == 1:
            return pl.BlockSpec((tile, w), lambda i: (i, 0))
        return pl.BlockSpec((d, tile // d, w), lambda i: (0, i, 0))

    in_specs = [rows(a.shape[1], d) for a, d in row_ins]
    in_specs += [whole(a.shape) for a in res_ins]
    out_specs = [rows(w, d) for w, _, d in row_outs]
    out_specs += [whole_out(shp) for shp, _ in acc_outs]
    out_shape = [jax.ShapeDtypeStruct((s, w) if d == 1 else (d, s // d, w), dt) for w, dt, d in row_outs]
    out_shape += [jax.ShapeDtypeStruct(tuple(shp), dt) for shp, dt in acc_outs]

    def wrapped(*refs):
        at = [0, nri, nri + nre, nri + nre + nro, nri + nre + nro + nao]
        body(pl.program_id(0), n, refs[at[0]:at[1]], refs[at[1]:at[2]], refs[at[2]:at[3]], refs[at[3]:at[4]],
             refs[at[4]:])

    args = [a if d == 1 else a.reshape(d, s // d, a.shape[1]) for a, d in row_ins]
    outs = _pcall(wrapped, name=name, grid=(n,), in_specs=in_specs, out_specs=out_specs, out_shape=out_shape,
                  scratch_shapes=list(scratch), compiler_params=_ARB)(*args, *res_ins)
    return [o.reshape(s, o.shape[-1]) if k < nro and row_outs[k][2] != 1 else o for k, o in enumerate(outs)]


def _stage_shape(name):
    return pltpu.VMEM((4, ROW_TILES.get(name, ROW_TILE), LANES), F32)


def _from_residue(ref, col, stage, slot):
    d, n = ref.shape[0], ref.shape[1]
    for r in range(d):
        stage.at[slot][pl.ds(r, n, stride=d), :] = ref[r, :, col:col + LANES].astype(F32)
    return stage[slot]


def _natural(ref, stage, slot0):
    if len(ref.shape) == 2:
        return ref[...]
    return jnp.concatenate([_from_residue(ref, c * LANES, stage, (slot0 + c) % 4)
                            for c in range(ref.shape[2] // LANES)], axis=1)


def _to_residue(val, ref, col, stage, slot):
    d, n = ref.shape[0], ref.shape[1]
    stage[slot] = val
    for r in range(d):
        ref[r, :, col:col + LANES] = stage.at[slot][pl.ds(r, n, stride=d), :].astype(ref.dtype)


def _fold8(t):
    r, w = t.shape
    return jnp.sum(t.reshape(r // 8, 8, w), axis=0)


def _sigmoid(t):
    return 0.5 * (jnp.tanh(0.5 * t) + 1.0)


def _head_lanes():
    return lax.broadcasted_iota(jnp.int32, (1, LANES), 1) < HEAD_DIM


def _head_mean(t, lo):
    s_lo = jnp.sum(jnp.where(lo, t, 0.0), axis=1, keepdims=True)
    s_hi = jnp.sum(jnp.where(lo, 0.0, t), axis=1, keepdims=True)
    return jnp.where(lo, s_lo, s_hi) * (1.0 / HEAD_DIM)


def _rms_mod(xv, g, sc, sh):
    rstd = lax.rsqrt(jnp.mean(xv * xv, axis=1, keepdims=True) + EPS)
    return (xv * rstd * g) * (1.0 + sc) + sh


def _rms_mod_bwd(xv, dh, g, sc):
    rstd = lax.rsqrt(jnp.mean(xv * xv, axis=1, keepdims=True) + EPS)
    xhat = xv * rstd
    dn = dh * (1.0 + sc)
    dxhat = dn * g
    dx = rstd * (dxhat - xhat * jnp.mean(dxhat * xhat, axis=1, keepdims=True))
    return dx, dh, dh * (xhat * g), dn * xhat


def _mix_weights(ls):
    m = jnp.maximum(jnp.maximum(ls[0], ls[1]), ls[2])
    es = [jnp.exp(t - m) for t in ls]
    den = es[0] + es[1] + es[2]
    return [e / den for e in es]


def _rope_tables():
    half = ROT_DIM // 2
    inv_freq = ROPE_THETA ** (-(jnp.arange(half, dtype=F32) * 2.0) / ROT_DIM)
    lane = np.arange(LANES) % HEAD_DIM
    ang = jnp.arange(SEQ).astype(F32)[:, None] * jnp.tile(inv_freq, LANES // half)[None, :]
    cos, sin = jnp.cos(ang), jnp.sin(ang)
    first, second = jnp.asarray(lane < half)[None, :], jnp.asarray((lane >= half) & (lane < ROT_DIM))[None, :]
    cos_t = jnp.where(first | second, cos, 1.0)
    return cos_t, jnp.where(second, sin, 0.0), jnp.where(first, -sin, 0.0)


_SECTIONS = ((0, WA, 0, False), (WA, 2 * WA, 1, False), (2 * WA, 3 * WA, -1, False),
             (3 * WA, 3 * WA + WB, 2, True), (3 * WA + WB, 3 * WA + 2 * WB, 3, True), (3 * WA + 2 * WB, W_QKV, -1, False))


def _pre_attn_fwd(x, cos_t, sa_t, sb_t, g1, sc1, sh1, w_qkv, w_gates, gains, *, name):
    half = ROT_DIM // 2
    dilated = [(g, dd) for g, (_, dd) in enumerate(DIL_CONFIGS) if dd > 1]

    def body(i, n, rin, res, rout, aout, scr):
        x_ref, cos_ref, sa_ref, sb_ref = rin
        g_ref, sc_ref, sh_ref, wq_ref, wg_ref, gains_ref = res
        h1_ref, qkvn_ref, pre_ref, gates_ref = rout[:4]
        group_ref = {g: rout[4 + k] for k, (g, _) in enumerate(dilated)}
        (stage,) = scr
        staged = 0
        hb = _rms_mod(x_ref[...], g_ref[...], sc_ref[...], sh_ref[...]).astype(BF16)
        h1_ref[...] = hb
        gates_ref[...] = _dot(hb, wg_ref[...]).astype(BF16)
        lo = _head_lanes()
        cosv, sav, sbv = cos_ref[...], sa_ref[...], sb_ref[...]
        pre_at = 0
        for si, (c0, c1, kind, rot) in enumerate(_SECTIONS):
            sec = _dot(hb, wq_ref[:, c0:c1])
            for ch in range((c1 - c0) // LANES):
                t = sec[:, ch * LANES:(ch + 1) * LANES]
                if kind >= 0:
                    pre_ref[:, pre_at:pre_at + LANES] = t.astype(BF16)
                    pre_at += LANES
                    t = t * lax.rsqrt(_head_mean(t * t, lo) + EPS) * gains_ref[kind:kind + 1, :]
                    if rot:
                        t = t * cosv + pltpu.roll(t, half, 1) * sav + pltpu.roll(t, LANES - half, 1) * sbv
                qkvn_ref[:, c0 + ch * LANES:c0 + (ch + 1) * LANES] = t.astype(BF16)
                group = ch * LANES // WB_OUT if si >= 3 else 0
                if group in group_ref:
                    col = (si - 3) * WB_OUT + ch * LANES % WB_OUT
                    _to_residue(t, group_ref[group], col, stage, staged % 4)
                    staged += 1

    return _row_call(body, name=name, row_ins=[x, cos_t, sa_t, sb_t], res_ins=[g1, sc1, sh1, w_qkv, w_gates, gains],
                     row_outs=[(D_MODEL, BF16), (W_QKV, BF16), (W_QK, BF16), (W_GATES, BF16)]
                     + [(3 * WB_OUT, BF16, dd) for _, dd in dilated], scratch=[_stage_shape(name)])


def _pre_attn_bwd(qk_pre, d_parts, dgates, x, dx1, cos_t, sa_t, sb_t, w_qkv, w_gates, gains, g1, sc1, *, name):
    half = ROT_DIM // 2
    nparts = len(d_parts)
    where = []
    residue = [isinstance(part, tuple) for part in d_parts]
    for pi, part in enumerate(d_parts):
        width = (part[0] if residue[pi] else part).shape[1]
        where += [(pi, cj) for cj in range(width // LANES)]
    assert len(where) == W_QKV // LANES

    def body(i, n, rin, res, rout, aout, scr):
        pre_ref, d_refs = rin[0], rin[1:1 + nparts]
        dgates_ref, x_ref, dx1_ref, cos_ref, sa_ref, sb_ref = rin[1 + nparts:]
        wq_ref, wg_ref, gains_ref, g_ref, sc_ref = res
        dqkv_ref, gx_ref = rout
        dgains_ref, sums_ref = aout
        accg, accs, stage = scr
        staged = 0

        @pl.when(i == 0)
        def _():
            accg[...] = jnp.zeros_like(accg)
            accs[...] = jnp.zeros_like(accs)

        lo = _head_lanes()
        cosv, sav, sbv = cos_ref[...], sa_ref[...], sb_ref[...]
        dh = _dot_nt(dgates_ref[...], wg_ref[...])
        pre_at = 0
        for c0, c1, kind, rot in _SECTIONS:
            for ch in range((c1 - c0) // LANES):
                pi, cj = where[c0 // LANES + ch]
                if residue[pi]:
                    dt = _from_residue(d_refs[pi], cj * LANES, stage, staged % 4)
                    staged += 1
                else:
                    dt = d_refs[pi][:, cj * LANES:(cj + 1) * LANES]
                if kind >= 0:
                    if rot:
                        dt = dt * cosv + pltpu.roll(dt * sav, LANES - half, 1) + pltpu.roll(dt * sbv, half, 1)
                    t = pre_ref[:, pre_at:pre_at + LANES].astype(F32)
                    pre_at += LANES
                    rstd = lax.rsqrt(_head_mean(t * t, lo) + EPS)
                    xhat = t * rstd
                    accg[kind] += _fold8(dt * xhat)
                    dxhat = dt * gains_ref[kind:kind + 1, :]
                    dt = rstd * (dxhat - xhat * _head_mean(dxhat * xhat, lo))
                dqkv_ref[:, c0 + ch * LANES:c0 + (ch + 1) * LANES] = dt.astype(BF16)
            dh = dh + _dot_nt(dqkv_ref[:, c0:c1], wq_ref[:, c0:c1])
        dx, t_sh, t_sc, t_g = _rms_mod_bwd(x_ref[...], dh, g_ref[...], sc_ref[...])
        gx_ref[...] = dx1_ref[...] + dx
        accs[0] += _fold8(t_sh)
        accs[1] += _fold8(t_sc)
        accs[2] += _fold8(t_g)

        @pl.when(i == n - 1)
        def _():
            t = jnp.sum(accg[...], axis=1)
            dgains_ref[...] = t + pltpu.roll(t, HEAD_DIM, 1)
            sums_ref[...] = jnp.sum(accs[...], axis=1)

    return _row_call(
        body, name=name, row_ins=[qk_pre, *d_parts, dgates, x, dx1, cos_t, sa_t, sb_t],
        res_ins=[w_qkv, w_gates, gains, g1, sc1], row_outs=[(W_QKV, BF16), (D_MODEL, F32)],
        acc_outs=[((4, LANES), F32), ((3, D_MODEL), F32)],
        scratch=[pltpu.VMEM((4, 8, LANES), F32), pltpu.VMEM((3, 8, D_MODEL), F32), _stage_shape(name)])


def _post_attn_fwd(o_a, o_g, l_g, gates, x, w_pa, w_pb, w_o, b_gate, gt1, g2, sc2, sh2, *, name):
    d = D_MODEL

    def body(i, n, rin, res, rout, aout, scr):
        oa_ref, o0, o1, o2, l0, l1, l2, gates_ref, x_ref = rin
        wpa_ref, wpb_ref, wo_ref, b_ref, gt_ref, g_ref, sc_ref, sh_ref = res
        ob_ref, merged_ref, mo_ref, x1_ref, h2_ref = rout
        (stage,) = scr
        ogs = [_natural(r, stage, 0) for r in (o0, o1, o2)]
        ws = _mix_weights([_natural(r, stage, 2) for r in (l0, l1, l2)])
        obb = (ws[0] * ogs[0] + ws[1] * ogs[1] + ws[2] * ogs[2]).astype(BF16)
        ob_ref[...] = obb
        pa = _dot(oa_ref[...].astype(BF16), wpa_ref[...])
        pb = _dot(obb, wpb_ref[...])
        ga = _sigmoid(gates_ref[:, :d].astype(F32) + b_ref[:, :d])
        gb = _sigmoid(gates_ref[:, d:].astype(F32) + b_ref[:, d:])
        merged = (ga * pa + gb * pb).astype(BF16)
        merged_ref[...] = merged
        mo = _dot(merged, wo_ref[...])
        mo_ref[...] = mo.astype(BF16)
        x1 = x_ref[...] + gt_ref[...] * mo
        x1_ref[...] = x1
        h2_ref[...] = _rms_mod(x1, g_ref[...], sc_ref[...], sh_ref[...]).astype(BF16)

    return _row_call(body, name=name, row_ins=[o_a, *o_g, *l_g, gates, x],
                     res_ins=[w_pa, w_pb, w_o, b_gate, gt1, g2, sc2, sh2],
                     row_outs=[(WB_OUT, BF16), (d, BF16), (d, BF16), (d, F32), (d, BF16)], scratch=[_stage_shape(name)])


def _ffn_fwd(h2, w_ffn_in, *, name):
    def body(i, n, rin, res, rout, aout, scr):
        (h_ref,), (w_ref,), (act_ref, ff_ref) = rin, res, rout
        hv = h_ref[...]
        for q in range(2):
            a = _dot(hv, w_ref[:, q * FF_CHIP:(q + 1) * FF_CHIP])
            up = _dot(hv, w_ref[:, D_FF + q * FF_CHIP:D_FF + (q + 1) * FF_CHIP])
            sl = slice(q * FF_CHIP, (q + 1) * FF_CHIP)
            act_ref[:, sl] = (a * _sigmoid(a) * up).astype(BF16)
            ff_ref[:, sl] = a.astype(BF16)
            ff_ref[:, D_FF + q * FF_CHIP:D_FF + (q + 1) * FF_CHIP] = up.astype(BF16)

    return _row_call(body, name=name, row_ins=[h2], res_ins=[w_ffn_in], row_outs=[(D_FF, BF16), (2 * D_FF, BF16)])


def _ffn_mid(act, ff, x1, tgt, w_ffn_out, gt2, *, name):
    d = D_MODEL

    def body(i, n, rin, res, rout, aout, scr):
        act_ref, ff_ref, x1_ref, tgt_ref = rin
        wo_ref, gt_ref = res
        dy_ref, dffo_ref, dff_ref = rout
        dgt_ref, loss_ref = aout
        (acc,) = scr

        @pl.when(i == 0)
        def _():
            acc[...] = jnp.zeros_like(acc)

        ffo = _dot(act_ref[...], wo_ref[...])
        gtv = gt_ref[...]
        e = x1_ref[...] + gtv * ffo - tgt_ref[...]
        dy = e * (1.0 / d)
        dy_ref[...] = dy
        dffo = (gtv * dy).astype(BF16)
        dffo_ref[...] = dffo
        acc[0] += _fold8(dy * ffo)
        acc[1] += _fold8(e * e)
        for q in range(2):
            sl = slice(q * FF_CHIP, (q + 1) * FF_CHIP)
            su = slice(D_FF + q * FF_CHIP, D_FF + (q + 1) * FF_CHIP)
            dact = _dot_nt(dffo, wo_ref[sl, :])
            a = ff_ref[:, sl].astype(F32)
            up = ff_ref[:, su].astype(F32)
            sg = _sigmoid(a)
            dff_ref[:, sl] = (dact * up * (sg * (1.0 + a * (1.0 - sg)))).astype(BF16)
            dff_ref[:, su] = (dact * (a * sg)).astype(BF16)

        @pl.when(i == n - 1)
        def _():
            dgt_ref[...] = jnp.sum(acc[0], axis=0, keepdims=True)
            tot = jnp.sum(jnp.sum(acc[1], axis=0, keepdims=True), axis=1, keepdims=True)
            loss_ref[...] = jnp.broadcast_to(tot * (0.5 / d), (1, LANES))

    return _row_call(body, name=name, row_ins=[act, ff, x1, tgt], res_ins=[w_ffn_out, gt2],
                     row_outs=[(d, F32), (d, BF16), (2 * D_FF, BF16)], acc_outs=[((1, d), F32), ((1, LANES), F32)],
                     scratch=[pltpu.VMEM((2, 8, d), F32)])


def _ffn_in_bwd(dff, x1, dy, mo, w_ffn_in, g2, sc2, gt1, *, name):
    d = D_MODEL

    def body(i, n, rin, res, rout, aout, scr):
        dff_ref, x1_ref, dy_ref, mo_ref = rin
        w_ref, g_ref, sc_ref, gt_ref = res
        dx1_ref, dmo_ref = rout
        (sums_ref,) = aout
        (acc,) = scr

        @pl.when(i == 0)
        def _():
            acc[...] = jnp.zeros_like(acc)

        dh = _dot_nt(dff_ref[...], w_ref[...])
        dx, t_sh, t_sc, t_g = _rms_mod_bwd(x1_ref[...], dh, g_ref[...], sc_ref[...])
        dx1 = dy_ref[...] + dx
        dx1_ref[...] = dx1
        dmo_ref[...] = (gt_ref[...] * dx1).astype(BF16)
        acc[0] += _fold8(t_sh)
        acc[1] += _fold8(t_sc)
        acc[2] += _fold8(t_g)
        acc[3] += _fold8(dx1 * mo_ref[...].astype(F32))

        @pl.when(i == n - 1)
        def _():
            sums_ref[...] = jnp.sum(acc[...], axis=1)

    return _row_call(body, name=name, row_ins=[dff, x1, dy, mo], res_ins=[w_ffn_in, g2, sc2, gt1],
                     row_outs=[(d, F32), (d, BF16)], acc_outs=[((4, d), F32)], scratch=[pltpu.VMEM((4, 8, d), F32)])


def _post_attn_bwd(dmo, gates, o_a, o_g, l_g, w_pa, w_pb, w_o, b_gate, *, name):
    d = D_MODEL

    def body(i, n, rin, res, rout, aout, scr):
        dmo_ref, gates_ref, oa_ref, o0, o1, o2, l0, l1, l2 = rin
        wpa_ref, wpb_ref, wo_ref, b_ref = res
        dpa_ref, dpb_ref, dgates_ref, doa_ref = rout[:4]
        do_refs, dl_refs = rout[4:7], rout[7:10]
        (dbg_ref,) = aout
        acc, stage = scr

        @pl.when(i == 0)
        def _():
            acc[...] = jnp.zeros_like(acc)

        ogs = [_natural(r, stage, 0) for r in (o0, o1, o2)]
        ws = _mix_weights([_natural(r, stage, 2) for r in (l0, l1, l2)])
        obb = (ws[0] * ogs[0] + ws[1] * ogs[1] + ws[2] * ogs[2]).astype(BF16)
        pa = _dot(oa_ref[...].astype(BF16), wpa_ref[...])
        pb = _dot(obb, wpb_ref[...])
        ga = _sigmoid(gates_ref[:, :d].astype(F32) + b_ref[:, :d])
        gb = _sigmoid(gates_ref[:, d:].astype(F32) + b_ref[:, d:])
        dm = _dot_nt(dmo_ref[...], wo_ref[...])
        dpa = (dm * ga).astype(BF16)
        dpb = (dm * gb).astype(BF16)
        dpa_ref[...] = dpa
        dpb_ref[...] = dpb
        dga = dm * pa * ga * (1.0 - ga)
        dgb = dm * pb * gb * (1.0 - gb)
        dgates_ref[:, :d] = dga.astype(BF16)
        dgates_ref[:, d:] = dgb.astype(BF16)
        acc[:, :d] += _fold8(dga)
        acc[:, d:] += _fold8(dgb)
        doa_ref[...] = _dot_nt(dpa, wpa_ref[...])
        dob = _dot_nt(dpb, wpb_ref[...])
        lo = _head_lanes()
        for ch in range(WB_OUT // LANES):
            sl = slice(ch * LANES, (ch + 1) * LANES)
            dv = dob[:, sl]
            wc = [w[:, sl] for w in ws]
            ts = [_head_mean(dv * og[:, sl], lo) * float(HEAD_DIM) for og in ogs]
            tbar = wc[0] * ts[0] + wc[1] * ts[1] + wc[2] * ts[2]
            for g in range(3):
                for k, (ref, val) in enumerate(((do_refs[g], wc[g] * dv), (dl_refs[g], wc[g] * (ts[g] - tbar)))):
                    if len(ref.shape) == 2:
                        ref[:, sl] = val
                    else:
                        _to_residue(val, ref, ch * LANES, stage, (2 * g + k) % 4)

        @pl.when(i == n - 1)
        def _():
            dbg_ref[...] = jnp.sum(acc[...], axis=0, keepdims=True)

    return _row_call(body, name=name, row_ins=[dmo, gates, o_a, *o_g, *l_g], res_ins=[w_pa, w_pb, w_o, b_gate],
                     row_outs=[(d, BF16), (d, BF16), (2 * d, BF16), (WA, F32)]
                     + 2 * [(WB_OUT, F32, dd) for _, dd in DIL_CONFIGS],
                     acc_outs=[((1, 2 * d), F32)], scratch=[pltpu.VMEM((8, 2 * d), F32), _stage_shape(name)])


def _na_class_tables():
    ro = np.full((3, NA_QROWS, 2 * NA_PAIRS), NA_RO_NONE, np.int64)
    slot = np.zeros((3, NA_QROWS, NA_PAIRS), np.int64)
    for t in range(3):
        for a in range(NA_QROWS):
            qr = _NA_CLASS_R0[t] + a
            rs = min(max(qr - NA_KH // 2, 0), ROWS - NA_KH)
            for b in range(NA_KROWS):
                kr = _NA_CLASS_K0[t] + b
                if rs <= kr < rs + NA_KH:
                    ro[t, a, b] = kr - qr + (NA_KH - 1)
            for j in range(NA_PAIRS):
                slot[t, a, j] = 2 * j - a + (_NA_CLASS_K0[t] - _NA_CLASS_R0[t] + NA_KH - 1) + (NA_QROWS - 1)
    assert slot.min() >= 0 and slot.max() < NA_SLOTS
    return ro, slot


def _na_build_bias(i, cls_ref, rp_ref, cm_ref, bias_scr):
    ro, _ = _na_class_tables()
    lo = _head_lanes()
    first = jnp.logical_or(i == 0, cls_ref[i] != cls_ref[jnp.maximum(i - 1, 0)])
    for t in range(3):
        @pl.when(jnp.logical_and(first, cls_ref[i] == t))
        def _():
            for hh in range(2):
                for a in range(NA_QROWS):
                    for j in range(NA_PAIRS):
                        r0, r1 = int(ro[t, a, 2 * j]), int(ro[t, a, 2 * j + 1])
                        x0 = jnp.broadcast_to(rp_ref[hh, r0:r0 + 1, :], (GRID_W, LANES))
                        x1 = jnp.broadcast_to(rp_ref[hh, r1:r1 + 1, :], (GRID_W, LANES))
                        blk = jnp.where(lo, pltpu.roll(x0, GRID_W + 1, 1, stride=1, stride_axis=0),
                                        pltpu.roll(x1, 1, 1, stride=1, stride_axis=0))
                        bias_scr[hh, a * GRID_W:(a + 1) * GRID_W, j * LANES:(j + 1) * LANES] = blk + cm_ref[...]
    return first


def _attn_fwd(qkv, qc0, kc0, vc0, npairs, table, kstart, cls, nk, *, name, na=None, qb=Q_BLOCK):
    s = qkv.shape[0]

    def body(ks_ref, cls_ref, q_ref, k_ref, v_ref, b_ref, *rest):
        if na:
            cm_ref, o_ref, lse_ref, bias_scr = rest
        else:
            o_ref, lse_ref = rest
        i = pl.program_id(1)
        if na:
            _na_build_bias(i, cls_ref, b_ref, cm_ref, bias_scr)
        ks = pl.multiple_of(ks_ref[i], 64)
        q2 = q_ref[...]
        k2 = k_ref[pl.ds(ks, nk), :]
        v2 = v_ref[pl.ds(ks, nk), :]
        lo = _head_lanes()
        outs, lses = [], []
        for h in range(2):
            qm = jnp.where(lo if h == 0 else jnp.logical_not(lo), q2, jnp.zeros_like(q2))
            sc = _dot_nt(qm, k2) * SCALE + (bias_scr[h, :, :nk] if na else b_ref[0, 0])
            m = jnp.max(sc, axis=1, keepdims=True)
            p = jnp.exp(sc - m)
            l = jnp.sum(p, axis=1, keepdims=True)
            pv = _dot(p.astype(BF16), v2)
            outs.append(pv / l)
            lses.append(m + jnp.log(l))
        o_ref[...] = jnp.where(lo, outs[0], outs[1])
        lse_ref[...] = jnp.where(lo, lses[0], lses[1])

    w = npairs * LANES
    in_specs = [
        pl.BlockSpec((qb, LANES), lambda p, i, ks, cl: (i, qc0 + p)),
        pl.BlockSpec((s, LANES), lambda p, i, ks, cl: (0, kc0 + p)),
        pl.BlockSpec((s, LANES), lambda p, i, ks, cl: (0, vc0 + p)),
    ]
    if na:
        in_specs += _na_bias_specs()
        args, scratch = (kstart, cls, qkv, qkv, qkv, *na), [pltpu.VMEM((2, Q_BLOCK, NA_W), F32)]
    else:
        in_specs.append(pl.BlockSpec((1, 1, qb, nk), lambda p, i, ks, cl: (cl[i], 0, 0, 0)))
        args, scratch = (kstart, cls, qkv, qkv, qkv, table), []
    grid_spec = pltpu.PrefetchScalarGridSpec(
        num_scalar_prefetch=2, grid=(npairs, s // qb), in_specs=in_specs,
        out_specs=[pl.BlockSpec((qb, LANES), lambda p, i, ks, cl: (i, p)),
                   pl.BlockSpec((qb, LANES), lambda p, i, ks, cl: (i, p))],
        scratch_shapes=scratch,
    )
    return _pcall(body, name=name, grid_spec=grid_spec,
                  out_shape=[jax.ShapeDtypeStruct((s, w), F32), jax.ShapeDtypeStruct((s, w), F32)],
                  compiler_params=pltpu.CompilerParams(dimension_semantics=("parallel", "arbitrary")),
                  )(*args)


def _na_bias_specs():
    return [pl.BlockSpec((2, 16, LANES), lambda p, i, ks, cl: (p, 0, 0)),
            pl.BlockSpec((GRID_W, LANES), lambda p, i, ks, cl: (0, 0))]


def _attn_bwd(qkv, qc0, kc0, vc0, npairs, table, kstart, cls, nk, do, o, lse, *, name, dlse=None, na=None,
              qb=Q_BLOCK, order=None):
    s = qkv.shape[0]
    has_dlse = dlse is not None
    _, slot = _na_class_tables()

    def body(ks_ref, cls_ref, q_ref, k_ref, v_ref, b_ref, *rest):
        if na:
            cm_ref, rest = rest[0], rest[1:]
        do_ref, o_ref, lse_ref, rest = rest[0], rest[1], rest[2], rest[3:]
        if has_dlse:
            dlse_ref, rest = rest[0], rest[1:]
        if order is not None:
            rest = rest[1:]
        dq_ref, dk_ref, dv_ref = rest[0], rest[1], rest[2]
        if na:
            bank_ref, bias_scr, dbias_scr, bank_scr = rest[3:]
        i = pl.program_id(1)
        if na:
            first = _na_build_bias(i, cls_ref, b_ref, cm_ref, bias_scr)

        @pl.when(i == 0)
        def _():
            dk_ref[...] = jnp.zeros_like(dk_ref)
            dv_ref[...] = jnp.zeros_like(dv_ref)
            if na:
                dbias_scr[...] = jnp.zeros_like(dbias_scr)
                bank_scr[...] = jnp.zeros_like(bank_scr)

        ks = pl.multiple_of(ks_ref[i], 64)
        q2 = q_ref[...]
        k2 = k_ref[pl.ds(ks, nk), :]
        v2 = v_ref[pl.ds(ks, nk), :]
        do2 = do_ref[...]
        lse2 = lse_ref[...]
        doo = do2 * o_ref[...]
        do2b = do2.astype(BF16)
        lo = _head_lanes()
        lane = lax.broadcasted_iota(jnp.int32, (1, LANES), 1)
        dqs, dks, dvs = [], [], []
        for h in range(2):
            mh = lo if h == 0 else jnp.logical_not(lo)
            qm = jnp.where(mh, q2, jnp.zeros_like(q2))
            sc = _dot_nt(qm, k2) * SCALE + (bias_scr[h, :, :nk] if na else b_ref[0, 0])
            lse_h = jnp.max(jnp.where(mh, lse2, NEG), axis=1, keepdims=True)
            p = jnp.exp(sc - lse_h)
            delta = jnp.sum(jnp.where(mh, doo, 0.0), axis=1, keepdims=True)
            dom = jnp.where(mh, do2b, jnp.zeros_like(do2b))
            dp = _dot_nt(dom, v2)
            t = dp - delta
            if has_dlse:
                t = t + jnp.sum(jnp.where(lane == h * HEAD_DIM, dlse_ref[...], 0.0), axis=1, keepdims=True)
            ds = p * t
            if na:
                @pl.when(first)
                def _():
                    dbias_scr[h, :, :nk] = ds

                @pl.when(jnp.logical_not(first))
                def _():
                    dbias_scr[h, :, :nk] += ds
            dsb = ds.astype(BF16)
            dqs.append(_dot(dsb, k2))
            dks.append(lax.dot_general(dsb, q2, _TN, preferred_element_type=F32))
            dvs.append(lax.dot_general(p.astype(BF16), do2b, _TN, preferred_element_type=F32))
        dq_ref[...] = jnp.where(lo, dqs[0], dqs[1]) * SCALE
        dk_ref[pl.ds(ks, nk), :] += jnp.where(lo, dks[0], dks[1]) * SCALE
        dv_ref[pl.ds(ks, nk), :] += jnp.where(lo, dvs[0], dvs[1])
        if na:
            last = jnp.logical_or(i == N_QBLK - 1, cls_ref[i] != cls_ref[jnp.minimum(i + 1, N_QBLK - 1)])
            for t in range(3):
                @pl.when(jnp.logical_and(last, cls_ref[i] == t))
                def _():
                    for hh in range(2):
                        for a in range(NA_QROWS):
                            for j in range(NA_PAIRS):
                                bank_scr[hh, int(slot[t, a, j])] += dbias_scr[
                                    hh, a * GRID_W:(a + 1) * GRID_W, j * LANES:(j + 1) * LANES]

            @pl.when(i == N_QBLK - 1)
            def _():
                bank_ref[...] = bank_scr[...]

    w = npairs * LANES
    blk = lambda: pl.BlockSpec((qb, LANES), lambda p, i, ks, cl: (i, p))
    full = lambda: pl.BlockSpec((s, LANES), lambda p, i, ks, cl: (0, p))
    in_specs = [
        pl.BlockSpec((qb, LANES), lambda p, i, ks, cl: (i, qc0 + p)),
        pl.BlockSpec((s, LANES), lambda p, i, ks, cl: (0, kc0 + p)),
        pl.BlockSpec((s, LANES), lambda p, i, ks, cl: (0, vc0 + p)),
    ]
    if na:
        in_specs += _na_bias_specs()
        args = [kstart, cls, qkv, qkv, qkv, *na]
    else:
        in_specs.append(pl.BlockSpec((1, 1, qb, nk), lambda p, i, ks, cl: (cl[i], 0, 0, 0)))
        args = [kstart, cls, qkv, qkv, qkv, table]
    in_specs += [blk(), blk(), blk()]
    args += [do, o, lse]
    if has_dlse:
        in_specs.append(blk())
        args.append(dlse)
    out_specs = [blk(), full(), full()]
    out_shape = [jax.ShapeDtypeStruct((s, w), F32)] * 3
    scratch = []
    if na:
        bank_shape = (2, NA_SLOTS, GRID_W, LANES)
        out_specs.append(pl.BlockSpec(bank_shape, lambda p, i, ks, cl: (p, 0, 0, 0)))
        out_shape.append(jax.ShapeDtypeStruct((2 * npairs,) + bank_shape[1:], F32))
        scratch = [pltpu.VMEM((2, Q_BLOCK, NA_W), F32), pltpu.VMEM((2, Q_BLOCK, NA_W), F32), pltpu.VMEM(bank_shape, F32)]
    if order is not None:
        in_specs.append(pl.BlockSpec(order.shape, lambda p, i, ks, cl: (0, 0)))
        args.append(order)
    grid_spec = pltpu.PrefetchScalarGridSpec(num_scalar_prefetch=2, grid=(npairs, s // qb), in_specs=in_specs,
                                             out_specs=out_specs, scratch_shapes=scratch)
    return _pcall(body, name=name, grid_spec=grid_spec, out_shape=out_shape,
                  compiler_params=pltpu.CompilerParams(dimension_semantics=("arbitrary", "arbitrary")))(*args)


_NA_CLASS_R0 = (0, NA_QROWS, ROWS - NA_QROWS)
_NA_CLASS_K0 = (0, 0, ROWS - NA_KROWS)
_RPB_RO = 2 * NA_KH - 1
_RPB_CO = 2 * NA_KW - 1
_BANK_ROWS = 48


def _na_constants():
    col = np.arange(GRID_W)
    cs = np.clip(col - NA_KW // 2, 0, GRID_W - NA_KW)
    vcol = (col[None, :] >= cs[:, None]) & (col[None, :] < cs[:, None] + NA_KW)
    colmask = np.where(np.concatenate([vcol, vcol], axis=1), 0.0, NEG).astype(np.float32)
    co = col[None, :] - col[:, None] + (NA_KW - 1)
    oh_col = np.zeros((GRID_W * GRID_W, LANES), np.float32)
    for qc in range(GRID_W):
        for kc in range(GRID_W):
            if vcol[qc, kc]:
                oh_col[qc * GRID_W + kc, co[qc, kc]] = 1.0
    ks = np.clip(np.arange(N_QBLK) * NA_QROWS - NA_KH // 2, 0, ROWS - NA_KROWS) * GRID_W
    cls = np.ones(N_QBLK, np.int32)
    cls[0], cls[-1] = 0, 2
    return colmask, oh_col, ks.astype(np.int32), cls


def _bank_reduce(bank, oh_col, *, name):
    def body(d_ref, ohc_ref, o_ref):
        o_ref[0] = jnp.dot(d_ref[0], ohc_ref[...], preferred_element_type=F32, precision=lax.Precision.HIGHEST)

    return _pcall(
        body, name=name, grid=(NA_HEADS,),
        in_specs=[pl.BlockSpec((1, _BANK_ROWS, GRID_W * GRID_W), lambda h: (h, 0, 0)),
                  pl.BlockSpec((GRID_W * GRID_W, LANES), lambda h: (0, 0))],
        out_specs=pl.BlockSpec((1, _BANK_ROWS, LANES), lambda h: (h, 0, 0)),
        out_shape=jax.ShapeDtypeStruct((NA_HEADS, _BANK_ROWS, LANES), F32), compiler_params=_PAR,
    )(bank, oh_col)


def _dil_constants(dilation):
    seg = SEQ // dilation
    nb = seg // DIL_QB
    nk = min(DIL_QB + 2 * DIL_HALF, seg)
    starts = [min(max(blk * DIL_QB - DIL_HALF, 0), seg - nk) for blk in range(nb)]
    shifts = sorted({w0 - blk * DIL_QB for blk, w0 in enumerate(starts)}, reverse=True)
    qi = np.arange(DIL_QB)[:, None]
    ki = np.arange(nk)[None, :]
    mask = np.stack([np.where(np.abs(ki + sh - qi) <= DIL_HALF, 0.0, NEG) for sh in shifts]).astype(np.float32)
    ks, cls = [], []
    for i in range(SEQ // DIL_QB):
        sub, blk = divmod(i, nb)
        cls.append(shifts.index(starts[blk] - blk * DIL_QB))
        ks.append(sub * seg + starts[blk])
    return mask.reshape(len(shifts), 1, DIL_QB, nk), np.asarray(ks, np.int32), np.asarray(cls, np.int32), nk


_VM = pl.BlockSpec(memory_space=pltpu.VMEM)


def _ada_fwd(c_all, w, b, *, name):
    def body(c_ref, w_ref, b_ref, o_ref):
        cv = c_ref[...]
        o_ref[...] = jnp.dot(cv * _sigmoid(cv), w_ref[...], preferred_element_type=F32,
                             precision=lax.Precision.HIGHEST) + b_ref[...]

    return _pcall(body, name=name, in_specs=[_VM, _VM, _VM], out_specs=_VM,
                  out_shape=jax.ShapeDtypeStruct((c_all.shape[0], w.shape[1]), F32))(c_all, w, b)


def _ada_bwd(c_all_t, dmod, *, name):
    def body(c_ref, d_ref, o_ref):
        cv = c_ref[...]
        o_ref[...] = jnp.dot(cv * _sigmoid(cv), d_ref[...], preferred_element_type=F32,
                             precision=lax.Precision.HIGHEST)

    return _pcall(body, name=name, in_specs=[_VM, _VM], out_specs=_VM,
                  out_shape=jax.ShapeDtypeStruct((c_all_t.shape[0], dmod.shape[1]), F32))(c_all_t, dmod)


def _row_sum(t, *, name):
    def body(t_ref, o_ref):
        o_ref[...] = jnp.sum(t_ref[...], axis=0, keepdims=True)

    return _pcall(body, name=name, in_specs=[_VM], out_specs=_VM,
                  out_shape=jax.ShapeDtypeStruct((1, t.shape[1]), F32))(t)


def _row_tile(rows):
    tr = rows
    for cand in range(8, 513, 8):
        if rows % cand == 0:
            tr = cand
    return tr


def _adamw_math(wv, gv, mv, vv):
    nm = ADAM_B1 * mv + (1.0 - ADAM_B1) * gv
    nv = ADAM_B2 * vv + (1.0 - ADAM_B2) * (gv * gv)
    m_hat = nm / (1.0 - ADAM_B1 ** ADAM_STEP)
    v_hat = nv / (1.0 - ADAM_B2 ** ADAM_STEP)
    return -ADAM_LR * (m_hat / (jnp.sqrt(v_hat) + ADAM_EPS) + ADAM_WD * wv), nm, nv


def _adamw(w, g, m, v, *, name):
    rows, cols = w.shape
    tr = _row_tile(rows)

    def body(w_ref, g_ref, m_ref, v_ref, d_ref, nm_ref, nv_ref):
        d_ref[...], nm_ref[...], nv_ref[...] = _adamw_math(w_ref[...], g_ref[...], m_ref[...], v_ref[...])

    spec = pl.BlockSpec((tr, cols), lambda i: (i, 0))
    return _pcall(body, name=name, grid=(rows // tr,), in_specs=[spec] * 4, out_specs=[spec] * 3,
                  out_shape=[jax.ShapeDtypeStruct((rows, cols), F32)] * 3, compiler_params=_PAR)(w, g, m, v)


def _adamw_halves(w, g_mine, g_other, m, v, c_arr, *, name):
    rows, cols = w.shape
    hr = rows // 2
    tr = _row_tile(hr)
    nt = hr // tr

    def body(c_ref, w_ref, t_ref, o_ref, m_ref, v_ref, g_ref, d_ref, nm_ref, nv_ref):
        gv = jnp.where(pl.program_id(0) == c_ref[0], t_ref[...], o_ref[...])
        g_ref[...] = gv
        d_ref[...], nm_ref[...], nv_ref[...] = _adamw_math(w_ref[...], gv, m_ref[...], v_ref[...])

    full = pl.BlockSpec((tr, cols), lambda h, i, c: (h * nt + i, 0))
    half = pl.BlockSpec((tr, cols), lambda h, i, c: (i, 0))
    grid_spec = pltpu.PrefetchScalarGridSpec(num_scalar_prefetch=1, grid=(2, nt),
                                             in_specs=[full, half, half, full, full], out_specs=[full] * 4)
    return _pcall(body, name=name, grid_spec=grid_spec, out_shape=[jax.ShapeDtypeStruct((rows, cols), F32)] * 4,
                  compiler_params=pltpu.CompilerParams(dimension_semantics=("parallel", "parallel")),
                  )(c_arr, w, g_mine, g_other, m, v)


_MESH = pl.DeviceIdType.MESH
_ANY = pl.BlockSpec(memory_space=pl.ANY)
_CHIP_FLIPS = ((1, 0), (0, 1), (1, 1))


def _pos():
    return lax.axis_index("x"), lax.axis_index("y"), lax.axis_index("c")


def _flip(v, f):
    return 1 - v if f else v


def _sem_pairs(n):
    return [pltpu.SemaphoreType.DMA((n,)), pltpu.SemaphoreType.DMA((n,))]


def _small_allgather(blk, *, name):
    m_per, n = blk.shape

    def body(x_ref, out_ref, send_sems, recv_sems, local_sem):
        x, y, c = _pos()
        me, sibling = (x, y, c), (x, y, 1 - c)
        chips = [(_flip(x, fx), _flip(y, fy)) for fx, fy in _CHIP_FLIPS]

        def rows(px, py, pc):
            return out_ref.at[pl.ds((4 * px + 2 * py + pc) * m_per, m_per), :]

        def copy(k, block, to, src=None):
            return pltpu.make_async_remote_copy(
                src_ref=rows(*block) if src is None else src, dst_ref=rows(*block),
                send_sem=send_sems.at[k], recv_sem=recv_sems.at[k], device_id=to, device_id_type=_MESH)

        mine = pltpu.make_async_copy(x_ref, rows(*me), local_sem)
        mine.start()
        first = [copy(0, me, sibling, src=x_ref)]
        first += [copy(1 + j, me, (*chip, c), src=x_ref) for j, chip in enumerate(chips)]
        for cp in first:
            cp.start()
        passed = [copy(4 + j, (*chip, c), sibling) for j, chip in enumerate(chips)]
        for j, chip in enumerate(chips):
            copy(1 + j, (*chip, c), me).wait_recv()
            passed[j].start()
        copy(0, sibling, me).wait_recv()
        for j, chip in enumerate(chips):
            copy(4 + j, (*chip, 1 - c), me).wait_recv()
        for cp in first + passed:
            cp.wait_send()
        mine.wait()

    return _pcall(
        body, name=name, out_shape=jax.ShapeDtypeStruct((N_DEV * m_per, n), blk.dtype),
        in_specs=[_VM], out_specs=_VM,
        scratch_shapes=_sem_pairs(7) + [pltpu.SemaphoreType.DMA],
    )(blk)


_HBM = pl.BlockSpec(memory_space=pltpu.HBM)
_SEM = pl.BlockSpec(memory_space=pltpu.SEMAPHORE)
_EFFECT = pltpu.SideEffectType.DATAFLOW_SIDE_EFFECTING


def _split_start(srcs, lands, plan, ncopies, after, *, name, alias_sources=True):
    ns, nl = len(srcs), len(lands)
    thru = (*srcs, *lands) if alias_sources else tuple(lands)

    def body(*refs):
        src_refs, land_refs = refs[:ns], refs[ns:ns + nl]
        send_sems, recv_sems = refs[ns + nl + 1], refs[ns + nl + 2]
        token = refs[-1]
        x, y, c = _pos()
        for k, (src, dst, to, _) in enumerate(plan(x, y, c, src_refs, land_refs)):
            pltpu.make_async_remote_copy(src_ref=src, dst_ref=dst, send_sem=send_sems.at[k], recv_sem=recv_sems.at[k],
                                         device_id=to, device_id_type=_MESH).start()
        token[...] = jnp.zeros_like(token)

    hbm = lambda a: pltpu.HBM(a.shape, a.dtype)
    out = _pcall(
        body, name=name,
        out_shape=(pltpu.SemaphoreType.DMA((ncopies,)), pltpu.SemaphoreType.DMA((ncopies,)),
                   *[hbm(a) for a in thru], jax.ShapeDtypeStruct((8, LANES), F32)),
        in_specs=[_HBM] * (ns + nl) + [_ANY], out_specs=(_SEM, _SEM, *[_HBM] * len(thru), _VM),
        input_output_aliases={ns + nl - len(thru) + i: 2 + i for i in range(len(thru))},
        compiler_params=pltpu.CompilerParams(has_side_effects=_EFFECT),
    )(*[pltpu.with_memory_space_constraint(a, pltpu.HBM) for a in (*srcs, *lands)], after)
    if alias_sources:
        return out[0], out[1], list(out[2:2 + ns]), list(out[2 + ns:2 + ns + nl]), out[-1]
    return out[0], out[1], list(srcs), list(out[2:2 + nl]), out[-1]


def _split_wait(send_sems, recv_sems, srcs, lands, plan, after, *, name, with_sources=False, alias_sources=True):
    ns, nl = len(srcs), len(lands)
    thru = (*srcs, *lands) if alias_sources else tuple(lands)

    def body(*refs):
        src_refs, land_refs = refs[:ns], refs[ns:ns + nl]
        send_sems, recv_sems = refs[ns + nl], refs[ns + nl + 1]
        x, y, c = _pos()
        for k, (src, _, _, mine) in enumerate(plan(x, y, c, src_refs, land_refs)):
            cp = pltpu.make_async_remote_copy(src_ref=src, dst_ref=mine, send_sem=send_sems.at[k],
                                              recv_sem=recv_sems.at[k], device_id=(x, y, c), device_id_type=_MESH)
            cp.wait_send()
            cp.wait_recv()

    hbm = lambda a: pltpu.HBM(a.shape, a.dtype)
    out = _pcall(
        body, name=name, out_shape=tuple(hbm(a) for a in thru),
        in_specs=[_HBM] * (ns + nl) + [_SEM, _SEM, _ANY], out_specs=tuple([_HBM] * len(thru)),
        input_output_aliases={ns + nl - len(thru) + i: i for i in range(len(thru))},
        compiler_params=pltpu.CompilerParams(has_side_effects=_EFFECT),
    )(*srcs, *lands, send_sems, recv_sems, after)
    got_srcs, got_lands = (list(out[:ns]), list(out[ns:])) if alias_sources else (list(srcs), list(out))
    return (got_srcs, got_lands) if with_sources else got_lands


def _ag_plan(nw):
    def plan(x, y, c, sh_refs, full_refs):
        j = 2 * x + y
        out = []
        for wi in range(nw):
            for fx, fy in _CHIP_FLIPS:
                px, py = _flip(x, fx), _flip(y, fy)
                out.append((sh_refs[wi].at[c], full_refs[wi].at[j, c], (px, py, c), full_refs[wi].at[2 * px + py, c]))
            out.append((sh_refs[wi], full_refs[wi].at[j], (x, y, 1 - c), full_refs[wi].at[j]))
        return out
    return plan


def _ag_plan_direct(nw):
    def plan(x, y, c, sh_refs, full_refs):
        j = 2 * x + y
        out = []
        for wi in range(nw):
            for fx, fy in _CHIP_FLIPS:
                px, py = _flip(x, fx), _flip(y, fy)
                for rel in (0, 1):
                    t = _flip(c, rel)
                    out.append((sh_refs[wi].at[c], full_refs[wi].at[j, c], (px, py, t), full_refs[wi].at[2 * px + py, t]))
            out.append((sh_refs[wi], full_refs[wi].at[j], (x, y, 1 - c), full_refs[wi].at[j]))
        return out
    return plan


def _ag_pass(fulls, *, name):
    nw = len(fulls)

    def body(*refs):
        in_refs, out_refs = refs[:nw], refs[nw:2 * nw]
        send_sems, recv_sems = refs[2 * nw:]
        x, y, c = _pos()
        cps = []
        for wi in range(nw):
            for k, (fx, fy) in enumerate(_CHIP_FLIPS):
                jp = 2 * _flip(x, fx) + _flip(y, fy)
                sems = dict(send_sem=send_sems.at[3 * wi + k], recv_sem=recv_sems.at[3 * wi + k], device_id_type=_MESH)
                send = pltpu.make_async_remote_copy(src_ref=in_refs[wi].at[jp, c], dst_ref=out_refs[wi].at[jp, c],
                                                    device_id=(x, y, 1 - c), **sems)
                recv = pltpu.make_async_remote_copy(src_ref=in_refs[wi].at[jp, c], dst_ref=out_refs[wi].at[jp, 1 - c],
                                                    device_id=(x, y, c), **sems)
                cps.append((send, recv))
        for send, _ in cps:
            send.start()
        for send, recv in cps:
            send.wait_send()
            recv.wait_recv()

    return _pcall(body, name=name, out_shape=[jax.ShapeDtypeStruct(f.shape, f.dtype) for f in fulls],
                  in_specs=[_ANY] * nw, out_specs=[_ANY] * nw, input_output_aliases={i: i for i in range(nw)},
                  scratch_shapes=_sem_pairs(3 * nw))(*fulls)


def _sib_plan(nw):
    def plan(x, y, c, g_refs, ra_refs):
        return [(g_refs[wi].at[k, 1 - c], ra_refs[wi].at[k], (x, y, 1 - c), ra_refs[wi].at[k])
                for wi in range(nw) for k in range(N_CHIP)]
    return plan


def _rs_plan(nw):
    def plan(x, y, c, s_refs, rb_refs):
        out = []
        for wi in range(nw):
            for k, (fx, fy) in enumerate(_CHIP_FLIPS):
                px, py = _flip(x, fx), _flip(y, fy)
                out.append((s_refs[wi].at[2 * px + py], rb_refs[wi].at[k], (px, py, c), rb_refs[wi].at[k]))
        return out
    return plan


def _sibling_swap(ts, *, name):
    nw = len(ts)

    def body(*refs):
        t_refs, out_refs = refs[:nw], refs[nw:2 * nw]
        send_sems, recv_sems = refs[2 * nw:]
        x, y, c = _pos()
        cps = [pltpu.make_async_remote_copy(src_ref=t_refs[wi], dst_ref=out_refs[wi], send_sem=send_sems.at[wi],
                                            recv_sem=recv_sems.at[wi], device_id=(x, y, 1 - c), device_id_type=_MESH)
               for wi in range(nw)]
        for cp in cps:
            cp.start()
        for cp in cps:
            cp.wait()

    return _pcall(body, name=name, out_shape=[jax.ShapeDtypeStruct(t.shape, t.dtype) for t in ts],
                  in_specs=[_ANY] * nw, out_specs=[_ANY] * nw, scratch_shapes=_sem_pairs(nw))(*ts)


def _rs_add(g, ra, c_arr, *, name):
    n, _, r, w = g.shape

    def body(c_ref, g_ref, ra_ref, sb_ref):
        sb_ref[...] = (g_ref[...] + ra_ref[...]).astype(BF16)

    grid_spec = pltpu.PrefetchScalarGridSpec(
        num_scalar_prefetch=1, grid=(n,),
        in_specs=[pl.BlockSpec((None, None, r, w), lambda k, c: (k, c[0], 0, 0)),
                  pl.BlockSpec((None, r, w), lambda k, c: (k, 0, 0))],
        out_specs=pl.BlockSpec((None, r, w), lambda k, c: (k, 0, 0)))
    return _pcall(body, name=name, grid_spec=grid_spec, out_shape=jax.ShapeDtypeStruct((n, r, w), BF16),
                  compiler_params=_PAR)(c_arr, g, ra)


def _rs_final(g, ra, rb, jc_arr, *, name):
    _, _, r, w = g.shape

    def body(jc_ref, g_ref, ra_ref, rb_ref, t_ref):
        s = g_ref[...] + ra_ref[...]
        t_ref[...] = ((s + rb_ref[0].astype(F32)) + rb_ref[1].astype(F32)) + rb_ref[2].astype(F32)

    grid_spec = pltpu.PrefetchScalarGridSpec(
        num_scalar_prefetch=1, grid=(1,),
        in_specs=[pl.BlockSpec((None, None, r, w), lambda i, jc: (jc[0], jc[1], 0, 0)),
                  pl.BlockSpec((None, r, w), lambda i, jc: (jc[0], 0, 0)),
                  pl.BlockSpec((3, r, w), lambda i, jc: (0, 0, 0))],
        out_specs=pl.BlockSpec((r, w), lambda i, jc: (0, 0)))
    return _pcall(body, name=name, grid_spec=grid_spec, out_shape=jax.ShapeDtypeStruct((r, w), F32),
                  compiler_params=_ARB)(jc_arr, g, ra, rb)


def _tile2(g):
    return jnp.concatenate([g, g], axis=1)


_BIG = ("w_in", "w_ffn_in", "w_ffn_out", "w_o", "w_proj_a", "w_proj_b")
_BIG_SHARD = {"w_in": (1024, 1472), "w_ffn_in": (1024, 1408), "w_ffn_out": (704, 1024), "w_o": (256, 1024),
              "w_proj_a": (512, 256), "w_proj_b": (256, 256)}


def _device_step(x2, tgt, mod, first_weights, late_weights, early_grads, mid_grads, g_norm1, g_norm2, b_gate, g_qa, g_ka,
                 g_qb, g_kb, rpb):
    d = D_MODEL
    sh1, sc1, gt1, sh2, sc2, gt2 = [mod[:, k * d:(k + 1) * d] for k in range(6)]

    colmask, oh_col, na_ks, na_cls = _na_constants()
    rp = jnp.pad(rpb, ((0, 0), (0, 16 - _RPB_RO), (RP_LANE0, LANES - RP_LANE0 - _RPB_CO)), constant_values=NEG)
    na = (rp, jnp.asarray(colmask))
    na_ks, na_cls = jnp.asarray(na_ks), jnp.asarray(na_cls)
    dil = [_dil_constants(dd) for _, dd in DIL_CONFIGS]
    gains = jnp.concatenate([_tile2(g_qa), _tile2(g_ka), _tile2(g_qb), _tile2(g_kb)], axis=0)
    cos_t, sa_t, sb_t = _rope_tables()

    wts = first_weights(cos_t)
    h1, qkvn, qk_pre, gates, *qkv_dil = _pre_attn_fwd(x2, cos_t, sa_t, sb_t, g_norm1, sc1, sh1, wts["w_qkv"],
                                                      wts["w_gates"], gains, name="pre_attn_fwd")
    o_a, lse_a = _attn_fwd(qkvn, 0, 4, 8, 4, None, na_ks, na_cls, NA_NK, name="attn_a_fwd", na=na)
    arrs, o_g, l_g = [], [], []
    res = lambda t, dd: t if dd == 1 else (t, dd)
    for g, (_, dd) in enumerate(DIL_CONFIGS):
        tab_g, ks_g, cls_g, nk_g = jnp.asarray(dil[g][0]), jnp.asarray(dil[g][1]), jnp.asarray(dil[g][2]), dil[g][3]
        arr, cb = (qkvn, (12, 18, 24)) if dd == 1 else (qkv_dil.pop(0), (0, 2, 4))
        op, lp = _attn_fwd(arr, cb[0], cb[1], cb[2], 2, tab_g, ks_g, cls_g, nk_g, name=f"attn_d{g}_fwd", qb=DIL_QB)
        arrs.append((arr, cb, tab_g, ks_g, cls_g, nk_g))
        o_g.append(res(op, dd))
        l_g.append(res(lp, dd))
    wts = dict(wts, **late_weights(o_a))
    o_b, merged, mo, x1, h2 = _post_attn_fwd(o_a, o_g, l_g, gates, x2, wts["w_pa"], wts["w_pb"], wts["w_o"], b_gate,
                                             gt1, g_norm2, sc2, sh2, name="post_attn_fwd")
    act, ff = _ffn_fwd(h2, wts["w_ffn_in"], name="ffn_fwd")

    dy, dffo, dff, dgt2, loss_v = _ffn_mid(act, ff, x1, tgt, wts["w_ffn_out"], gt2, name="ffn_mid")
    grads = {}
    g_ffn_out = _wgrad(act, dffo, name="wg_ffn_out")
    grads["w_ffn_out"] = g_ffn_out.reshape(N_CHIP, D_FF // N_CHIP, d)
    grads["w_ffn_in"] = _wgrad(h2, dff, name="wg_ffn_in", chips=N_CHIP)
    dx1, dmo, sums2 = _ffn_in_bwd(dff, x1, dy, mo, wts["w_ffn_in"], g_norm2, sc2, gt1, name="ffn_in_bwd")
    grads["w_o"] = _wgrad(merged, dmo, name="wg_o").reshape(N_CHIP, d // N_CHIP, d)
    pab = _post_attn_bwd(dmo, gates, o_a, o_g, l_g, wts["w_pa"], wts["w_pb"], wts["w_o"], b_gate, name="post_attn_bwd")
    dpa, dpb, dgates, do_a = pab[:4]
    do_g, dl_g, dbg = pab[4:7], pab[7:10], pab[10]
    g_pa = _wgrad(o_a, dpa, name="wg_pa")
    g_pb = _wgrad(o_b, dpb, name="wg_pb")
    grads["w_proj_a"] = g_pa.reshape(WA, N_CHIP, d // N_CHIP).transpose(1, 0, 2)
    grads["w_proj_b"] = g_pb.reshape(WB_OUT, N_CHIP, d // N_CHIP).transpose(1, 0, 2)
    order = early_grads(grads)
    dqs, dks, dvs = [], [], []
    for g, (_, dd) in enumerate(DIL_CONFIGS):
        arr, cb, tab_g, ks_g, cls_g, nk_g = arrs[g]
        plain = lambda t: t[0] if isinstance(t, tuple) else t
        dq, dk, dv = _attn_bwd(arr, cb[0], cb[1], cb[2], 2, tab_g, ks_g, cls_g, nk_g, do_g[g], plain(o_g[g]),
                               plain(l_g[g]), name=f"attn_d{g}_bwd", dlse=dl_g[g], qb=DIL_QB, order=order)
        dqs.append(res(dq, dd))
        dks.append(res(dk, dd))
        dvs.append(res(dv, dd))
    order = mid_grads(dv)
    dqa, dka, dva, bank = _attn_bwd(qkvn, 0, 4, 8, 4, None, na_ks, na_cls, NA_NK, do_a, o_a, lse_a,
                                    name="attn_a_bwd", na=na, order=order)
    dqkv, grad_x, dgains, sums1 = _pre_attn_bwd(qk_pre, [dqa, dka, dva] + dqs + dks + dvs, dgates, x2, dx1, cos_t, sa_t,
                                                sb_t, wts["w_qkv"], wts["w_gates"], gains, g_norm1, sc1,
                                                name="pre_attn_bwd")
    g_qkv = _wgrad(h1, dqkv, name="wg_qkv")
    g_gates = _wgrad(h1, dgates, name="wg_gates")
    nc, cut = _BIG_SHARD["w_in"][1], 3 * _BIG_SHARD["w_in"][1] - W_QKV
    grads["w_in"] = jnp.stack([g_qkv[:, :nc], g_qkv[:, nc:2 * nc],
                               jnp.concatenate([g_qkv[:, 2 * nc:], g_gates[:, :cut]], axis=1), g_gates[:, cut:]])

    bank = bank.reshape(NA_HEADS, NA_SLOTS, GRID_W, 2, GRID_W).transpose(0, 1, 3, 2, 4)
    bank = jnp.pad(bank.reshape(NA_HEADS, 2 * NA_SLOTS, GRID_W * GRID_W), ((0, 0), (0, _BANK_ROWS - 2 * NA_SLOTS), (0, 0)))
    g2 = _bank_reduce(bank, jnp.asarray(oh_col), name="rpb_reduce")[:, :2 * NA_SLOTS].reshape(NA_HEADS, NA_SLOTS, 2, LANES)
    g_rpb = g2[:, 3:3 + _RPB_RO, 0, :_RPB_CO] + g2[:, 2:2 + _RPB_RO, 1, :_RPB_CO]

    dmod = jnp.concatenate([sums1[0:1], sums1[1:2], sums2[3:4], sums2[0:1], sums2[1:2], dgt2], axis=1)
    small = dict(g_norm1=sums1[2:3], g_norm2=sums2[2:3], b_gate=dbg, g_qa=dgains[0:1, :HEAD_DIM],
                 g_ka=dgains[1:2, :HEAD_DIM], g_qb=dgains[2:3, :HEAD_DIM], g_kb=dgains[3:4, :HEAD_DIM], rpb=g_rpb)
    return loss_v, grad_x, grads, dmod, small


_SMALL = ("b_ada", "g_norm1", "g_norm2", "b_gate", "g_qa", "g_ka", "g_qb", "g_kb", "rpb")
_SMALL_N = {"b_ada": 6 * D_MODEL, "g_norm1": D_MODEL, "g_norm2": D_MODEL, "b_gate": 2 * D_MODEL, "g_qa": HEAD_DIM,
            "g_ka": HEAD_DIM, "g_qb": HEAD_DIM, "g_kb": HEAD_DIM, "rpb": NA_HEADS * _RPB_RO * _RPB_CO}


def _pack_small(parts):
    flat = [parts[n].reshape(1, _SMALL_N[n]) for n in _SMALL]
    used = sum(_SMALL_N.values())
    return jnp.concatenate(flat + [jnp.zeros((1, STATS_W - used), F32)], axis=1)


def _unpack_small(v, shapes):
    out, at = {}, 0
    for n in _SMALL:
        out[n] = v[:, at:at + _SMALL_N[n]].reshape(shapes[n])
        at += _SMALL_N[n]
    return out


def _join_cols(t):
    _, r, c = t.shape
    return t.transpose(1, 0, 2).reshape(r, N_CHIP * c)


def kernel(x, c, w_ada, b_ada, g_norm1, g_norm2, w_in, b_gate, g_qa, g_ka, g_qb, g_kb, rpb, w_proj_a, w_proj_b, w_o, w_ffn_in, w_ffn_out, loss_target, m_w_ada, m_b_ada, m_g_norm1, m_g_norm2, m_w_in, m_b_gate, m_g_qa, m_g_ka, m_g_qb, m_g_kb, m_rpb, m_w_proj_a, m_w_proj_b, m_w_o, m_w_ffn_in, m_w_ffn_out, v_w_ada, v_b_ada, v_g_norm1, v_g_norm2, v_w_in, v_b_gate, v_g_qa, v_g_ka, v_g_qb, v_g_kb, v_rpb, v_w_proj_a, v_w_proj_b, v_w_o, v_w_ffn_in, v_w_ffn_out):
    names = ("w_ada", "b_ada", "g_norm1", "g_norm2", "w_in", "b_gate", "g_qa", "g_ka", "g_qb", "g_kb", "rpb",
             "w_proj_a", "w_proj_b", "w_o", "w_ffn_in", "w_ffn_out")
    w = dict(zip(names, (w_ada, b_ada, g_norm1, g_norm2, w_in, b_gate, g_qa, g_ka, g_qb, g_kb, rpb, w_proj_a, w_proj_b,
                         w_o, w_ffn_in, w_ffn_out)))
    m = dict(zip(names, (m_w_ada, m_b_ada, m_g_norm1, m_g_norm2, m_w_in, m_b_gate, m_g_qa, m_g_ka, m_g_qb, m_g_kb, m_rpb,
                         m_w_proj_a, m_w_proj_b, m_w_o, m_w_ffn_in, m_w_ffn_out)))
    v = dict(zip(names, (v_w_ada, v_b_ada, v_g_norm1, v_g_norm2, v_w_in, v_b_gate, v_g_qa, v_g_ka, v_g_qb, v_g_kb, v_rpb,
                         v_w_proj_a, v_w_proj_b, v_w_o, v_w_ffn_in, v_w_ffn_out)))
    d = D_MODEL
    xi, yi, ci = _pos()
    chip = 2 * xi + yi
    me = 2 * chip + ci
    ada_cols = 6 * d // N_CHIP

    c_arr, jc_arr = ci.reshape(1).astype(jnp.int32), jnp.stack([chip, ci]).astype(jnp.int32)
    first, rest = _BIG[:1], _BIG[1:]

    c_all = _small_allgather(c.reshape(8, d // 8), name="ag_c").reshape(N_DEV, d)
    b_sh = lax.dynamic_slice(b_ada, (0, chip * ada_cols), (1, ada_cols))
    mod_part = _ada_fwd(c_all, w_ada[0], b_sh, name="ada_fwd")
    mod_all = _small_allgather(mod_part, name="ag_mod").reshape(N_CHIP, 2, 8, ada_cols)[:, 0]
    mod = lax.dynamic_index_in_dim(mod_all, me, axis=1, keepdims=False).reshape(1, 6 * d)

    halves = {n: (2, _BIG_SHARD[n][0] // 2, _BIG_SHARD[n][1]) for n in _BIG}
    shards = {n: w[n][0].astype(BF16).reshape(halves[n]) for n in _BIG}
    land = lambda n: lax.empty((N_CHIP,) + halves[n], BF16)
    ag1 = _split_start([shards[n] for n in first], [land(n) for n in first], _ag_plan(1), 4, mod, name="ag1_start")
    ag2 = _split_start([shards[n] for n in rest], [land(n) for n in rest], _ag_plan_direct(len(rest)), 7 * len(rest),
                       ag1[4], name="ag2_start")
    rpb_after = rpb[0] + ag2[4][0, 0]

    def first_weights(after):
        after = after[:1, :1] + ag2[4][:1, :1]
        full1 = _split_wait(ag1[0], ag1[1], ag1[2], ag1[3], _ag_plan(1), after, name="ag1_wait")
        p_in = _ag_pass(full1, name="ag1_pass")[0].reshape((N_CHIP,) + _BIG_SHARD["w_in"])
        cut = W_QKV - 2 * _BIG_SHARD["w_in"][1]
        return dict(w_qkv=jnp.concatenate([p_in[0], p_in[1], p_in[2][:, :cut]], axis=1),
                    w_gates=jnp.concatenate([p_in[2][:, cut:], p_in[3]], axis=1))

    def late_weights(after):
        full2 = _split_wait(ag2[0], ag2[1], ag2[2], ag2[3], _ag_plan_direct(len(rest)), after, name="ag2_wait")
        full ={n: fu.reshape((N_CHIP,) + _BIG_SHARD[n]) for n, fu in zip(rest, full2)}
        return dict(w_pa=_join_cols(full["w_proj_a"]), w_pb=_join_cols(full["w_proj_b"]), w_o=full["w_o"].reshape(d, d),
                    w_ffn_in=_join_cols(full["w_ffn_in"]), w_ffn_out=full["w_ffn_out"].reshape(D_FF, d))

    def sib_begin(group, grads, tag):
        gps = [grads[n].reshape((N_CHIP,) + halves[n]) for n in group]
        lands = [lax.empty((N_CHIP,) + halves[n][1:], F32) for n in group]
        return _split_start(gps, lands, _sib_plan(len(group)), N_CHIP * len(group), gps[0], name=f"rs_sib_{tag}_start",
                            alias_sources=False)

    def rs_begin(group, sib, after, tag):
        gps, ras = _split_wait(sib[0], sib[1], sib[2], sib[3], _sib_plan(len(group)), after,
                               name=f"rs_sib_{tag}_wait", with_sources=True, alias_sources=False)
        sbs = [_rs_add(gp, ra, c_arr, name=f"rs_add_{n}") for n, gp, ra in zip(group, gps, ras)]
        lands = [lax.empty((3,) + halves[n][1:], BF16) for n in group]
        st = _split_start(sbs, lands, _rs_plan(len(group)), 3 * len(group), sbs[0], name=f"rs_{tag}_start")
        return (gps, ras), st

    def rs_end(group, begun, after, tag):
        (gps, ras), st = begun
        rbs = _split_wait(st[0], st[1], st[2], st[3], _rs_plan(len(group)), after, name=f"rs_{tag}_wait")
        return [_rs_final(gp, ra, rb, jc_arr, name=f"rs_final_{n}") for n, gp, ra, rb in zip(group, gps, ras, rbs)]

    begun = {}

    def early_grads(grads):
        begun["sib_rest"] = sib_begin(rest, grads, "rest")
        return begun["sib_rest"][4]

    def mid_grads(after):
        begun["rest"] = rs_begin(rest, begun["sib_rest"], after, "rest")
        return begun["rest"][1][4]

    loss_v, grad_x, grads, dmod, small = _device_step(
        x[0], loss_target[0], mod, first_weights, late_weights, early_grads, mid_grads, g_norm1, g_norm2, b_gate, g_qa,
        g_ka, g_qb, g_kb, rpb_after)
    sib_first = sib_begin(first, grads, "first")

    g, delta, new_m, new_v = {}, {}, {}, {}

    def finish(group, ts, tag):
        others = _sibling_swap(ts, name=f"rs_pair_{tag}")
        for n, t, o in zip(group, ts, others):
            gg, dl, nm, nv = _adamw_halves(w[n][0], t, o, m[n][0], v[n][0], c_arr, name=f"adamw_{n}")
            g[n], delta[n], new_m[n], new_v[n] = gg[None], dl[None], nm[None], nv[None]

    finish(rest, rs_end(rest, begun["rest"], sib_first[4], "rest"), "rest")
    done_rest = sum(new_v[n][0, :1, :1] for n in rest)
    begun["first"] = rs_begin(first, sib_first, done_rest, "first")

    stats = _pack_small(dict(b_ada=dmod, **small)) + begun["first"][1][4][0, 0]
    stats = stats.at[:, STATS_W - 1].set(loss_v[0, 0])
    rows = _small_allgather(stats.reshape(8, STATS_W // 8), name="ag_stats").reshape(N_DEV, STATS_W)
    dmod_sh = lax.dynamic_slice(rows, (0, chip * ada_cols), (8, ada_cols))
    g_ada = _ada_bwd(c_all.T, dmod_sh, name="ada_bwd")
    tot = _row_sum(rows, name="stats_sum")
    g_small = _unpack_small(tot, {n: w[n].shape for n in _SMALL})

    finish(first, rs_end(first, begun["first"], tot, "first"), "first")

    dl, nm, nv = _adamw(w_ada[0], g_ada, m_w_ada[0], v_w_ada[0], name="adamw_w_ada")
    g["w_ada"], delta["w_ada"], new_m["w_ada"], new_v["w_ada"] = g_ada[None], dl[None], nm[None], nv[None]
    shapes = {n: w[n].shape for n in _SMALL}
    dl, nm, nv = _adamw(_pack_small({n: w[n] for n in _SMALL}), tot, _pack_small({n: m[n] for n in _SMALL}),
                        _pack_small({n: v[n] for n in _SMALL}), name="adamw_small")
    delta.update(_unpack_small(dl, shapes))
    new_m.update(_unpack_small(nm, shapes))
    new_v.update(_unpack_small(nv, shapes))
    g.update(g_small)

    loss = tot[0, STATS_W - 1]
    return (loss, grad_x[None], *[g[n] for n in names], *[delta[n] for n in names], *[new_m[n] for n in names],
            *[new_v[n] for n in names])
```

```python
import numpy as np

import jax
import jax.numpy as jnp
from jax import lax
from jax.experimental import pallas as pl
from jax.experimental.pallas import tpu as pltpu

F32 = jnp.float32
BF16 = jnp.bfloat16

D_MODEL = 1024
SEQ = 8192
HEAD_DIM = 64
GRID_W = 64
ROWS = SEQ // GRID_W
NA_HEADS = 8
NA_KH = 8
NA_KW = 16
DIL_CONFIGS = ((128, 1), (512, 4), (2048, 16))
ROT_DIM = 16
ROPE_THETA = 500000.0
D_FF = 2816
EPS = 1e-6
NEG = -1e30
WA = 512
WB = 768
WB_OUT = 256
W_QKV = 3 * WA + 3 * WB
W_QK = 2 * WA + 2 * WB
W_GATES = 2 * D_MODEL
SCALE = HEAD_DIM ** -0.5

ADAM_LR = 0.001
ADAM_B1 = 0.9
ADAM_B2 = 0.999
ADAM_EPS = 1e-08
ADAM_WD = 0.01
ADAM_STEP = 10

LANES = 128
ROW_TILE = 256
ROW_TILES = {"ffn_fwd": 512, "post_attn_fwd": 512, "post_attn_bwd": 512,
             "ffn_in_bwd": 512}
WGRAD_TILES = {"wg_ffn_out": (1408, 1024, 2048), "wg_ffn_in": (512, 2816, 2048), "wg_o": (1024, 1024, 2048),
               "wg_pa": (512, 1024, 2048), "wg_pb": (256, 1024, 2048), "wg_qkv": (1024, 1920, 2048),
               "wg_gates": (1024, 1024, 2048)}
Q_BLOCK = 256
NA_QROWS = Q_BLOCK // GRID_W
NA_KROWS = NA_QROWS + NA_KH - 1
NA_NK = NA_KROWS * GRID_W
NA_PAIRS = (NA_KROWS + 1) // 2
NA_W = NA_PAIRS * LANES
NA_RO_NONE = 15
NA_SLOTS = 21
RP_LANE0 = GRID_W - NA_KW
DIL_HALF = 64
DIL_QB = 512
N_QBLK = SEQ // Q_BLOCK

N_DEV = 8
N_CHIP = 4
FF_CHIP = 2 * D_FF // N_CHIP
STATS_W = 14336


def _pcall(body, *, name, **kw):
    return pl.pallas_call(body, name=name, **kw)


_NT = (((1,), (1,)), ((), ()))
_TN = (((0,), (0,)), ((), ()))
_ARB = pltpu.CompilerParams(dimension_semantics=("arbitrary",))
_PAR = pltpu.CompilerParams(dimension_semantics=("parallel",))


def _dot(a, b):
    return jnp.dot(a, b, preferred_element_type=F32)


def _dot_nt(a, b):
    return lax.dot_general(a, b, _NT, preferred_element_type=F32)


def _wgrad(a, b, *, name, chips=None):
    s, ma = a.shape
    nb = b.shape[1]
    tm, tn, tk = WGRAD_TILES[name]
    nk = s // tk
    nc = nb // chips if chips else tn
    cpb = tn // nc

    def body(a_ref, b_ref, o_ref, acc):
        k = pl.program_id(2)
        r = lax.dot_general(a_ref[...].astype(BF16), b_ref[...].astype(BF16), _TN, preferred_element_type=F32)

        @pl.when(k == 0)
        def _():
            acc[...] = r

        @pl.when(k > 0)
        def _():
            acc[...] += r

        @pl.when(k == nk - 1)
        def _():
            if chips:
                for q in range(cpb):
                    o_ref[q] = acc[:, q * nc:(q + 1) * nc]
            else:
                o_ref[...] = acc[...]

    if chips:
        o_spec = pl.BlockSpec((cpb, tm, nc), lambda i, j, k: (j, i, 0))
        out_shape = jax.ShapeDtypeStruct((chips, ma, nc), F32)
    else:
        o_spec = pl.BlockSpec((tm, tn), lambda i, j, k: (i, j))
        out_shape = jax.ShapeDtypeStruct((ma, nb), F32)
    return _pcall(
        body, name=name, grid=(ma // tm, nb // tn, nk),
        in_specs=[pl.BlockSpec((tk, tm), lambda i, j, k: (k, i)), pl.BlockSpec((tk, tn), lambda i, j, k: (k, j))],
        out_specs=o_spec, out_shape=out_shape, scratch_shapes=[pltpu.VMEM((tm, tn), F32)],
        compiler_params=pltpu.CompilerParams(dimension_semantics=("parallel", "parallel", "arbitrary")),
    )(a, b)


def _row_call(body, *, name, row_ins, res_ins, row_outs, acc_outs=(), scratch=()):
    row_ins = [a if isinstance(a, tuple) else (a, 1) for a in row_ins]
    row_outs = [o if len(o) == 3 else (*o, 1) for o in row_outs]
    s = row_ins[0][0].shape[0]
    tile = ROW_TILES.get(name, ROW_TILE)
    n = s // tile
    nri, nre, nro, nao = len(row_ins), len(res_ins), len(row_outs), len(acc_outs)

    def whole(shape):
        nd = len(shape)
        return pl.BlockSpec(tuple(shape), lambda i: (0,) * nd, pipeline_mode=pl.Buffered(1))

    def whole_out(shape):
        nd = len(shape)
        return pl.BlockSpec(tuple(shape), lambda i: (0,) * nd)

    def rows(w, d):
        if d == 1:
            return pl.BlockSpec((tile, w), lambda i: (i, 0))
        return pl.BlockSpec((d, tile // d, w), lambda i: (0, i, 0))

    in_specs = [rows(a.shape[1], d) for a, d in row_ins]
    in_specs += [whole(a.shape) for a in res_ins]
    out_specs = [rows(w, d) for w, _, d in row_outs]
    out_specs += [whole_out(shp) for shp, _ in acc_outs]
    out_shape = [jax.ShapeDtypeStruct((s, w) if d == 1 else (d, s // d, w), dt) for w, dt, d in row_outs]
    out_shape += [jax.ShapeDtypeStruct(tuple(shp), dt) for shp, dt in acc_outs]

    def wrapped(*refs):
        at = [0, nri, nri + nre, nri + nre + nro, nri + nre + nro + nao]
        body(pl.program_id(0), n, refs[at[0]:at[1]], refs[at[1]:at[2]], refs[at[2]:at[3]], refs[at[3]:at[4]],
             refs[at[4]:])

    args = [a if d == 1 else a.reshape(d, s // d, a.shape[1]) for a, d in row_ins]
    outs = _pcall(wrapped, name=name, grid=(n,), in_specs=in_specs, out_specs=out_specs, out_shape=out_shape,
                  scratch_shapes=list(scratch), compiler_params=_ARB)(*args, *res_ins)
    return [o.reshape(s, o.shape[-1]) if k < nro and row_outs[k][2] != 1 else o for k, o in enumerate(outs)]


def _stage_shape(name):
    return pltpu.VMEM((4, ROW_TILES.get(name, ROW_TILE), LANES), F32)


def _from_residue(ref, col, stage, slot):
    d, n = ref.shape[0], ref.shape[1]
    for r in range(d):
        stage.at[slot][pl.ds(r, n, stride=d), :] = ref[r, :, col:col + LANES].astype(F32)
    return stage[slot]


def _natural(ref, stage, slot0):
    if len(ref.shape) == 2:
        return ref[...]
    return jnp.concatenate([_from_residue(ref, c * LANES, stage, (slot0 + c) % 4)
                            for c in range(ref.shape[2] // LANES)], axis=1)


def _to_residue(val, ref, col, stage, slot):
    d, n = ref.shape[0], ref.shape[1]
    stage[slot] = val
    for r in range(d):
        ref[r, :, col:col + LANES] = stage.at[slot][pl.ds(r, n, stride=d), :].astype(ref.dtype)


def _fold8(t):
    r, w = t.shape
    return jnp.sum(t.reshape(r // 8, 8, w), axis=0)


def _sigmoid(t):
    return 0.5 * (jnp.tanh(0.5 * t) + 1.0)


def _head_lanes():
    return lax.broadcasted_iota(jnp.int32, (1, LANES), 1) < HEAD_DIM


def _head_mean(t, lo):
    s_lo = jnp.sum(jnp.where(lo, t, 0.0), axis=1, keepdims=True)
    s_hi = jnp.sum(jnp.where(lo, 0.0, t), axis=1, keepdims=True)
    return jnp.where(lo, s_lo, s_hi) * (1.0 / HEAD_DIM)


def _rms_mod(xv, g, sc, sh):
    rstd = lax.rsqrt(jnp.mean(xv * xv, axis=1, keepdims=True) + EPS)
    return (xv * rstd * g) * (1.0 + sc) + sh


def _rms_mod_bwd(xv, dh, g, sc):
    rstd = lax.rsqrt(jnp.mean(xv * xv, axis=1, keepdims=True) + EPS)
    xhat = xv * rstd
    dn = dh * (1.0 + sc)
    dxhat = dn * g
    dx = rstd * (dxhat - xhat * jnp.mean(dxhat * xhat, axis=1, keepdims=True))
    return dx, dh, dh * (xhat * g), dn * xhat


def _mix_weights(ls):
    m = jnp.maximum(jnp.maximum(ls[0], ls[1]), ls[2])
    es = [jnp.exp(t - m) for t in ls]
    den = es[0] + es[1] + es[2]
    return [e / den for e in es]


def _rope_tables():
    half = ROT_DIM // 2
    inv_freq = ROPE_THETA ** (-(jnp.arange(half, dtype=F32) * 2.0) / ROT_DIM)
    lane = np.arange(LANES) % HEAD_DIM
    ang = jnp.arange(SEQ).astype(F32)[:, None] * jnp.tile(inv_freq, LANES // half)[None, :]
    cos, sin = jnp.cos(ang), jnp.sin(ang)
    first, second = jnp.asarray(lane < half)[None, :], jnp.asarray((lane >= half) & (lane < ROT_DIM))[None, :]
    cos_t = jnp.where(first | second, cos, 1.0)
    return cos_t, jnp.where(second, sin, 0.0), jnp.where(first, -sin, 0.0)


_SECTIONS = ((0, WA, 0, False), (WA, 2 * WA, 1, False), (2 * WA, 3 * WA, -1, False),
             (3 * WA, 3 * WA + WB, 2, True), (3 * WA + WB, 3 * WA + 2 * WB, 3, True), (3 * WA + 2 * WB, W_QKV, -1, False))


def _pre_attn_fwd(x, cos_t, sa_t, sb_t, g1, sc1, sh1, w_qkv, w_gates, gains, *, name):
    half = ROT_DIM // 2
    dilated = [(g, dd) for g, (_, dd) in enumerate(DIL_CONFIGS) if dd > 1]

    def body(i, n, rin, res, rout, aout, scr):
        x_ref, cos_ref, sa_ref, sb_ref = rin
        g_ref, sc_ref, sh_ref, wq_ref, wg_ref, gains_ref = res
        h1_ref, qkvn_ref, pre_ref, gates_ref = rout[:4]
        group_ref = {g: rout[4 + k] for k, (g, _) in enumerate(dilated)}
        (stage,) = scr
        staged = 0
        hb = _rms_mod(x_ref[...], g_ref[...], sc_ref[...], sh_ref[...]).astype(BF16)
        h1_ref[...] = hb
        gates_ref[...] = _dot(hb, wg_ref[...]).astype(BF16)
        lo = _head_lanes()
        cosv, sav, sbv = cos_ref[...], sa_ref[...], sb_ref[...]
        pre_at = 0
        for si, (c0, c1, kind, rot) in enumerate(_SECTIONS):
            sec = _dot(hb, wq_ref[:, c0:c1])
            for ch in range((c1 - c0) // LANES):
                t = sec[:, ch * LANES:(ch + 1) * LANES]
                if kind >= 0:
                    pre_ref[:, pre_at:pre_at + LANES] = t.astype(BF16)
                    pre_at += LANES
                    t = t * lax.rsqrt(_head_mean(t * t, lo) + EPS) * gains_ref[kind:kind + 1, :]
                    if rot:
                        t = t * cosv + pltpu.roll(t, half, 1) * sav + pltpu.roll(t, LANES - half, 1) * sbv
                qkvn_ref[:, c0 + ch * LANES:c0 + (ch + 1) * LANES] = t.astype(BF16)
                group = ch * LANES // WB_OUT if si >= 3 else 0
                if group in group_ref:
                    col = (si - 3) * WB_OUT + ch * LANES % WB_OUT
                    _to_residue(t, group_ref[group], col, stage, staged % 4)
                    staged += 1

    return _row_call(body, name=name, row_ins=[x, cos_t, sa_t, sb_t], res_ins=[g1, sc1, sh1, w_qkv, w_gates, gains],
                     row_outs=[(D_MODEL, BF16), (W_QKV, BF16), (W_QK, BF16), (W_GATES, BF16)]
                     + [(3 * WB_OUT, BF16, dd) for _, dd in dilated], scratch=[_stage_shape(name)])


def _pre_attn_bwd(qk_pre, d_parts, dgates, x, dx1, cos_t, sa_t, sb_t, w_qkv, w_gates, gains, g1, sc1, *, name):
    half = ROT_DIM // 2
    nparts = len(d_parts)
    where = []
    residue = [isinstance(part, tuple) for part in d_parts]
    for pi, part in enumerate(d_parts):
        width = (part[0] if residue[pi] else part).shape[1]
        where += [(pi, cj) for cj in range(width // LANES)]
    assert len(where) == W_QKV // LANES

    def body(i, n, rin, res, rout, aout, scr):
        pre_ref, d_refs = rin[0], rin[1:1 + nparts]
        dgates_ref, x_ref, dx1_ref, cos_ref, sa_ref, sb_ref = rin[1 + nparts:]
        wq_ref, wg_ref, gains_ref, g_ref, sc_ref = res
        dqkv_ref, gx_ref = rout
        dgains_ref, sums_ref = aout
        accg, accs, stage = scr
        staged = 0

        @pl.when(i == 0)
        def _():
            accg[...] = jnp.zeros_like(accg)
            accs[...] = jnp.zeros_like(accs)

        lo = _head_lanes()
        cosv, sav, sbv = cos_ref[...], sa_ref[...], sb_ref[...]
        dh = _dot_nt(dgates_ref[...], wg_ref[...])
        pre_at = 0
        for c0, c1, kind, rot in _SECTIONS:
            for ch in range((c1 - c0) // LANES):
                pi, cj = where[c0 // LANES + ch]
                if residue[pi]:
                    dt = _from_residue(d_refs[pi], cj * LANES, stage, staged % 4)
                    staged += 1
                else:
                    dt = d_refs[pi][:, cj * LANES:(cj + 1) * LANES]
                if kind >= 0:
                    if rot:
                        dt = dt * cosv + pltpu.roll(dt * sav, LANES - half, 1) + pltpu.roll(dt * sbv, half, 1)
                    t = pre_ref[:, pre_at:pre_at + LANES].astype(F32)
                    pre_at += LANES
                    rstd = lax.rsqrt(_head_mean(t * t, lo) + EPS)
                    xhat = t * rstd
                    accg[kind] += _fold8(dt * xhat)
                    dxhat = dt * gains_ref[kind:kind + 1, :]
                    dt = rstd * (dxhat - xhat * _head_mean(dxhat * xhat, lo))
                dqkv_ref[:, c0 + ch * LANES:c0 + (ch + 1) * LANES] = dt.astype(BF16)
            dh = dh + _dot_nt(dqkv_ref[:, c0:c1], wq_ref[:, c0:c1])
        dx, t_sh, t_sc, t_g = _rms_mod_bwd(x_ref[...], dh, g_ref[...], sc_ref[...])
        gx_ref[...] = dx1_ref[...] + dx
        accs[0] += _fold8(t_sh)
        accs[1] += _fold8(t_sc)
        accs[2] += _fold8(t_g)

        @pl.when(i == n - 1)
        def _():
            t = jnp.sum(accg[...], axis=1)
            dgains_ref[...] = t + pltpu.roll(t, HEAD_DIM, 1)
            sums_ref[...] = jnp.sum(accs[...], axis=1)

    return _row_call(
        body, name=name, row_ins=[qk_pre, *d_parts, dgates, x, dx1, cos_t, sa_t, sb_t],
        res_ins=[w_qkv, w_gates, gains, g1, sc1], row_outs=[(W_QKV, BF16), (D_MODEL, F32)],
        acc_outs=[((4, LANES), F32), ((3, D_MODEL), F32)],
        scratch=[pltpu.VMEM((4, 8, LANES), F32), pltpu.VMEM((3, 8, D_MODEL), F32), _stage_shape(name)])


def _post_attn_fwd(o_a, o_g, l_g, gates, x, w_pa, w_pb, w_o, b_gate, gt1, g2, sc2, sh2, *, name):
    d = D_MODEL

    def body(i, n, rin, res, rout, aout, scr):
        oa_ref, o0, o1, o2, l0, l1, l2, gates_ref, x_ref = rin
        wpa_ref, wpb_ref, wo_ref, b_ref, gt_ref, g_ref, sc_ref, sh_ref = res
        ob_ref, merged_ref, mo_ref, x1_ref, h2_ref = rout
        (stage,) = scr
        ogs = [_natural(r, stage, 0) for r in (o0, o1, o2)]
        ws = _mix_weights([_natural(r, stage, 2) for r in (l0, l1, l2)])
        obb = (ws[0] * ogs[0] + ws[1] * ogs[1] + ws[2] * ogs[2]).astype(BF16)
        ob_ref[...] = obb
        pa = _dot(oa_ref[...].astype(BF16), wpa_ref[...])
        pb = _dot(obb, wpb_ref[...])
        ga = _sigmoid(gates_ref[:, :d].astype(F32) + b_ref[:, :d])
        gb = _sigmoid(gates_ref[:, d:].astype(F32) + b_ref[:, d:])
        merged = (ga * pa + gb * pb).astype(BF16)
        merged_ref[...] = merged
        mo = _dot(merged, wo_ref[...])
        mo_ref[...] = mo.astype(BF16)
        x1 = x_ref[...] + gt_ref[...] * mo
        x1_ref[...] = x1
        h2_ref[...] = _rms_mod(x1, g_ref[...], sc_ref[...], sh_ref[...]).astype(BF16)

    return _row_call(body, name=name, row_ins=[o_a, *o_g, *l_g, gates, x],
                     res_ins=[w_pa, w_pb, w_o, b_gate, gt1, g2, sc2, sh2],
                     row_outs=[(WB_OUT, BF16), (d, BF16), (d, BF16), (d, F32), (d, BF16)], scratch=[_stage_shape(name)])


def _ffn_fwd(h2, w_ffn_in, *, name):
    def body(i, n, rin, res, rout, aout, scr):
        (h_ref,), (w_ref,), (act_ref, ff_ref) = rin, res, rout
        hv = h_ref[...]
        for q in range(2):
            a = _dot(hv, w_ref[:, q * FF_CHIP:(q + 1) * FF_CHIP])
            up = _dot(hv, w_ref[:, D_FF + q * FF_CHIP:D_FF + (q + 1) * FF_CHIP])
            sl = slice(q * FF_CHIP, (q + 1) * FF_CHIP)
            act_ref[:, sl] = (a * _sigmoid(a) * up).astype(BF16)
            ff_ref[:, sl] = a.astype(BF16)
            ff_ref[:, D_FF + q * FF_CHIP:D_FF + (q + 1) * FF_CHIP] = up.astype(BF16)

    return _row_call(body, name=name, row_ins=[h2], res_ins=[w_ffn_in], row_outs=[(D_FF, BF16), (2 * D_FF, BF16)])


def _ffn_mid(act, ff, x1, tgt, w_ffn_out, gt2, *, name):
    d = D_MODEL

    def body(i, n, rin, res, rout, aout, scr):
        act_ref, ff_ref, x1_ref, tgt_ref = rin
        wo_ref, gt_ref = res
        dy_ref, dffo_ref, dff_ref = rout
        dgt_ref, loss_ref = aout
        (acc,) = scr

        @pl.when(i == 0)
        def _():
            acc[...] = jnp.zeros_like(acc)

        ffo = _dot(act_ref[...], wo_ref[...])
        gtv = gt_ref[...]
        e = x1_ref[...] + gtv * ffo - tgt_ref[...]
        dy = e * (1.0 / d)
        dy_ref[...] = dy
        dffo = (gtv * dy).astype(BF16)
        dffo_ref[...] = dffo
        acc[0] += _fold8(dy * ffo)
        acc[1] += _fold8(e * e)
        for q in range(2):
            sl = slice(q * FF_CHIP, (q + 1) * FF_CHIP)
            su = slice(D_FF + q * FF_CHIP, D_FF + (q + 1) * FF_CHIP)
            dact = _dot_nt(dffo, wo_ref[sl, :])
            a = ff_ref[:, sl].astype(F32)
            up = ff_ref[:, su].astype(F32)
            sg = _sigmoid(a)
            dff_ref[:, sl] = (dact * up * (sg * (1.0 + a * (1.0 - sg)))).astype(BF16)
            dff_ref[:, su] = (dact * (a * sg)).astype(BF16)

        @pl.when(i == n - 1)
        def _():
            dgt_ref[...] = jnp.sum(acc[0], axis=0, keepdims=True)
            tot = jnp.sum(jnp.sum(acc[1], axis=0, keepdims=True), axis=1, keepdims=True)
            loss_ref[...] = jnp.broadcast_to(tot * (0.5 / d), (1, LANES))

    return _row_call(body, name=name, row_ins=[act, ff, x1, tgt], res_ins=[w_ffn_out, gt2],
                     row_outs=[(d, F32), (d, BF16), (2 * D_FF, BF16)], acc_outs=[((1, d), F32), ((1, LANES), F32)],
                     scratch=[pltpu.VMEM((2, 8, d), F32)])


def _ffn_in_bwd(dff, x1, dy, mo, w_ffn_in, g2, sc2, gt1, *, name):
    d = D_MODEL

    def body(i, n, rin, res, rout, aout, scr):
        dff_ref, x1_ref, dy_ref, mo_ref = rin
        w_ref, g_ref, sc_ref, gt_ref = res
        dx1_ref, dmo_ref = rout
        (sums_ref,) = aout
        (acc,) = scr

        @pl.when(i == 0)
        def _():
            acc[...] = jnp.zeros_like(acc)

        dh = _dot_nt(dff_ref[...], w_ref[...])
        dx, t_sh, t_sc, t_g = _rms_mod_bwd(x1_ref[...], dh, g_ref[...], sc_ref[...])
        dx1 = dy_ref[...] + dx
        dx1_ref[...] = dx1
        dmo_ref[...] = (gt_ref[...] * dx1).astype(BF16)
        acc[0] += _fold8(t_sh)
        acc[1] += _fold8(t_sc)
        acc[2] += _fold8(t_g)
        acc[3] += _fold8(dx1 * mo_ref[...].astype(F32))

        @pl.when(i == n - 1)
        def _():
            sums_ref[...] = jnp.sum(acc[...], axis=1)

    return _row_call(body, name=name, row_ins=[dff, x1, dy, mo], res_ins=[w_ffn_in, g2, sc2, gt1],
                     row_outs=[(d, F32), (d, BF16)], acc_outs=[((4, d), F32)], scratch=[pltpu.VMEM((4, 8, d), F32)])


def _post_attn_bwd(dmo, gates, o_a, o_g, l_g, w_pa, w_pb, w_o, b_gate, *, name):
    d = D_MODEL

    def body(i, n, rin, res, rout, aout, scr):
        dmo_ref, gates_ref, oa_ref, o0, o1, o2, l0, l1, l2 = rin
        wpa_ref, wpb_ref, wo_ref, b_ref = res
        dpa_ref, dpb_ref, dgates_ref, doa_ref = rout[:4]
        do_refs, dl_refs = rout[4:7], rout[7:10]
        (dbg_ref,) = aout
        acc, stage = scr

        @pl.when(i == 0)
        def _():
            acc[...] = jnp.zeros_like(acc)

        ogs = [_natural(r, stage, 0) for r in (o0, o1, o2)]
        ws = _mix_weights([_natural(r, stage, 2) for r in (l0, l1, l2)])
        obb = (ws[0] * ogs[0] + ws[1] * ogs[1] + ws[2] * ogs[2]).astype(BF16)
        pa = _dot(oa_ref[...].astype(BF16), wpa_ref[...])
        pb = _dot(obb, wpb_ref[...])
        ga = _sigmoid(gates_ref[:, :d].astype(F32) + b_ref[:, :d])
        gb = _sigmoid(gates_ref[:, d:].astype(F32) + b_ref[:, d:])
        dm = _dot_nt(dmo_ref[...], wo_ref[...])
        dpa = (dm * ga).astype(BF16)
        dpb = (dm * gb).astype(BF16)
        dpa_ref[...] = dpa
        dpb_ref[...] = dpb
        dga = dm * pa * ga * (1.0 - ga)
        dgb = dm * pb * gb * (1.0 - gb)
        dgates_ref[:, :d] = dga.astype(BF16)
        dgates_ref[:, d:] = dgb.astype(BF16)
        acc[:, :d] += _fold8(dga)
        acc[:, d:] += _fold8(dgb)
        doa_ref[...] = _dot_nt(dpa, wpa_ref[...])
        dob = _dot_nt(dpb, wpb_ref[...])
        lo = _head_lanes()
        for ch in range(WB_OUT // LANES):
            sl = slice(ch * LANES, (ch + 1) * LANES)
            dv = dob[:, sl]
            wc = [w[:, sl] for w in ws]
            ts = [_head_mean(dv * og[:, sl], lo) * float(HEAD_DIM) for og in ogs]
            tbar = wc[0] * ts[0] + wc[1] * ts[1] + wc[2] * ts[2]
            for g in range(3):
                for k, (ref, val) in enumerate(((do_refs[g], wc[g] * dv), (dl_refs[g], wc[g] * (ts[g] - tbar)))):
                    if len(ref.shape) == 2:
                        ref[:, sl] = val
                    else:
                        _to_residue(val, ref, ch * LANES, stage, (2 * g + k) % 4)

        @pl.when(i == n - 1)
        def _():
            dbg_ref[...] = jnp.sum(acc[...], axis=0, keepdims=True)

    return _row_call(body, name=name, row_ins=[dmo, gates, o_a, *o_g, *l_g], res_ins=[w_pa, w_pb, w_o, b_gate],
                     row_outs=[(d, BF16), (d, BF16), (2 * d, BF16), (WA, F32)]
                     + 2 * [(WB_OUT, F32, dd) for _, dd in DIL_CONFIGS],
                     acc_outs=[((1, 2 * d), F32)], scratch=[pltpu.VMEM((8, 2 * d), F32), _stage_shape(name)])


def _na_class_tables():
    ro = np.full((3, NA_QROWS, 2 * NA_PAIRS), NA_RO_NONE, np.int64)
    slot = np.zeros((3, NA_QROWS, NA_PAIRS), np.int64)
    for t in range(3):
        for a in range(NA_QROWS):
            qr = _NA_CLASS_R0[t] + a
            rs = min(max(qr - NA_KH // 2, 0), ROWS - NA_KH)
            for b in range(NA_KROWS):
                kr = _NA_CLASS_K0[t] + b
                if rs <= kr < rs + NA_KH:
                    ro[t, a, b] = kr - qr + (NA_KH - 1)
            for j in range(NA_PAIRS):
                slot[t, a, j] = 2 * j - a + (_NA_CLASS_K0[t] - _NA_CLASS_R0[t] + NA_KH - 1) + (NA_QROWS - 1)
    assert slot.min() >= 0 and slot.max() < NA_SLOTS
    return ro, slot


def _na_build_bias(i, cls_ref, rp_ref, cm_ref, bias_scr):
    ro, _ = _na_class_tables()
    lo = _head_lanes()
    first = jnp.logical_or(i == 0, cls_ref[i] != cls_ref[jnp.maximum(i - 1, 0)])
    for t in range(3):
        @pl.when(jnp.logical_and(first, cls_ref[i] == t))
        def _():
            for hh in range(2):
                for a in range(NA_QROWS):
                    for j in range(NA_PAIRS):
                        r0, r1 = int(ro[t, a, 2 * j]), int(ro[t, a, 2 * j + 1])
                        x0 = jnp.broadcast_to(rp_ref[hh, r0:r0 + 1, :], (GRID_W, LANES))
                        x1 = jnp.broadcast_to(rp_ref[hh, r1:r1 + 1, :], (GRID_W, LANES))
                        blk = jnp.where(lo, pltpu.roll(x0, GRID_W + 1, 1, stride=1, stride_axis=0),
                                        pltpu.roll(x1, 1, 1, stride=1, stride_axis=0))
                        bias_scr[hh, a * GRID_W:(a + 1) * GRID_W, j * LANES:(j + 1) * LANES] = blk + cm_ref[...]
    return first


def _attn_fwd(qkv, qc0, kc0, vc0, npairs, table, kstart, cls, nk, *, name, na=None, qb=Q_BLOCK):
    s = qkv.shape[0]

    def body(ks_ref, cls_ref, q_ref, k_ref, v_ref, b_ref, *rest):
        if na:
            cm_ref, o_ref, lse_ref, bias_scr = rest
        else:
            o_ref, lse_ref = rest
        i = pl.program_id(1)
        if na:
            _na_build_bias(i, cls_ref, b_ref, cm_ref, bias_scr)
        ks = pl.multiple_of(ks_ref[i], 64)
        q2 = q_ref[...]
        k2 = k_ref[pl.ds(ks, nk), :]
        v2 = v_ref[pl.ds(ks, nk), :]
        lo = _head_lanes()
        outs, lses = [], []
        for h in range(2):
            qm = jnp.where(lo if h == 0 else jnp.logical_not(lo), q2, jnp.zeros_like(q2))
            sc = _dot_nt(qm, k2) * SCALE + (bias_scr[h, :, :nk] if na else b_ref[0, 0])
            m = jnp.max(sc, axis=1, keepdims=True)
            p = jnp.exp(sc - m)
            l = jnp.sum(p, axis=1, keepdims=True)
            pv = _dot(p.astype(BF16), v2)
            outs.append(pv / l)
            lses.append(m + jnp.log(l))
        o_ref[...] = jnp.where(lo, outs[0], outs[1])
        lse_ref[...] = jnp.where(lo, lses[0], lses[1])

    w = npairs * LANES
    in_specs = [
        pl.BlockSpec((qb, LANES), lambda p, i, ks, cl: (i, qc0 + p)),
        pl.BlockSpec((s, LANES), lambda p, i, ks, cl: (0, kc0 + p)),
        pl.BlockSpec((s, LANES), lambda p, i, ks, cl: (0, vc0 + p)),
    ]
    if na:
        in_specs += _na_bias_specs()
        args, scratch = (kstart, cls, qkv, qkv, qkv, *na), [pltpu.VMEM((2, Q_BLOCK, NA_W), F32)]
    else:
        in_specs.append(pl.BlockSpec((1, 1, qb, nk), lambda p, i, ks, cl: (cl[i], 0, 0, 0)))
        args, scratch = (kstart, cls, qkv, qkv, qkv, table), []
    grid_spec = pltpu.PrefetchScalarGridSpec(
        num_scalar_prefetch=2, grid=(npairs, s // qb), in_specs=in_specs,
        out_specs=[pl.BlockSpec((qb, LANES), lambda p, i, ks, cl: (i, p)),
                   pl.BlockSpec((qb, LANES), lambda p, i, ks, cl: (i, p))],
        scratch_shapes=scratch,
    )
    return _pcall(body, name=name, grid_spec=grid_spec,
                  out_shape=[jax.ShapeDtypeStruct((s, w), F32), jax.ShapeDtypeStruct((s, w), F32)],
                  compiler_params=pltpu.CompilerParams(dimension_semantics=("parallel", "arbitrary")),
                  )(*args)


def _na_bias_specs():
    return [pl.BlockSpec((2, 16, LANES), lambda p, i, ks, cl: (p, 0, 0)),
            pl.BlockSpec((GRID_W, LANES), lambda p, i, ks, cl: (0, 0))]


def _attn_bwd(qkv, qc0, kc0, vc0, npairs, table, kstart, cls, nk, do, o, lse, *, name, dlse=None, na=None,
              qb=Q_BLOCK, order=None):
    s = qkv.shape[0]
    has_dlse = dlse is not None
    _, slot = _na_class_tables()

    def body(ks_ref, cls_ref, q_ref, k_ref, v_ref, b_ref, *rest):
        if na:
            cm_ref, rest = rest[0], rest[1:]
        do_ref, o_ref, lse_ref, rest = rest[0], rest[1], rest[2], rest[3:]
        if has_dlse:
            dlse_ref, rest = rest[0], rest[1:]
        if order is not None:
            rest = rest[1:]
        dq_ref, dk_ref, dv_ref = rest[0], rest[1], rest[2]
        if na:
            grp_ref, bias_scr, dbias_scr, bank_scr = rest[3:]
        i = pl.program_id(1)
        if na:
            first = _na_build_bias(i, cls_ref, b_ref, cm_ref, bias_scr)

        @pl.when(i == 0)
        def _():
            dk_ref[...] = jnp.zeros_like(dk_ref)
            dv_ref[...] = jnp.zeros_like(dv_ref)
            if na:
                dbias_scr[...] = jnp.zeros_like(dbias_scr)
                bank_scr[...] = jnp.zeros_like(bank_scr)

        ks = pl.multiple_of(ks_ref[i], 64)
        q2 = q_ref[...]
        k2 = k_ref[pl.ds(ks, nk), :]
        v2 = v_ref[pl.ds(ks, nk), :]
        do2 = do_ref[...]
        lse2 = lse_ref[...]
        doo = do2 * o_ref[...]
        do2b = do2.astype(BF16)
        lo = _head_lanes()
        lane = lax.broadcasted_iota(jnp.int32, (1, LANES), 1)
        dqs, dks, dvs = [], [], []
        for h in range(2):
            mh = lo if h == 0 else jnp.logical_not(lo)
            qm = jnp.where(mh, q2, jnp.zeros_like(q2))
            sc = _dot_nt(qm, k2) * SCALE + (bias_scr[h, :, :nk] if na else b_ref[0, 0])
            lse_h = jnp.max(jnp.where(mh, lse2, NEG), axis=1, keepdims=True)
            p = jnp.exp(sc - lse_h)
            delta = jnp.sum(jnp.where(mh, doo, 0.0), axis=1, keepdims=True)
            dom = jnp.where(mh, do2b, jnp.zeros_like(do2b))
            dp = _dot_nt(dom, v2)
            t = dp - delta
            if has_dlse:
                t = t + jnp.sum(jnp.where(lane == h * HEAD_DIM, dlse_ref[...], 0.0), axis=1, keepdims=True)
            ds = p * t
            if na:
                @pl.when(first)
                def _():
                    dbias_scr[h, :, :nk] = ds

                @pl.when(jnp.logical_not(first))
                def _():
                    dbias_scr[h, :, :nk] += ds
            dsb = ds.astype(BF16)
            dqs.append(_dot(dsb, k2))
            dks.append(lax.dot_general(dsb, q2, _TN, preferred_element_type=F32))
            dvs.append(lax.dot_general(p.astype(BF16), do2b, _TN, preferred_element_type=F32))
        dq_ref[...] = jnp.where(lo, dqs[0], dqs[1]) * SCALE
        dk_ref[pl.ds(ks, nk), :] += jnp.where(lo, dks[0], dks[1]) * SCALE
        dv_ref[pl.ds(ks, nk), :] += jnp.where(lo, dvs[0], dvs[1])
        if na:
            last = jnp.logical_or(i == N_QBLK - 1, cls_ref[i] != cls_ref[jnp.minimum(i + 1, N_QBLK - 1)])
            for t in range(3):
                @pl.when(jnp.logical_and(last, cls_ref[i] == t))
                def _():
                    for hh in range(2):
                        for a in range(NA_QROWS):
                            for j in range(NA_PAIRS):
                                bank_scr[hh, int(slot[t, a, j])] += dbias_scr[
                                    hh, a * GRID_W:(a + 1) * GRID_W, j * LANES:(j + 1) * LANES]

            @pl.when(i == N_QBLK - 1)
            def _():
                r_i = lax.broadcasted_iota(jnp.int32, (GRID_W, GRID_W), 0)
                c_i = lax.broadcasted_iota(jnp.int32, (GRID_W, GRID_W), 1)
                rev = (r_i + c_i == GRID_W - 1).astype(F32)
                grp_ref[...] = jnp.zeros_like(grp_ref)
                for hh in range(2):
                    for m in range(NA_SLOTS):
                        xr = jnp.dot(rev, bank_scr[hh, m], preferred_element_type=F32, precision=lax.Precision.HIGHEST)
                        for half, ro in ((0, m - (NA_QROWS - 1)), (1, m - (NA_QROWS - 2))):
                            if 0 <= ro < _RPB_RO:
                                part = jnp.where(lo if half == 0 else jnp.logical_not(lo), xr, 0.0)
                                part = pltpu.roll(part, half * GRID_W, 1, stride=1, stride_axis=0)
                                grp_ref[hh, ro:ro + 1, :] += jnp.sum(part, axis=0, keepdims=True)

    w = npairs * LANES
    blk = lambda: pl.BlockSpec((qb, LANES), lambda p, i, ks, cl: (i, p))
    full = lambda: pl.BlockSpec((s, LANES), lambda p, i, ks, cl: (0, p))
    in_specs = [
        pl.BlockSpec((qb, LANES), lambda p, i, ks, cl: (i, qc0 + p)),
        pl.BlockSpec((s, LANES), lambda p, i, ks, cl: (0, kc0 + p)),
        pl.BlockSpec((s, LANES), lambda p, i, ks, cl: (0, vc0 + p)),
    ]
    if na:
        in_specs += _na_bias_specs()
        args = [kstart, cls, qkv, qkv, qkv, *na]
    else:
        in_specs.append(pl.BlockSpec((1, 1, qb, nk), lambda p, i, ks, cl: (cl[i], 0, 0, 0)))
        args = [kstart, cls, qkv, qkv, qkv, table]
    in_specs += [blk(), blk(), blk()]
    args += [do, o, lse]
    if has_dlse:
        in_specs.append(blk())
        args.append(dlse)
    out_specs = [blk(), full(), full()]
    out_shape = [jax.ShapeDtypeStruct((s, w), F32)] * 3
    scratch = []
    if na:
        out_specs.append(pl.BlockSpec((2, 16, LANES), lambda p, i, ks, cl: (p, 0, 0)))
        out_shape.append(jax.ShapeDtypeStruct((2 * npairs, 16, LANES), F32))
        scratch = [pltpu.VMEM((2, Q_BLOCK, NA_W), F32), pltpu.VMEM((2, Q_BLOCK, NA_W), F32),
                   pltpu.VMEM((2, NA_SLOTS, GRID_W, LANES), F32)]
    if order is not None:
        in_specs.append(pl.BlockSpec(order.shape, lambda p, i, ks, cl: (0, 0)))
        args.append(order)
    grid_spec = pltpu.PrefetchScalarGridSpec(num_scalar_prefetch=2, grid=(npairs, s // qb), in_specs=in_specs,
                                             out_specs=out_specs, scratch_shapes=scratch)
    return _pcall(body, name=name, grid_spec=grid_spec, out_shape=out_shape,
                  compiler_params=pltpu.CompilerParams(dimension_semantics=("arbitrary", "arbitrary")))(*args)


_NA_CLASS_R0 = (0, NA_QROWS, ROWS - NA_QROWS)
_NA_CLASS_K0 = (0, 0, ROWS - NA_KROWS)
_RPB_RO = 2 * NA_KH - 1
_RPB_CO = 2 * NA_KW - 1


def _na_constants():
    col = np.arange(GRID_W)
    cs = np.clip(col - NA_KW // 2, 0, GRID_W - NA_KW)
    vcol = (col[None, :] >= cs[:, None]) & (col[None, :] < cs[:, None] + NA_KW)
    colmask = np.where(np.concatenate([vcol, vcol], axis=1), 0.0, NEG).astype(np.float32)
    ks = np.clip(np.arange(N_QBLK) * NA_QROWS - NA_KH // 2, 0, ROWS - NA_KROWS) * GRID_W
    cls = np.ones(N_QBLK, np.int32)
    cls[0], cls[-1] = 0, 2
    return colmask, ks.astype(np.int32), cls


def _dil_constants(dilation):
    seg = SEQ // dilation
    nb = seg // DIL_QB
    nk = min(DIL_QB + 2 * DIL_HALF, seg)
    starts = [min(max(blk * DIL_QB - DIL_HALF, 0), seg - nk) for blk in range(nb)]
    shifts = sorted({w0 - blk * DIL_QB for blk, w0 in enumerate(starts)}, reverse=True)
    qi = np.arange(DIL_QB)[:, None]
    ki = np.arange(nk)[None, :]
    mask = np.stack([np.where(np.abs(ki + sh - qi) <= DIL_HALF, 0.0, NEG) for sh in shifts]).astype(np.float32)
    ks, cls = [], []
    for i in range(SEQ // DIL_QB):
        sub, blk = divmod(i, nb)
        cls.append(shifts.index(starts[blk] - blk * DIL_QB))
        ks.append(sub * seg + starts[blk])
    return mask.reshape(len(shifts), 1, DIL_QB, nk), np.asarray(ks, np.int32), np.asarray(cls, np.int32), nk


_VM = pl.BlockSpec(memory_space=pltpu.VMEM)


def _ada_fwd(c_all, w, b, *, name):
    def body(c_ref, w_ref, b_ref, o_ref):
        cv = c_ref[...]
        o_ref[...] = jnp.dot(cv * _sigmoid(cv), w_ref[...], preferred_element_type=F32,
                             precision=lax.Precision.HIGHEST) + b_ref[...]

    return _pcall(body, name=name, in_specs=[_VM, _VM, _VM], out_specs=_VM,
                  out_shape=jax.ShapeDtypeStruct((c_all.shape[0], w.shape[1]), F32))(c_all, w, b)


def _ada_bwd(c_all_t, dmod, *, name):
    def body(c_ref, d_ref, o_ref):
        cv = c_ref[...]
        o_ref[...] = jnp.dot(cv * _sigmoid(cv), d_ref[...], preferred_element_type=F32,
                             precision=lax.Precision.HIGHEST)

    return _pcall(body, name=name, in_specs=[_VM, _VM], out_specs=_VM,
                  out_shape=jax.ShapeDtypeStruct((c_all_t.shape[0], dmod.shape[1]), F32))(c_all_t, dmod)


def _row_sum(t, *, name):
    def body(t_ref, o_ref):
        o_ref[...] = jnp.sum(t_ref[...], axis=0, keepdims=True)

    return _pcall(body, name=name, in_specs=[_VM], out_specs=_VM,
                  out_shape=jax.ShapeDtypeStruct((1, t.shape[1]), F32))(t)


def _row_tile(rows):
    tr = rows
    for cand in range(8, 513, 8):
        if rows % cand == 0:
            tr = cand
    return tr


def _adamw_math(wv, gv, mv, vv):
    nm = ADAM_B1 * mv + (1.0 - ADAM_B1) * gv
    nv = ADAM_B2 * vv + (1.0 - ADAM_B2) * (gv * gv)
    m_hat = nm / (1.0 - ADAM_B1 ** ADAM_STEP)
    v_hat = nv / (1.0 - ADAM_B2 ** ADAM_STEP)
    return -ADAM_LR * (m_hat / (jnp.sqrt(v_hat) + ADAM_EPS) + ADAM_WD * wv), nm, nv


def _adamw(w, g, m, v, *, name):
    rows, cols = w.shape
    tr = _row_tile(rows)

    def body(w_ref, g_ref, m_ref, v_ref, d_ref, nm_ref, nv_ref):
        d_ref[...], nm_ref[...], nv_ref[...] = _adamw_math(w_ref[...], g_ref[...], m_ref[...], v_ref[...])

    spec = pl.BlockSpec((tr, cols), lambda i: (i, 0))
    return _pcall(body, name=name, grid=(rows // tr,), in_specs=[spec] * 4, out_specs=[spec] * 3,
                  out_shape=[jax.ShapeDtypeStruct((rows, cols), F32)] * 3, compiler_params=_PAR)(w, g, m, v)


def _adamw_halves(w, g_mine, g_other, m, v, c_arr, *, name):
    rows, cols = w.shape
    hr = rows // 2
    tr = _row_tile(hr)
    nt = hr // tr

    def body(c_ref, w_ref, t_ref, o_ref, m_ref, v_ref, g_ref, d_ref, nm_ref, nv_ref):
        gv = jnp.where(pl.program_id(0) == c_ref[0], t_ref[...], o_ref[...])
        g_ref[...] = gv
        d_ref[...], nm_ref[...], nv_ref[...] = _adamw_math(w_ref[...], gv, m_ref[...], v_ref[...])

    full = pl.BlockSpec((tr, cols), lambda h, i, c: (h * nt + i, 0))
    half = pl.BlockSpec((tr, cols), lambda h, i, c: (i, 0))
    grid_spec = pltpu.PrefetchScalarGridSpec(num_scalar_prefetch=1, grid=(2, nt),
                                             in_specs=[full, half, half, full, full], out_specs=[full] * 4)
    return _pcall(body, name=name, grid_spec=grid_spec, out_shape=[jax.ShapeDtypeStruct((rows, cols), F32)] * 4,
                  compiler_params=pltpu.CompilerParams(dimension_semantics=("parallel", "parallel")),
                  )(c_arr, w, g_mine, g_other, m, v)


_MESH = pl.DeviceIdType.MESH
_ANY = pl.BlockSpec(memory_space=pl.ANY)
_CHIP_FLIPS = ((1, 0), (0, 1), (1, 1))


def _pos():
    return lax.axis_index("x"), lax.axis_index("y"), lax.axis_index("c")


def _flip(v, f):
    return 1 - v if f else v


def _sem_pairs(n):
    return [pltpu.SemaphoreType.DMA((n,)), pltpu.SemaphoreType.DMA((n,))]


def _small_allgather(blk, *, name):
    m_per, n = blk.shape

    def body(x_ref, out_ref, send_sems, recv_sems, local_sem):
        x, y, c = _pos()
        me, sibling = (x, y, c), (x, y, 1 - c)
        chips = [(_flip(x, fx), _flip(y, fy)) for fx, fy in _CHIP_FLIPS]

        def rows(px, py, pc):
            return out_ref.at[pl.ds((4 * px + 2 * py + pc) * m_per, m_per), :]

        def copy(k, block, to, src=None):
            return pltpu.make_async_remote_copy(
                src_ref=rows(*block) if src is None else src, dst_ref=rows(*block),
                send_sem=send_sems.at[k], recv_sem=recv_sems.at[k], device_id=to, device_id_type=_MESH)

        mine = pltpu.make_async_copy(x_ref, rows(*me), local_sem)
        mine.start()
        first = [copy(0, me, sibling, src=x_ref)]
        first += [copy(1 + j, me, (*chip, c), src=x_ref) for j, chip in enumerate(chips)]
        for cp in first:
            cp.start()
        passed = [copy(4 + j, (*chip, c), sibling) for j, chip in enumerate(chips)]
        for j, chip in enumerate(chips):
            copy(1 + j, (*chip, c), me).wait_recv()
            passed[j].start()
        copy(0, sibling, me).wait_recv()
        for j, chip in enumerate(chips):
            copy(4 + j, (*chip, 1 - c), me).wait_recv()
        for cp in first + passed:
            cp.wait_send()
        mine.wait()

    return _pcall(
        body, name=name, out_shape=jax.ShapeDtypeStruct((N_DEV * m_per, n), blk.dtype),
        in_specs=[_VM], out_specs=_VM,
        scratch_shapes=_sem_pairs(7) + [pltpu.SemaphoreType.DMA],
    )(blk)


_HBM = pl.BlockSpec(memory_space=pltpu.HBM)
_SEM = pl.BlockSpec(memory_space=pltpu.SEMAPHORE)
_EFFECT = pltpu.SideEffectType.DATAFLOW_SIDE_EFFECTING


def _split_start(srcs, lands, plan, ncopies, after, *, name, alias_sources=True):
    ns, nl = len(srcs), len(lands)
    thru = (*srcs, *lands) if alias_sources else tuple(lands)

    def body(*refs):
        src_refs, land_refs = refs[:ns], refs[ns:ns + nl]
        send_sems, recv_sems = refs[ns + nl + 1], refs[ns + nl + 2]
        token = refs[-1]
        x, y, c = _pos()
        for k, (src, dst, to, _) in enumerate(plan(x, y, c, src_refs, land_refs)):
            pltpu.make_async_remote_copy(src_ref=src, dst_ref=dst, send_sem=send_sems.at[k], recv_sem=recv_sems.at[k],
                                         device_id=to, device_id_type=_MESH).start()
        token[...] = jnp.zeros_like(token)

    hbm = lambda a: pltpu.HBM(a.shape, a.dtype)
    out = _pcall(
        body, name=name,
        out_shape=(pltpu.SemaphoreType.DMA((ncopies,)), pltpu.SemaphoreType.DMA((ncopies,)),
                   *[hbm(a) for a in thru], jax.ShapeDtypeStruct((8, LANES), F32)),
        in_specs=[_HBM] * (ns + nl) + [_ANY], out_specs=(_SEM, _SEM, *[_HBM] * len(thru), _VM),
        input_output_aliases={ns + nl - len(thru) + i: 2 + i for i in range(len(thru))},
        compiler_params=pltpu.CompilerParams(has_side_effects=_EFFECT),
    )(*[pltpu.with_memory_space_constraint(a, pltpu.HBM) for a in (*srcs, *lands)], after)
    if alias_sources:
        return out[0], out[1], list(out[2:2 + ns]), list(out[2 + ns:2 + ns + nl]), out[-1]
    return out[0], out[1], list(srcs), list(out[2:2 + nl]), out[-1]


def _split_wait(send_sems, recv_sems, srcs, lands, plan, after, *, name, with_sources=False, alias_sources=True):
    ns, nl = len(srcs), len(lands)
    thru = (*srcs, *lands) if alias_sources else tuple(lands)

    def body(*refs):
        src_refs, land_refs = refs[:ns], refs[ns:ns + nl]
        send_sems, recv_sems = refs[ns + nl], refs[ns + nl + 1]
        x, y, c = _pos()
        for k, (src, _, _, mine) in enumerate(plan(x, y, c, src_refs, land_refs)):
            cp = pltpu.make_async_remote_copy(src_ref=src, dst_ref=mine, send_sem=send_sems.at[k],
                                              recv_sem=recv_sems.at[k], device_id=(x, y, c), device_id_type=_MESH)
            cp.wait_send()
            cp.wait_recv()

    hbm = lambda a: pltpu.HBM(a.shape, a.dtype)
    out = _pcall(
        body, name=name, out_shape=tuple(hbm(a) for a in thru),
        in_specs=[_HBM] * (ns + nl) + [_SEM, _SEM, _ANY], out_specs=tuple([_HBM] * len(thru)),
        input_output_aliases={ns + nl - len(thru) + i: i for i in range(len(thru))},
        compiler_params=pltpu.CompilerParams(has_side_effects=_EFFECT),
    )(*srcs, *lands, send_sems, recv_sems, after)
    got_srcs, got_lands = (list(out[:ns]), list(out[ns:])) if alias_sources else (list(srcs), list(out))
    return (got_srcs, got_lands) if with_sources else got_lands


def _ag_plan(nw):
    def plan(x, y, c, sh_refs, full_refs):
        j = 2 * x + y
        out = []
        for wi in range(nw):
            for fx, fy in _CHIP_FLIPS:
                px, py = _flip(x, fx), _flip(y, fy)
                out.append((sh_refs[wi].at[c], full_refs[wi].at[j, c], (px, py, c), full_refs[wi].at[2 * px + py, c]))
            out.append((sh_refs[wi], full_refs[wi].at[j], (x, y, 1 - c), full_refs[wi].at[j]))
        return out
    return plan


def _ag_plan_direct(nw):
    def plan(x, y, c, sh_refs, full_refs):
        j = 2 * x + y
        out = []
        for wi in range(nw):
            for fx, fy in _CHIP_FLIPS:
                px, py = _flip(x, fx), _flip(y, fy)
                for rel in (0, 1):
                    t = _flip(c, rel)
                    out.append((sh_refs[wi].at[c], full_refs[wi].at[j, c], (px, py, t), full_refs[wi].at[2 * px + py, t]))
            out.append((sh_refs[wi], full_refs[wi].at[j], (x, y, 1 - c), full_refs[wi].at[j]))
        return out
    return plan


def _ag_pass(fulls, *, name):
    nw = len(fulls)

    def body(*refs):
        in_refs, out_refs = refs[:nw], refs[nw:2 * nw]
        send_sems, recv_sems = refs[2 * nw:]
        x, y, c = _pos()
        cps = []
        for wi in range(nw):
            for k, (fx, fy) in enumerate(_CHIP_FLIPS):
                jp = 2 * _flip(x, fx) + _flip(y, fy)
                sems = dict(send_sem=send_sems.at[3 * wi + k], recv_sem=recv_sems.at[3 * wi + k], device_id_type=_MESH)
                send = pltpu.make_async_remote_copy(src_ref=in_refs[wi].at[jp, c], dst_ref=out_refs[wi].at[jp, c],
                                                    device_id=(x, y, 1 - c), **sems)
                recv = pltpu.make_async_remote_copy(src_ref=in_refs[wi].at[jp, c], dst_ref=out_refs[wi].at[jp, 1 - c],
                                                    device_id=(x, y, c), **sems)
                cps.append((send, recv))
        for send, _ in cps:
            send.start()
        for send, recv in cps:
            send.wait_send()
            recv.wait_recv()

    return _pcall(body, name=name, out_shape=[jax.ShapeDtypeStruct(f.shape, f.dtype) for f in fulls],
                  in_specs=[_ANY] * nw, out_specs=[_ANY] * nw, input_output_aliases={i: i for i in range(nw)},
                  scratch_shapes=_sem_pairs(3 * nw))(*fulls)


def _sib_plan(nw):
    def plan(x, y, c, g_refs, ra_refs):
        return [(g_refs[wi].at[k, 1 - c], ra_refs[wi].at[k], (x, y, 1 - c), ra_refs[wi].at[k])
                for wi in range(nw) for k in range(N_CHIP)]
    return plan


def _rs_plan(nw):
    def plan(x, y, c, s_refs, rb_refs):
        out = []
        for wi in range(nw):
            for k, (fx, fy) in enumerate(_CHIP_FLIPS):
                px, py = _flip(x, fx), _flip(y, fy)
                out.append((s_refs[wi].at[2 * px + py], rb_refs[wi].at[k], (px, py, c), rb_refs[wi].at[k]))
        return out
    return plan


def _sibling_swap(ts, *, name):
    nw = len(ts)

    def body(*refs):
        t_refs, out_refs = refs[:nw], refs[nw:2 * nw]
        send_sems, recv_sems = refs[2 * nw:]
        x, y, c = _pos()
        cps = [pltpu.make_async_remote_copy(src_ref=t_refs[wi], dst_ref=out_refs[wi], send_sem=send_sems.at[wi],
                                            recv_sem=recv_sems.at[wi], device_id=(x, y, 1 - c), device_id_type=_MESH)
               for wi in range(nw)]
        for cp in cps:
            cp.start()
        for cp in cps:
            cp.wait()

    return _pcall(body, name=name, out_shape=[jax.ShapeDtypeStruct(t.shape, t.dtype) for t in ts],
                  in_specs=[_ANY] * nw, out_specs=[_ANY] * nw, scratch_shapes=_sem_pairs(nw))(*ts)


def _rs_add(g, ra, c_arr, *, name):
    n, _, r, w = g.shape

    def body(c_ref, g_ref, ra_ref, sb_ref):
        sb_ref[...] = (g_ref[...] + ra_ref[...]).astype(BF16)

    grid_spec = pltpu.PrefetchScalarGridSpec(
        num_scalar_prefetch=1, grid=(n,),
        in_specs=[pl.BlockSpec((None, None, r, w), lambda k, c: (k, c[0], 0, 0)),
                  pl.BlockSpec((None, r, w), lambda k, c: (k, 0, 0))],
        out_specs=pl.BlockSpec((None, r, w), lambda k, c: (k, 0, 0)))
    return _pcall(body, name=name, grid_spec=grid_spec, out_shape=jax.ShapeDtypeStruct((n, r, w), BF16),
                  compiler_params=_PAR)(c_arr, g, ra)


def _rs_final(g, ra, rb, jc_arr, *, name):
    _, _, r, w = g.shape

    def body(jc_ref, g_ref, ra_ref, rb_ref, t_ref):
        s = g_ref[...] + ra_ref[...]
        t_ref[...] = ((s + rb_ref[0].astype(F32)) + rb_ref[1].astype(F32)) + rb_ref[2].astype(F32)

    grid_spec = pltpu.PrefetchScalarGridSpec(
        num_scalar_prefetch=1, grid=(1,),
        in_specs=[pl.BlockSpec((None, None, r, w), lambda i, jc: (jc[0], jc[1], 0, 0)),
                  pl.BlockSpec((None, r, w), lambda i, jc: (jc[0], 0, 0)),
                  pl.BlockSpec((3, r, w), lambda i, jc: (0, 0, 0))],
        out_specs=pl.BlockSpec((r, w), lambda i, jc: (0, 0)))
    return _pcall(body, name=name, grid_spec=grid_spec, out_shape=jax.ShapeDtypeStruct((r, w), F32),
                  compiler_params=_ARB)(jc_arr, g, ra, rb)


def _tile2(g):
    return jnp.concatenate([g, g], axis=1)


_BIG = ("w_in", "w_ffn_in", "w_ffn_out", "w_o", "w_proj_a", "w_proj_b")
_BIG_SHARD = {"w_in": (1024, 1472), "w_ffn_in": (1024, 1408), "w_ffn_out": (704, 1024), "w_o": (256, 1024),
              "w_proj_a": (512, 256), "w_proj_b": (256, 256)}


def _device_step(x2, tgt, mod, first_weights, late_weights, early_grads, mid_grads, g_norm1, g_norm2, b_gate, g_qa, g_ka,
                 g_qb, g_kb, rpb):
    d = D_MODEL
    sh1, sc1, gt1, sh2, sc2, gt2 = [mod[:, k * d:(k + 1) * d] for k in range(6)]

    colmask, na_ks, na_cls = _na_constants()
    rp = jnp.pad(rpb, ((0, 0), (0, 16 - _RPB_RO), (RP_LANE0, LANES - RP_LANE0 - _RPB_CO)), constant_values=NEG)
    na = (rp, jnp.asarray(colmask))
    na_ks, na_cls = jnp.asarray(na_ks), jnp.asarray(na_cls)
    dil = [_dil_constants(dd) for _, dd in DIL_CONFIGS]
    gains = jnp.concatenate([_tile2(g_qa), _tile2(g_ka), _tile2(g_qb), _tile2(g_kb)], axis=0)
    cos_t, sa_t, sb_t = _rope_tables()

    wts = first_weights(cos_t)
    h1, qkvn, qk_pre, gates, *qkv_dil = _pre_attn_fwd(x2, cos_t, sa_t, sb_t, g_norm1, sc1, sh1, wts["w_qkv"],
                                                      wts["w_gates"], gains, name="pre_attn_fwd")
    o_a, lse_a = _attn_fwd(qkvn, 0, 4, 8, 4, None, na_ks, na_cls, NA_NK, name="attn_a_fwd", na=na)
    arrs, o_g, l_g = [], [], []
    res = lambda t, dd: t if dd == 1 else (t, dd)
    for g, (_, dd) in enumerate(DIL_CONFIGS):
        tab_g, ks_g, cls_g, nk_g = jnp.asarray(dil[g][0]), jnp.asarray(dil[g][1]), jnp.asarray(dil[g][2]), dil[g][3]
        arr, cb = (qkvn, (12, 18, 24)) if dd == 1 else (qkv_dil.pop(0), (0, 2, 4))
        op, lp = _attn_fwd(arr, cb[0], cb[1], cb[2], 2, tab_g, ks_g, cls_g, nk_g, name=f"attn_d{g}_fwd", qb=DIL_QB)
        arrs.append((arr, cb, tab_g, ks_g, cls_g, nk_g))
        o_g.append(res(op, dd))
        l_g.append(res(lp, dd))
    wts = dict(wts, **late_weights(o_a))
    o_b, merged, mo, x1, h2 = _post_attn_fwd(o_a, o_g, l_g, gates, x2, wts["w_pa"], wts["w_pb"], wts["w_o"], b_gate,
                                             gt1, g_norm2, sc2, sh2, name="post_attn_fwd")
    act, ff = _ffn_fwd(h2, wts["w_ffn_in"], name="ffn_fwd")

    dy, dffo, dff, dgt2, loss_v = _ffn_mid(act, ff, x1, tgt, wts["w_ffn_out"], gt2, name="ffn_mid")
    grads = {}
    g_ffn_out = _wgrad(act, dffo, name="wg_ffn_out")
    grads["w_ffn_out"] = g_ffn_out.reshape(N_CHIP, D_FF // N_CHIP, d)
    grads["w_ffn_in"] = _wgrad(h2, dff, name="wg_ffn_in", chips=N_CHIP)
    dx1, dmo, sums2 = _ffn_in_bwd(dff, x1, dy, mo, wts["w_ffn_in"], g_norm2, sc2, gt1, name="ffn_in_bwd")
    grads["w_o"] = _wgrad(merged, dmo, name="wg_o").reshape(N_CHIP, d // N_CHIP, d)
    pab = _post_attn_bwd(dmo, gates, o_a, o_g, l_g, wts["w_pa"], wts["w_pb"], wts["w_o"], b_gate, name="post_attn_bwd")
    dpa, dpb, dgates, do_a = pab[:4]
    do_g, dl_g, dbg = pab[4:7], pab[7:10], pab[10]
    g_pa = _wgrad(o_a, dpa, name="wg_pa")
    g_pb = _wgrad(o_b, dpb, name="wg_pb")
    grads["w_proj_a"] = g_pa.reshape(WA, N_CHIP, d // N_CHIP).transpose(1, 0, 2)
    grads["w_proj_b"] = g_pb.reshape(WB_OUT, N_CHIP, d // N_CHIP).transpose(1, 0, 2)
    order = early_grads(grads)
    dqs, dks, dvs = [], [], []
    for g, (_, dd) in enumerate(DIL_CONFIGS):
        arr, cb, tab_g, ks_g, cls_g, nk_g = arrs[g]
        plain = lambda t: t[0] if isinstance(t, tuple) else t
        dq, dk, dv = _attn_bwd(arr, cb[0], cb[1], cb[2], 2, tab_g, ks_g, cls_g, nk_g, do_g[g], plain(o_g[g]),
                               plain(l_g[g]), name=f"attn_d{g}_bwd", dlse=dl_g[g], qb=DIL_QB, order=order)
        dqs.append(res(dq, dd))
        dks.append(res(dk, dd))
        dvs.append(res(dv, dd))
    order = mid_grads(dv)
    dqa, dka, dva, d_rp = _attn_bwd(qkvn, 0, 4, 8, 4, None, na_ks, na_cls, NA_NK, do_a, o_a, lse_a,
                                    name="attn_a_bwd", na=na, order=order)
    dqkv, grad_x, dgains, sums1 = _pre_attn_bwd(qk_pre, [dqa, dka, dva] + dqs + dks + dvs, dgates, x2, dx1, cos_t, sa_t,
                                                sb_t, wts["w_qkv"], wts["w_gates"], gains, g_norm1, sc1,
                                                name="pre_attn_bwd")
    g_qkv = _wgrad(h1, dqkv, name="wg_qkv")
    g_gates = _wgrad(h1, dgates, name="wg_gates")
    nc, cut = _BIG_SHARD["w_in"][1], 3 * _BIG_SHARD["w_in"][1] - W_QKV
    grads["w_in"] = jnp.stack([g_qkv[:, :nc], g_qkv[:, nc:2 * nc],
                               jnp.concatenate([g_qkv[:, 2 * nc:], g_gates[:, :cut]], axis=1), g_gates[:, cut:]])

    g_rpb = d_rp[:, :_RPB_RO, RP_LANE0:RP_LANE0 + _RPB_CO]

    dmod = jnp.concatenate([sums1[0:1], sums1[1:2], sums2[3:4], sums2[0:1], sums2[1:2], dgt2], axis=1)
    small = dict(g_norm1=sums1[2:3], g_norm2=sums2[2:3], b_gate=dbg, g_qa=dgains[0:1, :HEAD_DIM],
                 g_ka=dgains[1:2, :HEAD_DIM], g_qb=dgains[2:3, :HEAD_DIM], g_kb=dgains[3:4, :HEAD_DIM], rpb=g_rpb)
    return loss_v, grad_x, grads, dmod, small


_SMALL = ("b_ada", "g_norm1", "g_norm2", "b_gate", "g_qa", "g_ka", "g_qb", "g_kb", "rpb")
_SMALL_N = {"b_ada": 6 * D_MODEL, "g_norm1": D_MODEL, "g_norm2": D_MODEL, "b_gate": 2 * D_MODEL, "g_qa": HEAD_DIM,
            "g_ka": HEAD_DIM, "g_qb": HEAD_DIM, "g_kb": HEAD_DIM, "rpb": NA_HEADS * _RPB_RO * _RPB_CO}


def _pack_small(parts):
    flat = [parts[n].reshape(1, _SMALL_N[n]) for n in _SMALL]
    used = sum(_SMALL_N.values())
    return jnp.concatenate(flat + [jnp.zeros((1, STATS_W - used), F32)], axis=1)


def _unpack_small(v, shapes):
    out, at = {}, 0
    for n in _SMALL:
        out[n] = v[:, at:at + _SMALL_N[n]].reshape(shapes[n])
        at += _SMALL_N[n]
    return out


def _join_cols(t):
    _, r, c = t.shape
    return t.transpose(1, 0, 2).reshape(r, N_CHIP * c)


def kernel(x, c, w_ada, b_ada, g_norm1, g_norm2, w_in, b_gate, g_qa, g_ka, g_qb, g_kb, rpb, w_proj_a, w_proj_b, w_o, w_ffn_in, w_ffn_out, loss_target, m_w_ada, m_b_ada, m_g_norm1, m_g_norm2, m_w_in, m_b_gate, m_g_qa, m_g_ka, m_g_qb, m_g_kb, m_rpb, m_w_proj_a, m_w_proj_b, m_w_o, m_w_ffn_in, m_w_ffn_out, v_w_ada, v_b_ada, v_g_norm1, v_g_norm2, v_w_in, v_b_gate, v_g_qa, v_g_ka, v_g_qb, v_g_kb, v_rpb, v_w_proj_a, v_w_proj_b, v_w_o, v_w_ffn_in, v_w_ffn_out):
    names = ("w_ada", "b_ada", "g_norm1", "g_norm2", "w_in", "b_gate", "g_qa", "g_ka", "g_qb", "g_kb", "rpb",
             "w_proj_a", "w_proj_b", "w_o", "w_ffn_in", "w_ffn_out")
    w = dict(zip(names, (w_ada, b_ada, g_norm1, g_norm2, w_in, b_gate, g_qa, g_ka, g_qb, g_kb, rpb, w_proj_a, w_proj_b,
                         w_o, w_ffn_in, w_ffn_out)))
    m = dict(zip(names, (m_w_ada, m_b_ada, m_g_norm1, m_g_norm2, m_w_in, m_b_gate, m_g_qa, m_g_ka, m_g_qb, m_g_kb, m_rpb,
                         m_w_proj_a, m_w_proj_b, m_w_o, m_w_ffn_in, m_w_ffn_out)))
    v = dict(zip(names, (v_w_ada, v_b_ada, v_g_norm1, v_g_norm2, v_w_in, v_b_gate, v_g_qa, v_g_ka, v_g_qb, v_g_kb, v_rpb,
                         v_w_proj_a, v_w_proj_b, v_w_o, v_w_ffn_in, v_w_ffn_out)))
    d = D_MODEL
    xi, yi, ci = _pos()
    chip = 2 * xi + yi
    me = 2 * chip + ci
    ada_cols = 6 * d // N_CHIP

    c_arr, jc_arr = ci.reshape(1).astype(jnp.int32), jnp.stack([chip, ci]).astype(jnp.int32)
    first, rest = _BIG[:1], _BIG[1:]

    c_all = _small_allgather(c.reshape(8, d // 8), name="ag_c").reshape(N_DEV, d)
    b_sh = lax.dynamic_slice(b_ada, (0, chip * ada_cols), (1, ada_cols))
    mod_part = _ada_fwd(c_all, w_ada[0], b_sh, name="ada_fwd")
    mod_all = _small_allgather(mod_part, name="ag_mod").reshape(N_CHIP, 2, 8, ada_cols)[:, 0]
    mod = lax.dynamic_index_in_dim(mod_all, me, axis=1, keepdims=False).reshape(1, 6 * d)

    halves = {n: (2, _BIG_SHARD[n][0] // 2, _BIG_SHARD[n][1]) for n in _BIG}
    shards = {n: w[n][0].astype(BF16).reshape(halves[n]) for n in _BIG}
    land = lambda n: lax.empty((N_CHIP,) + halves[n], BF16)
    ag1 = _split_start([shards[n] for n in first], [land(n) for n in first], _ag_plan(1), 4, mod, name="ag1_start")
    ag2 = _split_start([shards[n] for n in rest], [land(n) for n in rest], _ag_plan_direct(len(rest)), 7 * len(rest),
                       ag1[4], name="ag2_start")
    rpb_after = rpb[0] + ag2[4][0, 0]

    def first_weights(after):
        after = after[:1, :1] + ag2[4][:1, :1]
        full1 = _split_wait(ag1[0], ag1[1], ag1[2], ag1[3], _ag_plan(1), after, name="ag1_wait")
        p_in = _ag_pass(full1, name="ag1_pass")[0].reshape((N_CHIP,) + _BIG_SHARD["w_in"])
        cut = W_QKV - 2 * _BIG_SHARD["w_in"][1]
        return dict(w_qkv=jnp.concatenate([p_in[0], p_in[1], p_in[2][:, :cut]], axis=1),
                    w_gates=jnp.concatenate([p_in[2][:, cut:], p_in[3]], axis=1))

    def late_weights(after):
        full2 = _split_wait(ag2[0], ag2[1], ag2[2], ag2[3], _ag_plan_direct(len(rest)), after, name="ag2_wait")
        full ={n: fu.reshape((N_CHIP,) + _BIG_SHARD[n]) for n, fu in zip(rest, full2)}
        return dict(w_pa=_join_cols(full["w_proj_a"]), w_pb=_join_cols(full["w_proj_b"]), w_o=full["w_o"].reshape(d, d),
                    w_ffn_in=_join_cols(full["w_ffn_in"]), w_ffn_out=full["w_ffn_out"].reshape(D_FF, d))

    def sib_begin(group, grads, tag):
        gps = [grads[n].reshape((N_CHIP,) + halves[n]) for n in group]
        lands = [lax.empty((N_CHIP,) + halves[n][1:], F32) for n in group]
        return _split_start(gps, lands, _sib_plan(len(group)), N_CHIP * len(group), gps[0], name=f"rs_sib_{tag}_start",
                            alias_sources=False)

    def rs_begin(group, sib, after, tag):
        gps, ras = _split_wait(sib[0], sib[1], sib[2], sib[3], _sib_plan(len(group)), after,
                               name=f"rs_sib_{tag}_wait", with_sources=True, alias_sources=False)
        sbs = [_rs_add(gp, ra, c_arr, name=f"rs_add_{n}") for n, gp, ra in zip(group, gps, ras)]
        lands = [lax.empty((3,) + halves[n][1:], BF16) for n in group]
        st = _split_start(sbs, lands, _rs_plan(len(group)), 3 * len(group), sbs[0], name=f"rs_{tag}_start")
        return (gps, ras), st

    def rs_end(group, begun, after, tag):
        (gps, ras), st = begun
        rbs = _split_wait(st[0], st[1], st[2], st[3], _rs_plan(len(group)), after, name=f"rs_{tag}_wait")
        return [_rs_final(gp, ra, rb, jc_arr, name=f"rs_final_{n}") for n, gp, ra, rb in zip(group, gps, ras, rbs)]

    begun = {}

    def early_grads(grads):
        begun["sib_rest"] = sib_begin(rest, grads, "rest")
        return begun["sib_rest"][4]

    def mid_grads(after):
        begun["rest"] = rs_begin(rest, begun["sib_rest"], after, "rest")
        return begun["rest"][1][4]

    loss_v, grad_x, grads, dmod, small = _device_step(
        x[0], loss_target[0], mod, first_weights, late_weights, early_grads, mid_grads, g_norm1, g_norm2, b_gate, g_qa,
        g_ka, g_qb, g_kb, rpb_after)
    sib_first = sib_begin(first, grads, "first")

    g, delta, new_m, new_v = {}, {}, {}, {}

    def finish(group, ts, tag):
        others = _sibling_swap(ts, name=f"rs_pair_{tag}")
        for n, t, o in zip(group, ts, others):
            gg, dl, nm, nv = _adamw_halves(w[n][0], t, o, m[n][0], v[n][0], c_arr, name=f"adamw_{n}")
            g[n], delta[n], new_m[n], new_v[n] = gg[None], dl[None], nm[None], nv[None]

    finish(rest, rs_end(rest, begun["rest"], sib_first[4], "rest"), "rest")
    done_rest = sum(new_v[n][0, :1, :1] for n in rest)
    begun["first"] = rs_begin(first, sib_first, done_rest, "first")

    stats = _pack_small(dict(b_ada=dmod, **small)) + begun["first"][1][4][0, 0]
    stats = stats.at[:, STATS_W - 1].set(loss_v[0, 0])
    rows = _small_allgather(stats.reshape(8, STATS_W // 8), name="ag_stats").reshape(N_DEV, STATS_W)
    dmod_sh = lax.dynamic_slice(rows, (0, chip * ada_cols), (8, ada_cols))
    g_ada = _ada_bwd(c_all.T, dmod_sh, name="ada_bwd")
    tot = _row_sum(rows, name="stats_sum")
    g_small = _unpack_small(tot, {n: w[n].shape for n in _SMALL})

    finish(first, rs_end(first, begun["first"], tot, "first"), "first")

    dl, nm, nv = _adamw(w_ada[0], g_ada, m_w_ada[0], v_w_ada[0], name="adamw_w_ada")
    g["w_ada"], delta["w_ada"], new_m["w_ada"], new_v["w_ada"] = g_ada[None], dl[None], nm[None], nv[None]
    shapes = {n: w[n].shape for n in _SMALL}
    dl, nm, nv = _adamw(_pack_small({n: w[n] for n in _SMALL}), tot, _pack_small({n: m[n] for n in _SMALL}),
                        _pack_small({n: v[n] for n in _SMALL}), name="adamw_small")
    delta.update(_unpack_small(dl, shapes))
    new_m.update(_unpack_small(nm, shapes))
    new_v.update(_unpack_small(nv, shapes))
    g.update(g_small)

    loss = tot[0, STATS_W - 1]
    return (loss, grad_x[None], *[g[n] for n in names], *[delta[n] for n in names], *[new_m[n] for n in names],
            *[new_v[n] for n in names])
```

```python
import numpy as np

import jax
import jax.numpy as jnp
from jax import lax
from jax.experimental import pallas as pl
from jax.experimental.pallas import tpu as pltpu

F32 = jnp.float32
BF16 = jnp.bfloat16

D_MODEL = 1024
SEQ = 8192
HEAD_DIM = 64
GRID_W = 64
ROWS = SEQ // GRID_W
NA_HEADS = 8
NA_KH = 8
NA_KW = 16
DIL_CONFIGS = ((128, 1), (512, 4), (2048, 16))
ROT_DIM = 16
ROPE_THETA = 500000.0
D_FF = 2816
EPS = 1e-6
NEG = -1e30
WA = 512
WB = 768
WB_OUT = 256
W_QKV = 3 * WA + 3 * WB
W_QK = 2 * WA + 2 * WB
W_GATES = 2 * D_MODEL
SCALE = HEAD_DIM ** -0.5

ADAM_LR = 0.001
ADAM_B1 = 0.9
ADAM_B2 = 0.999
ADAM_EPS = 1e-08
ADAM_WD = 0.01
ADAM_STEP = 10

LANES = 128
ROW_TILE = 256
ROW_TILES = {"ffn_fwd": 512, "post_attn_fwd": 512, "post_attn_bwd": 512,
             "ffn_in_bwd": 512}
Q_BLOCK = 256
NA_QROWS = Q_BLOCK // GRID_W
NA_KROWS = NA_QROWS + NA_KH - 1
NA_NK = NA_KROWS * GRID_W
NA_PAIRS = (NA_KROWS + 1) // 2
NA_W = NA_PAIRS * LANES
NA_RO_NONE = 15
NA_SLOTS = 21
RP_LANE0 = GRID_W - NA_KW
DIL_HALF = 64
DIL_QB = 512
N_QBLK = SEQ // Q_BLOCK

N_DEV = 8
N_CHIP = 4
FF_CHIP = 2 * D_FF // N_CHIP
STATS_W = 14336


def _pcall(body, *, name, **kw):
    return pl.pallas_call(body, name=name, **kw)


_NT = (((1,), (1,)), ((), ()))
_TN = (((0,), (0,)), ((), ()))
_ARB = pltpu.CompilerParams(dimension_semantics=("arbitrary",))
_PAR = pltpu.CompilerParams(dimension_semantics=("parallel",))


def _dot(a, b):
    return jnp.dot(a, b, preferred_element_type=F32)


def _dot_nt(a, b):
    return lax.dot_general(a, b, _NT, preferred_element_type=F32)


def _wgrad(a, b, *, name, tm, tn, tk=1024, chips=None):
    s, ma = a.shape
    nb = b.shape[1]
    nk = s // tk
    nc = nb // chips if chips else tn
    cpb = tn // nc

    def body(a_ref, b_ref, o_ref, acc):
        k = pl.program_id(2)
        r = lax.dot_general(a_ref[...].astype(BF16), b_ref[...].astype(BF16), _TN, preferred_element_type=F32)

        @pl.when(k == 0)
        def _():
            acc[...] = r

        @pl.when(k > 0)
        def _():
            acc[...] += r

        @pl.when(k == nk - 1)
        def _():
            if chips:
                for q in range(cpb):
                    o_ref[q] = acc[:, q * nc:(q + 1) * nc]
            else:
                o_ref[...] = acc[...]

    if chips:
        o_spec = pl.BlockSpec((cpb, tm, nc), lambda i, j, k: (j, i, 0))
        out_shape = jax.ShapeDtypeStruct((chips, ma, nc), F32)
    else:
        o_spec = pl.BlockSpec((tm, tn), lambda i, j, k: (i, j))
        out_shape = jax.ShapeDtypeStruct((ma, nb), F32)
    return _pcall(
        body, name=name, grid=(ma // tm, nb // tn, nk),
        in_specs=[pl.BlockSpec((tk, tm), lambda i, j, k: (k, i)), pl.BlockSpec((tk, tn), lambda i, j, k: (k, j))],
        out_specs=o_spec, out_shape=out_shape, scratch_shapes=[pltpu.VMEM((tm, tn), F32)],
        compiler_params=pltpu.CompilerParams(dimension_semantics=("parallel", "parallel", "arbitrary")),
    )(a, b)


def _row_call(body, *, name, row_ins, res_ins, row_outs, acc_outs=(), scratch=()):
    row_ins = [a if isinstance(a, tuple) else (a, 1) for a in row_ins]
    row_outs = [o if len(o) == 3 else (*o, 1) for o in row_outs]
    s = row_ins[0][0].shape[0]
    tile = ROW_TILES.get(name, ROW_TILE)
    n = s // tile
    nri, nre, nro, nao = len(row_ins), len(res_ins), len(row_outs), len(acc_outs)

    def whole(shape):
        nd = len(shape)
        return pl.BlockSpec(tuple(shape), lambda i: (0,) * nd, pipeline_mode=pl.Buffered(1))

    def whole_out(shape):
        nd = len(shape)
        return pl.BlockSpec(tuple(shape), lambda i: (0,) * nd)

    def rows(w, d):
        if d == 1:
            return pl.BlockSpec((tile, w), lambda i: (i, 0))
        return pl.BlockSpec((d, tile // d, w), lambda i: (0, i, 0))

    in_specs = [rows(a.shape[1], d) for a, d in row_ins]
    in_specs += [whole(a.shape) for a in res_ins]
    out_specs = [rows(w, d) for w, _, d in row_outs]
    out_specs += [whole_out(shp) for shp, _ in acc_outs]
    out_shape = [jax.ShapeDtypeStruct((s, w) if d == 1 else (d, s // d, w), dt) for w, dt, d in row_outs]
    out_shape += [jax.ShapeDtypeStruct(tuple(shp), dt) for shp, dt in acc_outs]

    def wrapped(*refs):
        at = [0, nri, nri + nre, nri + nre + nro, nri + nre + nro + nao]
        body(pl.program_id(0), n, refs[at[0]:at[1]], refs[at[1]:at[2]], refs[at[2]:at[3]], refs[at[3]:at[4]],
             refs[at[4]:])

    args = [a if d == 1 else a.reshape(d, s // d, a.shape[1]) for a, d in row_ins]
    outs = _pcall(wrapped, name=name, grid=(n,), in_specs=in_specs, out_specs=out_specs, out_shape=out_shape,
                  scratch_shapes=list(scratch), compiler_params=_ARB)(*args, *res_ins)
    return [o.reshape(s, o.shape[-1]) if k < nro and row_outs[k][2] != 1 else o for k, o in enumerate(outs)]


def _stage_shape(name):
    return pltpu.VMEM((4, ROW_TILES.get(name, ROW_TILE), LANES), F32)


def _from_residue(ref, col, stage, slot):
    d, n = ref.shape[0], ref.shape[1]
    for r in range(d):
        stage.at[slot][pl.ds(r, n, stride=d), :] = ref[r, :, col:col + LANES].astype(F32)
    return stage[slot]


def _natural(ref, stage, slot0):
    if len(ref.shape) == 2:
        return ref[...]
    return jnp.concatenate([_from_residue(ref, c * LANES, stage, (slot0 + c) % 4)
                            for c in range(ref.shape[2] // LANES)], axis=1)


def _to_residue(val, ref, col, stage, slot):
    d, n = ref.shape[0], ref.shape[1]
    stage[slot] = val
    for r in range(d):
        ref[r, :, col:col + LANES] = stage.at[slot][pl.ds(r, n, stride=d), :].astype(ref.dtype)


def _fold8(t):
    r, w = t.shape
    return jnp.sum(t.reshape(r // 8, 8, w), axis=0)


def _sigmoid(t):
    return 0.5 * (jnp.tanh(0.5 * t) + 1.0)


def _head_lanes():
    return lax.broadcasted_iota(jnp.int32, (1, LANES), 1) < HEAD_DIM


def _head_mean(t, lo):
    s_lo = jnp.sum(jnp.where(lo, t, 0.0), axis=1, keepdims=True)
    s_hi = jnp.sum(jnp.where(lo, 0.0, t), axis=1, keepdims=True)
    return jnp.where(lo, s_lo, s_hi) * (1.0 / HEAD_DIM)


def _rms_mod(xv, g, sc, sh):
    rstd = lax.rsqrt(jnp.mean(xv * xv, axis=1, keepdims=True) + EPS)
    return (xv * rstd * g) * (1.0 + sc) + sh


def _rms_mod_bwd(xv, dh, g, sc):
    rstd = lax.rsqrt(jnp.mean(xv * xv, axis=1, keepdims=True) + EPS)
    xhat = xv * rstd
    dn = dh * (1.0 + sc)
    dxhat = dn * g
    dx = rstd * (dxhat - xhat * jnp.mean(dxhat * xhat, axis=1, keepdims=True))
    return dx, dh, dh * (xhat * g), dn * xhat


def _mix_weights(ls):
    m = jnp.maximum(jnp.maximum(ls[0], ls[1]), ls[2])
    es = [jnp.exp(t - m) for t in ls]
    den = es[0] + es[1] + es[2]
    return [e / den for e in es]


def _rope_tables():
    half = ROT_DIM // 2
    inv_freq = ROPE_THETA ** (-(jnp.arange(half, dtype=F32) * 2.0) / ROT_DIM)
    lane = np.arange(LANES) % HEAD_DIM
    ang = jnp.arange(SEQ).astype(F32)[:, None] * jnp.tile(inv_freq, LANES // half)[None, :]
    cos, sin = jnp.cos(ang), jnp.sin(ang)
    first, second = jnp.asarray(lane < half)[None, :], jnp.asarray((lane >= half) & (lane < ROT_DIM))[None, :]
    cos_t = jnp.where(first | second, cos, 1.0)
    return cos_t, jnp.where(second, sin, 0.0), jnp.where(first, -sin, 0.0)


_SECTIONS = ((0, WA, 0, False), (WA, 2 * WA, 1, False), (2 * WA, 3 * WA, -1, False),
             (3 * WA, 3 * WA + WB, 2, True), (3 * WA + WB, 3 * WA + 2 * WB, 3, True), (3 * WA + 2 * WB, W_QKV, -1, False))


def _pre_attn_fwd(x, cos_t, sa_t, sb_t, g1, sc1, sh1, w_qkv, w_gates, gains, *, name):
    half = ROT_DIM // 2
    dilated = [(g, dd) for g, (_, dd) in enumerate(DIL_CONFIGS) if dd > 1]

    def body(i, n, rin, res, rout, aout, scr):
        x_ref, cos_ref, sa_ref, sb_ref = rin
        g_ref, sc_ref, sh_ref, wq_ref, wg_ref, gains_ref = res
        h1_ref, qkvn_ref, pre_ref, gates_ref = rout[:4]
        group_ref = {g: rout[4 + k] for k, (g, _) in enumerate(dilated)}
        (stage,) = scr
        staged = 0
        hb = _rms_mod(x_ref[...], g_ref[...], sc_ref[...], sh_ref[...]).astype(BF16)
        h1_ref[...] = hb
        gates_ref[...] = _dot(hb, wg_ref[...]).astype(BF16)
        lo = _head_lanes()
        cosv, sav, sbv = cos_ref[...], sa_ref[...], sb_ref[...]
        pre_at = 0
        for si, (c0, c1, kind, rot) in enumerate(_SECTIONS):
            sec = _dot(hb, wq_ref[:, c0:c1])
            for ch in range((c1 - c0) // LANES):
                t = sec[:, ch * LANES:(ch + 1) * LANES]
                if kind >= 0:
                    pre_ref[:, pre_at:pre_at + LANES] = t.astype(BF16)
                    pre_at += LANES
                    t = t * lax.rsqrt(_head_mean(t * t, lo) + EPS) * gains_ref[kind:kind + 1, :]
                    if rot:
                        t = t * cosv + pltpu.roll(t, half, 1) * sav + pltpu.roll(t, LANES - half, 1) * sbv
                qkvn_ref[:, c0 + ch * LANES:c0 + (ch + 1) * LANES] = t.astype(BF16)
                group = ch * LANES // WB_OUT if si >= 3 else 0
                if group in group_ref:
                    col = (si - 3) * WB_OUT + ch * LANES % WB_OUT
                    _to_residue(t, group_ref[group], col, stage, staged % 4)
                    staged += 1

    return _row_call(body, name=name, row_ins=[x, cos_t, sa_t, sb_t], res_ins=[g1, sc1, sh1, w_qkv, w_gates, gains],
                     row_outs=[(D_MODEL, BF16), (W_QKV, BF16), (W_QK, BF16), (W_GATES, BF16)]
                     + [(3 * WB_OUT, BF16, dd) for _, dd in dilated], scratch=[_stage_shape(name)])


def _pre_attn_bwd(qk_pre, d_parts, dgates, x, dx1, cos_t, sa_t, sb_t, w_qkv, w_gates, gains, g1, sc1, *, name):
    half = ROT_DIM // 2
    nparts = len(d_parts)
    where = []
    residue = [isinstance(part, tuple) for part in d_parts]
    for pi, part in enumerate(d_parts):
        width = (part[0] if residue[pi] else part).shape[1]
        where += [(pi, cj) for cj in range(width // LANES)]
    assert len(where) == W_QKV // LANES

    def body(i, n, rin, res, rout, aout, scr):
        pre_ref, d_refs = rin[0], rin[1:1 + nparts]
        dgates_ref, x_ref, dx1_ref, cos_ref, sa_ref, sb_ref = rin[1 + nparts:]
        wq_ref, wg_ref, gains_ref, g_ref, sc_ref = res
        dqkv_ref, gx_ref = rout
        dgains_ref, sums_ref = aout
        accg, accs, stage = scr
        staged = 0

        @pl.when(i == 0)
        def _():
            accg[...] = jnp.zeros_like(accg)
            accs[...] = jnp.zeros_like(accs)

        lo = _head_lanes()
        cosv, sav, sbv = cos_ref[...], sa_ref[...], sb_ref[...]
        dh = _dot_nt(dgates_ref[...], wg_ref[...])
        pre_at = 0
        for c0, c1, kind, rot in _SECTIONS:
            for ch in range((c1 - c0) // LANES):
                pi, cj = where[c0 // LANES + ch]
                if residue[pi]:
                    dt = _from_residue(d_refs[pi], cj * LANES, stage, staged % 4)
                    staged += 1
                else:
                    dt = d_refs[pi][:, cj * LANES:(cj + 1) * LANES]
                if kind >= 0:
                    if rot:
                        dt = dt * cosv + pltpu.roll(dt * sav, LANES - half, 1) + pltpu.roll(dt * sbv, half, 1)
                    t = pre_ref[:, pre_at:pre_at + LANES].astype(F32)
                    pre_at += LANES
                    rstd = lax.rsqrt(_head_mean(t * t, lo) + EPS)
                    xhat = t * rstd
                    accg[kind] += _fold8(dt * xhat)
                    dxhat = dt * gains_ref[kind:kind + 1, :]
                    dt = rstd * (dxhat - xhat * _head_mean(dxhat * xhat, lo))
                dqkv_ref[:, c0 + ch * LANES:c0 + (ch + 1) * LANES] = dt.astype(BF16)
            dh = dh + _dot_nt(dqkv_ref[:, c0:c1], wq_ref[:, c0:c1])
        dx, t_sh, t_sc, t_g = _rms_mod_bwd(x_ref[...], dh, g_ref[...], sc_ref[...])
        gx_ref[...] = dx1_ref[...] + dx
        accs[0] += _fold8(t_sh)
        accs[1] += _fold8(t_sc)
        accs[2] += _fold8(t_g)

        @pl.when(i == n - 1)
        def _():
            t = jnp.sum(accg[...], axis=1)
            dgains_ref[...] = t + pltpu.roll(t, HEAD_DIM, 1)
            sums_ref[...] = jnp.sum(accs[...], axis=1)

    return _row_call(
        body, name=name, row_ins=[qk_pre, *d_parts, dgates, x, dx1, cos_t, sa_t, sb_t],
        res_ins=[w_qkv, w_gates, gains, g1, sc1], row_outs=[(W_QKV, BF16), (D_MODEL, F32)],
        acc_outs=[((4, LANES), F32), ((3, D_MODEL), F32)],
        scratch=[pltpu.VMEM((4, 8, LANES), F32), pltpu.VMEM((3, 8, D_MODEL), F32), _stage_shape(name)])


def _post_attn_fwd(o_a, o_g, l_g, gates, x, w_pa, w_pb, w_o, b_gate, gt1, g2, sc2, sh2, *, name):
    d = D_MODEL

    def body(i, n, rin, res, rout, aout, scr):
        oa_ref, o0, o1, o2, l0, l1, l2, gates_ref, x_ref = rin
        wpa_ref, wpb_ref, wo_ref, b_ref, gt_ref, g_ref, sc_ref, sh_ref = res
        mo_ref, x1_ref, h2_ref = rout
        (stage,) = scr
        ogs = [_natural(r, stage, 0) for r in (o0, o1, o2)]
        ws = _mix_weights([_natural(r, stage, 2) for r in (l0, l1, l2)])
        obb = (ws[0] * ogs[0] + ws[1] * ogs[1] + ws[2] * ogs[2]).astype(BF16)
        pa = _dot(oa_ref[...].astype(BF16), wpa_ref[...])
        pb = _dot(obb, wpb_ref[...])
        ga = _sigmoid(gates_ref[:, :d].astype(F32) + b_ref[:, :d])
        gb = _sigmoid(gates_ref[:, d:].astype(F32) + b_ref[:, d:])
        merged = (ga * pa + gb * pb).astype(BF16)
        mo = _dot(merged, wo_ref[...])
        mo_ref[...] = mo.astype(BF16)
        x1 = x_ref[...] + gt_ref[...] * mo
        x1_ref[...] = x1
        h2_ref[...] = _rms_mod(x1, g_ref[...], sc_ref[...], sh_ref[...]).astype(BF16)

    return _row_call(body, name=name, row_ins=[o_a, *o_g, *l_g, gates, x],
                     res_ins=[w_pa, w_pb, w_o, b_gate, gt1, g2, sc2, sh2],
                     row_outs=[(d, BF16), (d, F32), (d, BF16)], scratch=[_stage_shape(name)])


def _ffn_fwd(h2, w_ffn_in, *, name):
    def body(i, n, rin, res, rout, aout, scr):
        (h_ref,), (w_ref,), (act_ref, ff_ref) = rin, res, rout
        hv = h_ref[...]
        for q in range(2):
            a = _dot(hv, w_ref[:, q * FF_CHIP:(q + 1) * FF_CHIP])
            up = _dot(hv, w_ref[:, D_FF + q * FF_CHIP:D_FF + (q + 1) * FF_CHIP])
            sl = slice(q * FF_CHIP, (q + 1) * FF_CHIP)
            act_ref[:, sl] = (a * _sigmoid(a) * up).astype(BF16)
            ff_ref[:, sl] = a.astype(BF16)
            ff_ref[:, D_FF + q * FF_CHIP:D_FF + (q + 1) * FF_CHIP] = up.astype(BF16)

    return _row_call(body, name=name, row_ins=[h2], res_ins=[w_ffn_in], row_outs=[(D_FF, BF16), (2 * D_FF, BF16)])


def _ffn_mid(act, ff, x1, tgt, w_ffn_out, gt2, *, name):
    d = D_MODEL

    def body(i, n, rin, res, rout, aout, scr):
        act_ref, ff_ref, x1_ref, tgt_ref = rin
        wo_ref, gt_ref = res
        dy_ref, dffo_ref, dff_ref = rout
        dgt_ref, loss_ref = aout
        (acc,) = scr

        @pl.when(i == 0)
        def _():
            acc[...] = jnp.zeros_like(acc)

        ffo = _dot(act_ref[...], wo_ref[...])
        gtv = gt_ref[...]
        e = x1_ref[...] + gtv * ffo - tgt_ref[...]
        dy = e * (1.0 / d)
        dy_ref[...] = dy
        dffo = (gtv * dy).astype(BF16)
        dffo_ref[...] = dffo
        acc[0] += _fold8(dy * ffo)
        acc[1] += _fold8(e * e)
        for q in range(2):
            sl = slice(q * FF_CHIP, (q + 1) * FF_CHIP)
            su = slice(D_FF + q * FF_CHIP, D_FF + (q + 1) * FF_CHIP)
            dact = _dot_nt(dffo, wo_ref[sl, :])
            a = ff_ref[:, sl].astype(F32)
            up = ff_ref[:, su].astype(F32)
            sg = _sigmoid(a)
            dff_ref[:, sl] = (dact * up * (sg * (1.0 + a * (1.0 - sg)))).astype(BF16)
            dff_ref[:, su] = (dact * (a * sg)).astype(BF16)

        @pl.when(i == n - 1)
        def _():
            dgt_ref[...] = jnp.sum(acc[0], axis=0, keepdims=True)
            tot = jnp.sum(jnp.sum(acc[1], axis=0, keepdims=True), axis=1, keepdims=True)
            loss_ref[...] = jnp.broadcast_to(tot * (0.5 / d), (1, LANES))

    return _row_call(body, name=name, row_ins=[act, ff, x1, tgt], res_ins=[w_ffn_out, gt2],
                     row_outs=[(d, F32), (d, BF16), (2 * D_FF, BF16)], acc_outs=[((1, d), F32), ((1, LANES), F32)],
                     scratch=[pltpu.VMEM((2, 8, d), F32)])


def _ffn_in_bwd(dff, x1, dy, mo, w_ffn_in, g2, sc2, gt1, *, name):
    d = D_MODEL

    def body(i, n, rin, res, rout, aout, scr):
        dff_ref, x1_ref, dy_ref, mo_ref = rin
        w_ref, g_ref, sc_ref, gt_ref = res
        dx1_ref, dmo_ref = rout
        (sums_ref,) = aout
        (acc,) = scr

        @pl.when(i == 0)
        def _():
            acc[...] = jnp.zeros_like(acc)

        dh = _dot_nt(dff_ref[...], w_ref[...])
        dx, t_sh, t_sc, t_g = _rms_mod_bwd(x1_ref[...], dh, g_ref[...], sc_ref[...])
        dx1 = dy_ref[...] + dx
        dx1_ref[...] = dx1
        dmo_ref[...] = (gt_ref[...] * dx1).astype(BF16)
        acc[0] += _fold8(t_sh)
        acc[1] += _fold8(t_sc)
        acc[2] += _fold8(t_g)
        acc[3] += _fold8(dx1 * mo_ref[...].astype(F32))

        @pl.when(i == n - 1)
        def _():
            sums_ref[...] = jnp.sum(acc[...], axis=1)

    return _row_call(body, name=name, row_ins=[dff, x1, dy, mo], res_ins=[w_ffn_in, g2, sc2, gt1],
                     row_outs=[(d, F32), (d, BF16)], acc_outs=[((4, d), F32)], scratch=[pltpu.VMEM((4, 8, d), F32)])


def _post_attn_bwd(dmo, gates, o_a, o_g, l_g, w_pa, w_pb, w_o, b_gate, *, name):
    d = D_MODEL

    def body(i, n, rin, res, rout, aout, scr):
        dmo_ref, gates_ref, oa_ref, o0, o1, o2, l0, l1, l2 = rin
        wpa_ref, wpb_ref, wo_ref, b_ref = res
        dgates_ref, doa_ref = rout[:2]
        do_refs, dl_refs = rout[2:5], rout[5:8]
        dbg_ref, gpa_ref, gpb_ref, go_ref = aout
        acc, stage = scr

        @pl.when(i == 0)
        def _():
            acc[...] = jnp.zeros_like(acc)
            gpa_ref[...] = jnp.zeros_like(gpa_ref)
            gpb_ref[...] = jnp.zeros_like(gpb_ref)
            go_ref[...] = jnp.zeros_like(go_ref)

        ogs = [_natural(r, stage, 0) for r in (o0, o1, o2)]
        ws = _mix_weights([_natural(r, stage, 2) for r in (l0, l1, l2)])
        obb = (ws[0] * ogs[0] + ws[1] * ogs[1] + ws[2] * ogs[2]).astype(BF16)
        oa = oa_ref[...].astype(BF16)
        pa = _dot(oa, wpa_ref[...])
        pb = _dot(obb, wpb_ref[...])
        ga = _sigmoid(gates_ref[:, :d].astype(F32) + b_ref[:, :d])
        gb = _sigmoid(gates_ref[:, d:].astype(F32) + b_ref[:, d:])
        merged = (ga * pa + gb * pb).astype(BF16)
        go_ref[...] += lax.dot_general(merged, dmo_ref[...], _TN, preferred_element_type=F32)
        dm = _dot_nt(dmo_ref[...], wo_ref[...])
        dpa = (dm * ga).astype(BF16)
        dpb = (dm * gb).astype(BF16)
        gpa_ref[...] += lax.dot_general(oa, dpa, _TN, preferred_element_type=F32)
        gpb_ref[...] += lax.dot_general(obb, dpb, _TN, preferred_element_type=F32)
        dga = dm * pa * ga * (1.0 - ga)
        dgb = dm * pb * gb * (1.0 - gb)
        dgates_ref[:, :d] = dga.astype(BF16)
        dgates_ref[:, d:] = dgb.astype(BF16)
        acc[:, :d] += _fold8(dga)
        acc[:, d:] += _fold8(dgb)
        doa_ref[...] = _dot_nt(dpa, wpa_ref[...])
        dob = _dot_nt(dpb, wpb_ref[...])
        lo = _head_lanes()
        for ch in range(WB_OUT // LANES):
            sl = slice(ch * LANES, (ch + 1) * LANES)
            dv = dob[:, sl]
            wc = [w[:, sl] for w in ws]
            ts = [_head_mean(dv * og[:, sl], lo) * float(HEAD_DIM) for og in ogs]
            tbar = wc[0] * ts[0] + wc[1] * ts[1] + wc[2] * ts[2]
            for g in range(3):
                for k, (ref, val) in enumerate(((do_refs[g], wc[g] * dv), (dl_refs[g], wc[g] * (ts[g] - tbar)))):
                    if len(ref.shape) == 2:
                        ref[:, sl] = val
                    else:
                        _to_residue(val, ref, ch * LANES, stage, (2 * g + k) % 4)

        @pl.when(i == n - 1)
        def _():
            dbg_ref[...] = jnp.sum(acc[...], axis=0, keepdims=True)

    return _row_call(body, name=name, row_ins=[dmo, gates, o_a, *o_g, *l_g], res_ins=[w_pa, w_pb, w_o, b_gate],
                     row_outs=[(2 * d, BF16), (WA, F32)]
                     + 2 * [(WB_OUT, F32, dd) for _, dd in DIL_CONFIGS],
                     acc_outs=[((1, 2 * d), F32), ((WA, d), F32), ((WB_OUT, d), F32), ((d, d), F32)],
                     scratch=[pltpu.VMEM((8, 2 * d), F32), _stage_shape(name)])


def _na_class_tables():
    ro = np.full((3, NA_QROWS, 2 * NA_PAIRS), NA_RO_NONE, np.int64)
    slot = np.zeros((3, NA_QROWS, NA_PAIRS), np.int64)
    for t in range(3):
        for a in range(NA_QROWS):
            qr = _NA_CLASS_R0[t] + a
            rs = min(max(qr - NA_KH // 2, 0), ROWS - NA_KH)
            for b in range(NA_KROWS):
                kr = _NA_CLASS_K0[t] + b
                if rs <= kr < rs + NA_KH:
                    ro[t, a, b] = kr - qr + (NA_KH - 1)
            for j in range(NA_PAIRS):
                slot[t, a, j] = 2 * j - a + (_NA_CLASS_K0[t] - _NA_CLASS_R0[t] + NA_KH - 1) + (NA_QROWS - 1)
    assert slot.min() >= 0 and slot.max() < NA_SLOTS
    return ro, slot


def _na_build_bias(i, cls_ref, rp_ref, cm_ref, bias_scr):
    ro, _ = _na_class_tables()
    lo = _head_lanes()
    first = jnp.logical_or(i == 0, cls_ref[i] != cls_ref[jnp.maximum(i - 1, 0)])
    for t in range(3):
        @pl.when(jnp.logical_and(first, cls_ref[i] == t))
        def _():
            for hh in range(2):
                for a in range(NA_QROWS):
                    for j in range(NA_PAIRS):
                        r0, r1 = int(ro[t, a, 2 * j]), int(ro[t, a, 2 * j + 1])
                        x0 = jnp.broadcast_to(rp_ref[hh, r0:r0 + 1, :], (GRID_W, LANES))
                        x1 = jnp.broadcast_to(rp_ref[hh, r1:r1 + 1, :], (GRID_W, LANES))
                        blk = jnp.where(lo, pltpu.roll(x0, GRID_W + 1, 1, stride=1, stride_axis=0),
                                        pltpu.roll(x1, 1, 1, stride=1, stride_axis=0))
                        bias_scr[hh, a * GRID_W:(a + 1) * GRID_W, j * LANES:(j + 1) * LANES] = blk + cm_ref[...]
    return first


def _attn_fwd(qkv, qc0, kc0, vc0, npairs, table, kstart, cls, nk, *, name, na=None, qb=Q_BLOCK):
    s = qkv.shape[0]

    def body(ks_ref, cls_ref, q_ref, k_ref, v_ref, b_ref, *rest):
        if na:
            cm_ref, o_ref, lse_ref, bias_scr = rest
        else:
            o_ref, lse_ref = rest
        i = pl.program_id(1)
        if na:
            _na_build_bias(i, cls_ref, b_ref, cm_ref, bias_scr)
        ks = pl.multiple_of(ks_ref[i], 64)
        q2 = q_ref[...]
        k2 = k_ref[pl.ds(ks, nk), :]
        v2 = v_ref[pl.ds(ks, nk), :]
        lo = _head_lanes()
        outs, lses = [], []
        for h in range(2):
            qm = jnp.where(lo if h == 0 else jnp.logical_not(lo), q2, jnp.zeros_like(q2))
            sc = _dot_nt(qm, k2) * SCALE + (bias_scr[h, :, :nk] if na else b_ref[0, 0])
            m = jnp.max(sc, axis=1, keepdims=True)
            p = jnp.exp(sc - m)
            l = jnp.sum(p, axis=1, keepdims=True)
            pv = _dot(p.astype(BF16), v2)
            outs.append(pv / l)
            lses.append(m + jnp.log(l))
        o_ref[...] = jnp.where(lo, outs[0], outs[1])
        lse_ref[...] = jnp.where(lo, lses[0], lses[1])

    w = npairs * LANES
    in_specs = [
        pl.BlockSpec((qb, LANES), lambda p, i, ks, cl: (i, qc0 + p)),
        pl.BlockSpec((s, LANES), lambda p, i, ks, cl: (0, kc0 + p)),
        pl.BlockSpec((s, LANES), lambda p, i, ks, cl: (0, vc0 + p)),
    ]
    if na:
        in_specs += _na_bias_specs()
        args, scratch = (kstart, cls, qkv, qkv, qkv, *na), [pltpu.VMEM((2, Q_BLOCK, NA_W), F32)]
    else:
        in_specs.append(pl.BlockSpec((1, 1, qb, nk), lambda p, i, ks, cl: (cl[i], 0, 0, 0)))
        args, scratch = (kstart, cls, qkv, qkv, qkv, table), []
    grid_spec = pltpu.PrefetchScalarGridSpec(
        num_scalar_prefetch=2, grid=(npairs, s // qb), in_specs=in_specs,
        out_specs=[pl.BlockSpec((qb, LANES), lambda p, i, ks, cl: (i, p)),
                   pl.BlockSpec((qb, LANES), lambda p, i, ks, cl: (i, p))],
        scratch_shapes=scratch,
    )
    return _pcall(body, name=name, grid_spec=grid_spec,
                  out_shape=[jax.ShapeDtypeStruct((s, w), F32), jax.ShapeDtypeStruct((s, w), F32)],
                  compiler_params=pltpu.CompilerParams(dimension_semantics=("parallel", "arbitrary")),
                  )(*args)


def _na_bias_specs():
    return [pl.BlockSpec((2, 16, LANES), lambda p, i, ks, cl: (p, 0, 0)),
            pl.BlockSpec((GRID_W, LANES), lambda p, i, ks, cl: (0, 0))]


def _attn_bwd(qkv, qc0, kc0, vc0, npairs, table, kstart, cls, nk, do, o, lse, *, name, dlse=None, na=None,
              qb=Q_BLOCK, order=None):
    s = qkv.shape[0]
    has_dlse = dlse is not None
    _, slot = _na_class_tables()

    def body(ks_ref, cls_ref, q_ref, k_ref, v_ref, b_ref, *rest):
        if na:
            cm_ref, rest = rest[0], rest[1:]
        do_ref, o_ref, lse_ref, rest = rest[0], rest[1], rest[2], rest[3:]
        if has_dlse:
            dlse_ref, rest = rest[0], rest[1:]
        if order is not None:
            rest = rest[1:]
        dq_ref, dk_ref, dv_ref = rest[0], rest[1], rest[2]
        if na:
            bank_ref, bias_scr, dbias_scr, bank_scr = rest[3:]
        i = pl.program_id(1)
        if na:
            first = _na_build_bias(i, cls_ref, b_ref, cm_ref, bias_scr)

        @pl.when(i == 0)
        def _():
            dk_ref[...] = jnp.zeros_like(dk_ref)
            dv_ref[...] = jnp.zeros_like(dv_ref)
            if na:
                dbias_scr[...] = jnp.zeros_like(dbias_scr)
                bank_scr[...] = jnp.zeros_like(bank_scr)

        ks = pl.multiple_of(ks_ref[i], 64)
        q2 = q_ref[...]
        k2 = k_ref[pl.ds(ks, nk), :]
        v2 = v_ref[pl.ds(ks, nk), :]
        do2 = do_ref[...]
        lse2 = lse_ref[...]
        doo = do2 * o_ref[...]
        do2b = do2.astype(BF16)
        lo = _head_lanes()
        lane = lax.broadcasted_iota(jnp.int32, (1, LANES), 1)
        dqs, dks, dvs = [], [], []
        for h in range(2):
            mh = lo if h == 0 else jnp.logical_not(lo)
            qm = jnp.where(mh, q2, jnp.zeros_like(q2))
            sc = _dot_nt(qm, k2) * SCALE + (bias_scr[h, :, :nk] if na else b_ref[0, 0])
            lse_h = jnp.max(jnp.where(mh, lse2, NEG), axis=1, keepdims=True)
            p = jnp.exp(sc - lse_h)
            delta = jnp.sum(jnp.where(mh, doo, 0.0), axis=1, keepdims=True)
            dom = jnp.where(mh, do2b, jnp.zeros_like(do2b))
            dp = _dot_nt(dom, v2)
            t = dp - delta
            if has_dlse:
                t = t + jnp.sum(jnp.where(lane == h * HEAD_DIM, dlse_ref[...], 0.0), axis=1, keepdims=True)
            ds = p * t
            if na:
                @pl.when(first)
                def _():
                    dbias_scr[h, :, :nk] = ds

                @pl.when(jnp.logical_not(first))
                def _():
                    dbias_scr[h, :, :nk] += ds
            dsb = ds.astype(BF16)
            dqs.append(_dot(dsb, k2))
            dks.append(lax.dot_general(dsb, q2, _TN, preferred_element_type=F32))
            dvs.append(lax.dot_general(p.astype(BF16), do2b, _TN, preferred_element_type=F32))
        dq_ref[...] = jnp.where(lo, dqs[0], dqs[1]) * SCALE
        dk_ref[pl.ds(ks, nk), :] += jnp.where(lo, dks[0], dks[1]) * SCALE
        dv_ref[pl.ds(ks, nk), :] += jnp.where(lo, dvs[0], dvs[1])
        if na:
            last = jnp.logical_or(i == N_QBLK - 1, cls_ref[i] != cls_ref[jnp.minimum(i + 1, N_QBLK - 1)])
            for t in range(3):
                @pl.when(jnp.logical_and(last, cls_ref[i] == t))
                def _():
                    for hh in range(2):
                        for a in range(NA_QROWS):
                            for j in range(NA_PAIRS):
                                bank_scr[hh, int(slot[t, a, j])] += dbias_scr[
                                    hh, a * GRID_W:(a + 1) * GRID_W, j * LANES:(j + 1) * LANES]

            @pl.when(i == N_QBLK - 1)
            def _():
                bank_ref[...] = bank_scr[...]

    w = npairs * LANES
    blk = lambda: pl.BlockSpec((qb, LANES), lambda p, i, ks, cl: (i, p))
    full = lambda: pl.BlockSpec((s, LANES), lambda p, i, ks, cl: (0, p))
    in_specs = [
        pl.BlockSpec((qb, LANES), lambda p, i, ks, cl: (i, qc0 + p)),
        pl.BlockSpec((s, LANES), lambda p, i, ks, cl: (0, kc0 + p)),
        pl.BlockSpec((s, LANES), lambda p, i, ks, cl: (0, vc0 + p)),
    ]
    if na:
        in_specs += _na_bias_specs()
        args = [kstart, cls, qkv, qkv, qkv, *na]
    else:
        in_specs.append(pl.BlockSpec((1, 1, qb, nk), lambda p, i, ks, cl: (cl[i], 0, 0, 0)))
        args = [kstart, cls, qkv, qkv, qkv, table]
    in_specs += [blk(), blk(), blk()]
    args += [do, o, lse]
    if has_dlse:
        in_specs.append(blk())
        args.append(dlse)
    out_specs = [blk(), full(), full()]
    out_shape = [jax.ShapeDtypeStruct((s, w), F32)] * 3
    scratch = []
    if na:
        bank_shape = (2, NA_SLOTS, GRID_W, LANES)
        out_specs.append(pl.BlockSpec(bank_shape, lambda p, i, ks, cl: (p, 0, 0, 0)))
        out_shape.append(jax.ShapeDtypeStruct((2 * npairs,) + bank_shape[1:], F32))
        scratch = [pltpu.VMEM((2, Q_BLOCK, NA_W), F32), pltpu.VMEM((2, Q_BLOCK, NA_W), F32), pltpu.VMEM(bank_shape, F32)]
    if order is not None:
        in_specs.append(pl.BlockSpec(order.shape, lambda p, i, ks, cl: (0, 0)))
        args.append(order)
    grid_spec = pltpu.PrefetchScalarGridSpec(num_scalar_prefetch=2, grid=(npairs, s // qb), in_specs=in_specs,
                                             out_specs=out_specs, scratch_shapes=scratch)
    return _pcall(body, name=name, grid_spec=grid_spec, out_shape=out_shape,
                  compiler_params=pltpu.CompilerParams(dimension_semantics=("arbitrary", "arbitrary")))(*args)


_NA_CLASS_R0 = (0, NA_QROWS, ROWS - NA_QROWS)
_NA_CLASS_K0 = (0, 0, ROWS - NA_KROWS)
_RPB_RO = 2 * NA_KH - 1
_RPB_CO = 2 * NA_KW - 1
_BANK_ROWS = 48


def _na_constants():
    col = np.arange(GRID_W)
    cs = np.clip(col - NA_KW // 2, 0, GRID_W - NA_KW)
    vcol = (col[None, :] >= cs[:, None]) & (col[None, :] < cs[:, None] + NA_KW)
    colmask = np.where(np.concatenate([vcol, vcol], axis=1), 0.0, NEG).astype(np.float32)
    co = col[None, :] - col[:, None] + (NA_KW - 1)
    oh_col = np.zeros((GRID_W * GRID_W, LANES), np.float32)
    for qc in range(GRID_W):
        for kc in range(GRID_W):
            if vcol[qc, kc]:
                oh_col[qc * GRID_W + kc, co[qc, kc]] = 1.0
    ks = np.clip(np.arange(N_QBLK) * NA_QROWS - NA_KH // 2, 0, ROWS - NA_KROWS) * GRID_W
    cls = np.ones(N_QBLK, np.int32)
    cls[0], cls[-1] = 0, 2
    return colmask, oh_col, ks.astype(np.int32), cls


def _bank_reduce(bank, oh_col, *, name):
    def body(d_ref, ohc_ref, o_ref):
        o_ref[0] = jnp.dot(d_ref[0], ohc_ref[...], preferred_element_type=F32, precision=lax.Precision.HIGHEST)

    return _pcall(
        body, name=name, grid=(NA_HEADS,),
        in_specs=[pl.BlockSpec((1, _BANK_ROWS, GRID_W * GRID_W), lambda h: (h, 0, 0)),
                  pl.BlockSpec((GRID_W * GRID_W, LANES), lambda h: (0, 0))],
        out_specs=pl.BlockSpec((1, _BANK_ROWS, LANES), lambda h: (h, 0, 0)),
        out_shape=jax.ShapeDtypeStruct((NA_HEADS, _BANK_ROWS, LANES), F32), compiler_params=_PAR,
    )(bank, oh_col)


def _dil_constants(dilation):
    seg = SEQ // dilation
    nb = seg // DIL_QB
    nk = min(DIL_QB + 2 * DIL_HALF, seg)
    starts = [min(max(blk * DIL_QB - DIL_HALF, 0), seg - nk) for blk in range(nb)]
    shifts = sorted({w0 - blk * DIL_QB for blk, w0 in enumerate(starts)}, reverse=True)
    qi = np.arange(DIL_QB)[:, None]
    ki = np.arange(nk)[None, :]
    mask = np.stack([np.where(np.abs(ki + sh - qi) <= DIL_HALF, 0.0, NEG) for sh in shifts]).astype(np.float32)
    ks, cls = [], []
    for i in range(SEQ // DIL_QB):
        sub, blk = divmod(i, nb)
        cls.append(shifts.index(starts[blk] - blk * DIL_QB))
        ks.append(sub * seg + starts[blk])
    return mask.reshape(len(shifts), 1, DIL_QB, nk), np.asarray(ks, np.int32), np.asarray(cls, np.int32), nk


_VM = pl.BlockSpec(memory_space=pltpu.VMEM)


def _ada_fwd(c_all, w, b, *, name):
    def body(c_ref, w_ref, b_ref, o_ref):
        cv = c_ref[...]
        o_ref[...] = jnp.dot(cv * _sigmoid(cv), w_ref[...], preferred_element_type=F32,
                             precision=lax.Precision.HIGHEST) + b_ref[...]

    return _pcall(body, name=name, in_specs=[_VM, _VM, _VM], out_specs=_VM,
                  out_shape=jax.ShapeDtypeStruct((c_all.shape[0], w.shape[1]), F32))(c_all, w, b)


def _ada_bwd(c_all_t, dmod, *, name):
    def body(c_ref, d_ref, o_ref):
        cv = c_ref[...]
        o_ref[...] = jnp.dot(cv * _sigmoid(cv), d_ref[...], preferred_element_type=F32,
                             precision=lax.Precision.HIGHEST)

    return _pcall(body, name=name, in_specs=[_VM, _VM], out_specs=_VM,
                  out_shape=jax.ShapeDtypeStruct((c_all_t.shape[0], dmod.shape[1]), F32))(c_all_t, dmod)


def _row_sum(t, *, name):
    def body(t_ref, o_ref):
        o_ref[...] = jnp.sum(t_ref[...], axis=0, keepdims=True)

    return _pcall(body, name=name, in_specs=[_VM], out_specs=_VM,
                  out_shape=jax.ShapeDtypeStruct((1, t.shape[1]), F32))(t)


def _row_tile(rows):
    tr = rows
    for cand in range(8, 513, 8):
        if rows % cand == 0:
            tr = cand
    return tr


def _adamw_math(wv, gv, mv, vv):
    nm = ADAM_B1 * mv + (1.0 - ADAM_B1) * gv
    nv = ADAM_B2 * vv + (1.0 - ADAM_B2) * (gv * gv)
    m_hat = nm / (1.0 - ADAM_B1 ** ADAM_STEP)
    v_hat = nv / (1.0 - ADAM_B2 ** ADAM_STEP)
    return -ADAM_LR * (m_hat / (jnp.sqrt(v_hat) + ADAM_EPS) + ADAM_WD * wv), nm, nv


def _adamw(w, g, m, v, *, name):
    rows, cols = w.shape
    tr = _row_tile(rows)

    def body(w_ref, g_ref, m_ref, v_ref, d_ref, nm_ref, nv_ref):
        d_ref[...], nm_ref[...], nv_ref[...] = _adamw_math(w_ref[...], g_ref[...], m_ref[...], v_ref[...])

    spec = pl.BlockSpec((tr, cols), lambda i: (i, 0))
    return _pcall(body, name=name, grid=(rows // tr,), in_specs=[spec] * 4, out_specs=[spec] * 3,
                  out_shape=[jax.ShapeDtypeStruct((rows, cols), F32)] * 3, compiler_params=_PAR)(w, g, m, v)


def _adamw_halves(w, g_mine, g_other, m, v, c_arr, *, name):
    rows, cols = w.shape
    hr = rows // 2
    tr = _row_tile(hr)
    nt = hr // tr

    def body(c_ref, w_ref, t_ref, o_ref, m_ref, v_ref, g_ref, d_ref, nm_ref, nv_ref):
        gv = jnp.where(pl.program_id(0) == c_ref[0], t_ref[...], o_ref[...])
        g_ref[...] = gv
        d_ref[...], nm_ref[...], nv_ref[...] = _adamw_math(w_ref[...], gv, m_ref[...], v_ref[...])

    full = pl.BlockSpec((tr, cols), lambda h, i, c: (h * nt + i, 0))
    half = pl.BlockSpec((tr, cols), lambda h, i, c: (i, 0))
    grid_spec = pltpu.PrefetchScalarGridSpec(num_scalar_prefetch=1, grid=(2, nt),
                                             in_specs=[full, half, half, full, full], out_specs=[full] * 4)
    return _pcall(body, name=name, grid_spec=grid_spec, out_shape=[jax.ShapeDtypeStruct((rows, cols), F32)] * 4,
                  compiler_params=pltpu.CompilerParams(dimension_semantics=("parallel", "parallel")),
                  )(c_arr, w, g_mine, g_other, m, v)


_MESH = pl.DeviceIdType.MESH
_ANY = pl.BlockSpec(memory_space=pl.ANY)
_CHIP_FLIPS = ((1, 0), (0, 1), (1, 1))


def _pos():
    return lax.axis_index("x"), lax.axis_index("y"), lax.axis_index("c")


def _flip(v, f):
    return 1 - v if f else v


def _sem_pairs(n):
    return [pltpu.SemaphoreType.DMA((n,)), pltpu.SemaphoreType.DMA((n,))]


def _small_allgather(blk, *, name):
    m_per, n = blk.shape

    def body(x_ref, out_ref, send_sems, recv_sems, local_sem):
        x, y, c = _pos()
        me, sibling = (x, y, c), (x, y, 1 - c)
        chips = [(_flip(x, fx), _flip(y, fy)) for fx, fy in _CHIP_FLIPS]

        def rows(px, py, pc):
            return out_ref.at[pl.ds((4 * px + 2 * py + pc) * m_per, m_per), :]

        def copy(k, block, to, src=None):
            return pltpu.make_async_remote_copy(
                src_ref=rows(*block) if src is None else src, dst_ref=rows(*block),
                send_sem=send_sems.at[k], recv_sem=recv_sems.at[k], device_id=to, device_id_type=_MESH)

        mine = pltpu.make_async_copy(x_ref, rows(*me), local_sem)
        mine.start()
        first = [copy(0, me, sibling, src=x_ref)]
        first += [copy(1 + j, me, (*chip, c), src=x_ref) for j, chip in enumerate(chips)]
        for cp in first:
            cp.start()
        passed = [copy(4 + j, (*chip, c), sibling) for j, chip in enumerate(chips)]
        for j, chip in enumerate(chips):
            copy(1 + j, (*chip, c), me).wait_recv()
            passed[j].start()
        copy(0, sibling, me).wait_recv()
        for j, chip in enumerate(chips):
            copy(4 + j, (*chip, 1 - c), me).wait_recv()
        for cp in first + passed:
            cp.wait_send()
        mine.wait()

    return _pcall(
        body, name=name, out_shape=jax.ShapeDtypeStruct((N_DEV * m_per, n), blk.dtype),
        in_specs=[_VM], out_specs=_VM,
        scratch_shapes=_sem_pairs(7) + [pltpu.SemaphoreType.DMA],
    )(blk)


_HBM = pl.BlockSpec(memory_space=pltpu.HBM)
_SEM = pl.BlockSpec(memory_space=pltpu.SEMAPHORE)
_EFFECT = pltpu.SideEffectType.DATAFLOW_SIDE_EFFECTING


def _split_start(srcs, lands, plan, ncopies, after, *, name, alias_sources=True):
    ns, nl = len(srcs), len(lands)
    thru = (*srcs, *lands) if alias_sources else tuple(lands)

    def body(*refs):
        src_refs, land_refs = refs[:ns], refs[ns:ns + nl]
        send_sems, recv_sems = refs[ns + nl + 1], refs[ns + nl + 2]
        token = refs[-1]
        x, y, c = _pos()
        for k, (src, dst, to, _) in enumerate(plan(x, y, c, src_refs, land_refs)):
            pltpu.make_async_remote_copy(src_ref=src, dst_ref=dst, send_sem=send_sems.at[k], recv_sem=recv_sems.at[k],
                                         device_id=to, device_id_type=_MESH).start()
        token[...] = jnp.zeros_like(token)

    hbm = lambda a: pltpu.HBM(a.shape, a.dtype)
    out = _pcall(
        body, name=name,
        out_shape=(pltpu.SemaphoreType.DMA((ncopies,)), pltpu.SemaphoreType.DMA((ncopies,)),
                   *[hbm(a) for a in thru], jax.ShapeDtypeStruct((8, LANES), F32)),
        in_specs=[_HBM] * (ns + nl) + [_ANY], out_specs=(_SEM, _SEM, *[_HBM] * len(thru), _VM),
        input_output_aliases={ns + nl - len(thru) + i: 2 + i for i in range(len(thru))},
        compiler_params=pltpu.CompilerParams(has_side_effects=_EFFECT),
    )(*[pltpu.with_memory_space_constraint(a, pltpu.HBM) for a in (*srcs, *lands)], after)
    if alias_sources:
        return out[0], out[1], list(out[2:2 + ns]), list(out[2 + ns:2 + ns + nl]), out[-1]
    return out[0], out[1], list(srcs), list(out[2:2 + nl]), out[-1]


def _split_wait(send_sems, recv_sems, srcs, lands, plan, after, *, name, with_sources=False, alias_sources=True):
    ns, nl = len(srcs), len(lands)
    thru = (*srcs, *lands) if alias_sources else tuple(lands)

    def body(*refs):
        src_refs, land_refs = refs[:ns], refs[ns:ns + nl]
        send_sems, recv_sems = refs[ns + nl], refs[ns + nl + 1]
        x, y, c = _pos()
        for k, (src, _, _, mine) in enumerate(plan(x, y, c, src_refs, land_refs)):
            cp = pltpu.make_async_remote_copy(src_ref=src, dst_ref=mine, send_sem=send_sems.at[k],
                                              recv_sem=recv_sems.at[k], device_id=(x, y, c), device_id_type=_MESH)
            cp.wait_send()
            cp.wait_recv()

    hbm = lambda a: pltpu.HBM(a.shape, a.dtype)
    out = _pcall(
        body, name=name, out_shape=tuple(hbm(a) for a in thru),
        in_specs=[_HBM] * (ns + nl) + [_SEM, _SEM, _ANY], out_specs=tuple([_HBM] * len(thru)),
        input_output_aliases={ns + nl - len(thru) + i: i for i in range(len(thru))},
        compiler_params=pltpu.CompilerParams(has_side_effects=_EFFECT),
    )(*srcs, *lands, send_sems, recv_sems, after)
    got_srcs, got_lands = (list(out[:ns]), list(out[ns:])) if alias_sources else (list(srcs), list(out))
    return (got_srcs, got_lands) if with_sources else got_lands


def _ag_plan(nw):
    def plan(x, y, c, sh_refs, full_refs):
        j = 2 * x + y
        out = []
        for wi in range(nw):
            for fx, fy in _CHIP_FLIPS:
                px, py = _flip(x, fx), _flip(y, fy)
                out.append((sh_refs[wi].at[c], full_refs[wi].at[j, c], (px, py, c), full_refs[wi].at[2 * px + py, c]))
            out.append((sh_refs[wi], full_refs[wi].at[j], (x, y, 1 - c), full_refs[wi].at[j]))
        return out
    return plan


def _ag_plan_direct(nw):
    def plan(x, y, c, sh_refs, full_refs):
        j = 2 * x + y
        out = []
        for wi in range(nw):
            for fx, fy in _CHIP_FLIPS:
                px, py = _flip(x, fx), _flip(y, fy)
                for rel in (0, 1):
                    t = _flip(c, rel)
                    out.append((sh_refs[wi].at[c], full_refs[wi].at[j, c], (px, py, t), full_refs[wi].at[2 * px + py, t]))
            out.append((sh_refs[wi], full_refs[wi].at[j], (x, y, 1 - c), full_refs[wi].at[j]))
        return out
    return plan


def _ag_pass(fulls, *, name):
    nw = len(fulls)

    def body(*refs):
        in_refs, out_refs = refs[:nw], refs[nw:2 * nw]
        send_sems, recv_sems = refs[2 * nw:]
        x, y, c = _pos()
        cps = []
        for wi in range(nw):
            for k, (fx, fy) in enumerate(_CHIP_FLIPS):
                jp = 2 * _flip(x, fx) + _flip(y, fy)
                sems = dict(send_sem=send_sems.at[3 * wi + k], recv_sem=recv_sems.at[3 * wi + k], device_id_type=_MESH)
                send = pltpu.make_async_remote_copy(src_ref=in_refs[wi].at[jp, c], dst_ref=out_refs[wi].at[jp, c],
                                                    device_id=(x, y, 1 - c), **sems)
                recv = pltpu.make_async_remote_copy(src_ref=in_refs[wi].at[jp, c], dst_ref=out_refs[wi].at[jp, 1 - c],
                                                    device_id=(x, y, c), **sems)
                cps.append((send, recv))
        for send, _ in cps:
            send.start()
        for send, recv in cps:
            send.wait_send()
            recv.wait_recv()

    return _pcall(body, name=name, out_shape=[jax.ShapeDtypeStruct(f.shape, f.dtype) for f in fulls],
                  in_specs=[_ANY] * nw, out_specs=[_ANY] * nw, input_output_aliases={i: i for i in range(nw)},
                  scratch_shapes=_sem_pairs(3 * nw))(*fulls)


def _sib_plan(nw):
    def plan(x, y, c, g_refs, ra_refs):
        return [(g_refs[wi].at[k, 1 - c], ra_refs[wi].at[k], (x, y, 1 - c), ra_refs[wi].at[k])
                for wi in range(nw) for k in range(N_CHIP)]
    return plan


def _rs_plan(nw):
    def plan(x, y, c, s_refs, rb_refs):
        out = []
        for wi in range(nw):
            for k, (fx, fy) in enumerate(_CHIP_FLIPS):
                px, py = _flip(x, fx), _flip(y, fy)
                out.append((s_refs[wi].at[2 * px + py], rb_refs[wi].at[k], (px, py, c), rb_refs[wi].at[k]))
        return out
    return plan


def _sibling_swap(ts, *, name):
    nw = len(ts)

    def body(*refs):
        t_refs, out_refs = refs[:nw], refs[nw:2 * nw]
        send_sems, recv_sems = refs[2 * nw:]
        x, y, c = _pos()
        cps = [pltpu.make_async_remote_copy(src_ref=t_refs[wi], dst_ref=out_refs[wi], send_sem=send_sems.at[wi],
                                            recv_sem=recv_sems.at[wi], device_id=(x, y, 1 - c), device_id_type=_MESH)
               for wi in range(nw)]
        for cp in cps:
            cp.start()
        for cp in cps:
            cp.wait()

    return _pcall(body, name=name, out_shape=[jax.ShapeDtypeStruct(t.shape, t.dtype) for t in ts],
                  in_specs=[_ANY] * nw, out_specs=[_ANY] * nw, scratch_shapes=_sem_pairs(nw))(*ts)


def _rs_add(g, ra, c_arr, *, name):
    n, _, r, w = g.shape

    def body(c_ref, g_ref, ra_ref, s_ref, sb_ref):
        t = g_ref[...] + ra_ref[...]
        s_ref[...] = t
        sb_ref[...] = t.astype(BF16)

    grid_spec = pltpu.PrefetchScalarGridSpec(
        num_scalar_prefetch=1, grid=(n,),
        in_specs=[pl.BlockSpec((None, None, r, w), lambda k, c: (k, c[0], 0, 0)),
                  pl.BlockSpec((None, r, w), lambda k, c: (k, 0, 0))],
        out_specs=[pl.BlockSpec((None, r, w), lambda k, c: (k, 0, 0))] * 2)
    return _pcall(body, name=name, grid_spec=grid_spec,
                  out_shape=[jax.ShapeDtypeStruct((n, r, w), F32), jax.ShapeDtypeStruct((n, r, w), BF16)],
                  compiler_params=_PAR)(c_arr, g, ra)


def _rs_final(s, rb, j_arr, *, name):
    _, r, w = s.shape

    def body(j_ref, s_ref, rb_ref, t_ref):
        t_ref[...] = ((s_ref[...] + rb_ref[0].astype(F32)) + rb_ref[1].astype(F32)) + rb_ref[2].astype(F32)

    grid_spec = pltpu.PrefetchScalarGridSpec(
        num_scalar_prefetch=1, grid=(1,),
        in_specs=[pl.BlockSpec((None, r, w), lambda i, j: (j[0], 0, 0)),
                  pl.BlockSpec((3, r, w), lambda i, j: (0, 0, 0))],
        out_specs=pl.BlockSpec((r, w), lambda i, j: (0, 0)))
    return _pcall(body, name=name, grid_spec=grid_spec, out_shape=jax.ShapeDtypeStruct((r, w), F32),
                  compiler_params=_ARB)(j_arr, s, rb)


def _tile2(g):
    return jnp.concatenate([g, g], axis=1)


_BIG = ("w_in", "w_ffn_in", "w_ffn_out", "w_o", "w_proj_a", "w_proj_b")
_BIG_SHARD = {"w_in": (1024, 1472), "w_ffn_in": (1024, 1408), "w_ffn_out": (704, 1024), "w_o": (256, 1024),
              "w_proj_a": (512, 256), "w_proj_b": (256, 256)}


def _device_step(x2, tgt, mod, first_weights, late_weights, early_grads, mid_grads, g_norm1, g_norm2, b_gate, g_qa, g_ka,
                 g_qb, g_kb, rpb):
    d = D_MODEL
    sh1, sc1, gt1, sh2, sc2, gt2 = [mod[:, k * d:(k + 1) * d] for k in range(6)]

    colmask, oh_col, na_ks, na_cls = _na_constants()
    rp = jnp.pad(rpb, ((0, 0), (0, 16 - _RPB_RO), (RP_LANE0, LANES - RP_LANE0 - _RPB_CO)), constant_values=NEG)
    na = (rp, jnp.asarray(colmask))
    na_ks, na_cls = jnp.asarray(na_ks), jnp.asarray(na_cls)
    dil = [_dil_constants(dd) for _, dd in DIL_CONFIGS]
    gains = jnp.concatenate([_tile2(g_qa), _tile2(g_ka), _tile2(g_qb), _tile2(g_kb)], axis=0)
    cos_t, sa_t, sb_t = _rope_tables()

    wts = first_weights(cos_t)
    h1, qkvn, qk_pre, gates, *qkv_dil = _pre_attn_fwd(x2, cos_t, sa_t, sb_t, g_norm1, sc1, sh1, wts["w_qkv"],
                                                      wts["w_gates"], gains, name="pre_attn_fwd")
    o_a, lse_a = _attn_fwd(qkvn, 0, 4, 8, 4, None, na_ks, na_cls, NA_NK, name="attn_a_fwd", na=na)
    arrs, o_g, l_g = [], [], []
    res = lambda t, dd: t if dd == 1 else (t, dd)
    for g, (_, dd) in enumerate(DIL_CONFIGS):
        tab_g, ks_g, cls_g, nk_g = jnp.asarray(dil[g][0]), jnp.asarray(dil[g][1]), jnp.asarray(dil[g][2]), dil[g][3]
        arr, cb = (qkvn, (12, 18, 24)) if dd == 1 else (qkv_dil.pop(0), (0, 2, 4))
        op, lp = _attn_fwd(arr, cb[0], cb[1], cb[2], 2, tab_g, ks_g, cls_g, nk_g, name=f"attn_d{g}_fwd", qb=DIL_QB)
        arrs.append((arr, cb, tab_g, ks_g, cls_g, nk_g))
        o_g.append(res(op, dd))
        l_g.append(res(lp, dd))
    wts = dict(wts, **late_weights(o_a))
    mo, x1, h2 = _post_attn_fwd(o_a, o_g, l_g, gates, x2, wts["w_pa"], wts["w_pb"], wts["w_o"], b_gate,
                                gt1, g_norm2, sc2, sh2, name="post_attn_fwd")
    act, ff = _ffn_fwd(h2, wts["w_ffn_in"], name="ffn_fwd")

    dy, dffo, dff, dgt2, loss_v = _ffn_mid(act, ff, x1, tgt, wts["w_ffn_out"], gt2, name="ffn_mid")
    grads = {}
    g_ffn_out = _wgrad(act, dffo, name="wg_ffn_out", tm=D_FF // 2, tn=d, tk=2048)
    grads["w_ffn_out"] = g_ffn_out.reshape(N_CHIP, D_FF // N_CHIP, d)
    grads["w_ffn_in"] = _wgrad(h2, dff, name="wg_ffn_in", tm=512, tn=2 * FF_CHIP, tk=2048, chips=N_CHIP)
    dx1, dmo, sums2 = _ffn_in_bwd(dff, x1, dy, mo, wts["w_ffn_in"], g_norm2, sc2, gt1, name="ffn_in_bwd")
    pab = _post_attn_bwd(dmo, gates, o_a, o_g, l_g, wts["w_pa"], wts["w_pb"], wts["w_o"], b_gate, name="post_attn_bwd")
    dgates, do_a = pab[:2]
    do_g, dl_g, (dbg, g_pa, g_pb, g_o) = pab[2:5], pab[5:8], pab[8:]
    grads["w_o"] = g_o.reshape(N_CHIP, d // N_CHIP, d)
    grads["w_proj_a"] = g_pa.reshape(WA, N_CHIP, d // N_CHIP).transpose(1, 0, 2)
    grads["w_proj_b"] = g_pb.reshape(WB_OUT, N_CHIP, d // N_CHIP).transpose(1, 0, 2)
    order = early_grads(grads)
    dqs, dks, dvs = [], [], []
    for g, (_, dd) in enumerate(DIL_CONFIGS):
        arr, cb, tab_g, ks_g, cls_g, nk_g = arrs[g]
        plain = lambda t: t[0] if isinstance(t, tuple) else t
        dq, dk, dv = _attn_bwd(arr, cb[0], cb[1], cb[2], 2, tab_g, ks_g, cls_g, nk_g, do_g[g], plain(o_g[g]),
                               plain(l_g[g]), name=f"attn_d{g}_bwd", dlse=dl_g[g], qb=DIL_QB, order=order)
        dqs.append(res(dq, dd))
        dks.append(res(dk, dd))
        dvs.append(res(dv, dd))
    order = mid_grads(dv)
    dqa, dka, dva, bank = _attn_bwd(qkvn, 0, 4, 8, 4, None, na_ks, na_cls, NA_NK, do_a, o_a, lse_a,
                                    name="attn_a_bwd", na=na, order=order)
    dqkv, grad_x, dgains, sums1 = _pre_attn_bwd(qk_pre, [dqa, dka, dva] + dqs + dks + dvs, dgates, x2, dx1, cos_t, sa_t,
                                                sb_t, wts["w_qkv"], wts["w_gates"], gains, g_norm1, sc1,
                                                name="pre_attn_bwd")
    g_qkv = _wgrad(h1, dqkv, name="wg_qkv", tm=d, tn=W_QKV // 2, tk=2048)
    g_gates = _wgrad(h1, dgates, name="wg_gates", tm=d, tn=W_GATES // 2, tk=2048)
    nc, cut = _BIG_SHARD["w_in"][1], 3 * _BIG_SHARD["w_in"][1] - W_QKV
    grads["w_in"] = jnp.stack([g_qkv[:, :nc], g_qkv[:, nc:2 * nc],
                               jnp.concatenate([g_qkv[:, 2 * nc:], g_gates[:, :cut]], axis=1), g_gates[:, cut:]])

    bank = bank.reshape(NA_HEADS, NA_SLOTS, GRID_W, 2, GRID_W).transpose(0, 1, 3, 2, 4)
    bank = jnp.pad(bank.reshape(NA_HEADS, 2 * NA_SLOTS, GRID_W * GRID_W), ((0, 0), (0, _BANK_ROWS - 2 * NA_SLOTS), (0, 0)))
    g2 = _bank_reduce(bank, jnp.asarray(oh_col), name="rpb_reduce")[:, :2 * NA_SLOTS].reshape(NA_HEADS, NA_SLOTS, 2, LANES)
    g_rpb = g2[:, 3:3 + _RPB_RO, 0, :_RPB_CO] + g2[:, 2:2 + _RPB_RO, 1, :_RPB_CO]

    dmod = jnp.concatenate([sums1[0:1], sums1[1:2], sums2[3:4], sums2[0:1], sums2[1:2], dgt2], axis=1)
    small = dict(g_norm1=sums1[2:3], g_norm2=sums2[2:3], b_gate=dbg, g_qa=dgains[0:1, :HEAD_DIM],
                 g_ka=dgains[1:2, :HEAD_DIM], g_qb=dgains[2:3, :HEAD_DIM], g_kb=dgains[3:4, :HEAD_DIM], rpb=g_rpb)
    return loss_v, grad_x, grads, dmod, small


_SMALL = ("b_ada", "g_norm1", "g_norm2", "b_gate", "g_qa", "g_ka", "g_qb", "g_kb", "rpb")
_SMALL_N = {"b_ada": 6 * D_MODEL, "g_norm1": D_MODEL, "g_norm2": D_MODEL, "b_gate": 2 * D_MODEL, "g_qa": HEAD_DIM,
            "g_ka": HEAD_DIM, "g_qb": HEAD_DIM, "g_kb": HEAD_DIM, "rpb": NA_HEADS * _RPB_RO * _RPB_CO}


def _pack_small(parts):
    flat = [parts[n].reshape(1, _SMALL_N[n]) for n in _SMALL]
    used = sum(_SMALL_N.values())
    return jnp.concatenate(flat + [jnp.zeros((1, STATS_W - used), F32)], axis=1)


def _unpack_small(v, shapes):
    out, at = {}, 0
    for n in _SMALL:
        out[n] = v[:, at:at + _SMALL_N[n]].reshape(shapes[n])
        at += _SMALL_N[n]
    return out


def _join_cols(t):
    _, r, c = t.shape
    return t.transpose(1, 0, 2).reshape(r, N_CHIP * c)


def kernel(x, c, w_ada, b_ada, g_norm1, g_norm2, w_in, b_gate, g_qa, g_ka, g_qb, g_kb, rpb, w_proj_a, w_proj_b, w_o, w_ffn_in, w_ffn_out, loss_target, m_w_ada, m_b_ada, m_g_norm1, m_g_norm2, m_w_in, m_b_gate, m_g_qa, m_g_ka, m_g_qb, m_g_kb, m_rpb, m_w_proj_a, m_w_proj_b, m_w_o, m_w_ffn_in, m_w_ffn_out, v_w_ada, v_b_ada, v_g_norm1, v_g_norm2, v_w_in, v_b_gate, v_g_qa, v_g_ka, v_g_qb, v_g_kb, v_rpb, v_w_proj_a, v_w_proj_b, v_w_o, v_w_ffn_in, v_w_ffn_out):
    names = ("w_ada", "b_ada", "g_norm1", "g_norm2", "w_in", "b_gate", "g_qa", "g_ka", "g_qb", "g_kb", "rpb",
             "w_proj_a", "w_proj_b", "w_o", "w_ffn_in", "w_ffn_out")
    w = dict(zip(names, (w_ada, b_ada, g_norm1, g_norm2, w_in, b_gate, g_qa, g_ka, g_qb, g_kb, rpb, w_proj_a, w_proj_b,
                         w_o, w_ffn_in, w_ffn_out)))
    m = dict(zip(names, (m_w_ada, m_b_ada, m_g_norm1, m_g_norm2, m_w_in, m_b_gate, m_g_qa, m_g_ka, m_g_qb, m_g_kb, m_rpb,
                         m_w_proj_a, m_w_proj_b, m_w_o, m_w_ffn_in, m_w_ffn_out)))
    v = dict(zip(names, (v_w_ada, v_b_ada, v_g_norm1, v_g_norm2, v_w_in, v_b_gate, v_g_qa, v_g_ka, v_g_qb, v_g_kb, v_rpb,
                         v_w_proj_a, v_w_proj_b, v_w_o, v_w_ffn_in, v_w_ffn_out)))
    d = D_MODEL
    xi, yi, ci = _pos()
    chip = 2 * xi + yi
    me = 2 * chip + ci
    ada_cols = 6 * d // N_CHIP

    c_arr, chip_arr = ci.reshape(1).astype(jnp.int32), chip.reshape(1).astype(jnp.int32)
    first, rest = _BIG[:1], _BIG[1:]

    c_all = _small_allgather(c.reshape(8, d // 8), name="ag_c").reshape(N_DEV, d)
    b_sh = lax.dynamic_slice(b_ada, (0, chip * ada_cols), (1, ada_cols))
    mod_part = _ada_fwd(c_all, w_ada[0], b_sh, name="ada_fwd")
    mod_all = _small_allgather(mod_part, name="ag_mod").reshape(N_CHIP, 2, 8, ada_cols)[:, 0]
    mod = lax.dynamic_index_in_dim(mod_all, me, axis=1, keepdims=False).reshape(1, 6 * d)

    halves = {n: (2, _BIG_SHARD[n][0] // 2, _BIG_SHARD[n][1]) for n in _BIG}
    shards = {n: w[n][0].astype(BF16).reshape(halves[n]) for n in _BIG}
    land = lambda n: lax.empty((N_CHIP,) + halves[n], BF16)
    ag1 = _split_start([shards[n] for n in first], [land(n) for n in first], _ag_plan(1), 4, mod, name="ag1_start")
    ag2 = _split_start([shards[n] for n in rest], [land(n) for n in rest], _ag_plan_direct(len(rest)), 7 * len(rest),
                       ag1[4], name="ag2_start")
    rpb_after = rpb[0] + ag2[4][0, 0]

    def first_weights(after):
        after = after[:1, :1] + ag2[4][:1, :1]
        full1 = _split_wait(ag1[0], ag1[1], ag1[2], ag1[3], _ag_plan(1), after, name="ag1_wait")
        p_in = _ag_pass(full1, name="ag1_pass")[0].reshape((N_CHIP,) + _BIG_SHARD["w_in"])
        cut = W_QKV - 2 * _BIG_SHARD["w_in"][1]
        return dict(w_qkv=jnp.concatenate([p_in[0], p_in[1], p_in[2][:, :cut]], axis=1),
                    w_gates=jnp.concatenate([p_in[2][:, cut:], p_in[3]], axis=1))

    def late_weights(after):
        full2 = _split_wait(ag2[0], ag2[1], ag2[2], ag2[3], _ag_plan_direct(len(rest)), after, name="ag2_wait")
        full ={n: fu.reshape((N_CHIP,) + _BIG_SHARD[n]) for n, fu in zip(rest, full2)}
        return dict(w_pa=_join_cols(full["w_proj_a"]), w_pb=_join_cols(full["w_proj_b"]), w_o=full["w_o"].reshape(d, d),
                    w_ffn_in=_join_cols(full["w_ffn_in"]), w_ffn_out=full["w_ffn_out"].reshape(D_FF, d))

    def sib_begin(group, grads, tag):
        gps = [grads[n].reshape((N_CHIP,) + halves[n]) for n in group]
        lands = [lax.empty((N_CHIP,) + halves[n][1:], F32) for n in group]
        return _split_start(gps, lands, _sib_plan(len(group)), N_CHIP * len(group), gps[0], name=f"rs_sib_{tag}_start",
                            alias_sources=False)

    def rs_begin(group, sib, after, tag):
        gps, ras = _split_wait(sib[0], sib[1], sib[2], sib[3], _sib_plan(len(group)), after,
                               name=f"rs_sib_{tag}_wait", with_sources=True, alias_sources=False)
        sums = [_rs_add(gp, ra, c_arr, name=f"rs_add_{n}") for n, gp, ra in zip(group, gps, ras)]
        lands = [lax.empty((3,) + halves[n][1:], BF16) for n in group]
        st = _split_start([sb for _, sb in sums], lands, _rs_plan(len(group)), 3 * len(group), sums[0][0],
                          name=f"rs_{tag}_start")
        return sums, st

    def rs_end(group, begun, after, tag):
        sums, st = begun
        rbs = _split_wait(st[0], st[1], st[2], st[3], _rs_plan(len(group)), after, name=f"rs_{tag}_wait")
        return [_rs_final(sf, rb, chip_arr, name=f"rs_final_{n}") for n, (sf, _), rb in zip(group, sums, rbs)]

    begun = {}

    def early_grads(grads):
        begun["sib_rest"] = sib_begin(rest, grads, "rest")
        return begun["sib_rest"][4]

    def mid_grads(after):
        begun["rest"] = rs_begin(rest, begun["sib_rest"], after, "rest")
        return begun["rest"][1][4]

    loss_v, grad_x, grads, dmod, small = _device_step(
        x[0], loss_target[0], mod, first_weights, late_weights, early_grads, mid_grads, g_norm1, g_norm2, b_gate, g_qa,
        g_ka, g_qb, g_kb, rpb_after)
    sib_first = sib_begin(first, grads, "first")

    g, delta, new_m, new_v = {}, {}, {}, {}

    def finish(group, ts, tag):
        others = _sibling_swap(ts, name=f"rs_pair_{tag}")
        for n, t, o in zip(group, ts, others):
            gg, dl, nm, nv = _adamw_halves(w[n][0], t, o, m[n][0], v[n][0], c_arr, name=f"adamw_{n}")
            g[n], delta[n], new_m[n], new_v[n] = gg[None], dl[None], nm[None], nv[None]

    finish(rest, rs_end(rest, begun["rest"], sib_first[4], "rest"), "rest")
    done_rest = sum(new_v[n][0, :1, :1] for n in rest)
    begun["first"] = rs_begin(first, sib_first, done_rest, "first")

    stats = _pack_small(dict(b_ada=dmod, **small)) + begun["first"][1][4][0, 0]
    stats = stats.at[:, STATS_W - 1].set(loss_v[0, 0])
    rows = _small_allgather(stats.reshape(8, STATS_W // 8), name="ag_stats").reshape(N_DEV, STATS_W)
    dmod_sh = lax.dynamic_slice(rows, (0, chip * ada_cols), (8, ada_cols))
    g_ada = _ada_bwd(c_all.T, dmod_sh, name="ada_bwd")
    tot = _row_sum(rows, name="stats_sum")
    g_small = _unpack_small(tot, {n: w[n].shape for n in _SMALL})

    finish(first, rs_end(first, begun["first"], tot, "first"), "first")

    dl, nm, nv = _adamw(w_ada[0], g_ada, m_w_ada[0], v_w_ada[0], name="adamw_w_ada")
    g["w_ada"], delta["w_ada"], new_m["w_ada"], new_v["w_ada"] = g_ada[None], dl[None], nm[None], nv[None]
    shapes = {n: w[n].shape for n in _SMALL}
    dl, nm, nv = _adamw(_pack_small({n: w[n] for n in _SMALL}), tot, _pack_small({n: m[n] for n in _SMALL}),
                        _pack_small({n: v[n] for n in _SMALL}), name="adamw_small")
    delta.update(_unpack_small(dl, shapes))
    new_m.update(_unpack_small(nm, shapes))
    new_v.update(_unpack_small(nv, shapes))
    g.update(g_small)

    loss = tot[0, STATS_W - 1]
    return (loss, grad_x[None], *[g[n] for n in names], *[delta[n] for n in names], *[new_m[n] for n in names],
            *[new_v[n] for n in names])
```

```python
import numpy as np

import jax
import jax.numpy as jnp
from jax import lax
from jax.experimental import pallas as pl
from jax.experimental.pallas import tpu as pltpu

F32 = jnp.float32
BF16 = jnp.bfloat16

D_MODEL = 1024
SEQ = 8192
HEAD_DIM = 64
GRID_W = 64
ROWS = SEQ // GRID_W
NA_HEADS = 8
NA_KH = 8
NA_KW = 16
DIL_CONFIGS = ((128, 1), (512, 4), (2048, 16))
ROT_DIM = 16
ROPE_THETA = 500000.0
D_FF = 2816
EPS = 1e-6
NEG = -1e30
WA = 512
WB = 768
WB_OUT = 256
W_QKV = 3 * WA + 3 * WB
W_QK = 2 * WA + 2 * WB
W_GATES = 2 * D_MODEL
SCALE = HEAD_DIM ** -0.5

ADAM_LR = 0.001
ADAM_B1 = 0.9
ADAM_B2 = 0.999
ADAM_EPS = 1e-08
ADAM_WD = 0.01
ADAM_STEP = 10

LANES = 128
ROW_TILE = 256
ROW_TILES = {"ffn_fwd": 512, "post_attn_fwd": 512, "post_attn_bwd": 512,
             "ffn_in_bwd": 512}
Q_BLOCK = 256
NA_QROWS = Q_BLOCK // GRID_W
NA_KROWS = NA_QROWS + NA_KH - 1
NA_NK = NA_KROWS * GRID_W
NA_PAIRS = (NA_KROWS + 1) // 2
NA_W = NA_PAIRS * LANES
NA_RO_NONE = 15
NA_SLOTS = 21
RP_LANE0 = GRID_W - NA_KW
DIL_HALF = 64
DIL_QB = 512
N_QBLK = SEQ // Q_BLOCK

N_DEV = 8
N_CHIP = 4
FF_CHIP = 2 * D_FF // N_CHIP
STATS_W = 14336


def _pcall(body, *, name, **kw):
    return pl.pallas_call(body, name=name, **kw)


_NT = (((1,), (1,)), ((), ()))
_TN = (((0,), (0,)), ((), ()))
_ARB = pltpu.CompilerParams(dimension_semantics=("arbitrary",))
_PAR = pltpu.CompilerParams(dimension_semantics=("parallel",))


def _dot(a, b):
    return jnp.dot(a, b, preferred_element_type=F32)


def _dot_nt(a, b):
    return lax.dot_general(a, b, _NT, preferred_element_type=F32)


def _wgrad(a, b, *, name, tm, tn, tk=1024, chips=None):
    s, ma = a.shape
    nb = b.shape[1]
    nk = s // tk
    nc = nb // chips if chips else tn
    cpb = tn // nc

    def body(a_ref, b_ref, o_ref, acc):
        k = pl.program_id(2)
        r = lax.dot_general(a_ref[...].astype(BF16), b_ref[...].astype(BF16), _TN, preferred_element_type=F32)

        @pl.when(k == 0)
        def _():
            acc[...] = r

        @pl.when(k > 0)
        def _():
            acc[...] += r

        @pl.when(k == nk - 1)
        def _():
            if chips:
                for q in range(cpb):
                    o_ref[q] = acc[:, q * nc:(q + 1) * nc]
            else:
                o_ref[...] = acc[...]

    if chips:
        o_spec = pl.BlockSpec((cpb, tm, nc), lambda i, j, k: (j, i, 0))
        out_shape = jax.ShapeDtypeStruct((chips, ma, nc), F32)
    else:
        o_spec = pl.BlockSpec((tm, tn), lambda i, j, k: (i, j))
        out_shape = jax.ShapeDtypeStruct((ma, nb), F32)
    return _pcall(
        body, name=name, grid=(ma // tm, nb // tn, nk),
        in_specs=[pl.BlockSpec((tk, tm), lambda i, j, k: (k, i)), pl.BlockSpec((tk, tn), lambda i, j, k: (k, j))],
        out_specs=o_spec, out_shape=out_shape, scratch_shapes=[pltpu.VMEM((tm, tn), F32)],
        compiler_params=pltpu.CompilerParams(dimension_semantics=("parallel", "parallel", "arbitrary")),
    )(a, b)


def _row_call(body, *, name, row_ins, res_ins, row_outs, acc_outs=(), scratch=()):
    row_ins = [a if isinstance(a, tuple) else (a, 1) for a in row_ins]
    row_outs = [o if len(o) == 3 else (*o, 1) for o in row_outs]
    s = row_ins[0][0].shape[0]
    tile = ROW_TILES.get(name, ROW_TILE)
    n = s // tile
    nri, nre, nro, nao = len(row_ins), len(res_ins), len(row_outs), len(acc_outs)

    def whole(shape):
        nd = len(shape)
        return pl.BlockSpec(tuple(shape), lambda i: (0,) * nd, pipeline_mode=pl.Buffered(1))

    def whole_out(shape):
        nd = len(shape)
        return pl.BlockSpec(tuple(shape), lambda i: (0,) * nd)

    def rows(w, d):
        if d == 1:
            return pl.BlockSpec((tile, w), lambda i: (i, 0))
        return pl.BlockSpec((d, tile // d, w), lambda i: (0, i, 0))

    in_specs = [rows(a.shape[1], d) for a, d in row_ins]
    in_specs += [whole(a.shape) for a in res_ins]
    out_specs = [rows(w, d) for w, _, d in row_outs]
    out_specs += [whole_out(shp) for shp, _ in acc_outs]
    out_shape = [jax.ShapeDtypeStruct((s, w) if d == 1 else (d, s // d, w), dt) for w, dt, d in row_outs]
    out_shape += [jax.ShapeDtypeStruct(tuple(shp), dt) for shp, dt in acc_outs]

    def wrapped(*refs):
        at = [0, nri, nri + nre, nri + nre + nro, nri + nre + nro + nao]
        body(pl.program_id(0), n, refs[at[0]:at[1]], refs[at[1]:at[2]], refs[at[2]:at[3]], refs[at[3]:at[4]],
             refs[at[4]:])

    args = [a if d == 1 else a.reshape(d, s // d, a.shape[1]) for a, d in row_ins]
    outs = _pcall(wrapped, name=name, grid=(n,), in_specs=in_specs, out_specs=out_specs, out_shape=out_shape,
                  scratch_shapes=list(scratch), compiler_params=_ARB)(*args, *res_ins)
    return [o.reshape(s, o.shape[-1]) if k < nro and row_outs[k][2] != 1 else o for k, o in enumerate(outs)]


def _stage_shape(name):
    return pltpu.VMEM((4, ROW_TILES.get(name, ROW_TILE), LANES), F32)


def _from_residue(ref, col, stage, slot):
    d, n = ref.shape[0], ref.shape[1]
    for r in range(d):
        stage.at[slot][pl.ds(r, n, stride=d), :] = ref[r, :, col:col + LANES].astype(F32)
    return stage[slot]


def _natural(ref, stage, slot0):
    if len(ref.shape) == 2:
        return ref[...]
    return jnp.concatenate([_from_residue(ref, c * LANES, stage, (slot0 + c) % 4)
                            for c in range(ref.shape[2] // LANES)], axis=1)


def _to_residue(val, ref, col, stage, slot):
    d, n = ref.shape[0], ref.shape[1]
    stage[slot] = val
    for r in range(d):
        ref[r, :, col:col + LANES] = stage.at[slot][pl.ds(r, n, stride=d), :].astype(ref.dtype)


def _fold8(t):
    r, w = t.shape
    return jnp.sum(t.reshape(r // 8, 8, w), axis=0)


def _sigmoid(t):
    return 0.5 * (jnp.tanh(0.5 * t) + 1.0)


def _head_lanes():
    return lax.broadcasted_iota(jnp.int32, (1, LANES), 1) < HEAD_DIM


def _head_mean(t, lo):
    s_lo = jnp.sum(jnp.where(lo, t, 0.0), axis=1, keepdims=True)
    s_hi = jnp.sum(jnp.where(lo, 0.0, t), axis=1, keepdims=True)
    return jnp.where(lo, s_lo, s_hi) * (1.0 / HEAD_DIM)


def _rms_mod(xv, g, sc, sh):
    rstd = lax.rsqrt(jnp.mean(xv * xv, axis=1, keepdims=True) + EPS)
    return (xv * rstd * g) * (1.0 + sc) + sh


def _rms_mod_bwd(xv, dh, g, sc):
    rstd = lax.rsqrt(jnp.mean(xv * xv, axis=1, keepdims=True) + EPS)
    xhat = xv * rstd
    dn = dh * (1.0 + sc)
    dxhat = dn * g
    dx = rstd * (dxhat - xhat * jnp.mean(dxhat * xhat, axis=1, keepdims=True))
    return dx, dh, dh * (xhat * g), dn * xhat


def _mix_weights(ls):
    m = jnp.maximum(jnp.maximum(ls[0], ls[1]), ls[2])
    es = [jnp.exp(t - m) for t in ls]
    den = es[0] + es[1] + es[2]
    return [e / den for e in es]


def _rope_tables():
    half = ROT_DIM // 2
    inv_freq = ROPE_THETA ** (-(jnp.arange(half, dtype=F32) * 2.0) / ROT_DIM)
    lane = np.arange(LANES) % HEAD_DIM
    ang = jnp.arange(SEQ).astype(F32)[:, None] * jnp.tile(inv_freq, LANES // half)[None, :]
    cos, sin = jnp.cos(ang), jnp.sin(ang)
    first, second = jnp.asarray(lane < half)[None, :], jnp.asarray((lane >= half) & (lane < ROT_DIM))[None, :]
    cos_t = jnp.where(first | second, cos, 1.0)
    return cos_t, jnp.where(second, sin, 0.0), jnp.where(first, -sin, 0.0)


_SECTIONS = ((0, WA, 0, False), (WA, 2 * WA, 1, False), (2 * WA, 3 * WA, -1, False),
             (3 * WA, 3 * WA + WB, 2, True), (3 * WA + WB, 3 * WA + 2 * WB, 3, True), (3 * WA + 2 * WB, W_QKV, -1, False))


def _norm1_fwd(x, g1, sc1, sh1, *, name):
    def body(i, n, rin, res, rout, aout, scr):
        rout[0][...] = _rms_mod(rin[0][...], res[0][...], res[1][...], res[2][...]).astype(BF16)

    return _row_call(body, name=name, row_ins=[x], res_ins=[g1, sc1, sh1], row_outs=[(D_MODEL, BF16)])[0]


def _pre_attn_fwd(h1, cos_t, sa_t, sb_t, w_qkv, w_gates, gains, *, name):
    half = ROT_DIM // 2
    dilated = [(g, dd) for g, (_, dd) in enumerate(DIL_CONFIGS) if dd > 1]

    def body(i, n, rin, res, rout, aout, scr):
        h_ref, cos_ref, sa_ref, sb_ref = rin
        wq_ref, wg_ref, gains_ref = res
        qkvn_ref, pre_ref, gates_ref = rout[:3]
        group_ref = {g: rout[3 + k] for k, (g, _) in enumerate(dilated)}
        (stage,) = scr
        staged = 0
        hb = h_ref[...]
        gates_ref[...] = _dot(hb, wg_ref[...]).astype(BF16)
        lo = _head_lanes()
        cosv, sav, sbv = cos_ref[...], sa_ref[...], sb_ref[...]
        pre_at = 0
        for si, (c0, c1, kind, rot) in enumerate(_SECTIONS):
            sec = _dot(hb, wq_ref[:, c0:c1])
            for ch in range((c1 - c0) // LANES):
                t = sec[:, ch * LANES:(ch + 1) * LANES]
                if kind >= 0:
                    pre_ref[:, pre_at:pre_at + LANES] = t.astype(BF16)
                    pre_at += LANES
                    t = t * lax.rsqrt(_head_mean(t * t, lo) + EPS) * gains_ref[kind:kind + 1, :]
                    if rot:
                        t = t * cosv + pltpu.roll(t, half, 1) * sav + pltpu.roll(t, LANES - half, 1) * sbv
                qkvn_ref[:, c0 + ch * LANES:c0 + (ch + 1) * LANES] = t.astype(BF16)
                group = ch * LANES // WB_OUT if si >= 3 else 0
                if group in group_ref:
                    col = (si - 3) * WB_OUT + ch * LANES % WB_OUT
                    _to_residue(t, group_ref[group], col, stage, staged % 4)
                    staged += 1

    return _row_call(body, name=name, row_ins=[h1, cos_t, sa_t, sb_t], res_ins=[w_qkv, w_gates, gains],
                     row_outs=[(W_QKV, BF16), (W_QK, BF16), (W_GATES, BF16)]
                     + [(3 * WB_OUT, BF16, dd) for _, dd in dilated], scratch=[_stage_shape(name)])


def _pre_attn_bwd(qk_pre, d_parts, dgates, x, dx1, cos_t, sa_t, sb_t, w_qkv, w_gates, gains, g1, sc1, *, name):
    half = ROT_DIM // 2
    nparts = len(d_parts)
    where = []
    residue = [isinstance(part, tuple) for part in d_parts]
    for pi, part in enumerate(d_parts):
        width = (part[0] if residue[pi] else part).shape[1]
        where += [(pi, cj) for cj in range(width // LANES)]
    assert len(where) == W_QKV // LANES

    def body(i, n, rin, res, rout, aout, scr):
        pre_ref, d_refs = rin[0], rin[1:1 + nparts]
        dgates_ref, x_ref, dx1_ref, cos_ref, sa_ref, sb_ref = rin[1 + nparts:]
        wq_ref, wg_ref, gains_ref, g_ref, sc_ref = res
        dqkv_ref, gx_ref = rout
        dgains_ref, sums_ref = aout
        accg, accs, stage = scr
        staged = 0

        @pl.when(i == 0)
        def _():
            accg[...] = jnp.zeros_like(accg)
            accs[...] = jnp.zeros_like(accs)

        lo = _head_lanes()
        cosv, sav, sbv = cos_ref[...], sa_ref[...], sb_ref[...]
        dh = _dot_nt(dgates_ref[...], wg_ref[...])
        pre_at = 0
        for c0, c1, kind, rot in _SECTIONS:
            for ch in range((c1 - c0) // LANES):
                pi, cj = where[c0 // LANES + ch]
                if residue[pi]:
                    dt = _from_residue(d_refs[pi], cj * LANES, stage, staged % 4)
                    staged += 1
                else:
                    dt = d_refs[pi][:, cj * LANES:(cj + 1) * LANES]
                if kind >= 0:
                    if rot:
                        dt = dt * cosv + pltpu.roll(dt * sav, LANES - half, 1) + pltpu.roll(dt * sbv, half, 1)
                    t = pre_ref[:, pre_at:pre_at + LANES].astype(F32)
                    pre_at += LANES
                    rstd = lax.rsqrt(_head_mean(t * t, lo) + EPS)
                    xhat = t * rstd
                    accg[kind] += _fold8(dt * xhat)
                    dxhat = dt * gains_ref[kind:kind + 1, :]
                    dt = rstd * (dxhat - xhat * _head_mean(dxhat * xhat, lo))
                dqkv_ref[:, c0 + ch * LANES:c0 + (ch + 1) * LANES] = dt.astype(BF16)
            dh = dh + _dot_nt(dqkv_ref[:, c0:c1], wq_ref[:, c0:c1])
        dx, t_sh, t_sc, t_g = _rms_mod_bwd(x_ref[...], dh, g_ref[...], sc_ref[...])
        gx_ref[...] = dx1_ref[...] + dx
        accs[0] += _fold8(t_sh)
        accs[1] += _fold8(t_sc)
        accs[2] += _fold8(t_g)

        @pl.when(i == n - 1)
        def _():
            t = jnp.sum(accg[...], axis=1)
            dgains_ref[...] = t + pltpu.roll(t, HEAD_DIM, 1)
            sums_ref[...] = jnp.sum(accs[...], axis=1)

    return _row_call(
        body, name=name, row_ins=[qk_pre, *d_parts, dgates, x, dx1, cos_t, sa_t, sb_t],
        res_ins=[w_qkv, w_gates, gains, g1, sc1], row_outs=[(W_QKV, BF16), (D_MODEL, F32)],
        acc_outs=[((4, LANES), F32), ((3, D_MODEL), F32)],
        scratch=[pltpu.VMEM((4, 8, LANES), F32), pltpu.VMEM((3, 8, D_MODEL), F32), _stage_shape(name)])


def _post_attn_fwd(o_a, o_g, l_g, gates, x, w_pa, w_pb, w_o, b_gate, gt1, g2, sc2, sh2, *, name):
    d = D_MODEL

    def body(i, n, rin, res, rout, aout, scr):
        oa_ref, o0, o1, o2, l0, l1, l2, gates_ref, x_ref = rin
        wpa_ref, wpb_ref, wo_ref, b_ref, gt_ref, g_ref, sc_ref, sh_ref = res
        mo_ref, x1_ref, h2_ref = rout
        (stage,) = scr
        ogs = [_natural(r, stage, 0) for r in (o0, o1, o2)]
        ws = _mix_weights([_natural(r, stage, 2) for r in (l0, l1, l2)])
        obb = (ws[0] * ogs[0] + ws[1] * ogs[1] + ws[2] * ogs[2]).astype(BF16)
        pa = _dot(oa_ref[...].astype(BF16), wpa_ref[...])
        pb = _dot(obb, wpb_ref[...])
        ga = _sigmoid(gates_ref[:, :d].astype(F32) + b_ref[:, :d])
        gb = _sigmoid(gates_ref[:, d:].astype(F32) + b_ref[:, d:])
        merged = (ga * pa + gb * pb).astype(BF16)
        mo = _dot(merged, wo_ref[...])
        mo_ref[...] = mo.astype(BF16)
        x1 = x_ref[...] + gt_ref[...] * mo
        x1_ref[...] = x1
        h2_ref[...] = _rms_mod(x1, g_ref[...], sc_ref[...], sh_ref[...]).astype(BF16)

    return _row_call(body, name=name, row_ins=[o_a, *o_g, *l_g, gates, x],
                     res_ins=[w_pa, w_pb, w_o, b_gate, gt1, g2, sc2, sh2],
                     row_outs=[(d, BF16), (d, F32), (d, BF16)], scratch=[_stage_shape(name)])


def _ffn_fwd(h2, w_ffn_in, *, name):
    def body(i, n, rin, res, rout, aout, scr):
        (h_ref,), (w_ref,), (act_ref, ff_ref) = rin, res, rout
        hv = h_ref[...]
        for q in range(2):
            a = _dot(hv, w_ref[:, q * FF_CHIP:(q + 1) * FF_CHIP])
            up = _dot(hv, w_ref[:, D_FF + q * FF_CHIP:D_FF + (q + 1) * FF_CHIP])
            sl = slice(q * FF_CHIP, (q + 1) * FF_CHIP)
            act_ref[:, sl] = (a * _sigmoid(a) * up).astype(BF16)
            ff_ref[:, sl] = a.astype(BF16)
            ff_ref[:, D_FF + q * FF_CHIP:D_FF + (q + 1) * FF_CHIP] = up.astype(BF16)

    return _row_call(body, name=name, row_ins=[h2], res_ins=[w_ffn_in], row_outs=[(D_FF, BF16), (2 * D_FF, BF16)])


def _ffn_mid(act, ff, x1, tgt, w_ffn_out, gt2, *, name):
    d = D_MODEL

    def body(i, n, rin, res, rout, aout, scr):
        act_ref, ff_ref, x1_ref, tgt_ref = rin
        wo_ref, gt_ref = res
        dy_ref, dffo_ref, dff_ref = rout
        dgt_ref, loss_ref = aout
        (acc,) = scr

        @pl.when(i == 0)
        def _():
            acc[...] = jnp.zeros_like(acc)

        ffo = _dot(act_ref[...], wo_ref[...])
        gtv = gt_ref[...]
        e = x1_ref[...] + gtv * ffo - tgt_ref[...]
        dy = e * (1.0 / d)
        dy_ref[...] = dy
        dffo = (gtv * dy).astype(BF16)
        dffo_ref[...] = dffo
        acc[0] += _fold8(dy * ffo)
        acc[1] += _fold8(e * e)
        for q in range(2):
            sl = slice(q * FF_CHIP, (q + 1) * FF_CHIP)
            su = slice(D_FF + q * FF_CHIP, D_FF + (q + 1) * FF_CHIP)
            dact = _dot_nt(dffo, wo_ref[sl, :])
            a = ff_ref[:, sl].astype(F32)
            up = ff_ref[:, su].astype(F32)
            sg = _sigmoid(a)
            dff_ref[:, sl] = (dact * up * (sg * (1.0 + a * (1.0 - sg)))).astype(BF16)
            dff_ref[:, su] = (dact * (a * sg)).astype(BF16)

        @pl.when(i == n - 1)
        def _():
            dgt_ref[...] = jnp.sum(acc[0], axis=0, keepdims=True)
            tot = jnp.sum(jnp.sum(acc[1], axis=0, keepdims=True), axis=1, keepdims=True)
            loss_ref[...] = jnp.broadcast_to(tot * (0.5 / d), (1, LANES))

    return _row_call(body, name=name, row_ins=[act, ff, x1, tgt], res_ins=[w_ffn_out, gt2],
                     row_outs=[(d, F32), (d, BF16), (2 * D_FF, BF16)], acc_outs=[((1, d), F32), ((1, LANES), F32)],
                     scratch=[pltpu.VMEM((2, 8, d), F32)])


def _ffn_in_bwd(dff, x1, dy, mo, w_ffn_in, g2, sc2, gt1, *, name):
    d = D_MODEL

    def body(i, n, rin, res, rout, aout, scr):
        dff_ref, x1_ref, dy_ref, mo_ref = rin
        w_ref, g_ref, sc_ref, gt_ref = res
        dx1_ref, dmo_ref = rout
        (sums_ref,) = aout
        (acc,) = scr

        @pl.when(i == 0)
        def _():
            acc[...] = jnp.zeros_like(acc)

        dh = _dot_nt(dff_ref[...], w_ref[...])
        dx, t_sh, t_sc, t_g = _rms_mod_bwd(x1_ref[...], dh, g_ref[...], sc_ref[...])
        dx1 = dy_ref[...] + dx
        dx1_ref[...] = dx1
        dmo_ref[...] = (gt_ref[...] * dx1).astype(BF16)
        acc[0] += _fold8(t_sh)
        acc[1] += _fold8(t_sc)
        acc[2] += _fold8(t_g)
        acc[3] += _fold8(dx1 * mo_ref[...].astype(F32))

        @pl.when(i == n - 1)
        def _():
            sums_ref[...] = jnp.sum(acc[...], axis=1)

    return _row_call(body, name=name, row_ins=[dff, x1, dy, mo], res_ins=[w_ffn_in, g2, sc2, gt1],
                     row_outs=[(d, F32), (d, BF16)], acc_outs=[((4, d), F32)], scratch=[pltpu.VMEM((4, 8, d), F32)])


def _post_attn_bwd(dmo, gates, o_a, o_g, l_g, w_pa, w_pb, w_o, b_gate, *, name):
    d = D_MODEL

    def body(i, n, rin, res, rout, aout, scr):
        dmo_ref, gates_ref, oa_ref, o0, o1, o2, l0, l1, l2 = rin
        wpa_ref, wpb_ref, wo_ref, b_ref = res
        dgates_ref, doa_ref = rout[:2]
        do_refs, dl_refs = rout[2:5], rout[5:8]
        dbg_ref, gpa_ref, gpb_ref, go_ref = aout
        acc, stage = scr

        @pl.when(i == 0)
        def _():
            acc[...] = jnp.zeros_like(acc)
            gpa_ref[...] = jnp.zeros_like(gpa_ref)
            gpb_ref[...] = jnp.zeros_like(gpb_ref)
            go_ref[...] = jnp.zeros_like(go_ref)

        ogs = [_natural(r, stage, 0) for r in (o0, o1, o2)]
        ws = _mix_weights([_natural(r, stage, 2) for r in (l0, l1, l2)])
        obb = (ws[0] * ogs[0] + ws[1] * ogs[1] + ws[2] * ogs[2]).astype(BF16)
        oa = oa_ref[...].astype(BF16)
        pa = _dot(oa, wpa_ref[...])
        pb = _dot(obb, wpb_ref[...])
        ga = _sigmoid(gates_ref[:, :d].astype(F32) + b_ref[:, :d])
        gb = _sigmoid(gates_ref[:, d:].astype(F32) + b_ref[:, d:])
        merged = (ga * pa + gb * pb).astype(BF16)
        go_ref[...] += lax.dot_general(merged, dmo_ref[...], _TN, preferred_element_type=F32)
        dm = _dot_nt(dmo_ref[...], wo_ref[...])
        dpa = (dm * ga).astype(BF16)
        dpb = (dm * gb).astype(BF16)
        gpa_ref[...] += lax.dot_general(oa, dpa, _TN, preferred_element_type=F32)
        gpb_ref[...] += lax.dot_general(obb, dpb, _TN, preferred_element_type=F32)
        dga = dm * pa * ga * (1.0 - ga)
        dgb = dm * pb * gb * (1.0 - gb)
        dgates_ref[:, :d] = dga.astype(BF16)
        dgates_ref[:, d:] = dgb.astype(BF16)
        acc[:, :d] += _fold8(dga)
        acc[:, d:] += _fold8(dgb)
        doa_ref[...] = _dot_nt(dpa, wpa_ref[...])
        dob = _dot_nt(dpb, wpb_ref[...])
        lo = _head_lanes()
        for ch in range(WB_OUT // LANES):
            sl = slice(ch * LANES, (ch + 1) * LANES)
            dv = dob[:, sl]
            wc = [w[:, sl] for w in ws]
            ts = [_head_mean(dv * og[:, sl], lo) * float(HEAD_DIM) for og in ogs]
            tbar = wc[0] * ts[0] + wc[1] * ts[1] + wc[2] * ts[2]
            for g in range(3):
                for k, (ref, val) in enumerate(((do_refs[g], wc[g] * dv), (dl_refs[g], wc[g] * (ts[g] - tbar)))):
                    if len(ref.shape) == 2:
                        ref[:, sl] = val
                    else:
                        _to_residue(val, ref, ch * LANES, stage, (2 * g + k) % 4)

        @pl.when(i == n - 1)
        def _():
            dbg_ref[...] = jnp.sum(acc[...], axis=0, keepdims=True)

    return _row_call(body, name=name, row_ins=[dmo, gates, o_a, *o_g, *l_g], res_ins=[w_pa, w_pb, w_o, b_gate],
                     row_outs=[(2 * d, BF16), (WA, F32)]
                     + 2 * [(WB_OUT, F32, dd) for _, dd in DIL_CONFIGS],
                     acc_outs=[((1, 2 * d), F32), ((WA, d), F32), ((WB_OUT, d), F32), ((d, d), F32)],
                     scratch=[pltpu.VMEM((8, 2 * d), F32), _stage_shape(name)])


def _na_class_tables():
    ro = np.full((3, NA_QROWS, 2 * NA_PAIRS), NA_RO_NONE, np.int64)
    slot = np.zeros((3, NA_QROWS, NA_PAIRS), np.int64)
    for t in range(3):
        for a in range(NA_QROWS):
            qr = _NA_CLASS_R0[t] + a
            rs = min(max(qr - NA_KH // 2, 0), ROWS - NA_KH)
            for b in range(NA_KROWS):
                kr = _NA_CLASS_K0[t] + b
                if rs <= kr < rs + NA_KH:
                    ro[t, a, b] = kr - qr + (NA_KH - 1)
            for j in range(NA_PAIRS):
                slot[t, a, j] = 2 * j - a + (_NA_CLASS_K0[t] - _NA_CLASS_R0[t] + NA_KH - 1) + (NA_QROWS - 1)
    assert slot.min() >= 0 and slot.max() < NA_SLOTS
    return ro, slot


def _na_build_bias(i, cls_ref, rp_ref, cm_ref, bias_scr):
    ro, _ = _na_class_tables()
    lo = _head_lanes()
    first = jnp.logical_or(i == 0, cls_ref[i] != cls_ref[jnp.maximum(i - 1, 0)])
    for t in range(3):
        @pl.when(jnp.logical_and(first, cls_ref[i] == t))
        def _():
            for hh in range(2):
                for a in range(NA_QROWS):
                    for j in range(NA_PAIRS):
                        r0, r1 = int(ro[t, a, 2 * j]), int(ro[t, a, 2 * j + 1])
                        x0 = jnp.broadcast_to(rp_ref[hh, r0:r0 + 1, :], (GRID_W, LANES))
                        x1 = jnp.broadcast_to(rp_ref[hh, r1:r1 + 1, :], (GRID_W, LANES))
                        blk = jnp.where(lo, pltpu.roll(x0, GRID_W + 1, 1, stride=1, stride_axis=0),
                                        pltpu.roll(x1, 1, 1, stride=1, stride_axis=0))
                        bias_scr[hh, a * GRID_W:(a + 1) * GRID_W, j * LANES:(j + 1) * LANES] = blk + cm_ref[...]
    return first


def _attn_fwd(qkv, qc0, kc0, vc0, npairs, table, kstart, cls, nk, *, name, na=None, qb=Q_BLOCK):
    s = qkv.shape[0]

    def body(ks_ref, cls_ref, q_ref, k_ref, v_ref, b_ref, *rest):
        if na:
            cm_ref, o_ref, lse_ref, bias_scr = rest
        else:
            o_ref, lse_ref = rest
        i = pl.program_id(1)
        if na:
            _na_build_bias(i, cls_ref, b_ref, cm_ref, bias_scr)
        ks = pl.multiple_of(ks_ref[i], 64)
        q2 = q_ref[...]
        k2 = k_ref[pl.ds(ks, nk), :]
        v2 = v_ref[pl.ds(ks, nk), :]
        lo = _head_lanes()
        outs, lses = [], []
        for h in range(2):
            qm = jnp.where(lo if h == 0 else jnp.logical_not(lo), q2, jnp.zeros_like(q2))
            sc = _dot_nt(qm, k2) * SCALE + (bias_scr[h, :, :nk] if na else b_ref[0, 0])
            m = jnp.max(sc, axis=1, keepdims=True)
            p = jnp.exp(sc - m)
            l = jnp.sum(p, axis=1, keepdims=True)
            pv = _dot(p.astype(BF16), v2)
            outs.append(pv / l)
            lses.append(m + jnp.log(l))
        o_ref[...] = jnp.where(lo, outs[0], outs[1])
        lse_ref[...] = jnp.where(lo, lses[0], lses[1])

    w = npairs * LANES
    in_specs = [
        pl.BlockSpec((qb, LANES), lambda p, i, ks, cl: (i, qc0 + p)),
        pl.BlockSpec((s, LANES), lambda p, i, ks, cl: (0, kc0 + p)),
        pl.BlockSpec((s, LANES), lambda p, i, ks, cl: (0, vc0 + p)),
    ]
    if na:
        in_specs += _na_bias_specs()
        args, scratch = (kstart, cls, qkv, qkv, qkv, *na), [pltpu.VMEM((2, Q_BLOCK, NA_W), F32)]
    else:
        in_specs.append(pl.BlockSpec((1, 1, qb, nk), lambda p, i, ks, cl: (cl[i], 0, 0, 0)))
        args, scratch = (kstart, cls, qkv, qkv, qkv, table), []
    grid_spec = pltpu.PrefetchScalarGridSpec(
        num_scalar_prefetch=2, grid=(npairs, s // qb), in_specs=in_specs,
        out_specs=[pl.BlockSpec((qb, LANES), lambda p, i, ks, cl: (i, p)),
                   pl.BlockSpec((qb, LANES), lambda p, i, ks, cl: (i, p))],
        scratch_shapes=scratch,
    )
    return _pcall(body, name=name, grid_spec=grid_spec,
                  out_shape=[jax.ShapeDtypeStruct((s, w), F32), jax.ShapeDtypeStruct((s, w), F32)],
                  compiler_params=pltpu.CompilerParams(dimension_semantics=("parallel", "arbitrary")),
                  )(*args)


def _na_bias_specs():
    return [pl.BlockSpec((2, 16, LANES), lambda p, i, ks, cl: (p, 0, 0)),
            pl.BlockSpec((GRID_W, LANES), lambda p, i, ks, cl: (0, 0))]


def _attn_bwd(qkv, qc0, kc0, vc0, npairs, table, kstart, cls, nk, do, o, lse, *, name, dlse=None, na=None,
              qb=Q_BLOCK, order=None):
    s = qkv.shape[0]
    has_dlse = dlse is not None
    _, slot = _na_class_tables()

    def body(ks_ref, cls_ref, q_ref, k_ref, v_ref, b_ref, *rest):
        if na:
            cm_ref, rest = rest[0], rest[1:]
        do_ref, o_ref, lse_ref, rest = rest[0], rest[1], rest[2], rest[3:]
        if has_dlse:
            dlse_ref, rest = rest[0], rest[1:]
        if order is not None:
            rest = rest[1:]
        dq_ref, dk_ref, dv_ref = rest[0], rest[1], rest[2]
        if na:
            bank_ref, bias_scr, dbias_scr, bank_scr = rest[3:]
        i = pl.program_id(1)
        if na:
            first = _na_build_bias(i, cls_ref, b_ref, cm_ref, bias_scr)

        @pl.when(i == 0)
        def _():
            dk_ref[...] = jnp.zeros_like(dk_ref)
            dv_ref[...] = jnp.zeros_like(dv_ref)
            if na:
                dbias_scr[...] = jnp.zeros_like(dbias_scr)
                bank_scr[...] = jnp.zeros_like(bank_scr)

        ks = pl.multiple_of(ks_ref[i], 64)
        q2 = q_ref[...]
        k2 = k_ref[pl.ds(ks, nk), :]
        v2 = v_ref[pl.ds(ks, nk), :]
        do2 = do_ref[...]
        lse2 = lse_ref[...]
        doo = do2 * o_ref[...]
        do2b = do2.astype(BF16)
        lo = _head_lanes()
        lane = lax.broadcasted_iota(jnp.int32, (1, LANES), 1)
        dqs, dks, dvs = [], [], []
        for h in range(2):
            mh = lo if h == 0 else jnp.logical_not(lo)
            qm = jnp.where(mh, q2, jnp.zeros_like(q2))
            sc = _dot_nt(qm, k2) * SCALE + (bias_scr[h, :, :nk] if na else b_ref[0, 0])
            lse_h = jnp.max(jnp.where(mh, lse2, NEG), axis=1, keepdims=True)
            p = jnp.exp(sc - lse_h)
            delta = jnp.sum(jnp.where(mh, doo, 0.0), axis=1, keepdims=True)
            dom = jnp.where(mh, do2b, jnp.zeros_like(do2b))
            dp = _dot_nt(dom, v2)
            t = dp - delta
            if has_dlse:
                t = t + jnp.sum(jnp.where(lane == h * HEAD_DIM, dlse_ref[...], 0.0), axis=1, keepdims=True)
            ds = p * t
            if na:
                @pl.when(first)
                def _():
                    dbias_scr[h, :, :nk] = ds

                @pl.when(jnp.logical_not(first))
                def _():
                    dbias_scr[h, :, :nk] += ds
            dsb = ds.astype(BF16)
            dqs.append(_dot(dsb, k2))
            dks.append(lax.dot_general(dsb, q2, _TN, preferred_element_type=F32))
            dvs.append(lax.dot_general(p.astype(BF16), do2b, _TN, preferred_element_type=F32))
        dq_ref[...] = jnp.where(lo, dqs[0], dqs[1]) * SCALE
        dk_ref[pl.ds(ks, nk), :] += jnp.where(lo, dks[0], dks[1]) * SCALE
        dv_ref[pl.ds(ks, nk), :] += jnp.where(lo, dvs[0], dvs[1])
        if na:
            last = jnp.logical_or(i == N_QBLK - 1, cls_ref[i] != cls_ref[jnp.minimum(i + 1, N_QBLK - 1)])
            for t in range(3):
                @pl.when(jnp.logical_and(last, cls_ref[i] == t))
                def _():
                    for hh in range(2):
                        for a in range(NA_QROWS):
                            for j in range(NA_PAIRS):
                                bank_scr[hh, int(slot[t, a, j])] += dbias_scr[
                                    hh, a * GRID_W:(a + 1) * GRID_W, j * LANES:(j + 1) * LANES]

            @pl.when(i == N_QBLK - 1)
            def _():
                bank_ref[...] = bank_scr[...]

    w = npairs * LANES
    blk = lambda: pl.BlockSpec((qb, LANES), lambda p, i, ks, cl: (i, p))
    full = lambda: pl.BlockSpec((s, LANES), lambda p, i, ks, cl: (0, p))
    in_specs = [
        pl.BlockSpec((qb, LANES), lambda p, i, ks, cl: (i, qc0 + p)),
        pl.BlockSpec((s, LANES), lambda p, i, ks, cl: (0, kc0 + p)),
        pl.BlockSpec((s, LANES), lambda p, i, ks, cl: (0, vc0 + p)),
    ]
    if na:
        in_specs += _na_bias_specs()
        args = [kstart, cls, qkv, qkv, qkv, *na]
    else:
        in_specs.append(pl.BlockSpec((1, 1, qb, nk), lambda p, i, ks, cl: (cl[i], 0, 0, 0)))
        args = [kstart, cls, qkv, qkv, qkv, table]
    in_specs += [blk(), blk(), blk()]
    args += [do, o, lse]
    if has_dlse:
        in_specs.append(blk())
        args.append(dlse)
    out_specs = [blk(), full(), full()]
    out_shape = [jax.ShapeDtypeStruct((s, w), F32)] * 3
    scratch = []
    if na:
        bank_shape = (2, NA_SLOTS, GRID_W, LANES)
        out_specs.append(pl.BlockSpec(bank_shape, lambda p, i, ks, cl: (p, 0, 0, 0)))
        out_shape.append(jax.ShapeDtypeStruct((2 * npairs,) + bank_shape[1:], F32))
        scratch = [pltpu.VMEM((2, Q_BLOCK, NA_W), F32), pltpu.VMEM((2, Q_BLOCK, NA_W), F32), pltpu.VMEM(bank_shape, F32)]
    if order is not None:
        in_specs.append(pl.BlockSpec(order.shape, lambda p, i, ks, cl: (0, 0)))
        args.append(order)
    grid_spec = pltpu.PrefetchScalarGridSpec(num_scalar_prefetch=2, grid=(npairs, s // qb), in_specs=in_specs,
                                             out_specs=out_specs, scratch_shapes=scratch)
    return _pcall(body, name=name, grid_spec=grid_spec, out_shape=out_shape,
                  compiler_params=pltpu.CompilerParams(dimension_semantics=("arbitrary", "arbitrary")))(*args)


_NA_CLASS_R0 = (0, NA_QROWS, ROWS - NA_QROWS)
_NA_CLASS_K0 = (0, 0, ROWS - NA_KROWS)
_RPB_RO = 2 * NA_KH - 1
_RPB_CO = 2 * NA_KW - 1
_BANK_ROWS = 48


def _na_constants():
    col = np.arange(GRID_W)
    cs = np.clip(col - NA_KW // 2, 0, GRID_W - NA_KW)
    vcol = (col[None, :] >= cs[:, None]) & (col[None, :] < cs[:, None] + NA_KW)
    colmask = np.where(np.concatenate([vcol, vcol], axis=1), 0.0, NEG).astype(np.float32)
    co = col[None, :] - col[:, None] + (NA_KW - 1)
    oh_col = np.zeros((GRID_W * GRID_W, LANES), np.float32)
    for qc in range(GRID_W):
        for kc in range(GRID_W):
            if vcol[qc, kc]:
                oh_col[qc * GRID_W + kc, co[qc, kc]] = 1.0
    ks = np.clip(np.arange(N_QBLK) * NA_QROWS - NA_KH // 2, 0, ROWS - NA_KROWS) * GRID_W
    cls = np.ones(N_QBLK, np.int32)
    cls[0], cls[-1] = 0, 2
    return colmask, oh_col, ks.astype(np.int32), cls


def _bank_reduce(bank, oh_col, *, name):
    def body(d_ref, ohc_ref, o_ref):
        o_ref[0] = jnp.dot(d_ref[0], ohc_ref[...], preferred_element_type=F32, precision=lax.Precision.HIGHEST)

    return _pcall(
        body, name=name, grid=(NA_HEADS,),
        in_specs=[pl.BlockSpec((1, _BANK_ROWS, GRID_W * GRID_W), lambda h: (h, 0, 0)),
                  pl.BlockSpec((GRID_W * GRID_W, LANES), lambda h: (0, 0))],
        out_specs=pl.BlockSpec((1, _BANK_ROWS, LANES), lambda h: (h, 0, 0)),
        out_shape=jax.ShapeDtypeStruct((NA_HEADS, _BANK_ROWS, LANES), F32), compiler_params=_PAR,
    )(bank, oh_col)


def _dil_constants(dilation):
    seg = SEQ // dilation
    nb = seg // DIL_QB
    nk = min(DIL_QB + 2 * DIL_HALF, seg)
    starts = [min(max(blk * DIL_QB - DIL_HALF, 0), seg - nk) for blk in range(nb)]
    shifts = sorted({w0 - blk * DIL_QB for blk, w0 in enumerate(starts)}, reverse=True)
    qi = np.arange(DIL_QB)[:, None]
    ki = np.arange(nk)[None, :]
    mask = np.stack([np.where(np.abs(ki + sh - qi) <= DIL_HALF, 0.0, NEG) for sh in shifts]).astype(np.float32)
    ks, cls = [], []
    for i in range(SEQ // DIL_QB):
        sub, blk = divmod(i, nb)
        cls.append(shifts.index(starts[blk] - blk * DIL_QB))
        ks.append(sub * seg + starts[blk])
    return mask.reshape(len(shifts), 1, DIL_QB, nk), np.asarray(ks, np.int32), np.asarray(cls, np.int32), nk


_VM = pl.BlockSpec(memory_space=pltpu.VMEM)


def _ada_fwd(c_all, w, b, *, name):
    def body(c_ref, w_ref, b_ref, o_ref):
        cv = c_ref[...]
        o_ref[...] = jnp.dot(cv * _sigmoid(cv), w_ref[...], preferred_element_type=F32,
                             precision=lax.Precision.HIGHEST) + b_ref[...]

    return _pcall(body, name=name, in_specs=[_VM, _VM, _VM], out_specs=_VM,
                  out_shape=jax.ShapeDtypeStruct((c_all.shape[0], w.shape[1]), F32))(c_all, w, b)


def _ada_bwd(c_all_t, dmod, *, name):
    def body(c_ref, d_ref, o_ref):
        cv = c_ref[...]
        o_ref[...] = jnp.dot(cv * _sigmoid(cv), d_ref[...], preferred_element_type=F32,
                             precision=lax.Precision.HIGHEST)

    return _pcall(body, name=name, in_specs=[_VM, _VM], out_specs=_VM,
                  out_shape=jax.ShapeDtypeStruct((c_all_t.shape[0], dmod.shape[1]), F32))(c_all_t, dmod)


def _row_sum(t, *, name):
    def body(t_ref, o_ref):
        o_ref[...] = jnp.sum(t_ref[...], axis=0, keepdims=True)

    return _pcall(body, name=name, in_specs=[_VM], out_specs=_VM,
                  out_shape=jax.ShapeDtypeStruct((1, t.shape[1]), F32))(t)


def _row_tile(rows):
    tr = rows
    for cand in range(8, 513, 8):
        if rows % cand == 0:
            tr = cand
    return tr


def _adamw_math(wv, gv, mv, vv):
    nm = ADAM_B1 * mv + (1.0 - ADAM_B1) * gv
    nv = ADAM_B2 * vv + (1.0 - ADAM_B2) * (gv * gv)
    m_hat = nm / (1.0 - ADAM_B1 ** ADAM_STEP)
    v_hat = nv / (1.0 - ADAM_B2 ** ADAM_STEP)
    return -ADAM_LR * (m_hat / (jnp.sqrt(v_hat) + ADAM_EPS) + ADAM_WD * wv), nm, nv


def _adamw(w, g, m, v, *, name):
    rows, cols = w.shape
    tr = _row_tile(rows)

    def body(w_ref, g_ref, m_ref, v_ref, d_ref, nm_ref, nv_ref):
        d_ref[...], nm_ref[...], nv_ref[...] = _adamw_math(w_ref[...], g_ref[...], m_ref[...], v_ref[...])

    spec = pl.BlockSpec((tr, cols), lambda i: (i, 0))
    return _pcall(body, name=name, grid=(rows // tr,), in_specs=[spec] * 4, out_specs=[spec] * 3,
                  out_shape=[jax.ShapeDtypeStruct((rows, cols), F32)] * 3, compiler_params=_PAR)(w, g, m, v)


def _adamw_halves(w, g_mine, g_other, m, v, c_arr, *, name):
    rows, cols = w.shape
    hr = rows // 2
    tr = _row_tile(hr)
    nt = hr // tr

    def body(c_ref, w_ref, t_ref, o_ref, m_ref, v_ref, g_ref, d_ref, nm_ref, nv_ref):
        gv = jnp.where(pl.program_id(0) == c_ref[0], t_ref[...], o_ref[...])
        g_ref[...] = gv
        d_ref[...], nm_ref[...], nv_ref[...] = _adamw_math(w_ref[...], gv, m_ref[...], v_ref[...])

    full = pl.BlockSpec((tr, cols), lambda h, i, c: (h * nt + i, 0))
    half = pl.BlockSpec((tr, cols), lambda h, i, c: (i, 0))
    grid_spec = pltpu.PrefetchScalarGridSpec(num_scalar_prefetch=1, grid=(2, nt),
                                             in_specs=[full, half, half, full, full], out_specs=[full] * 4)
    return _pcall(body, name=name, grid_spec=grid_spec, out_shape=[jax.ShapeDtypeStruct((rows, cols), F32)] * 4,
                  compiler_params=pltpu.CompilerParams(dimension_semantics=("parallel", "parallel")),
                  )(c_arr, w, g_mine, g_other, m, v)


_MESH = pl.DeviceIdType.MESH
_ANY = pl.BlockSpec(memory_space=pl.ANY)
_CHIP_FLIPS = ((1, 0), (0, 1), (1, 1))


def _pos():
    return lax.axis_index("x"), lax.axis_index("y"), lax.axis_index("c")


def _flip(v, f):
    return 1 - v if f else v


def _sem_pairs(n):
    return [pltpu.SemaphoreType.DMA((n,)), pltpu.SemaphoreType.DMA((n,))]


def _small_allgather(blk, *, name):
    m_per, n = blk.shape

    def body(x_ref, out_ref, send_sems, recv_sems, local_sem):
        x, y, c = _pos()
        me, sibling = (x, y, c), (x, y, 1 - c)
        chips = [(_flip(x, fx), _flip(y, fy)) for fx, fy in _CHIP_FLIPS]

        def rows(px, py, pc):
            return out_ref.at[pl.ds((4 * px + 2 * py + pc) * m_per, m_per), :]

        def copy(k, block, to, src=None):
            return pltpu.make_async_remote_copy(
                src_ref=rows(*block) if src is None else src, dst_ref=rows(*block),
                send_sem=send_sems.at[k], recv_sem=recv_sems.at[k], device_id=to, device_id_type=_MESH)

        mine = pltpu.make_async_copy(x_ref, rows(*me), local_sem)
        mine.start()
        first = [copy(0, me, sibling, src=x_ref)]
        first += [copy(1 + j, me, (*chip, c), src=x_ref) for j, chip in enumerate(chips)]
        for cp in first:
            cp.start()
        passed = [copy(4 + j, (*chip, c), sibling) for j, chip in enumerate(chips)]
        for j, chip in enumerate(chips):
            copy(1 + j, (*chip, c), me).wait_recv()
            passed[j].start()
        copy(0, sibling, me).wait_recv()
        for j, chip in enumerate(chips):
            copy(4 + j, (*chip, 1 - c), me).wait_recv()
        for cp in first + passed:
            cp.wait_send()
        mine.wait()

    return _pcall(
        body, name=name, out_shape=jax.ShapeDtypeStruct((N_DEV * m_per, n), blk.dtype),
        in_specs=[_VM], out_specs=_VM,
        scratch_shapes=_sem_pairs(7) + [pltpu.SemaphoreType.DMA],
    )(blk)


_HBM = pl.BlockSpec(memory_space=pltpu.HBM)
_SEM = pl.BlockSpec(memory_space=pltpu.SEMAPHORE)
_EFFECT = pltpu.SideEffectType.DATAFLOW_SIDE_EFFECTING


def _split_start(srcs, lands, plan, ncopies, after, *, name, alias_sources=True):
    ns, nl = len(srcs), len(lands)
    thru = (*srcs, *lands) if alias_sources else tuple(lands)

    def body(*refs):
        src_refs, land_refs = refs[:ns], refs[ns:ns + nl]
        send_sems, recv_sems = refs[ns + nl + 1], refs[ns + nl + 2]
        token = refs[-1]
        x, y, c = _pos()
        for k, (src, dst, to, _) in enumerate(plan(x, y, c, src_refs, land_refs)):
            pltpu.make_async_remote_copy(src_ref=src, dst_ref=dst, send_sem=send_sems.at[k], recv_sem=recv_sems.at[k],
                                         device_id=to, device_id_type=_MESH).start()
        token[...] = jnp.zeros_like(token)

    hbm = lambda a: pltpu.HBM(a.shape, a.dtype)
    out = _pcall(
        body, name=name,
        out_shape=(pltpu.SemaphoreType.DMA((ncopies,)), pltpu.SemaphoreType.DMA((ncopies,)),
                   *[hbm(a) for a in thru], jax.ShapeDtypeStruct((8, LANES), F32)),
        in_specs=[_HBM] * (ns + nl) + [_ANY], out_specs=(_SEM, _SEM, *[_HBM] * len(thru), _VM),
        input_output_aliases={ns + nl - len(thru) + i: 2 + i for i in range(len(thru))},
        compiler_params=pltpu.CompilerParams(has_side_effects=_EFFECT),
    )(*[pltpu.with_memory_space_constraint(a, pltpu.HBM) for a in (*srcs, *lands)], after)
    if alias_sources:
        return out[0], out[1], list(out[2:2 + ns]), list(out[2 + ns:2 + ns + nl]), out[-1]
    return out[0], out[1], list(srcs), list(out[2:2 + nl]), out[-1]


def _split_wait(send_sems, recv_sems, srcs, lands, plan, after, *, name, with_sources=False, alias_sources=True):
    ns, nl = len(srcs), len(lands)
    thru = (*srcs, *lands) if alias_sources else tuple(lands)

    def body(*refs):
        src_refs, land_refs = refs[:ns], refs[ns:ns + nl]
        send_sems, recv_sems = refs[ns + nl], refs[ns + nl + 1]
        x, y, c = _pos()
        for k, (src, _, _, mine) in enumerate(plan(x, y, c, src_refs, land_refs)):
            cp = pltpu.make_async_remote_copy(src_ref=src, dst_ref=mine, send_sem=send_sems.at[k],
                                              recv_sem=recv_sems.at[k], device_id=(x, y, c), device_id_type=_MESH)
            cp.wait_send()
            cp.wait_recv()

    hbm = lambda a: pltpu.HBM(a.shape, a.dtype)
    out = _pcall(
        body, name=name, out_shape=tuple(hbm(a) for a in thru),
        in_specs=[_HBM] * (ns + nl) + [_SEM, _SEM, _ANY], out_specs=tuple([_HBM] * len(thru)),
        input_output_aliases={ns + nl - len(thru) + i: i for i in range(len(thru))},
        compiler_params=pltpu.CompilerParams(has_side_effects=_EFFECT),
    )(*srcs, *lands, send_sems, recv_sems, after)
    got_srcs, got_lands = (list(out[:ns]), list(out[ns:])) if alias_sources else (list(srcs), list(out))
    return (got_srcs, got_lands) if with_sources else got_lands


def _ag_plan(nw):
    def plan(x, y, c, sh_refs, full_refs):
        j = 2 * x + y
        out = []
        for wi in range(nw):
            for fx, fy in _CHIP_FLIPS:
                px, py = _flip(x, fx), _flip(y, fy)
                out.append((sh_refs[wi].at[c], full_refs[wi].at[j, c], (px, py, c), full_refs[wi].at[2 * px + py, c]))
            out.append((sh_refs[wi], full_refs[wi].at[j], (x, y, 1 - c), full_refs[wi].at[j]))
        return out
    return plan


def _ag_plan_direct(nw):
    def plan(x, y, c, sh_refs, full_refs):
        j = 2 * x + y
        out = []
        for wi in range(nw):
            for fx, fy in _CHIP_FLIPS:
                px, py = _flip(x, fx), _flip(y, fy)
                for rel in (0, 1):
                    t = _flip(c, rel)
                    out.append((sh_refs[wi].at[c], full_refs[wi].at[j, c], (px, py, t), full_refs[wi].at[2 * px + py, t]))
            out.append((sh_refs[wi], full_refs[wi].at[j], (x, y, 1 - c), full_refs[wi].at[j]))
        return out
    return plan


def _ag_pass(fulls, *, name):
    nw = len(fulls)

    def body(*refs):
        in_refs, out_refs = refs[:nw], refs[nw:2 * nw]
        send_sems, recv_sems = refs[2 * nw:]
        x, y, c = _pos()
        cps = []
        for wi in range(nw):
            for k, (fx, fy) in enumerate(_CHIP_FLIPS):
                jp = 2 * _flip(x, fx) + _flip(y, fy)
                sems = dict(send_sem=send_sems.at[3 * wi + k], recv_sem=recv_sems.at[3 * wi + k], device_id_type=_MESH)
                send = pltpu.make_async_remote_copy(src_ref=in_refs[wi].at[jp, c], dst_ref=out_refs[wi].at[jp, c],
                                                    device_id=(x, y, 1 - c), **sems)
                recv = pltpu.make_async_remote_copy(src_ref=in_refs[wi].at[jp, c], dst_ref=out_refs[wi].at[jp, 1 - c],
                                                    device_id=(x, y, c), **sems)
                cps.append((send, recv))
        for send, _ in cps:
            send.start()
        for send, recv in cps:
            send.wait_send()
            recv.wait_recv()

    return _pcall(body, name=name, out_shape=[jax.ShapeDtypeStruct(f.shape, f.dtype) for f in fulls],
                  in_specs=[_ANY] * nw, out_specs=[_ANY] * nw, input_output_aliases={i: i for i in range(nw)},
                  scratch_shapes=_sem_pairs(3 * nw))(*fulls)


def _sib_plan(nw):
    def plan(x, y, c, g_refs, ra_refs):
        return [(g_refs[wi].at[k, 1 - c], ra_refs[wi].at[k], (x, y, 1 - c), ra_refs[wi].at[k])
                for wi in range(nw) for k in range(N_CHIP)]
    return plan


def _rs_plan(nw):
    def plan(x, y, c, s_refs, rb_refs):
        out = []
        for wi in range(nw):
            for k, (fx, fy) in enumerate(_CHIP_FLIPS):
                px, py = _flip(x, fx), _flip(y, fy)
                out.append((s_refs[wi].at[2 * px + py], rb_refs[wi].at[k], (px, py, c), rb_refs[wi].at[k]))
        return out
    return plan


def _sibling_swap(ts, *, name):
    nw = len(ts)

    def body(*refs):
        t_refs, out_refs = refs[:nw], refs[nw:2 * nw]
        send_sems, recv_sems = refs[2 * nw:]
        x, y, c = _pos()
        cps = [pltpu.make_async_remote_copy(src_ref=t_refs[wi], dst_ref=out_refs[wi], send_sem=send_sems.at[wi],
                                            recv_sem=recv_sems.at[wi], device_id=(x, y, 1 - c), device_id_type=_MESH)
               for wi in range(nw)]
        for cp in cps:
            cp.start()
        for cp in cps:
            cp.wait()

    return _pcall(body, name=name, out_shape=[jax.ShapeDtypeStruct(t.shape, t.dtype) for t in ts],
                  in_specs=[_ANY] * nw, out_specs=[_ANY] * nw, scratch_shapes=_sem_pairs(nw))(*ts)


def _rs_add(g, ra, c_arr, *, name):
    n, _, r, w = g.shape

    def body(c_ref, g_ref, ra_ref, s_ref, sb_ref):
        t = g_ref[...] + ra_ref[...]
        s_ref[...] = t
        sb_ref[...] = t.astype(BF16)

    grid_spec = pltpu.PrefetchScalarGridSpec(
        num_scalar_prefetch=1, grid=(n,),
        in_specs=[pl.BlockSpec((None, None, r, w), lambda k, c: (k, c[0], 0, 0)),
                  pl.BlockSpec((None, r, w), lambda k, c: (k, 0, 0))],
        out_specs=[pl.BlockSpec((None, r, w), lambda k, c: (k, 0, 0))] * 2)
    return _pcall(body, name=name, grid_spec=grid_spec,
                  out_shape=[jax.ShapeDtypeStruct((n, r, w), F32), jax.ShapeDtypeStruct((n, r, w), BF16)],
                  compiler_params=_PAR)(c_arr, g, ra)


def _rs_final(s, rb, j_arr, *, name):
    _, r, w = s.shape

    def body(j_ref, s_ref, rb_ref, t_ref):
        t_ref[...] = ((s_ref[...] + rb_ref[0].astype(F32)) + rb_ref[1].astype(F32)) + rb_ref[2].astype(F32)

    grid_spec = pltpu.PrefetchScalarGridSpec(
        num_scalar_prefetch=1, grid=(1,),
        in_specs=[pl.BlockSpec((None, r, w), lambda i, j: (j[0], 0, 0)),
                  pl.BlockSpec((3, r, w), lambda i, j: (0, 0, 0))],
        out_specs=pl.BlockSpec((r, w), lambda i, j: (0, 0)))
    return _pcall(body, name=name, grid_spec=grid_spec, out_shape=jax.ShapeDtypeStruct((r, w), F32),
                  compiler_params=_ARB)(j_arr, s, rb)


def _tile2(g):
    return jnp.concatenate([g, g], axis=1)


_BIG = ("w_in", "w_ffn_in", "w_ffn_out", "w_o", "w_proj_a", "w_proj_b")
_BIG_SHARD = {"w_in": (1024, 1472), "w_ffn_in": (1024, 1408), "w_ffn_out": (704, 1024), "w_o": (256, 1024),
              "w_proj_a": (512, 256), "w_proj_b": (256, 256)}


def _device_step(x2, tgt, mod, first_weights, late_weights, early_grads, mid_grads, g_norm1, g_norm2, b_gate, g_qa, g_ka,
                 g_qb, g_kb, rpb):
    d = D_MODEL
    sh1, sc1, gt1, sh2, sc2, gt2 = [mod[:, k * d:(k + 1) * d] for k in range(6)]

    colmask, oh_col, na_ks, na_cls = _na_constants()
    rp = jnp.pad(rpb, ((0, 0), (0, 16 - _RPB_RO), (RP_LANE0, LANES - RP_LANE0 - _RPB_CO)), constant_values=NEG)
    na = (rp, jnp.asarray(colmask))
    na_ks, na_cls = jnp.asarray(na_ks), jnp.asarray(na_cls)
    dil = [_dil_constants(dd) for _, dd in DIL_CONFIGS]
    gains = jnp.concatenate([_tile2(g_qa), _tile2(g_ka), _tile2(g_qb), _tile2(g_kb)], axis=0)
    cos_t, sa_t, sb_t = _rope_tables()

    h1 = _norm1_fwd(x2, g_norm1, sc1, sh1, name="norm1_fwd")
    wts = first_weights(h1[:8, :LANES].astype(F32))
    qkvn, qk_pre, gates, *qkv_dil = _pre_attn_fwd(h1, cos_t, sa_t, sb_t, wts["w_qkv"], wts["w_gates"], gains,
                                                  name="pre_attn_fwd")
    o_a, lse_a = _attn_fwd(qkvn, 0, 4, 8, 4, None, na_ks, na_cls, NA_NK, name="attn_a_fwd", na=na)
    arrs, o_g, l_g = [], [], []
    res = lambda t, dd: t if dd == 1 else (t, dd)
    for g, (_, dd) in enumerate(DIL_CONFIGS):
        tab_g, ks_g, cls_g, nk_g = jnp.asarray(dil[g][0]), jnp.asarray(dil[g][1]), jnp.asarray(dil[g][2]), dil[g][3]
        arr, cb = (qkvn, (12, 18, 24)) if dd == 1 else (qkv_dil.pop(0), (0, 2, 4))
        op, lp = _attn_fwd(arr, cb[0], cb[1], cb[2], 2, tab_g, ks_g, cls_g, nk_g, name=f"attn_d{g}_fwd", qb=DIL_QB)
        arrs.append((arr, cb, tab_g, ks_g, cls_g, nk_g))
        o_g.append(res(op, dd))
        l_g.append(res(lp, dd))
    wts = dict(wts, **late_weights(o_a))
    mo, x1, h2 = _post_attn_fwd(o_a, o_g, l_g, gates, x2, wts["w_pa"], wts["w_pb"], wts["w_o"], b_gate,
                                gt1, g_norm2, sc2, sh2, name="post_attn_fwd")
    act, ff = _ffn_fwd(h2, wts["w_ffn_in"], name="ffn_fwd")

    dy, dffo, dff, dgt2, loss_v = _ffn_mid(act, ff, x1, tgt, wts["w_ffn_out"], gt2, name="ffn_mid")
    grads = {}
    g_ffn_out = _wgrad(act, dffo, name="wg_ffn_out", tm=D_FF // 2, tn=d, tk=2048)
    grads["w_ffn_out"] = g_ffn_out.reshape(N_CHIP, D_FF // N_CHIP, d)
    grads["w_ffn_in"] = _wgrad(h2, dff, name="wg_ffn_in", tm=512, tn=2 * FF_CHIP, tk=2048, chips=N_CHIP)
    dx1, dmo, sums2 = _ffn_in_bwd(dff, x1, dy, mo, wts["w_ffn_in"], g_norm2, sc2, gt1, name="ffn_in_bwd")
    pab = _post_attn_bwd(dmo, gates, o_a, o_g, l_g, wts["w_pa"], wts["w_pb"], wts["w_o"], b_gate, name="post_attn_bwd")
    dgates, do_a = pab[:2]
    do_g, dl_g, (dbg, g_pa, g_pb, g_o) = pab[2:5], pab[5:8], pab[8:]
    grads["w_o"] = g_o.reshape(N_CHIP, d // N_CHIP, d)
    grads["w_proj_a"] = g_pa.reshape(WA, N_CHIP, d // N_CHIP).transpose(1, 0, 2)
    grads["w_proj_b"] = g_pb.reshape(WB_OUT, N_CHIP, d // N_CHIP).transpose(1, 0, 2)
    order = early_grads(grads)
    dqs, dks, dvs = [], [], []
    for g, (_, dd) in enumerate(DIL_CONFIGS):
        arr, cb, tab_g, ks_g, cls_g, nk_g = arrs[g]
        plain = lambda t: t[0] if isinstance(t, tuple) else t
        dq, dk, dv = _attn_bwd(arr, cb[0], cb[1], cb[2], 2, tab_g, ks_g, cls_g, nk_g, do_g[g], plain(o_g[g]),
                               plain(l_g[g]), name=f"attn_d{g}_bwd", dlse=dl_g[g], qb=DIL_QB, order=order)
        dqs.append(res(dq, dd))
        dks.append(res(dk, dd))
        dvs.append(res(dv, dd))
    order = mid_grads(dv)
    dqa, dka, dva, bank = _attn_bwd(qkvn, 0, 4, 8, 4, None, na_ks, na_cls, NA_NK, do_a, o_a, lse_a,
                                    name="attn_a_bwd", na=na, order=order)
    dqkv, grad_x, dgains, sums1 = _pre_attn_bwd(qk_pre, [dqa, dka, dva] + dqs + dks + dvs, dgates, x2, dx1, cos_t, sa_t,
                                                sb_t, wts["w_qkv"], wts["w_gates"], gains, g_norm1, sc1,
                                                name="pre_attn_bwd")
    g_qkv = _wgrad(h1, dqkv, name="wg_qkv", tm=d, tn=W_QKV // 2, tk=2048)
    g_gates = _wgrad(h1, dgates, name="wg_gates", tm=d, tn=W_GATES // 2, tk=2048)
    nc, cut = _BIG_SHARD["w_in"][1], 3 * _BIG_SHARD["w_in"][1] - W_QKV
    grads["w_in"] = jnp.stack([g_qkv[:, :nc], g_qkv[:, nc:2 * nc],
                               jnp.concatenate([g_qkv[:, 2 * nc:], g_gates[:, :cut]], axis=1), g_gates[:, cut:]])

    bank = bank.reshape(NA_HEADS, NA_SLOTS, GRID_W, 2, GRID_W).transpose(0, 1, 3, 2, 4)
    bank = jnp.pad(bank.reshape(NA_HEADS, 2 * NA_SLOTS, GRID_W * GRID_W), ((0, 0), (0, _BANK_ROWS - 2 * NA_SLOTS), (0, 0)))
    g2 = _bank_reduce(bank, jnp.asarray(oh_col), name="rpb_reduce")[:, :2 * NA_SLOTS].reshape(NA_HEADS, NA_SLOTS, 2, LANES)
    g_rpb = g2[:, 3:3 + _RPB_RO, 0, :_RPB_CO] + g2[:, 2:2 + _RPB_RO, 1, :_RPB_CO]

    dmod = jnp.concatenate([sums1[0:1], sums1[1:2], sums2[3:4], sums2[0:1], sums2[1:2], dgt2], axis=1)
    small = dict(g_norm1=sums1[2:3], g_norm2=sums2[2:3], b_gate=dbg, g_qa=dgains[0:1, :HEAD_DIM],
                 g_ka=dgains[1:2, :HEAD_DIM], g_qb=dgains[2:3, :HEAD_DIM], g_kb=dgains[3:4, :HEAD_DIM], rpb=g_rpb)
    return loss_v, grad_x, grads, dmod, small


_SMALL = ("b_ada", "g_norm1", "g_norm2", "b_gate", "g_qa", "g_ka", "g_qb", "g_kb", "rpb")
_SMALL_N = {"b_ada": 6 * D_MODEL, "g_norm1": D_MODEL, "g_norm2": D_MODEL, "b_gate": 2 * D_MODEL, "g_qa": HEAD_DIM,
            "g_ka": HEAD_DIM, "g_qb": HEAD_DIM, "g_kb": HEAD_DIM, "rpb": NA_HEADS * _RPB_RO * _RPB_CO}


def _pack_small(parts):
    flat = [parts[n].reshape(1, _SMALL_N[n]) for n in _SMALL]
    used = sum(_SMALL_N.values())
    return jnp.concatenate(flat + [jnp.zeros((1, STATS_W - used), F32)], axis=1)


def _unpack_small(v, shapes):
    out, at = {}, 0
    for n in _SMALL:
        out[n] = v[:, at:at + _SMALL_N[n]].reshape(shapes[n])
        at += _SMALL_N[n]
    return out


def _join_cols(t):
    _, r, c = t.shape
    return t.transpose(1, 0, 2).reshape(r, N_CHIP * c)


def kernel(x, c, w_ada, b_ada, g_norm1, g_norm2, w_in, b_gate, g_qa, g_ka, g_qb, g_kb, rpb, w_proj_a, w_proj_b, w_o, w_ffn_in, w_ffn_out, loss_target, m_w_ada, m_b_ada, m_g_norm1, m_g_norm2, m_w_in, m_b_gate, m_g_qa, m_g_ka, m_g_qb, m_g_kb, m_rpb, m_w_proj_a, m_w_proj_b, m_w_o, m_w_ffn_in, m_w_ffn_out, v_w_ada, v_b_ada, v_g_norm1, v_g_norm2, v_w_in, v_b_gate, v_g_qa, v_g_ka, v_g_qb, v_g_kb, v_rpb, v_w_proj_a, v_w_proj_b, v_w_o, v_w_ffn_in, v_w_ffn_out):
    names = ("w_ada", "b_ada", "g_norm1", "g_norm2", "w_in", "b_gate", "g_qa", "g_ka", "g_qb", "g_kb", "rpb",
             "w_proj_a", "w_proj_b", "w_o", "w_ffn_in", "w_ffn_out")
    w = dict(zip(names, (w_ada, b_ada, g_norm1, g_norm2, w_in, b_gate, g_qa, g_ka, g_qb, g_kb, rpb, w_proj_a, w_proj_b,
                         w_o, w_ffn_in, w_ffn_out)))
    m = dict(zip(names, (m_w_ada, m_b_ada, m_g_norm1, m_g_norm2, m_w_in, m_b_gate, m_g_qa, m_g_ka, m_g_qb, m_g_kb, m_rpb,
                         m_w_proj_a, m_w_proj_b, m_w_o, m_w_ffn_in, m_w_ffn_out)))
    v = dict(zip(names, (v_w_ada, v_b_ada, v_g_norm1, v_g_norm2, v_w_in, v_b_gate, v_g_qa, v_g_ka, v_g_qb, v_g_kb, v_rpb,
                         v_w_proj_a, v_w_proj_b, v_w_o, v_w_ffn_in, v_w_ffn_out)))
    d = D_MODEL
    xi, yi, ci = _pos()
    chip = 2 * xi + yi
    me = 2 * chip + ci
    ada_cols = 6 * d // N_CHIP

    c_arr, chip_arr = ci.reshape(1).astype(jnp.int32), chip.reshape(1).astype(jnp.int32)
    first, rest = _BIG[:1], _BIG[1:]

    c_all = _small_allgather(c.reshape(8, d // 8), name="ag_c").reshape(N_DEV, d)
    b_sh = lax.dynamic_slice(b_ada, (0, chip * ada_cols), (1, ada_cols))
    mod_part = _ada_fwd(c_all, w_ada[0], b_sh, name="ada_fwd")
    mod_all = _small_allgather(mod_part, name="ag_mod").reshape(N_CHIP, 2, 8, ada_cols)[:, 0]
    mod = lax.dynamic_index_in_dim(mod_all, me, axis=1, keepdims=False).reshape(1, 6 * d)

    halves = {n: (2, _BIG_SHARD[n][0] // 2, _BIG_SHARD[n][1]) for n in _BIG}
    shards = {n: w[n][0].astype(BF16).reshape(halves[n]) for n in _BIG}
    land = lambda n: lax.empty((N_CHIP,) + halves[n], BF16)
    ag1 = _split_start([shards[n] for n in first], [land(n) for n in first], _ag_plan(1), 4, mod, name="ag1_start")
    ag2 = _split_start([shards[n] for n in rest], [land(n) for n in rest], _ag_plan_direct(len(rest)), 7 * len(rest),
                       ag1[4], name="ag2_start")
    rpb_after = rpb[0] + ag2[4][0, 0]

    def first_weights(after):
        after = after[:1, :1] + ag2[4][:1, :1]
        full1 = _split_wait(ag1[0], ag1[1], ag1[2], ag1[3], _ag_plan(1), after, name="ag1_wait")
        p_in = _ag_pass(full1, name="ag1_pass")[0].reshape((N_CHIP,) + _BIG_SHARD["w_in"])
        cut = W_QKV - 2 * _BIG_SHARD["w_in"][1]
        return dict(w_qkv=jnp.concatenate([p_in[0], p_in[1], p_in[2][:, :cut]], axis=1),
                    w_gates=jnp.concatenate([p_in[2][:, cut:], p_in[3]], axis=1))

    def late_weights(after):
        full2 = _split_wait(ag2[0], ag2[1], ag2[2], ag2[3], _ag_plan_direct(len(rest)), after, name="ag2_wait")
        full ={n: fu.reshape((N_CHIP,) + _BIG_SHARD[n]) for n, fu in zip(rest, full2)}
        return dict(w_pa=_join_cols(full["w_proj_a"]), w_pb=_join_cols(full["w_proj_b"]), w_o=full["w_o"].reshape(d, d),
                    w_ffn_in=_join_cols(full["w_ffn_in"]), w_ffn_out=full["w_ffn_out"].reshape(D_FF, d))

    def sib_begin(group, grads, tag):
        gps = [grads[n].reshape((N_CHIP,) + halves[n]) for n in group]
        lands = [lax.empty((N_CHIP,) + halves[n][1:], F32) for n in group]
        return _split_start(gps, lands, _sib_plan(len(group)), N_CHIP * len(group), gps[0], name=f"rs_sib_{tag}_start",
                            alias_sources=False)

    def rs_begin(group, sib, after, tag):
        gps, ras = _split_wait(sib[0], sib[1], sib[2], sib[3], _sib_plan(len(group)), after,
                               name=f"rs_sib_{tag}_wait", with_sources=True, alias_sources=False)
        sums = [_rs_add(gp, ra, c_arr, name=f"rs_add_{n}") for n, gp, ra in zip(group, gps, ras)]
        lands = [lax.empty((3,) + halves[n][1:], BF16) for n in group]
        st = _split_start([sb for _, sb in sums], lands, _rs_plan(len(group)), 3 * len(group), sums[0][0],
                          name=f"rs_{tag}_start")
        return sums, st

    def rs_end(group, begun, after, tag):
        sums, st = begun
        rbs = _split_wait(st[0], st[1], st[2], st[3], _rs_plan(len(group)), after, name=f"rs_{tag}_wait")
        return [_rs_final(sf, rb, chip_arr, name=f"rs_final_{n}") for n, (sf, _), rb in zip(group, sums, rbs)]

    begun = {}

    def early_grads(grads):
        begun["sib_rest"] = sib_begin(rest, grads, "rest")
        return begun["sib_rest"][4]

    def mid_grads(after):
        begun["rest"] = rs_begin(rest, begun["sib_rest"], after, "rest")
        return begun["rest"][1][4]

    loss_v, grad_x, grads, dmod, small = _device_step(
        x[0], loss_target[0], mod, first_weights, late_weights, early_grads, mid_grads, g_norm1, g_norm2, b_gate, g_qa,
        g_ka, g_qb, g_kb, rpb_after)
    sib_first = sib_begin(first, grads, "first")

    g, delta, new_m, new_v = {}, {}, {}, {}

    def finish(group, ts, tag):
        others = _sibling_swap(ts, name=f"rs_pair_{tag}")
        for n, t, o in zip(group, ts, others):
            gg, dl, nm, nv = _adamw_halves(w[n][0], t, o, m[n][0], v[n][0], c_arr, name=f"adamw_{n}")
            g[n], delta[n], new_m[n], new_v[n] = gg[None], dl[None], nm[None], nv[None]

    finish(rest, rs_end(rest, begun["rest"], sib_first[4], "rest"), "rest")
    done_rest = sum(new_v[n][0, :1, :1] for n in rest)
    begun["first"] = rs_begin(first, sib_first, done_rest, "first")

    stats = _pack_small(dict(b_ada=dmod, **small)) + begun["first"][1][4][0, 0]
    stats = stats.at[:, STATS_W - 1].set(loss_v[0, 0])
    rows = _small_allgather(stats.reshape(8, STATS_W // 8), name="ag_stats").reshape(N_DEV, STATS_W)
    dmod_sh = lax.dynamic_slice(rows, (0, chip * ada_cols), (8, ada_cols))
    g_ada = _ada_bwd(c_all.T, dmod_sh, name="ada_bwd")
    tot = _row_sum(rows, name="stats_sum")
    g_small = _unpack_small(tot, {n: w[n].shape for n in _SMALL})

    finish(first, rs_end(first, begun["first"], tot, "first"), "first")

    dl, nm, nv = _adamw(w_ada[0], g_ada, m_w_ada[0], v_w_ada[0], name="adamw_w_ada")
    g["w_ada"], delta["w_ada"], new_m["w_ada"], new_v["w_ada"] = g_ada[None], dl[None], nm[None], nv[None]
    shapes = {n: w[n].shape for n in _SMALL}
    dl, nm, nv = _adamw(_pack_small({n: w[n] for n in _SMALL}), tot, _pack_small({n: m[n] for n in _SMALL}),
                        _pack_small({n: v[n] for n in _SMALL}), name="adamw_small")
    delta.update(_unpack_small(dl, shapes))
    new_m.update(_unpack_small(nm, shapes))
    new_v.update(_unpack_small(nv, shapes))
    g.update(g_small)

    loss = tot[0, STATS_W - 1]
    return (loss, grad_x[None], *[g[n] for n in names], *[delta[n] for n in names], *[new_m[n] for n in names],
            *[new_v[n] for n in names])
```
